```python
import jax, jax.numpy as jnp
from jax import lax
import numpy as np

D_MODEL = 1024
BATCH = 2
SEQ = 8192
DEPTH = 2
DEC_BATCH = 128
DEC_SEQ = 4
PAST_LEN = 2048
PAGE_SIZE = 128

N_A_LAYERS = (DEPTH + 1) // 2
N_C_LAYERS = DEPTH // 2
EPS = 1e-6
NSA_HEADS = 8
NSA_KV = 2
NSA_REP = NSA_HEADS // NSA_KV
HD = 64
CMP_BLOCK = 32
SLC_BLOCK = 64
SLC_TOPK = 16
WINDOW = 512
Q_BLOCK = 128
ROPE_DIM = HD // 4
ROPE_THETA = 500000.0
GLA_HEADS = 4
GLA_DK = 64
GLA_DV = 128
GLA_LOWRANK = 16
GLA_TAU = 16.0
GLA_CHUNK = 64
D_RNN = 1280
LRU_BLOCKS = 10
LRU_BW = D_RNN // LRU_BLOCKS
CONV_W = 4
LRU_C = 8.0
N_GROUPS = 4
EXP_PER_GROUP = 8
N_EXPERTS = N_GROUPS * EXP_PER_GROUP
E_HID = 512
TOP_K_IN_GROUP = 2
MOE_BLOCK = 128
NSA_Q_W = NSA_HEADS * HD
NSA_KV_W = 2 * NSA_KV * HD
GLA_K_W = GLA_HEADS * GLA_DK
GLA_V_W = GLA_HEADS * GLA_DV
A_SIZES = (NSA_Q_W, NSA_KV_W, NSA_KV_W, NSA_KV_W, 3 * NSA_HEADS, GLA_K_W, GLA_K_W, GLA_V_W, GLA_LOWRANK, GLA_V_W)
A_IN = NSA_Q_W + 3 * NSA_KV_W + 3 * NSA_HEADS + 2 * GLA_K_W + 2 * GLA_V_W + GLA_LOWRANK
A_MIX = NSA_Q_W + GLA_V_W

kernel_name = 'hybrid_nsa_gla_rglru_hmoe_step'

F32 = jnp.float32


def rmsnorm(x, g):
    xf = x.astype(F32)
    y = xf * lax.rsqrt(jnp.mean(xf * xf, axis=-1, keepdims=True) + EPS)
    return (y * g.astype(F32)).astype(x.dtype)


def rope(x, pos):
    half = ROPE_DIM // 2
    inv = 1.0 / (ROPE_THETA ** (jnp.arange(0, ROPE_DIM, 2, dtype=F32) / ROPE_DIM))
    ang = pos.astype(F32)[:, None] * inv[None, :]
    cos = jnp.cos(ang)[:, None, :]
    sin = jnp.sin(ang)[:, None, :]
    xr = x[..., :ROPE_DIM].astype(F32)
    x1, x2 = xr[..., :half], xr[..., half:]
    rot = jnp.concatenate([x1 * cos - x2 * sin, x1 * sin + x2 * cos], axis=-1)
    return jnp.concatenate([rot.astype(x.dtype), x[..., ROPE_DIM:]], axis=-1)


def masked_softmax(s, mask):
    s = jnp.where(mask, s, -jnp.inf)
    m = jnp.max(s, axis=-1, keepdims=True)
    m = jnp.where(jnp.isfinite(m), m, 0.0)
    e = jnp.where(mask, jnp.exp(s - m), 0.0)
    return e / jnp.maximum(jnp.sum(e, axis=-1, keepdims=True), 1e-30)


def nsa_key_blocks(kv_cmp, kv_slc, pe, w_cmp):
    B, T = kv_cmp.shape[:2]
    nc = T // CMP_BLOCK
    blk = kv_cmp[:, :nc * CMP_BLOCK].reshape(B, nc, CMP_BLOCK, 2, NSA_KV, HD) + pe[:, :, None, :]
    ckv = jnp.einsum('bnlcgd,lcde->bncge', blk, w_cmp)
    ck = rope(ckv[:, :, 0], (jnp.arange(nc, dtype=jnp.int32) + 1) * CMP_BLOCK - 1)
    cv = ckv[:, :, 1]
    ns = -(-T // SLC_BLOCK)
    skv = jnp.pad(kv_slc, ((0, 0), (0, ns * SLC_BLOCK - T), (0, 0), (0, 0), (0, 0)))
    skv = skv.reshape(B, ns, SLC_BLOCK, 2, NSA_KV, HD).transpose(3, 0, 4, 1, 2, 5)
    return ck, cv, skv[0], skv[1]


def nsa_query_block(q, qpos, gates, ck, cv, sk, sv, wk, wv, wpos):
    B, QB = q.shape[:2]
    qg = q.reshape(B, QB, NSA_KV, NSA_REP, HD)
    nc = ck.shape[1]
    c_end = (jnp.arange(nc, dtype=jnp.int32) + 1) * CMP_BLOCK - 1
    c_mask = c_end[None, :] <= qpos[:, None]
    s_c = jnp.einsum('bqgrd,bngd->bgrqn', qg, ck).astype(F32)
    p_c = masked_softmax(s_c, c_mask)
    o_c = jnp.einsum('bgrqn,bngd->bqgrd', p_c.astype(cv.dtype), cv)
    ns = sk.shape[2]
    ratio = SLC_BLOCK // CMP_BLOCK
    p_grp = jnp.sum(p_c, axis=2)
    p_grp = jnp.pad(p_grp, ((0, 0), (0, 0), (0, 0), (0, ns * ratio - nc)))
    p_slc = p_grp.reshape(B, NSA_KV, QB, ns, ratio).sum(-1)
    blk = jnp.arange(ns, dtype=jnp.int32)[None, :]
    cur = (qpos // SLC_BLOCK)[:, None]
    forced = (blk == 0) | (blk == cur) | (blk == cur - 1)
    score = jnp.where(forced, jnp.inf, p_slc)
    score = jnp.where(blk <= cur, score, -jnp.inf)
    n_top = min(SLC_TOPK, ns)
    top_v, top_i = lax.top_k(score, n_top)
    bi = jnp.arange(B)[:, None, None, None]
    gi = jnp.arange(NSA_KV)[None, :, None, None]
    k_sel = sk[bi, gi, top_i]
    v_sel = sv[bi, gi, top_i]
    k_pos = top_i[..., None] * SLC_BLOCK + jnp.arange(SLC_BLOCK, dtype=jnp.int32)
    s_mask = (top_v > -jnp.inf)[..., None] & (k_pos <= qpos[None, None, :, None, None])
    s_s = jnp.einsum('bqgrd,bgqnld->bgrqnl', qg, k_sel).astype(F32)
    s_s = s_s.reshape(B, NSA_KV, NSA_REP, QB, n_top * SLC_BLOCK)
    p_s = masked_softmax(s_s, s_mask.reshape(B, NSA_KV, 1, QB, n_top * SLC_BLOCK))
    p_s = p_s.reshape(B, NSA_KV, NSA_REP, QB, n_top, SLC_BLOCK)
    o_s = jnp.einsum('bgrqnl,bgqnld->bqgrd', p_s.astype(v_sel.dtype), v_sel)
    d = qpos[:, None] - wpos[None, :]
    w_mask = (d >= 0) & (d < WINDOW) & (wpos[None, :] >= 0)
    s_w = jnp.einsum('bqgrd,bkgd->bgrqk', qg, wk).astype(F32)
    p_w = masked_softmax(s_w, w_mask)
    o_w = jnp.einsum('bgrqk,bkgd->bqgrd', p_w.astype(wv.dtype), wv)
    g = gates.reshape(B, QB, NSA_KV, NSA_REP, 3)
    o = g[..., 0:1] * o_c + g[..., 1:2] * o_s + g[..., 2:3] * o_w
    return o.reshape(B, QB, NSA_Q_W)


def nsa_prompt(q, gates, ck, cv, sk, sv, kv_win):
    B, T = q.shape[:2]
    wkv = jnp.pad(kv_win, ((0, 0), (WINDOW, 0), (0, 0), (0, 0), (0, 0)))

    def one_block(i):
        s0 = i * Q_BLOCK
        qb = lax.dynamic_slice_in_dim(q, s0, Q_BLOCK, axis=1)
        gb = lax.dynamic_slice_in_dim(gates, s0, Q_BLOCK, axis=1)
        wb = lax.dynamic_slice_in_dim(wkv, s0, WINDOW + Q_BLOCK, axis=1)
        qpos = s0 + jnp.arange(Q_BLOCK, dtype=jnp.int32)
        wpos = s0 - WINDOW + jnp.arange(WINDOW + Q_BLOCK, dtype=jnp.int32)
        return nsa_query_block(qb, qpos, gb, ck, cv, sk, sv, wb[:, :, 0], wb[:, :, 1], wpos)

    out = lax.map(one_block, jnp.arange(T // Q_BLOCK, dtype=jnp.int32))
    return jnp.swapaxes(out, 0, 1).reshape(B, T, NSA_Q_W)


def gla_chunked(q, k, v, log_a, s0, chunk):
    B, T, H, DK = q.shape
    nck = T // chunk

    def to_chunks(x):
        return jnp.swapaxes(x.reshape(B, nck, chunk, *x.shape[2:]), 0, 1)

    causal = jnp.tril(jnp.ones((chunk, chunk), dtype=bool))

    def step(S, inp):
        qc, kc, vc, lac = inp
        qf, kf, vf = qc.astype(F32), kc.astype(F32), vc.astype(F32)
        cum = jnp.cumsum(lac, axis=1)
        o_inter = jnp.einsum('bchk,bhkv->bchv', qf * jnp.exp(cum), S)
        diff = cum[:, :, None] - cum[:, None, :]
        diff = jnp.where(causal[None, :, :, None, None], diff, -jnp.inf)
        att = jnp.einsum('bihk,bjhk,bijhk->bhij', qf, kf, jnp.exp(diff))
        o_intra = jnp.einsum('bhij,bjhv->bihv', att, vf)
        last = cum[:, -1]
        kdec = kf * jnp.exp(last[:, None] - cum)
        S = jnp.exp(last)[..., None] * S + jnp.einsum('bchk,bchv->bhkv', kdec, vf)
        return S, o_inter + o_intra

    S, o = lax.scan(step, s0, (to_chunks(q), to_chunks(k), to_chunks(v), to_chunks(log_a)))
    return jnp.swapaxes(o, 0, 1).reshape(B, T, H, v.shape[-1]), S


def mixer_a(hn, pos, p, j, past):
    B, T, _ = hn.shape
    u = hn @ p['a_w_in'][j]
    offs = np.cumsum(np.array(A_SIZES))[:-1].tolist()
    q, kvc, kvs, kvw, gts, gq, gk, gv, glr, gg = jnp.split(u, offs, axis=-1)
    q = rope(q.reshape(B, T, NSA_HEADS, HD), pos) * (HD ** -0.5)

    def as_kv(t, rotate):
        t = t.reshape(B, T, 2, NSA_KV, HD)
        if rotate:
            t = jnp.stack([rope(t[:, :, 0], pos), t[:, :, 1]], axis=2)
        return t

    kv_cmp = as_kv(kvc, False)
    kv_slc = as_kv(kvs, True)
    kv_win = as_kv(kvw, True)
    gates = jax.nn.sigmoid(gts.reshape(B, T, NSA_HEADS, 3))
    pe, wc = p['a_cmp_pe'][j], p['a_cmp_w'][j]
    gq = gq.reshape(B, T, GLA_HEADS, GLA_DK) * (GLA_DK ** -0.5)
    gk = gk.reshape(B, T, GLA_HEADS, GLA_DK)
    gv = gv.reshape(B, T, GLA_HEADS, GLA_DV)
    z = (glr @ p['a_gla_wa2'][j]).astype(F32) + p['a_gla_ba'][j].astype(F32)
    log_a = (jax.nn.log_sigmoid(z) / GLA_TAU).reshape(B, T, GLA_HEADS, GLA_DK)
    if past is None:
        ck, cv, sk, sv = nsa_key_blocks(kv_cmp, kv_slc, pe, wc)
        o_nsa = nsa_prompt(q, gates, ck, cv, sk, sv, kv_win)
        win_new = kv_win[:, T - min(WINDOW, T):]
        s0 = jnp.zeros((B, GLA_HEADS, GLA_DK, GLA_DV), F32)
        o_gla, sT = gla_chunked(gq, gk, gv, log_a, s0, GLA_CHUNK)
    else:
        pt = past['page_table']
        old_cmp = past['cache_cmp_kv'][j, pt].reshape(B, -1, 2, NSA_KV, HD)
        old_slc = past['cache_slc_kv'][j, pt].reshape(B, -1, 2, NSA_KV, HD)
        ck, cv, sk, sv = nsa_key_blocks(jnp.concatenate([old_cmp, kv_cmp], axis=1),
                                        jnp.concatenate([old_slc, kv_slc], axis=1), pe, wc)
        buf = past['cache_win_kv'][j]
        wkv = jnp.concatenate([buf, kv_win], axis=1)
        n_w = wkv.shape[1]
        wpos = pos[0] + T - n_w + jnp.arange(n_w, dtype=jnp.int32)
        o_nsa = nsa_query_block(q, pos, gates, ck, cv, sk, sv, wkv[:, :, 0], wkv[:, :, 1], wpos)
        win_new = wkv[:, n_w - buf.shape[1]:]
        o_gla, sT = gla_chunked(gq, gk, gv, log_a, past['state_gla'][j].astype(F32), T)
    o_gla = rmsnorm(o_gla.astype(hn.dtype), p['a_gla_norm'][j]) * jax.nn.silu(gg.reshape(B, T, GLA_HEADS, GLA_DV))
    mix = jnp.concatenate([o_nsa, o_gla.reshape(B, T, GLA_V_W)], axis=-1) @ p['a_w_out'][j]
    return mix, (kv_cmp, kv_slc, win_new, sT.astype(hn.dtype))


def block_diag(x, w, b):
    lead = x.shape[:-1]
    xb = x.reshape(*lead, LRU_BLOCKS, LRU_BW)
    return (jnp.einsum('btnw,nwv->btnv', xb, w) + b.reshape(LRU_BLOCKS, LRU_BW)).reshape(*lead, D_RNN)


def rglru(x, h0, w_a, b_a, w_x, b_x, lam):
    xf = x.astype(F32)
    r = jax.nn.sigmoid(block_diag(xf, w_a.astype(F32), b_a.astype(F32)))
    i = jax.nn.sigmoid(block_diag(xf, w_x.astype(F32), b_x.astype(F32)))
    log_a = -LRU_C * r * jax.nn.softplus(-lam.astype(F32))
    a = jnp.exp(log_a)
    u = jnp.sqrt(-jnp.expm1(2.0 * log_a)) * (i * xf)

    def step(h, inp):
        a_t, u_t = inp
        h = a_t * h + u_t
        return h, h

    hT, hs = lax.scan(step, h0, (jnp.swapaxes(a, 0, 1), jnp.swapaxes(u, 0, 1)))
    return jnp.swapaxes(hs, 0, 1), hT


def mixer_c(hn, p, j, past):
    B, T, _ = hn.shape
    u = hn @ p['c_w_in'][j]
    gate_b, x_b = jnp.split(u, 2, axis=-1)
    if past is None:
        buf = jnp.zeros((B, CONV_W - 1, D_RNN), x_b.dtype)
        h0 = jnp.zeros((B, D_RNN), F32)
    else:
        buf = past['state_conv'][j].astype(x_b.dtype)
        h0 = past['state_lru'][j].astype(F32)
    xp = jnp.concatenate([buf, x_b], axis=1)
    w = p['c_conv_w'][j][:, None, :].astype(xp.dtype)
    xc = lax.conv_general_dilated(xp, w, (1,), 'VALID', dimension_numbers=('NWC', 'WIO', 'NWC'),
                                  feature_group_count=D_RNN) + p['c_conv_b'][j]
    new_buf = xp[:, xp.shape[1] - (CONV_W - 1):]
    y, hT = rglru(xc, h0, p['c_w_a'][j], p['c_b_a'][j], p['c_w_x'][j], p['c_b_x'][j], p['c_lam'][j])
    out = (jax.nn.gelu(gate_b) * y.astype(hn.dtype)) @ p['c_w_out'][j]
    return out, (hT.astype(hn.dtype), new_buf)


def hier_moe(x, w_rg, b_rg, w_re, b_re, w1, w3, w2):
    B, T, D = x.shape
    n = B * T
    xt = x.reshape(n, D)
    xf = xt.astype(F32)
    rows = jnp.arange(n)
    g_logit = xf @ w_rg.astype(F32) + b_rg.astype(F32)
    g_prob = jax.nn.softmax(g_logit, axis=-1)
    g_top = jnp.argmax(g_logit, axis=-1).astype(jnp.int32)
    g_w = g_prob[rows, g_top][:, None]
    e_logit = (xf @ w_re.astype(F32) + b_re.astype(F32)).reshape(n, N_GROUPS, EXP_PER_GROUP)
    e_logit = e_logit[rows, g_top]
    e_val, e_idx = lax.top_k(e_logit, TOP_K_IN_GROUP)
    gate = jax.nn.softmax(e_val, axis=-1) * g_w
    expert = (g_top[:, None] * EXP_PER_GROUP + e_idx).astype(jnp.int32)
    nk = n * TOP_K_IN_GROUP
    flat_e = expert.reshape(nk)
    flat_t = jnp.repeat(jnp.arange(n, dtype=jnp.int32), TOP_K_IN_GROUP)
    flat_w = gate.reshape(nk)
    order = jnp.argsort(flat_e)
    se, st, sw = flat_e[order], flat_t[order], flat_w[order]
    counts = jnp.bincount(flat_e, length=N_EXPERTS)
    padded = ((counts + MOE_BLOCK - 1) // MOE_BLOCK) * MOE_BLOCK
    pad_end = jnp.cumsum(padded)
    pad_start = pad_end - padded
    cnt_start = jnp.cumsum(counts) - counts
    dest = pad_start[se] + (jnp.arange(nk, dtype=jnp.int32) - cnt_start[se])
    n_slots = (-(-nk // MOE_BLOCK) + N_EXPERTS) * MOE_BLOCK
    nb = n_slots // MOE_BLOCK
    slot_tok = jnp.full((n_slots,), n, jnp.int32).at[dest].set(st)
    slot_w = jnp.zeros((n_slots,), F32).at[dest].set(sw)
    blk_start = jnp.arange(nb, dtype=jnp.int32) * MOE_BLOCK
    blk_exp = jnp.minimum(jnp.searchsorted(pad_end, blk_start, side='right'), N_EXPERTS - 1)
    x_pad = jnp.concatenate([xt, jnp.zeros((1, D), xt.dtype)], axis=0)

    def expert_block(args):
        toks, e = args
        xb = x_pad[toks]
        hdn = jax.nn.silu(xb @ w1[e]) * (xb @ w3[e])
        return hdn @ w2[e]

    yb = lax.map(expert_block, (slot_tok.reshape(nb, MOE_BLOCK), blk_exp))
    yb = yb.reshape(n_slots, D).astype(F32) * slot_w[:, None]
    y = jnp.zeros((n + 1, D), F32).at[slot_tok].add(yb)[:n]
    return y.astype(x.dtype).reshape(B, T, D)


def run_trunk(x, p, past):
    B, T, _ = x.shape
    past_len = 0 if past is None else past['page_table'].shape[1] * PAGE_SIZE
    pos = past_len + jnp.arange(T, dtype=jnp.int32)
    new = {'cmp': [], 'slc': [], 'win': [], 'gla': [], 'lru': [], 'conv': []}
    h = x
    for l in range(DEPTH):
        hn = rmsnorm(h, p['norm_mix'][l])
        if l % 2 == 0:
            mix, (c_rows, s_rows, w_rows, g_state) = mixer_a(hn, pos, p, l // 2, past)
            new['cmp'].append(c_rows)
            new['slc'].append(s_rows)
            new['win'].append(w_rows)
            new['gla'].append(g_state)
        else:
            mix, (h_state, c_buf) = mixer_c(hn, p, l // 2, past)
            new['lru'].append(h_state)
            new['conv'].append(c_buf)
        h = h + mix
        h = h + hier_moe(rmsnorm(h, p['norm_ffn'][l]), p['m_w_rg'][l], p['m_b_rg'][l], p['m_w_re'][l],
                         p['m_b_re'][l], p['m_w1'][l], p['m_w3'][l], p['m_w2'][l])
    return rmsnorm(h, p['norm_final']), new


def setup_inputs(seed: int = 0) -> dict:
    key = jax.random.key(seed)
    ks = iter(jax.random.split(key, 48))

    def nrm(shape, scale=1.0):
        return jax.random.normal(next(ks), shape, F32) * scale

    n_pages = PAST_LEN // PAGE_SIZE
    n_pool = (DEC_BATCH * n_pages * 5) // 4
    win_buf = min(WINDOW, PAST_LEN)
    page_table = jax.random.permutation(next(ks), n_pool)[:DEC_BATCH * n_pages]
    page_table = page_table.reshape(DEC_BATCH, n_pages).astype(jnp.int32)
    lam_u = jax.random.uniform(next(ks), (N_C_LAYERS, D_RNN), F32, 0.9, 0.999)
    lam_s = lam_u ** (1.0 / LRU_C)
    c_lam = jnp.log(lam_s) - jnp.log1p(-lam_s)
    return {
        'x_prompt': nrm((BATCH, SEQ, D_MODEL)),
        'x_sample': nrm((DEC_BATCH, DEC_SEQ, D_MODEL)),
        'cache_cmp_kv': nrm((N_A_LAYERS, n_pool, PAGE_SIZE, 2, NSA_KV, HD)),
        'cache_slc_kv': nrm((N_A_LAYERS, n_pool, PAGE_SIZE, 2, NSA_KV, HD)),
        'cache_win_kv': nrm((N_A_LAYERS, DEC_BATCH, win_buf, 2, NSA_KV, HD)),
        'state_gla': nrm((N_A_LAYERS, DEC_BATCH, GLA_HEADS, GLA_DK, GLA_DV), 0.5),
        'state_lru': nrm((N_C_LAYERS, DEC_BATCH, D_RNN), 0.5),
        'state_conv': nrm((N_C_LAYERS, DEC_BATCH, CONV_W - 1, D_RNN)),
        'page_table': page_table,
        'norm_mix': 1.0 + nrm((DEPTH, D_MODEL), 0.01),
        'norm_ffn': 1.0 + nrm((DEPTH, D_MODEL), 0.01),
        'norm_final': 1.0 + nrm((D_MODEL,), 0.01),
        'a_w_in': nrm((N_A_LAYERS, D_MODEL, A_IN), D_MODEL ** -0.5),
        'a_cmp_pe': nrm((N_A_LAYERS, CMP_BLOCK, 2, HD), 0.1),
        'a_cmp_w': nrm((N_A_LAYERS, CMP_BLOCK, 2, HD, HD), (CMP_BLOCK * HD) ** -0.5),
        'a_gla_wa2': nrm((N_A_LAYERS, GLA_LOWRANK, GLA_K_W), GLA_LOWRANK ** -0.5),
        'a_gla_ba': nrm((N_A_LAYERS, GLA_K_W), 0.1),
        'a_gla_norm': 1.0 + nrm((N_A_LAYERS, GLA_DV), 0.01),
        'a_w_out': nrm((N_A_LAYERS, A_MIX, D_MODEL), A_MIX ** -0.5),
        'c_w_in': nrm((N_C_LAYERS, D_MODEL, 2 * D_RNN), D_MODEL ** -0.5),
        'c_conv_w': nrm((N_C_LAYERS, CONV_W, D_RNN), CONV_W ** -0.5),
        'c_conv_b': nrm((N_C_LAYERS, D_RNN), 0.01),
        'c_w_a': nrm((N_C_LAYERS, LRU_BLOCKS, LRU_BW, LRU_BW), LRU_BW ** -0.5),
        'c_b_a': nrm((N_C_LAYERS, D_RNN), 0.01),
        'c_w_x': nrm((N_C_LAYERS, LRU_BLOCKS, LRU_BW, LRU_BW), LRU_BW ** -0.5),
        'c_b_x': nrm((N_C_LAYERS, D_RNN), 0.01),
        'c_lam': c_lam,
        'c_w_out': nrm((N_C_LAYERS, D_RNN, D_MODEL), D_RNN ** -0.5),
        'm_w_rg': nrm((DEPTH, D_MODEL, N_GROUPS), D_MODEL ** -0.5),
        'm_b_rg': nrm((DEPTH, N_GROUPS), 0.01),
        'm_w_re': nrm((DEPTH, D_MODEL, N_EXPERTS), D_MODEL ** -0.5),
        'm_b_re': nrm((DEPTH, N_EXPERTS), 0.01),
        'm_w1': nrm((DEPTH, N_EXPERTS, D_MODEL, E_HID), D_MODEL ** -0.5),
        'm_w3': nrm((DEPTH, N_EXPERTS, D_MODEL, E_HID), D_MODEL ** -0.5),
        'm_w2': nrm((DEPTH, N_EXPERTS, E_HID, D_MODEL), E_HID ** -0.5),
    }


def reference(x_prompt, x_sample, cache_cmp_kv, cache_slc_kv, cache_win_kv, state_gla, state_lru, state_conv,
              page_table, norm_mix, norm_ffn, norm_final, a_w_in, a_cmp_pe, a_cmp_w, a_gla_wa2, a_gla_ba,
              a_gla_norm, a_w_out, c_w_in, c_conv_w, c_conv_b, c_w_a, c_b_a, c_w_x, c_b_x, c_lam, c_w_out,
              m_w_rg, m_b_rg, m_w_re, m_b_re, m_w1, m_w3, m_w2):
    p = {'norm_mix': norm_mix, 'norm_ffn': norm_ffn, 'norm_final': norm_final,
         'a_w_in': a_w_in, 'a_cmp_pe': a_cmp_pe, 'a_cmp_w': a_cmp_w, 'a_gla_wa2': a_gla_wa2,
         'a_gla_ba': a_gla_ba, 'a_gla_norm': a_gla_norm, 'a_w_out': a_w_out,
         'c_w_in': c_w_in, 'c_conv_w': c_conv_w, 'c_conv_b': c_conv_b, 'c_w_a': c_w_a, 'c_b_a': c_b_a,
         'c_w_x': c_w_x, 'c_b_x': c_b_x, 'c_lam': c_lam, 'c_w_out': c_w_out,
         'm_w_rg': m_w_rg, 'm_b_rg': m_b_rg, 'm_w_re': m_w_re, 'm_b_re': m_b_re,
         'm_w1': m_w1, 'm_w3': m_w3, 'm_w2': m_w2}
    past = {'cache_cmp_kv': cache_cmp_kv, 'cache_slc_kv': cache_slc_kv, 'cache_win_kv': cache_win_kv,
            'state_gla': state_gla, 'state_lru': state_lru, 'state_conv': state_conv,
            'page_table': page_table}
    y_prompt, np_ = run_trunk(x_prompt, p, None)
    y_sample, ns_ = run_trunk(x_sample, p, past)
    cmp_p, cmp_s = jnp.stack(np_['cmp']), jnp.stack(ns_['cmp'])
    slc_p, slc_s = jnp.stack(np_['slc']), jnp.stack(ns_['slc'])
    win_p, win_s = jnp.stack(np_['win']), jnp.stack(ns_['win'])
    gla_p, gla_s = jnp.stack(np_['gla']), jnp.stack(ns_['gla'])
    lru_p, lru_s = jnp.stack(np_['lru']), jnp.stack(ns_['lru'])
    conv_p, conv_s = jnp.stack(np_['conv']), jnp.stack(ns_['conv'])
    return (y_prompt, y_sample, cmp_p, cmp_s, slc_p, slc_s, win_p, win_s, gla_p, gla_s, lru_p, lru_s, conv_p, conv_s)
```

```python
import functools
import jax, jax.numpy as jnp
from jax import lax
import numpy as np
from jax.experimental import pallas as pl
from jax.experimental.pallas import tpu as pltpu

D_MODEL = 1024
BATCH = 2
SEQ = 8192
DEPTH = 2
DEC_BATCH = 128
DEC_SEQ = 4
PAST_LEN = 2048
PAGE_SIZE = 128
EPS = 1e-6
NSA_HEADS = 8
NSA_KV = 2
NSA_REP = NSA_HEADS // NSA_KV
HD = 64
CMP_BLOCK = 32
SLC_BLOCK = 64
SLC_TOPK = 16
WINDOW = 512
Q_BLOCK = 128
ROPE_DIM = HD // 4
ROPE_THETA = 500000.0
GLA_HEADS = 4
GLA_DK = 64
GLA_DV = 128
GLA_LOWRANK = 16
GLA_TAU = 16.0
GLA_CHUNK = 64
D_RNN = 1280
LRU_BLOCKS = 10
LRU_BW = D_RNN // LRU_BLOCKS
CONV_W = 4
LRU_C = 8.0
N_GROUPS = 4
EXP_PER_GROUP = 8
N_EXPERTS = N_GROUPS * EXP_PER_GROUP
E_HID = 512
TOP_K_IN_GROUP = 2
MOE_BLOCK = 128
NSA_Q_W = NSA_HEADS * HD
NSA_KV_W = 2 * NSA_KV * HD
GLA_K_W = GLA_HEADS * GLA_DK
GLA_V_W = GLA_HEADS * GLA_DV
A_SIZES = (NSA_Q_W, NSA_KV_W, NSA_KV_W, NSA_KV_W, 3 * NSA_HEADS, GLA_K_W, GLA_K_W, GLA_V_W, GLA_LOWRANK, GLA_V_W)
A_IN = sum(A_SIZES)
A_MIX = NSA_Q_W + GLA_V_W

F32 = jnp.float32
BF16 = jnp.bfloat16
VMEM_LIMIT_BYTES = 56 * 1024 * 1024
ROW_TILE = 512


def _norm_matmul_body(x_ref, g_ref, w_ref, o_ref):
    x = x_ref[...]
    y = x * lax.rsqrt(jnp.mean(x * x, axis=-1, keepdims=True) + EPS) * g_ref[...]
    o_ref[...] = jnp.dot(y.astype(BF16), w_ref[...], preferred_element_type=F32)


def _matmul_body(x_ref, w_ref, o_ref):
    o_ref[...] = jnp.dot(x_ref[...].astype(BF16), w_ref[...], preferred_element_type=F32)


def norm_matmul(x, g, w):
    n, k = x.shape
    nn = w.shape[1]
    return pl.pallas_call(
        _norm_matmul_body,
        grid=(n // ROW_TILE,),
        in_specs=[pl.BlockSpec((ROW_TILE, k), lambda i: (i, 0)),
                  pl.BlockSpec((1, k), lambda i: (0, 0)),
                  pl.BlockSpec((k, nn), lambda i: (0, 0))],
        out_specs=pl.BlockSpec((ROW_TILE, nn), lambda i: (i, 0)),
        out_shape=jax.ShapeDtypeStruct((n, nn), F32),
        compiler_params=pltpu.CompilerParams(dimension_semantics=("arbitrary",),
                                             vmem_limit_bytes=VMEM_LIMIT_BYTES),
        name="norm_matmul",
    )(x, g.reshape(1, k), w.astype(BF16))


def matmul(x, w):
    n, k = x.shape
    nn = w.shape[1]
    return pl.pallas_call(
        _matmul_body,
        grid=(n // ROW_TILE,),
        in_specs=[pl.BlockSpec((ROW_TILE, k), lambda i: (i, 0)),
                  pl.BlockSpec((k, nn), lambda i: (0, 0))],
        out_specs=pl.BlockSpec((ROW_TILE, nn), lambda i: (i, 0)),
        out_shape=jax.ShapeDtypeStruct((n, nn), F32),
        compiler_params=pltpu.CompilerParams(dimension_semantics=("arbitrary",),
                                             vmem_limit_bytes=VMEM_LIMIT_BYTES),
        name="matmul",
    )(x, w.astype(BF16))


def rmsnorm(x, g):
    y = x * lax.rsqrt(jnp.mean(x * x, axis=-1, keepdims=True) + EPS)
    return y * g


def rope(x, pos):
    half = ROPE_DIM // 2
    inv = 1.0 / (ROPE_THETA ** (jnp.arange(0, ROPE_DIM, 2, dtype=F32) / ROPE_DIM))
    ang = pos.astype(F32)[:, None] * inv[None, :]
    cos = jnp.cos(ang)[:, None, :]
    sin = jnp.sin(ang)[:, None, :]
    xr = x[..., :ROPE_DIM]
    x1, x2 = xr[..., :half], xr[..., half:]
    rot = jnp.concatenate([x1 * cos - x2 * sin, x1 * sin + x2 * cos], axis=-1)
    return jnp.concatenate([rot, x[..., ROPE_DIM:]], axis=-1)


def masked_softmax(s, mask):
    s = jnp.where(mask, s, -jnp.inf)
    m = jnp.max(s, axis=-1, keepdims=True)
    m = jnp.where(jnp.isfinite(m), m, 0.0)
    e = jnp.where(mask, jnp.exp(s - m), 0.0)
    return e / jnp.maximum(jnp.sum(e, axis=-1, keepdims=True), 1e-30)


def nsa_key_blocks(kv_cmp, kv_slc, pe, w_cmp):
    B, T = kv_cmp.shape[:2]
    nc = T // CMP_BLOCK
    blk = kv_cmp[:, :nc * CMP_BLOCK].reshape(B, nc, CMP_BLOCK, 2, NSA_KV, HD) + pe[:, :, None, :]
    ckv = jnp.einsum('bnlcgd,lcde->bncge', blk, w_cmp)
    ck = rope(ckv[:, :, 0], (jnp.arange(nc, dtype=jnp.int32) + 1) * CMP_BLOCK - 1)
    cv = ckv[:, :, 1]
    ns = -(-T // SLC_BLOCK)
    skv = jnp.pad(kv_slc, ((0, 0), (0, ns * SLC_BLOCK - T), (0, 0), (0, 0), (0, 0)))
    skv = skv.reshape(B, ns, SLC_BLOCK, 2, NSA_KV, HD).transpose(3, 0, 4, 1, 2, 5)
    return ck, cv, skv[0], skv[1]


def nsa_query_block(q, qpos, gates, ck, cv, sk, sv, wk, wv, wpos):
    B, QB = q.shape[:2]
    qg = q.reshape(B, QB, NSA_KV, NSA_REP, HD)
    nc = ck.shape[1]
    c_end = (jnp.arange(nc, dtype=jnp.int32) + 1) * CMP_BLOCK - 1
    c_mask = c_end[None, :] <= qpos[:, None]
    s_c = jnp.einsum('bqgrd,bngd->bgrqn', qg, ck)
    p_c = masked_softmax(s_c, c_mask)
    o_c = jnp.einsum('bgrqn,bngd->bqgrd', p_c, cv)
    ns = sk.shape[2]
    ratio = SLC_BLOCK // CMP_BLOCK
    p_grp = jnp.sum(p_c, axis=2)
    p_grp = jnp.pad(p_grp, ((0, 0), (0, 0), (0, 0), (0, ns * ratio - nc)))
    p_slc = p_grp.reshape(B, NSA_KV, QB, ns, ratio).sum(-1)
    blk = jnp.arange(ns, dtype=jnp.int32)[None, :]
    cur = (qpos // SLC_BLOCK)[:, None]
    forced = (blk == 0) | (blk == cur) | (blk == cur - 1)
    score = jnp.where(forced, jnp.inf, p_slc)
    score = jnp.where(blk <= cur, score, -jnp.inf)
    n_top = min(SLC_TOPK, ns)
    top_v, top_i = lax.top_k(score, n_top)
    bi = jnp.arange(B)[:, None, None, None]
    gi = jnp.arange(NSA_KV)[None, :, None, None]
    k_sel = sk[bi, gi, top_i]
    v_sel = sv[bi, gi, top_i]
    k_pos = top_i[..., None] * SLC_BLOCK + jnp.arange(SLC_BLOCK, dtype=jnp.int32)
    s_mask = (top_v > -jnp.inf)[..., None] & (k_pos <= qpos[None, None, :, None, None])
    s_s = jnp.einsum('bqgrd,bgqnld->bgrqnl', qg, k_sel)
    s_s = s_s.reshape(B, NSA_KV, NSA_REP, QB, n_top * SLC_BLOCK)
    p_s = masked_softmax(s_s, s_mask.reshape(B, NSA_KV, 1, QB, n_top * SLC_BLOCK))
    p_s = p_s.reshape(B, NSA_KV, NSA_REP, QB, n_top, SLC_BLOCK)
    o_s = jnp.einsum('bgrqnl,bgqnld->bqgrd', p_s, v_sel)
    d = qpos[:, None] - wpos[None, :]
    w_mask = (d >= 0) & (d < WINDOW) & (wpos[None, :] >= 0)
    s_w = jnp.einsum('bqgrd,bkgd->bgrqk', qg, wk)
    p_w = masked_softmax(s_w, w_mask)
    o_w = jnp.einsum('bgrqk,bkgd->bqgrd', p_w, wv)
    g = gates.reshape(B, QB, NSA_KV, NSA_REP, 3)
    o = g[..., 0:1] * o_c + g[..., 1:2] * o_s + g[..., 2:3] * o_w
    return o.reshape(B, QB, NSA_Q_W)


def nsa_prompt(q, gates, ck, cv, sk, sv, kv_win):
    B, T = q.shape[:2]
    wkv = jnp.pad(kv_win, ((0, 0), (WINDOW, 0), (0, 0), (0, 0), (0, 0)))

    def one_block(i):
        s0 = i * Q_BLOCK
        qb = lax.dynamic_slice_in_dim(q, s0, Q_BLOCK, axis=1)
        gb = lax.dynamic_slice_in_dim(gates, s0, Q_BLOCK, axis=1)
        wb = lax.dynamic_slice_in_dim(wkv, s0, WINDOW + Q_BLOCK, axis=1)
        qpos = s0 + jnp.arange(Q_BLOCK, dtype=jnp.int32)
        wpos = s0 - WINDOW + jnp.arange(WINDOW + Q_BLOCK, dtype=jnp.int32)
        return nsa_query_block(qb, qpos, gb, ck, cv, sk, sv, wb[:, :, 0], wb[:, :, 1], wpos)

    out = lax.map(one_block, jnp.arange(T // Q_BLOCK, dtype=jnp.int32))
    return jnp.swapaxes(out, 0, 1).reshape(B, T, NSA_Q_W)


def gla_chunked(q, k, v, log_a, s0, chunk):
    B, T, H, DK = q.shape
    nck = T // chunk

    def to_chunks(x):
        return jnp.swapaxes(x.reshape(B, nck, chunk, *x.shape[2:]), 0, 1)

    causal = jnp.tril(jnp.ones((chunk, chunk), dtype=bool))

    def step(S, inp):
        qf, kf, vf, lac = inp
        cum = jnp.cumsum(lac, axis=1)
        o_inter = jnp.einsum('bchk,bhkv->bchv', qf * jnp.exp(cum), S)
        diff = cum[:, :, None] - cum[:, None, :]
        diff = jnp.where(causal[None, :, :, None, None], diff, -jnp.inf)
        att = jnp.einsum('bihk,bjhk,bijhk->bhij', qf, kf, jnp.exp(diff))
        o_intra = jnp.einsum('bhij,bjhv->bihv', att, vf)
        last = cum[:, -1]
        kdec = kf * jnp.exp(last[:, None] - cum)
        S = jnp.exp(last)[..., None] * S + jnp.einsum('bchk,bchv->bhkv', kdec, vf)
        return S, o_inter + o_intra

    S, o = lax.scan(step, s0, (to_chunks(q), to_chunks(k), to_chunks(v), to_chunks(log_a)))
    return jnp.swapaxes(o, 0, 1).reshape(B, T, H, v.shape[-1]), S


def mixer_a(u, pos, p, past):
    B, T, _ = u.shape
    offs = np.cumsum(np.array(A_SIZES))[:-1].tolist()
    q, kvc, kvs, kvw, gts, gq, gk, gv, glr, gg = jnp.split(u, offs, axis=-1)
    q = rope(q.reshape(B, T, NSA_HEADS, HD), pos) * (HD ** -0.5)

    def as_kv(t, rotate):
        t = t.reshape(B, T, 2, NSA_KV, HD)
        if rotate:
            t = jnp.stack([rope(t[:, :, 0], pos), t[:, :, 1]], axis=2)
        return t

    kv_cmp = as_kv(kvc, False)
    kv_slc = as_kv(kvs, True)
    kv_win = as_kv(kvw, True)
    gates = jax.nn.sigmoid(gts.reshape(B, T, NSA_HEADS, 3))
    pe, wc = p['a_cmp_pe'][0], p['a_cmp_w'][0]
    gq = gq.reshape(B, T, GLA_HEADS, GLA_DK) * (GLA_DK ** -0.5)
    gk = gk.reshape(B, T, GLA_HEADS, GLA_DK)
    gv = gv.reshape(B, T, GLA_HEADS, GLA_DV)
    z = jnp.dot(glr, p['a_gla_wa2'][0], precision=lax.Precision.HIGHEST) + p['a_gla_ba'][0]
    log_a = (jax.nn.log_sigmoid(z) / GLA_TAU).reshape(B, T, GLA_HEADS, GLA_DK)
    if past is None:
        ck, cv, sk, sv = nsa_key_blocks(kv_cmp, kv_slc, pe, wc)
        o_nsa = nsa_prompt(q, gates, ck, cv, sk, sv, kv_win)
        win_new = kv_win[:, T - min(WINDOW, T):]
        s0 = jnp.zeros((B, GLA_HEADS, GLA_DK, GLA_DV), F32)
        o_gla, sT = gla_chunked(gq, gk, gv, log_a, s0, GLA_CHUNK)
    else:
        pt = past['page_table']
        old_cmp = past['cache_cmp_kv'][0, pt].reshape(B, -1, 2, NSA_KV, HD)
        old_slc = past['cache_slc_kv'][0, pt].reshape(B, -1, 2, NSA_KV, HD)
        ck, cv, sk, sv = nsa_key_blocks(jnp.concatenate([old_cmp, kv_cmp], axis=1),
                                        jnp.concatenate([old_slc, kv_slc], axis=1), pe, wc)
        buf = past['cache_win_kv'][0]
        wkv = jnp.concatenate([buf, kv_win], axis=1)
        n_w = wkv.shape[1]
        wpos = pos[0] + T - n_w + jnp.arange(n_w, dtype=jnp.int32)
        o_nsa = nsa_query_block(q, pos, gates, ck, cv, sk, sv, wkv[:, :, 0], wkv[:, :, 1], wpos)
        win_new = wkv[:, n_w - buf.shape[1]:]
        o_gla, sT = gla_chunked(gq, gk, gv, log_a, past['state_gla'][0], T)
    o_gla = rmsnorm(o_gla, p['a_gla_norm'][0]) * jax.nn.silu(gg.reshape(B, T, GLA_HEADS, GLA_DV))
    mix_in = jnp.concatenate([o_nsa, o_gla.reshape(B, T, GLA_V_W)], axis=-1)
    return mix_in, (kv_cmp, kv_slc, win_new, sT)


def block_diag(x, w, b):
    lead = x.shape[:-1]
    xb = x.reshape(*lead, LRU_BLOCKS, LRU_BW)
    return (jnp.einsum('btnw,nwv->btnv', xb, w) + b.reshape(LRU_BLOCKS, LRU_BW)).reshape(*lead, D_RNN)


def rglru(x, h0, w_a, b_a, w_x, b_x, lam):
    r = jax.nn.sigmoid(block_diag(x, w_a, b_a))
    i = jax.nn.sigmoid(block_diag(x, w_x, b_x))
    log_a = -LRU_C * r * jax.nn.softplus(-lam)
    a = jnp.exp(log_a)
    u = jnp.sqrt(-jnp.expm1(2.0 * log_a)) * (i * x)

    def step(h, inp):
        a_t, u_t = inp
        h = a_t * h + u_t
        return h, h

    hT, hs = lax.scan(step, h0, (jnp.swapaxes(a, 0, 1), jnp.swapaxes(u, 0, 1)))
    return jnp.swapaxes(hs, 0, 1), hT


def mixer_c(u, p, past):
    B, T, _ = u.shape
    gate_b, x_b = jnp.split(u, 2, axis=-1)
    if past is None:
        buf = jnp.zeros((B, CONV_W - 1, D_RNN), F32)
        h0 = jnp.zeros((B, D_RNN), F32)
    else:
        buf = past['state_conv'][0]
        h0 = past['state_lru'][0]
    xp = jnp.concatenate([buf, x_b], axis=1)
    w = p['c_conv_w'][0][:, None, :]
    xc = lax.conv_general_dilated(xp, w, (1,), 'VALID', dimension_numbers=('NWC', 'WIO', 'NWC'),
                                  feature_group_count=D_RNN) + p['c_conv_b'][0]
    new_buf = xp[:, xp.shape[1] - (CONV_W - 1):]
    y, hT = rglru(xc, h0, p['c_w_a'][0], p['c_b_a'][0], p['c_w_x'][0], p['c_b_x'][0], p['c_lam'][0])
    return jax.nn.gelu(gate_b) * y, (hT, new_buf)


def hier_moe(xt, w_rg, b_rg, w_re, b_re, w1, w3, w2):
    n, D = xt.shape
    rows = jnp.arange(n)
    hp = lax.Precision.HIGHEST
    g_logit = jnp.dot(xt, w_rg, precision=hp) + b_rg
    g_prob = jax.nn.softmax(g_logit, axis=-1)
    g_top = jnp.argmax(g_logit, axis=-1).astype(jnp.int32)
    g_w = g_prob[rows, g_top][:, None]
    e_logit = (jnp.dot(xt, w_re, precision=hp) + b_re).reshape(n, N_GROUPS, EXP_PER_GROUP)
    e_logit = e_logit[rows, g_top]
    e_val, e_idx = lax.top_k(e_logit, TOP_K_IN_GROUP)
    gate = jax.nn.softmax(e_val, axis=-1) * g_w
    expert = (g_top[:, None] * EXP_PER_GROUP + e_idx).astype(jnp.int32)
    nk = n * TOP_K_IN_GROUP
    flat_e = expert.reshape(nk)
    flat_t = jnp.repeat(jnp.arange(n, dtype=jnp.int32), TOP_K_IN_GROUP)
    flat_w = gate.reshape(nk)
    order = jnp.argsort(flat_e)
    se, st, sw = flat_e[order], flat_t[order], flat_w[order]
    counts = jnp.bincount(flat_e, length=N_EXPERTS)
    padded = ((counts + MOE_BLOCK - 1) // MOE_BLOCK) * MOE_BLOCK
    pad_end = jnp.cumsum(padded)
    pad_start = pad_end - padded
    cnt_start = jnp.cumsum(counts) - counts
    dest = pad_start[se] + (jnp.arange(nk, dtype=jnp.int32) - cnt_start[se])
    n_slots = (-(-nk // MOE_BLOCK) + N_EXPERTS) * MOE_BLOCK
    nb = n_slots // MOE_BLOCK
    slot_tok = jnp.full((n_slots,), n, jnp.int32).at[dest].set(st)
    slot_w = jnp.zeros((n_slots,), F32).at[dest].set(sw)
    blk_start = jnp.arange(nb, dtype=jnp.int32) * MOE_BLOCK
    blk_exp = jnp.minimum(jnp.searchsorted(pad_end, blk_start, side='right'), N_EXPERTS - 1)
    x_pad = jnp.concatenate([xt, jnp.zeros((1, D), xt.dtype)], axis=0)

    def expert_block(args):
        toks, e = args
        xb = x_pad[toks]
        hdn = jax.nn.silu(xb @ w1[e]) * (xb @ w3[e])
        return hdn @ w2[e]

    yb = lax.map(expert_block, (slot_tok.reshape(nb, MOE_BLOCK), blk_exp))
    yb = yb.reshape(n_slots, D) * slot_w[:, None]
    return jnp.zeros((n + 1, D), F32).at[slot_tok].add(yb)[:n]


def run_trunk(x, p, past):
    B, T, _ = x.shape
    n = B * T
    past_len = 0 if past is None else PAST_LEN
    pos = past_len + jnp.arange(T, dtype=jnp.int32)
    h = x.reshape(n, D_MODEL)
    u = norm_matmul(h, p['norm_mix'][0], p['a_w_in'][0])
    mix_in, (c_rows, s_rows, w_rows, g_state) = mixer_a(u.reshape(B, T, A_IN), pos, p, past)
    h = h + matmul(mix_in.reshape(n, A_MIX), p['a_w_out'][0])
    h = h + hier_moe(rmsnorm(h, p['norm_ffn'][0]), p['m_w_rg'][0], p['m_b_rg'][0], p['m_w_re'][0],
                     p['m_b_re'][0], p['m_w1'][0], p['m_w3'][0], p['m_w2'][0])
    u = norm_matmul(h, p['norm_mix'][1], p['c_w_in'][0])
    y, (h_state, c_buf) = mixer_c(u.reshape(B, T, 2 * D_RNN), p, past)
    h = h + matmul(y.reshape(n, D_RNN), p['c_w_out'][0])
    h = h + hier_moe(rmsnorm(h, p['norm_ffn'][1]), p['m_w_rg'][1], p['m_b_rg'][1], p['m_w_re'][1],
                     p['m_b_re'][1], p['m_w1'][1], p['m_w3'][1], p['m_w2'][1])
    y_out = rmsnorm(h, p['norm_final']).reshape(B, T, D_MODEL)
    return y_out, (c_rows, s_rows, w_rows, g_state, h_state, c_buf)


def kernel(x_prompt, x_sample, cache_cmp_kv, cache_slc_kv, cache_win_kv, state_gla, state_lru, state_conv,
           page_table, norm_mix, norm_ffn, norm_final, a_w_in, a_cmp_pe, a_cmp_w, a_gla_wa2, a_gla_ba,
           a_gla_norm, a_w_out, c_w_in, c_conv_w, c_conv_b, c_w_a, c_b_a, c_w_x, c_b_x, c_lam, c_w_out,
           m_w_rg, m_b_rg, m_w_re, m_b_re, m_w1, m_w3, m_w2):
    p = {'norm_mix': norm_mix, 'norm_ffn': norm_ffn, 'norm_final': norm_final,
         'a_w_in': a_w_in, 'a_cmp_pe': a_cmp_pe, 'a_cmp_w': a_cmp_w, 'a_gla_wa2': a_gla_wa2,
         'a_gla_ba': a_gla_ba, 'a_gla_norm': a_gla_norm, 'a_w_out': a_w_out,
         'c_w_in': c_w_in, 'c_conv_w': c_conv_w, 'c_conv_b': c_conv_b, 'c_w_a': c_w_a, 'c_b_a': c_b_a,
         'c_w_x': c_w_x, 'c_b_x': c_b_x, 'c_lam': c_lam, 'c_w_out': c_w_out,
         'm_w_rg': m_w_rg, 'm_b_rg': m_b_rg, 'm_w_re': m_w_re, 'm_b_re': m_b_re,
         'm_w1': m_w1, 'm_w3': m_w3, 'm_w2': m_w2}
    past = {'cache_cmp_kv': cache_cmp_kv, 'cache_slc_kv': cache_slc_kv, 'cache_win_kv': cache_win_kv,
            'state_gla': state_gla, 'state_lru': state_lru, 'state_conv': state_conv,
            'page_table': page_table}
    y_p, sp = run_trunk(x_prompt, p, None)
    y_s, ss = run_trunk(x_sample, p, past)
    outs = [y_p, y_s]
    for a, b in zip(sp, ss):
        outs += [a[None], b[None]]
    return tuple(outs)
```

```python
import functools
import jax, jax.numpy as jnp
from jax import lax
import numpy as np
from jax.experimental import pallas as pl
from jax.experimental.pallas import tpu as pltpu

D_MODEL = 1024
BATCH = 2
SEQ = 8192
DEPTH = 2
DEC_BATCH = 128
DEC_SEQ = 4
PAST_LEN = 2048
PAGE_SIZE = 128
EPS = 1e-6
NSA_HEADS = 8
NSA_KV = 2
NSA_REP = NSA_HEADS // NSA_KV
HD = 64
CMP_BLOCK = 32
SLC_BLOCK = 64
SLC_TOPK = 16
WINDOW = 512
Q_BLOCK = 128
ROPE_DIM = HD // 4
ROPE_THETA = 500000.0
GLA_HEADS = 4
GLA_DK = 64
GLA_DV = 128
GLA_LOWRANK = 16
GLA_TAU = 16.0
GLA_CHUNK = 64
D_RNN = 1280
LRU_BLOCKS = 10
LRU_BW = D_RNN // LRU_BLOCKS
CONV_W = 4
LRU_C = 8.0
N_GROUPS = 4
EXP_PER_GROUP = 8
N_EXPERTS = N_GROUPS * EXP_PER_GROUP
E_HID = 512
TOP_K_IN_GROUP = 2
MOE_BLOCK = 128
NSA_Q_W = NSA_HEADS * HD
NSA_KV_W = 2 * NSA_KV * HD
GLA_K_W = GLA_HEADS * GLA_DK
GLA_V_W = GLA_HEADS * GLA_DV
A_SIZES = (NSA_Q_W, NSA_KV_W, NSA_KV_W, NSA_KV_W, 3 * NSA_HEADS, GLA_K_W, GLA_K_W, GLA_V_W, GLA_LOWRANK, GLA_V_W)
A_IN = sum(A_SIZES)
A_MIX = NSA_Q_W + GLA_V_W

F32 = jnp.float32
BF16 = jnp.bfloat16
VMEM_LIMIT_BYTES = 56 * 1024 * 1024
ROW_TILE = 512


def _norm_matmul_body(x_ref, g_ref, w_ref, o_ref):
    x = x_ref[...]
    y = x * lax.rsqrt(jnp.mean(x * x, axis=-1, keepdims=True) + EPS) * g_ref[...]
    o_ref[...] = jnp.dot(y.astype(BF16), w_ref[...], preferred_element_type=F32)


def _matmul_body(x_ref, w_ref, o_ref):
    o_ref[...] = jnp.dot(x_ref[...].astype(BF16), w_ref[...], preferred_element_type=F32)


def norm_matmul(x, g, w):
    n, k = x.shape
    nn = w.shape[1]
    return pl.pallas_call(
        _norm_matmul_body,
        grid=(n // ROW_TILE,),
        in_specs=[pl.BlockSpec((ROW_TILE, k), lambda i: (i, 0)),
                  pl.BlockSpec((1, k), lambda i: (0, 0)),
                  pl.BlockSpec((k, nn), lambda i: (0, 0))],
        out_specs=pl.BlockSpec((ROW_TILE, nn), lambda i: (i, 0)),
        out_shape=jax.ShapeDtypeStruct((n, nn), F32),
        compiler_params=pltpu.CompilerParams(dimension_semantics=("arbitrary",),
                                             vmem_limit_bytes=VMEM_LIMIT_BYTES),
        name="norm_matmul",
    )(x, g.reshape(1, k), w.astype(BF16))


def matmul(x, w):
    n, k = x.shape
    nn = w.shape[1]
    return pl.pallas_call(
        _matmul_body,
        grid=(n // ROW_TILE,),
        in_specs=[pl.BlockSpec((ROW_TILE, k), lambda i: (i, 0)),
                  pl.BlockSpec((k, nn), lambda i: (0, 0))],
        out_specs=pl.BlockSpec((ROW_TILE, nn), lambda i: (i, 0)),
        out_shape=jax.ShapeDtypeStruct((n, nn), F32),
        compiler_params=pltpu.CompilerParams(dimension_semantics=("arbitrary",),
                                             vmem_limit_bytes=VMEM_LIMIT_BYTES),
        name="matmul",
    )(x, w.astype(BF16))


LRU_TIME_TILE = 256
SUBLANES = 8


def _softplus(x):
    return jnp.maximum(x, 0.0) + jnp.log1p(jnp.exp(-jnp.abs(x)))


def _gelu_tanh(x):
    return x * (0.5 * (1.0 + jnp.tanh(0.7978845608028654 * (x + 0.044715 * (x * x * x)))))


def _lru_gates(xc, wa_ref, ba, wx_ref, bx, lam):
    xcb = xc.astype(BF16)
    r_parts, i_parts = [], []
    for n in range(LRU_BLOCKS):
        xs = xcb[:, n * LRU_BW:(n + 1) * LRU_BW]
        r_parts.append(jnp.dot(xs, wa_ref[n], preferred_element_type=F32))
        i_parts.append(jnp.dot(xs, wx_ref[n], preferred_element_type=F32))
    r = jax.nn.sigmoid(jnp.concatenate(r_parts, axis=-1) + ba)
    i = jax.nn.sigmoid(jnp.concatenate(i_parts, axis=-1) + bx)
    log_a = -LRU_C * r * _softplus(-lam)
    a = jnp.exp(log_a)
    u = jnp.sqrt(-jnp.tanh(log_a) * (a * a + 1.0)) * (i * xc)
    return a, u


def _lru_seq_body(u_ref, cw_ref, cb_ref, wa_ref, ba_ref, wx_ref, bx_ref, lam_ref, y_ref, hT_ref, xp_sc, h_sc):
    tt = LRU_TIME_TILE

    @pl.when(pl.program_id(1) == 0)
    def _():
        xp_sc[0:SUBLANES, :] = jnp.zeros((SUBLANES, D_RNN), F32)
        h_sc[...] = jnp.zeros((1, D_RNN), F32)

    xp_sc[SUBLANES:SUBLANES + tt, :] = u_ref[0, :, D_RNN:]
    xc = cb_ref[...]
    for w in range(CONV_W):
        off = SUBLANES - (CONV_W - 1) + w
        xc = xc + cw_ref[w:w + 1, :] * xp_sc[off:off + tt, :]
    a, u = _lru_gates(xc, wa_ref, ba_ref[...], wx_ref, bx_ref[...], lam_ref[...])
    row = lax.broadcasted_iota(jnp.int32, (tt, D_RNN), 0)
    d = 1
    while d < tt:
        keep = row >= d
        a_prev = jnp.where(keep, pltpu.roll(a, d, 0), 1.0)
        u_prev = jnp.where(keep, pltpu.roll(u, d, 0), 0.0)
        u = a * u_prev + u
        a = a * a_prev
        d *= 2
    h = a * h_sc[...] + u
    h_sc[...] = h[tt - 1:tt, :]
    hT_ref[0] = h[tt - 1:tt, :]
    y_ref[0] = (_gelu_tanh(u_ref[0, :, :D_RNN]) * h).astype(BF16)
    xp_sc[0:SUBLANES, :] = xp_sc[tt:tt + SUBLANES, :]


def _lru_weight_args(cw, cb, wa, ba, wx, bx, lam):
    row = lambda v: v.reshape(1, D_RNN)
    return (cw, row(cb), wa.astype(BF16), row(ba), wx.astype(BF16), row(bx), row(lam))


def lru_seq(u, cw, cb, wa, ba, wx, bx, lam):
    B, T, _ = u.shape
    tt = LRU_TIME_TILE
    full = lambda shape: pl.BlockSpec(shape, lambda b, t: (0,) * len(shape))
    y, hT = pl.pallas_call(
        _lru_seq_body,
        grid=(B, T // tt),
        in_specs=[pl.BlockSpec((1, tt, 2 * D_RNN), lambda b, t: (b, t, 0)),
                  full((CONV_W, D_RNN)), full((1, D_RNN)),
                  full((LRU_BLOCKS, LRU_BW, LRU_BW)), full((1, D_RNN)),
                  full((LRU_BLOCKS, LRU_BW, LRU_BW)), full((1, D_RNN)), full((1, D_RNN))],
        out_specs=[pl.BlockSpec((1, tt, D_RNN), lambda b, t: (b, t, 0)),
                   pl.BlockSpec((1, 1, D_RNN), lambda b, t: (b, 0, 0))],
        out_shape=[jax.ShapeDtypeStruct((B, T, D_RNN), BF16),
                   jax.ShapeDtypeStruct((B, 1, D_RNN), F32)],
        scratch_shapes=[pltpu.VMEM((tt + SUBLANES, D_RNN), F32), pltpu.VMEM((1, D_RNN), F32)],
        compiler_params=pltpu.CompilerParams(dimension_semantics=("arbitrary", "arbitrary"),
                                             vmem_limit_bytes=VMEM_LIMIT_BYTES),
        name="lru_seq",
    )(u, *_lru_weight_args(cw, cb, wa, ba, wx, bx, lam))
    return y, hT.reshape(B, D_RNN)


def _lru_step_body(u_ref, cs_ref, h0_ref, cw_ref, cb_ref, wa_ref, ba_ref, wx_ref, bx_ref, lam_ref, y_ref, hT_ref):
    n_t = u_ref.shape[0]
    hist = [cs_ref[:, w, :] for w in range(CONV_W - 1)] + [u_ref[t, :, D_RNN:] for t in range(n_t)]
    h = h0_ref[...]
    for t in range(n_t):
        xc = cb_ref[...]
        for w in range(CONV_W):
            xc = xc + cw_ref[w:w + 1, :] * hist[t + w]
        a, u = _lru_gates(xc, wa_ref, ba_ref[...], wx_ref, bx_ref[...], lam_ref[...])
        h = a * h + u
        y_ref[t] = (_gelu_tanh(u_ref[t, :, :D_RNN]) * h).astype(BF16)
    hT_ref[...] = h


def lru_step(u, conv_state, h0, cw, cb, wa, ba, wx, bx, lam):
    T, B, _ = u.shape
    return pl.pallas_call(
        _lru_step_body,
        out_shape=[jax.ShapeDtypeStruct((T, B, D_RNN), BF16), jax.ShapeDtypeStruct((B, D_RNN), F32)],
        compiler_params=pltpu.CompilerParams(vmem_limit_bytes=VMEM_LIMIT_BYTES),
        name="lru_step",
    )(u, conv_state, h0, *_lru_weight_args(cw, cb, wa, ba, wx, bx, lam))


LANES = 128
ROUTE_E1, ROUTE_E2, ROUTE_G1, ROUTE_G2, ROUTE_R1, ROUTE_R2 = range(6)
EXPERT_LANE0 = N_GROUPS


def _lane_pick(val_by_lane):
    rows = next(iter(val_by_lane.values())).shape[0]
    lane = lax.broadcasted_iota(jnp.int32, (rows, LANES), 1)
    out = jnp.zeros((rows, LANES), F32)
    for l, v in val_by_lane.items():
        out = jnp.where(lane == l, v, out)
    return out


def _route_rows(logits, tri_ref, carry_ref):
    rows = logits.shape[0]
    lane = lax.broadcasted_iota(jnp.int32, (rows, LANES), 1)
    neg = -jnp.inf
    gl = jnp.where(lane < N_GROUPS, logits, neg)
    gmax = jnp.max(gl, axis=-1, keepdims=True)
    gtop = jnp.min(jnp.where(gl == gmax, lane, LANES), axis=-1, keepdims=True)
    gsum = jnp.sum(jnp.where(lane < N_GROUPS, jnp.exp(logits - gmax), 0.0), axis=-1, keepdims=True)
    g_w = 1.0 / gsum
    lo = EXPERT_LANE0 + EXP_PER_GROUP * gtop
    el = jnp.where((lane >= lo) & (lane < lo + EXP_PER_GROUP), logits, neg)
    v1 = jnp.max(el, axis=-1, keepdims=True)
    i1 = jnp.min(jnp.where(el == v1, lane, LANES), axis=-1, keepdims=True)
    el2 = jnp.where(lane == i1, neg, el)
    v2 = jnp.max(el2, axis=-1, keepdims=True)
    i2 = jnp.min(jnp.where(el2 == v2, lane, LANES), axis=-1, keepdims=True)
    p2 = jnp.exp(v2 - v1)
    den = 1.0 + p2
    gate1 = (1.0 / den) * g_w
    gate2 = (p2 / den) * g_w
    hit1 = lane == i1
    hit2 = lane == i2
    onehot = jnp.where(hit1 | hit2, 1.0, 0.0)
    before = jnp.dot(tri_ref[...], onehot.astype(BF16), preferred_element_type=F32) + carry_ref[...]
    rank1 = jnp.sum(jnp.where(hit1, before, 0.0), axis=-1, keepdims=True)
    rank2 = jnp.sum(jnp.where(hit2, before, 0.0), axis=-1, keepdims=True)
    carry_ref[...] = carry_ref[...] + jnp.sum(onehot, axis=0, keepdims=True)
    return _lane_pick({ROUTE_E1: (i1 - EXPERT_LANE0).astype(F32), ROUTE_E2: (i2 - EXPERT_LANE0).astype(F32),
                       ROUTE_G1: gate1, ROUTE_G2: gate2, ROUTE_R1: rank1, ROUTE_R2: rank2})


def _post_mixer_body(h_ref, m_ref, wo_ref, g_ref, wr_ref, br_ref, tri_ref,
                     h1_ref, hn_ref, route_ref, cnt_ref, carry_sc):
    @pl.when(pl.program_id(0) == 0)
    def _():
        carry_sc[...] = jnp.zeros((1, LANES), F32)

    h1 = h_ref[...] + jnp.dot(m_ref[...], wo_ref[...], preferred_element_type=F32)
    h1_ref[...] = h1
    hn = (h1 * lax.rsqrt(jnp.mean(h1 * h1, axis=-1, keepdims=True) + EPS) * g_ref[...]).astype(BF16)
    hn_ref[...] = hn
    logits = jnp.dot(hn, wr_ref[...], preferred_element_type=F32) + br_ref[...]
    route_ref[...] = _route_rows(logits, tri_ref, carry_sc)
    cnt_ref[...] = carry_sc[...]


def post_mixer(h, mix_in, w_out, g_ffn, w_rg, b_rg, w_re, b_re):
    n, k = mix_in.shape
    pad = LANES - N_GROUPS - N_EXPERTS
    wr = jnp.concatenate([w_rg, w_re, jnp.zeros((D_MODEL, pad), F32)], axis=1).astype(BF16)
    br = jnp.concatenate([b_rg, b_re, jnp.zeros((pad,), F32)]).reshape(1, LANES)
    tri = jnp.tril(jnp.ones((ROW_TILE, ROW_TILE), BF16), -1)
    full = lambda shape: pl.BlockSpec(shape, lambda i: (0,) * len(shape))
    rows = lambda w: pl.BlockSpec((ROW_TILE, w), lambda i: (i, 0))
    return pl.pallas_call(
        _post_mixer_body,
        grid=(n // ROW_TILE,),
        in_specs=[rows(D_MODEL), rows(k), full((k, D_MODEL)), full((1, D_MODEL)),
                  full((D_MODEL, LANES)), full((1, LANES)), full((ROW_TILE, ROW_TILE))],
        out_specs=[rows(D_MODEL), rows(D_MODEL), rows(LANES), full((1, LANES))],
        out_shape=[jax.ShapeDtypeStruct((n, D_MODEL), F32), jax.ShapeDtypeStruct((n, D_MODEL), BF16),
                   jax.ShapeDtypeStruct((n, LANES), F32), jax.ShapeDtypeStruct((1, LANES), F32)],
        scratch_shapes=[pltpu.VMEM((1, LANES), F32)],
        compiler_params=pltpu.CompilerParams(dimension_semantics=("arbitrary",),
                                             vmem_limit_bytes=VMEM_LIMIT_BYTES),
        name="post_mixer",
    )(h, mix_in, w_out.astype(BF16), g_ffn.reshape(1, D_MODEL), wr, br, tri)


MOE_ROWS = 256


def _ffn_body(be_ref, nb_ref, x_ref, w1_ref, w3_ref, w2_ref, y_ref, w1_sc, w3_sc, w2_sc):
    i = pl.program_id(0)
    new_expert = jnp.logical_or(i == 0, be_ref[i] != be_ref[jnp.maximum(i - 1, 0)])

    @pl.when(jnp.logical_and(new_expert, i < nb_ref[0]))
    def _():
        w1_sc[...] = w1_ref[0].astype(BF16)
        w3_sc[...] = w3_ref[0].astype(BF16)
        w2_sc[...] = w2_ref[0].astype(BF16)

    @pl.when(i < nb_ref[0])
    def _():
        x = x_ref[...]
        a = jnp.dot(x, w1_sc[...], preferred_element_type=F32)
        b = jnp.dot(x, w3_sc[...], preferred_element_type=F32)
        hdn = (a * jax.nn.sigmoid(a) * b).astype(BF16)
        y_ref[...] = jnp.dot(hdn, w2_sc[...], preferred_element_type=F32)

    @pl.when(i >= nb_ref[0])
    def _():
        y_ref[...] = jnp.zeros(y_ref.shape, F32)


def expert_ffn(xs, blk_exp, n_active, w1, w3, w2):
    n_slots = xs.shape[0]
    nb = n_slots // MOE_ROWS
    wmap = lambda i, be, na: (be[i], 0, 0)
    xmap = lambda i, be, na: (jnp.minimum(i, na[0] - 1), 0)
    return pl.pallas_call(
        _ffn_body,
        grid_spec=pltpu.PrefetchScalarGridSpec(
            num_scalar_prefetch=2,
            grid=(nb,),
            in_specs=[pl.BlockSpec((MOE_ROWS, D_MODEL), xmap),
                      pl.BlockSpec((1, D_MODEL, E_HID), wmap),
                      pl.BlockSpec((1, D_MODEL, E_HID), wmap),
                      pl.BlockSpec((1, E_HID, D_MODEL), wmap)],
            out_specs=pl.BlockSpec((MOE_ROWS, D_MODEL), lambda i, be, na: (i, 0)),
            scratch_shapes=[pltpu.VMEM((D_MODEL, E_HID), BF16), pltpu.VMEM((D_MODEL, E_HID), BF16),
                            pltpu.VMEM((E_HID, D_MODEL), BF16)],
        ),
        out_shape=jax.ShapeDtypeStruct((n_slots, D_MODEL), F32),
        compiler_params=pltpu.CompilerParams(dimension_semantics=("arbitrary",),
                                             vmem_limit_bytes=VMEM_LIMIT_BYTES),
        name="expert_ffn",
    )(blk_exp, n_active, xs, w1, w3, w2)


def moe_dispatch(route, counts_row, n):
    e = route[:, ROUTE_E1:ROUTE_E2 + 1].astype(jnp.int32)
    rank = route[:, ROUTE_R1:ROUTE_R2 + 1].astype(jnp.int32)
    counts = counts_row[0, EXPERT_LANE0:EXPERT_LANE0 + N_EXPERTS].astype(jnp.int32)
    padded = ((counts + MOE_ROWS - 1) // MOE_ROWS) * MOE_ROWS
    pad_end = jnp.cumsum(padded)
    pad_start = pad_end - padded
    dest = pad_start[e] + rank
    nb = -(-(n * TOP_K_IN_GROUP) // MOE_ROWS) + N_EXPERTS
    n_slots = nb * MOE_ROWS
    tok = jnp.broadcast_to(jnp.arange(n, dtype=jnp.int32)[:, None], (n, TOP_K_IN_GROUP))
    slot_tok = jnp.zeros((n_slots,), jnp.int32).at[dest.reshape(-1)].set(tok.reshape(-1))
    blk_start = jnp.arange(nb, dtype=jnp.int32) * MOE_ROWS
    blk_exp = jnp.minimum(jnp.searchsorted(pad_end, blk_start, side='right'), N_EXPERTS - 1).astype(jnp.int32)
    n_active = (pad_end[-1] // MOE_ROWS).astype(jnp.int32).reshape(1)
    return slot_tok, dest, blk_exp, n_active


def _combine(h_ref, yg_ref, route_ref):
    lane = lax.broadcasted_iota(jnp.int32, (ROW_TILE, LANES), 1)
    r = route_ref[...]
    g1 = jnp.sum(jnp.where(lane == ROUTE_G1, r, 0.0), axis=-1, keepdims=True)
    g2 = jnp.sum(jnp.where(lane == ROUTE_G2, r, 0.0), axis=-1, keepdims=True)
    return h_ref[...] + (yg_ref[:, :D_MODEL] * g1 + yg_ref[:, D_MODEL:] * g2)


def _combine_proj_body(h_ref, yg_ref, route_ref, g_ref, w_ref, h2_ref, u_ref):
    h2 = _combine(h_ref, yg_ref, route_ref)
    h2_ref[...] = h2
    hn = h2 * lax.rsqrt(jnp.mean(h2 * h2, axis=-1, keepdims=True) + EPS) * g_ref[...]
    u_ref[...] = jnp.dot(hn.astype(BF16), w_ref[...], preferred_element_type=F32)


def _combine_norm_body(h_ref, yg_ref, route_ref, g_ref, y_ref):
    h2 = _combine(h_ref, yg_ref, route_ref)
    y_ref[...] = h2 * lax.rsqrt(jnp.mean(h2 * h2, axis=-1, keepdims=True) + EPS) * g_ref[...]


def combine_proj(h, yg, route, g, w):
    n = h.shape[0]
    nn = w.shape[1]
    full = lambda shape: pl.BlockSpec(shape, lambda i: (0,) * len(shape))
    rows = lambda wd: pl.BlockSpec((ROW_TILE, wd), lambda i: (i, 0))
    return pl.pallas_call(
        _combine_proj_body,
        grid=(n // ROW_TILE,),
        in_specs=[rows(D_MODEL), rows(2 * D_MODEL), rows(LANES), full((1, D_MODEL)), full((D_MODEL, nn))],
        out_specs=[rows(D_MODEL), rows(nn)],
        out_shape=[jax.ShapeDtypeStruct((n, D_MODEL), F32), jax.ShapeDtypeStruct((n, nn), F32)],
        compiler_params=pltpu.CompilerParams(dimension_semantics=("arbitrary",),
                                             vmem_limit_bytes=VMEM_LIMIT_BYTES),
        name="combine_proj",
    )(h, yg, route, g.reshape(1, D_MODEL), w.astype(BF16))


def combine_norm(h, yg, route, g):
    n = h.shape[0]
    full = lambda shape: pl.BlockSpec(shape, lambda i: (0,) * len(shape))
    rows = lambda wd: pl.BlockSpec((ROW_TILE, wd), lambda i: (i, 0))
    return pl.pallas_call(
        _combine_norm_body,
        grid=(n // ROW_TILE,),
        in_specs=[rows(D_MODEL), rows(2 * D_MODEL), rows(LANES), full((1, D_MODEL))],
        out_specs=rows(D_MODEL),
        out_shape=jax.ShapeDtypeStruct((n, D_MODEL), F32),
        compiler_params=pltpu.CompilerParams(dimension_semantics=("arbitrary",),
                                             vmem_limit_bytes=VMEM_LIMIT_BYTES),
        name="combine_norm",
    )(h, yg, route, g.reshape(1, D_MODEL))


def moe_experts(hn, route, counts_row, w1, w3, w2):
    n = hn.shape[0]
    slot_tok, dest, blk_exp, n_active = moe_dispatch(route, counts_row, n)
    xs = jnp.take(hn, slot_tok, axis=0)
    ys = expert_ffn(xs, blk_exp, n_active, w1, w3, w2)
    return jnp.take(ys, dest.reshape(-1), axis=0).reshape(n, 2 * D_MODEL)


def rmsnorm(x, g):
    y = x * lax.rsqrt(jnp.mean(x * x, axis=-1, keepdims=True) + EPS)
    return y * g


def rope(x, pos):
    half = ROPE_DIM // 2
    inv = 1.0 / (ROPE_THETA ** (jnp.arange(0, ROPE_DIM, 2, dtype=F32) / ROPE_DIM))
    ang = pos.astype(F32)[:, None] * inv[None, :]
    cos = jnp.cos(ang)[:, None, :]
    sin = jnp.sin(ang)[:, None, :]
    xr = x[..., :ROPE_DIM]
    x1, x2 = xr[..., :half], xr[..., half:]
    rot = jnp.concatenate([x1 * cos - x2 * sin, x1 * sin + x2 * cos], axis=-1)
    return jnp.concatenate([rot, x[..., ROPE_DIM:]], axis=-1)


def masked_softmax(s, mask):
    s = jnp.where(mask, s, -jnp.inf)
    m = jnp.max(s, axis=-1, keepdims=True)
    m = jnp.where(jnp.isfinite(m), m, 0.0)
    e = jnp.where(mask, jnp.exp(s - m), 0.0)
    return e / jnp.maximum(jnp.sum(e, axis=-1, keepdims=True), 1e-30)


def nsa_key_blocks(kv_cmp, kv_slc, pe, w_cmp):
    B, T = kv_cmp.shape[:2]
    nc = T // CMP_BLOCK
    blk = kv_cmp[:, :nc * CMP_BLOCK].reshape(B, nc, CMP_BLOCK, 2, NSA_KV, HD) + pe[:, :, None, :]
    ckv = jnp.einsum('bnlcgd,lcde->bncge', blk, w_cmp)
    ck = rope(ckv[:, :, 0], (jnp.arange(nc, dtype=jnp.int32) + 1) * CMP_BLOCK - 1)
    cv = ckv[:, :, 1]
    ns = -(-T // SLC_BLOCK)
    skv = jnp.pad(kv_slc, ((0, 0), (0, ns * SLC_BLOCK - T), (0, 0), (0, 0), (0, 0)))
    skv = skv.reshape(B, ns, SLC_BLOCK, 2, NSA_KV, HD).transpose(3, 0, 4, 1, 2, 5)
    return ck, cv, skv[0], skv[1]


def nsa_query_block(q, qpos, gates, ck, cv, sk, sv, wk, wv, wpos):
    B, QB = q.shape[:2]
    qg = q.reshape(B, QB, NSA_KV, NSA_REP, HD)
    nc = ck.shape[1]
    c_end = (jnp.arange(nc, dtype=jnp.int32) + 1) * CMP_BLOCK - 1
    c_mask = c_end[None, :] <= qpos[:, None]
    s_c = jnp.einsum('bqgrd,bngd->bgrqn', qg, ck)
    p_c = masked_softmax(s_c, c_mask)
    o_c = jnp.einsum('bgrqn,bngd->bqgrd', p_c, cv)
    ns = sk.shape[2]
    ratio = SLC_BLOCK // CMP_BLOCK
    p_grp = jnp.sum(p_c, axis=2)
    p_grp = jnp.pad(p_grp, ((0, 0), (0, 0), (0, 0), (0, ns * ratio - nc)))
    p_slc = p_grp.reshape(B, NSA_KV, QB, ns, ratio).sum(-1)
    blk = jnp.arange(ns, dtype=jnp.int32)[None, :]
    cur = (qpos // SLC_BLOCK)[:, None]
    forced = (blk == 0) | (blk == cur) | (blk == cur - 1)
    score = jnp.where(forced, jnp.inf, p_slc)
    score = jnp.where(blk <= cur, score, -jnp.inf)
    n_top = min(SLC_TOPK, ns)
    top_v, top_i = lax.top_k(score, n_top)
    bi = jnp.arange(B)[:, None, None, None]
    gi = jnp.arange(NSA_KV)[None, :, None, None]
    k_sel = sk[bi, gi, top_i]
    v_sel = sv[bi, gi, top_i]
    k_pos = top_i[..., None] * SLC_BLOCK + jnp.arange(SLC_BLOCK, dtype=jnp.int32)
    s_mask = (top_v > -jnp.inf)[..., None] & (k_pos <= qpos[None, None, :, None, None])
    s_s = jnp.einsum('bqgrd,bgqnld->bgrqnl', qg, k_sel)
    s_s = s_s.reshape(B, NSA_KV, NSA_REP, QB, n_top * SLC_BLOCK)
    p_s = masked_softmax(s_s, s_mask.reshape(B, NSA_KV, 1, QB, n_top * SLC_BLOCK))
    p_s = p_s.reshape(B, NSA_KV, NSA_REP, QB, n_top, SLC_BLOCK)
    o_s = jnp.einsum('bgrqnl,bgqnld->bqgrd', p_s, v_sel)
    d = qpos[:, None] - wpos[None, :]
    w_mask = (d >= 0) & (d < WINDOW) & (wpos[None, :] >= 0)
    s_w = jnp.einsum('bqgrd,bkgd->bgrqk', qg, wk)
    p_w = masked_softmax(s_w, w_mask)
    o_w = jnp.einsum('bgrqk,bkgd->bqgrd', p_w, wv)
    g = gates.reshape(B, QB, NSA_KV, NSA_REP, 3)
    o = g[..., 0:1] * o_c + g[..., 1:2] * o_s + g[..., 2:3] * o_w
    return o.reshape(B, QB, NSA_Q_W)


def nsa_prompt(q, gates, ck, cv, sk, sv, kv_win):
    B, T = q.shape[:2]
    wkv = jnp.pad(kv_win, ((0, 0), (WINDOW, 0), (0, 0), (0, 0), (0, 0)))

    def one_block(i):
        s0 = i * Q_BLOCK
        qb = lax.dynamic_slice_in_dim(q, s0, Q_BLOCK, axis=1)
        gb = lax.dynamic_slice_in_dim(gates, s0, Q_BLOCK, axis=1)
        wb = lax.dynamic_slice_in_dim(wkv, s0, WINDOW + Q_BLOCK, axis=1)
        qpos = s0 + jnp.arange(Q_BLOCK, dtype=jnp.int32)
        wpos = s0 - WINDOW + jnp.arange(WINDOW + Q_BLOCK, dtype=jnp.int32)
        return nsa_query_block(qb, qpos, gb, ck, cv, sk, sv, wb[:, :, 0], wb[:, :, 1], wpos)

    out = lax.map(one_block, jnp.arange(T // Q_BLOCK, dtype=jnp.int32))
    return jnp.swapaxes(out, 0, 1).reshape(B, T, NSA_Q_W)


def gla_chunked(q, k, v, log_a, s0, chunk):
    B, T, H, DK = q.shape
    nck = T // chunk

    def to_chunks(x):
        return jnp.swapaxes(x.reshape(B, nck, chunk, *x.shape[2:]), 0, 1)

    causal = jnp.tril(jnp.ones((chunk, chunk), dtype=bool))

    def step(S, inp):
        qf, kf, vf, lac = inp
        cum = jnp.cumsum(lac, axis=1)
        o_inter = jnp.einsum('bchk,bhkv->bchv', qf * jnp.exp(cum), S)
        diff = cum[:, :, None] - cum[:, None, :]
        diff = jnp.where(causal[None, :, :, None, None], diff, -jnp.inf)
        att = jnp.einsum('bihk,bjhk,bijhk->bhij', qf, kf, jnp.exp(diff))
        o_intra = jnp.einsum('bhij,bjhv->bihv', att, vf)
        last = cum[:, -1]
        kdec = kf * jnp.exp(last[:, None] - cum)
        S = jnp.exp(last)[..., None] * S + jnp.einsum('bchk,bchv->bhkv', kdec, vf)
        return S, o_inter + o_intra

    S, o = lax.scan(step, s0, (to_chunks(q), to_chunks(k), to_chunks(v), to_chunks(log_a)))
    return jnp.swapaxes(o, 0, 1).reshape(B, T, H, v.shape[-1]), S


def mixer_a(u, pos, p, past):
    B, T, _ = u.shape
    offs = np.cumsum(np.array(A_SIZES))[:-1].tolist()
    q, kvc, kvs, kvw, gts, gq, gk, gv, glr, gg = jnp.split(u, offs, axis=-1)
    q = rope(q.reshape(B, T, NSA_HEADS, HD), pos) * (HD ** -0.5)

    def as_kv(t, rotate):
        t = t.reshape(B, T, 2, NSA_KV, HD)
        if rotate:
            t = jnp.stack([rope(t[:, :, 0], pos), t[:, :, 1]], axis=2)
        return t

    kv_cmp = as_kv(kvc, False)
    kv_slc = as_kv(kvs, True)
    kv_win = as_kv(kvw, True)
    gates = jax.nn.sigmoid(gts.reshape(B, T, NSA_HEADS, 3))
    pe, wc = p['a_cmp_pe'][0], p['a_cmp_w'][0]
    gq = gq.reshape(B, T, GLA_HEADS, GLA_DK) * (GLA_DK ** -0.5)
    gk = gk.reshape(B, T, GLA_HEADS, GLA_DK)
    gv = gv.reshape(B, T, GLA_HEADS, GLA_DV)
    z = jnp.dot(glr, p['a_gla_wa2'][0], precision=lax.Precision.HIGHEST) + p['a_gla_ba'][0]
    log_a = (jax.nn.log_sigmoid(z) / GLA_TAU).reshape(B, T, GLA_HEADS, GLA_DK)
    if past is None:
        ck, cv, sk, sv = nsa_key_blocks(kv_cmp, kv_slc, pe, wc)
        o_nsa = nsa_prompt(q, gates, ck, cv, sk, sv, kv_win)
        win_new = kv_win[:, T - min(WINDOW, T):]
        s0 = jnp.zeros((B, GLA_HEADS, GLA_DK, GLA_DV), F32)
        o_gla, sT = gla_chunked(gq, gk, gv, log_a, s0, GLA_CHUNK)
    else:
        pt = past['page_table']
        old_cmp = past['cache_cmp_kv'][0, pt].reshape(B, -1, 2, NSA_KV, HD)
        old_slc = past['cache_slc_kv'][0, pt].reshape(B, -1, 2, NSA_KV, HD)
        ck, cv, sk, sv = nsa_key_blocks(jnp.concatenate([old_cmp, kv_cmp], axis=1),
                                        jnp.concatenate([old_slc, kv_slc], axis=1), pe, wc)
        buf = past['cache_win_kv'][0]
        wkv = jnp.concatenate([buf, kv_win], axis=1)
        n_w = wkv.shape[1]
        wpos = pos[0] + T - n_w + jnp.arange(n_w, dtype=jnp.int32)
        o_nsa = nsa_query_block(q, pos, gates, ck, cv, sk, sv, wkv[:, :, 0], wkv[:, :, 1], wpos)
        win_new = wkv[:, n_w - buf.shape[1]:]
        o_gla, sT = gla_chunked(gq, gk, gv, log_a, past['state_gla'][0], T)
    o_gla = rmsnorm(o_gla, p['a_gla_norm'][0]) * jax.nn.silu(gg.reshape(B, T, GLA_HEADS, GLA_DV))
    mix_in = jnp.concatenate([o_nsa, o_gla.reshape(B, T, GLA_V_W)], axis=-1)
    return mix_in, (kv_cmp, kv_slc, win_new, sT)


def block_diag(x, w, b):
    lead = x.shape[:-1]
    xb = x.reshape(*lead, LRU_BLOCKS, LRU_BW)
    return (jnp.einsum('btnw,nwv->btnv', xb, w) + b.reshape(LRU_BLOCKS, LRU_BW)).reshape(*lead, D_RNN)


def rglru(x, h0, w_a, b_a, w_x, b_x, lam):
    r = jax.nn.sigmoid(block_diag(x, w_a, b_a))
    i = jax.nn.sigmoid(block_diag(x, w_x, b_x))
    log_a = -LRU_C * r * jax.nn.softplus(-lam)
    a = jnp.exp(log_a)
    u = jnp.sqrt(-jnp.expm1(2.0 * log_a)) * (i * x)

    def step(h, inp):
        a_t, u_t = inp
        h = a_t * h + u_t
        return h, h

    hT, hs = lax.scan(step, h0, (jnp.swapaxes(a, 0, 1), jnp.swapaxes(u, 0, 1)))
    return jnp.swapaxes(hs, 0, 1), hT


def mixer_c(u, p, past):
    B, T, _ = u.shape
    gate_b, x_b = jnp.split(u, 2, axis=-1)
    if past is None:
        buf = jnp.zeros((B, CONV_W - 1, D_RNN), F32)
        h0 = jnp.zeros((B, D_RNN), F32)
    else:
        buf = past['state_conv'][0]
        h0 = past['state_lru'][0]
    xp = jnp.concatenate([buf, x_b], axis=1)
    w = p['c_conv_w'][0][:, None, :]
    xc = lax.conv_general_dilated(xp, w, (1,), 'VALID', dimension_numbers=('NWC', 'WIO', 'NWC'),
                                  feature_group_count=D_RNN) + p['c_conv_b'][0]
    new_buf = xp[:, xp.shape[1] - (CONV_W - 1):]
    y, hT = rglru(xc, h0, p['c_w_a'][0], p['c_b_a'][0], p['c_w_x'][0], p['c_b_x'][0], p['c_lam'][0])
    return jax.nn.gelu(gate_b) * y, (hT, new_buf)


def hier_moe(xt, w_rg, b_rg, w_re, b_re, w1, w3, w2):
    n, D = xt.shape
    rows = jnp.arange(n)
    hp = lax.Precision.HIGHEST
    g_logit = jnp.dot(xt, w_rg, precision=hp) + b_rg
    g_prob = jax.nn.softmax(g_logit, axis=-1)
    g_top = jnp.argmax(g_logit, axis=-1).astype(jnp.int32)
    g_w = g_prob[rows, g_top][:, None]
    e_logit = (jnp.dot(xt, w_re, precision=hp) + b_re).reshape(n, N_GROUPS, EXP_PER_GROUP)
    e_logit = e_logit[rows, g_top]
    e_val, e_idx = lax.top_k(e_logit, TOP_K_IN_GROUP)
    gate = jax.nn.softmax(e_val, axis=-1) * g_w
    expert = (g_top[:, None] * EXP_PER_GROUP + e_idx).astype(jnp.int32)
    nk = n * TOP_K_IN_GROUP
    flat_e = expert.reshape(nk)
    flat_t = jnp.repeat(jnp.arange(n, dtype=jnp.int32), TOP_K_IN_GROUP)
    flat_w = gate.reshape(nk)
    order = jnp.argsort(flat_e)
    se, st, sw = flat_e[order], flat_t[order], flat_w[order]
    counts = jnp.bincount(flat_e, length=N_EXPERTS)
    padded = ((counts + MOE_BLOCK - 1) // MOE_BLOCK) * MOE_BLOCK
    pad_end = jnp.cumsum(padded)
    pad_start = pad_end - padded
    cnt_start = jnp.cumsum(counts) - counts
    dest = pad_start[se] + (jnp.arange(nk, dtype=jnp.int32) - cnt_start[se])
    n_slots = (-(-nk // MOE_BLOCK) + N_EXPERTS) * MOE_BLOCK
    nb = n_slots // MOE_BLOCK
    slot_tok = jnp.full((n_slots,), n, jnp.int32).at[dest].set(st)
    slot_w = jnp.zeros((n_slots,), F32).at[dest].set(sw)
    blk_start = jnp.arange(nb, dtype=jnp.int32) * MOE_BLOCK
    blk_exp = jnp.minimum(jnp.searchsorted(pad_end, blk_start, side='right'), N_EXPERTS - 1)
    x_pad = jnp.concatenate([xt, jnp.zeros((1, D), xt.dtype)], axis=0)

    def expert_block(args):
        toks, e = args
        xb = x_pad[toks]
        hdn = jax.nn.silu(xb @ w1[e]) * (xb @ w3[e])
        return hdn @ w2[e]

    yb = lax.map(expert_block, (slot_tok.reshape(nb, MOE_BLOCK), blk_exp))
    yb = yb.reshape(n_slots, D) * slot_w[:, None]
    return jnp.zeros((n + 1, D), F32).at[slot_tok].add(yb)[:n]


N_PROMPT = BATCH * SEQ
N_SAMPLE = DEC_BATCH * DEC_SEQ
N_TOK = N_PROMPT + N_SAMPLE


def run_trunk(x_prompt, x_sample, p, past):
    xs = jnp.swapaxes(x_sample, 0, 1).reshape(N_SAMPLE, D_MODEL)
    h = jnp.concatenate([x_prompt.reshape(N_PROMPT, D_MODEL), xs], axis=0)
    u = norm_matmul(h, p['norm_mix'][0], p['a_w_in'][0])
    up = u[:N_PROMPT].reshape(BATCH, SEQ, A_IN)
    us = jnp.swapaxes(u[N_PROMPT:].reshape(DEC_SEQ, DEC_BATCH, A_IN), 0, 1)
    mix_p, new_p = mixer_a(up, jnp.arange(SEQ, dtype=jnp.int32), p, None)
    mix_s, new_s = mixer_a(us, PAST_LEN + jnp.arange(DEC_SEQ, dtype=jnp.int32), p, past)
    mix_in = jnp.concatenate([mix_p.reshape(N_PROMPT, A_MIX),
                              jnp.swapaxes(mix_s, 0, 1).reshape(N_SAMPLE, A_MIX)], axis=0).astype(BF16)
    h, hn, route, counts = post_mixer(h, mix_in, p['a_w_out'][0], p['norm_ffn'][0], p['m_w_rg'][0],
                                      p['m_b_rg'][0], p['m_w_re'][0], p['m_b_re'][0])
    yg = moe_experts(hn, route, counts, p['m_w1'][0], p['m_w3'][0], p['m_w2'][0])
    h, u = combine_proj(h, yg, route, p['norm_mix'][1], p['c_w_in'][0])
    lru_w = (p['c_conv_w'][0], p['c_conv_b'][0], p['c_w_a'][0], p['c_b_a'][0], p['c_w_x'][0], p['c_b_x'][0],
             p['c_lam'][0])
    up = u[:N_PROMPT].reshape(BATCH, SEQ, 2 * D_RNN)
    us = u[N_PROMPT:].reshape(DEC_SEQ, DEC_BATCH, 2 * D_RNN)
    y_p, lru_p = lru_seq(up, *lru_w)
    y_s, lru_s = lru_step(us, past['state_conv'][0], past['state_lru'][0], *lru_w)
    conv_p = up[:, SEQ - (CONV_W - 1):, D_RNN:]
    conv_s = jnp.swapaxes(us[DEC_SEQ - (CONV_W - 1):, :, D_RNN:], 0, 1)
    mix_in = jnp.concatenate([y_p.reshape(N_PROMPT, D_RNN), y_s.reshape(N_SAMPLE, D_RNN)], axis=0)
    h, hn, route, counts = post_mixer(h, mix_in, p['c_w_out'][0], p['norm_ffn'][1], p['m_w_rg'][1],
                                      p['m_b_rg'][1], p['m_w_re'][1], p['m_b_re'][1])
    yg = moe_experts(hn, route, counts, p['m_w1'][1], p['m_w3'][1], p['m_w2'][1])
    y = combine_norm(h, yg, route, p['norm_final'])
    y_prompt = y[:N_PROMPT].reshape(BATCH, SEQ, D_MODEL)
    y_sample = jnp.swapaxes(y[N_PROMPT:].reshape(DEC_SEQ, DEC_BATCH, D_MODEL), 0, 1)
    return (y_prompt, y_sample), new_p + (lru_p, conv_p), new_s + (lru_s, conv_s)


def kernel(x_prompt, x_sample, cache_cmp_kv, cache_slc_kv, cache_win_kv, state_gla, state_lru, state_conv,
           page_table, norm_mix, norm_ffn, norm_final, a_w_in, a_cmp_pe, a_cmp_w, a_gla_wa2, a_gla_ba,
           a_gla_norm, a_w_out, c_w_in, c_conv_w, c_conv_b, c_w_a, c_b_a, c_w_x, c_b_x, c_lam, c_w_out,
           m_w_rg, m_b_rg, m_w_re, m_b_re, m_w1, m_w3, m_w2):
    p = {'norm_mix': norm_mix, 'norm_ffn': norm_ffn, 'norm_final': norm_final,
         'a_w_in': a_w_in, 'a_cmp_pe': a_cmp_pe, 'a_cmp_w': a_cmp_w, 'a_gla_wa2': a_gla_wa2,
         'a_gla_ba': a_gla_ba, 'a_gla_norm': a_gla_norm, 'a_w_out': a_w_out,
         'c_w_in': c_w_in, 'c_conv_w': c_conv_w, 'c_conv_b': c_conv_b, 'c_w_a': c_w_a, 'c_b_a': c_b_a,
         'c_w_x': c_w_x, 'c_b_x': c_b_x, 'c_lam': c_lam, 'c_w_out': c_w_out,
         'm_w_rg': m_w_rg, 'm_b_rg': m_b_rg, 'm_w_re': m_w_re, 'm_b_re': m_b_re,
         'm_w1': m_w1, 'm_w3': m_w3, 'm_w2': m_w2}
    past = {'cache_cmp_kv': cache_cmp_kv, 'cache_slc_kv': cache_slc_kv, 'cache_win_kv': cache_win_kv,
            'state_gla': state_gla, 'state_lru': state_lru, 'state_conv': state_conv,
            'page_table': page_table}
    (y_p, y_s), sp, ss = run_trunk(x_prompt, x_sample, p, past)
    outs = [y_p, y_s]
    for a, b in zip(sp, ss):
        outs += [a[None], b[None]]
    return tuple(outs)
```

```python
import functools
import jax, jax.numpy as jnp
from jax import lax
import numpy as np
from jax.experimental import pallas as pl
from jax.experimental.pallas import tpu as pltpu

D_MODEL = 1024
BATCH = 2
SEQ = 8192
DEPTH = 2
DEC_BATCH = 128
DEC_SEQ = 4
PAST_LEN = 2048
PAGE_SIZE = 128
EPS = 1e-6
NSA_HEADS = 8
NSA_KV = 2
NSA_REP = NSA_HEADS // NSA_KV
HD = 64
CMP_BLOCK = 32
SLC_BLOCK = 64
SLC_TOPK = 16
WINDOW = 512
Q_BLOCK = 128
ROPE_DIM = HD // 4
ROPE_THETA = 500000.0
GLA_HEADS = 4
GLA_DK = 64
GLA_DV = 128
GLA_LOWRANK = 16
GLA_TAU = 16.0
GLA_CHUNK = 64
D_RNN = 1280
LRU_BLOCKS = 10
LRU_BW = D_RNN // LRU_BLOCKS
CONV_W = 4
LRU_C = 8.0
N_GROUPS = 4
EXP_PER_GROUP = 8
N_EXPERTS = N_GROUPS * EXP_PER_GROUP
E_HID = 512
TOP_K_IN_GROUP = 2
MOE_BLOCK = 128
NSA_Q_W = NSA_HEADS * HD
NSA_KV_W = 2 * NSA_KV * HD
GLA_K_W = GLA_HEADS * GLA_DK
GLA_V_W = GLA_HEADS * GLA_DV
A_SIZES = (NSA_Q_W, NSA_KV_W, NSA_KV_W, NSA_KV_W, 3 * NSA_HEADS, GLA_K_W, GLA_K_W, GLA_V_W, GLA_LOWRANK, GLA_V_W)
A_IN = sum(A_SIZES)
A_MIX = NSA_Q_W + GLA_V_W

F32 = jnp.float32
BF16 = jnp.bfloat16
VMEM_LIMIT_BYTES = 56 * 1024 * 1024
ROW_TILE = 512


def _norm_matmul_body(x_ref, g_ref, w_ref, o_ref):
    x = x_ref[...]
    y = x * lax.rsqrt(jnp.mean(x * x, axis=-1, keepdims=True) + EPS) * g_ref[...]
    o_ref[...] = jnp.dot(y.astype(BF16), w_ref[...], preferred_element_type=F32)


def _matmul_body(x_ref, w_ref, o_ref):
    o_ref[...] = jnp.dot(x_ref[...].astype(BF16), w_ref[...], preferred_element_type=F32)


def norm_matmul(x, g, w):
    n, k = x.shape
    nn = w.shape[1]
    return pl.pallas_call(
        _norm_matmul_body,
        grid=(n // ROW_TILE,),
        in_specs=[pl.BlockSpec((ROW_TILE, k), lambda i: (i, 0)),
                  pl.BlockSpec((1, k), lambda i: (0, 0)),
                  pl.BlockSpec((k, nn), lambda i: (0, 0))],
        out_specs=pl.BlockSpec((ROW_TILE, nn), lambda i: (i, 0)),
        out_shape=jax.ShapeDtypeStruct((n, nn), F32),
        compiler_params=pltpu.CompilerParams(dimension_semantics=("arbitrary",),
                                             vmem_limit_bytes=VMEM_LIMIT_BYTES),
        name="norm_matmul",
    )(x, g.reshape(1, k), w.astype(BF16))


def matmul(x, w):
    n, k = x.shape
    nn = w.shape[1]
    return pl.pallas_call(
        _matmul_body,
        grid=(n // ROW_TILE,),
        in_specs=[pl.BlockSpec((ROW_TILE, k), lambda i: (i, 0)),
                  pl.BlockSpec((k, nn), lambda i: (0, 0))],
        out_specs=pl.BlockSpec((ROW_TILE, nn), lambda i: (i, 0)),
        out_shape=jax.ShapeDtypeStruct((n, nn), F32),
        compiler_params=pltpu.CompilerParams(dimension_semantics=("arbitrary",),
                                             vmem_limit_bytes=VMEM_LIMIT_BYTES),
        name="matmul",
    )(x, w.astype(BF16))


LRU_TIME_TILE = 256
SUBLANES = 8


def _softplus(x):
    return jnp.maximum(x, 0.0) + jnp.log1p(jnp.exp(-jnp.abs(x)))


def _gelu_tanh(x):
    return x * (0.5 * (1.0 + jnp.tanh(0.7978845608028654 * (x + 0.044715 * (x * x * x)))))


def _lru_gates(xc, wa_ref, ba, wx_ref, bx, lam):
    xcb = xc.astype(BF16)
    r_parts, i_parts = [], []
    for n in range(LRU_BLOCKS):
        xs = xcb[:, n * LRU_BW:(n + 1) * LRU_BW]
        r_parts.append(jnp.dot(xs, wa_ref[n], preferred_element_type=F32))
        i_parts.append(jnp.dot(xs, wx_ref[n], preferred_element_type=F32))
    r = jax.nn.sigmoid(jnp.concatenate(r_parts, axis=-1) + ba)
    i = jax.nn.sigmoid(jnp.concatenate(i_parts, axis=-1) + bx)
    log_a = -LRU_C * r * _softplus(-lam)
    a = jnp.exp(log_a)
    u = jnp.sqrt(-jnp.tanh(log_a) * (a * a + 1.0)) * (i * xc)
    return a, u


def _lru_seq_body(u_ref, cw_ref, cb_ref, wa_ref, ba_ref, wx_ref, bx_ref, lam_ref, y_ref, hT_ref, xp_sc, h_sc):
    tt = LRU_TIME_TILE

    @pl.when(pl.program_id(1) == 0)
    def _():
        xp_sc[0:SUBLANES, :] = jnp.zeros((SUBLANES, D_RNN), F32)
        h_sc[...] = jnp.zeros((1, D_RNN), F32)

    xp_sc[SUBLANES:SUBLANES + tt, :] = u_ref[0, :, D_RNN:]
    xc = cb_ref[...]
    for w in range(CONV_W):
        off = SUBLANES - (CONV_W - 1) + w
        xc = xc + cw_ref[w:w + 1, :] * xp_sc[off:off + tt, :]
    a, u = _lru_gates(xc, wa_ref, ba_ref[...], wx_ref, bx_ref[...], lam_ref[...])
    row = lax.broadcasted_iota(jnp.int32, (tt, D_RNN), 0)
    d = 1
    while d < tt:
        keep = row >= d
        a_prev = jnp.where(keep, pltpu.roll(a, d, 0), 1.0)
        u_prev = jnp.where(keep, pltpu.roll(u, d, 0), 0.0)
        u = a * u_prev + u
        a = a * a_prev
        d *= 2
    h = a * h_sc[...] + u
    h_sc[...] = h[tt - 1:tt, :]
    hT_ref[0] = h[tt - 1:tt, :]
    y_ref[0] = (_gelu_tanh(u_ref[0, :, :D_RNN]) * h).astype(BF16)
    xp_sc[0:SUBLANES, :] = xp_sc[tt:tt + SUBLANES, :]


def _lru_weight_args(cw, cb, wa, ba, wx, bx, lam):
    row = lambda v: v.reshape(1, D_RNN)
    return (cw, row(cb), wa.astype(BF16), row(ba), wx.astype(BF16), row(bx), row(lam))


def lru_seq(u, cw, cb, wa, ba, wx, bx, lam):
    B, T, _ = u.shape
    tt = LRU_TIME_TILE
    full = lambda shape: pl.BlockSpec(shape, lambda b, t: (0,) * len(shape))
    y, hT = pl.pallas_call(
        _lru_seq_body,
        grid=(B, T // tt),
        in_specs=[pl.BlockSpec((1, tt, 2 * D_RNN), lambda b, t: (b, t, 0)),
                  full((CONV_W, D_RNN)), full((1, D_RNN)),
                  full((LRU_BLOCKS, LRU_BW, LRU_BW)), full((1, D_RNN)),
                  full((LRU_BLOCKS, LRU_BW, LRU_BW)), full((1, D_RNN)), full((1, D_RNN))],
        out_specs=[pl.BlockSpec((1, tt, D_RNN), lambda b, t: (b, t, 0)),
                   pl.BlockSpec((1, 1, D_RNN), lambda b, t: (b, 0, 0))],
        out_shape=[jax.ShapeDtypeStruct((B, T, D_RNN), BF16),
                   jax.ShapeDtypeStruct((B, 1, D_RNN), F32)],
        scratch_shapes=[pltpu.VMEM((tt + SUBLANES, D_RNN), F32), pltpu.VMEM((1, D_RNN), F32)],
        compiler_params=pltpu.CompilerParams(dimension_semantics=("arbitrary", "arbitrary"),
                                             vmem_limit_bytes=VMEM_LIMIT_BYTES),
        name="lru_seq",
    )(u, *_lru_weight_args(cw, cb, wa, ba, wx, bx, lam))
    return y, hT.reshape(B, D_RNN)


def _lru_step_body(u_ref, cs_ref, h0_ref, cw_ref, cb_ref, wa_ref, ba_ref, wx_ref, bx_ref, lam_ref, y_ref, hT_ref):
    n_t = u_ref.shape[0]
    hist = [cs_ref[:, w, :] for w in range(CONV_W - 1)] + [u_ref[t, :, D_RNN:] for t in range(n_t)]
    h = h0_ref[...]
    for t in range(n_t):
        xc = cb_ref[...]
        for w in range(CONV_W):
            xc = xc + cw_ref[w:w + 1, :] * hist[t + w]
        a, u = _lru_gates(xc, wa_ref, ba_ref[...], wx_ref, bx_ref[...], lam_ref[...])
        h = a * h + u
        y_ref[t] = (_gelu_tanh(u_ref[t, :, :D_RNN]) * h).astype(BF16)
    hT_ref[...] = h


def lru_step(u, conv_state, h0, cw, cb, wa, ba, wx, bx, lam):
    T, B, _ = u.shape
    return pl.pallas_call(
        _lru_step_body,
        out_shape=[jax.ShapeDtypeStruct((T, B, D_RNN), BF16), jax.ShapeDtypeStruct((B, D_RNN), F32)],
        compiler_params=pltpu.CompilerParams(vmem_limit_bytes=VMEM_LIMIT_BYTES),
        name="lru_step",
    )(u, conv_state, h0, *_lru_weight_args(cw, cb, wa, ba, wx, bx, lam))


LANES = 128
ROUTE_E1, ROUTE_E2, ROUTE_G1, ROUTE_G2, ROUTE_R1, ROUTE_R2 = range(6)
EXPERT_LANE0 = N_GROUPS


def _lane_pick(val_by_lane):
    rows = next(iter(val_by_lane.values())).shape[0]
    lane = lax.broadcasted_iota(jnp.int32, (rows, LANES), 1)
    out = jnp.zeros((rows, LANES), F32)
    for l, v in val_by_lane.items():
        out = jnp.where(lane == l, v, out)
    return out


def _route_rows(logits, tri_ref, carry_ref):
    rows = logits.shape[0]
    lane = lax.broadcasted_iota(jnp.int32, (rows, LANES), 1)
    neg = -jnp.inf
    gl = jnp.where(lane < N_GROUPS, logits, neg)
    gmax = jnp.max(gl, axis=-1, keepdims=True)
    gtop = jnp.min(jnp.where(gl == gmax, lane, LANES), axis=-1, keepdims=True)
    gsum = jnp.sum(jnp.where(lane < N_GROUPS, jnp.exp(logits - gmax), 0.0), axis=-1, keepdims=True)
    g_w = 1.0 / gsum
    lo = EXPERT_LANE0 + EXP_PER_GROUP * gtop
    el = jnp.where((lane >= lo) & (lane < lo + EXP_PER_GROUP), logits, neg)
    v1 = jnp.max(el, axis=-1, keepdims=True)
    i1 = jnp.min(jnp.where(el == v1, lane, LANES), axis=-1, keepdims=True)
    el2 = jnp.where(lane == i1, neg, el)
    v2 = jnp.max(el2, axis=-1, keepdims=True)
    i2 = jnp.min(jnp.where(el2 == v2, lane, LANES), axis=-1, keepdims=True)
    p2 = jnp.exp(v2 - v1)
    den = 1.0 + p2
    gate1 = (1.0 / den) * g_w
    gate2 = (p2 / den) * g_w
    hit1 = lane == i1
    hit2 = lane == i2
    onehot = jnp.where(hit1 | hit2, 1.0, 0.0)
    before = jnp.dot(tri_ref[...], onehot.astype(BF16), preferred_element_type=F32) + carry_ref[...]
    rank1 = jnp.sum(jnp.where(hit1, before, 0.0), axis=-1, keepdims=True)
    rank2 = jnp.sum(jnp.where(hit2, before, 0.0), axis=-1, keepdims=True)
    carry_ref[...] = carry_ref[...] + jnp.sum(onehot, axis=0, keepdims=True)
    return _lane_pick({ROUTE_E1: (i1 - EXPERT_LANE0).astype(F32), ROUTE_E2: (i2 - EXPERT_LANE0).astype(F32),
                       ROUTE_G1: gate1, ROUTE_G2: gate2, ROUTE_R1: rank1, ROUTE_R2: rank2})


def _post_mixer_body(n_mix, h_ref, *refs):
    m_refs, wo_refs = refs[:n_mix], refs[n_mix:2 * n_mix]
    g_ref, wr_ref, br_ref, tri_ref, h1_ref, hn_ref, route_ref, cnt_ref, carry_sc = refs[2 * n_mix:]

    @pl.when(pl.program_id(0) == 0)
    def _():
        carry_sc[...] = jnp.zeros((1, LANES), F32)

    mix = jnp.dot(m_refs[0][...], wo_refs[0][...], preferred_element_type=F32)
    for m_ref, wo_ref in zip(m_refs[1:], wo_refs[1:]):
        mix = mix + jnp.dot(m_ref[...], wo_ref[...], preferred_element_type=F32)
    h1 = h_ref[...] + mix
    h1_ref[...] = h1
    hn = (h1 * lax.rsqrt(jnp.mean(h1 * h1, axis=-1, keepdims=True) + EPS) * g_ref[...]).astype(BF16)
    hn_ref[...] = hn
    logits = jnp.dot(hn, wr_ref[...], preferred_element_type=F32) + br_ref[...]
    route_ref[...] = _route_rows(logits, tri_ref, carry_sc)
    cnt_ref[...] = carry_sc[...]


def post_mixer(h, mix_ins, w_out, g_ffn, w_rg, b_rg, w_re, b_re):
    n = h.shape[0]
    ks = [m.shape[1] for m in mix_ins]
    offs = np.cumsum([0] + ks)
    w_parts = [w_out[offs[j]:offs[j + 1]].astype(BF16) for j in range(len(ks))]
    pad = LANES - N_GROUPS - N_EXPERTS
    wr = jnp.concatenate([w_rg, w_re, jnp.zeros((D_MODEL, pad), F32)], axis=1).astype(BF16)
    br = jnp.concatenate([b_rg, b_re, jnp.zeros((pad,), F32)]).reshape(1, LANES)
    tri = jnp.tril(jnp.ones((ROW_TILE, ROW_TILE), BF16), -1)
    full = lambda shape: pl.BlockSpec(shape, lambda i: (0,) * len(shape))
    rows = lambda w: pl.BlockSpec((ROW_TILE, w), lambda i: (i, 0))
    return pl.pallas_call(
        functools.partial(_post_mixer_body, len(ks)),
        grid=(n // ROW_TILE,),
        in_specs=[rows(D_MODEL)] + [rows(k) for k in ks] + [full((k, D_MODEL)) for k in ks]
                 + [full((1, D_MODEL)), full((D_MODEL, LANES)), full((1, LANES)), full((ROW_TILE, ROW_TILE))],
        out_specs=[rows(D_MODEL), rows(D_MODEL), rows(LANES), full((1, LANES))],
        out_shape=[jax.ShapeDtypeStruct((n, D_MODEL), F32), jax.ShapeDtypeStruct((n, D_MODEL), BF16),
                   jax.ShapeDtypeStruct((n, LANES), F32), jax.ShapeDtypeStruct((1, LANES), F32)],
        scratch_shapes=[pltpu.VMEM((1, LANES), F32)],
        compiler_params=pltpu.CompilerParams(dimension_semantics=("arbitrary",),
                                             vmem_limit_bytes=VMEM_LIMIT_BYTES),
        name="post_mixer",
    )(h, *mix_ins, *w_parts, g_ffn.reshape(1, D_MODEL), wr, br, tri)


MOE_ROWS = 256


def _ffn_body(be_ref, nb_ref, x_ref, w1_ref, w3_ref, w2_ref, y_ref, w1_sc, w3_sc, w2_sc):
    i = pl.program_id(0)
    new_expert = jnp.logical_or(i == 0, be_ref[i] != be_ref[jnp.maximum(i - 1, 0)])

    @pl.when(jnp.logical_and(new_expert, i < nb_ref[0]))
    def _():
        w1_sc[...] = w1_ref[0].astype(BF16)
        w3_sc[...] = w3_ref[0].astype(BF16)
        w2_sc[...] = w2_ref[0].astype(BF16)

    @pl.when(i < nb_ref[0])
    def _():
        x = x_ref[...]
        a = jnp.dot(x, w1_sc[...], preferred_element_type=F32)
        b = jnp.dot(x, w3_sc[...], preferred_element_type=F32)
        hdn = (a * jax.nn.sigmoid(a) * b).astype(BF16)
        y_ref[...] = jnp.dot(hdn, w2_sc[...], preferred_element_type=F32)

    @pl.when(i >= nb_ref[0])
    def _():
        y_ref[...] = jnp.zeros(y_ref.shape, F32)


def expert_ffn(xs, blk_exp, n_active, w1, w3, w2):
    n_slots = xs.shape[0]
    nb = n_slots // MOE_ROWS
    wmap = lambda i, be, na: (be[i], 0, 0)
    xmap = lambda i, be, na: (jnp.minimum(i, na[0] - 1), 0)
    return pl.pallas_call(
        _ffn_body,
        grid_spec=pltpu.PrefetchScalarGridSpec(
            num_scalar_prefetch=2,
            grid=(nb,),
            in_specs=[pl.BlockSpec((MOE_ROWS, D_MODEL), xmap),
                      pl.BlockSpec((1, D_MODEL, E_HID), wmap),
                      pl.BlockSpec((1, D_MODEL, E_HID), wmap),
                      pl.BlockSpec((1, E_HID, D_MODEL), wmap)],
            out_specs=pl.BlockSpec((MOE_ROWS, D_MODEL), lambda i, be, na: (i, 0)),
            scratch_shapes=[pltpu.VMEM((D_MODEL, E_HID), BF16), pltpu.VMEM((D_MODEL, E_HID), BF16),
                            pltpu.VMEM((E_HID, D_MODEL), BF16)],
        ),
        out_shape=jax.ShapeDtypeStruct((n_slots, D_MODEL), F32),
        compiler_params=pltpu.CompilerParams(dimension_semantics=("arbitrary",),
                                             vmem_limit_bytes=VMEM_LIMIT_BYTES),
        name="expert_ffn",
    )(blk_exp, n_active, xs, w1, w3, w2)


def moe_dispatch(route, counts_row, n):
    e = route[:, ROUTE_E1:ROUTE_E2 + 1].astype(jnp.int32)
    rank = route[:, ROUTE_R1:ROUTE_R2 + 1].astype(jnp.int32)
    counts = counts_row[0, EXPERT_LANE0:EXPERT_LANE0 + N_EXPERTS].astype(jnp.int32)
    padded = ((counts + MOE_ROWS - 1) // MOE_ROWS) * MOE_ROWS
    pad_end = jnp.cumsum(padded)
    pad_start = pad_end - padded
    dest = pad_start[e] + rank
    nb = -(-(n * TOP_K_IN_GROUP) // MOE_ROWS) + N_EXPERTS
    n_slots = nb * MOE_ROWS
    tok = jnp.broadcast_to(jnp.arange(n, dtype=jnp.int32)[:, None], (n, TOP_K_IN_GROUP))
    slot_tok = jnp.zeros((n_slots,), jnp.int32).at[dest.reshape(-1)].set(tok.reshape(-1))
    blk_start = jnp.arange(nb, dtype=jnp.int32) * MOE_ROWS
    blk_exp = jnp.minimum(jnp.searchsorted(pad_end, blk_start, side='right'), N_EXPERTS - 1).astype(jnp.int32)
    n_active = (pad_end[-1] // MOE_ROWS).astype(jnp.int32).reshape(1)
    return slot_tok, dest, blk_exp, n_active


def _combine(h_ref, yg_ref, route_ref):
    lane = lax.broadcasted_iota(jnp.int32, (ROW_TILE, LANES), 1)
    r = route_ref[...]
    g1 = jnp.sum(jnp.where(lane == ROUTE_G1, r, 0.0), axis=-1, keepdims=True)
    g2 = jnp.sum(jnp.where(lane == ROUTE_G2, r, 0.0), axis=-1, keepdims=True)
    return h_ref[...] + (yg_ref[:, :D_MODEL] * g1 + yg_ref[:, D_MODEL:] * g2)


def _combine_proj_body(h_ref, yg_ref, route_ref, g_ref, w_ref, h2_ref, u_ref):
    h2 = _combine(h_ref, yg_ref, route_ref)
    h2_ref[...] = h2
    hn = h2 * lax.rsqrt(jnp.mean(h2 * h2, axis=-1, keepdims=True) + EPS) * g_ref[...]
    u_ref[...] = jnp.dot(hn.astype(BF16), w_ref[...], preferred_element_type=F32)


def _combine_norm_body(h_ref, yg_ref, route_ref, g_ref, y_ref):
    h2 = _combine(h_ref, yg_ref, route_ref)
    y_ref[...] = h2 * lax.rsqrt(jnp.mean(h2 * h2, axis=-1, keepdims=True) + EPS) * g_ref[...]


def combine_proj(h, yg, route, g, w):
    n = h.shape[0]
    nn = w.shape[1]
    full = lambda shape: pl.BlockSpec(shape, lambda i: (0,) * len(shape))
    rows = lambda wd: pl.BlockSpec((ROW_TILE, wd), lambda i: (i, 0))
    return pl.pallas_call(
        _combine_proj_body,
        grid=(n // ROW_TILE,),
        in_specs=[rows(D_MODEL), rows(2 * D_MODEL), rows(LANES), full((1, D_MODEL)), full((D_MODEL, nn))],
        out_specs=[rows(D_MODEL), rows(nn)],
        out_shape=[jax.ShapeDtypeStruct((n, D_MODEL), F32), jax.ShapeDtypeStruct((n, nn), F32)],
        compiler_params=pltpu.CompilerParams(dimension_semantics=("arbitrary",),
                                             vmem_limit_bytes=VMEM_LIMIT_BYTES),
        name="combine_proj",
    )(h, yg, route, g.reshape(1, D_MODEL), w.astype(BF16))


def combine_norm(h, yg, route, g):
    n = h.shape[0]
    full = lambda shape: pl.BlockSpec(shape, lambda i: (0,) * len(shape))
    rows = lambda wd: pl.BlockSpec((ROW_TILE, wd), lambda i: (i, 0))
    return pl.pallas_call(
        _combine_norm_body,
        grid=(n // ROW_TILE,),
        in_specs=[rows(D_MODEL), rows(2 * D_MODEL), rows(LANES), full((1, D_MODEL))],
        out_specs=rows(D_MODEL),
        out_shape=jax.ShapeDtypeStruct((n, D_MODEL), F32),
        compiler_params=pltpu.CompilerParams(dimension_semantics=("arbitrary",),
                                             vmem_limit_bytes=VMEM_LIMIT_BYTES),
        name="combine_norm",
    )(h, yg, route, g.reshape(1, D_MODEL))


def moe_experts(hn, route, counts_row, w1, w3, w2):
    n = hn.shape[0]
    slot_tok, dest, blk_exp, n_active = moe_dispatch(route, counts_row, n)
    xs = jnp.take(hn, slot_tok, axis=0)
    ys = expert_ffn(xs, blk_exp, n_active, w1, w3, w2)
    return jnp.take(ys, dest.reshape(-1), axis=0).reshape(n, 2 * D_MODEL)


A_Q0, A_KVC0, A_KVS0, A_KVW0 = 0, 512, 768, 1024
A_GQ0, A_GK0, A_GV0, A_GG0, A_MISC0 = 1280, 1536, 1792, 2304, 2816
A_COLS = A_MISC0 + LANES
MISC_GATES0 = 0
MISC_LR0 = 3 * NSA_HEADS
KV_HALF = NSA_KV * HD


def _rope_lanes(x, cos_t, sin_lo, sin_hi):
    reps = x.shape[1] // LANES
    tile = (lambda t: jnp.concatenate([t] * reps, axis=1)) if reps > 1 else (lambda t: t)
    w = x.shape[1]
    half = ROPE_DIM // 2
    return x * tile(cos_t) + pltpu.roll(x, half, 1) * tile(sin_hi) + pltpu.roll(x, w - half, 1) * tile(sin_lo)


def _dup_groups(x):
    lane = lax.broadcasted_iota(jnp.int32, x.shape, 1)
    sw = pltpu.roll(x, HD, 1)
    lo = lane < HD
    return jnp.concatenate([jnp.where(lo, x, sw), jnp.where(lo, sw, x)], axis=1)


def _inproj_a_body(h_ref, g_ref, w_ref, wa2_ref, ba_ref, cos_ref, slo_ref, shi_ref,
                   q_ref, kvc_ref, kvs_ref, kvw_ref, ksd_ref, vsd_ref, kwd_ref, vwd_ref,
                   gates_ref, gq_ref, gk_ref, gv_ref, la_ref, sgg_ref):
    x = h_ref[...]
    y = (x * lax.rsqrt(jnp.mean(x * x, axis=-1, keepdims=True) + EPS) * g_ref[...]).astype(BF16)
    proj = lambda a, b: jnp.dot(y, w_ref[:, a:b], preferred_element_type=F32)
    cos_t, sin_lo, sin_hi = cos_ref[...], slo_ref[...], shi_ref[...]
    q_ref[...] = (_rope_lanes(proj(A_Q0, A_KVC0), cos_t, sin_lo, sin_hi) * (HD ** -0.5)).astype(BF16)
    kvc_ref[...] = proj(A_KVC0, A_KVS0)
    for a0, kv_ref, kd_ref, vd_ref in ((A_KVS0, kvs_ref, ksd_ref, vsd_ref), (A_KVW0, kvw_ref, kwd_ref, vwd_ref)):
        kv = proj(a0, a0 + 2 * KV_HALF)
        k = _rope_lanes(kv[:, :KV_HALF], cos_t, sin_lo, sin_hi)
        v = kv[:, KV_HALF:]
        kv_ref[:, :KV_HALF] = k
        kv_ref[:, KV_HALF:] = v
        kd_ref[...] = _dup_groups(k).astype(BF16)
        vd_ref[...] = _dup_groups(v).astype(BF16)
    misc = proj(A_MISC0, A_COLS)
    gates_ref[...] = jax.nn.sigmoid(misc)
    z = jnp.dot(misc.astype(BF16), wa2_ref[...], preferred_element_type=F32) + ba_ref[...]
    la_ref[...] = -_softplus(-z) * (1.0 / GLA_TAU)
    gq_ref[...] = proj(A_GQ0, A_GK0) * (GLA_DK ** -0.5)
    gk_ref[...] = proj(A_GK0, A_GV0)
    gv_ref[...] = proj(A_GV0, A_GG0).astype(BF16)
    gg = proj(A_GG0, A_MISC0)
    sgg_ref[...] = gg * jax.nn.sigmoid(gg)


def _rope_tables(pos):
    half = ROPE_DIM // 2
    inv = 1.0 / (ROPE_THETA ** (jnp.arange(0, ROPE_DIM, 2, dtype=F32) / ROPE_DIM))
    ang = pos.astype(F32)[:, None] * inv[None, :]
    cos, sin = jnp.cos(ang), jnp.sin(ang)
    n = pos.shape[0]
    one = jnp.ones((n, HD - ROPE_DIM), F32)
    zero = jnp.zeros((n, HD - ROPE_DIM), F32)
    zh = jnp.zeros((n, half), F32)
    seg = lambda a, b, rest: jnp.concatenate([a, b, rest] * (LANES // HD), axis=1)
    return seg(cos, cos, one), seg(-sin, zh, zero), seg(zh, sin, zero)


def inproj_a(h, g, w_in, wa2, ba, pos):
    n = h.shape[0]
    zpad = jnp.zeros((D_MODEL, LANES - 3 * NSA_HEADS - GLA_LOWRANK), F32)
    o = np.cumsum((0,) + A_SIZES)
    w = jnp.concatenate([w_in[:, o[0]:o[4]], w_in[:, o[5]:o[8]], w_in[:, o[9]:o[10]],
                         w_in[:, o[4]:o[5]], w_in[:, o[8]:o[9]], zpad], axis=1).astype(BF16)
    wa2p = jnp.zeros((LANES, GLA_K_W), F32).at[MISC_LR0:MISC_LR0 + GLA_LOWRANK].set(wa2).astype(BF16)
    cos_t, sin_lo, sin_hi = _rope_tables(pos)
    full = lambda shape: pl.BlockSpec(shape, lambda i: (0,) * len(shape))
    rows = lambda wd: pl.BlockSpec((ROW_TILE, wd), lambda i: (i, 0))
    widths = [(NSA_Q_W, BF16), (NSA_KV_W, F32), (NSA_KV_W, F32), (NSA_KV_W, F32),
              (NSA_KV_W, BF16), (NSA_KV_W, BF16), (NSA_KV_W, BF16), (NSA_KV_W, BF16),
              (LANES, F32), (GLA_K_W, F32), (GLA_K_W, F32), (GLA_V_W, BF16), (GLA_K_W, F32), (GLA_V_W, F32)]
    return pl.pallas_call(
        _inproj_a_body,
        grid=(n // ROW_TILE,),
        in_specs=[rows(D_MODEL), full((1, D_MODEL)), full((D_MODEL, A_COLS)), full((LANES, GLA_K_W)),
                  full((1, GLA_K_W)), rows(LANES), rows(LANES), rows(LANES)],
        out_specs=[rows(wd) for wd, _ in widths],
        out_shape=[jax.ShapeDtypeStruct((n, wd), dt) for wd, dt in widths],
        compiler_params=pltpu.CompilerParams(dimension_semantics=("arbitrary",),
                                             vmem_limit_bytes=VMEM_LIMIT_BYTES),
        name="inproj_a",
    )(h, g.reshape(1, D_MODEL), w, wa2p, ba.reshape(1, GLA_K_W), cos_t, sin_lo, sin_hi)


CMP_ROW = CMP_BLOCK * NSA_KV_W
CMP_TILE = 256


def _cmp_body(x_ref, pe_ref, w_ref, o_ref):
    x = (x_ref[...] + pe_ref[...]).astype(BF16)
    o_ref[...] = jnp.dot(x, w_ref[...], preferred_element_type=F32)


def cmp_blocks(x, pe, w_cmp):
    rows = x.shape[0]
    pe_flat = jnp.broadcast_to(pe[:, :, None, :], (CMP_BLOCK, 2, NSA_KV, HD)).reshape(1, CMP_ROW)
    eye_c = jnp.eye(2, dtype=F32)
    eye_g = jnp.eye(NSA_KV, dtype=F32)
    w_full = jnp.einsum('lcde,cx,gy->lcgdxye', w_cmp, eye_c, eye_g).reshape(CMP_ROW, NSA_KV_W).astype(BF16)
    return pl.pallas_call(
        _cmp_body,
        grid=(rows // CMP_TILE,),
        in_specs=[pl.BlockSpec((CMP_TILE, CMP_ROW), lambda i: (i, 0)),
                  pl.BlockSpec((1, CMP_ROW), lambda i: (0, 0)),
                  pl.BlockSpec((CMP_ROW, NSA_KV_W), lambda i: (0, 0))],
        out_specs=pl.BlockSpec((CMP_TILE, NSA_KV_W), lambda i: (i, 0)),
        out_shape=jax.ShapeDtypeStruct((rows, NSA_KV_W), F32),
        compiler_params=pltpu.CompilerParams(dimension_semantics=("arbitrary",),
                                             vmem_limit_bytes=VMEM_LIMIT_BYTES),
        name="cmp_blocks",
    )(x, pe_flat, w_full)


KEY_TILE = 256
NEG_BIG = -1e30
N_CMP_PROMPT = SEQ // CMP_BLOCK
N_SLC_PROMPT = SEQ // SLC_BLOCK
SLC_SHIFT = SLC_BLOCK.bit_length() - 1


def _stack_heads(q_tile, g):
    lane = lax.broadcasted_iota(jnp.int32, (q_tile.shape[0], LANES), 1)
    parts = []
    for r in range(NSA_REP):
        hd = g * NSA_REP + r
        pair = q_tile[:, (hd // 2) * LANES:(hd // 2 + 1) * LANES]
        keep = (lane < HD) if hd % 2 == 0 else (lane >= HD)
        parts.append(jnp.where(keep, pair, jnp.zeros_like(pair)))
    return jnp.concatenate(parts, axis=0)


def _nt_dot(a, b):
    return lax.dot_general(a, b, (((1,), (1,)), ((), ())), preferred_element_type=F32)


def _flash_tile(qs, k, v, mask, m_sc, l_sc, acc_sc):
    nq, tk = mask.shape
    s = _nt_dot(qs, k).reshape(NSA_REP, nq, tk)
    s = jnp.where(mask[None], s, NEG_BIG)
    m_old = m_sc[...]
    m_new = jnp.maximum(m_old, jnp.max(s, axis=-1, keepdims=True).reshape(NSA_REP * nq, 1))
    alpha = jnp.exp(m_old - m_new)
    p = jnp.where(mask[None], jnp.exp(s - m_new.reshape(NSA_REP, nq, 1)), 0.0)
    l_sc[...] = alpha * l_sc[...] + jnp.sum(p, axis=-1, keepdims=True).reshape(NSA_REP * nq, 1)
    acc_sc[...] = alpha * acc_sc[...] + jnp.dot(p.reshape(NSA_REP * nq, tk).astype(BF16), v,
                                                preferred_element_type=F32)
    m_sc[...] = m_new


def _flash_init(m_sc, l_sc, acc_sc):
    m_sc[...] = jnp.full(m_sc.shape, NEG_BIG, F32)
    l_sc[...] = jnp.zeros(l_sc.shape, F32)
    acc_sc[...] = jnp.zeros(acc_sc.shape, F32)


def _flash_out(l_sc, acc_sc):
    return acc_sc[...] / jnp.maximum(l_sc[...], 1e-30)


def _select_blocks(p_slc, qpos, n_top):
    blk = lax.broadcasted_iota(jnp.int32, p_slc.shape, 1)
    cur = qpos >> SLC_SHIFT
    forced = (blk == 0) | (blk == cur) | (blk == cur - 1)
    score = jnp.where(forced, jnp.inf, p_slc)
    score = jnp.where(blk <= cur, score, -jnp.inf)
    sel = jnp.zeros(p_slc.shape, F32)
    for _ in range(n_top):
        m = jnp.max(score, axis=-1, keepdims=True)
        idx = jnp.min(jnp.where(score == m, blk, p_slc.shape[1]), axis=-1, keepdims=True)
        hit = blk == idx
        sel = jnp.where(hit & (m > -jnp.inf), 1.0, sel)
        score = jnp.where(hit, -jnp.inf, score)
    return sel


def _nsa_prompt_body(q_ref, gates_ref, ck_ref, cv_ref, ks_ref, vs_ref, kw_ref, vw_ref, o_ref,
                     m_sc, l_sc, acc_sc):
    i = pl.program_id(1)
    nq = Q_BLOCK
    qpos = i * nq + lax.broadcasted_iota(jnp.int32, (nq, 1), 0)
    q_tile = q_ref[...]
    gates = gates_ref[...]
    lane = lax.broadcasted_iota(jnp.int32, (nq, LANES), 1)
    col = lax.broadcasted_iota(jnp.int32, (1, N_CMP_PROMPT), 1)
    half_c = N_CMP_PROMPT // 2
    cblk = jnp.where(col < half_c, 2 * col, 2 * (col - half_c) + 1)
    c_end = (cblk + 1) * CMP_BLOCK - 1
    kcol = lax.broadcasted_iota(jnp.int32, (1, KEY_TILE), 1)
    blk_row = lax.broadcasted_iota(jnp.int32, (N_SLC_PROMPT, KEY_TILE), 0)
    key_col = lax.broadcasted_iota(jnp.int32, (N_SLC_PROMPT, KEY_TILE), 1)
    for g in range(NSA_KV):
        gl = slice(g * LANES, (g + 1) * LANES)
        qs = _stack_heads(q_tile, g)
        c_mask = c_end <= qpos
        s = _nt_dot(qs, ck_ref[0, :, gl]).reshape(NSA_REP, nq, N_CMP_PROMPT)
        s = jnp.where(c_mask[None], s, -jnp.inf)
        m = jnp.max(s, axis=-1, keepdims=True)
        m = jnp.where(m > -jnp.inf, m, 0.0)
        e = jnp.where(c_mask[None], jnp.exp(s - m), 0.0)
        p = e / jnp.maximum(jnp.sum(e, axis=-1, keepdims=True), 1e-30)
        o_c = jnp.dot(p.reshape(NSA_REP * nq, N_CMP_PROMPT).astype(BF16), cv_ref[0, :, gl],
                      preferred_element_type=F32)
        p_grp = p[0] + p[1] + p[2] + p[3]
        p_slc = p_grp[:, :half_c] + p_grp[:, half_c:]
        sel = _select_blocks(p_slc, qpos, SLC_TOPK).astype(BF16)
        _flash_init(m_sc, l_sc, acc_sc)

        def slc_tile(kt, _):
            k0 = pl.multiple_of(kt * KEY_TILE, KEY_TILE)
            kpos = k0 + kcol
            expand = jnp.where(((k0 + key_col) >> SLC_SHIFT) == blk_row, 1.0, 0.0).astype(BF16)
            chosen = jnp.dot(sel, expand, preferred_element_type=F32) > 0.5
            _flash_tile(qs, ks_ref[pl.ds(k0, KEY_TILE), gl], vs_ref[pl.ds(k0, KEY_TILE), gl],
                        chosen & (kpos <= qpos), m_sc, l_sc, acc_sc)
            return 0

        lax.fori_loop(0, i // 2 + 1, slc_tile, 0)
        o_s = _flash_out(l_sc, acc_sc)
        _flash_init(m_sc, l_sc, acc_sc)

        def win_tile(kt, _):
            k0 = pl.multiple_of(kt * KEY_TILE, KEY_TILE)
            d = qpos - (k0 + kcol)
            _flash_tile(qs, kw_ref[pl.ds(k0, KEY_TILE), gl], vw_ref[pl.ds(k0, KEY_TILE), gl],
                        (d >= 0) & (d < WINDOW), m_sc, l_sc, acc_sc)
            return 0

        lax.fori_loop(jnp.maximum(i - WINDOW // nq, 0) // 2, i // 2 + 1, win_tile, 0)
        o_w = _flash_out(l_sc, acc_sc)
        outs = []
        for r in range(NSA_REP):
            hd = g * NSA_REP + r
            gcol = lambda br: jnp.sum(jnp.where(lane == 3 * hd + br, gates, 0.0), axis=-1, keepdims=True)
            rs = slice(r * nq, (r + 1) * nq)
            outs.append(gcol(0) * o_c[rs] + gcol(1) * o_s[rs] + gcol(2) * o_w[rs])
        for j in range(NSA_REP // 2):
            pair = jnp.where(lane < HD, outs[2 * j], outs[2 * j + 1])
            c0 = (g * NSA_REP // 2 + j) * LANES
            o_ref[:, c0:c0 + LANES] = pair.astype(BF16)


def nsa_prompt_attn(q, gates, ck, cv, ks, vs, kw, vw):
    nqb = SEQ // Q_BLOCK
    tok = lambda wd: pl.BlockSpec((Q_BLOCK, wd), lambda b, i: (b * nqb + i, 0))
    seq = pl.BlockSpec((SEQ, NSA_KV_W), lambda b, i: (b, 0))
    cmp = pl.BlockSpec((1, N_CMP_PROMPT, NSA_KV_W), lambda b, i: (b, 0, 0))
    return pl.pallas_call(
        _nsa_prompt_body,
        grid=(BATCH, nqb),
        in_specs=[tok(NSA_Q_W), tok(LANES), cmp, cmp, seq, seq, seq, seq],
        out_specs=tok(NSA_Q_W),
        out_shape=jax.ShapeDtypeStruct((N_PROMPT, NSA_Q_W), BF16),
        scratch_shapes=[pltpu.VMEM((NSA_REP * Q_BLOCK, 1), F32), pltpu.VMEM((NSA_REP * Q_BLOCK, 1), F32),
                        pltpu.VMEM((NSA_REP * Q_BLOCK, LANES), F32)],
        compiler_params=pltpu.CompilerParams(dimension_semantics=("arbitrary", "arbitrary"),
                                             vmem_limit_bytes=VMEM_LIMIT_BYTES),
        name="nsa_prompt",
    )(q, gates, ck, cv, ks, vs, kw, vw)


GLA_PAIRS = GLA_HEADS // 2
GLA_ROWS = 128


def _gla_rows(q, k, la, v, sgg, gnorm, st_ref, sub):
    R = q.shape[0]
    row = lax.broadcasted_iota(jnp.int32, (R, GLA_K_W), 0)
    rin = row % sub
    cum = la
    d = 1
    while d < sub:
        cum = cum + jnp.where(rin >= d, pltpu.roll(cum, d, 0), 0.0)
        d *= 2
    lane = lax.broadcasted_iota(jnp.int32, (sub, LANES), 1)
    lo = lane < GLA_DK
    rsub = lax.broadcasted_iota(jnp.int32, (sub, LANES), 0)
    out_rows = []
    for c in range(R // sub):
        rs = slice(c * sub, (c + 1) * sub)
        cum_c = cum[rs]
        last = cum_c[sub - 1:sub]
        qe = q[rs] * jnp.exp(cum_c)
        kdec = k[rs] * jnp.exp(last - cum_c)
        v_c = v[rs]
        heads = []
        for pr in range(GLA_PAIRS):
            ls = slice(pr * LANES, (pr + 1) * LANES)
            st = st_ref[pr]
            st_b = st.astype(BF16)
            qe_p, kd_p, q_p, k_p, cum_p = qe[:, ls], kdec[:, ls], q[rs, ls], k[rs, ls], cum_c[:, ls]
            v_pair = [v_c[:, (2 * pr + hh) * GLA_DV:(2 * pr + hh + 1) * GLA_DV] for hh in range(2)]
            upd = jnp.zeros((GLA_DV, LANES), F32)
            o_pair = []
            for hh in range(2):
                keep = lo if hh == 0 else jnp.logical_not(lo)
                o_pair.append(_nt_dot(jnp.where(keep, qe_p, 0.0).astype(BF16), st_b))
                upd = upd + lax.dot_general(v_pair[hh].astype(BF16), jnp.where(keep, kd_p, 0.0).astype(BF16),
                                            (((0,), (0,)), ((), ())), preferred_element_type=F32)
            for j in range(sub):
                dj = jnp.where(rsub >= j, jnp.exp(cum_p - cum_p[j:j + 1]), 0.0)
                w = q_p * k_p[j:j + 1] * dj
                a_lo = jnp.sum(jnp.where(lo, w, 0.0), axis=-1, keepdims=True)
                a_hi = jnp.sum(jnp.where(lo, 0.0, w), axis=-1, keepdims=True)
                o_pair[0] = o_pair[0] + a_lo * v_pair[0][j:j + 1]
                o_pair[1] = o_pair[1] + a_hi * v_pair[1][j:j + 1]
            st_ref[pr] = st * jnp.exp(last[:, ls]) + upd
            heads += o_pair
        o = []
        for hd in range(GLA_HEADS):
            x = heads[hd]
            o.append(x * lax.rsqrt(jnp.mean(x * x, axis=-1, keepdims=True) + EPS) * gnorm)
        out_rows.append(jnp.concatenate(o, axis=1))
    return jnp.concatenate(out_rows, axis=0) * sgg


def _gla_seq_body(q_ref, k_ref, la_ref, v_ref, sgg_ref, gn_ref, o_ref, st_out_ref, st_sc):
    @pl.when(pl.program_id(1) == 0)
    def _():
        st_sc[...] = jnp.zeros(st_sc.shape, F32)

    o = _gla_rows(q_ref[...], k_ref[...], la_ref[...], v_ref[...].astype(F32), sgg_ref[...], gn_ref[...],
                  st_sc, GLA_SUB)
    o_ref[...] = o.astype(BF16)
    st_out_ref[0] = st_sc[...]


GLA_SUB = 16


def gla_seq(q, k, la, v, sgg, gnorm):
    nt = SEQ // GLA_ROWS
    rows = lambda wd: pl.BlockSpec((GLA_ROWS, wd), lambda b, t: (b * nt + t, 0))
    return pl.pallas_call(
        _gla_seq_body,
        grid=(BATCH, nt),
        in_specs=[rows(GLA_K_W), rows(GLA_K_W), rows(GLA_K_W), rows(GLA_V_W), rows(GLA_V_W),
                  pl.BlockSpec((1, GLA_DV), lambda b, t: (0, 0))],
        out_specs=[rows(GLA_V_W), pl.BlockSpec((1, GLA_PAIRS, GLA_DV, LANES), lambda b, t: (b, 0, 0, 0))],
        out_shape=[jax.ShapeDtypeStruct((N_PROMPT, GLA_V_W), BF16),
                   jax.ShapeDtypeStruct((BATCH, GLA_PAIRS, GLA_DV, LANES), F32)],
        scratch_shapes=[pltpu.VMEM((GLA_PAIRS, GLA_DV, LANES), F32)],
        compiler_params=pltpu.CompilerParams(dimension_semantics=("arbitrary", "arbitrary"),
                                             vmem_limit_bytes=VMEM_LIMIT_BYTES),
        name="gla_seq",
    )(q, k, la, v, sgg, gnorm.reshape(1, GLA_DV))


GLA_STEP_SEQS = 8


def _gla_step_body(q_ref, k_ref, la_ref, v_ref, sgg_ref, gn_ref, st_in_ref, o_ref, st_out_ref):
    st_out_ref[...] = st_in_ref[...]
    q, k, la, v, sgg = q_ref[...], k_ref[...], la_ref[...], v_ref[...].astype(F32), sgg_ref[...]
    for j in range(GLA_STEP_SEQS):
        rs = slice(j * DEC_SEQ, (j + 1) * DEC_SEQ)
        o = _gla_rows(q[rs], k[rs], la[rs], v[rs], sgg[rs], gn_ref[...], st_out_ref.at[j], DEC_SEQ)
        o_ref[rs, :] = o.astype(BF16)


def gla_step(q, k, la, v, sgg, gnorm, st_in, row0):
    rows_per = GLA_STEP_SEQS * DEC_SEQ
    blk0 = row0 // rows_per
    rows = lambda wd: pl.BlockSpec((rows_per, wd), lambda i: (blk0 + i, 0))
    st_spec = pl.BlockSpec((GLA_STEP_SEQS, GLA_PAIRS, GLA_DV, LANES), lambda i: (i, 0, 0, 0))
    return pl.pallas_call(
        _gla_step_body,
        grid=(DEC_BATCH // GLA_STEP_SEQS,),
        in_specs=[rows(GLA_K_W), rows(GLA_K_W), rows(GLA_K_W), rows(GLA_V_W), rows(GLA_V_W),
                  pl.BlockSpec((1, GLA_DV), lambda i: (0, 0)), st_spec],
        out_specs=[pl.BlockSpec((rows_per, GLA_V_W), lambda i: (i, 0)), st_spec],
        out_shape=[jax.ShapeDtypeStruct((N_SAMPLE, GLA_V_W), BF16),
                   jax.ShapeDtypeStruct((DEC_BATCH, GLA_PAIRS, GLA_DV, LANES), F32)],
        compiler_params=pltpu.CompilerParams(dimension_semantics=("arbitrary",),
                                             vmem_limit_bytes=VMEM_LIMIT_BYTES),
        name="gla_step",
    )(q, k, la, v, sgg, gnorm.reshape(1, GLA_DV), st_in)


def _gla_state_to_pairs(s):
    B = s.shape[0]
    return s.reshape(B, GLA_PAIRS, 2, GLA_DK, GLA_DV).transpose(0, 1, 4, 2, 3).reshape(B, GLA_PAIRS, GLA_DV, LANES)


def _gla_state_from_pairs(st):
    B = st.shape[0]
    return st.reshape(B, GLA_PAIRS, GLA_DV, 2, GLA_DK).transpose(0, 1, 3, 4, 2).reshape(B, GLA_HEADS, GLA_DK, GLA_DV)


def rmsnorm(x, g):
    y = x * lax.rsqrt(jnp.mean(x * x, axis=-1, keepdims=True) + EPS)
    return y * g


def rope(x, pos):
    half = ROPE_DIM // 2
    inv = 1.0 / (ROPE_THETA ** (jnp.arange(0, ROPE_DIM, 2, dtype=F32) / ROPE_DIM))
    ang = pos.astype(F32)[:, None] * inv[None, :]
    cos = jnp.cos(ang)[:, None, :]
    sin = jnp.sin(ang)[:, None, :]
    xr = x[..., :ROPE_DIM]
    x1, x2 = xr[..., :half], xr[..., half:]
    rot = jnp.concatenate([x1 * cos - x2 * sin, x1 * sin + x2 * cos], axis=-1)
    return jnp.concatenate([rot, x[..., ROPE_DIM:]], axis=-1)


def masked_softmax(s, mask):
    s = jnp.where(mask, s, -jnp.inf)
    m = jnp.max(s, axis=-1, keepdims=True)
    m = jnp.where(jnp.isfinite(m), m, 0.0)
    e = jnp.where(mask, jnp.exp(s - m), 0.0)
    return e / jnp.maximum(jnp.sum(e, axis=-1, keepdims=True), 1e-30)


def nsa_key_blocks(kv_cmp, kv_slc, pe, w_cmp):
    B, T = kv_cmp.shape[:2]
    nc = T // CMP_BLOCK
    blk = kv_cmp[:, :nc * CMP_BLOCK].reshape(B, nc, CMP_BLOCK, 2, NSA_KV, HD) + pe[:, :, None, :]
    ckv = jnp.einsum('bnlcgd,lcde->bncge', blk, w_cmp)
    ck = rope(ckv[:, :, 0], (jnp.arange(nc, dtype=jnp.int32) + 1) * CMP_BLOCK - 1)
    cv = ckv[:, :, 1]
    ns = -(-T // SLC_BLOCK)
    skv = jnp.pad(kv_slc, ((0, 0), (0, ns * SLC_BLOCK - T), (0, 0), (0, 0), (0, 0)))
    skv = skv.reshape(B, ns, SLC_BLOCK, 2, NSA_KV, HD).transpose(3, 0, 4, 1, 2, 5)
    return ck, cv, skv[0], skv[1]


def nsa_query_block(q, qpos, gates, ck, cv, sk, sv, wk, wv, wpos):
    B, QB = q.shape[:2]
    qg = q.reshape(B, QB, NSA_KV, NSA_REP, HD)
    nc = ck.shape[1]
    c_end = (jnp.arange(nc, dtype=jnp.int32) + 1) * CMP_BLOCK - 1
    c_mask = c_end[None, :] <= qpos[:, None]
    s_c = jnp.einsum('bqgrd,bngd->bgrqn', qg, ck)
    p_c = masked_softmax(s_c, c_mask)
    o_c = jnp.einsum('bgrqn,bngd->bqgrd', p_c, cv)
    ns = sk.shape[2]
    ratio = SLC_BLOCK // CMP_BLOCK
    p_grp = jnp.sum(p_c, axis=2)
    p_grp = jnp.pad(p_grp, ((0, 0), (0, 0), (0, 0), (0, ns * ratio - nc)))
    p_slc = p_grp.reshape(B, NSA_KV, QB, ns, ratio).sum(-1)
    blk = jnp.arange(ns, dtype=jnp.int32)[None, :]
    cur = (qpos // SLC_BLOCK)[:, None]
    forced = (blk == 0) | (blk == cur) | (blk == cur - 1)
    score = jnp.where(forced, jnp.inf, p_slc)
    score = jnp.where(blk <= cur, score, -jnp.inf)
    n_top = min(SLC_TOPK, ns)
    top_v, top_i = lax.top_k(score, n_top)
    bi = jnp.arange(B)[:, None, None, None]
    gi = jnp.arange(NSA_KV)[None, :, None, None]
    k_sel = sk[bi, gi, top_i]
    v_sel = sv[bi, gi, top_i]
    k_pos = top_i[..., None] * SLC_BLOCK + jnp.arange(SLC_BLOCK, dtype=jnp.int32)
    s_mask = (top_v > -jnp.inf)[..., None] & (k_pos <= qpos[None, None, :, None, None])
    s_s = jnp.einsum('bqgrd,bgqnld->bgrqnl', qg, k_sel)
    s_s = s_s.reshape(B, NSA_KV, NSA_REP, QB, n_top * SLC_BLOCK)
    p_s = masked_softmax(s_s, s_mask.reshape(B, NSA_KV, 1, QB, n_top * SLC_BLOCK))
    p_s = p_s.reshape(B, NSA_KV, NSA_REP, QB, n_top, SLC_BLOCK)
    o_s = jnp.einsum('bgrqnl,bgqnld->bqgrd', p_s, v_sel)
    d = qpos[:, None] - wpos[None, :]
    w_mask = (d >= 0) & (d < WINDOW) & (wpos[None, :] >= 0)
    s_w = jnp.einsum('bqgrd,bkgd->bgrqk', qg, wk)
    p_w = masked_softmax(s_w, w_mask)
    o_w = jnp.einsum('bgrqk,bkgd->bqgrd', p_w, wv)
    g = gates.reshape(B, QB, NSA_KV, NSA_REP, 3)
    o = g[..., 0:1] * o_c + g[..., 1:2] * o_s + g[..., 2:3] * o_w
    return o.reshape(B, QB, NSA_Q_W)


def nsa_prompt(q, gates, ck, cv, sk, sv, kv_win):
    B, T = q.shape[:2]
    wkv = jnp.pad(kv_win, ((0, 0), (WINDOW, 0), (0, 0), (0, 0), (0, 0)))

    def one_block(i):
        s0 = i * Q_BLOCK
        qb = lax.dynamic_slice_in_dim(q, s0, Q_BLOCK, axis=1)
        gb = lax.dynamic_slice_in_dim(gates, s0, Q_BLOCK, axis=1)
        wb = lax.dynamic_slice_in_dim(wkv, s0, WINDOW + Q_BLOCK, axis=1)
        qpos = s0 + jnp.arange(Q_BLOCK, dtype=jnp.int32)
        wpos = s0 - WINDOW + jnp.arange(WINDOW + Q_BLOCK, dtype=jnp.int32)
        return nsa_query_block(qb, qpos, gb, ck, cv, sk, sv, wb[:, :, 0], wb[:, :, 1], wpos)

    out = lax.map(one_block, jnp.arange(T // Q_BLOCK, dtype=jnp.int32))
    return jnp.swapaxes(out, 0, 1).reshape(B, T, NSA_Q_W)


def gla_chunked(q, k, v, log_a, s0, chunk):
    B, T, H, DK = q.shape
    nck = T // chunk

    def to_chunks(x):
        return jnp.swapaxes(x.reshape(B, nck, chunk, *x.shape[2:]), 0, 1)

    causal = jnp.tril(jnp.ones((chunk, chunk), dtype=bool))

    def step(S, inp):
        qf, kf, vf, lac = inp
        cum = jnp.cumsum(lac, axis=1)
        o_inter = jnp.einsum('bchk,bhkv->bchv', qf * jnp.exp(cum), S)
        diff = cum[:, :, None] - cum[:, None, :]
        diff = jnp.where(causal[None, :, :, None, None], diff, -jnp.inf)
        att = jnp.einsum('bihk,bjhk,bijhk->bhij', qf, kf, jnp.exp(diff))
        o_intra = jnp.einsum('bhij,bjhv->bihv', att, vf)
        last = cum[:, -1]
        kdec = kf * jnp.exp(last[:, None] - cum)
        S = jnp.exp(last)[..., None] * S + jnp.einsum('bchk,bchv->bhkv', kdec, vf)
        return S, o_inter + o_intra

    S, o = lax.scan(step, s0, (to_chunks(q), to_chunks(k), to_chunks(v), to_chunks(log_a)))
    return jnp.swapaxes(o, 0, 1).reshape(B, T, H, v.shape[-1]), S


def _cmp_for_attn(x):
    B, nc = x.shape[:2]
    x = jnp.repeat(x, 2, axis=2).reshape(B, nc, NSA_KV_W)
    return jnp.concatenate([x[:, 0::2], x[:, 1::2]], axis=1).astype(BF16)


def mixer_a(h, p, past):
    pos = jnp.concatenate([jnp.tile(jnp.arange(SEQ, dtype=jnp.int32), BATCH),
                           PAST_LEN + jnp.tile(jnp.arange(DEC_SEQ, dtype=jnp.int32), DEC_BATCH)])
    (q, kvc, kvs, kvw, ksd, vsd, kwd, vwd, gates, gq, gk, gv, la, sgg) = inproj_a(
        h, p['norm_mix'][0], p['a_w_in'][0], p['a_gla_wa2'][0], p['a_gla_ba'][0], pos)
    pe, wc, gnorm = p['a_cmp_pe'][0], p['a_cmp_w'][0], p['a_gla_norm'][0]
    P = N_PROMPT
    ckv = cmp_blocks(kvc[:P].reshape(BATCH * N_CMP_PROMPT, CMP_ROW), pe, wc)
    ckv = ckv.reshape(BATCH, N_CMP_PROMPT, 2, NSA_KV, HD)
    c_pos = (jnp.arange(N_CMP_PROMPT, dtype=jnp.int32) + 1) * CMP_BLOCK - 1
    ck = _cmp_for_attn(rope(ckv[:, :, 0], c_pos))
    cv = _cmp_for_attn(ckv[:, :, 1])
    o_nsa_p = nsa_prompt_attn(q, gates, ck, cv, ksd, vsd, kwd, vwd)
    o_gla_p, st_p = gla_seq(gq, gk, la, gv, sgg, gnorm)
    kv5 = lambda x, lead: x.reshape(*lead, 2, NSA_KV, HD)
    new_p = (kv5(kvc[:P], (BATCH, SEQ)), kv5(kvs[:P], (BATCH, SEQ)),
             kv5(kvw[:P], (BATCH, SEQ))[:, SEQ - WINDOW:], _gla_state_from_pairs(st_p))
    o_gla_s, st_s = gla_step(gq, gk, la, gv, sgg, gnorm, _gla_state_to_pairs(past['state_gla'][0]), P)
    bt = lambda x: x[P:].reshape(DEC_BATCH, DEC_SEQ, x.shape[-1])
    kv_cmp, kv_slc, kv_win = (kv5(bt(x), (DEC_BATCH, DEC_SEQ)) for x in (kvc, kvs, kvw))
    pt = past['page_table']
    old_cmp = past['cache_cmp_kv'][0, pt].reshape(DEC_BATCH, -1, 2, NSA_KV, HD)
    old_slc = past['cache_slc_kv'][0, pt].reshape(DEC_BATCH, -1, 2, NSA_KV, HD)
    ckd, cvd, sk, sv = nsa_key_blocks(jnp.concatenate([old_cmp, kv_cmp], axis=1),
                                      jnp.concatenate([old_slc, kv_slc], axis=1), pe, wc)
    buf = past['cache_win_kv'][0]
    wkv = jnp.concatenate([buf, kv_win], axis=1)
    n_w = wkv.shape[1]
    dpos = PAST_LEN + jnp.arange(DEC_SEQ, dtype=jnp.int32)
    wpos = PAST_LEN + DEC_SEQ - n_w + jnp.arange(n_w, dtype=jnp.int32)
    qd = bt(q).astype(F32).reshape(DEC_BATCH, DEC_SEQ, NSA_HEADS, HD)
    gd = bt(gates)[..., :3 * NSA_HEADS]
    o_nsa_s = nsa_query_block(qd, dpos, gd, ckd, cvd, sk, sv, wkv[:, :, 0], wkv[:, :, 1], wpos)
    o_nsa_s = o_nsa_s.reshape(N_SAMPLE, NSA_Q_W).astype(BF16)
    new_s = (kv_cmp, kv_slc, wkv[:, n_w - buf.shape[1]:], _gla_state_from_pairs(st_s))
    o_nsa = jnp.concatenate([o_nsa_p, o_nsa_s], axis=0)
    o_gla = jnp.concatenate([o_gla_p, o_gla_s], axis=0)
    return o_nsa, o_gla, new_p, new_s


def block_diag(x, w, b):
    lead = x.shape[:-1]
    xb = x.reshape(*lead, LRU_BLOCKS, LRU_BW)
    return (jnp.einsum('btnw,nwv->btnv', xb, w) + b.reshape(LRU_BLOCKS, LRU_BW)).reshape(*lead, D_RNN)


def rglru(x, h0, w_a, b_a, w_x, b_x, lam):
    r = jax.nn.sigmoid(block_diag(x, w_a, b_a))
    i = jax.nn.sigmoid(block_diag(x, w_x, b_x))
    log_a = -LRU_C * r * jax.nn.softplus(-lam)
    a = jnp.exp(log_a)
    u = jnp.sqrt(-jnp.expm1(2.0 * log_a)) * (i * x)

    def step(h, inp):
        a_t, u_t = inp
        h = a_t * h + u_t
        return h, h

    hT, hs = lax.scan(step, h0, (jnp.swapaxes(a, 0, 1), jnp.swapaxes(u, 0, 1)))
    return jnp.swapaxes(hs, 0, 1), hT


def mixer_c(u, p, past):
    B, T, _ = u.shape
    gate_b, x_b = jnp.split(u, 2, axis=-1)
    if past is None:
        buf = jnp.zeros((B, CONV_W - 1, D_RNN), F32)
        h0 = jnp.zeros((B, D_RNN), F32)
    else:
        buf = past['state_conv'][0]
        h0 = past['state_lru'][0]
    xp = jnp.concatenate([buf, x_b], axis=1)
    w = p['c_conv_w'][0][:, None, :]
    xc = lax.conv_general_dilated(xp, w, (1,), 'VALID', dimension_numbers=('NWC', 'WIO', 'NWC'),
                                  feature_group_count=D_RNN) + p['c_conv_b'][0]
    new_buf = xp[:, xp.shape[1] - (CONV_W - 1):]
    y, hT = rglru(xc, h0, p['c_w_a'][0], p['c_b_a'][0], p['c_w_x'][0], p['c_b_x'][0], p['c_lam'][0])
    return jax.nn.gelu(gate_b) * y, (hT, new_buf)


def hier_moe(xt, w_rg, b_rg, w_re, b_re, w1, w3, w2):
    n, D = xt.shape
    rows = jnp.arange(n)
    hp = lax.Precision.HIGHEST
    g_logit = jnp.dot(xt, w_rg, precision=hp) + b_rg
    g_prob = jax.nn.softmax(g_logit, axis=-1)
    g_top = jnp.argmax(g_logit, axis=-1).astype(jnp.int32)
    g_w = g_prob[rows, g_top][:, None]
    e_logit = (jnp.dot(xt, w_re, precision=hp) + b_re).reshape(n, N_GROUPS, EXP_PER_GROUP)
    e_logit = e_logit[rows, g_top]
    e_val, e_idx = lax.top_k(e_logit, TOP_K_IN_GROUP)
    gate = jax.nn.softmax(e_val, axis=-1) * g_w
    expert = (g_top[:, None] * EXP_PER_GROUP + e_idx).astype(jnp.int32)
    nk = n * TOP_K_IN_GROUP
    flat_e = expert.reshape(nk)
    flat_t = jnp.repeat(jnp.arange(n, dtype=jnp.int32), TOP_K_IN_GROUP)
    flat_w = gate.reshape(nk)
    order = jnp.argsort(flat_e)
    se, st, sw = flat_e[order], flat_t[order], flat_w[order]
    counts = jnp.bincount(flat_e, length=N_EXPERTS)
    padded = ((counts + MOE_BLOCK - 1) // MOE_BLOCK) * MOE_BLOCK
    pad_end = jnp.cumsum(padded)
    pad_start = pad_end - padded
    cnt_start = jnp.cumsum(counts) - counts
    dest = pad_start[se] + (jnp.arange(nk, dtype=jnp.int32) - cnt_start[se])
    n_slots = (-(-nk // MOE_BLOCK) + N_EXPERTS) * MOE_BLOCK
    nb = n_slots // MOE_BLOCK
    slot_tok = jnp.full((n_slots,), n, jnp.int32).at[dest].set(st)
    slot_w = jnp.zeros((n_slots,), F32).at[dest].set(sw)
    blk_start = jnp.arange(nb, dtype=jnp.int32) * MOE_BLOCK
    blk_exp = jnp.minimum(jnp.searchsorted(pad_end, blk_start, side='right'), N_EXPERTS - 1)
    x_pad = jnp.concatenate([xt, jnp.zeros((1, D), xt.dtype)], axis=0)

    def expert_block(args):
        toks, e = args
        xb = x_pad[toks]
        hdn = jax.nn.silu(xb @ w1[e]) * (xb @ w3[e])
        return hdn @ w2[e]

    yb = lax.map(expert_block, (slot_tok.reshape(nb, MOE_BLOCK), blk_exp))
    yb = yb.reshape(n_slots, D) * slot_w[:, None]
    return jnp.zeros((n + 1, D), F32).at[slot_tok].add(yb)[:n]


N_PROMPT = BATCH * SEQ
N_SAMPLE = DEC_BATCH * DEC_SEQ
N_TOK = N_PROMPT + N_SAMPLE


def run_trunk(x_prompt, x_sample, p, past):
    h = jnp.concatenate([x_prompt.reshape(N_PROMPT, D_MODEL), x_sample.reshape(N_SAMPLE, D_MODEL)], axis=0)
    o_nsa, o_gla, new_p, new_s = mixer_a(h, p, past)
    h, hn, route, counts = post_mixer(h, [o_nsa, o_gla], p['a_w_out'][0], p['norm_ffn'][0], p['m_w_rg'][0],
                                      p['m_b_rg'][0], p['m_w_re'][0], p['m_b_re'][0])
    yg = moe_experts(hn, route, counts, p['m_w1'][0], p['m_w3'][0], p['m_w2'][0])
    h, u = combine_proj(h, yg, route, p['norm_mix'][1], p['c_w_in'][0])
    lru_w = (p['c_conv_w'][0], p['c_conv_b'][0], p['c_w_a'][0], p['c_b_a'][0], p['c_w_x'][0], p['c_b_x'][0],
             p['c_lam'][0])
    up = u[:N_PROMPT].reshape(BATCH, SEQ, 2 * D_RNN)
    us = u[N_PROMPT:].reshape(DEC_BATCH, DEC_SEQ, 2 * D_RNN)
    y_p, lru_p = lru_seq(up, *lru_w)
    y_s, lru_s = lru_step(jnp.swapaxes(us, 0, 1), past['state_conv'][0], past['state_lru'][0], *lru_w)
    conv_p = up[:, SEQ - (CONV_W - 1):, D_RNN:]
    conv_s = us[:, DEC_SEQ - (CONV_W - 1):, D_RNN:]
    mix_in = jnp.concatenate([y_p.reshape(N_PROMPT, D_RNN),
                              jnp.swapaxes(y_s, 0, 1).reshape(N_SAMPLE, D_RNN)], axis=0)
    h, hn, route, counts = post_mixer(h, [mix_in], p['c_w_out'][0], p['norm_ffn'][1], p['m_w_rg'][1],
                                      p['m_b_rg'][1], p['m_w_re'][1], p['m_b_re'][1])
    yg = moe_experts(hn, route, counts, p['m_w1'][1], p['m_w3'][1], p['m_w2'][1])
    y = combine_norm(h, yg, route, p['norm_final'])
    y_prompt = y[:N_PROMPT].reshape(BATCH, SEQ, D_MODEL)
    y_sample = y[N_PROMPT:].reshape(DEC_BATCH, DEC_SEQ, D_MODEL)
    return (y_prompt, y_sample), new_p + (lru_p, conv_p), new_s + (lru_s, conv_s)


def kernel(x_prompt, x_sample, cache_cmp_kv, cache_slc_kv, cache_win_kv, state_gla, state_lru, state_conv,
           page_table, norm_mix, norm_ffn, norm_final, a_w_in, a_cmp_pe, a_cmp_w, a_gla_wa2, a_gla_ba,
           a_gla_norm, a_w_out, c_w_in, c_conv_w, c_conv_b, c_w_a, c_b_a, c_w_x, c_b_x, c_lam, c_w_out,
           m_w_rg, m_b_rg, m_w_re, m_b_re, m_w1, m_w3, m_w2):
    p = {'norm_mix': norm_mix, 'norm_ffn': norm_ffn, 'norm_final': norm_final,
         'a_w_in': a_w_in, 'a_cmp_pe': a_cmp_pe, 'a_cmp_w': a_cmp_w, 'a_gla_wa2': a_gla_wa2,
         'a_gla_ba': a_gla_ba, 'a_gla_norm': a_gla_norm, 'a_w_out': a_w_out,
         'c_w_in': c_w_in, 'c_conv_w': c_conv_w, 'c_conv_b': c_conv_b, 'c_w_a': c_w_a, 'c_b_a': c_b_a,
         'c_w_x': c_w_x, 'c_b_x': c_b_x, 'c_lam': c_lam, 'c_w_out': c_w_out,
         'm_w_rg': m_w_rg, 'm_b_rg': m_b_rg, 'm_w_re': m_w_re, 'm_b_re': m_b_re,
         'm_w1': m_w1, 'm_w3': m_w3, 'm_w2': m_w2}
    past = {'cache_cmp_kv': cache_cmp_kv, 'cache_slc_kv': cache_slc_kv, 'cache_win_kv': cache_win_kv,
            'state_gla': state_gla, 'state_lru': state_lru, 'state_conv': state_conv,
            'page_table': page_table}
    (y_p, y_s), sp, ss = run_trunk(x_prompt, x_sample, p, past)
    outs = [y_p, y_s]
    for a, b in zip(sp, ss):
        outs += [a[None], b[None]]
    return tuple(outs)
```

```python
import functools
import jax, jax.numpy as jnp
from jax import lax
import numpy as np
from jax.experimental import pallas as pl
from jax.experimental.pallas import tpu as pltpu

D_MODEL = 1024
BATCH = 2
SEQ = 8192
DEC_BATCH = 128
DEC_SEQ = 4
PAST_LEN = 2048
PAGE_SIZE = 128
EPS = 1e-6
NSA_HEADS = 8
NSA_KV = 2
NSA_REP = NSA_HEADS // NSA_KV
HD = 64
CMP_BLOCK = 32
SLC_BLOCK = 64
SLC_TOPK = 16
WINDOW = 512
Q_BLOCK = 128
ROPE_DIM = HD // 4
ROPE_THETA = 500000.0
GLA_HEADS = 4
GLA_DK = 64
GLA_DV = 128
GLA_LOWRANK = 16
GLA_TAU = 16.0
D_RNN = 1280
LRU_BLOCKS = 10
LRU_BW = D_RNN // LRU_BLOCKS
CONV_W = 4
LRU_C = 8.0
N_GROUPS = 4
EXP_PER_GROUP = 8
N_EXPERTS = N_GROUPS * EXP_PER_GROUP
E_HID = 512
TOP_K_IN_GROUP = 2
NSA_Q_W = NSA_HEADS * HD
NSA_KV_W = 2 * NSA_KV * HD
GLA_K_W = GLA_HEADS * GLA_DK
GLA_V_W = GLA_HEADS * GLA_DV
A_SIZES = (NSA_Q_W, NSA_KV_W, NSA_KV_W, NSA_KV_W, 3 * NSA_HEADS, GLA_K_W, GLA_K_W, GLA_V_W, GLA_LOWRANK, GLA_V_W)
N_PROMPT = BATCH * SEQ
N_SAMPLE = DEC_BATCH * DEC_SEQ
N_TOK = N_PROMPT + N_SAMPLE
N_PAGES = PAST_LEN // PAGE_SIZE

F32 = jnp.float32
BF16 = jnp.bfloat16
VMEM_LIMIT_BYTES = 56 * 1024 * 1024
LANES = 128
SUBLANES = 8
ROW_TILE = 512


def _params(n_axes):
    return pltpu.CompilerParams(dimension_semantics=("arbitrary",) * n_axes, vmem_limit_bytes=VMEM_LIMIT_BYTES)


def _full(shape):
    return pl.BlockSpec(shape, lambda *_: (0,) * len(shape))


def _rows(width):
    return pl.BlockSpec((ROW_TILE, width), lambda i: (i, 0))


def _rms(x, g):
    return x * lax.rsqrt(jnp.mean(x * x, axis=-1, keepdims=True) + EPS) * g


def _softplus(x):
    return jnp.maximum(x, 0.0) + jnp.log1p(jnp.exp(-jnp.abs(x)))


def _gelu_tanh(x):
    return x * (0.5 * (1.0 + jnp.tanh(0.7978845608028654 * (x + 0.044715 * (x * x * x)))))


def _nt_dot(a, b):
    return lax.dot_general(a, b, (((1,), (1,)), ((), ())), preferred_element_type=F32)


def _tn_dot(a, b):
    return lax.dot_general(a, b, (((0,), (0,)), ((), ())), preferred_element_type=F32)


A_Q0, A_KVC0, A_KVS0, A_KVW0 = 0, 512, 768, 1024
A_GQ0, A_GK0, A_GV0, A_GG0, A_MISC0 = 1280, 1536, 1792, 2304, 2816
A_COLS = A_MISC0 + LANES
MISC_LR0 = 3 * NSA_HEADS
KV_HALF = NSA_KV * HD


def _rope_lanes(x, cos_t, sin_lo, sin_hi):
    reps = x.shape[1] // LANES
    tile = (lambda t: jnp.concatenate([t] * reps, axis=1)) if reps > 1 else (lambda t: t)
    w = x.shape[1]
    half = ROPE_DIM // 2
    return x * tile(cos_t) + pltpu.roll(x, half, 1) * tile(sin_hi) + pltpu.roll(x, w - half, 1) * tile(sin_lo)


def _inproj_a_body(h_ref, g_ref, w_ref, wa2_ref, ba_ref, cos_ref, slo_ref, shi_ref,
                   q_ref, kvc_ref, kvs_ref, kvw_ref, ks_ref, vst_ref, kw_ref, vwt_ref,
                   gates_ref, gq_ref, gk_ref, gv_ref, la_ref, sgg_ref):
    y = _rms(h_ref[...], g_ref[...]).astype(BF16)
    proj = lambda a, b: jnp.dot(y, w_ref[:, a:b], preferred_element_type=F32)
    cos_t, sin_lo, sin_hi = cos_ref[...], slo_ref[...], shi_ref[...]
    q_ref[...] = (_rope_lanes(proj(A_Q0, A_KVC0), cos_t, sin_lo, sin_hi) * (HD ** -0.5)).astype(BF16)
    kvc_ref[...] = proj(A_KVC0, A_KVS0)
    for a0, kv_ref, k_ref, vt_ref in ((A_KVS0, kvs_ref, ks_ref, vst_ref), (A_KVW0, kvw_ref, kw_ref, vwt_ref)):
        kv = proj(a0, a0 + 2 * KV_HALF)
        k = _rope_lanes(kv[:, :KV_HALF], cos_t, sin_lo, sin_hi)
        v = kv[:, KV_HALF:]
        kv_ref[:, :KV_HALF] = k
        kv_ref[:, KV_HALF:] = v
        k_ref[...] = k.astype(BF16)
        vt_ref[...] = v.T.astype(BF16)
    misc = proj(A_MISC0, A_COLS)
    gates_ref[...] = jax.nn.sigmoid(misc)
    z = jnp.dot(misc.astype(BF16), wa2_ref[...], preferred_element_type=F32) + ba_ref[...]
    la_ref[...] = -_softplus(-z) * (1.0 / GLA_TAU)
    gq_ref[...] = proj(A_GQ0, A_GK0) * (GLA_DK ** -0.5)
    gk_ref[...] = proj(A_GK0, A_GV0)
    gv_ref[...] = proj(A_GV0, A_GG0).astype(BF16)
    gg = proj(A_GG0, A_MISC0)
    sgg_ref[...] = gg * jax.nn.sigmoid(gg)


def _rope_tables(pos):
    half = ROPE_DIM // 2
    inv = 1.0 / (ROPE_THETA ** (jnp.arange(0, ROPE_DIM, 2, dtype=F32) / ROPE_DIM))
    ang = pos.astype(F32)[:, None] * inv[None, :]
    cos, sin = jnp.cos(ang), jnp.sin(ang)
    n = pos.shape[0]
    one = jnp.ones((n, HD - ROPE_DIM), F32)
    zero = jnp.zeros((n, HD - ROPE_DIM), F32)
    zh = jnp.zeros((n, half), F32)
    seg = lambda a, b, rest: jnp.concatenate([a, b, rest] * (LANES // HD), axis=1)
    return seg(cos, cos, one), seg(-sin, zh, zero), seg(zh, sin, zero)


def inproj_a(h, g, w_in, wa2, ba, pos):
    n = h.shape[0]
    zpad = jnp.zeros((D_MODEL, LANES - 3 * NSA_HEADS - GLA_LOWRANK), F32)
    o = np.cumsum((0,) + A_SIZES)
    w = jnp.concatenate([w_in[:, o[0]:o[4]], w_in[:, o[5]:o[8]], w_in[:, o[9]:o[10]],
                         w_in[:, o[4]:o[5]], w_in[:, o[8]:o[9]], zpad], axis=1).astype(BF16)
    wa2p = jnp.zeros((LANES, GLA_K_W), F32).at[MISC_LR0:MISC_LR0 + GLA_LOWRANK].set(wa2).astype(BF16)
    cols = pl.BlockSpec((KV_HALF, ROW_TILE), lambda i: (0, i))
    outs = [(_rows(NSA_Q_W), (n, NSA_Q_W), BF16), (_rows(NSA_KV_W), (n, NSA_KV_W), F32),
            (_rows(NSA_KV_W), (n, NSA_KV_W), F32), (_rows(NSA_KV_W), (n, NSA_KV_W), F32),
            (_rows(KV_HALF), (n, KV_HALF), BF16), (cols, (KV_HALF, n), BF16),
            (_rows(KV_HALF), (n, KV_HALF), BF16), (cols, (KV_HALF, n), BF16),
            (_rows(LANES), (n, LANES), F32), (_rows(GLA_K_W), (n, GLA_K_W), F32),
            (_rows(GLA_K_W), (n, GLA_K_W), F32), (_rows(GLA_V_W), (n, GLA_V_W), BF16),
            (_rows(GLA_K_W), (n, GLA_K_W), F32), (_rows(GLA_V_W), (n, GLA_V_W), F32)]
    return pl.pallas_call(
        _inproj_a_body,
        grid=(n // ROW_TILE,),
        in_specs=[_rows(D_MODEL), _full((1, D_MODEL)), _full((D_MODEL, A_COLS)), _full((LANES, GLA_K_W)),
                  _full((1, GLA_K_W)), _rows(LANES), _rows(LANES), _rows(LANES)],
        out_specs=[s for s, _, _ in outs],
        out_shape=[jax.ShapeDtypeStruct(shape, dt) for _, shape, dt in outs],
        compiler_params=_params(1),
        name="inproj_a",
    )(h, g.reshape(1, D_MODEL), w, wa2p, ba.reshape(1, GLA_K_W), *_rope_tables(pos))


CMP_ROW = CMP_BLOCK * NSA_KV_W
CMP_TILE = 256


def _cmp_body(x_ref, pe_ref, w_ref, o_ref):
    x = (x_ref[...] + pe_ref[...]).astype(BF16)
    o_ref[...] = jnp.dot(x, w_ref[...], preferred_element_type=F32)


def cmp_blocks(x, pe, w_cmp):
    rows = x.shape[0]
    pe_flat = jnp.broadcast_to(pe[:, :, None, :], (CMP_BLOCK, 2, NSA_KV, HD)).reshape(1, CMP_ROW)
    w_full = jnp.einsum('lcde,cx,gy->lcgdxye', w_cmp, jnp.eye(2, dtype=F32), jnp.eye(NSA_KV, dtype=F32))
    w_full = w_full.reshape(CMP_ROW, NSA_KV_W).astype(BF16)
    return pl.pallas_call(
        _cmp_body,
        grid=(rows // CMP_TILE,),
        in_specs=[pl.BlockSpec((CMP_TILE, CMP_ROW), lambda i: (i, 0)), _full((1, CMP_ROW)),
                  _full((CMP_ROW, NSA_KV_W))],
        out_specs=pl.BlockSpec((CMP_TILE, NSA_KV_W), lambda i: (i, 0)),
        out_shape=jax.ShapeDtypeStruct((rows, NSA_KV_W), F32),
        compiler_params=_params(1),
        name="cmp_blocks",
    )(x, pe_flat, w_full)


KEY_TILE = 256
NEG_BIG = -1e30
N_CMP_PROMPT = SEQ // CMP_BLOCK
N_SLC_PROMPT = SEQ // SLC_BLOCK
SLC_SHIFT = SLC_BLOCK.bit_length() - 1
N_CMP_DEC = PAST_LEN // CMP_BLOCK
N_SLC_DEC = -(-(PAST_LEN + DEC_SEQ) // SLC_BLOCK)
N_SLC_DEC_PAD = -(-N_SLC_DEC // LANES) * LANES
DEC_COLS_PER_GROUP = NSA_REP * DEC_SEQ


def _tile_cols(x, reps):
    return jnp.concatenate([x] * reps, axis=1) if reps > 1 else x


def _even_first_cmp_end(n_cmp):
    j = lax.broadcasted_iota(jnp.int32, (n_cmp, 1), 0)
    blk = jnp.where(j < n_cmp // 2, 2 * j, 2 * (j - n_cmp // 2) + 1)
    return (blk + 1) * CMP_BLOCK - 1


def _softmax_cols(s, mask):
    s = jnp.where(mask, s, -jnp.inf)
    m = jnp.max(s, axis=0, keepdims=True)
    m = jnp.where(m > -jnp.inf, m, 0.0)
    e = jnp.where(mask, jnp.exp(s - m), 0.0)
    return e / jnp.maximum(jnp.sum(e, axis=0, keepdims=True), 1e-30)


def _select_cols(p_slc, qpos, n_top):
    ns = p_slc.shape[0]
    blk = lax.broadcasted_iota(jnp.int32, p_slc.shape, 0)
    cur = qpos >> SLC_SHIFT
    forced = (blk == 0) | (blk == cur) | (blk == cur - 1)
    score = jnp.where(forced, jnp.inf, p_slc)
    score = jnp.where(blk <= cur, score, -jnp.inf)
    sel = jnp.zeros(p_slc.shape, F32)
    for _ in range(n_top):
        m = jnp.max(score, axis=0, keepdims=True)
        idx = jnp.min(jnp.where(score == m, blk, ns), axis=0, keepdims=True)
        hit = blk == idx
        sel = jnp.where(hit & (m > -jnp.inf), 1.0, sel)
        score = jnp.where(hit, -jnp.inf, score)
    return sel


def _flash_init(m_sc, l_sc, acc_sc):
    m_sc[...] = jnp.full(m_sc.shape, NEG_BIG, F32)
    l_sc[...] = jnp.zeros(l_sc.shape, F32)
    acc_sc[...] = jnp.zeros(acc_sc.shape, F32)


def _flash_cols(k, qt, mask, pv, m_sc, l_sc, acc_sc):
    s = jnp.where(mask, jnp.dot(k, qt, preferred_element_type=F32), NEG_BIG)
    m_old = m_sc[...]
    m_new = jnp.maximum(m_old, jnp.max(s, axis=0, keepdims=True))
    alpha = jnp.exp(m_old - m_new)
    p = jnp.where(mask, jnp.exp(s - m_new), 0.0)
    l_sc[...] = alpha * l_sc[...] + jnp.sum(p, axis=0, keepdims=True)
    acc_sc[...] = alpha * acc_sc[...] + pv(p.astype(BF16))
    m_sc[...] = m_new


def _flash_out(l_sc, acc_sc):
    return acc_sc[...] / jnp.maximum(l_sc[...], 1e-30)


def _pv_split(vt, p):
    c2 = p.shape[1] // 2
    return jnp.concatenate([jnp.dot(vt[:HD], p[:, :c2], preferred_element_type=F32),
                            jnp.dot(vt[HD:], p[:, c2:], preferred_element_type=F32)], axis=1)


def _nsa_prompt_body(q_ref, gates_ref, ckv_ref, cos_ref, slo_ref, shi_ref, ks_ref, vst_ref, kw_ref, vwt_ref,
                     o_ref, m_sc, l_sc, acc_sc):
    i = pl.program_id(1)
    nq = Q_BLOCK
    cols = NSA_HEADS * nq
    qpos = i * nq + lax.broadcasted_iota(jnp.int32, (1, nq), 1)
    q = q_ref[...].astype(F32)
    pairs = [q[:, j * LANES:(j + 1) * LANES].T for j in range(NSA_HEADS // 2)]
    zero = jnp.zeros((HD, cols // 2), F32)
    qt_g = [jnp.concatenate([pairs[2 * g][:HD], pairs[2 * g][HD:], pairs[2 * g + 1][:HD], pairs[2 * g + 1][HD:]],
                            axis=1) for g in range(NSA_KV)]
    qt = jnp.concatenate([jnp.concatenate([qt_g[0], zero], axis=1),
                          jnp.concatenate([zero, qt_g[1]], axis=1)], axis=0).astype(BF16)
    ckv = ckv_ref[0]
    ck = _rope_lanes(ckv[:, :KV_HALF], cos_ref[...], slo_ref[...], shi_ref[...]).astype(BF16)
    cvt = ckv[:, KV_HALF:].T.astype(BF16)
    c_mask = _even_first_cmp_end(N_CMP_PROMPT) <= qpos
    p = _softmax_cols(jnp.dot(ck, qt, preferred_element_type=F32), _tile_cols(c_mask, NSA_HEADS))
    o_c = _pv_split(cvt, p.astype(BF16))
    sel = []
    for g in range(NSA_KV):
        c0 = g * NSA_REP * nq
        p_grp = p[:, c0:c0 + nq]
        for r in range(1, NSA_REP):
            p_grp = p_grp + p[:, c0 + r * nq:c0 + (r + 1) * nq]
        p_slc = p_grp[:N_CMP_PROMPT // 2] + p_grp[N_CMP_PROMPT // 2:]
        sel.append(_select_cols(p_slc, qpos, SLC_TOPK))
    sel = jnp.concatenate(sel, axis=1).astype(BF16)
    key_row = lax.broadcasted_iota(jnp.int32, (KEY_TILE, 1), 0)
    key_blk = lax.broadcasted_iota(jnp.int32, (KEY_TILE, N_SLC_PROMPT), 0)
    blk_col = lax.broadcasted_iota(jnp.int32, (KEY_TILE, N_SLC_PROMPT), 1)
    _flash_init(m_sc, l_sc, acc_sc)

    def slc_tile(kt, _):
        k0 = pl.multiple_of(kt * KEY_TILE, KEY_TILE)
        expand = jnp.where(((k0 + key_blk) >> SLC_SHIFT) == blk_col, 1.0, 0.0).astype(BF16)
        chosen = jnp.dot(expand, sel, preferred_element_type=F32) > 0.5
        causal = (k0 + key_row) <= qpos
        mask = jnp.concatenate([_tile_cols(chosen[:, :nq] & causal, NSA_REP),
                                _tile_cols(chosen[:, nq:] & causal, NSA_REP)], axis=1)
        vt = vst_ref[:, pl.ds(k0, KEY_TILE)]
        _flash_cols(ks_ref[pl.ds(k0, KEY_TILE), :], qt, mask, lambda pb: _pv_split(vt, pb), m_sc, l_sc, acc_sc)
        return 0

    lax.fori_loop(0, i // 2 + 1, slc_tile, 0)
    o_s = _flash_out(l_sc, acc_sc)
    _flash_init(m_sc, l_sc, acc_sc)

    def win_tile(kt, _):
        k0 = pl.multiple_of(kt * KEY_TILE, KEY_TILE)
        d = qpos - (k0 + key_row)
        vt = vwt_ref[:, pl.ds(k0, KEY_TILE)]
        _flash_cols(kw_ref[pl.ds(k0, KEY_TILE), :], qt, _tile_cols((d >= 0) & (d < WINDOW), NSA_HEADS),
                    lambda pb: _pv_split(vt, pb), m_sc, l_sc, acc_sc)
        return 0

    lax.fori_loop(jnp.maximum(i - WINDOW // nq, 0) // 2, i // 2 + 1, win_tile, 0)
    o_w = _flash_out(l_sc, acc_sc)
    gates_t = gates_ref[...].T
    merged = []
    for hd in range(NSA_HEADS):
        cs = slice(hd * nq, (hd + 1) * nq)
        gate = lambda br: gates_t[3 * hd + br:3 * hd + br + 1]
        merged.append(gate(0) * o_c[:, cs] + gate(1) * o_s[:, cs] + gate(2) * o_w[:, cs])
    for j in range(NSA_HEADS // 2):
        pair = jnp.concatenate([merged[2 * j], merged[2 * j + 1]], axis=0).T
        o_ref[:, j * LANES:(j + 1) * LANES] = pair.astype(BF16)


def nsa_prompt_attn(q, gates, ckv, ks, vst, kw, vwt):
    nqb = SEQ // Q_BLOCK
    tok = lambda wd: pl.BlockSpec((Q_BLOCK, wd), lambda b, i: (b * nqb + i, 0))
    seq_rows = pl.BlockSpec((SEQ, KV_HALF), lambda b, i: (b, 0))
    seq_cols = pl.BlockSpec((KV_HALF, SEQ), lambda b, i: (0, b))
    n_cmp = N_CMP_PROMPT
    c_blk = jnp.concatenate([jnp.arange(0, n_cmp, 2), jnp.arange(1, n_cmp, 2)]).astype(jnp.int32)
    cols = NSA_HEADS * Q_BLOCK
    return pl.pallas_call(
        _nsa_prompt_body,
        grid=(BATCH, nqb),
        in_specs=[tok(NSA_Q_W), tok(LANES), pl.BlockSpec((1, n_cmp, NSA_KV_W), lambda b, i: (b, 0, 0)),
                  _full((n_cmp, LANES)), _full((n_cmp, LANES)), _full((n_cmp, LANES)),
                  seq_rows, seq_cols, seq_rows, seq_cols],
        out_specs=tok(NSA_Q_W),
        out_shape=jax.ShapeDtypeStruct((N_PROMPT, NSA_Q_W), BF16),
        scratch_shapes=[pltpu.VMEM((1, cols), F32), pltpu.VMEM((1, cols), F32), pltpu.VMEM((HD, cols), F32)],
        compiler_params=_params(2),
        name="nsa_prompt",
    )(q, gates, ckv, *_rope_tables((c_blk + 1) * CMP_BLOCK - 1), ks, vst, kw, vwt)


def _nsa_decode_body(pt_ref, qt_ref, gate_ref, ckv_ref, cos_ref, slo_ref, shi_ref, hsum_ref, *refs):
    page_refs = refs[:N_PAGES]
    win_ref, kvs_new_ref, kvw_new_ref, o_ref, win_out_ref, m_sc, l_sc, acc_sc = refs[N_PAGES:]
    qt = qt_ref[0]
    lane = lax.broadcasted_iota(jnp.int32, (1, LANES), 1)
    qpos = PAST_LEN + (lane & (DEC_SEQ - 1))
    group0 = lane < DEC_COLS_PER_GROUP
    ckv = ckv_ref[0]
    ck = _rope_lanes(ckv[:, :KV_HALF], cos_ref[...], slo_ref[...], shi_ref[...]).astype(BF16)
    c_mask = _even_first_cmp_end(N_CMP_DEC) <= qpos
    p = _softmax_cols(jnp.dot(ck, qt, preferred_element_type=F32), c_mask)
    o_c = _tn_dot(ckv[:, KV_HALF:].astype(BF16), p.astype(BF16))
    p_grp = jnp.dot(p, hsum_ref[...], preferred_element_type=F32, precision=lax.Precision.HIGHEST)
    p_slc = jnp.concatenate([p_grp[:N_CMP_DEC // 2] + p_grp[N_CMP_DEC // 2:],
                             jnp.zeros((N_SLC_DEC_PAD - N_CMP_DEC // 2, LANES), F32)], axis=0)
    sel = _select_cols(p_slc, qpos, SLC_TOPK)

    def new_rows(ref):
        row = ref[0]
        kv = jnp.concatenate([row[:, t * NSA_KV_W:(t + 1) * NSA_KV_W] for t in range(DEC_SEQ)], axis=0)
        return jnp.concatenate([kv, jnp.zeros((SUBLANES - DEC_SEQ, NSA_KV_W), F32)], axis=0)

    new_row = lax.broadcasted_iota(jnp.int32, (SUBLANES, 1), 0)
    new_pos = PAST_LEN + new_row
    new_valid = new_row < DEC_SEQ
    _flash_init(m_sc, l_sc, acc_sc)
    kv_old = jnp.concatenate([r[0] for r in page_refs], axis=0)
    key_blk = lax.broadcasted_iota(jnp.int32, (PAST_LEN, N_SLC_DEC_PAD), 0) >> SLC_SHIFT
    blk_col = lax.broadcasted_iota(jnp.int32, (PAST_LEN, N_SLC_DEC_PAD), 1)
    chosen = jnp.dot(jnp.where(key_blk == blk_col, 1.0, 0.0).astype(BF16), sel.astype(BF16),
                     preferred_element_type=F32) > 0.5
    old_pos = lax.broadcasted_iota(jnp.int32, (PAST_LEN, 1), 0)
    v_old = kv_old[:, KV_HALF:].astype(BF16)
    _flash_cols(kv_old[:, :KV_HALF].astype(BF16), qt, chosen & (old_pos <= qpos),
                lambda pb: _tn_dot(v_old, pb), m_sc, l_sc, acc_sc)
    kv_new = new_rows(kvs_new_ref)
    sel_new = sel[(PAST_LEN >> SLC_SHIFT):(PAST_LEN >> SLC_SHIFT) + 1] > 0.5
    v_new = kv_new[:, KV_HALF:].astype(BF16)
    _flash_cols(kv_new[:, :KV_HALF].astype(BF16), qt, sel_new & new_valid & (new_pos <= qpos),
                lambda pb: _tn_dot(v_new, pb), m_sc, l_sc, acc_sc)
    o_s = _flash_out(l_sc, acc_sc)
    _flash_init(m_sc, l_sc, acc_sc)
    n_buf = min(WINDOW, PAST_LEN)
    win = win_ref[0]
    d = qpos - (PAST_LEN - n_buf + lax.broadcasted_iota(jnp.int32, (n_buf, 1), 0))
    v_win = win[:, KV_HALF:].astype(BF16)
    _flash_cols(win[:, :KV_HALF].astype(BF16), qt, (d >= 0) & (d < WINDOW),
                lambda pb: _tn_dot(v_win, pb), m_sc, l_sc, acc_sc)
    kw_new = new_rows(kvw_new_ref)
    d = qpos - new_pos
    vw_new = kw_new[:, KV_HALF:].astype(BF16)
    _flash_cols(kw_new[:, :KV_HALF].astype(BF16), qt, new_valid & (d >= 0) & (d < WINDOW),
                lambda pb: _tn_dot(vw_new, pb), m_sc, l_sc, acc_sc)
    o_w = _flash_out(l_sc, acc_sc)
    g = gate_ref[0]
    o = g[0:1] * o_c + g[1:2] * o_s + g[2:3] * o_w
    o_ref[0] = jnp.where(group0, o[:HD], o[HD:])
    win_out_ref[0, 0:n_buf - DEC_SEQ, :] = win_ref[0, DEC_SEQ:n_buf, :]
    win_out_ref[0, n_buf - DEC_SEQ:n_buf, :] = kw_new[:DEC_SEQ]


def nsa_decode_attn(qt, gate_rows, ckv, slc_pool, win_buf, kvs_new, kvw_new, page_table):
    n_buf = win_buf.shape[1]
    per_b = lambda *shape: pl.BlockSpec((1,) + shape, lambda b, pt: (b,) + (0,) * len(shape))
    const = lambda *shape: pl.BlockSpec(shape, lambda b, pt: (0,) * len(shape))
    page = lambda j: pl.BlockSpec((1, PAGE_SIZE, NSA_KV_W), lambda b, pt: (pt[b, j], 0, 0))
    c_blk = jnp.concatenate([jnp.arange(0, N_CMP_DEC, 2), jnp.arange(1, N_CMP_DEC, 2)]).astype(jnp.int32)
    col = jnp.arange(LANES)
    used = col < NSA_KV * DEC_COLS_PER_GROUP
    same = (col[:, None] // DEC_COLS_PER_GROUP == col[None, :] // DEC_COLS_PER_GROUP) & \
           (col[:, None] % DEC_SEQ == col[None, :] % DEC_SEQ) & used[:, None] & used[None, :]
    return pl.pallas_call(
        _nsa_decode_body,
        grid_spec=pltpu.PrefetchScalarGridSpec(
            num_scalar_prefetch=1,
            grid=(DEC_BATCH,),
            in_specs=[per_b(LANES, LANES), per_b(3, LANES), per_b(N_CMP_DEC, NSA_KV_W),
                      const(N_CMP_DEC, LANES), const(N_CMP_DEC, LANES), const(N_CMP_DEC, LANES),
                      const(LANES, LANES)] + [page(j) for j in range(N_PAGES)]
                     + [per_b(n_buf, NSA_KV_W), per_b(1, DEC_SEQ * NSA_KV_W), per_b(1, DEC_SEQ * NSA_KV_W)],
            out_specs=[per_b(HD, LANES), per_b(n_buf, NSA_KV_W)],
            scratch_shapes=[pltpu.VMEM((1, LANES), F32), pltpu.VMEM((1, LANES), F32),
                            pltpu.VMEM((LANES, LANES), F32)],
        ),
        out_shape=[jax.ShapeDtypeStruct((DEC_BATCH, HD, LANES), F32),
                   jax.ShapeDtypeStruct((DEC_BATCH, n_buf, NSA_KV_W), F32)],
        compiler_params=_params(1),
        name="nsa_decode",
    )(page_table, qt, gate_rows, ckv, *_rope_tables((c_blk + 1) * CMP_BLOCK - 1), same.astype(F32),
      *([slc_pool] * N_PAGES), win_buf, kvs_new, kvw_new)


GLA_PAIRS = GLA_HEADS // 2
GLA_ROWS = 128
GLA_SUB = 16
GLA_STEP_SEQS = 8


def _gla_rows(q, k, la, v, sgg, gnorm, st_ref, sub):
    R = q.shape[0]
    row = lax.broadcasted_iota(jnp.int32, (R, GLA_K_W), 0)
    rin = row % sub
    cum = la
    d = 1
    while d < sub:
        cum = cum + jnp.where(rin >= d, pltpu.roll(cum, d, 0), 0.0)
        d *= 2
    lane = lax.broadcasted_iota(jnp.int32, (sub, LANES), 1)
    lo = lane < GLA_DK
    rsub = lax.broadcasted_iota(jnp.int32, (sub, LANES), 0)
    out_rows = []
    for c in range(R // sub):
        rs = slice(c * sub, (c + 1) * sub)
        cum_c = cum[rs]
        last = cum_c[sub - 1:sub]
        qe = q[rs] * jnp.exp(cum_c)
        kdec = k[rs] * jnp.exp(last - cum_c)
        v_c = v[rs]
        heads = []
        for pr in range(GLA_PAIRS):
            ls = slice(pr * LANES, (pr + 1) * LANES)
            st = st_ref[pr]
            st_b = st.astype(BF16)
            qe_p, kd_p, q_p, k_p, cum_p = qe[:, ls], kdec[:, ls], q[rs, ls], k[rs, ls], cum_c[:, ls]
            v_pair = [v_c[:, (2 * pr + hh) * GLA_DV:(2 * pr + hh + 1) * GLA_DV] for hh in range(2)]
            upd = jnp.zeros((GLA_DV, LANES), F32)
            o_pair = []
            for hh in range(2):
                keep = lo if hh == 0 else jnp.logical_not(lo)
                o_pair.append(_nt_dot(jnp.where(keep, qe_p, 0.0).astype(BF16), st_b))
                upd = upd + _tn_dot(v_pair[hh].astype(BF16), jnp.where(keep, kd_p, 0.0).astype(BF16))
            for j in range(sub):
                dj = jnp.where(rsub >= j, jnp.exp(cum_p - cum_p[j:j + 1]), 0.0)
                w = q_p * k_p[j:j + 1] * dj
                a_lo = jnp.sum(jnp.where(lo, w, 0.0), axis=-1, keepdims=True)
                a_hi = jnp.sum(jnp.where(lo, 0.0, w), axis=-1, keepdims=True)
                o_pair[0] = o_pair[0] + a_lo * v_pair[0][j:j + 1]
                o_pair[1] = o_pair[1] + a_hi * v_pair[1][j:j + 1]
            st_ref[pr] = st * jnp.exp(last[:, ls]) + upd
            heads += o_pair
        out_rows.append(jnp.concatenate([_rms(x, gnorm) for x in heads], axis=1))
    return jnp.concatenate(out_rows, axis=0) * sgg


def _gla_seq_body(q_ref, k_ref, la_ref, v_ref, sgg_ref, gn_ref, o_ref, st_out_ref, st_sc):
    @pl.when(pl.program_id(1) == 0)
    def _():
        st_sc[...] = jnp.zeros(st_sc.shape, F32)

    o = _gla_rows(q_ref[...], k_ref[...], la_ref[...], v_ref[...].astype(F32), sgg_ref[...], gn_ref[...],
                  st_sc, GLA_SUB)
    o_ref[...] = o.astype(BF16)
    st_out_ref[0] = st_sc[...]


def gla_seq(q, k, la, v, sgg, gnorm):
    nt = SEQ // GLA_ROWS
    rows = lambda wd: pl.BlockSpec((GLA_ROWS, wd), lambda b, t: (b * nt + t, 0))
    return pl.pallas_call(
        _gla_seq_body,
        grid=(BATCH, nt),
        in_specs=[rows(GLA_K_W), rows(GLA_K_W), rows(GLA_K_W), rows(GLA_V_W), rows(GLA_V_W),
                  _full((1, GLA_DV))],
        out_specs=[rows(GLA_V_W), pl.BlockSpec((1, GLA_PAIRS, GLA_DV, LANES), lambda b, t: (b, 0, 0, 0))],
        out_shape=[jax.ShapeDtypeStruct((N_PROMPT, GLA_V_W), BF16),
                   jax.ShapeDtypeStruct((BATCH, GLA_PAIRS, GLA_DV, LANES), F32)],
        scratch_shapes=[pltpu.VMEM((GLA_PAIRS, GLA_DV, LANES), F32)],
        compiler_params=_params(2),
        name="gla_seq",
    )(q, k, la, v, sgg, gnorm.reshape(1, GLA_DV))


def _gla_step_body(q_ref, k_ref, la_ref, v_ref, sgg_ref, gn_ref, st_in_ref, o_ref, st_out_ref):
    st_out_ref[...] = st_in_ref[...]
    q, k, la, v, sgg = q_ref[...], k_ref[...], la_ref[...], v_ref[...].astype(F32), sgg_ref[...]
    for j in range(GLA_STEP_SEQS):
        rs = slice(j * DEC_SEQ, (j + 1) * DEC_SEQ)
        o = _gla_rows(q[rs], k[rs], la[rs], v[rs], sgg[rs], gn_ref[...], st_out_ref.at[j], DEC_SEQ)
        o_ref[rs, :] = o.astype(BF16)


def gla_step(q, k, la, v, sgg, gnorm, st_in, row0):
    rows_per = GLA_STEP_SEQS * DEC_SEQ
    blk0 = row0 // rows_per
    rows = lambda wd: pl.BlockSpec((rows_per, wd), lambda i: (blk0 + i, 0))
    st_spec = pl.BlockSpec((GLA_STEP_SEQS, GLA_PAIRS, GLA_DV, LANES), lambda i: (i, 0, 0, 0))
    return pl.pallas_call(
        _gla_step_body,
        grid=(DEC_BATCH // GLA_STEP_SEQS,),
        in_specs=[rows(GLA_K_W), rows(GLA_K_W), rows(GLA_K_W), rows(GLA_V_W), rows(GLA_V_W),
                  _full((1, GLA_DV)), st_spec],
        out_specs=[pl.BlockSpec((rows_per, GLA_V_W), lambda i: (i, 0)), st_spec],
        out_shape=[jax.ShapeDtypeStruct((N_SAMPLE, GLA_V_W), BF16),
                   jax.ShapeDtypeStruct((DEC_BATCH, GLA_PAIRS, GLA_DV, LANES), F32)],
        compiler_params=_params(1),
        name="gla_step",
    )(q, k, la, v, sgg, gnorm.reshape(1, GLA_DV), st_in)


def _gla_state_to_pairs(s):
    B = s.shape[0]
    return s.reshape(B, GLA_PAIRS, 2, GLA_DK, GLA_DV).transpose(0, 1, 4, 2, 3).reshape(B, GLA_PAIRS, GLA_DV, LANES)


def _gla_state_from_pairs(st):
    B = st.shape[0]
    return st.reshape(B, GLA_PAIRS, GLA_DV, 2, GLA_DK).transpose(0, 1, 3, 4, 2).reshape(B, GLA_HEADS, GLA_DK, GLA_DV)


LRU_TIME_TILE = 256


def _lru_gates(xc, wa_ref, ba, wx_ref, bx, lam):
    xcb = xc.astype(BF16)
    r_parts, i_parts = [], []
    for n in range(LRU_BLOCKS):
        xs = xcb[:, n * LRU_BW:(n + 1) * LRU_BW]
        r_parts.append(jnp.dot(xs, wa_ref[n], preferred_element_type=F32))
        i_parts.append(jnp.dot(xs, wx_ref[n], preferred_element_type=F32))
    r = jax.nn.sigmoid(jnp.concatenate(r_parts, axis=-1) + ba)
    i = jax.nn.sigmoid(jnp.concatenate(i_parts, axis=-1) + bx)
    log_a = -LRU_C * r * _softplus(-lam)
    a = jnp.exp(log_a)
    u = jnp.sqrt(-jnp.tanh(log_a) * (a * a + 1.0)) * (i * xc)
    return a, u


def _lru_seq_body(u_ref, cw_ref, cb_ref, wa_ref, ba_ref, wx_ref, bx_ref, lam_ref, y_ref, hT_ref, xp_sc, h_sc):
    tt = LRU_TIME_TILE

    @pl.when(pl.program_id(1) == 0)
    def _():
        xp_sc[0:SUBLANES, :] = jnp.zeros((SUBLANES, D_RNN), F32)
        h_sc[...] = jnp.zeros((1, D_RNN), F32)

    xp_sc[SUBLANES:SUBLANES + tt, :] = u_ref[:, D_RNN:]
    xc = cb_ref[...]
    for w in range(CONV_W):
        off = SUBLANES - (CONV_W - 1) + w
        xc = xc + cw_ref[w:w + 1, :] * xp_sc[off:off + tt, :]
    a, u = _lru_gates(xc, wa_ref, ba_ref[...], wx_ref, bx_ref[...], lam_ref[...])
    row = lax.broadcasted_iota(jnp.int32, (tt, D_RNN), 0)
    d = 1
    while d < tt:
        keep = row >= d
        a_prev = jnp.where(keep, pltpu.roll(a, d, 0), 1.0)
        u_prev = jnp.where(keep, pltpu.roll(u, d, 0), 0.0)
        u = a * u_prev + u
        a = a * a_prev
        d *= 2
    h = a * h_sc[...] + u
    h_sc[...] = h[tt - 1:tt, :]
    hT_ref[0] = h[tt - 1:tt, :]
    y_ref[...] = (_gelu_tanh(u_ref[:, :D_RNN]) * h).astype(BF16)
    xp_sc[0:SUBLANES, :] = xp_sc[tt:tt + SUBLANES, :]


def _lru_weight_args(cw, cb, wa, ba, wx, bx, lam):
    row = lambda v: v.reshape(1, D_RNN)
    return (cw, row(cb), wa.astype(BF16), row(ba), wx.astype(BF16), row(bx), row(lam))


_LRU_WEIGHT_SPECS = [_full((CONV_W, D_RNN)), _full((1, D_RNN)), _full((LRU_BLOCKS, LRU_BW, LRU_BW)),
                     _full((1, D_RNN)), _full((LRU_BLOCKS, LRU_BW, LRU_BW)), _full((1, D_RNN)),
                     _full((1, D_RNN))]


def lru_seq(u, cw, cb, wa, ba, wx, bx, lam):
    tt = LRU_TIME_TILE
    nt = SEQ // tt
    y, hT = pl.pallas_call(
        _lru_seq_body,
        grid=(BATCH, nt),
        in_specs=[pl.BlockSpec((tt, 2 * D_RNN), lambda b, t: (b * nt + t, 0))] + _LRU_WEIGHT_SPECS,
        out_specs=[pl.BlockSpec((tt, D_RNN), lambda b, t: (b * nt + t, 0)),
                   pl.BlockSpec((1, 1, D_RNN), lambda b, t: (b, 0, 0))],
        out_shape=[jax.ShapeDtypeStruct((N_PROMPT, D_RNN), BF16), jax.ShapeDtypeStruct((BATCH, 1, D_RNN), F32)],
        scratch_shapes=[pltpu.VMEM((tt + SUBLANES, D_RNN), F32), pltpu.VMEM((1, D_RNN), F32)],
        compiler_params=_params(2),
        name="lru_seq",
    )(u, *_lru_weight_args(cw, cb, wa, ba, wx, bx, lam))
    return y, hT.reshape(BATCH, D_RNN)


def _lru_step_body(u_ref, cs_ref, h0_ref, cw_ref, cb_ref, wa_ref, ba_ref, wx_ref, bx_ref, lam_ref, y_ref, hT_ref):
    n_t = u_ref.shape[0]
    hist = [cs_ref[:, w, :] for w in range(CONV_W - 1)] + [u_ref[t, :, D_RNN:] for t in range(n_t)]
    h = h0_ref[...]
    for t in range(n_t):
        xc = cb_ref[...]
        for w in range(CONV_W):
            xc = xc + cw_ref[w:w + 1, :] * hist[t + w]
        a, u = _lru_gates(xc, wa_ref, ba_ref[...], wx_ref, bx_ref[...], lam_ref[...])
        h = a * h + u
        y_ref[t] = (_gelu_tanh(u_ref[t, :, :D_RNN]) * h).astype(BF16)
    hT_ref[...] = h


def lru_step(u, conv_state, h0, cw, cb, wa, ba, wx, bx, lam):
    T, B, _ = u.shape
    return pl.pallas_call(
        _lru_step_body,
        out_shape=[jax.ShapeDtypeStruct((T, B, D_RNN), BF16), jax.ShapeDtypeStruct((B, D_RNN), F32)],
        compiler_params=pltpu.CompilerParams(vmem_limit_bytes=VMEM_LIMIT_BYTES),
        name="lru_step",
    )(u, conv_state, h0, *_lru_weight_args(cw, cb, wa, ba, wx, bx, lam))


ROUTE_E1, ROUTE_E2, ROUTE_G1, ROUTE_G2, ROUTE_R1, ROUTE_R2 = range(6)
EXPERT_LANE0 = N_GROUPS


def _lane_pick(val_by_lane):
    rows = next(iter(val_by_lane.values())).shape[0]
    lane = lax.broadcasted_iota(jnp.int32, (rows, LANES), 1)
    out = jnp.zeros((rows, LANES), F32)
    for l, v in val_by_lane.items():
        out = jnp.where(lane == l, v, out)
    return out


def _route_rows(logits, tri_ref, carry_ref):
    rows = logits.shape[0]
    lane = lax.broadcasted_iota(jnp.int32, (rows, LANES), 1)
    neg = -jnp.inf
    gl = jnp.where(lane < N_GROUPS, logits, neg)
    gmax = jnp.max(gl, axis=-1, keepdims=True)
    gtop = jnp.min(jnp.where(gl == gmax, lane, LANES), axis=-1, keepdims=True)
    gsum = jnp.sum(jnp.where(lane < N_GROUPS, jnp.exp(logits - gmax), 0.0), axis=-1, keepdims=True)
    g_w = 1.0 / gsum
    lo = EXPERT_LANE0 + EXP_PER_GROUP * gtop
    el = jnp.where((lane >= lo) & (lane < lo + EXP_PER_GROUP), logits, neg)
    v1 = jnp.max(el, axis=-1, keepdims=True)
    i1 = jnp.min(jnp.where(el == v1, lane, LANES), axis=-1, keepdims=True)
    el2 = jnp.where(lane == i1, neg, el)
    v2 = jnp.max(el2, axis=-1, keepdims=True)
    i2 = jnp.min(jnp.where(el2 == v2, lane, LANES), axis=-1, keepdims=True)
    p2 = jnp.exp(v2 - v1)
    den = 1.0 + p2
    gate1 = (1.0 / den) * g_w
    gate2 = (p2 / den) * g_w
    hit1 = lane == i1
    hit2 = lane == i2
    onehot = jnp.where(hit1 | hit2, 1.0, 0.0)
    before = jnp.dot(tri_ref[...], onehot.astype(BF16), preferred_element_type=F32) + carry_ref[...]
    rank1 = jnp.sum(jnp.where(hit1, before, 0.0), axis=-1, keepdims=True)
    rank2 = jnp.sum(jnp.where(hit2, before, 0.0), axis=-1, keepdims=True)
    carry_ref[...] = carry_ref[...] + jnp.sum(onehot, axis=0, keepdims=True)
    return _lane_pick({ROUTE_E1: (i1 - EXPERT_LANE0).astype(F32), ROUTE_E2: (i2 - EXPERT_LANE0).astype(F32),
                       ROUTE_G1: gate1, ROUTE_G2: gate2, ROUTE_R1: rank1, ROUTE_R2: rank2})


def _post_mixer_body(n_mix, h_ref, *refs):
    m_refs, wo_refs = refs[:n_mix], refs[n_mix:2 * n_mix]
    g_ref, wr_ref, br_ref, tri_ref, h1_ref, hn_ref, route_ref, cnt_ref, carry_sc = refs[2 * n_mix:]

    @pl.when(pl.program_id(0) == 0)
    def _():
        carry_sc[...] = jnp.zeros((1, LANES), F32)

    mix = jnp.dot(m_refs[0][...], wo_refs[0][...], preferred_element_type=F32)
    for m_ref, wo_ref in zip(m_refs[1:], wo_refs[1:]):
        mix = mix + jnp.dot(m_ref[...], wo_ref[...], preferred_element_type=F32)
    h1 = h_ref[...] + mix
    h1_ref[...] = h1
    hn = _rms(h1, g_ref[...]).astype(BF16)
    hn_ref[...] = hn
    logits = jnp.dot(hn, wr_ref[...], preferred_element_type=F32) + br_ref[...]
    route_ref[...] = _route_rows(logits, tri_ref, carry_sc)
    cnt_ref[...] = carry_sc[...]


def post_mixer(h, mix_ins, w_out, g_ffn, w_rg, b_rg, w_re, b_re):
    n = h.shape[0]
    ks = [m.shape[1] for m in mix_ins]
    offs = np.cumsum([0] + ks)
    w_parts = [w_out[offs[j]:offs[j + 1]].astype(BF16) for j in range(len(ks))]
    pad = LANES - N_GROUPS - N_EXPERTS
    wr = jnp.concatenate([w_rg, w_re, jnp.zeros((D_MODEL, pad), F32)], axis=1).astype(BF16)
    br = jnp.concatenate([b_rg, b_re, jnp.zeros((pad,), F32)]).reshape(1, LANES)
    tri = jnp.tril(jnp.ones((ROW_TILE, ROW_TILE), BF16), -1)
    return pl.pallas_call(
        functools.partial(_post_mixer_body, len(ks)),
        grid=(n // ROW_TILE,),
        in_specs=[_rows(D_MODEL)] + [_rows(k) for k in ks] + [_full((k, D_MODEL)) for k in ks]
                 + [_full((1, D_MODEL)), _full((D_MODEL, LANES)), _full((1, LANES)), _full((ROW_TILE, ROW_TILE))],
        out_specs=[_rows(D_MODEL), _rows(D_MODEL), _rows(LANES), _full((1, LANES))],
        out_shape=[jax.ShapeDtypeStruct((n, D_MODEL), F32), jax.ShapeDtypeStruct((n, D_MODEL), BF16),
                   jax.ShapeDtypeStruct((n, LANES), F32), jax.ShapeDtypeStruct((1, LANES), F32)],
        scratch_shapes=[pltpu.VMEM((1, LANES), F32)],
        compiler_params=_params(1),
        name="post_mixer",
    )(h, *mix_ins, *w_parts, g_ffn.reshape(1, D_MODEL), wr, br, tri)


MOE_ROWS = 256


def _ffn_body(be_ref, nb_ref, x_ref, w1_ref, w3_ref, w2_ref, y_ref, w1_sc, w3_sc, w2_sc):
    i = pl.program_id(0)
    new_expert = jnp.logical_or(i == 0, be_ref[i] != be_ref[jnp.maximum(i - 1, 0)])

    @pl.when(jnp.logical_and(new_expert, i < nb_ref[0]))
    def _():
        w1_sc[...] = w1_ref[0].astype(BF16)
        w3_sc[...] = w3_ref[0].astype(BF16)
        w2_sc[...] = w2_ref[0].astype(BF16)

    @pl.when(i < nb_ref[0])
    def _():
        x = x_ref[...]
        a = jnp.dot(x, w1_sc[...], preferred_element_type=F32)
        b = jnp.dot(x, w3_sc[...], preferred_element_type=F32)
        hdn = (a * jax.nn.sigmoid(a) * b).astype(BF16)
        y_ref[...] = jnp.dot(hdn, w2_sc[...], preferred_element_type=F32)

    @pl.when(i >= nb_ref[0])
    def _():
        y_ref[...] = jnp.zeros(y_ref.shape, F32)


def expert_ffn(xs, blk_exp, n_active, w1, w3, w2):
    n_slots = xs.shape[0]
    nb = n_slots // MOE_ROWS
    wmap = lambda i, be, na: (be[i], 0, 0)
    xmap = lambda i, be, na: (jnp.minimum(i, na[0] - 1), 0)
    return pl.pallas_call(
        _ffn_body,
        grid_spec=pltpu.PrefetchScalarGridSpec(
            num_scalar_prefetch=2,
            grid=(nb,),
            in_specs=[pl.BlockSpec((MOE_ROWS, D_MODEL), xmap),
                      pl.BlockSpec((1, D_MODEL, E_HID), wmap),
                      pl.BlockSpec((1, D_MODEL, E_HID), wmap),
                      pl.BlockSpec((1, E_HID, D_MODEL), wmap)],
            out_specs=pl.BlockSpec((MOE_ROWS, D_MODEL), lambda i, be, na: (i, 0)),
            scratch_shapes=[pltpu.VMEM((D_MODEL, E_HID), BF16), pltpu.VMEM((D_MODEL, E_HID), BF16),
                            pltpu.VMEM((E_HID, D_MODEL), BF16)],
        ),
        out_shape=jax.ShapeDtypeStruct((n_slots, D_MODEL), F32),
        compiler_params=_params(1),
        name="expert_ffn",
    )(blk_exp, n_active, xs, w1, w3, w2)


def moe_dispatch(route, counts_row, n):
    e = route[:, ROUTE_E1:ROUTE_E2 + 1].astype(jnp.int32)
    rank = route[:, ROUTE_R1:ROUTE_R2 + 1].astype(jnp.int32)
    counts = counts_row[0, EXPERT_LANE0:EXPERT_LANE0 + N_EXPERTS].astype(jnp.int32)
    padded = ((counts + MOE_ROWS - 1) // MOE_ROWS) * MOE_ROWS
    pad_end = jnp.cumsum(padded)
    pad_start = pad_end - padded
    dest = pad_start[e] + rank
    nb = -(-(n * TOP_K_IN_GROUP) // MOE_ROWS) + N_EXPERTS
    n_slots = nb * MOE_ROWS
    tok = jnp.broadcast_to(jnp.arange(n, dtype=jnp.int32)[:, None], (n, TOP_K_IN_GROUP))
    slot_tok = jnp.zeros((n_slots,), jnp.int32).at[dest.reshape(-1)].set(tok.reshape(-1))
    blk_start = jnp.arange(nb, dtype=jnp.int32) * MOE_ROWS
    blk_exp = jnp.minimum(jnp.searchsorted(pad_end, blk_start, side='right'), N_EXPERTS - 1).astype(jnp.int32)
    n_active = (pad_end[-1] // MOE_ROWS).astype(jnp.int32).reshape(1)
    return slot_tok, dest, blk_exp, n_active


def moe_experts(hn, route, counts_row, w1, w3, w2):
    n = hn.shape[0]
    slot_tok, dest, blk_exp, n_active = moe_dispatch(route, counts_row, n)
    xs = jnp.take(hn, slot_tok, axis=0)
    ys = expert_ffn(xs, blk_exp, n_active, w1, w3, w2)
    return jnp.take(ys, dest.reshape(-1), axis=0).reshape(n, 2 * D_MODEL)


def _combine(h_ref, yg_ref, route_ref):
    lane = lax.broadcasted_iota(jnp.int32, (ROW_TILE, LANES), 1)
    r = route_ref[...]
    g1 = jnp.sum(jnp.where(lane == ROUTE_G1, r, 0.0), axis=-1, keepdims=True)
    g2 = jnp.sum(jnp.where(lane == ROUTE_G2, r, 0.0), axis=-1, keepdims=True)
    return h_ref[...] + (yg_ref[:, :D_MODEL] * g1 + yg_ref[:, D_MODEL:] * g2)


def _combine_proj_body(h_ref, yg_ref, route_ref, g_ref, w_ref, h2_ref, u_ref):
    h2 = _combine(h_ref, yg_ref, route_ref)
    h2_ref[...] = h2
    u_ref[...] = jnp.dot(_rms(h2, g_ref[...]).astype(BF16), w_ref[...], preferred_element_type=F32)


def _combine_norm_body(h_ref, yg_ref, route_ref, g_ref, y_ref):
    y_ref[...] = _rms(_combine(h_ref, yg_ref, route_ref), g_ref[...])


def combine_proj(h, yg, route, g, w):
    n = h.shape[0]
    nn = w.shape[1]
    return pl.pallas_call(
        _combine_proj_body,
        grid=(n // ROW_TILE,),
        in_specs=[_rows(D_MODEL), _rows(2 * D_MODEL), _rows(LANES), _full((1, D_MODEL)), _full((D_MODEL, nn))],
        out_specs=[_rows(D_MODEL), _rows(nn)],
        out_shape=[jax.ShapeDtypeStruct((n, D_MODEL), F32), jax.ShapeDtypeStruct((n, nn), F32)],
        compiler_params=_params(1),
        name="combine_proj",
    )(h, yg, route, g.reshape(1, D_MODEL), w.astype(BF16))


def combine_norm(h, yg, route, g):
    n = h.shape[0]
    return pl.pallas_call(
        _combine_norm_body,
        grid=(n // ROW_TILE,),
        in_specs=[_rows(D_MODEL), _rows(2 * D_MODEL), _rows(LANES), _full((1, D_MODEL))],
        out_specs=_rows(D_MODEL),
        out_shape=jax.ShapeDtypeStruct((n, D_MODEL), F32),
        compiler_params=_params(1),
        name="combine_norm",
    )(h, yg, route, g.reshape(1, D_MODEL))


def _even_first(x, axis):
    n = x.shape[axis]
    idx = jnp.concatenate([jnp.arange(0, n, 2), jnp.arange(1, n, 2)])
    return jnp.take(x, idx, axis=axis)


def _decode_query_cols(q_s, gates_s):
    B, T, G, R = DEC_BATCH, DEC_SEQ, NSA_KV, NSA_REP
    qg = q_s.reshape(B, T, G, R, HD).transpose(0, 2, 4, 3, 1).reshape(B, G, HD, R * T)
    qt = jnp.zeros((B, G, HD, G, R * T), BF16)
    for g in range(G):
        qt = qt.at[:, g, :, g, :].set(qg[:, g])
    qt = jnp.pad(qt.reshape(B, G * HD, G * R * T), ((0, 0), (0, 0), (0, LANES - G * R * T)))
    gr = gates_s[:, :3 * NSA_HEADS].reshape(B, T, NSA_HEADS, 3).transpose(0, 3, 2, 1).reshape(B, 3, NSA_HEADS * T)
    return qt, jnp.pad(gr, ((0, 0), (0, 0), (0, LANES - NSA_HEADS * T)))


def mixer_a(h, p, past):
    pos = jnp.concatenate([jnp.tile(jnp.arange(SEQ, dtype=jnp.int32), BATCH),
                           PAST_LEN + jnp.tile(jnp.arange(DEC_SEQ, dtype=jnp.int32), DEC_BATCH)])
    (q, kvc, kvs, kvw, ks, vst, kw, vwt, gates, gq, gk, gv, la, sgg) = inproj_a(
        h, p['norm_mix'][0], p['a_w_in'][0], p['a_gla_wa2'][0], p['a_gla_ba'][0], pos)
    pe, wc, gnorm = p['a_cmp_pe'][0], p['a_cmp_w'][0], p['a_gla_norm'][0]
    P = N_PROMPT
    kv5 = lambda x, lead: x.reshape(*lead, 2, NSA_KV, HD)
    ckv = cmp_blocks(kvc[:P].reshape(BATCH * N_CMP_PROMPT, CMP_ROW), pe, wc)
    ckv = _even_first(ckv.reshape(BATCH, N_CMP_PROMPT, NSA_KV_W), 1)
    o_nsa_p = nsa_prompt_attn(q, gates, ckv, ks, vst, kw, vwt)
    o_gla_p, st_p = gla_seq(gq, gk, la, gv, sgg, gnorm)
    new_p = (kv5(kvc[:P], (BATCH, SEQ)), kv5(kvs[:P], (BATCH, SEQ)),
             kv5(kvw[:P], (BATCH, SEQ))[:, SEQ - WINDOW:], _gla_state_from_pairs(st_p))
    o_gla_s, st_s = gla_step(gq, gk, la, gv, sgg, gnorm, _gla_state_to_pairs(past['state_gla'][0]), P)
    n_pool = past['cache_cmp_kv'].shape[1]
    blocks_per_page = PAGE_SIZE // CMP_BLOCK
    ckv_pool = cmp_blocks(past['cache_cmp_kv'][0].reshape(n_pool * blocks_per_page, CMP_ROW), pe, wc)
    ckv_seq = ckv_pool.reshape(n_pool, blocks_per_page, NSA_KV_W)[past['page_table']]
    ckv_seq = _even_first(ckv_seq.reshape(DEC_BATCH, N_CMP_DEC, NSA_KV_W), 1)
    qt, gate_rows = _decode_query_cols(q[P:], gates[P:])
    per_seq = lambda x: x[P:].reshape(DEC_BATCH, 1, DEC_SEQ * NSA_KV_W)
    o_t, win_new = nsa_decode_attn(
        qt, gate_rows, ckv_seq, past['cache_slc_kv'][0].reshape(n_pool, PAGE_SIZE, NSA_KV_W),
        past['cache_win_kv'][0].reshape(DEC_BATCH, -1, NSA_KV_W), per_seq(kvs), per_seq(kvw), past['page_table'])
    o_nsa_s = o_t[:, :, :NSA_HEADS * DEC_SEQ].reshape(DEC_BATCH, HD, NSA_HEADS, DEC_SEQ)
    o_nsa_s = o_nsa_s.transpose(0, 3, 2, 1).reshape(N_SAMPLE, NSA_Q_W).astype(BF16)
    new_s = (kv5(kvc[P:], (DEC_BATCH, DEC_SEQ)), kv5(kvs[P:], (DEC_BATCH, DEC_SEQ)),
             kv5(win_new, (DEC_BATCH, win_new.shape[1])), _gla_state_from_pairs(st_s))
    o_nsa = jnp.concatenate([o_nsa_p, o_nsa_s], axis=0)
    o_gla = jnp.concatenate([o_gla_p, o_gla_s], axis=0)
    return o_nsa, o_gla, new_p, new_s


def run_trunk(x_prompt, x_sample, p, past):
    h = jnp.concatenate([x_prompt.reshape(N_PROMPT, D_MODEL), x_sample.reshape(N_SAMPLE, D_MODEL)], axis=0)
    o_nsa, o_gla, new_p, new_s = mixer_a(h, p, past)
    h, hn, route, counts = post_mixer(h, [o_nsa, o_gla], p['a_w_out'][0], p['norm_ffn'][0], p['m_w_rg'][0],
                                      p['m_b_rg'][0], p['m_w_re'][0], p['m_b_re'][0])
    yg = moe_experts(hn, route, counts, p['m_w1'][0], p['m_w3'][0], p['m_w2'][0])
    h, u = combine_proj(h, yg, route, p['norm_mix'][1], p['c_w_in'][0])
    lru_w = (p['c_conv_w'][0], p['c_conv_b'][0], p['c_w_a'][0], p['c_b_a'][0], p['c_w_x'][0], p['c_b_x'][0],
             p['c_lam'][0])
    us = u[N_PROMPT:].reshape(DEC_BATCH, DEC_SEQ, 2 * D_RNN)
    y_p, lru_p = lru_seq(u, *lru_w)
    y_s, lru_s = lru_step(jnp.swapaxes(us, 0, 1), past['state_conv'][0], past['state_lru'][0], *lru_w)
    conv_p = u[:N_PROMPT].reshape(BATCH, SEQ, 2 * D_RNN)[:, SEQ - (CONV_W - 1):, D_RNN:]
    conv_s = us[:, DEC_SEQ - (CONV_W - 1):, D_RNN:]
    mix_in = jnp.concatenate([y_p, jnp.swapaxes(y_s, 0, 1).reshape(N_SAMPLE, D_RNN)], axis=0)
    h, hn, route, counts = post_mixer(h, [mix_in], p['c_w_out'][0], p['norm_ffn'][1], p['m_w_rg'][1],
                                      p['m_b_rg'][1], p['m_w_re'][1], p['m_b_re'][1])
    yg = moe_experts(hn, route, counts, p['m_w1'][1], p['m_w3'][1], p['m_w2'][1])
    y = combine_norm(h, yg, route, p['norm_final'])
    y_prompt = y[:N_PROMPT].reshape(BATCH, SEQ, D_MODEL)
    y_sample = y[N_PROMPT:].reshape(DEC_BATCH, DEC_SEQ, D_MODEL)
    return (y_prompt, y_sample), new_p + (lru_p, conv_p), new_s + (lru_s, conv_s)


def kernel(x_prompt, x_sample, cache_cmp_kv, cache_slc_kv, cache_win_kv, state_gla, state_lru, state_conv,
           page_table, norm_mix, norm_ffn, norm_final, a_w_in, a_cmp_pe, a_cmp_w, a_gla_wa2, a_gla_ba,
           a_gla_norm, a_w_out, c_w_in, c_conv_w, c_conv_b, c_w_a, c_b_a, c_w_x, c_b_x, c_lam, c_w_out,
           m_w_rg, m_b_rg, m_w_re, m_b_re, m_w1, m_w3, m_w2):
    p = {'norm_mix': norm_mix, 'norm_ffn': norm_ffn, 'norm_final': norm_final,
         'a_w_in': a_w_in, 'a_cmp_pe': a_cmp_pe, 'a_cmp_w': a_cmp_w, 'a_gla_wa2': a_gla_wa2,
         'a_gla_ba': a_gla_ba, 'a_gla_norm': a_gla_norm, 'a_w_out': a_w_out,
         'c_w_in': c_w_in, 'c_conv_w': c_conv_w, 'c_conv_b': c_conv_b, 'c_w_a': c_w_a, 'c_b_a': c_b_a,
         'c_w_x': c_w_x, 'c_b_x': c_b_x, 'c_lam': c_lam, 'c_w_out': c_w_out,
         'm_w_rg': m_w_rg, 'm_b_rg': m_b_rg, 'm_w_re': m_w_re, 'm_b_re': m_b_re,
         'm_w1': m_w1, 'm_w3': m_w3, 'm_w2': m_w2}
    past = {'cache_cmp_kv': cache_cmp_kv, 'cache_slc_kv': cache_slc_kv, 'cache_win_kv': cache_win_kv,
            'state_gla': state_gla, 'state_lru': state_lru, 'state_conv': state_conv,
            'page_table': page_table}
    (y_p, y_s), sp, ss = run_trunk(x_prompt, x_sample, p, past)
    outs = [y_p, y_s]
    for a, b in zip(sp, ss):
        outs += [a[None], b[None]]
    return tuple(outs)
```

```python
import functools
import jax, jax.numpy as jnp
from jax import lax
import numpy as np
from jax.experimental import pallas as pl
from jax.experimental.pallas import tpu as pltpu

D_MODEL = 1024
BATCH = 2
SEQ = 8192
DEC_BATCH = 128
DEC_SEQ = 4
PAST_LEN = 2048
PAGE_SIZE = 128
EPS = 1e-6
NSA_HEADS = 8
NSA_KV = 2
NSA_REP = NSA_HEADS // NSA_KV
HD = 64
CMP_BLOCK = 32
SLC_BLOCK = 64
SLC_TOPK = 16
WINDOW = 512
Q_BLOCK = 128
ROPE_DIM = HD // 4
ROPE_THETA = 500000.0
GLA_HEADS = 4
GLA_DK = 64
GLA_DV = 128
GLA_LOWRANK = 16
GLA_TAU = 16.0
D_RNN = 1280
LRU_BLOCKS = 10
LRU_BW = D_RNN // LRU_BLOCKS
CONV_W = 4
LRU_C = 8.0
N_GROUPS = 4
EXP_PER_GROUP = 8
N_EXPERTS = N_GROUPS * EXP_PER_GROUP
E_HID = 512
TOP_K_IN_GROUP = 2
NSA_Q_W = NSA_HEADS * HD
NSA_KV_W = 2 * NSA_KV * HD
GLA_K_W = GLA_HEADS * GLA_DK
GLA_V_W = GLA_HEADS * GLA_DV
A_SIZES = (NSA_Q_W, NSA_KV_W, NSA_KV_W, NSA_KV_W, 3 * NSA_HEADS, GLA_K_W, GLA_K_W, GLA_V_W, GLA_LOWRANK, GLA_V_W)
N_PROMPT = BATCH * SEQ
N_SAMPLE = DEC_BATCH * DEC_SEQ
N_TOK = N_PROMPT + N_SAMPLE
N_PAGES = PAST_LEN // PAGE_SIZE

F32 = jnp.float32
BF16 = jnp.bfloat16
VMEM_LIMIT_BYTES = 56 * 1024 * 1024
LANES = 128
SUBLANES = 8
ROW_TILE = 512


def _params(n_axes):
    return pltpu.CompilerParams(dimension_semantics=("arbitrary",) * n_axes, vmem_limit_bytes=VMEM_LIMIT_BYTES)


def _full(shape):
    return pl.BlockSpec(shape, lambda *_: (0,) * len(shape))


def _rows(width):
    return pl.BlockSpec((ROW_TILE, width), lambda i: (i, 0))


def _rms(x, g):
    return x * lax.rsqrt(jnp.mean(x * x, axis=-1, keepdims=True) + EPS) * g


def _softplus(x):
    return jnp.maximum(x, 0.0) + jnp.log1p(jnp.exp(-jnp.abs(x)))


def _gelu_tanh(x):
    return x * (0.5 * (1.0 + jnp.tanh(0.7978845608028654 * (x + 0.044715 * (x * x * x)))))


def _nt_dot(a, b):
    return lax.dot_general(a, b, (((1,), (1,)), ((), ())), preferred_element_type=F32)


def _tn_dot(a, b):
    return lax.dot_general(a, b, (((0,), (0,)), ((), ())), preferred_element_type=F32)


A_Q0, A_KVC0, A_KVS0, A_KVW0 = 0, 512, 768, 1024
A_GQ0, A_GK0, A_GV0, A_GG0, A_MISC0 = 1280, 1536, 1792, 2304, 2816
A_COLS = A_MISC0 + LANES
MISC_LR0 = 3 * NSA_HEADS
KV_HALF = NSA_KV * HD


def _rope_lanes(x, cos_t, sin_lo, sin_hi):
    reps = x.shape[1] // LANES
    tile = (lambda t: jnp.concatenate([t] * reps, axis=1)) if reps > 1 else (lambda t: t)
    w = x.shape[1]
    half = ROPE_DIM // 2
    return x * tile(cos_t) + pltpu.roll(x, half, 1) * tile(sin_hi) + pltpu.roll(x, w - half, 1) * tile(sin_lo)


def _inproj_a_body(h_ref, g_ref, w_ref, wa2_ref, ba_ref, cos_ref, slo_ref, shi_ref,
                   q_ref, kvc_ref, kvs_ref, kvw_ref, kvct_ref, kvst_ref, kvwt_ref, ks_ref, vst_ref, kw_ref,
                   vwt_ref, gates_ref, gq_ref, gk_ref, gv_ref, la_ref, sgg_ref):
    y = _rms(h_ref[...], g_ref[...]).astype(BF16)
    proj = lambda a, b: jnp.dot(y, w_ref[:, a:b], preferred_element_type=F32)
    cos_t, sin_lo, sin_hi = cos_ref[...], slo_ref[...], shi_ref[...]
    q_ref[...] = (_rope_lanes(proj(A_Q0, A_KVC0), cos_t, sin_lo, sin_hi) * (HD ** -0.5)).astype(BF16)
    kvc = proj(A_KVC0, A_KVS0)
    kvc_ref[...] = kvc
    kvct_ref[...] = kvc.T
    for a0, kv_ref, kvt_ref, k_ref, vt_ref in ((A_KVS0, kvs_ref, kvst_ref, ks_ref, vst_ref),
                                               (A_KVW0, kvw_ref, kvwt_ref, kw_ref, vwt_ref)):
        kv = proj(a0, a0 + 2 * KV_HALF)
        k = _rope_lanes(kv[:, :KV_HALF], cos_t, sin_lo, sin_hi)
        vt = kv[:, KV_HALF:].T
        kv_ref[:, :KV_HALF] = k
        kv_ref[:, KV_HALF:] = kv[:, KV_HALF:]
        kvt_ref[:KV_HALF, :] = k.T
        kvt_ref[KV_HALF:, :] = vt
        k_ref[...] = k.astype(BF16)
        vt_ref[...] = vt.astype(BF16)
    misc = proj(A_MISC0, A_COLS)
    gates_ref[...] = jax.nn.sigmoid(misc)
    z = jnp.dot(misc.astype(BF16), wa2_ref[...], preferred_element_type=F32) + ba_ref[...]
    la_ref[...] = -_softplus(-z) * (1.0 / GLA_TAU)
    gq_ref[...] = proj(A_GQ0, A_GK0) * (GLA_DK ** -0.5)
    gk_ref[...] = proj(A_GK0, A_GV0)
    gv_ref[...] = proj(A_GV0, A_GG0).astype(BF16)
    gg = proj(A_GG0, A_MISC0)
    sgg_ref[...] = gg * jax.nn.sigmoid(gg)


def _rope_tables(pos):
    half = ROPE_DIM // 2
    inv = 1.0 / (ROPE_THETA ** (jnp.arange(0, ROPE_DIM, 2, dtype=F32) / ROPE_DIM))
    ang = pos.astype(F32)[:, None] * inv[None, :]
    cos, sin = jnp.cos(ang), jnp.sin(ang)
    n = pos.shape[0]
    one = jnp.ones((n, HD - ROPE_DIM), F32)
    zero = jnp.zeros((n, HD - ROPE_DIM), F32)
    zh = jnp.zeros((n, half), F32)
    seg = lambda a, b, rest: jnp.concatenate([a, b, rest] * (LANES // HD), axis=1)
    return seg(cos, cos, one), seg(-sin, zh, zero), seg(zh, sin, zero)


def inproj_a(h, g, w_in, wa2, ba, pos):
    n = h.shape[0]
    zpad = jnp.zeros((D_MODEL, LANES - 3 * NSA_HEADS - GLA_LOWRANK), F32)
    o = np.cumsum((0,) + A_SIZES)
    w = jnp.concatenate([w_in[:, o[0]:o[4]], w_in[:, o[5]:o[8]], w_in[:, o[9]:o[10]],
                         w_in[:, o[4]:o[5]], w_in[:, o[8]:o[9]], zpad], axis=1).astype(BF16)
    wa2p = jnp.zeros((LANES, GLA_K_W), F32).at[MISC_LR0:MISC_LR0 + GLA_LOWRANK].set(wa2).astype(BF16)
    cols = pl.BlockSpec((KV_HALF, ROW_TILE), lambda i: (0, i))
    kvt = (pl.BlockSpec((NSA_KV_W, ROW_TILE), lambda i: (0, i)), (NSA_KV_W, n), F32)
    outs = [(_rows(NSA_Q_W), (n, NSA_Q_W), BF16), (_rows(NSA_KV_W), (n, NSA_KV_W), F32),
            (_rows(NSA_KV_W), (n, NSA_KV_W), F32), (_rows(NSA_KV_W), (n, NSA_KV_W), F32), kvt, kvt, kvt,
            (_rows(KV_HALF), (n, KV_HALF), BF16), (cols, (KV_HALF, n), BF16),
            (_rows(KV_HALF), (n, KV_HALF), BF16), (cols, (KV_HALF, n), BF16),
            (_rows(LANES), (n, LANES), F32), (_rows(GLA_K_W), (n, GLA_K_W), F32),
            (_rows(GLA_K_W), (n, GLA_K_W), F32), (_rows(GLA_V_W), (n, GLA_V_W), BF16),
            (_rows(GLA_K_W), (n, GLA_K_W), F32), (_rows(GLA_V_W), (n, GLA_V_W), F32)]
    return pl.pallas_call(
        _inproj_a_body,
        grid=(n // ROW_TILE,),
        in_specs=[_rows(D_MODEL), _full((1, D_MODEL)), _full((D_MODEL, A_COLS)), _full((LANES, GLA_K_W)),
                  _full((1, GLA_K_W)), _rows(LANES), _rows(LANES), _rows(LANES)],
        out_specs=[s for s, _, _ in outs],
        out_shape=[jax.ShapeDtypeStruct(shape, dt) for _, shape, dt in outs],
        compiler_params=_params(1),
        name="inproj_a",
    )(h, g.reshape(1, D_MODEL), w, wa2p, ba.reshape(1, GLA_K_W), *_rope_tables(pos))


CMP_TILE_BLOCKS = 256
CMP_TILE_ROWS = CMP_TILE_BLOCKS * CMP_BLOCK
CMP_TILE_PAGES = CMP_TILE_ROWS // PAGE_SIZE


def _cmp_reduce(xk_ref, xv_ref, pe_ref, w_ref):
    acc = jnp.zeros((CMP_TILE_BLOCKS, NSA_KV_W), F32)
    for l in range(CMP_BLOCK):
        rows = pl.ds(l, CMP_TILE_BLOCKS, stride=CMP_BLOCK)
        xl = jnp.concatenate([xk_ref[rows, :], xv_ref[rows, :]], axis=1) + pe_ref[l:l + 1, :]
        acc = acc + jnp.dot(xl.astype(BF16), w_ref[l], preferred_element_type=F32)
    return acc


def _cmp_rows_body(xk_ref, xv_ref, pe_ref, w_ref, o_ref):
    o_ref[...] = _cmp_reduce(xk_ref, xv_ref, pe_ref, w_ref)


def _cmp_pages_body(x_ref, pe_ref, w_ref, o_ref, xk_sc, xv_sc):
    for pg in range(CMP_TILE_PAGES):
        rows = slice(pg * PAGE_SIZE, (pg + 1) * PAGE_SIZE)
        xk_sc[rows, :] = x_ref[pg, :KV_HALF, :].T
        xv_sc[rows, :] = x_ref[pg, KV_HALF:, :].T
    o_ref[...] = _cmp_reduce(xk_sc, xv_sc, pe_ref, w_ref)


def _cmp_weights(pe, w_cmp):
    pe_rows = jnp.broadcast_to(pe[:, :, None, :], (CMP_BLOCK, 2, NSA_KV, HD)).reshape(CMP_BLOCK, NSA_KV_W)
    w_bd = jnp.einsum('lcde,cx,gy->lcgdxye', w_cmp, jnp.eye(2, dtype=F32), jnp.eye(NSA_KV, dtype=F32))
    return pe_rows, w_bd.reshape(CMP_BLOCK, NSA_KV_W, NSA_KV_W).astype(BF16)


def cmp_blocks_rows(x, n_rows, pe, w_cmp):
    return pl.pallas_call(
        _cmp_rows_body,
        grid=(n_rows // CMP_TILE_ROWS,),
        in_specs=[pl.BlockSpec((CMP_TILE_ROWS, KV_HALF), lambda i: (i, 0)),
                  pl.BlockSpec((CMP_TILE_ROWS, KV_HALF), lambda i: (i, 1)), _full((CMP_BLOCK, NSA_KV_W)),
                  _full((CMP_BLOCK, NSA_KV_W, NSA_KV_W))],
        out_specs=pl.BlockSpec((CMP_TILE_BLOCKS, NSA_KV_W), lambda i: (i, 0)),
        out_shape=jax.ShapeDtypeStruct((n_rows // CMP_BLOCK, NSA_KV_W), F32),
        compiler_params=_params(1),
        name="cmp_blocks_rows",
    )(x, x, *_cmp_weights(pe, w_cmp))


def cmp_blocks_pages(xt, pe, w_cmp):
    n_pages = xt.shape[0]
    return pl.pallas_call(
        _cmp_pages_body,
        grid=(n_pages // CMP_TILE_PAGES,),
        in_specs=[pl.BlockSpec((CMP_TILE_PAGES, NSA_KV_W, PAGE_SIZE), lambda i: (i, 0, 0)),
                  _full((CMP_BLOCK, NSA_KV_W)), _full((CMP_BLOCK, NSA_KV_W, NSA_KV_W))],
        out_specs=pl.BlockSpec((CMP_TILE_BLOCKS, NSA_KV_W), lambda i: (i, 0)),
        out_shape=jax.ShapeDtypeStruct((n_pages * PAGE_SIZE // CMP_BLOCK, NSA_KV_W), F32),
        scratch_shapes=[pltpu.VMEM((CMP_TILE_ROWS, KV_HALF), F32), pltpu.VMEM((CMP_TILE_ROWS, KV_HALF), F32)],
        compiler_params=_params(1),
        name="cmp_blocks_pages",
    )(xt, *_cmp_weights(pe, w_cmp))


KEY_TILE = 256
NEG_BIG = -1e30
N_CMP_PROMPT = SEQ // CMP_BLOCK
N_SLC_PROMPT = SEQ // SLC_BLOCK
SLC_SHIFT = SLC_BLOCK.bit_length() - 1
N_CMP_DEC = PAST_LEN // CMP_BLOCK
N_SLC_DEC = -(-(PAST_LEN + DEC_SEQ) // SLC_BLOCK)
N_SLC_DEC_PAD = -(-N_SLC_DEC // LANES) * LANES
DEC_COLS_PER_GROUP = NSA_REP * DEC_SEQ


def _tile_cols(x, reps):
    return jnp.concatenate([x] * reps, axis=1) if reps > 1 else x


def _even_first_cmp_end(n_cmp):
    j = lax.broadcasted_iota(jnp.int32, (n_cmp, 1), 0)
    blk = jnp.where(j < n_cmp // 2, 2 * j, 2 * (j - n_cmp // 2) + 1)
    return (blk + 1) * CMP_BLOCK - 1


def _softmax_cols(s, mask):
    s = jnp.where(mask, s, -jnp.inf)
    m = jnp.max(s, axis=0, keepdims=True)
    m = jnp.where(m > -jnp.inf, m, 0.0)
    e = jnp.where(mask, jnp.exp(s - m), 0.0)
    return e / jnp.maximum(jnp.sum(e, axis=0, keepdims=True), 1e-30)


def _select_cols(p_slc, qpos, n_top):
    ns = p_slc.shape[0]
    blk = lax.broadcasted_iota(jnp.int32, p_slc.shape, 0)
    cur = qpos >> SLC_SHIFT
    forced = (blk == 0) | (blk == cur) | (blk == cur - 1)
    score = jnp.where(forced, jnp.inf, p_slc)
    score = jnp.where(blk <= cur, score, -jnp.inf)
    sel = jnp.zeros(p_slc.shape, F32)
    for _ in range(n_top):
        m = jnp.max(score, axis=0, keepdims=True)
        idx = jnp.min(jnp.where(score == m, blk, ns), axis=0, keepdims=True)
        hit = blk == idx
        sel = jnp.where(hit & (m > -jnp.inf), 1.0, sel)
        score = jnp.where(hit, -jnp.inf, score)
    return sel


def _flash_init(m_sc, l_sc, acc_sc):
    m_sc[...] = jnp.full(m_sc.shape, NEG_BIG, F32)
    l_sc[...] = jnp.zeros(l_sc.shape, F32)
    acc_sc[...] = jnp.zeros(acc_sc.shape, F32)


def _flash_cols(scores, mask, pv, m_sc, l_sc, acc_sc):
    s = jnp.where(mask, scores, NEG_BIG)
    m_old = m_sc[...]
    m_new = jnp.maximum(m_old, jnp.max(s, axis=0, keepdims=True))
    alpha = jnp.exp(m_old - m_new)
    p = jnp.where(mask, jnp.exp(s - m_new), 0.0)
    l_sc[...] = alpha * l_sc[...] + jnp.sum(p, axis=0, keepdims=True)
    acc_sc[...] = alpha * acc_sc[...] + pv(p.astype(BF16))
    m_sc[...] = m_new


def _flash_out(l_sc, acc_sc):
    return acc_sc[...] / jnp.maximum(l_sc[...], 1e-30)


def _pv_split(vt, p):
    c2 = p.shape[1] // 2
    return jnp.concatenate([jnp.dot(vt[:HD], p[:, :c2], preferred_element_type=F32),
                            jnp.dot(vt[HD:], p[:, c2:], preferred_element_type=F32)], axis=1)


def _nsa_prompt_body(q_ref, gates_ref, ckv_ref, cos_ref, slo_ref, shi_ref, ks_ref, vst_ref, kw_ref, vwt_ref,
                     o_ref, m_sc, l_sc, acc_sc):
    i = pl.program_id(1)
    nq = Q_BLOCK
    cols = NSA_HEADS * nq
    qpos = i * nq + lax.broadcasted_iota(jnp.int32, (1, nq), 1)
    q = q_ref[...].astype(F32)
    pairs = [q[:, j * LANES:(j + 1) * LANES].T for j in range(NSA_HEADS // 2)]
    zero = jnp.zeros((HD, cols // 2), F32)
    qt_g = [jnp.concatenate([pairs[2 * g][:HD], pairs[2 * g][HD:], pairs[2 * g + 1][:HD], pairs[2 * g + 1][HD:]],
                            axis=1) for g in range(NSA_KV)]
    qt = jnp.concatenate([jnp.concatenate([qt_g[0], zero], axis=1),
                          jnp.concatenate([zero, qt_g[1]], axis=1)], axis=0).astype(BF16)
    ckv = ckv_ref[0]
    ck = _rope_lanes(ckv[:, :KV_HALF], cos_ref[...], slo_ref[...], shi_ref[...]).astype(BF16)
    cvt = ckv[:, KV_HALF:].T.astype(BF16)
    c_mask = _even_first_cmp_end(N_CMP_PROMPT) <= qpos
    p = _softmax_cols(jnp.dot(ck, qt, preferred_element_type=F32), _tile_cols(c_mask, NSA_HEADS))
    o_c = _pv_split(cvt, p.astype(BF16))
    sel = []
    for g in range(NSA_KV):
        c0 = g * NSA_REP * nq
        p_grp = p[:, c0:c0 + nq]
        for r in range(1, NSA_REP):
            p_grp = p_grp + p[:, c0 + r * nq:c0 + (r + 1) * nq]
        p_slc = p_grp[:N_CMP_PROMPT // 2] + p_grp[N_CMP_PROMPT // 2:]
        sel.append(_select_cols(p_slc, qpos, SLC_TOPK))
    sel = jnp.concatenate(sel, axis=1).astype(BF16)
    key_row = lax.broadcasted_iota(jnp.int32, (KEY_TILE, 1), 0)
    key_blk = lax.broadcasted_iota(jnp.int32, (KEY_TILE, N_SLC_PROMPT), 0)
    blk_col = lax.broadcasted_iota(jnp.int32, (KEY_TILE, N_SLC_PROMPT), 1)
    _flash_init(m_sc, l_sc, acc_sc)

    def slc_tile(kt, _):
        k0 = pl.multiple_of(kt * KEY_TILE, KEY_TILE)
        expand = jnp.where(((k0 + key_blk) >> SLC_SHIFT) == blk_col, 1.0, 0.0).astype(BF16)
        chosen = jnp.dot(expand, sel, preferred_element_type=F32) > 0.5
        causal = (k0 + key_row) <= qpos
        mask = jnp.concatenate([_tile_cols(chosen[:, :nq] & causal, NSA_REP),
                                _tile_cols(chosen[:, nq:] & causal, NSA_REP)], axis=1)
        vt = vst_ref[:, pl.ds(k0, KEY_TILE)]
        scores = jnp.dot(ks_ref[pl.ds(k0, KEY_TILE), :], qt, preferred_element_type=F32)
        _flash_cols(scores, mask, lambda pb: _pv_split(vt, pb), m_sc, l_sc, acc_sc)
        return 0

    lax.fori_loop(0, i // 2 + 1, slc_tile, 0)
    o_s = _flash_out(l_sc, acc_sc)
    _flash_init(m_sc, l_sc, acc_sc)

    def win_tile(kt, _):
        k0 = pl.multiple_of(kt * KEY_TILE, KEY_TILE)
        d = qpos - (k0 + key_row)
        vt = vwt_ref[:, pl.ds(k0, KEY_TILE)]
        scores = jnp.dot(kw_ref[pl.ds(k0, KEY_TILE), :], qt, preferred_element_type=F32)
        _flash_cols(scores, _tile_cols((d >= 0) & (d < WINDOW), NSA_HEADS),
                    lambda pb: _pv_split(vt, pb), m_sc, l_sc, acc_sc)
        return 0

    lax.fori_loop(jnp.maximum(i - WINDOW // nq, 0) // 2, i // 2 + 1, win_tile, 0)
    o_w = _flash_out(l_sc, acc_sc)
    gates_t = gates_ref[...].T
    merged = []
    for hd in range(NSA_HEADS):
        cs = slice(hd * nq, (hd + 1) * nq)
        gate = lambda br: gates_t[3 * hd + br:3 * hd + br + 1]
        merged.append(gate(0) * o_c[:, cs] + gate(1) * o_s[:, cs] + gate(2) * o_w[:, cs])
    for j in range(NSA_HEADS // 2):
        pair = jnp.concatenate([merged[2 * j], merged[2 * j + 1]], axis=0).T
        o_ref[:, j * LANES:(j + 1) * LANES] = pair.astype(BF16)


def nsa_prompt_attn(q, gates, ckv, ks, vst, kw, vwt):
    nqb = SEQ // Q_BLOCK
    tok = lambda wd: pl.BlockSpec((Q_BLOCK, wd), lambda b, i: (b * nqb + i, 0))
    seq_rows = pl.BlockSpec((SEQ, KV_HALF), lambda b, i: (b, 0))
    seq_cols = pl.BlockSpec((KV_HALF, SEQ), lambda b, i: (0, b))
    n_cmp = N_CMP_PROMPT
    c_blk = jnp.concatenate([jnp.arange(0, n_cmp, 2), jnp.arange(1, n_cmp, 2)]).astype(jnp.int32)
    cols = NSA_HEADS * Q_BLOCK
    return pl.pallas_call(
        _nsa_prompt_body,
        grid=(BATCH, nqb),
        in_specs=[tok(NSA_Q_W), tok(LANES), pl.BlockSpec((1, n_cmp, NSA_KV_W), lambda b, i: (b, 0, 0)),
                  _full((n_cmp, LANES)), _full((n_cmp, LANES)), _full((n_cmp, LANES)),
                  seq_rows, seq_cols, seq_rows, seq_cols],
        out_specs=tok(NSA_Q_W),
        out_shape=jax.ShapeDtypeStruct((N_PROMPT, NSA_Q_W), BF16),
        scratch_shapes=[pltpu.VMEM((1, cols), F32), pltpu.VMEM((1, cols), F32), pltpu.VMEM((HD, cols), F32)],
        compiler_params=_params(2),
        name="nsa_prompt",
    )(q, gates, ckv, *_rope_tables((c_blk + 1) * CMP_BLOCK - 1), ks, vst, kw, vwt)


def _nsa_decode_body(pt_ref, qt_ref, gate_ref, ckv_ref, cos_ref, slo_ref, shi_ref, hsum_ref, *refs):
    page_refs = refs[:N_PAGES]
    win_ref, kvs_new_ref, kvw_new_ref, o_ref, win_out_ref, m_sc, l_sc, acc_sc = refs[N_PAGES:]
    qt = qt_ref[0]
    lane = lax.broadcasted_iota(jnp.int32, (1, LANES), 1)
    qpos = PAST_LEN + (lane & (DEC_SEQ - 1))
    group0 = lane < DEC_COLS_PER_GROUP
    ckv = ckv_ref[0]
    ck = _rope_lanes(ckv[:, :KV_HALF], cos_ref[...], slo_ref[...], shi_ref[...]).astype(BF16)
    c_mask = _even_first_cmp_end(N_CMP_DEC) <= qpos
    p = _softmax_cols(jnp.dot(ck, qt, preferred_element_type=F32), c_mask)
    o_c = _tn_dot(ckv[:, KV_HALF:].astype(BF16), p.astype(BF16))
    p_grp = jnp.dot(p, hsum_ref[...], preferred_element_type=F32, precision=lax.Precision.HIGHEST)
    p_slc = jnp.concatenate([p_grp[:N_CMP_DEC // 2] + p_grp[N_CMP_DEC // 2:],
                             jnp.zeros((N_SLC_DEC_PAD - N_CMP_DEC // 2, LANES), F32)], axis=0)
    sel = _select_cols(p_slc, qpos, SLC_TOPK)

    def new_rows(ref):
        row = ref[0]
        kv = jnp.concatenate([row[:, t * NSA_KV_W:(t + 1) * NSA_KV_W] for t in range(DEC_SEQ)], axis=0)
        return jnp.concatenate([kv, jnp.zeros((SUBLANES - DEC_SEQ, NSA_KV_W), F32)], axis=0)

    new_row = lax.broadcasted_iota(jnp.int32, (SUBLANES, 1), 0)
    new_pos = PAST_LEN + new_row
    new_valid = new_row < DEC_SEQ
    _flash_init(m_sc, l_sc, acc_sc)
    kt_old = jnp.concatenate([r[0, :KV_HALF, :] for r in page_refs], axis=1).astype(BF16)
    vt_old = jnp.concatenate([r[0, KV_HALF:, :] for r in page_refs], axis=1).astype(BF16)
    key_blk = lax.broadcasted_iota(jnp.int32, (PAST_LEN, N_SLC_DEC_PAD), 0) >> SLC_SHIFT
    blk_col = lax.broadcasted_iota(jnp.int32, (PAST_LEN, N_SLC_DEC_PAD), 1)
    chosen = jnp.dot(jnp.where(key_blk == blk_col, 1.0, 0.0).astype(BF16), sel.astype(BF16),
                     preferred_element_type=F32) > 0.5
    old_pos = lax.broadcasted_iota(jnp.int32, (PAST_LEN, 1), 0)
    _flash_cols(_tn_dot(kt_old, qt), chosen & (old_pos <= qpos),
                lambda pb: jnp.dot(vt_old, pb, preferred_element_type=F32), m_sc, l_sc, acc_sc)
    kv_new = new_rows(kvs_new_ref)
    sel_new = sel[(PAST_LEN >> SLC_SHIFT):(PAST_LEN >> SLC_SHIFT) + 1] > 0.5
    v_new = kv_new[:, KV_HALF:].astype(BF16)
    _flash_cols(jnp.dot(kv_new[:, :KV_HALF].astype(BF16), qt, preferred_element_type=F32),
                sel_new & new_valid & (new_pos <= qpos), lambda pb: _tn_dot(v_new, pb), m_sc, l_sc, acc_sc)
    o_s = _flash_out(l_sc, acc_sc)
    _flash_init(m_sc, l_sc, acc_sc)
    n_buf = win_ref.shape[2]
    win = win_ref[0]
    d = qpos - (PAST_LEN - n_buf + lax.broadcasted_iota(jnp.int32, (n_buf, 1), 0))
    vt_win = win[KV_HALF:].astype(BF16)
    _flash_cols(_tn_dot(win[:KV_HALF].astype(BF16), qt), (d >= 0) & (d < WINDOW),
                lambda pb: jnp.dot(vt_win, pb, preferred_element_type=F32), m_sc, l_sc, acc_sc)
    kw_new = new_rows(kvw_new_ref)
    d = qpos - new_pos
    vw_new = kw_new[:, KV_HALF:].astype(BF16)
    _flash_cols(jnp.dot(kw_new[:, :KV_HALF].astype(BF16), qt, preferred_element_type=F32),
                new_valid & (d >= 0) & (d < WINDOW), lambda pb: _tn_dot(vw_new, pb), m_sc, l_sc, acc_sc)
    o_w = _flash_out(l_sc, acc_sc)
    g = gate_ref[0]
    o = g[0:1] * o_c + g[1:2] * o_s + g[2:3] * o_w
    o_ref[0] = jnp.where(group0, o[:HD], o[HD:])
    key = lax.broadcasted_iota(jnp.int32, (SUBLANES, n_buf), 1)
    place = jnp.where((key == n_buf - DEC_SEQ + new_row) & new_valid, 1.0, 0.0)
    placed = lax.dot_general(kw_new, place, (((0,), (0,)), ((), ())), preferred_element_type=F32,
                             precision=lax.Precision.HIGHEST)
    keep = lax.broadcasted_iota(jnp.int32, (1, n_buf), 1) < n_buf - DEC_SEQ
    win_out_ref[0] = jnp.where(keep, pltpu.roll(win, n_buf - DEC_SEQ, 1), placed)


def nsa_decode_attn(qt, gate_rows, ckv, slc_pool, win_buf, kvs_new, kvw_new, page_table):
    n_buf = win_buf.shape[2]
    per_b = lambda *shape: pl.BlockSpec((1,) + shape, lambda b, pt: (b,) + (0,) * len(shape))
    const = lambda *shape: pl.BlockSpec(shape, lambda b, pt: (0,) * len(shape))
    page = lambda j: pl.BlockSpec((1, NSA_KV_W, PAGE_SIZE), lambda b, pt: (pt[b, j], 0, 0))
    c_blk = jnp.concatenate([jnp.arange(0, N_CMP_DEC, 2), jnp.arange(1, N_CMP_DEC, 2)]).astype(jnp.int32)
    col = jnp.arange(LANES)
    used = col < NSA_KV * DEC_COLS_PER_GROUP
    same = (col[:, None] // DEC_COLS_PER_GROUP == col[None, :] // DEC_COLS_PER_GROUP) & \
           (col[:, None] % DEC_SEQ == col[None, :] % DEC_SEQ) & used[:, None] & used[None, :]
    return pl.pallas_call(
        _nsa_decode_body,
        grid_spec=pltpu.PrefetchScalarGridSpec(
            num_scalar_prefetch=1,
            grid=(DEC_BATCH,),
            in_specs=[per_b(LANES, LANES), per_b(3, LANES), per_b(N_CMP_DEC, NSA_KV_W),
                      const(N_CMP_DEC, LANES), const(N_CMP_DEC, LANES), const(N_CMP_DEC, LANES),
                      const(LANES, LANES)] + [page(j) for j in range(N_PAGES)]
                     + [per_b(NSA_KV_W, n_buf), per_b(1, DEC_SEQ * NSA_KV_W), per_b(1, DEC_SEQ * NSA_KV_W)],
            out_specs=[per_b(HD, LANES), per_b(NSA_KV_W, n_buf)],
            scratch_shapes=[pltpu.VMEM((1, LANES), F32), pltpu.VMEM((1, LANES), F32),
                            pltpu.VMEM((LANES, LANES), F32)],
        ),
        out_shape=[jax.ShapeDtypeStruct((DEC_BATCH, HD, LANES), F32),
                   jax.ShapeDtypeStruct((DEC_BATCH, NSA_KV_W, n_buf), F32)],
        compiler_params=_params(1),
        name="nsa_decode",
    )(page_table, qt, gate_rows, ckv, *_rope_tables((c_blk + 1) * CMP_BLOCK - 1), same.astype(F32),
      *([slc_pool] * N_PAGES), win_buf, kvs_new, kvw_new)


GLA_PAIRS = GLA_HEADS // 2
GLA_ROWS = 128
GLA_SUB = 16
GLA_STEP_SEQS = 8


def _gla_rows(q, k, la, v, sgg, gnorm, st_ref, sub):
    R = q.shape[0]
    row = lax.broadcasted_iota(jnp.int32, (R, GLA_K_W), 0)
    rin = row % sub
    cum = la
    d = 1
    while d < sub:
        cum = cum + jnp.where(rin >= d, pltpu.roll(cum, d, 0), 0.0)
        d *= 2
    lane = lax.broadcasted_iota(jnp.int32, (sub, LANES), 1)
    lo = lane < GLA_DK
    rsub = lax.broadcasted_iota(jnp.int32, (sub, LANES), 0)
    out_rows = []
    for c in range(R // sub):
        rs = slice(c * sub, (c + 1) * sub)
        cum_c = cum[rs]
        last = cum_c[sub - 1:sub]
        qe = q[rs] * jnp.exp(cum_c)
        kdec = k[rs] * jnp.exp(last - cum_c)
        v_c = v[rs]
        heads = []
        for pr in range(GLA_PAIRS):
            ls = slice(pr * LANES, (pr + 1) * LANES)
            st = st_ref[pr]
            st_b = st.astype(BF16)
            qe_p, kd_p, q_p, k_p, cum_p = qe[:, ls], kdec[:, ls], q[rs, ls], k[rs, ls], cum_c[:, ls]
            v_pair = [v_c[:, (2 * pr + hh) * GLA_DV:(2 * pr + hh + 1) * GLA_DV] for hh in range(2)]
            upd = jnp.zeros((GLA_DV, LANES), F32)
            o_pair = []
            for hh in range(2):
                keep = lo if hh == 0 else jnp.logical_not(lo)
                o_pair.append(_nt_dot(jnp.where(keep, qe_p, 0.0).astype(BF16), st_b))
                upd = upd + _tn_dot(v_pair[hh].astype(BF16), jnp.where(keep, kd_p, 0.0).astype(BF16))
            for j in range(sub):
                dj = jnp.where(rsub >= j, jnp.exp(cum_p - cum_p[j:j + 1]), 0.0)
                w = q_p * k_p[j:j + 1] * dj
                a_lo = jnp.sum(jnp.where(lo, w, 0.0), axis=-1, keepdims=True)
                a_hi = jnp.sum(jnp.where(lo, 0.0, w), axis=-1, keepdims=True)
                o_pair[0] = o_pair[0] + a_lo * v_pair[0][j:j + 1]
                o_pair[1] = o_pair[1] + a_hi * v_pair[1][j:j + 1]
            st_ref[pr] = st * jnp.exp(last[:, ls]) + upd
            heads += o_pair
        out_rows.append(jnp.concatenate([_rms(x, gnorm) for x in heads], axis=1))
    return jnp.concatenate(out_rows, axis=0) * sgg


def _gla_seq_body(q_ref, k_ref, la_ref, v_ref, sgg_ref, gn_ref, o_ref, st_out_ref, st_sc):
    @pl.when(pl.program_id(1) == 0)
    def _():
        st_sc[...] = jnp.zeros(st_sc.shape, F32)

    o = _gla_rows(q_ref[...], k_ref[...], la_ref[...], v_ref[...].astype(F32), sgg_ref[...], gn_ref[...],
                  st_sc, GLA_SUB)
    o_ref[...] = o.astype(BF16)
    st_out_ref[0] = st_sc[...]


def gla_seq(q, k, la, v, sgg, gnorm):
    nt = SEQ // GLA_ROWS
    rows = lambda wd: pl.BlockSpec((GLA_ROWS, wd), lambda b, t: (b * nt + t, 0))
    return pl.pallas_call(
        _gla_seq_body,
        grid=(BATCH, nt),
        in_specs=[rows(GLA_K_W), rows(GLA_K_W), rows(GLA_K_W), rows(GLA_V_W), rows(GLA_V_W),
                  _full((1, GLA_DV))],
        out_specs=[rows(GLA_V_W), pl.BlockSpec((1, GLA_PAIRS, GLA_DV, LANES), lambda b, t: (b, 0, 0, 0))],
        out_shape=[jax.ShapeDtypeStruct((N_PROMPT, GLA_V_W), BF16),
                   jax.ShapeDtypeStruct((BATCH, GLA_PAIRS, GLA_DV, LANES), F32)],
        scratch_shapes=[pltpu.VMEM((GLA_PAIRS, GLA_DV, LANES), F32)],
        compiler_params=_params(2),
        name="gla_seq",
    )(q, k, la, v, sgg, gnorm.reshape(1, GLA_DV))


def _gla_step_body(q_ref, k_ref, la_ref, v_ref, sgg_ref, gn_ref, st_in_ref, o_ref, st_out_ref):
    st_out_ref[...] = st_in_ref[...]
    q, k, la, v, sgg = q_ref[...], k_ref[...], la_ref[...], v_ref[...].astype(F32), sgg_ref[...]
    for j in range(GLA_STEP_SEQS):
        rs = slice(j * DEC_SEQ, (j + 1) * DEC_SEQ)
        o = _gla_rows(q[rs], k[rs], la[rs], v[rs], sgg[rs], gn_ref[...], st_out_ref.at[j], DEC_SEQ)
        o_ref[rs, :] = o.astype(BF16)


def gla_step(q, k, la, v, sgg, gnorm, st_in, row0):
    rows_per = GLA_STEP_SEQS * DEC_SEQ
    blk0 = row0 // rows_per
    rows = lambda wd: pl.BlockSpec((rows_per, wd), lambda i: (blk0 + i, 0))
    st_spec = pl.BlockSpec((GLA_STEP_SEQS, GLA_PAIRS, GLA_DV, LANES), lambda i: (i, 0, 0, 0))
    return pl.pallas_call(
        _gla_step_body,
        grid=(DEC_BATCH // GLA_STEP_SEQS,),
        in_specs=[rows(GLA_K_W), rows(GLA_K_W), rows(GLA_K_W), rows(GLA_V_W), rows(GLA_V_W),
                  _full((1, GLA_DV)), st_spec],
        out_specs=[pl.BlockSpec((rows_per, GLA_V_W), lambda i: (i, 0)), st_spec],
        out_shape=[jax.ShapeDtypeStruct((N_SAMPLE, GLA_V_W), BF16),
                   jax.ShapeDtypeStruct((DEC_BATCH, GLA_PAIRS, GLA_DV, LANES), F32)],
        compiler_params=_params(1),
        name="gla_step",
    )(q, k, la, v, sgg, gnorm.reshape(1, GLA_DV), st_in)


def _gla_state_to_pairs(s):
    B = s.shape[0]
    return s.reshape(B, GLA_PAIRS, 2, GLA_DK, GLA_DV).transpose(0, 1, 4, 2, 3).reshape(B, GLA_PAIRS, GLA_DV, LANES)


def _gla_state_from_pairs(st):
    B = st.shape[0]
    return st.reshape(B, GLA_PAIRS, GLA_DV, 2, GLA_DK).transpose(0, 1, 3, 4, 2).reshape(B, GLA_HEADS, GLA_DK, GLA_DV)


LRU_TIME_TILE = 256


def _lru_gates(xc, wa_ref, ba, wx_ref, bx, lam):
    xcb = xc.astype(BF16)
    r_parts, i_parts = [], []
    for n in range(LRU_BLOCKS):
        xs = xcb[:, n * LRU_BW:(n + 1) * LRU_BW]
        r_parts.append(jnp.dot(xs, wa_ref[n], preferred_element_type=F32))
        i_parts.append(jnp.dot(xs, wx_ref[n], preferred_element_type=F32))
    r = jax.nn.sigmoid(jnp.concatenate(r_parts, axis=-1) + ba)
    i = jax.nn.sigmoid(jnp.concatenate(i_parts, axis=-1) + bx)
    log_a = -LRU_C * r * _softplus(-lam)
    a = jnp.exp(log_a)
    u = jnp.sqrt(-jnp.tanh(log_a) * (a * a + 1.0)) * (i * xc)
    return a, u


def _lru_seq_body(u_ref, cw_ref, cb_ref, wa_ref, ba_ref, wx_ref, bx_ref, lam_ref, y_ref, hT_ref, xp_sc, h_sc):
    tt = LRU_TIME_TILE

    @pl.when(pl.program_id(1) == 0)
    def _():
        xp_sc[0:SUBLANES, :] = jnp.zeros((SUBLANES, D_RNN), F32)
        h_sc[...] = jnp.zeros((1, D_RNN), F32)

    xp_sc[SUBLANES:SUBLANES + tt, :] = u_ref[:, D_RNN:]
    xc = cb_ref[...]
    for w in range(CONV_W):
        off = SUBLANES - (CONV_W - 1) + w
        xc = xc + cw_ref[w:w + 1, :] * xp_sc[off:off + tt, :]
    a, u = _lru_gates(xc, wa_ref, ba_ref[...], wx_ref, bx_ref[...], lam_ref[...])
    row = lax.broadcasted_iota(jnp.int32, (tt, D_RNN), 0)
    d = 1
    while d < tt:
        keep = row >= d
        a_prev = jnp.where(keep, pltpu.roll(a, d, 0), 1.0)
        u_prev = jnp.where(keep, pltpu.roll(u, d, 0), 0.0)
        u = a * u_prev + u
        a = a * a_prev
        d *= 2
    h = a * h_sc[...] + u
    h_sc[...] = h[tt - 1:tt, :]
    hT_ref[0] = h[tt - 1:tt, :]
    y_ref[...] = (_gelu_tanh(u_ref[:, :D_RNN]) * h).astype(BF16)
    xp_sc[0:SUBLANES, :] = xp_sc[tt:tt + SUBLANES, :]


def _lru_weight_args(cw, cb, wa, ba, wx, bx, lam):
    row = lambda v: v.reshape(1, D_RNN)
    return (cw, row(cb), wa.astype(BF16), row(ba), wx.astype(BF16), row(bx), row(lam))


_LRU_WEIGHT_SPECS = [_full((CONV_W, D_RNN)), _full((1, D_RNN)), _full((LRU_BLOCKS, LRU_BW, LRU_BW)),
                     _full((1, D_RNN)), _full((LRU_BLOCKS, LRU_BW, LRU_BW)), _full((1, D_RNN)),
                     _full((1, D_RNN))]


def lru_seq(u, cw, cb, wa, ba, wx, bx, lam):
    tt = LRU_TIME_TILE
    nt = SEQ // tt
    y, hT = pl.pallas_call(
        _lru_seq_body,
        grid=(BATCH, nt),
        in_specs=[pl.BlockSpec((tt, 2 * D_RNN), lambda b, t: (b * nt + t, 0))] + _LRU_WEIGHT_SPECS,
        out_specs=[pl.BlockSpec((tt, D_RNN), lambda b, t: (b * nt + t, 0)),
                   pl.BlockSpec((1, 1, D_RNN), lambda b, t: (b, 0, 0))],
        out_shape=[jax.ShapeDtypeStruct((N_PROMPT, D_RNN), BF16), jax.ShapeDtypeStruct((BATCH, 1, D_RNN), F32)],
        scratch_shapes=[pltpu.VMEM((tt + SUBLANES, D_RNN), F32), pltpu.VMEM((1, D_RNN), F32)],
        compiler_params=_params(2),
        name="lru_seq",
    )(u, *_lru_weight_args(cw, cb, wa, ba, wx, bx, lam))
    return y, hT.reshape(BATCH, D_RNN)


def _lru_step_body(u_ref, cs_ref, h0_ref, cw_ref, cb_ref, wa_ref, ba_ref, wx_ref, bx_ref, lam_ref, y_ref, hT_ref):
    n_t = u_ref.shape[0]
    hist = [cs_ref[:, w, :] for w in range(CONV_W - 1)] + [u_ref[t, :, D_RNN:] for t in range(n_t)]
    h = h0_ref[...]
    for t in range(n_t):
        xc = cb_ref[...]
        for w in range(CONV_W):
            xc = xc + cw_ref[w:w + 1, :] * hist[t + w]
        a, u = _lru_gates(xc, wa_ref, ba_ref[...], wx_ref, bx_ref[...], lam_ref[...])
        h = a * h + u
        y_ref[t] = (_gelu_tanh(u_ref[t, :, :D_RNN]) * h).astype(BF16)
    hT_ref[...] = h


def lru_step(u, conv_state, h0, cw, cb, wa, ba, wx, bx, lam):
    T, B, _ = u.shape
    return pl.pallas_call(
        _lru_step_body,
        out_shape=[jax.ShapeDtypeStruct((T, B, D_RNN), BF16), jax.ShapeDtypeStruct((B, D_RNN), F32)],
        compiler_params=pltpu.CompilerParams(vmem_limit_bytes=VMEM_LIMIT_BYTES),
        name="lru_step",
    )(u, conv_state, h0, *_lru_weight_args(cw, cb, wa, ba, wx, bx, lam))


ROUTE_E1, ROUTE_E2, ROUTE_G1, ROUTE_G2, ROUTE_R1, ROUTE_R2 = range(6)
EXPERT_LANE0 = N_GROUPS


def _lane_pick(val_by_lane):
    rows = next(iter(val_by_lane.values())).shape[0]
    lane = lax.broadcasted_iota(jnp.int32, (rows, LANES), 1)
    out = jnp.zeros((rows, LANES), F32)
    for l, v in val_by_lane.items():
        out = jnp.where(lane == l, v, out)
    return out


def _route_rows(logits, tri_ref, carry_ref):
    rows = logits.shape[0]
    lane = lax.broadcasted_iota(jnp.int32, (rows, LANES), 1)
    neg = -jnp.inf
    gl = jnp.where(lane < N_GROUPS, logits, neg)
    gmax = jnp.max(gl, axis=-1, keepdims=True)
    gtop = jnp.min(jnp.where(gl == gmax, lane, LANES), axis=-1, keepdims=True)
    gsum = jnp.sum(jnp.where(lane < N_GROUPS, jnp.exp(logits - gmax), 0.0), axis=-1, keepdims=True)
    g_w = 1.0 / gsum
    lo = EXPERT_LANE0 + EXP_PER_GROUP * gtop
    el = jnp.where((lane >= lo) & (lane < lo + EXP_PER_GROUP), logits, neg)
    v1 = jnp.max(el, axis=-1, keepdims=True)
    i1 = jnp.min(jnp.where(el == v1, lane, LANES), axis=-1, keepdims=True)
    el2 = jnp.where(lane == i1, neg, el)
    v2 = jnp.max(el2, axis=-1, keepdims=True)
    i2 = jnp.min(jnp.where(el2 == v2, lane, LANES), axis=-1, keepdims=True)
    p2 = jnp.exp(v2 - v1)
    den = 1.0 + p2
    gate1 = (1.0 / den) * g_w
    gate2 = (p2 / den) * g_w
    hit1 = lane == i1
    hit2 = lane == i2
    onehot = jnp.where(hit1 | hit2, 1.0, 0.0)
    before = jnp.dot(tri_ref[...], onehot.astype(BF16), preferred_element_type=F32) + carry_ref[...]
    rank1 = jnp.sum(jnp.where(hit1, before, 0.0), axis=-1, keepdims=True)
    rank2 = jnp.sum(jnp.where(hit2, before, 0.0), axis=-1, keepdims=True)
    carry_ref[...] = carry_ref[...] + jnp.sum(onehot, axis=0, keepdims=True)
    return _lane_pick({ROUTE_E1: (i1 - EXPERT_LANE0).astype(F32), ROUTE_E2: (i2 - EXPERT_LANE0).astype(F32),
                       ROUTE_G1: gate1, ROUTE_G2: gate2, ROUTE_R1: rank1, ROUTE_R2: rank2})


def _pack_bf16_halves(x):
    w = x.shape[1] // 2
    bits = lambda v: pltpu.bitcast(v.astype(F32), jnp.uint32)
    return (bits(x[:, :w]) >> 16) | bits(x[:, w:])


def _unpack_bf16_halves(p):
    lo = pltpu.bitcast(p << 16, F32).astype(BF16)
    hi = pltpu.bitcast(p & jnp.uint32(0xFFFF0000), F32).astype(BF16)
    return jnp.concatenate([lo, hi], axis=1)


def _post_mixer_body(n_mix, h_ref, *refs):
    m_refs, wo_refs = refs[:n_mix], refs[n_mix:2 * n_mix]
    g_ref, wr_ref, br_ref, tri_ref, h1_ref, hn_ref, route_ref, cnt_ref, carry_sc = refs[2 * n_mix:]

    @pl.when(pl.program_id(0) == 0)
    def _():
        carry_sc[...] = jnp.zeros((1, LANES), F32)

    mix = jnp.dot(m_refs[0][...], wo_refs[0][...], preferred_element_type=F32)
    for m_ref, wo_ref in zip(m_refs[1:], wo_refs[1:]):
        mix = mix + jnp.dot(m_ref[...], wo_ref[...], preferred_element_type=F32)
    h1 = h_ref[...] + mix
    h1_ref[...] = h1
    hn = _rms(h1, g_ref[...]).astype(BF16)
    hn_ref[...] = _pack_bf16_halves(hn)
    logits = jnp.dot(hn, wr_ref[...], preferred_element_type=F32) + br_ref[...]
    route_ref[...] = _route_rows(logits, tri_ref, carry_sc)
    cnt_ref[...] = carry_sc[...]


def post_mixer(h, mix_ins, w_out, g_ffn, w_rg, b_rg, w_re, b_re):
    n = h.shape[0]
    ks = [m.shape[1] for m in mix_ins]
    offs = np.cumsum([0] + ks)
    w_parts = [w_out[offs[j]:offs[j + 1]].astype(BF16) for j in range(len(ks))]
    pad = LANES - N_GROUPS - N_EXPERTS
    wr = jnp.concatenate([w_rg, w_re, jnp.zeros((D_MODEL, pad), F32)], axis=1).astype(BF16)
    br = jnp.concatenate([b_rg, b_re, jnp.zeros((pad,), F32)]).reshape(1, LANES)
    tri = jnp.tril(jnp.ones((ROW_TILE, ROW_TILE), BF16), -1)
    return pl.pallas_call(
        functools.partial(_post_mixer_body, len(ks)),
        grid=(n // ROW_TILE,),
        in_specs=[_rows(D_MODEL)] + [_rows(k) for k in ks] + [_full((k, D_MODEL)) for k in ks]
                 + [_full((1, D_MODEL)), _full((D_MODEL, LANES)), _full((1, LANES)), _full((ROW_TILE, ROW_TILE))],
        out_specs=[_rows(D_MODEL), _rows(D_MODEL // 2), _rows(LANES), _full((1, LANES))],
        out_shape=[jax.ShapeDtypeStruct((n, D_MODEL), F32), jax.ShapeDtypeStruct((n, D_MODEL // 2), jnp.uint32),
                   jax.ShapeDtypeStruct((n, LANES), F32), jax.ShapeDtypeStruct((1, LANES), F32)],
        scratch_shapes=[pltpu.VMEM((1, LANES), F32)],
        compiler_params=_params(1),
        name="post_mixer",
    )(h, *mix_ins, *w_parts, g_ffn.reshape(1, D_MODEL), wr, br, tri)


MOE_ROWS = 256


def _ffn_body(be_ref, nb_ref, x_ref, w1_ref, w3_ref, w2_ref, y_ref, w1_sc, w3_sc, w2_sc):
    i = pl.program_id(0)
    new_expert = jnp.logical_or(i == 0, be_ref[i] != be_ref[jnp.maximum(i - 1, 0)])

    @pl.when(jnp.logical_and(new_expert, i < nb_ref[0]))
    def _():
        w1_sc[...] = w1_ref[0, 0].astype(BF16)
        w3_sc[...] = w3_ref[0, 0].astype(BF16)
        w2_sc[...] = w2_ref[0, 0].astype(BF16)

    @pl.when(i < nb_ref[0])
    def _():
        x = _unpack_bf16_halves(x_ref[...])
        a = jnp.dot(x, w1_sc[...], preferred_element_type=F32)
        b = jnp.dot(x, w3_sc[...], preferred_element_type=F32)
        hdn = (a * jax.nn.sigmoid(a) * b).astype(BF16)
        y_ref[...] = jnp.dot(hdn, w2_sc[...], preferred_element_type=F32)

    @pl.when(i >= nb_ref[0])
    def _():
        y_ref[...] = jnp.zeros(y_ref.shape, F32)


def expert_ffn(xs, blk_exp, n_active, w1, w3, w2, layer):
    n_slots = xs.shape[0]
    nb = n_slots // MOE_ROWS
    wmap = lambda i, be, na: (layer, be[i], 0, 0)
    xmap = lambda i, be, na: (jnp.minimum(i, na[0] - 1), 0)
    return pl.pallas_call(
        _ffn_body,
        grid_spec=pltpu.PrefetchScalarGridSpec(
            num_scalar_prefetch=2,
            grid=(nb,),
            in_specs=[pl.BlockSpec((MOE_ROWS, D_MODEL // 2), xmap),
                      pl.BlockSpec((1, 1, D_MODEL, E_HID), wmap),
                      pl.BlockSpec((1, 1, D_MODEL, E_HID), wmap),
                      pl.BlockSpec((1, 1, E_HID, D_MODEL), wmap)],
            out_specs=pl.BlockSpec((MOE_ROWS, D_MODEL), lambda i, be, na: (i, 0)),
            scratch_shapes=[pltpu.VMEM((D_MODEL, E_HID), BF16), pltpu.VMEM((D_MODEL, E_HID), BF16),
                            pltpu.VMEM((E_HID, D_MODEL), BF16)],
        ),
        out_shape=jax.ShapeDtypeStruct((n_slots, D_MODEL), F32),
        compiler_params=_params(1),
        name="expert_ffn",
    )(blk_exp, n_active, xs, w1, w3, w2)


def moe_dispatch(route, counts_row, n):
    e = route[:, ROUTE_E1:ROUTE_E2 + 1].astype(jnp.int32)
    rank = route[:, ROUTE_R1:ROUTE_R2 + 1].astype(jnp.int32)
    counts = counts_row[0, EXPERT_LANE0:EXPERT_LANE0 + N_EXPERTS].astype(jnp.int32)
    padded = ((counts + MOE_ROWS - 1) // MOE_ROWS) * MOE_ROWS
    pad_end = jnp.cumsum(padded)
    pad_start = pad_end - padded
    dest = pad_start[e] + rank
    nb = -(-(n * TOP_K_IN_GROUP) // MOE_ROWS) + N_EXPERTS
    n_slots = nb * MOE_ROWS
    tok = jnp.broadcast_to(jnp.arange(n, dtype=jnp.int32)[:, None], (n, TOP_K_IN_GROUP))
    slot_tok = jnp.zeros((n_slots,), jnp.int32).at[dest.reshape(-1)].set(
        tok.reshape(-1), unique_indices=True, mode='promise_in_bounds')
    blk_start = jnp.arange(nb, dtype=jnp.int32) * MOE_ROWS
    blk_exp = jnp.sum((pad_end[None, :] <= blk_start[:, None]).astype(jnp.int32), axis=1)
    blk_exp = jnp.minimum(blk_exp, N_EXPERTS - 1)
    n_active = (pad_end[-1] // MOE_ROWS).astype(jnp.int32).reshape(1)
    return slot_tok, dest, blk_exp, n_active


def moe_experts(hn, route, counts_row, w1, w3, w2, layer):
    n = hn.shape[0]
    slot_tok, dest, blk_exp, n_active = moe_dispatch(route, counts_row, n)
    xs = hn.at[slot_tok].get(mode='promise_in_bounds')
    ys = expert_ffn(xs, blk_exp, n_active, w1, w3, w2, layer)
    return [ys.at[dest[:, j]].get(mode='promise_in_bounds') for j in range(TOP_K_IN_GROUP)]


def _combine(h_ref, y1_ref, y2_ref, route_ref):
    lane = lax.broadcasted_iota(jnp.int32, (ROW_TILE, LANES), 1)
    r = route_ref[...]
    g1 = jnp.sum(jnp.where(lane == ROUTE_G1, r, 0.0), axis=-1, keepdims=True)
    g2 = jnp.sum(jnp.where(lane == ROUTE_G2, r, 0.0), axis=-1, keepdims=True)
    return h_ref[...] + (y1_ref[...] * g1 + y2_ref[...] * g2)


def _combine_proj_body(h_ref, y1_ref, y2_ref, route_ref, g_ref, w_ref, h2_ref, u_ref):
    h2 = _combine(h_ref, y1_ref, y2_ref, route_ref)
    h2_ref[...] = h2
    u_ref[...] = jnp.dot(_rms(h2, g_ref[...]).astype(BF16), w_ref[...], preferred_element_type=F32)


def _combine_norm_body(h_ref, y1_ref, y2_ref, route_ref, g_ref, y_ref):
    y_ref[...] = _rms(_combine(h_ref, y1_ref, y2_ref, route_ref), g_ref[...])


def combine_proj(h, ys, route, g, w):
    n = h.shape[0]
    nn = w.shape[1]
    return pl.pallas_call(
        _combine_proj_body,
        grid=(n // ROW_TILE,),
        in_specs=[_rows(D_MODEL), _rows(D_MODEL), _rows(D_MODEL), _rows(LANES), _full((1, D_MODEL)),
                  _full((D_MODEL, nn))],
        out_specs=[_rows(D_MODEL), _rows(nn)],
        out_shape=[jax.ShapeDtypeStruct((n, D_MODEL), F32), jax.ShapeDtypeStruct((n, nn), F32)],
        compiler_params=_params(1),
        name="combine_proj",
    )(h, *ys, route, g.reshape(1, D_MODEL), w.astype(BF16))


def combine_norm(h, ys, route, g):
    n = h.shape[0]
    return pl.pallas_call(
        _combine_norm_body,
        grid=(n // ROW_TILE,),
        in_specs=[_rows(D_MODEL), _rows(D_MODEL), _rows(D_MODEL), _rows(LANES), _full((1, D_MODEL))],
        out_specs=_rows(D_MODEL),
        out_shape=jax.ShapeDtypeStruct((n, D_MODEL), F32),
        compiler_params=_params(1),
        name="combine_norm",
    )(h, *ys, route, g.reshape(1, D_MODEL))


def _even_first(x, axis):
    n = x.shape[axis]
    idx = jnp.concatenate([jnp.arange(0, n, 2), jnp.arange(1, n, 2)])
    return jnp.take(x, idx, axis=axis)


def _decode_query_cols(q_s, gates_s):
    B, T, G, R = DEC_BATCH, DEC_SEQ, NSA_KV, NSA_REP
    qg = q_s.reshape(B, T, G, R, HD).transpose(0, 2, 4, 3, 1).reshape(B, G, HD, R * T)
    qt = jnp.zeros((B, G, HD, G, R * T), BF16)
    for g in range(G):
        qt = qt.at[:, g, :, g, :].set(qg[:, g])
    qt = jnp.pad(qt.reshape(B, G * HD, G * R * T), ((0, 0), (0, 0), (0, LANES - G * R * T)))
    gr = gates_s[:, :3 * NSA_HEADS].reshape(B, T, NSA_HEADS, 3).transpose(0, 3, 2, 1).reshape(B, 3, NSA_HEADS * T)
    return qt, jnp.pad(gr, ((0, 0), (0, 0), (0, LANES - NSA_HEADS * T)))


def _feature_major(cache):
    lead, rows = cache.shape[:2]
    return jnp.transpose(cache, (0, 2, 3, 4, 1)).reshape(lead, NSA_KV_W, rows)


def _kv_rows_from_feature_major(xt, lead):
    x = xt.reshape(2, NSA_KV, HD, *lead)
    n = len(lead)
    return jnp.transpose(x, tuple(range(3, 3 + n)) + (0, 1, 2))


def mixer_a(h, p, past):
    pos = jnp.concatenate([jnp.tile(jnp.arange(SEQ, dtype=jnp.int32), BATCH),
                           PAST_LEN + jnp.tile(jnp.arange(DEC_SEQ, dtype=jnp.int32), DEC_BATCH)])
    (q, kvc, kvs, kvw, kvct, kvst, kvwt, ks, vst, kw, vwt, gates, gq, gk, gv, la, sgg) = inproj_a(
        h, p['norm_mix'][0], p['a_w_in'][0], p['a_gla_wa2'][0], p['a_gla_ba'][0], pos)
    pe, wc, gnorm = p['a_cmp_pe'][0], p['a_cmp_w'][0], p['a_gla_norm'][0]
    P = N_PROMPT
    ckv = cmp_blocks_rows(kvc, P, pe, wc)
    ckv = _even_first(ckv.reshape(BATCH, N_CMP_PROMPT, NSA_KV_W), 1)
    o_nsa_p = nsa_prompt_attn(q, gates, ckv, ks, vst, kw, vwt)
    o_gla_p, st_p = gla_seq(gq, gk, la, gv, sgg, gnorm)
    win_p = jnp.concatenate([kvwt[:, (b + 1) * SEQ - WINDOW:(b + 1) * SEQ] for b in range(BATCH)], axis=1)
    new_p = (_kv_rows_from_feature_major(kvct[:, :P], (BATCH, SEQ)),
             _kv_rows_from_feature_major(kvst[:, :P], (BATCH, SEQ)),
             _kv_rows_from_feature_major(win_p, (BATCH, WINDOW)), _gla_state_from_pairs(st_p))
    o_gla_s, st_s = gla_step(gq, gk, la, gv, sgg, gnorm, _gla_state_to_pairs(past['state_gla'][0]), P)
    n_pool = past['cache_cmp_kv'].shape[1]
    ckv_pool = cmp_blocks_pages(_feature_major(past['cache_cmp_kv'][0]), pe, wc)
    ckv_seq = ckv_pool.reshape(n_pool, PAGE_SIZE // CMP_BLOCK, NSA_KV_W)[past['page_table']]
    ckv_seq = _even_first(ckv_seq.reshape(DEC_BATCH, N_CMP_DEC, NSA_KV_W), 1)
    qt, gate_rows = _decode_query_cols(q[P:], gates[P:])
    per_seq = lambda x: x[P:].reshape(DEC_BATCH, 1, DEC_SEQ * NSA_KV_W)
    o_t, win_new = nsa_decode_attn(qt, gate_rows, ckv_seq, _feature_major(past['cache_slc_kv'][0]),
                                   _feature_major(past['cache_win_kv'][0]), per_seq(kvs), per_seq(kvw),
                                   past['page_table'])
    o_nsa_s = o_t[:, :, :NSA_HEADS * DEC_SEQ].reshape(DEC_BATCH, HD, NSA_HEADS, DEC_SEQ)
    o_nsa_s = o_nsa_s.transpose(0, 3, 2, 1).reshape(N_SAMPLE, NSA_Q_W).astype(BF16)
    n_buf = win_new.shape[2]
    win_s = jnp.transpose(win_new.reshape(DEC_BATCH, 2, NSA_KV, HD, n_buf), (0, 4, 1, 2, 3))
    new_s = (_kv_rows_from_feature_major(kvct[:, P:], (DEC_BATCH, DEC_SEQ)),
             _kv_rows_from_feature_major(kvst[:, P:], (DEC_BATCH, DEC_SEQ)), win_s, _gla_state_from_pairs(st_s))
    o_nsa = jnp.concatenate([o_nsa_p, o_nsa_s], axis=0)
    o_gla = jnp.concatenate([o_gla_p, o_gla_s], axis=0)
    return o_nsa, o_gla, new_p, new_s


def run_trunk(x_prompt, x_sample, p, past):
    h = jnp.concatenate([x_prompt.reshape(N_PROMPT, D_MODEL), x_sample.reshape(N_SAMPLE, D_MODEL)], axis=0)
    o_nsa, o_gla, new_p, new_s = mixer_a(h, p, past)
    h, hn, route, counts = post_mixer(h, [o_nsa, o_gla], p['a_w_out'][0], p['norm_ffn'][0], p['m_w_rg'][0],
                                      p['m_b_rg'][0], p['m_w_re'][0], p['m_b_re'][0])
    ys = moe_experts(hn, route, counts, p['m_w1'], p['m_w3'], p['m_w2'], 0)
    h, u = combine_proj(h, ys, route, p['norm_mix'][1], p['c_w_in'][0])
    lru_w = (p['c_conv_w'][0], p['c_conv_b'][0], p['c_w_a'][0], p['c_b_a'][0], p['c_w_x'][0], p['c_b_x'][0],
             p['c_lam'][0])
    us = u[N_PROMPT:].reshape(DEC_BATCH, DEC_SEQ, 2 * D_RNN)
    y_p, lru_p = lru_seq(u, *lru_w)
    y_s, lru_s = lru_step(jnp.swapaxes(us, 0, 1), past['state_conv'][0], past['state_lru'][0], *lru_w)
    conv_p = jnp.stack([u[(b + 1) * SEQ - (CONV_W - 1):(b + 1) * SEQ, D_RNN:] for b in range(BATCH)])
    conv_s = us[:, DEC_SEQ - (CONV_W - 1):, D_RNN:]
    mix_in = jnp.concatenate([y_p, jnp.swapaxes(y_s, 0, 1).reshape(N_SAMPLE, D_RNN)], axis=0)
    h, hn, route, counts = post_mixer(h, [mix_in], p['c_w_out'][0], p['norm_ffn'][1], p['m_w_rg'][1],
                                      p['m_b_rg'][1], p['m_w_re'][1], p['m_b_re'][1])
    ys = moe_experts(hn, route, counts, p['m_w1'], p['m_w3'], p['m_w2'], 1)
    y = combine_norm(h, ys, route, p['norm_final'])
    y_prompt = y[:N_PROMPT].reshape(BATCH, SEQ, D_MODEL)
    y_sample = y[N_PROMPT:].reshape(DEC_BATCH, DEC_SEQ, D_MODEL)
    return (y_prompt, y_sample), new_p + (lru_p, conv_p), new_s + (lru_s, conv_s)


def kernel(x_prompt, x_sample, cache_cmp_kv, cache_slc_kv, cache_win_kv, state_gla, state_lru, state_conv,
           page_table, norm_mix, norm_ffn, norm_final, a_w_in, a_cmp_pe, a_cmp_w, a_gla_wa2, a_gla_ba,
           a_gla_norm, a_w_out, c_w_in, c_conv_w, c_conv_b, c_w_a, c_b_a, c_w_x, c_b_x, c_lam, c_w_out,
           m_w_rg, m_b_rg, m_w_re, m_b_re, m_w1, m_w3, m_w2):
    p = {'norm_mix': norm_mix, 'norm_ffn': norm_ffn, 'norm_final': norm_final,
         'a_w_in': a_w_in, 'a_cmp_pe': a_cmp_pe, 'a_cmp_w': a_cmp_w, 'a_gla_wa2': a_gla_wa2,
         'a_gla_ba': a_gla_ba, 'a_gla_norm': a_gla_norm, 'a_w_out': a_w_out,
         'c_w_in': c_w_in, 'c_conv_w': c_conv_w, 'c_conv_b': c_conv_b, 'c_w_a': c_w_a, 'c_b_a': c_b_a,
         'c_w_x': c_w_x, 'c_b_x': c_b_x, 'c_lam': c_lam, 'c_w_out': c_w_out,
         'm_w_rg': m_w_rg, 'm_b_rg': m_b_rg, 'm_w_re': m_w_re, 'm_b_re': m_b_re,
         'm_w1': m_w1, 'm_w3': m_w3, 'm_w2': m_w2}
    past = {'cache_cmp_kv': cache_cmp_kv, 'cache_slc_kv': cache_slc_kv, 'cache_win_kv': cache_win_kv,
            'state_gla': state_gla, 'state_lru': state_lru, 'state_conv': state_conv,
            'page_table': page_table}
    (y_p, y_s), sp, ss = run_trunk(x_prompt, x_sample, p, past)
    outs = [y_p, y_s]
    for a, b in zip(sp, ss):
        outs += [a[None], b[None]]
    return tuple(outs)
```

```python
import functools
import jax, jax.numpy as jnp
from jax import lax
import numpy as np
from jax.experimental import pallas as pl
from jax.experimental.pallas import tpu as pltpu

D_MODEL = 1024
BATCH = 2
SEQ = 8192
DEC_BATCH = 128
DEC_SEQ = 4
PAST_LEN = 2048
PAGE_SIZE = 128
EPS = 1e-6
NSA_HEADS = 8
NSA_KV = 2
NSA_REP = NSA_HEADS // NSA_KV
HD = 64
CMP_BLOCK = 32
SLC_BLOCK = 64
SLC_TOPK = 16
WINDOW = 512
Q_BLOCK = 128
ROPE_DIM = HD // 4
ROPE_THETA = 500000.0
GLA_HEADS = 4
GLA_DK = 64
GLA_DV = 128
GLA_LOWRANK = 16
GLA_TAU = 16.0
D_RNN = 1280
LRU_BLOCKS = 10
LRU_BW = D_RNN // LRU_BLOCKS
CONV_W = 4
LRU_C = 8.0
N_GROUPS = 4
EXP_PER_GROUP = 8
N_EXPERTS = N_GROUPS * EXP_PER_GROUP
E_HID = 512
TOP_K_IN_GROUP = 2
NSA_Q_W = NSA_HEADS * HD
NSA_KV_W = 2 * NSA_KV * HD
GLA_K_W = GLA_HEADS * GLA_DK
GLA_V_W = GLA_HEADS * GLA_DV
A_SIZES = (NSA_Q_W, NSA_KV_W, NSA_KV_W, NSA_KV_W, 3 * NSA_HEADS, GLA_K_W, GLA_K_W, GLA_V_W, GLA_LOWRANK, GLA_V_W)
N_PROMPT = BATCH * SEQ
N_SAMPLE = DEC_BATCH * DEC_SEQ
N_TOK = N_PROMPT + N_SAMPLE
N_PAGES = PAST_LEN // PAGE_SIZE

F32 = jnp.float32
BF16 = jnp.bfloat16
VMEM_LIMIT_BYTES = 56 * 1024 * 1024
LANES = 128
SUBLANES = 8
ROW_TILE = 512


def _params(n_axes):
    return pltpu.CompilerParams(dimension_semantics=("arbitrary",) * n_axes, vmem_limit_bytes=VMEM_LIMIT_BYTES)


def _full(shape):
    return pl.BlockSpec(shape, lambda *_: (0,) * len(shape))


def _rows(width):
    return pl.BlockSpec((ROW_TILE, width), lambda i: (i, 0))


def _rms(x, g):
    return x * lax.rsqrt(jnp.mean(x * x, axis=-1, keepdims=True) + EPS) * g


def _softplus(x):
    return jnp.maximum(x, 0.0) + jnp.log1p(jnp.exp(-jnp.abs(x)))


def _gelu_tanh(x):
    return x * (0.5 * (1.0 + jnp.tanh(0.7978845608028654 * (x + 0.044715 * (x * x * x)))))


def _nt_dot(a, b):
    return lax.dot_general(a, b, (((1,), (1,)), ((), ())), preferred_element_type=F32)


def _tn_dot(a, b):
    return lax.dot_general(a, b, (((0,), (0,)), ((), ())), preferred_element_type=F32)


A_Q0, A_KVC0, A_KVS0, A_KVW0 = 0, 512, 768, 1024
A_GQ0, A_GK0, A_GV0, A_GG0, A_MISC0 = 1280, 1536, 1792, 2304, 2816
A_COLS = A_MISC0 + LANES
MISC_LR0 = 3 * NSA_HEADS
KV_HALF = NSA_KV * HD


def _rope_lanes(x, cos_t, sin_lo, sin_hi):
    reps = x.shape[1] // LANES
    tile = (lambda t: jnp.concatenate([t] * reps, axis=1)) if reps > 1 else (lambda t: t)
    w = x.shape[1]
    half = ROPE_DIM // 2
    return x * tile(cos_t) + pltpu.roll(x, half, 1) * tile(sin_hi) + pltpu.roll(x, w - half, 1) * tile(sin_lo)


def _inproj_a_body(h_ref, g_ref, w_ref, wa2_ref, ba_ref, cos_ref, slo_ref, shi_ref,
                   q_ref, kvc_ref, kvs_ref, kvw_ref, kvct_ref, kvst_ref, kvwt_ref, ks_ref, vst_ref, kw_ref,
                   vwt_ref, gates_ref, gq_ref, gk_ref, gv_ref, la_ref, sgg_ref):
    y = _rms(h_ref[...], g_ref[...]).astype(BF16)
    proj = lambda a, b: jnp.dot(y, w_ref[:, a:b], preferred_element_type=F32)
    cos_t, sin_lo, sin_hi = cos_ref[...], slo_ref[...], shi_ref[...]
    q_ref[...] = (_rope_lanes(proj(A_Q0, A_KVC0), cos_t, sin_lo, sin_hi) * (HD ** -0.5)).astype(BF16)
    kvc = proj(A_KVC0, A_KVS0)
    kvc_ref[...] = kvc
    kvct_ref[...] = kvc.T
    for a0, kv_ref, kvt_ref, k_ref, vt_ref in ((A_KVS0, kvs_ref, kvst_ref, ks_ref, vst_ref),
                                               (A_KVW0, kvw_ref, kvwt_ref, kw_ref, vwt_ref)):
        kv = proj(a0, a0 + 2 * KV_HALF)
        k = _rope_lanes(kv[:, :KV_HALF], cos_t, sin_lo, sin_hi)
        vt = kv[:, KV_HALF:].T
        kv_ref[:, :KV_HALF] = k
        kv_ref[:, KV_HALF:] = kv[:, KV_HALF:]
        kvt_ref[:KV_HALF, :] = k.T
        kvt_ref[KV_HALF:, :] = vt
        k_ref[...] = k.astype(BF16)
        vt_ref[...] = vt.astype(BF16)
    misc = proj(A_MISC0, A_COLS)
    gates_ref[...] = jax.nn.sigmoid(misc)
    z = jnp.dot(misc.astype(BF16), wa2_ref[...], preferred_element_type=F32) + ba_ref[...]
    la_ref[...] = -_softplus(-z) * (1.0 / GLA_TAU)
    gq_ref[...] = proj(A_GQ0, A_GK0) * (GLA_DK ** -0.5)
    gk_ref[...] = proj(A_GK0, A_GV0)
    gv_ref[...] = proj(A_GV0, A_GG0).astype(BF16)
    gg = proj(A_GG0, A_MISC0)
    sgg_ref[...] = gg * jax.nn.sigmoid(gg)


def _rope_tables(pos):
    half = ROPE_DIM // 2
    inv = 1.0 / (ROPE_THETA ** (jnp.arange(0, ROPE_DIM, 2, dtype=F32) / ROPE_DIM))
    ang = pos.astype(F32)[:, None] * inv[None, :]
    cos, sin = jnp.cos(ang), jnp.sin(ang)
    n = pos.shape[0]
    one = jnp.ones((n, HD - ROPE_DIM), F32)
    zero = jnp.zeros((n, HD - ROPE_DIM), F32)
    zh = jnp.zeros((n, half), F32)
    seg = lambda a, b, rest: jnp.concatenate([a, b, rest] * (LANES // HD), axis=1)
    return seg(cos, cos, one), seg(-sin, zh, zero), seg(zh, sin, zero)


def inproj_a(h, g, w_in, wa2, ba, pos):
    n = h.shape[0]
    zpad = jnp.zeros((D_MODEL, LANES - 3 * NSA_HEADS - GLA_LOWRANK), F32)
    o = np.cumsum((0,) + A_SIZES)
    w = jnp.concatenate([w_in[:, o[0]:o[4]], w_in[:, o[5]:o[8]], w_in[:, o[9]:o[10]],
                         w_in[:, o[4]:o[5]], w_in[:, o[8]:o[9]], zpad], axis=1).astype(BF16)
    wa2p = jnp.zeros((LANES, GLA_K_W), F32).at[MISC_LR0:MISC_LR0 + GLA_LOWRANK].set(wa2).astype(BF16)
    cols = pl.BlockSpec((KV_HALF, ROW_TILE), lambda i: (0, i))
    kvt = (pl.BlockSpec((NSA_KV_W, ROW_TILE), lambda i: (0, i)), (NSA_KV_W, n), F32)
    outs = [(_rows(NSA_Q_W), (n, NSA_Q_W), BF16), (_rows(NSA_KV_W), (n, NSA_KV_W), F32),
            (_rows(NSA_KV_W), (n, NSA_KV_W), F32), (_rows(NSA_KV_W), (n, NSA_KV_W), F32), kvt, kvt, kvt,
            (_rows(KV_HALF), (n, KV_HALF), BF16), (cols, (KV_HALF, n), BF16),
            (_rows(KV_HALF), (n, KV_HALF), BF16), (cols, (KV_HALF, n), BF16),
            (_rows(LANES), (n, LANES), F32), (_rows(GLA_K_W), (n, GLA_K_W), F32),
            (_rows(GLA_K_W), (n, GLA_K_W), F32), (_rows(GLA_V_W), (n, GLA_V_W), BF16),
            (_rows(GLA_K_W), (n, GLA_K_W), F32), (_rows(GLA_V_W), (n, GLA_V_W), F32)]
    return pl.pallas_call(
        _inproj_a_body,
        grid=(n // ROW_TILE,),
        in_specs=[_rows(D_MODEL), _full((1, D_MODEL)), _full((D_MODEL, A_COLS)), _full((LANES, GLA_K_W)),
                  _full((1, GLA_K_W)), _rows(LANES), _rows(LANES), _rows(LANES)],
        out_specs=[s for s, _, _ in outs],
        out_shape=[jax.ShapeDtypeStruct(shape, dt) for _, shape, dt in outs],
        compiler_params=_params(1),
        name="inproj_a",
    )(h, g.reshape(1, D_MODEL), w, wa2p, ba.reshape(1, GLA_K_W), *_rope_tables(pos))


CMP_TILE_BLOCKS = 256
CMP_TILE_ROWS = CMP_TILE_BLOCKS * CMP_BLOCK
CMP_TILE_PAGES = CMP_TILE_ROWS // PAGE_SIZE


def _cmp_reduce(xk_ref, xv_ref, pe_ref, w_ref):
    acc = jnp.zeros((CMP_TILE_BLOCKS, NSA_KV_W), F32)
    for l in range(CMP_BLOCK):
        rows = pl.ds(l, CMP_TILE_BLOCKS, stride=CMP_BLOCK)
        xl = jnp.concatenate([xk_ref[rows, :], xv_ref[rows, :]], axis=1) + pe_ref[l:l + 1, :]
        acc = acc + jnp.dot(xl.astype(BF16), w_ref[l], preferred_element_type=F32)
    return acc


def _cmp_rows_body(xk_ref, xv_ref, pe_ref, w_ref, o_ref):
    o_ref[...] = _cmp_reduce(xk_ref, xv_ref, pe_ref, w_ref)


def _cmp_pages_body(x_ref, pe_ref, w_ref, o_ref, xk_sc, xv_sc):
    for pg in range(CMP_TILE_PAGES):
        rows = slice(pg * PAGE_SIZE, (pg + 1) * PAGE_SIZE)
        xk_sc[rows, :] = x_ref[pg, :KV_HALF, :].T
        xv_sc[rows, :] = x_ref[pg, KV_HALF:, :].T
    o_ref[...] = _cmp_reduce(xk_sc, xv_sc, pe_ref, w_ref)


def _cmp_weights(pe, w_cmp):
    pe_rows = jnp.broadcast_to(pe[:, :, None, :], (CMP_BLOCK, 2, NSA_KV, HD)).reshape(CMP_BLOCK, NSA_KV_W)
    w_bd = jnp.einsum('lcde,cx,gy->lcgdxye', w_cmp, jnp.eye(2, dtype=F32), jnp.eye(NSA_KV, dtype=F32))
    return pe_rows, w_bd.reshape(CMP_BLOCK, NSA_KV_W, NSA_KV_W).astype(BF16)


def cmp_blocks_rows(x, n_rows, pe, w_cmp):
    return pl.pallas_call(
        _cmp_rows_body,
        grid=(n_rows // CMP_TILE_ROWS,),
        in_specs=[pl.BlockSpec((CMP_TILE_ROWS, KV_HALF), lambda i: (i, 0)),
                  pl.BlockSpec((CMP_TILE_ROWS, KV_HALF), lambda i: (i, 1)), _full((CMP_BLOCK, NSA_KV_W)),
                  _full((CMP_BLOCK, NSA_KV_W, NSA_KV_W))],
        out_specs=pl.BlockSpec((CMP_TILE_BLOCKS, NSA_KV_W), lambda i: (i, 0)),
        out_shape=jax.ShapeDtypeStruct((n_rows // CMP_BLOCK, NSA_KV_W), F32),
        compiler_params=_params(1),
        name="cmp_blocks_rows",
    )(x, x, *_cmp_weights(pe, w_cmp))


def cmp_blocks_pages(xt, pe, w_cmp):
    n_pages = xt.shape[0]
    return pl.pallas_call(
        _cmp_pages_body,
        grid=(n_pages // CMP_TILE_PAGES,),
        in_specs=[pl.BlockSpec((CMP_TILE_PAGES, NSA_KV_W, PAGE_SIZE), lambda i: (i, 0, 0)),
                  _full((CMP_BLOCK, NSA_KV_W)), _full((CMP_BLOCK, NSA_KV_W, NSA_KV_W))],
        out_specs=pl.BlockSpec((CMP_TILE_BLOCKS, NSA_KV_W), lambda i: (i, 0)),
        out_shape=jax.ShapeDtypeStruct((n_pages * PAGE_SIZE // CMP_BLOCK, NSA_KV_W), F32),
        scratch_shapes=[pltpu.VMEM((CMP_TILE_ROWS, KV_HALF), F32), pltpu.VMEM((CMP_TILE_ROWS, KV_HALF), F32)],
        compiler_params=_params(1),
        name="cmp_blocks_pages",
    )(xt, *_cmp_weights(pe, w_cmp))


KEY_TILE = 256
NEG_BIG = -1e30
N_CMP_PROMPT = SEQ // CMP_BLOCK
N_SLC_PROMPT = SEQ // SLC_BLOCK
SLC_SHIFT = SLC_BLOCK.bit_length() - 1
N_CMP_DEC = PAST_LEN // CMP_BLOCK
N_SLC_DEC = -(-(PAST_LEN + DEC_SEQ) // SLC_BLOCK)
N_SLC_DEC_PAD = -(-N_SLC_DEC // LANES) * LANES
DEC_COLS_PER_GROUP = NSA_REP * DEC_SEQ


def _tile_cols(x, reps):
    return jnp.concatenate([x] * reps, axis=1) if reps > 1 else x


def _even_first_cmp_end(n_cmp):
    j = lax.broadcasted_iota(jnp.int32, (n_cmp, 1), 0)
    blk = jnp.where(j < n_cmp // 2, 2 * j, 2 * (j - n_cmp // 2) + 1)
    return (blk + 1) * CMP_BLOCK - 1


def _softmax_cols(s, mask):
    s = jnp.where(mask, s, -jnp.inf)
    m = jnp.max(s, axis=0, keepdims=True)
    m = jnp.where(m > -jnp.inf, m, 0.0)
    e = jnp.where(mask, jnp.exp(s - m), 0.0)
    return e / jnp.maximum(jnp.sum(e, axis=0, keepdims=True), 1e-30)


def _select_cols(p_slc, qpos, n_top):
    ns = p_slc.shape[0]
    blk = lax.broadcasted_iota(jnp.int32, p_slc.shape, 0)
    cur = qpos >> SLC_SHIFT
    forced = (blk == 0) | (blk == cur) | (blk == cur - 1)
    score = jnp.where(forced, jnp.inf, p_slc)
    score = jnp.where(blk <= cur, score, -jnp.inf)
    sel = jnp.zeros(p_slc.shape, F32)
    for _ in range(n_top):
        m = jnp.max(score, axis=0, keepdims=True)
        idx = jnp.min(jnp.where(score == m, blk, ns), axis=0, keepdims=True)
        hit = blk == idx
        sel = jnp.where(hit & (m > -jnp.inf), 1.0, sel)
        score = jnp.where(hit, -jnp.inf, score)
    return sel


def _flash_init(m_sc, l_sc, acc_sc):
    m_sc[...] = jnp.full(m_sc.shape, NEG_BIG, F32)
    l_sc[...] = jnp.zeros(l_sc.shape, F32)
    acc_sc[...] = jnp.zeros(acc_sc.shape, F32)


def _flash_cols(scores, mask, pv, m_sc, l_sc, acc_sc):
    s = jnp.where(mask, scores, NEG_BIG)
    m_old = m_sc[...]
    m_new = jnp.maximum(m_old, jnp.max(s, axis=0, keepdims=True))
    alpha = jnp.exp(m_old - m_new)
    p = jnp.where(mask, jnp.exp(s - m_new), 0.0)
    l_sc[...] = alpha * l_sc[...] + jnp.sum(p, axis=0, keepdims=True)
    acc_sc[...] = alpha * acc_sc[...] + pv(p.astype(BF16))
    m_sc[...] = m_new


def _flash_stream(score_fn, pv_fn, first, lo, n, stream_sc, m_sc, l_sc, acc_sc):
    (sa, ca), (sb, cb) = stream_sc

    def issue(s_ref, c_ref, kt, dead, self_tile=False):
        s = score_fn(kt, dead, self_tile)
        s_ref[...] = s
        c_ref[...] = jnp.max(s, axis=0, keepdims=True)

    def consume(s_ref, c_ref, kt):
        m_old = m_sc[...]
        m_new = jnp.maximum(m_old, c_ref[...])
        alpha = jnp.exp(m_old - m_new)
        p = jnp.exp(s_ref[...] - m_new)
        l_sc[...] = alpha * l_sc[...] + jnp.sum(p, axis=0, keepdims=True)
        acc_sc[...] = alpha * acc_sc[...] + pv_fn(kt, p.astype(BF16))
        m_sc[...] = m_new

    _flash_init(m_sc, l_sc, acc_sc)
    issue(sa, ca, first, 0.0, True)

    def two_tiles(jj, kt_a):
        t0 = lo + 2 * jj
        t1 = jnp.minimum(t0 + 1, lo + n - 1)
        issue(sb, cb, t0, 0.0)
        consume(sa, ca, kt_a)
        issue(sa, ca, t1, jnp.where(2 * jj + 1 < n, 0.0, NEG_BIG))
        consume(sb, cb, t0)
        return t1

    kt_a = lax.fori_loop(0, (n + 1) // 2, two_tiles, first)
    consume(sa, ca, kt_a)
    return acc_sc[...] / jnp.maximum(l_sc[...], 1e-30)


def _flash_out(l_sc, acc_sc):
    return acc_sc[...] / jnp.maximum(l_sc[...], 1e-30)


def _pv_split(vt, p):
    c2 = p.shape[1] // 2
    return jnp.concatenate([jnp.dot(vt[:HD], p[:, :c2], preferred_element_type=F32),
                            jnp.dot(vt[HD:], p[:, c2:], preferred_element_type=F32)], axis=1)


def _nsa_prompt_body(q_ref, gates_ref, ckv_ref, cos_ref, slo_ref, shi_ref, ks_ref, vst_ref, kw_ref, vwt_ref,
                     o_ref, m_sc, l_sc, acc_sc, sel_bias_sc, sa_sc, ca_sc, sb_sc, cb_sc):
    stream_sc = ((sa_sc, ca_sc), (sb_sc, cb_sc))
    i = pl.program_id(1)
    nq = Q_BLOCK
    cols = NSA_HEADS * nq
    qpos = i * nq + lax.broadcasted_iota(jnp.int32, (1, nq), 1)
    q = q_ref[...].astype(F32)
    pairs = [q[:, j * LANES:(j + 1) * LANES].T for j in range(NSA_HEADS // 2)]
    zero = jnp.zeros((HD, cols // 2), F32)
    qt_g = [jnp.concatenate([pairs[2 * g][:HD], pairs[2 * g][HD:], pairs[2 * g + 1][:HD], pairs[2 * g + 1][HD:]],
                            axis=1) for g in range(NSA_KV)]
    qt = jnp.concatenate([jnp.concatenate([qt_g[0], zero], axis=1),
                          jnp.concatenate([zero, qt_g[1]], axis=1)], axis=0).astype(BF16)
    ckv = ckv_ref[0]
    ck = _rope_lanes(ckv[:, :KV_HALF], cos_ref[...], slo_ref[...], shi_ref[...]).astype(BF16)
    cvt = ckv[:, KV_HALF:].T.astype(BF16)
    c_mask = _even_first_cmp_end(N_CMP_PROMPT) <= qpos
    p = _softmax_cols(jnp.dot(ck, qt, preferred_element_type=F32), _tile_cols(c_mask, NSA_HEADS))
    o_c = _pv_split(cvt, p.astype(BF16))
    sel = []
    for g in range(NSA_KV):
        c0 = g * NSA_REP * nq
        p_grp = p[:, c0:c0 + nq]
        for r in range(1, NSA_REP):
            p_grp = p_grp + p[:, c0 + r * nq:c0 + (r + 1) * nq]
        p_slc = p_grp[:N_CMP_PROMPT // 2] + p_grp[N_CMP_PROMPT // 2:]
        sel.append(_select_cols(p_slc, qpos, SLC_TOPK))
    sel = jnp.concatenate(sel, axis=1)
    sel_bias_sc[...] = jnp.where(sel > 0.5, 0.0, NEG_BIG)
    key_row = lax.broadcasted_iota(jnp.int32, (KEY_TILE, 1), 0)
    blocks_per_tile = KEY_TILE // SLC_BLOCK
    kt_self = i // (KEY_TILE // nq)

    def slc_scores(kt, dead, self_tile):
        k0 = pl.multiple_of(kt * KEY_TILE, KEY_TILE)
        rows = []
        for j in range(blocks_per_tile):
            b = sel_bias_sc[pl.ds(kt * blocks_per_tile + j, 1), :] + dead
            rows.append(jnp.concatenate(
                [jnp.broadcast_to(b[:, g * nq:(g + 1) * nq], (SLC_BLOCK, nq)) for g in range(NSA_KV)
                 for _ in range(NSA_REP)], axis=1))
        bias = jnp.concatenate(rows, axis=0)
        if self_tile:
            bias = bias + _tile_cols(jnp.where(k0 + key_row <= qpos, 0.0, NEG_BIG), NSA_HEADS)
        return jnp.dot(ks_ref[pl.ds(k0, KEY_TILE), :], qt, preferred_element_type=F32) + bias

    def slc_pv(kt, pb):
        return _pv_split(vst_ref[:, pl.ds(pl.multiple_of(kt * KEY_TILE, KEY_TILE), KEY_TILE)], pb)

    o_s = _flash_stream(slc_scores, slc_pv, kt_self, 0, kt_self, stream_sc, m_sc, l_sc, acc_sc)

    def win_scores(kt, dead, self_tile):
        k0 = pl.multiple_of(kt * KEY_TILE, KEY_TILE)
        d = qpos - (k0 + key_row)
        bias = jnp.where((d >= 0) & (d < WINDOW), 0.0, NEG_BIG) + dead
        return jnp.dot(kw_ref[pl.ds(k0, KEY_TILE), :], qt, preferred_element_type=F32) + _tile_cols(bias, NSA_HEADS)

    def win_pv(kt, pb):
        return _pv_split(vwt_ref[:, pl.ds(pl.multiple_of(kt * KEY_TILE, KEY_TILE), KEY_TILE)], pb)

    win_lo = jnp.maximum(i - WINDOW // nq, 0) // (KEY_TILE // nq)
    o_w = _flash_stream(win_scores, win_pv, kt_self, win_lo, kt_self - win_lo, stream_sc, m_sc, l_sc, acc_sc)
    gates_t = gates_ref[...].T
    merged = []
    for hd in range(NSA_HEADS):
        cs = slice(hd * nq, (hd + 1) * nq)
        gate = lambda br: gates_t[3 * hd + br:3 * hd + br + 1]
        merged.append(gate(0) * o_c[:, cs] + gate(1) * o_s[:, cs] + gate(2) * o_w[:, cs])
    for j in range(NSA_HEADS // 2):
        pair = jnp.concatenate([merged[2 * j], merged[2 * j + 1]], axis=0).T
        o_ref[:, j * LANES:(j + 1) * LANES] = pair.astype(BF16)


def nsa_prompt_attn(q, gates, ckv, ks, vst, kw, vwt):
    nqb = SEQ // Q_BLOCK
    tok = lambda wd: pl.BlockSpec((Q_BLOCK, wd), lambda b, i: (b * nqb + i, 0))
    seq_rows = pl.BlockSpec((SEQ, KV_HALF), lambda b, i: (b, 0))
    seq_cols = pl.BlockSpec((KV_HALF, SEQ), lambda b, i: (0, b))
    n_cmp = N_CMP_PROMPT
    c_blk = jnp.concatenate([jnp.arange(0, n_cmp, 2), jnp.arange(1, n_cmp, 2)]).astype(jnp.int32)
    cols = NSA_HEADS * Q_BLOCK
    return pl.pallas_call(
        _nsa_prompt_body,
        grid=(BATCH, nqb),
        in_specs=[tok(NSA_Q_W), tok(LANES), pl.BlockSpec((1, n_cmp, NSA_KV_W), lambda b, i: (b, 0, 0)),
                  _full((n_cmp, LANES)), _full((n_cmp, LANES)), _full((n_cmp, LANES)),
                  seq_rows, seq_cols, seq_rows, seq_cols],
        out_specs=tok(NSA_Q_W),
        out_shape=jax.ShapeDtypeStruct((N_PROMPT, NSA_Q_W), BF16),
        scratch_shapes=[pltpu.VMEM((1, cols), F32), pltpu.VMEM((1, cols), F32), pltpu.VMEM((HD, cols), F32),
                        pltpu.VMEM((N_SLC_PROMPT, NSA_KV * Q_BLOCK), F32),
                        pltpu.VMEM((KEY_TILE, cols), F32), pltpu.VMEM((1, cols), F32),
                        pltpu.VMEM((KEY_TILE, cols), F32), pltpu.VMEM((1, cols), F32)],
        compiler_params=_params(2),
        name="nsa_prompt",
    )(q, gates, ckv, *_rope_tables((c_blk + 1) * CMP_BLOCK - 1), ks, vst, kw, vwt)


def _nsa_decode_body(pt_ref, qt_ref, gate_ref, ckv_ref, cos_ref, slo_ref, shi_ref, hsum_ref, *refs):
    page_refs = refs[:N_PAGES]
    win_ref, kvs_new_ref, kvw_new_ref, o_ref, win_out_ref, m_sc, l_sc, acc_sc = refs[N_PAGES:]
    qt = qt_ref[0]
    lane = lax.broadcasted_iota(jnp.int32, (1, LANES), 1)
    qpos = PAST_LEN + (lane & (DEC_SEQ - 1))
    group0 = lane < DEC_COLS_PER_GROUP
    ckv = ckv_ref[0]
    ck = _rope_lanes(ckv[:, :KV_HALF], cos_ref[...], slo_ref[...], shi_ref[...]).astype(BF16)
    c_mask = _even_first_cmp_end(N_CMP_DEC) <= qpos
    p = _softmax_cols(jnp.dot(ck, qt, preferred_element_type=F32), c_mask)
    o_c = _tn_dot(ckv[:, KV_HALF:].astype(BF16), p.astype(BF16))
    p_grp = jnp.dot(p, hsum_ref[...], preferred_element_type=F32, precision=lax.Precision.HIGHEST)
    p_slc = jnp.concatenate([p_grp[:N_CMP_DEC // 2] + p_grp[N_CMP_DEC // 2:],
                             jnp.zeros((N_SLC_DEC_PAD - N_CMP_DEC // 2, LANES), F32)], axis=0)
    sel = _select_cols(p_slc, qpos, SLC_TOPK)

    def new_rows(ref):
        row = ref[0]
        kv = jnp.concatenate([row[:, t * NSA_KV_W:(t + 1) * NSA_KV_W] for t in range(DEC_SEQ)], axis=0)
        return jnp.concatenate([kv, jnp.zeros((SUBLANES - DEC_SEQ, NSA_KV_W), F32)], axis=0)

    new_row = lax.broadcasted_iota(jnp.int32, (SUBLANES, 1), 0)
    new_pos = PAST_LEN + new_row
    new_valid = new_row < DEC_SEQ
    _flash_init(m_sc, l_sc, acc_sc)
    kt_old = jnp.concatenate([r[0, :KV_HALF, :] for r in page_refs], axis=1).astype(BF16)
    vt_old = jnp.concatenate([r[0, KV_HALF:, :] for r in page_refs], axis=1).astype(BF16)
    key_blk = lax.broadcasted_iota(jnp.int32, (PAST_LEN, N_SLC_DEC_PAD), 0) >> SLC_SHIFT
    blk_col = lax.broadcasted_iota(jnp.int32, (PAST_LEN, N_SLC_DEC_PAD), 1)
    chosen = jnp.dot(jnp.where(key_blk == blk_col, 1.0, 0.0).astype(BF16), sel.astype(BF16),
                     preferred_element_type=F32) > 0.5
    old_pos = lax.broadcasted_iota(jnp.int32, (PAST_LEN, 1), 0)
    _flash_cols(_tn_dot(kt_old, qt), chosen & (old_pos <= qpos),
                lambda pb: jnp.dot(vt_old, pb, preferred_element_type=F32), m_sc, l_sc, acc_sc)
    kv_new = new_rows(kvs_new_ref)
    sel_new = sel[(PAST_LEN >> SLC_SHIFT):(PAST_LEN >> SLC_SHIFT) + 1] > 0.5
    v_new = kv_new[:, KV_HALF:].astype(BF16)
    _flash_cols(jnp.dot(kv_new[:, :KV_HALF].astype(BF16), qt, preferred_element_type=F32),
                sel_new & new_valid & (new_pos <= qpos), lambda pb: _tn_dot(v_new, pb), m_sc, l_sc, acc_sc)
    o_s = _flash_out(l_sc, acc_sc)
    _flash_init(m_sc, l_sc, acc_sc)
    n_buf = win_ref.shape[2]
    win = win_ref[0]
    d = qpos - (PAST_LEN - n_buf + lax.broadcasted_iota(jnp.int32, (n_buf, 1), 0))
    vt_win = win[KV_HALF:].astype(BF16)
    _flash_cols(_tn_dot(win[:KV_HALF].astype(BF16), qt), (d >= 0) & (d < WINDOW),
                lambda pb: jnp.dot(vt_win, pb, preferred_element_type=F32), m_sc, l_sc, acc_sc)
    kw_new = new_rows(kvw_new_ref)
    d = qpos - new_pos
    vw_new = kw_new[:, KV_HALF:].astype(BF16)
    _flash_cols(jnp.dot(kw_new[:, :KV_HALF].astype(BF16), qt, preferred_element_type=F32),
                new_valid & (d >= 0) & (d < WINDOW), lambda pb: _tn_dot(vw_new, pb), m_sc, l_sc, acc_sc)
    o_w = _flash_out(l_sc, acc_sc)
    g = gate_ref[0]
    o = g[0:1] * o_c + g[1:2] * o_s + g[2:3] * o_w
    o_ref[0] = jnp.where(group0, o[:HD], o[HD:])
    key = lax.broadcasted_iota(jnp.int32, (SUBLANES, n_buf), 1)
    place = jnp.where((key == n_buf - DEC_SEQ + new_row) & new_valid, 1.0, 0.0)
    placed = lax.dot_general(kw_new, place, (((0,), (0,)), ((), ())), preferred_element_type=F32,
                             precision=lax.Precision.HIGHEST)
    keep = lax.broadcasted_iota(jnp.int32, (1, n_buf), 1) < n_buf - DEC_SEQ
    win_out_ref[0] = jnp.where(keep, pltpu.roll(win, n_buf - DEC_SEQ, 1), placed)


def nsa_decode_attn(qt, gate_rows, ckv, slc_pool, win_buf, kvs_new, kvw_new, page_table):
    n_buf = win_buf.shape[2]
    per_b = lambda *shape: pl.BlockSpec((1,) + shape, lambda b, pt: (b,) + (0,) * len(shape))
    const = lambda *shape: pl.BlockSpec(shape, lambda b, pt: (0,) * len(shape))
    page = lambda j: pl.BlockSpec((1, NSA_KV_W, PAGE_SIZE), lambda b, pt: (pt[b, j], 0, 0))
    c_blk = jnp.concatenate([jnp.arange(0, N_CMP_DEC, 2), jnp.arange(1, N_CMP_DEC, 2)]).astype(jnp.int32)
    col = jnp.arange(LANES)
    used = col < NSA_KV * DEC_COLS_PER_GROUP
    same = (col[:, None] // DEC_COLS_PER_GROUP == col[None, :] // DEC_COLS_PER_GROUP) & \
           (col[:, None] % DEC_SEQ == col[None, :] % DEC_SEQ) & used[:, None] & used[None, :]
    return pl.pallas_call(
        _nsa_decode_body,
        grid_spec=pltpu.PrefetchScalarGridSpec(
            num_scalar_prefetch=1,
            grid=(DEC_BATCH,),
            in_specs=[per_b(LANES, LANES), per_b(3, LANES), per_b(N_CMP_DEC, NSA_KV_W),
                      const(N_CMP_DEC, LANES), const(N_CMP_DEC, LANES), const(N_CMP_DEC, LANES),
                      const(LANES, LANES)] + [page(j) for j in range(N_PAGES)]
                     + [per_b(NSA_KV_W, n_buf), per_b(1, DEC_SEQ * NSA_KV_W), per_b(1, DEC_SEQ * NSA_KV_W)],
            out_specs=[per_b(HD, LANES), per_b(NSA_KV_W, n_buf)],
            scratch_shapes=[pltpu.VMEM((1, LANES), F32), pltpu.VMEM((1, LANES), F32),
                            pltpu.VMEM((LANES, LANES), F32)],
        ),
        out_shape=[jax.ShapeDtypeStruct((DEC_BATCH, HD, LANES), F32),
                   jax.ShapeDtypeStruct((DEC_BATCH, NSA_KV_W, n_buf), F32)],
        compiler_params=_params(1),
        name="nsa_decode",
    )(page_table, qt, gate_rows, ckv, *_rope_tables((c_blk + 1) * CMP_BLOCK - 1), same.astype(F32),
      *([slc_pool] * N_PAGES), win_buf, kvs_new, kvw_new)


GLA_PAIRS = GLA_HEADS // 2
GLA_ROWS = 128
GLA_SUB = 16
GLA_STEP_SEQS = 8


def _gla_rows(q, k, la, v, sgg, gnorm, st_ref, sub):
    R = q.shape[0]
    row = lax.broadcasted_iota(jnp.int32, (R, GLA_K_W), 0)
    rin = row % sub
    cum = la
    d = 1
    while d < sub:
        cum = cum + jnp.where(rin >= d, pltpu.roll(cum, d, 0), 0.0)
        d *= 2
    lane = lax.broadcasted_iota(jnp.int32, (sub, LANES), 1)
    lo = lane < GLA_DK
    rsub = lax.broadcasted_iota(jnp.int32, (sub, LANES), 0)
    out_rows = []
    for c in range(R // sub):
        rs = slice(c * sub, (c + 1) * sub)
        cum_c = cum[rs]
        last = cum_c[sub - 1:sub]
        qe = q[rs] * jnp.exp(cum_c)
        kdec = k[rs] * jnp.exp(last - cum_c)
        v_c = v[rs]
        heads = []
        for pr in range(GLA_PAIRS):
            ls = slice(pr * LANES, (pr + 1) * LANES)
            st = st_ref[pr]
            st_b = st.astype(BF16)
            qe_p, kd_p, q_p, k_p, cum_p = qe[:, ls], kdec[:, ls], q[rs, ls], k[rs, ls], cum_c[:, ls]
            v_pair = [v_c[:, (2 * pr + hh) * GLA_DV:(2 * pr + hh + 1) * GLA_DV] for hh in range(2)]
            upd = jnp.zeros((GLA_DV, LANES), F32)
            o_pair = []
            for hh in range(2):
                keep = lo if hh == 0 else jnp.logical_not(lo)
                o_pair.append(_nt_dot(jnp.where(keep, qe_p, 0.0).astype(BF16), st_b))
                upd = upd + _tn_dot(v_pair[hh].astype(BF16), jnp.where(keep, kd_p, 0.0).astype(BF16))
            for j in range(sub):
                dj = jnp.where(rsub >= j, jnp.exp(cum_p - cum_p[j:j + 1]), 0.0)
                w = q_p * k_p[j:j + 1] * dj
                a_lo = jnp.sum(jnp.where(lo, w, 0.0), axis=-1, keepdims=True)
                a_hi = jnp.sum(jnp.where(lo, 0.0, w), axis=-1, keepdims=True)
                o_pair[0] = o_pair[0] + a_lo * v_pair[0][j:j + 1]
                o_pair[1] = o_pair[1] + a_hi * v_pair[1][j:j + 1]
            st_ref[pr] = st * jnp.exp(last[:, ls]) + upd
            heads += o_pair
        out_rows.append(jnp.concatenate([_rms(x, gnorm) for x in heads], axis=1))
    return jnp.concatenate(out_rows, axis=0) * sgg


def _gla_seq_body(q_ref, k_ref, la_ref, v_ref, sgg_ref, gn_ref, o_ref, st_out_ref, st_sc):
    @pl.when(pl.program_id(1) == 0)
    def _():
        st_sc[...] = jnp.zeros(st_sc.shape, F32)

    o = _gla_rows(q_ref[...], k_ref[...], la_ref[...], v_ref[...].astype(F32), sgg_ref[...], gn_ref[...],
                  st_sc, GLA_SUB)
    o_ref[...] = o.astype(BF16)
    st_out_ref[0] = st_sc[...]


def gla_seq(q, k, la, v, sgg, gnorm):
    nt = SEQ // GLA_ROWS
    rows = lambda wd: pl.BlockSpec((GLA_ROWS, wd), lambda b, t: (b * nt + t, 0))
    return pl.pallas_call(
        _gla_seq_body,
        grid=(BATCH, nt),
        in_specs=[rows(GLA_K_W), rows(GLA_K_W), rows(GLA_K_W), rows(GLA_V_W), rows(GLA_V_W),
                  _full((1, GLA_DV))],
        out_specs=[rows(GLA_V_W), pl.BlockSpec((1, GLA_PAIRS, GLA_DV, LANES), lambda b, t: (b, 0, 0, 0))],
        out_shape=[jax.ShapeDtypeStruct((N_PROMPT, GLA_V_W), BF16),
                   jax.ShapeDtypeStruct((BATCH, GLA_PAIRS, GLA_DV, LANES), F32)],
        scratch_shapes=[pltpu.VMEM((GLA_PAIRS, GLA_DV, LANES), F32)],
        compiler_params=_params(2),
        name="gla_seq",
    )(q, k, la, v, sgg, gnorm.reshape(1, GLA_DV))


def _gla_step_body(q_ref, k_ref, la_ref, v_ref, sgg_ref, gn_ref, st_in_ref, o_ref, st_out_ref):
    st_out_ref[...] = st_in_ref[...]
    q, k, la, v, sgg = q_ref[...], k_ref[...], la_ref[...], v_ref[...].astype(F32), sgg_ref[...]
    for j in range(GLA_STEP_SEQS):
        rs = slice(j * DEC_SEQ, (j + 1) * DEC_SEQ)
        o = _gla_rows(q[rs], k[rs], la[rs], v[rs], sgg[rs], gn_ref[...], st_out_ref.at[j], DEC_SEQ)
        o_ref[rs, :] = o.astype(BF16)


def gla_step(q, k, la, v, sgg, gnorm, st_in, row0):
    rows_per = GLA_STEP_SEQS * DEC_SEQ
    blk0 = row0 // rows_per
    rows = lambda wd: pl.BlockSpec((rows_per, wd), lambda i: (blk0 + i, 0))
    st_spec = pl.BlockSpec((GLA_STEP_SEQS, GLA_PAIRS, GLA_DV, LANES), lambda i: (i, 0, 0, 0))
    return pl.pallas_call(
        _gla_step_body,
        grid=(DEC_BATCH // GLA_STEP_SEQS,),
        in_specs=[rows(GLA_K_W), rows(GLA_K_W), rows(GLA_K_W), rows(GLA_V_W), rows(GLA_V_W),
                  _full((1, GLA_DV)), st_spec],
        out_specs=[pl.BlockSpec((rows_per, GLA_V_W), lambda i: (i, 0)), st_spec],
        out_shape=[jax.ShapeDtypeStruct((N_SAMPLE, GLA_V_W), BF16),
                   jax.ShapeDtypeStruct((DEC_BATCH, GLA_PAIRS, GLA_DV, LANES), F32)],
        compiler_params=_params(1),
        name="gla_step",
    )(q, k, la, v, sgg, gnorm.reshape(1, GLA_DV), st_in)


def _gla_state_to_pairs(s):
    B = s.shape[0]
    return s.reshape(B, GLA_PAIRS, 2, GLA_DK, GLA_DV).transpose(0, 1, 4, 2, 3).reshape(B, GLA_PAIRS, GLA_DV, LANES)


def _gla_state_from_pairs(st):
    B = st.shape[0]
    return st.reshape(B, GLA_PAIRS, GLA_DV, 2, GLA_DK).transpose(0, 1, 3, 4, 2).reshape(B, GLA_HEADS, GLA_DK, GLA_DV)


LRU_TIME_TILE = 256


def _lru_gates(xc, wa_ref, ba, wx_ref, bx, lam):
    xcb = xc.astype(BF16)
    r_parts, i_parts = [], []
    for n in range(LRU_BLOCKS):
        xs = xcb[:, n * LRU_BW:(n + 1) * LRU_BW]
        r_parts.append(jnp.dot(xs, wa_ref[n], preferred_element_type=F32))
        i_parts.append(jnp.dot(xs, wx_ref[n], preferred_element_type=F32))
    r = jax.nn.sigmoid(jnp.concatenate(r_parts, axis=-1) + ba)
    i = jax.nn.sigmoid(jnp.concatenate(i_parts, axis=-1) + bx)
    log_a = -LRU_C * r * _softplus(-lam)
    a = jnp.exp(log_a)
    u = jnp.sqrt(-jnp.tanh(log_a) * (a * a + 1.0)) * (i * xc)
    return a, u


def _lru_seq_body(u_ref, cw_ref, cb_ref, wa_ref, ba_ref, wx_ref, bx_ref, lam_ref, y_ref, hT_ref, xp_sc, h_sc):
    tt = LRU_TIME_TILE

    @pl.when(pl.program_id(1) == 0)
    def _():
        xp_sc[0:SUBLANES, :] = jnp.zeros((SUBLANES, D_RNN), F32)
        h_sc[...] = jnp.zeros((1, D_RNN), F32)

    xp_sc[SUBLANES:SUBLANES + tt, :] = u_ref[:, D_RNN:]
    xc = cb_ref[...]
    for w in range(CONV_W):
        off = SUBLANES - (CONV_W - 1) + w
        xc = xc + cw_ref[w:w + 1, :] * xp_sc[off:off + tt, :]
    a, u = _lru_gates(xc, wa_ref, ba_ref[...], wx_ref, bx_ref[...], lam_ref[...])
    row = lax.broadcasted_iota(jnp.int32, (tt, D_RNN), 0)
    d = 1
    while d < tt:
        keep = row >= d
        a_prev = jnp.where(keep, pltpu.roll(a, d, 0), 1.0)
        u_prev = jnp.where(keep, pltpu.roll(u, d, 0), 0.0)
        u = a * u_prev + u
        a = a * a_prev
        d *= 2
    h = a * h_sc[...] + u
    h_sc[...] = h[tt - 1:tt, :]
    hT_ref[0] = h[tt - 1:tt, :]
    y_ref[...] = (_gelu_tanh(u_ref[:, :D_RNN]) * h).astype(BF16)
    xp_sc[0:SUBLANES, :] = xp_sc[tt:tt + SUBLANES, :]


def _lru_weight_args(cw, cb, wa, ba, wx, bx, lam):
    row = lambda v: v.reshape(1, D_RNN)
    return (cw, row(cb), wa.astype(BF16), row(ba), wx.astype(BF16), row(bx), row(lam))


_LRU_WEIGHT_SPECS = [_full((CONV_W, D_RNN)), _full((1, D_RNN)), _full((LRU_BLOCKS, LRU_BW, LRU_BW)),
                     _full((1, D_RNN)), _full((LRU_BLOCKS, LRU_BW, LRU_BW)), _full((1, D_RNN)),
                     _full((1, D_RNN))]


def lru_seq(u, cw, cb, wa, ba, wx, bx, lam):
    tt = LRU_TIME_TILE
    nt = SEQ // tt
    y, hT = pl.pallas_call(
        _lru_seq_body,
        grid=(BATCH, nt),
        in_specs=[pl.BlockSpec((tt, 2 * D_RNN), lambda b, t: (b * nt + t, 0))] + _LRU_WEIGHT_SPECS,
        out_specs=[pl.BlockSpec((tt, D_RNN), lambda b, t: (b * nt + t, 0)),
                   pl.BlockSpec((1, 1, D_RNN), lambda b, t: (b, 0, 0))],
        out_shape=[jax.ShapeDtypeStruct((N_PROMPT, D_RNN), BF16), jax.ShapeDtypeStruct((BATCH, 1, D_RNN), F32)],
        scratch_shapes=[pltpu.VMEM((tt + SUBLANES, D_RNN), F32), pltpu.VMEM((1, D_RNN), F32)],
        compiler_params=_params(2),
        name="lru_seq",
    )(u, *_lru_weight_args(cw, cb, wa, ba, wx, bx, lam))
    return y, hT.reshape(BATCH, D_RNN)


def _lru_step_body(u_ref, cs_ref, h0_ref, cw_ref, cb_ref, wa_ref, ba_ref, wx_ref, bx_ref, lam_ref, y_ref, hT_ref):
    n_t = u_ref.shape[0]
    hist = [cs_ref[:, w, :] for w in range(CONV_W - 1)] + [u_ref[t, :, D_RNN:] for t in range(n_t)]
    h = h0_ref[...]
    for t in range(n_t):
        xc = cb_ref[...]
        for w in range(CONV_W):
            xc = xc + cw_ref[w:w + 1, :] * hist[t + w]
        a, u = _lru_gates(xc, wa_ref, ba_ref[...], wx_ref, bx_ref[...], lam_ref[...])
        h = a * h + u
        y_ref[t] = (_gelu_tanh(u_ref[t, :, :D_RNN]) * h).astype(BF16)
    hT_ref[...] = h


def lru_step(u, conv_state, h0, cw, cb, wa, ba, wx, bx, lam):
    T, B, _ = u.shape
    return pl.pallas_call(
        _lru_step_body,
        out_shape=[jax.ShapeDtypeStruct((T, B, D_RNN), BF16), jax.ShapeDtypeStruct((B, D_RNN), F32)],
        compiler_params=pltpu.CompilerParams(vmem_limit_bytes=VMEM_LIMIT_BYTES),
        name="lru_step",
    )(u, conv_state, h0, *_lru_weight_args(cw, cb, wa, ba, wx, bx, lam))


ROUTE_E1, ROUTE_E2, ROUTE_G1, ROUTE_G2, ROUTE_R1, ROUTE_R2 = range(6)
EXPERT_LANE0 = N_GROUPS


def _lane_pick(val_by_lane):
    rows = next(iter(val_by_lane.values())).shape[0]
    lane = lax.broadcasted_iota(jnp.int32, (rows, LANES), 1)
    out = jnp.zeros((rows, LANES), F32)
    for l, v in val_by_lane.items():
        out = jnp.where(lane == l, v, out)
    return out


def _route_rows(logits, tri_ref, carry_ref):
    rows = logits.shape[0]
    lane = lax.broadcasted_iota(jnp.int32, (rows, LANES), 1)
    neg = -jnp.inf
    gl = jnp.where(lane < N_GROUPS, logits, neg)
    gmax = jnp.max(gl, axis=-1, keepdims=True)
    gtop = jnp.min(jnp.where(gl == gmax, lane, LANES), axis=-1, keepdims=True)
    gsum = jnp.sum(jnp.where(lane < N_GROUPS, jnp.exp(logits - gmax), 0.0), axis=-1, keepdims=True)
    g_w = 1.0 / gsum
    lo = EXPERT_LANE0 + EXP_PER_GROUP * gtop
    el = jnp.where((lane >= lo) & (lane < lo + EXP_PER_GROUP), logits, neg)
    v1 = jnp.max(el, axis=-1, keepdims=True)
    i1 = jnp.min(jnp.where(el == v1, lane, LANES), axis=-1, keepdims=True)
    el2 = jnp.where(lane == i1, neg, el)
    v2 = jnp.max(el2, axis=-1, keepdims=True)
    i2 = jnp.min(jnp.where(el2 == v2, lane, LANES), axis=-1, keepdims=True)
    p2 = jnp.exp(v2 - v1)
    den = 1.0 + p2
    gate1 = (1.0 / den) * g_w
    gate2 = (p2 / den) * g_w
    hit1 = lane == i1
    hit2 = lane == i2
    onehot = jnp.where(hit1 | hit2, 1.0, 0.0)
    before = jnp.dot(tri_ref[...], onehot.astype(BF16), preferred_element_type=F32) + carry_ref[...]
    rank1 = jnp.sum(jnp.where(hit1, before, 0.0), axis=-1, keepdims=True)
    rank2 = jnp.sum(jnp.where(hit2, before, 0.0), axis=-1, keepdims=True)
    carry_ref[...] = carry_ref[...] + jnp.sum(onehot, axis=0, keepdims=True)
    return _lane_pick({ROUTE_E1: (i1 - EXPERT_LANE0).astype(F32), ROUTE_E2: (i2 - EXPERT_LANE0).astype(F32),
                       ROUTE_G1: gate1, ROUTE_G2: gate2, ROUTE_R1: rank1, ROUTE_R2: rank2})


def _pack_bf16_halves(x):
    w = x.shape[1] // 2
    bits = lambda v: pltpu.bitcast(v.astype(F32), jnp.uint32)
    return (bits(x[:, :w]) >> 16) | bits(x[:, w:])


def _unpack_bf16_halves(p):
    lo = pltpu.bitcast(p << 16, F32).astype(BF16)
    hi = pltpu.bitcast(p & jnp.uint32(0xFFFF0000), F32).astype(BF16)
    return jnp.concatenate([lo, hi], axis=1)


def _post_mixer_body(n_mix, h_ref, *refs):
    m_refs, wo_refs = refs[:n_mix], refs[n_mix:2 * n_mix]
    g_ref, wr_ref, br_ref, tri_ref, h1_ref, hn_ref, route_ref, cnt_ref, carry_sc = refs[2 * n_mix:]

    @pl.when(pl.program_id(0) == 0)
    def _():
        carry_sc[...] = jnp.zeros((1, LANES), F32)

    mix = jnp.dot(m_refs[0][...], wo_refs[0][...], preferred_element_type=F32)
    for m_ref, wo_ref in zip(m_refs[1:], wo_refs[1:]):
        mix = mix + jnp.dot(m_ref[...], wo_ref[...], preferred_element_type=F32)
    h1 = h_ref[...] + mix
    h1_ref[...] = h1
    hn = _rms(h1, g_ref[...]).astype(BF16)
    hn_ref[...] = _pack_bf16_halves(hn)
    logits = jnp.dot(hn, wr_ref[...], preferred_element_type=F32) + br_ref[...]
    route_ref[...] = _route_rows(logits, tri_ref, carry_sc)
    cnt_ref[...] = carry_sc[...]


def post_mixer(h, mix_ins, w_out, g_ffn, w_rg, b_rg, w_re, b_re):
    n = h.shape[0]
    ks = [m.shape[1] for m in mix_ins]
    offs = np.cumsum([0] + ks)
    w_parts = [w_out[offs[j]:offs[j + 1]].astype(BF16) for j in range(len(ks))]
    pad = LANES - N_GROUPS - N_EXPERTS
    wr = jnp.concatenate([w_rg, w_re, jnp.zeros((D_MODEL, pad), F32)], axis=1).astype(BF16)
    br = jnp.concatenate([b_rg, b_re, jnp.zeros((pad,), F32)]).reshape(1, LANES)
    tri = jnp.tril(jnp.ones((ROW_TILE, ROW_TILE), BF16), -1)
    return pl.pallas_call(
        functools.partial(_post_mixer_body, len(ks)),
        grid=(n // ROW_TILE,),
        in_specs=[_rows(D_MODEL)] + [_rows(k) for k in ks] + [_full((k, D_MODEL)) for k in ks]
                 + [_full((1, D_MODEL)), _full((D_MODEL, LANES)), _full((1, LANES)), _full((ROW_TILE, ROW_TILE))],
        out_specs=[_rows(D_MODEL), _rows(D_MODEL // 2), _rows(LANES), _full((1, LANES))],
        out_shape=[jax.ShapeDtypeStruct((n, D_MODEL), F32), jax.ShapeDtypeStruct((n, D_MODEL // 2), jnp.uint32),
                   jax.ShapeDtypeStruct((n, LANES), F32), jax.ShapeDtypeStruct((1, LANES), F32)],
        scratch_shapes=[pltpu.VMEM((1, LANES), F32)],
        compiler_params=_params(1),
        name="post_mixer",
    )(h, *mix_ins, *w_parts, g_ffn.reshape(1, D_MODEL), wr, br, tri)


MOE_ROWS = 256


def _ffn_body(be_ref, nb_ref, x_ref, w1_ref, w3_ref, w2_ref, y_ref, w1_sc, w3_sc, w2_sc):
    i = pl.program_id(0)
    new_expert = jnp.logical_or(i == 0, be_ref[i] != be_ref[jnp.maximum(i - 1, 0)])

    @pl.when(jnp.logical_and(new_expert, i < nb_ref[0]))
    def _():
        w1_sc[...] = w1_ref[0, 0].astype(BF16)
        w3_sc[...] = w3_ref[0, 0].astype(BF16)
        w2_sc[...] = w2_ref[0, 0].astype(BF16)

    @pl.when(i < nb_ref[0])
    def _():
        x = _unpack_bf16_halves(x_ref[...])
        a = jnp.dot(x, w1_sc[...], preferred_element_type=F32)
        b = jnp.dot(x, w3_sc[...], preferred_element_type=F32)
        hdn = (a * jax.nn.sigmoid(a) * b).astype(BF16)
        y_ref[...] = jnp.dot(hdn, w2_sc[...], preferred_element_type=F32)

    @pl.when(i >= nb_ref[0])
    def _():
        y_ref[...] = jnp.zeros(y_ref.shape, F32)


def expert_ffn(xs, blk_exp, n_active, w1, w3, w2, layer):
    n_slots = xs.shape[0]
    nb = n_slots // MOE_ROWS
    wmap = lambda i, be, na: (layer, be[i], 0, 0)
    xmap = lambda i, be, na: (jnp.minimum(i, na[0] - 1), 0)
    return pl.pallas_call(
        _ffn_body,
        grid_spec=pltpu.PrefetchScalarGridSpec(
            num_scalar_prefetch=2,
            grid=(nb,),
            in_specs=[pl.BlockSpec((MOE_ROWS, D_MODEL // 2), xmap),
                      pl.BlockSpec((1, 1, D_MODEL, E_HID), wmap),
                      pl.BlockSpec((1, 1, D_MODEL, E_HID), wmap),
                      pl.BlockSpec((1, 1, E_HID, D_MODEL), wmap)],
            out_specs=pl.BlockSpec((MOE_ROWS, D_MODEL), lambda i, be, na: (i, 0)),
            scratch_shapes=[pltpu.VMEM((D_MODEL, E_HID), BF16), pltpu.VMEM((D_MODEL, E_HID), BF16),
                            pltpu.VMEM((E_HID, D_MODEL), BF16)],
        ),
        out_shape=jax.ShapeDtypeStruct((n_slots, D_MODEL), F32),
        compiler_params=_params(1),
        name="expert_ffn",
    )(blk_exp, n_active, xs, w1, w3, w2)


def moe_dispatch(route, counts_row, n):
    e = route[:, ROUTE_E1:ROUTE_E2 + 1].astype(jnp.int32)
    rank = route[:, ROUTE_R1:ROUTE_R2 + 1].astype(jnp.int32)
    counts = counts_row[0, EXPERT_LANE0:EXPERT_LANE0 + N_EXPERTS].astype(jnp.int32)
    padded = ((counts + MOE_ROWS - 1) // MOE_ROWS) * MOE_ROWS
    pad_end = jnp.cumsum(padded)
    pad_start = pad_end - padded
    dest = pad_start[e] + rank
    nb = -(-(n * TOP_K_IN_GROUP) // MOE_ROWS) + N_EXPERTS
    n_slots = nb * MOE_ROWS
    tok = jnp.broadcast_to(jnp.arange(n, dtype=jnp.int32)[:, None], (n, TOP_K_IN_GROUP))
    slot_tok = jnp.zeros((n_slots,), jnp.int32).at[dest.reshape(-1)].set(
        tok.reshape(-1), unique_indices=True, mode='promise_in_bounds')
    blk_start = jnp.arange(nb, dtype=jnp.int32) * MOE_ROWS
    blk_exp = jnp.sum((pad_end[None, :] <= blk_start[:, None]).astype(jnp.int32), axis=1)
    blk_exp = jnp.minimum(blk_exp, N_EXPERTS - 1)
    n_active = (pad_end[-1] // MOE_ROWS).astype(jnp.int32).reshape(1)
    return slot_tok, dest, blk_exp, n_active


def moe_experts(hn, route, counts_row, w1, w3, w2, layer):
    n = hn.shape[0]
    slot_tok, dest, blk_exp, n_active = moe_dispatch(route, counts_row, n)
    xs = hn.at[slot_tok].get(mode='promise_in_bounds')
    ys = expert_ffn(xs, blk_exp, n_active, w1, w3, w2, layer)
    return [ys.at[dest[:, j]].get(mode='promise_in_bounds') for j in range(TOP_K_IN_GROUP)]


def _combine(h_ref, y1_ref, y2_ref, route_ref):
    lane = lax.broadcasted_iota(jnp.int32, (ROW_TILE, LANES), 1)
    r = route_ref[...]
    g1 = jnp.sum(jnp.where(lane == ROUTE_G1, r, 0.0), axis=-1, keepdims=True)
    g2 = jnp.sum(jnp.where(lane == ROUTE_G2, r, 0.0), axis=-1, keepdims=True)
    return h_ref[...] + (y1_ref[...] * g1 + y2_ref[...] * g2)


def _combine_proj_body(h_ref, y1_ref, y2_ref, route_ref, g_ref, w_ref, h2_ref, u_ref):
    h2 = _combine(h_ref, y1_ref, y2_ref, route_ref)
    h2_ref[...] = h2
    u_ref[...] = jnp.dot(_rms(h2, g_ref[...]).astype(BF16), w_ref[...], preferred_element_type=F32)


def _combine_norm_body(h_ref, y1_ref, y2_ref, route_ref, g_ref, y_ref):
    y_ref[...] = _rms(_combine(h_ref, y1_ref, y2_ref, route_ref), g_ref[...])


def combine_proj(h, ys, route, g, w):
    n = h.shape[0]
    nn = w.shape[1]
    return pl.pallas_call(
        _combine_proj_body,
        grid=(n // ROW_TILE,),
        in_specs=[_rows(D_MODEL), _rows(D_MODEL), _rows(D_MODEL), _rows(LANES), _full((1, D_MODEL)),
                  _full((D_MODEL, nn))],
        out_specs=[_rows(D_MODEL), _rows(nn)],
        out_shape=[jax.ShapeDtypeStruct((n, D_MODEL), F32), jax.ShapeDtypeStruct((n, nn), F32)],
        compiler_params=_params(1),
        name="combine_proj",
    )(h, *ys, route, g.reshape(1, D_MODEL), w.astype(BF16))


def combine_norm(h, ys, route, g):
    n = h.shape[0]
    return pl.pallas_call(
        _combine_norm_body,
        grid=(n // ROW_TILE,),
        in_specs=[_rows(D_MODEL), _rows(D_MODEL), _rows(D_MODEL), _rows(LANES), _full((1, D_MODEL))],
        out_specs=_rows(D_MODEL),
        out_shape=jax.ShapeDtypeStruct((n, D_MODEL), F32),
        compiler_params=_params(1),
        name="combine_norm",
    )(h, *ys, route, g.reshape(1, D_MODEL))


def _even_first(x, axis):
    n = x.shape[axis]
    idx = jnp.concatenate([jnp.arange(0, n, 2), jnp.arange(1, n, 2)])
    return jnp.take(x, idx, axis=axis)


def _decode_query_cols(q_s, gates_s):
    B, T, G, R = DEC_BATCH, DEC_SEQ, NSA_KV, NSA_REP
    qg = q_s.reshape(B, T, G, R, HD).transpose(0, 2, 4, 3, 1).reshape(B, G, HD, R * T)
    qt = jnp.zeros((B, G, HD, G, R * T), BF16)
    for g in range(G):
        qt = qt.at[:, g, :, g, :].set(qg[:, g])
    qt = jnp.pad(qt.reshape(B, G * HD, G * R * T), ((0, 0), (0, 0), (0, LANES - G * R * T)))
    gr = gates_s[:, :3 * NSA_HEADS].reshape(B, T, NSA_HEADS, 3).transpose(0, 3, 2, 1).reshape(B, 3, NSA_HEADS * T)
    return qt, jnp.pad(gr, ((0, 0), (0, 0), (0, LANES - NSA_HEADS * T)))


def _feature_major(cache):
    lead, rows = cache.shape[:2]
    return jnp.transpose(cache, (0, 2, 3, 4, 1)).reshape(lead, NSA_KV_W, rows)


def _kv_rows_from_feature_major(xt, lead):
    x = xt.reshape(2, NSA_KV, HD, *lead)
    n = len(lead)
    return jnp.transpose(x, tuple(range(3, 3 + n)) + (0, 1, 2))


def mixer_a(h, p, past):
    pos = jnp.concatenate([jnp.tile(jnp.arange(SEQ, dtype=jnp.int32), BATCH),
                           PAST_LEN + jnp.tile(jnp.arange(DEC_SEQ, dtype=jnp.int32), DEC_BATCH)])
    (q, kvc, kvs, kvw, kvct, kvst, kvwt, ks, vst, kw, vwt, gates, gq, gk, gv, la, sgg) = inproj_a(
        h, p['norm_mix'][0], p['a_w_in'][0], p['a_gla_wa2'][0], p['a_gla_ba'][0], pos)
    pe, wc, gnorm = p['a_cmp_pe'][0], p['a_cmp_w'][0], p['a_gla_norm'][0]
    P = N_PROMPT
    ckv = cmp_blocks_rows(kvc, P, pe, wc)
    ckv = _even_first(ckv.reshape(BATCH, N_CMP_PROMPT, NSA_KV_W), 1)
    o_nsa_p = nsa_prompt_attn(q, gates, ckv, ks, vst, kw, vwt)
    o_gla_p, st_p = gla_seq(gq, gk, la, gv, sgg, gnorm)
    win_p = jnp.concatenate([kvwt[:, (b + 1) * SEQ - WINDOW:(b + 1) * SEQ] for b in range(BATCH)], axis=1)
    new_p = (_kv_rows_from_feature_major(kvct[:, :P], (BATCH, SEQ)),
             _kv_rows_from_feature_major(kvst[:, :P], (BATCH, SEQ)),
             _kv_rows_from_feature_major(win_p, (BATCH, WINDOW)), _gla_state_from_pairs(st_p))
    o_gla_s, st_s = gla_step(gq, gk, la, gv, sgg, gnorm, _gla_state_to_pairs(past['state_gla'][0]), P)
    n_pool = past['cache_cmp_kv'].shape[1]
    ckv_pool = cmp_blocks_pages(_feature_major(past['cache_cmp_kv'][0]), pe, wc)
    ckv_seq = ckv_pool.reshape(n_pool, PAGE_SIZE // CMP_BLOCK, NSA_KV_W)[past['page_table']]
    ckv_seq = _even_first(ckv_seq.reshape(DEC_BATCH, N_CMP_DEC, NSA_KV_W), 1)
    qt, gate_rows = _decode_query_cols(q[P:], gates[P:])
    per_seq = lambda x: x[P:].reshape(DEC_BATCH, 1, DEC_SEQ * NSA_KV_W)
    o_t, win_new = nsa_decode_attn(qt, gate_rows, ckv_seq, _feature_major(past['cache_slc_kv'][0]),
                                   _feature_major(past['cache_win_kv'][0]), per_seq(kvs), per_seq(kvw),
                                   past['page_table'])
    o_nsa_s = o_t[:, :, :NSA_HEADS * DEC_SEQ].reshape(DEC_BATCH, HD, NSA_HEADS, DEC_SEQ)
    o_nsa_s = o_nsa_s.transpose(0, 3, 2, 1).reshape(N_SAMPLE, NSA_Q_W).astype(BF16)
    n_buf = win_new.shape[2]
    win_s = jnp.transpose(win_new.reshape(DEC_BATCH, 2, NSA_KV, HD, n_buf), (0, 4, 1, 2, 3))
    new_s = (_kv_rows_from_feature_major(kvct[:, P:], (DEC_BATCH, DEC_SEQ)),
             _kv_rows_from_feature_major(kvst[:, P:], (DEC_BATCH, DEC_SEQ)), win_s, _gla_state_from_pairs(st_s))
    o_nsa = jnp.concatenate([o_nsa_p, o_nsa_s], axis=0)
    o_gla = jnp.concatenate([o_gla_p, o_gla_s], axis=0)
    return o_nsa, o_gla, new_p, new_s


def run_trunk(x_prompt, x_sample, p, past):
    h = jnp.concatenate([x_prompt.reshape(N_PROMPT, D_MODEL), x_sample.reshape(N_SAMPLE, D_MODEL)], axis=0)
    o_nsa, o_gla, new_p, new_s = mixer_a(h, p, past)
    h, hn, route, counts = post_mixer(h, [o_nsa, o_gla], p['a_w_out'][0], p['norm_ffn'][0], p['m_w_rg'][0],
                                      p['m_b_rg'][0], p['m_w_re'][0], p['m_b_re'][0])
    ys = moe_experts(hn, route, counts, p['m_w1'], p['m_w3'], p['m_w2'], 0)
    h, u = combine_proj(h, ys, route, p['norm_mix'][1], p['c_w_in'][0])
    lru_w = (p['c_conv_w'][0], p['c_conv_b'][0], p['c_w_a'][0], p['c_b_a'][0], p['c_w_x'][0], p['c_b_x'][0],
             p['c_lam'][0])
    us = u[N_PROMPT:].reshape(DEC_BATCH, DEC_SEQ, 2 * D_RNN)
    y_p, lru_p = lru_seq(u, *lru_w)
    y_s, lru_s = lru_step(jnp.swapaxes(us, 0, 1), past['state_conv'][0], past['state_lru'][0], *lru_w)
    conv_p = jnp.stack([u[(b + 1) * SEQ - (CONV_W - 1):(b + 1) * SEQ, D_RNN:] for b in range(BATCH)])
    conv_s = us[:, DEC_SEQ - (CONV_W - 1):, D_RNN:]
    mix_in = jnp.concatenate([y_p, jnp.swapaxes(y_s, 0, 1).reshape(N_SAMPLE, D_RNN)], axis=0)
    h, hn, route, counts = post_mixer(h, [mix_in], p['c_w_out'][0], p['norm_ffn'][1], p['m_w_rg'][1],
                                      p['m_b_rg'][1], p['m_w_re'][1], p['m_b_re'][1])
    ys = moe_experts(hn, route, counts, p['m_w1'], p['m_w3'], p['m_w2'], 1)
    y = combine_norm(h, ys, route, p['norm_final'])
    y_prompt = y[:N_PROMPT].reshape(BATCH, SEQ, D_MODEL)
    y_sample = y[N_PROMPT:].reshape(DEC_BATCH, DEC_SEQ, D_MODEL)
    return (y_prompt, y_sample), new_p + (lru_p, conv_p), new_s + (lru_s, conv_s)


def kernel(x_prompt, x_sample, cache_cmp_kv, cache_slc_kv, cache_win_kv, state_gla, state_lru, state_conv,
           page_table, norm_mix, norm_ffn, norm_final, a_w_in, a_cmp_pe, a_cmp_w, a_gla_wa2, a_gla_ba,
           a_gla_norm, a_w_out, c_w_in, c_conv_w, c_conv_b, c_w_a, c_b_a, c_w_x, c_b_x, c_lam, c_w_out,
           m_w_rg, m_b_rg, m_w_re, m_b_re, m_w1, m_w3, m_w2):
    p = {'norm_mix': norm_mix, 'norm_ffn': norm_ffn, 'norm_final': norm_final,
         'a_w_in': a_w_in, 'a_cmp_pe': a_cmp_pe, 'a_cmp_w': a_cmp_w, 'a_gla_wa2': a_gla_wa2,
         'a_gla_ba': a_gla_ba, 'a_gla_norm': a_gla_norm, 'a_w_out': a_w_out,
         'c_w_in': c_w_in, 'c_conv_w': c_conv_w, 'c_conv_b': c_conv_b, 'c_w_a': c_w_a, 'c_b_a': c_b_a,
         'c_w_x': c_w_x, 'c_b_x': c_b_x, 'c_lam': c_lam, 'c_w_out': c_w_out,
         'm_w_rg': m_w_rg, 'm_b_rg': m_b_rg, 'm_w_re': m_w_re, 'm_b_re': m_b_re,
         'm_w1': m_w1, 'm_w3': m_w3, 'm_w2': m_w2}
    past = {'cache_cmp_kv': cache_cmp_kv, 'cache_slc_kv': cache_slc_kv, 'cache_win_kv': cache_win_kv,
            'state_gla': state_gla, 'state_lru': state_lru, 'state_conv': state_conv,
            'page_table': page_table}
    (y_p, y_s), sp, ss = run_trunk(x_prompt, x_sample, p, past)
    outs = [y_p, y_s]
    for a, b in zip(sp, ss):
        outs += [a[None], b[None]]
    return tuple(outs)
```

```python
import functools
import jax, jax.numpy as jnp
from jax import lax
import numpy as np
from jax.experimental import pallas as pl
from jax.experimental.pallas import tpu as pltpu

D_MODEL = 1024
BATCH = 2
SEQ = 8192
DEC_BATCH = 128
DEC_SEQ = 4
PAST_LEN = 2048
PAGE_SIZE = 128
EPS = 1e-6
NSA_HEADS = 8
NSA_KV = 2
NSA_REP = NSA_HEADS // NSA_KV
HD = 64
CMP_BLOCK = 32
SLC_BLOCK = 64
SLC_TOPK = 16
WINDOW = 512
Q_BLOCK = 128
ROPE_DIM = HD // 4
ROPE_THETA = 500000.0
GLA_HEADS = 4
GLA_DK = 64
GLA_DV = 128
GLA_LOWRANK = 16
GLA_TAU = 16.0
D_RNN = 1280
LRU_BLOCKS = 10
LRU_BW = D_RNN // LRU_BLOCKS
CONV_W = 4
LRU_C = 8.0
N_GROUPS = 4
EXP_PER_GROUP = 8
N_EXPERTS = N_GROUPS * EXP_PER_GROUP
E_HID = 512
TOP_K_IN_GROUP = 2
NSA_Q_W = NSA_HEADS * HD
NSA_KV_W = 2 * NSA_KV * HD
GLA_K_W = GLA_HEADS * GLA_DK
GLA_V_W = GLA_HEADS * GLA_DV
A_SIZES = (NSA_Q_W, NSA_KV_W, NSA_KV_W, NSA_KV_W, 3 * NSA_HEADS, GLA_K_W, GLA_K_W, GLA_V_W, GLA_LOWRANK, GLA_V_W)
N_PROMPT = BATCH * SEQ
N_SAMPLE = DEC_BATCH * DEC_SEQ
N_TOK = N_PROMPT + N_SAMPLE
N_PAGES = PAST_LEN // PAGE_SIZE

F32 = jnp.float32
BF16 = jnp.bfloat16
VMEM_LIMIT_BYTES = 56 * 1024 * 1024
LANES = 128
SUBLANES = 8
ROW_TILE = 512


def _params(n_axes):
    return pltpu.CompilerParams(dimension_semantics=("arbitrary",) * n_axes, vmem_limit_bytes=VMEM_LIMIT_BYTES)


def _full(shape):
    return pl.BlockSpec(shape, lambda *_: (0,) * len(shape))


def _rows(width):
    return pl.BlockSpec((ROW_TILE, width), lambda i: (i, 0))


def _rms(x, g):
    return x * lax.rsqrt(jnp.mean(x * x, axis=-1, keepdims=True) + EPS) * g


def _softplus(x):
    return jnp.maximum(x, 0.0) + jnp.log1p(jnp.exp(-jnp.abs(x)))


def _gelu_tanh(x):
    return x * (0.5 * (1.0 + jnp.tanh(0.7978845608028654 * (x + 0.044715 * (x * x * x)))))


def _nt_dot(a, b):
    return lax.dot_general(a, b, (((1,), (1,)), ((), ())), preferred_element_type=F32)


def _tn_dot(a, b):
    return lax.dot_general(a, b, (((0,), (0,)), ((), ())), preferred_element_type=F32)


A_Q0, A_KVC0, A_KVS0, A_KVW0 = 0, 512, 768, 1024
A_GQ0, A_GK0, A_GV0, A_GG0, A_MISC0 = 1280, 1536, 1792, 2304, 2816
A_COLS = A_MISC0 + LANES
MISC_LR0 = 3 * NSA_HEADS
KV_HALF = NSA_KV * HD


def _rope_lanes(x, cos_t, sin_lo, sin_hi):
    reps = x.shape[1] // LANES
    tile = (lambda t: jnp.concatenate([t] * reps, axis=1)) if reps > 1 else (lambda t: t)
    w = x.shape[1]
    half = ROPE_DIM // 2
    return x * tile(cos_t) + pltpu.roll(x, half, 1) * tile(sin_hi) + pltpu.roll(x, w - half, 1) * tile(sin_lo)


def _inproj_a_body(h_ref, g_ref, w_ref, wa2_ref, ba_ref, cos_ref, slo_ref, shi_ref,
                   q_ref, kvc_ref, kvs_ref, kvw_ref, kvct_ref, kvst_ref, kvwt_ref, ks_ref, vst_ref, kw_ref,
                   vwt_ref, gates_ref, gq_ref, gk_ref, gv_ref, la_ref, sgg_ref):
    y = _rms(h_ref[...], g_ref[...]).astype(BF16)
    proj = lambda a, b: jnp.dot(y, w_ref[:, a:b], preferred_element_type=F32)
    cos_t, sin_lo, sin_hi = cos_ref[...], slo_ref[...], shi_ref[...]
    q_ref[...] = (_rope_lanes(proj(A_Q0, A_KVC0), cos_t, sin_lo, sin_hi) * (HD ** -0.5)).astype(BF16)
    kvc = proj(A_KVC0, A_KVS0)
    kvc_ref[...] = kvc
    kvct_ref[...] = kvc.T
    for a0, kv_ref, kvt_ref, k_ref, vt_ref in ((A_KVS0, kvs_ref, kvst_ref, ks_ref, vst_ref),
                                               (A_KVW0, kvw_ref, kvwt_ref, kw_ref, vwt_ref)):
        kv = proj(a0, a0 + 2 * KV_HALF)
        k = _rope_lanes(kv[:, :KV_HALF], cos_t, sin_lo, sin_hi)
        vt = kv[:, KV_HALF:].T
        kv_ref[:, :KV_HALF] = k
        kv_ref[:, KV_HALF:] = kv[:, KV_HALF:]
        kvt_ref[:KV_HALF, :] = k.T
        kvt_ref[KV_HALF:, :] = vt
        k_ref[...] = k.astype(BF16)
        vt_ref[...] = vt.astype(BF16)
    misc = proj(A_MISC0, A_COLS)
    gates_ref[...] = jax.nn.sigmoid(misc)
    z = jnp.dot(misc.astype(BF16), wa2_ref[...], preferred_element_type=F32) + ba_ref[...]
    la_ref[...] = -_softplus(-z) * (1.0 / GLA_TAU)
    gq_ref[...] = proj(A_GQ0, A_GK0) * (GLA_DK ** -0.5)
    gk_ref[...] = proj(A_GK0, A_GV0)
    gv_ref[...] = proj(A_GV0, A_GG0).astype(BF16)
    gg = proj(A_GG0, A_MISC0)
    sgg_ref[...] = gg * jax.nn.sigmoid(gg)


def _rope_tables(pos):
    half = ROPE_DIM // 2
    inv = 1.0 / (ROPE_THETA ** (jnp.arange(0, ROPE_DIM, 2, dtype=F32) / ROPE_DIM))
    ang = pos.astype(F32)[:, None] * inv[None, :]
    cos, sin = jnp.cos(ang), jnp.sin(ang)
    n = pos.shape[0]
    one = jnp.ones((n, HD - ROPE_DIM), F32)
    zero = jnp.zeros((n, HD - ROPE_DIM), F32)
    zh = jnp.zeros((n, half), F32)
    seg = lambda a, b, rest: jnp.concatenate([a, b, rest] * (LANES // HD), axis=1)
    return seg(cos, cos, one), seg(-sin, zh, zero), seg(zh, sin, zero)


def inproj_a(h, g, w_in, wa2, ba, pos):
    n = h.shape[0]
    zpad = jnp.zeros((D_MODEL, LANES - 3 * NSA_HEADS - GLA_LOWRANK), F32)
    o = np.cumsum((0,) + A_SIZES)
    w = jnp.concatenate([w_in[:, o[0]:o[4]], w_in[:, o[5]:o[8]], w_in[:, o[9]:o[10]],
                         w_in[:, o[4]:o[5]], w_in[:, o[8]:o[9]], zpad], axis=1).astype(BF16)
    wa2p = jnp.zeros((LANES, GLA_K_W), F32).at[MISC_LR0:MISC_LR0 + GLA_LOWRANK].set(wa2).astype(BF16)
    cols = pl.BlockSpec((KV_HALF, ROW_TILE), lambda i: (0, i))
    kvt = (pl.BlockSpec((NSA_KV_W, ROW_TILE), lambda i: (0, i)), (NSA_KV_W, n), F32)
    outs = [(_rows(NSA_Q_W), (n, NSA_Q_W), BF16), (_rows(NSA_KV_W), (n, NSA_KV_W), F32),
            (_rows(NSA_KV_W), (n, NSA_KV_W), F32), (_rows(NSA_KV_W), (n, NSA_KV_W), F32), kvt, kvt, kvt,
            (_rows(KV_HALF), (n, KV_HALF), BF16), (cols, (KV_HALF, n), BF16),
            (_rows(KV_HALF), (n, KV_HALF), BF16), (cols, (KV_HALF, n), BF16),
            (_rows(LANES), (n, LANES), F32), (_rows(GLA_K_W), (n, GLA_K_W), F32),
            (_rows(GLA_K_W), (n, GLA_K_W), F32), (_rows(GLA_V_W), (n, GLA_V_W), BF16),
            (_rows(GLA_K_W), (n, GLA_K_W), F32), (_rows(GLA_V_W), (n, GLA_V_W), F32)]
    return pl.pallas_call(
        _inproj_a_body,
        grid=(n // ROW_TILE,),
        in_specs=[_rows(D_MODEL), _full((1, D_MODEL)), _full((D_MODEL, A_COLS)), _full((LANES, GLA_K_W)),
                  _full((1, GLA_K_W)), _rows(LANES), _rows(LANES), _rows(LANES)],
        out_specs=[s for s, _, _ in outs],
        out_shape=[jax.ShapeDtypeStruct(shape, dt) for _, shape, dt in outs],
        compiler_params=_params(1),
        name="inproj_a",
    )(h, g.reshape(1, D_MODEL), w, wa2p, ba.reshape(1, GLA_K_W), *_rope_tables(pos))


CMP_TILE_BLOCKS = 256
CMP_TILE_ROWS = CMP_TILE_BLOCKS * CMP_BLOCK
CMP_TILE_PAGES = CMP_TILE_ROWS // PAGE_SIZE


def _cmp_reduce(xk_ref, xv_ref, pe_ref, w_ref):
    acc = jnp.zeros((CMP_TILE_BLOCKS, NSA_KV_W), F32)
    for l in range(CMP_BLOCK):
        rows = pl.ds(l, CMP_TILE_BLOCKS, stride=CMP_BLOCK)
        xl = jnp.concatenate([xk_ref[rows, :], xv_ref[rows, :]], axis=1) + pe_ref[l:l + 1, :]
        acc = acc + jnp.dot(xl.astype(BF16), w_ref[l], preferred_element_type=F32)
    return acc


def _cmp_rows_body(xk_ref, xv_ref, pe_ref, w_ref, o_ref):
    o_ref[...] = _cmp_reduce(xk_ref, xv_ref, pe_ref, w_ref)


def _cmp_pages_body(x_ref, pe_ref, w_ref, o_ref, xk_sc, xv_sc):
    for pg in range(CMP_TILE_PAGES):
        rows = slice(pg * PAGE_SIZE, (pg + 1) * PAGE_SIZE)
        xk_sc[rows, :] = x_ref[pg, :KV_HALF, :].T
        xv_sc[rows, :] = x_ref[pg, KV_HALF:, :].T
    o_ref[...] = _cmp_reduce(xk_sc, xv_sc, pe_ref, w_ref)


def _cmp_weights(pe, w_cmp):
    pe_rows = jnp.broadcast_to(pe[:, :, None, :], (CMP_BLOCK, 2, NSA_KV, HD)).reshape(CMP_BLOCK, NSA_KV_W)
    w_bd = jnp.einsum('lcde,cx,gy->lcgdxye', w_cmp, jnp.eye(2, dtype=F32), jnp.eye(NSA_KV, dtype=F32))
    return pe_rows, w_bd.reshape(CMP_BLOCK, NSA_KV_W, NSA_KV_W).astype(BF16)


def cmp_blocks_rows(x, n_rows, pe, w_cmp):
    return pl.pallas_call(
        _cmp_rows_body,
        grid=(n_rows // CMP_TILE_ROWS,),
        in_specs=[pl.BlockSpec((CMP_TILE_ROWS, KV_HALF), lambda i: (i, 0)),
                  pl.BlockSpec((CMP_TILE_ROWS, KV_HALF), lambda i: (i, 1)), _full((CMP_BLOCK, NSA_KV_W)),
                  _full((CMP_BLOCK, NSA_KV_W, NSA_KV_W))],
        out_specs=pl.BlockSpec((CMP_TILE_BLOCKS, NSA_KV_W), lambda i: (i, 0)),
        out_shape=jax.ShapeDtypeStruct((n_rows // CMP_BLOCK, NSA_KV_W), F32),
        compiler_params=_params(1),
        name="cmp_blocks_rows",
    )(x, x, *_cmp_weights(pe, w_cmp))


def cmp_blocks_pages(xt, pe, w_cmp):
    n_pages = xt.shape[0]
    return pl.pallas_call(
        _cmp_pages_body,
        grid=(n_pages // CMP_TILE_PAGES,),
        in_specs=[pl.BlockSpec((CMP_TILE_PAGES, NSA_KV_W, PAGE_SIZE), lambda i: (i, 0, 0)),
                  _full((CMP_BLOCK, NSA_KV_W)), _full((CMP_BLOCK, NSA_KV_W, NSA_KV_W))],
        out_specs=pl.BlockSpec((CMP_TILE_BLOCKS, NSA_KV_W), lambda i: (i, 0)),
        out_shape=jax.ShapeDtypeStruct((n_pages * PAGE_SIZE // CMP_BLOCK, NSA_KV_W), F32),
        scratch_shapes=[pltpu.VMEM((CMP_TILE_ROWS, KV_HALF), F32), pltpu.VMEM((CMP_TILE_ROWS, KV_HALF), F32)],
        compiler_params=_params(1),
        name="cmp_blocks_pages",
    )(xt, *_cmp_weights(pe, w_cmp))


KEY_TILE = 256
NEG_BIG = -1e30
N_CMP_PROMPT = SEQ // CMP_BLOCK
N_SLC_PROMPT = SEQ // SLC_BLOCK
SLC_SHIFT = SLC_BLOCK.bit_length() - 1
N_CMP_DEC = PAST_LEN // CMP_BLOCK
N_SLC_DEC = -(-(PAST_LEN + DEC_SEQ) // SLC_BLOCK)
N_SLC_DEC_PAD = -(-N_SLC_DEC // LANES) * LANES
DEC_COLS_PER_GROUP = NSA_REP * DEC_SEQ


def _tile_cols(x, reps):
    return jnp.concatenate([x] * reps, axis=1) if reps > 1 else x


def _even_first_cmp_end(n_cmp):
    j = lax.broadcasted_iota(jnp.int32, (n_cmp, 1), 0)
    blk = jnp.where(j < n_cmp // 2, 2 * j, 2 * (j - n_cmp // 2) + 1)
    return (blk + 1) * CMP_BLOCK - 1


def _softmax_cols(s, mask):
    s = jnp.where(mask, s, -jnp.inf)
    m = jnp.max(s, axis=0, keepdims=True)
    m = jnp.where(m > -jnp.inf, m, 0.0)
    e = jnp.where(mask, jnp.exp(s - m), 0.0)
    return e / jnp.maximum(jnp.sum(e, axis=0, keepdims=True), 1e-30)


def _select_cols(p_slc, qpos, n_top):
    ns = p_slc.shape[0]
    blk = lax.broadcasted_iota(jnp.int32, p_slc.shape, 0)
    cur = qpos >> SLC_SHIFT
    forced = (blk == 0) | (blk == cur) | (blk == cur - 1)
    score = jnp.where(forced, jnp.inf, p_slc)
    score = jnp.where(blk <= cur, score, -jnp.inf)
    sel = jnp.zeros(p_slc.shape, F32)
    for _ in range(n_top):
        m = jnp.max(score, axis=0, keepdims=True)
        idx = jnp.min(jnp.where(score == m, blk, ns), axis=0, keepdims=True)
        hit = blk == idx
        sel = jnp.where(hit & (m > -jnp.inf), 1.0, sel)
        score = jnp.where(hit, -jnp.inf, score)
    return sel


def _flash_init(m_sc, l_sc, acc_sc):
    m_sc[...] = jnp.full(m_sc.shape, NEG_BIG, F32)
    l_sc[...] = jnp.zeros(l_sc.shape, F32)
    acc_sc[...] = jnp.zeros(acc_sc.shape, F32)


def _flash_cols(scores, mask, pv, m_sc, l_sc, acc_sc):
    s = jnp.where(mask, scores, NEG_BIG)
    m_old = m_sc[...]
    m_new = jnp.maximum(m_old, jnp.max(s, axis=0, keepdims=True))
    alpha = jnp.exp(m_old - m_new)
    p = jnp.where(mask, jnp.exp(s - m_new), 0.0)
    l_sc[...] = alpha * l_sc[...] + jnp.sum(p, axis=0, keepdims=True)
    acc_sc[...] = alpha * acc_sc[...] + pv(p.astype(BF16))
    m_sc[...] = m_new


def _flash_stream(score_fn, pv_fn, first, lo, n, stream_sc, m_sc, l_sc, acc_sc):
    (sa, ca), (sb, cb) = stream_sc

    def issue(s_ref, c_ref, kt, self_tile=False):
        s = score_fn(kt, self_tile)
        s_ref[...] = s
        c_ref[...] = jnp.max(s, axis=0, keepdims=True)

    def consume(s_ref, c_ref, kt):
        m_old = m_sc[...]
        m_new = jnp.maximum(m_old, c_ref[...])
        alpha = jnp.exp(m_old - m_new)
        p = jnp.exp(s_ref[...] - m_new)
        l_sc[...] = alpha * l_sc[...] + jnp.sum(p, axis=0, keepdims=True)
        acc_sc[...] = alpha * acc_sc[...] + pv_fn(kt, p.astype(BF16))
        m_sc[...] = m_new

    _flash_init(m_sc, l_sc, acc_sc)
    issue(sa, ca, first, True)

    def two_tiles(jj, kt_a):
        t0 = lo + 2 * jj
        issue(sb, cb, t0)
        consume(sa, ca, kt_a)
        issue(sa, ca, t0 + 1)
        consume(sb, cb, t0)
        return t0 + 1

    kt_a = lax.fori_loop(0, n // 2, two_tiles, first)

    @pl.when(n % 2 == 1)
    def _():
        issue(sb, cb, lo + n - 1)
        consume(sa, ca, kt_a)
        consume(sb, cb, lo + n - 1)

    @pl.when(n % 2 == 0)
    def _():
        consume(sa, ca, kt_a)

    return acc_sc[...] / jnp.maximum(l_sc[...], 1e-30)


def _flash_out(l_sc, acc_sc):
    return acc_sc[...] / jnp.maximum(l_sc[...], 1e-30)


def _pv_split(vt, p):
    c2 = p.shape[1] // 2
    return jnp.concatenate([jnp.dot(vt[:HD], p[:, :c2], preferred_element_type=F32),
                            jnp.dot(vt[HD:], p[:, c2:], preferred_element_type=F32)], axis=1)


def _nsa_prompt_body(q_ref, gates_ref, ckv_ref, cos_ref, slo_ref, shi_ref, ks_ref, vst_ref, kw_ref, vwt_ref,
                     blk_ref, o_ref, m_sc, l_sc, acc_sc, sa_sc, ca_sc, sb_sc, cb_sc):
    stream_sc = ((sa_sc, ca_sc), (sb_sc, cb_sc))
    i = pl.program_id(1)
    nq = Q_BLOCK
    cols = NSA_HEADS * nq
    qpos = i * nq + lax.broadcasted_iota(jnp.int32, (1, nq), 1)
    q = q_ref[...].astype(F32)
    pairs = [q[:, j * LANES:(j + 1) * LANES].T for j in range(NSA_HEADS // 2)]
    zero = jnp.zeros((HD, cols // 2), F32)
    qt_g = [jnp.concatenate([pairs[2 * g][:HD], pairs[2 * g][HD:], pairs[2 * g + 1][:HD], pairs[2 * g + 1][HD:]],
                            axis=1) for g in range(NSA_KV)]
    qt = jnp.concatenate([jnp.concatenate([qt_g[0], zero], axis=1),
                          jnp.concatenate([zero, qt_g[1]], axis=1)], axis=0).astype(BF16)
    ckv = ckv_ref[0]
    ck = _rope_lanes(ckv[:, :KV_HALF], cos_ref[...], slo_ref[...], shi_ref[...]).astype(BF16)
    cvt = ckv[:, KV_HALF:].T.astype(BF16)
    c_mask = _even_first_cmp_end(N_CMP_PROMPT) <= qpos
    p = _softmax_cols(jnp.dot(ck, qt, preferred_element_type=F32), _tile_cols(c_mask, NSA_HEADS))
    o_c = _pv_split(cvt, p.astype(BF16))
    sel = []
    for g in range(NSA_KV):
        c0 = g * NSA_REP * nq
        p_grp = p[:, c0:c0 + nq]
        for r in range(1, NSA_REP):
            p_grp = p_grp + p[:, c0 + r * nq:c0 + (r + 1) * nq]
        p_slc = p_grp[:N_CMP_PROMPT // 2] + p_grp[N_CMP_PROMPT // 2:]
        sel.append(_select_cols(p_slc, qpos, SLC_TOPK))
    sel = jnp.concatenate(sel, axis=1)
    sel_bias = jnp.where(sel > 0.5, 0.0, NEG_BIG).astype(BF16)
    qt_aug = jnp.concatenate([qt, jnp.concatenate([_tile_cols(sel_bias[:, :nq], NSA_REP),
                                                   _tile_cols(sel_bias[:, nq:], NSA_REP)], axis=1)], axis=0)
    key_row = lax.broadcasted_iota(jnp.int32, (KEY_TILE, 1), 0)
    kt_self = i // (KEY_TILE // nq)

    def slc_scores(kt, self_tile):
        k0 = pl.multiple_of(kt * KEY_TILE, KEY_TILE)
        k_aug = jnp.concatenate([ks_ref[pl.ds(k0, KEY_TILE), :], blk_ref[pl.ds(k0, KEY_TILE), :]], axis=1)
        s = jnp.dot(k_aug, qt_aug, preferred_element_type=F32)
        if self_tile:
            s = s + _tile_cols(jnp.where(k0 + key_row <= qpos, 0.0, NEG_BIG), NSA_HEADS)
        return s

    def slc_pv(kt, pb):
        return _pv_split(vst_ref[:, pl.ds(pl.multiple_of(kt * KEY_TILE, KEY_TILE), KEY_TILE)], pb)

    o_s = _flash_stream(slc_scores, slc_pv, kt_self, 0, kt_self, stream_sc, m_sc, l_sc, acc_sc)

    def win_scores(kt, self_tile):
        k0 = pl.multiple_of(kt * KEY_TILE, KEY_TILE)
        d = qpos - (k0 + key_row)
        bias = jnp.where((d >= 0) & (d < WINDOW), 0.0, NEG_BIG)
        return jnp.dot(kw_ref[pl.ds(k0, KEY_TILE), :], qt, preferred_element_type=F32) + _tile_cols(bias, NSA_HEADS)

    def win_pv(kt, pb):
        return _pv_split(vwt_ref[:, pl.ds(pl.multiple_of(kt * KEY_TILE, KEY_TILE), KEY_TILE)], pb)

    win_lo = jnp.maximum(i - WINDOW // nq, 0) // (KEY_TILE // nq)
    o_w = _flash_stream(win_scores, win_pv, kt_self, win_lo, kt_self - win_lo, stream_sc, m_sc, l_sc, acc_sc)
    gates_t = gates_ref[...].T
    merged = []
    for hd in range(NSA_HEADS):
        cs = slice(hd * nq, (hd + 1) * nq)
        gate = lambda br: gates_t[3 * hd + br:3 * hd + br + 1]
        merged.append(gate(0) * o_c[:, cs] + gate(1) * o_s[:, cs] + gate(2) * o_w[:, cs])
    for j in range(NSA_HEADS // 2):
        pair = jnp.concatenate([merged[2 * j], merged[2 * j + 1]], axis=0).T
        o_ref[:, j * LANES:(j + 1) * LANES] = pair.astype(BF16)


def nsa_prompt_attn(q, gates, ckv, ks, vst, kw, vwt):
    nqb = SEQ // Q_BLOCK
    tok = lambda wd: pl.BlockSpec((Q_BLOCK, wd), lambda b, i: (b * nqb + i, 0))
    seq_rows = pl.BlockSpec((SEQ, KV_HALF), lambda b, i: (b, 0))
    seq_cols = pl.BlockSpec((KV_HALF, SEQ), lambda b, i: (0, b))
    n_cmp = N_CMP_PROMPT
    c_blk = jnp.concatenate([jnp.arange(0, n_cmp, 2), jnp.arange(1, n_cmp, 2)]).astype(jnp.int32)
    blk_onehot = ((jnp.arange(SEQ)[:, None] >> SLC_SHIFT) == jnp.arange(N_SLC_PROMPT)[None, :]).astype(BF16)
    cols = NSA_HEADS * Q_BLOCK
    return pl.pallas_call(
        _nsa_prompt_body,
        grid=(BATCH, nqb),
        in_specs=[tok(NSA_Q_W), tok(LANES), pl.BlockSpec((1, n_cmp, NSA_KV_W), lambda b, i: (b, 0, 0)),
                  _full((n_cmp, LANES)), _full((n_cmp, LANES)), _full((n_cmp, LANES)),
                  seq_rows, seq_cols, seq_rows, seq_cols, _full((SEQ, N_SLC_PROMPT))],
        out_specs=tok(NSA_Q_W),
        out_shape=jax.ShapeDtypeStruct((N_PROMPT, NSA_Q_W), BF16),
        scratch_shapes=[pltpu.VMEM((1, cols), F32), pltpu.VMEM((1, cols), F32), pltpu.VMEM((HD, cols), F32),
                        pltpu.VMEM((KEY_TILE, cols), F32), pltpu.VMEM((1, cols), F32),
                        pltpu.VMEM((KEY_TILE, cols), F32), pltpu.VMEM((1, cols), F32)],
        compiler_params=_params(2),
        name="nsa_prompt",
    )(q, gates, ckv, *_rope_tables((c_blk + 1) * CMP_BLOCK - 1), ks, vst, kw, vwt, blk_onehot)


def _nsa_decode_body(pt_ref, qt_ref, gate_ref, ckv_ref, cos_ref, slo_ref, shi_ref, hsum_ref, *refs):
    page_refs = refs[:N_PAGES]
    win_ref, kvs_new_ref, kvw_new_ref, o_ref, win_out_ref, m_sc, l_sc, acc_sc = refs[N_PAGES:]
    qt = qt_ref[0]
    lane = lax.broadcasted_iota(jnp.int32, (1, LANES), 1)
    qpos = PAST_LEN + (lane & (DEC_SEQ - 1))
    group0 = lane < DEC_COLS_PER_GROUP
    ckv = ckv_ref[0]
    ck = _rope_lanes(ckv[:, :KV_HALF], cos_ref[...], slo_ref[...], shi_ref[...]).astype(BF16)
    c_mask = _even_first_cmp_end(N_CMP_DEC) <= qpos
    p = _softmax_cols(jnp.dot(ck, qt, preferred_element_type=F32), c_mask)
    o_c = _tn_dot(ckv[:, KV_HALF:].astype(BF16), p.astype(BF16))
    p_grp = jnp.dot(p, hsum_ref[...], preferred_element_type=F32, precision=lax.Precision.HIGHEST)
    p_slc = jnp.concatenate([p_grp[:N_CMP_DEC // 2] + p_grp[N_CMP_DEC // 2:],
                             jnp.zeros((N_SLC_DEC_PAD - N_CMP_DEC // 2, LANES), F32)], axis=0)
    sel = _select_cols(p_slc, qpos, SLC_TOPK)

    def new_rows(ref):
        row = ref[0]
        kv = jnp.concatenate([row[:, t * NSA_KV_W:(t + 1) * NSA_KV_W] for t in range(DEC_SEQ)], axis=0)
        return jnp.concatenate([kv, jnp.zeros((SUBLANES - DEC_SEQ, NSA_KV_W), F32)], axis=0)

    new_row = lax.broadcasted_iota(jnp.int32, (SUBLANES, 1), 0)
    new_pos = PAST_LEN + new_row
    new_valid = new_row < DEC_SEQ
    _flash_init(m_sc, l_sc, acc_sc)
    kt_old = jnp.concatenate([r[0, :KV_HALF, :] for r in page_refs], axis=1).astype(BF16)
    vt_old = jnp.concatenate([r[0, KV_HALF:, :] for r in page_refs], axis=1).astype(BF16)
    key_blk = lax.broadcasted_iota(jnp.int32, (PAST_LEN, N_SLC_DEC_PAD), 0) >> SLC_SHIFT
    blk_col = lax.broadcasted_iota(jnp.int32, (PAST_LEN, N_SLC_DEC_PAD), 1)
    chosen = jnp.dot(jnp.where(key_blk == blk_col, 1.0, 0.0).astype(BF16), sel.astype(BF16),
                     preferred_element_type=F32) > 0.5
    old_pos = lax.broadcasted_iota(jnp.int32, (PAST_LEN, 1), 0)
    _flash_cols(_tn_dot(kt_old, qt), chosen & (old_pos <= qpos),
                lambda pb: jnp.dot(vt_old, pb, preferred_element_type=F32), m_sc, l_sc, acc_sc)
    kv_new = new_rows(kvs_new_ref)
    sel_new = sel[(PAST_LEN >> SLC_SHIFT):(PAST_LEN >> SLC_SHIFT) + 1] > 0.5
    v_new = kv_new[:, KV_HALF:].astype(BF16)
    _flash_cols(jnp.dot(kv_new[:, :KV_HALF].astype(BF16), qt, preferred_element_type=F32),
                sel_new & new_valid & (new_pos <= qpos), lambda pb: _tn_dot(v_new, pb), m_sc, l_sc, acc_sc)
    o_s = _flash_out(l_sc, acc_sc)
    _flash_init(m_sc, l_sc, acc_sc)
    n_buf = win_ref.shape[2]
    win = win_ref[0]
    d = qpos - (PAST_LEN - n_buf + lax.broadcasted_iota(jnp.int32, (n_buf, 1), 0))
    vt_win = win[KV_HALF:].astype(BF16)
    _flash_cols(_tn_dot(win[:KV_HALF].astype(BF16), qt), (d >= 0) & (d < WINDOW),
                lambda pb: jnp.dot(vt_win, pb, preferred_element_type=F32), m_sc, l_sc, acc_sc)
    kw_new = new_rows(kvw_new_ref)
    d = qpos - new_pos
    vw_new = kw_new[:, KV_HALF:].astype(BF16)
    _flash_cols(jnp.dot(kw_new[:, :KV_HALF].astype(BF16), qt, preferred_element_type=F32),
                new_valid & (d >= 0) & (d < WINDOW), lambda pb: _tn_dot(vw_new, pb), m_sc, l_sc, acc_sc)
    o_w = _flash_out(l_sc, acc_sc)
    g = gate_ref[0]
    o = g[0:1] * o_c + g[1:2] * o_s + g[2:3] * o_w
    o_ref[0] = jnp.where(group0, o[:HD], o[HD:])
    key = lax.broadcasted_iota(jnp.int32, (SUBLANES, n_buf), 1)
    place = jnp.where((key == n_buf - DEC_SEQ + new_row) & new_valid, 1.0, 0.0)
    placed = lax.dot_general(kw_new, place, (((0,), (0,)), ((), ())), preferred_element_type=F32,
                             precision=lax.Precision.HIGHEST)
    keep = lax.broadcasted_iota(jnp.int32, (1, n_buf), 1) < n_buf - DEC_SEQ
    win_out_ref[0] = jnp.where(keep, pltpu.roll(win, n_buf - DEC_SEQ, 1), placed)


def nsa_decode_attn(qt, gate_rows, ckv, slc_pool, win_buf, kvs_new, kvw_new, page_table):
    n_buf = win_buf.shape[2]
    per_b = lambda *shape: pl.BlockSpec((1,) + shape, lambda b, pt: (b,) + (0,) * len(shape))
    const = lambda *shape: pl.BlockSpec(shape, lambda b, pt: (0,) * len(shape))
    page = lambda j: pl.BlockSpec((1, NSA_KV_W, PAGE_SIZE), lambda b, pt: (pt[b, j], 0, 0))
    c_blk = jnp.concatenate([jnp.arange(0, N_CMP_DEC, 2), jnp.arange(1, N_CMP_DEC, 2)]).astype(jnp.int32)
    col = jnp.arange(LANES)
    used = col < NSA_KV * DEC_COLS_PER_GROUP
    same = (col[:, None] // DEC_COLS_PER_GROUP == col[None, :] // DEC_COLS_PER_GROUP) & \
           (col[:, None] % DEC_SEQ == col[None, :] % DEC_SEQ) & used[:, None] & used[None, :]
    return pl.pallas_call(
        _nsa_decode_body,
        grid_spec=pltpu.PrefetchScalarGridSpec(
            num_scalar_prefetch=1,
            grid=(DEC_BATCH,),
            in_specs=[per_b(LANES, LANES), per_b(3, LANES), per_b(N_CMP_DEC, NSA_KV_W),
                      const(N_CMP_DEC, LANES), const(N_CMP_DEC, LANES), const(N_CMP_DEC, LANES),
                      const(LANES, LANES)] + [page(j) for j in range(N_PAGES)]
                     + [per_b(NSA_KV_W, n_buf), per_b(1, DEC_SEQ * NSA_KV_W), per_b(1, DEC_SEQ * NSA_KV_W)],
            out_specs=[per_b(HD, LANES), per_b(NSA_KV_W, n_buf)],
            scratch_shapes=[pltpu.VMEM((1, LANES), F32), pltpu.VMEM((1, LANES), F32),
                            pltpu.VMEM((LANES, LANES), F32)],
        ),
        out_shape=[jax.ShapeDtypeStruct((DEC_BATCH, HD, LANES), F32),
                   jax.ShapeDtypeStruct((DEC_BATCH, NSA_KV_W, n_buf), F32)],
        compiler_params=_params(1),
        name="nsa_decode",
    )(page_table, qt, gate_rows, ckv, *_rope_tables((c_blk + 1) * CMP_BLOCK - 1), same.astype(F32),
      *([slc_pool] * N_PAGES), win_buf, kvs_new, kvw_new)


GLA_PAIRS = GLA_HEADS // 2
GLA_ROWS = 128
GLA_SUB = 16
GLA_STEP_SEQS = 8


def _gla_rows(q, k, la, v, sgg, gnorm, st_ref, sub):
    R = q.shape[0]
    row = lax.broadcasted_iota(jnp.int32, (R, GLA_K_W), 0)
    rin = row % sub
    cum = la
    d = 1
    while d < sub:
        cum = cum + jnp.where(rin >= d, pltpu.roll(cum, d, 0), 0.0)
        d *= 2
    lane = lax.broadcasted_iota(jnp.int32, (sub, LANES), 1)
    lo = lane < GLA_DK
    rsub = lax.broadcasted_iota(jnp.int32, (sub, LANES), 0)
    out_rows = []
    for c in range(R // sub):
        rs = slice(c * sub, (c + 1) * sub)
        cum_c = cum[rs]
        last = cum_c[sub - 1:sub]
        qe = q[rs] * jnp.exp(cum_c)
        kdec = k[rs] * jnp.exp(last - cum_c)
        v_c = v[rs]
        heads = []
        for pr in range(GLA_PAIRS):
            ls = slice(pr * LANES, (pr + 1) * LANES)
            st = st_ref[pr]
            st_b = st.astype(BF16)
            qe_p, kd_p, q_p, k_p, cum_p = qe[:, ls], kdec[:, ls], q[rs, ls], k[rs, ls], cum_c[:, ls]
            v_pair = [v_c[:, (2 * pr + hh) * GLA_DV:(2 * pr + hh + 1) * GLA_DV] for hh in range(2)]
            upd = jnp.zeros((GLA_DV, LANES), F32)
            o_pair = []
            for hh in range(2):
                keep = lo if hh == 0 else jnp.logical_not(lo)
                o_pair.append(_nt_dot(jnp.where(keep, qe_p, 0.0).astype(BF16), st_b))
                upd = upd + _tn_dot(v_pair[hh].astype(BF16), jnp.where(keep, kd_p, 0.0).astype(BF16))
            for j in range(sub):
                dj = jnp.where(rsub >= j, jnp.exp(cum_p - cum_p[j:j + 1]), 0.0)
                w = q_p * k_p[j:j + 1] * dj
                a_lo = jnp.sum(jnp.where(lo, w, 0.0), axis=-1, keepdims=True)
                a_hi = jnp.sum(jnp.where(lo, 0.0, w), axis=-1, keepdims=True)
                o_pair[0] = o_pair[0] + a_lo * v_pair[0][j:j + 1]
                o_pair[1] = o_pair[1] + a_hi * v_pair[1][j:j + 1]
            st_ref[pr] = st * jnp.exp(last[:, ls]) + upd
            heads += o_pair
        out_rows.append(jnp.concatenate([_rms(x, gnorm) for x in heads], axis=1))
    return jnp.concatenate(out_rows, axis=0) * sgg


def _gla_seq_body(q_ref, k_ref, la_ref, v_ref, sgg_ref, gn_ref, o_ref, st_out_ref, st_sc):
    @pl.when(pl.program_id(1) == 0)
    def _():
        st_sc[...] = jnp.zeros(st_sc.shape, F32)

    o = _gla_rows(q_ref[...], k_ref[...], la_ref[...], v_ref[...].astype(F32), sgg_ref[...], gn_ref[...],
                  st_sc, GLA_SUB)
    o_ref[...] = o.astype(BF16)
    st_out_ref[0] = st_sc[...]


def gla_seq(q, k, la, v, sgg, gnorm):
    nt = SEQ // GLA_ROWS
    rows = lambda wd: pl.BlockSpec((GLA_ROWS, wd), lambda b, t: (b * nt + t, 0))
    return pl.pallas_call(
        _gla_seq_body,
        grid=(BATCH, nt),
        in_specs=[rows(GLA_K_W), rows(GLA_K_W), rows(GLA_K_W), rows(GLA_V_W), rows(GLA_V_W),
                  _full((1, GLA_DV))],
        out_specs=[rows(GLA_V_W), pl.BlockSpec((1, GLA_PAIRS, GLA_DV, LANES), lambda b, t: (b, 0, 0, 0))],
        out_shape=[jax.ShapeDtypeStruct((N_PROMPT, GLA_V_W), BF16),
                   jax.ShapeDtypeStruct((BATCH, GLA_PAIRS, GLA_DV, LANES), F32)],
        scratch_shapes=[pltpu.VMEM((GLA_PAIRS, GLA_DV, LANES), F32)],
        compiler_params=_params(2),
        name="gla_seq",
    )(q, k, la, v, sgg, gnorm.reshape(1, GLA_DV))


def _gla_step_body(q_ref, k_ref, la_ref, v_ref, sgg_ref, gn_ref, st_in_ref, o_ref, st_out_ref):
    st_out_ref[...] = st_in_ref[...]
    q, k, la, v, sgg = q_ref[...], k_ref[...], la_ref[...], v_ref[...].astype(F32), sgg_ref[...]
    for j in range(GLA_STEP_SEQS):
        rs = slice(j * DEC_SEQ, (j + 1) * DEC_SEQ)
        o = _gla_rows(q[rs], k[rs], la[rs], v[rs], sgg[rs], gn_ref[...], st_out_ref.at[j], DEC_SEQ)
        o_ref[rs, :] = o.astype(BF16)


def gla_step(q, k, la, v, sgg, gnorm, st_in, row0):
    rows_per = GLA_STEP_SEQS * DEC_SEQ
    blk0 = row0 // rows_per
    rows = lambda wd: pl.BlockSpec((rows_per, wd), lambda i: (blk0 + i, 0))
    st_spec = pl.BlockSpec((GLA_STEP_SEQS, GLA_PAIRS, GLA_DV, LANES), lambda i: (i, 0, 0, 0))
    return pl.pallas_call(
        _gla_step_body,
        grid=(DEC_BATCH // GLA_STEP_SEQS,),
        in_specs=[rows(GLA_K_W), rows(GLA_K_W), rows(GLA_K_W), rows(GLA_V_W), rows(GLA_V_W),
                  _full((1, GLA_DV)), st_spec],
        out_specs=[pl.BlockSpec((rows_per, GLA_V_W), lambda i: (i, 0)), st_spec],
        out_shape=[jax.ShapeDtypeStruct((N_SAMPLE, GLA_V_W), BF16),
                   jax.ShapeDtypeStruct((DEC_BATCH, GLA_PAIRS, GLA_DV, LANES), F32)],
        compiler_params=_params(1),
        name="gla_step",
    )(q, k, la, v, sgg, gnorm.reshape(1, GLA_DV), st_in)


def _gla_state_to_pairs(s):
    B = s.shape[0]
    return s.reshape(B, GLA_PAIRS, 2, GLA_DK, GLA_DV).transpose(0, 1, 4, 2, 3).reshape(B, GLA_PAIRS, GLA_DV, LANES)


def _gla_state_from_pairs(st):
    B = st.shape[0]
    return st.reshape(B, GLA_PAIRS, GLA_DV, 2, GLA_DK).transpose(0, 1, 3, 4, 2).reshape(B, GLA_HEADS, GLA_DK, GLA_DV)


LRU_TIME_TILE = 256


def _lru_gates(xc, wa_ref, ba, wx_ref, bx, lam):
    xcb = xc.astype(BF16)
    r_parts, i_parts = [], []
    for n in range(LRU_BLOCKS):
        xs = xcb[:, n * LRU_BW:(n + 1) * LRU_BW]
        r_parts.append(jnp.dot(xs, wa_ref[n], preferred_element_type=F32))
        i_parts.append(jnp.dot(xs, wx_ref[n], preferred_element_type=F32))
    r = jax.nn.sigmoid(jnp.concatenate(r_parts, axis=-1) + ba)
    i = jax.nn.sigmoid(jnp.concatenate(i_parts, axis=-1) + bx)
    log_a = -LRU_C * r * _softplus(-lam)
    a = jnp.exp(log_a)
    u = jnp.sqrt(1.0 - a * a) * (i * xc)
    return a, u


def _lru_seq_body(u_ref, cw_ref, cb_ref, wa_ref, ba_ref, wx_ref, bx_ref, lam_ref, y_ref, hT_ref, xp_sc, h_sc):
    tt = LRU_TIME_TILE

    @pl.when(pl.program_id(1) == 0)
    def _():
        xp_sc[0:SUBLANES, :] = jnp.zeros((SUBLANES, D_RNN), F32)
        h_sc[...] = jnp.zeros((1, D_RNN), F32)

    xp_sc[SUBLANES:SUBLANES + tt, :] = u_ref[:, D_RNN:]
    xc = cb_ref[...]
    for w in range(CONV_W):
        off = SUBLANES - (CONV_W - 1) + w
        xc = xc + cw_ref[w:w + 1, :] * xp_sc[off:off + tt, :]
    a, u = _lru_gates(xc, wa_ref, ba_ref[...], wx_ref, bx_ref[...], lam_ref[...])
    row = lax.broadcasted_iota(jnp.int32, (tt, D_RNN), 0) % SUBLANES
    d = 1
    while d < SUBLANES:
        keep = row >= d
        a_prev = jnp.where(keep, pltpu.roll(a, d, 0), 1.0)
        u_prev = jnp.where(keep, pltpu.roll(u, d, 0), 0.0)
        u = a * u_prev + u
        a = a * a_prev
        d *= 2
    carry = h_sc[...]
    hs = []
    for grp in range(tt // SUBLANES):
        rs = slice(grp * SUBLANES, (grp + 1) * SUBLANES)
        hs.append(a[rs] * carry + u[rs])
        carry = hs[-1][SUBLANES - 1:SUBLANES, :]
    h_sc[...] = carry
    hT_ref[0] = carry
    y_ref[...] = (_gelu_tanh(u_ref[:, :D_RNN]) * jnp.concatenate(hs, axis=0)).astype(BF16)
    xp_sc[0:SUBLANES, :] = xp_sc[tt:tt + SUBLANES, :]


def _lru_weight_args(cw, cb, wa, ba, wx, bx, lam):
    row = lambda v: v.reshape(1, D_RNN)
    return (cw, row(cb), wa.astype(BF16), row(ba), wx.astype(BF16), row(bx), row(lam))


_LRU_WEIGHT_SPECS = [_full((CONV_W, D_RNN)), _full((1, D_RNN)), _full((LRU_BLOCKS, LRU_BW, LRU_BW)),
                     _full((1, D_RNN)), _full((LRU_BLOCKS, LRU_BW, LRU_BW)), _full((1, D_RNN)),
                     _full((1, D_RNN))]


def lru_seq(u, cw, cb, wa, ba, wx, bx, lam):
    tt = LRU_TIME_TILE
    nt = SEQ // tt
    y, hT = pl.pallas_call(
        _lru_seq_body,
        grid=(BATCH, nt),
        in_specs=[pl.BlockSpec((tt, 2 * D_RNN), lambda b, t: (b * nt + t, 0))] + _LRU_WEIGHT_SPECS,
        out_specs=[pl.BlockSpec((tt, D_RNN), lambda b, t: (b * nt + t, 0)),
                   pl.BlockSpec((1, 1, D_RNN), lambda b, t: (b, 0, 0))],
        out_shape=[jax.ShapeDtypeStruct((N_PROMPT, D_RNN), BF16), jax.ShapeDtypeStruct((BATCH, 1, D_RNN), F32)],
        scratch_shapes=[pltpu.VMEM((tt + SUBLANES, D_RNN), F32), pltpu.VMEM((1, D_RNN), F32)],
        compiler_params=_params(2),
        name="lru_seq",
    )(u, *_lru_weight_args(cw, cb, wa, ba, wx, bx, lam))
    return y, hT.reshape(BATCH, D_RNN)


def _lru_step_body(u_ref, cs_ref, h0_ref, cw_ref, cb_ref, wa_ref, ba_ref, wx_ref, bx_ref, lam_ref, y_ref, hT_ref):
    n_t = u_ref.shape[0]
    hist = [cs_ref[:, w, :] for w in range(CONV_W - 1)] + [u_ref[t, :, D_RNN:] for t in range(n_t)]
    h = h0_ref[...]
    for t in range(n_t):
        xc = cb_ref[...]
        for w in range(CONV_W):
            xc = xc + cw_ref[w:w + 1, :] * hist[t + w]
        a, u = _lru_gates(xc, wa_ref, ba_ref[...], wx_ref, bx_ref[...], lam_ref[...])
        h = a * h + u
        y_ref[t] = (_gelu_tanh(u_ref[t, :, :D_RNN]) * h).astype(BF16)
    hT_ref[...] = h


def lru_step(u, conv_state, h0, cw, cb, wa, ba, wx, bx, lam):
    T, B, _ = u.shape
    return pl.pallas_call(
        _lru_step_body,
        out_shape=[jax.ShapeDtypeStruct((T, B, D_RNN), BF16), jax.ShapeDtypeStruct((B, D_RNN), F32)],
        compiler_params=pltpu.CompilerParams(vmem_limit_bytes=VMEM_LIMIT_BYTES),
        name="lru_step",
    )(u, conv_state, h0, *_lru_weight_args(cw, cb, wa, ba, wx, bx, lam))


ROUTE_E1, ROUTE_E2, ROUTE_G1, ROUTE_G2, ROUTE_R1, ROUTE_R2 = range(6)
EXPERT_LANE0 = N_GROUPS


def _lane_pick(val_by_lane):
    rows = next(iter(val_by_lane.values())).shape[0]
    lane = lax.broadcasted_iota(jnp.int32, (rows, LANES), 1)
    out = jnp.zeros((rows, LANES), F32)
    for l, v in val_by_lane.items():
        out = jnp.where(lane == l, v, out)
    return out


def _route_rows(logits, tri_ref, carry_ref):
    rows = logits.shape[0]
    lane = lax.broadcasted_iota(jnp.int32, (rows, LANES), 1)
    neg = -jnp.inf
    gl = jnp.where(lane < N_GROUPS, logits, neg)
    gmax = jnp.max(gl, axis=-1, keepdims=True)
    gtop = jnp.min(jnp.where(gl == gmax, lane, LANES), axis=-1, keepdims=True)
    gsum = jnp.sum(jnp.where(lane < N_GROUPS, jnp.exp(logits - gmax), 0.0), axis=-1, keepdims=True)
    g_w = 1.0 / gsum
    lo = EXPERT_LANE0 + EXP_PER_GROUP * gtop
    el = jnp.where((lane >= lo) & (lane < lo + EXP_PER_GROUP), logits, neg)
    v1 = jnp.max(el, axis=-1, keepdims=True)
    i1 = jnp.min(jnp.where(el == v1, lane, LANES), axis=-1, keepdims=True)
    el2 = jnp.where(lane == i1, neg, el)
    v2 = jnp.max(el2, axis=-1, keepdims=True)
    i2 = jnp.min(jnp.where(el2 == v2, lane, LANES), axis=-1, keepdims=True)
    p2 = jnp.exp(v2 - v1)
    den = 1.0 + p2
    gate1 = (1.0 / den) * g_w
    gate2 = (p2 / den) * g_w
    hit1 = lane == i1
    hit2 = lane == i2
    onehot = jnp.where(hit1 | hit2, 1.0, 0.0)
    before = jnp.dot(tri_ref[...], onehot.astype(BF16), preferred_element_type=F32) + carry_ref[...]
    rank1 = jnp.sum(jnp.where(hit1, before, 0.0), axis=-1, keepdims=True)
    rank2 = jnp.sum(jnp.where(hit2, before, 0.0), axis=-1, keepdims=True)
    carry_ref[...] = carry_ref[...] + jnp.sum(onehot, axis=0, keepdims=True)
    return _lane_pick({ROUTE_E1: (i1 - EXPERT_LANE0).astype(F32), ROUTE_E2: (i2 - EXPERT_LANE0).astype(F32),
                       ROUTE_G1: gate1, ROUTE_G2: gate2, ROUTE_R1: rank1, ROUTE_R2: rank2})


def _pack_bf16_halves(x):
    w = x.shape[1] // 2
    bits = lambda v: pltpu.bitcast(v.astype(F32), jnp.uint32)
    return (bits(x[:, :w]) >> 16) | bits(x[:, w:])


def _unpack_bf16_halves(p):
    lo = pltpu.bitcast(p << 16, F32).astype(BF16)
    hi = pltpu.bitcast(p & jnp.uint32(0xFFFF0000), F32).astype(BF16)
    return jnp.concatenate([lo, hi], axis=1)


def _post_mixer_body(n_mix, h_ref, *refs):
    m_refs, wo_refs = refs[:n_mix], refs[n_mix:2 * n_mix]
    g_ref, wr_ref, br_ref, tri_ref, h1_ref, hn_ref, route_ref, cnt_ref, carry_sc = refs[2 * n_mix:]

    @pl.when(pl.program_id(0) == 0)
    def _():
        carry_sc[...] = jnp.zeros((1, LANES), F32)

    mix = jnp.dot(m_refs[0][...], wo_refs[0][...], preferred_element_type=F32)
    for m_ref, wo_ref in zip(m_refs[1:], wo_refs[1:]):
        mix = mix + jnp.dot(m_ref[...], wo_ref[...], preferred_element_type=F32)
    h1 = h_ref[...] + mix
    h1_ref[...] = h1
    hn = _rms(h1, g_ref[...]).astype(BF16)
    hn_ref[...] = _pack_bf16_halves(hn)
    logits = jnp.dot(hn, wr_ref[...], preferred_element_type=F32) + br_ref[...]
    route_ref[...] = _route_rows(logits, tri_ref, carry_sc)
    cnt_ref[...] = carry_sc[...]


def post_mixer(h, mix_ins, w_out, g_ffn, w_rg, b_rg, w_re, b_re):
    n = h.shape[0]
    ks = [m.shape[1] for m in mix_ins]
    offs = np.cumsum([0] + ks)
    w_parts = [w_out[offs[j]:offs[j + 1]].astype(BF16) for j in range(len(ks))]
    pad = LANES - N_GROUPS - N_EXPERTS
    wr = jnp.concatenate([w_rg, w_re, jnp.zeros((D_MODEL, pad), F32)], axis=1).astype(BF16)
    br = jnp.concatenate([b_rg, b_re, jnp.zeros((pad,), F32)]).reshape(1, LANES)
    tri = jnp.tril(jnp.ones((ROW_TILE, ROW_TILE), BF16), -1)
    return pl.pallas_call(
        functools.partial(_post_mixer_body, len(ks)),
        grid=(n // ROW_TILE,),
        in_specs=[_rows(D_MODEL)] + [_rows(k) for k in ks] + [_full((k, D_MODEL)) for k in ks]
                 + [_full((1, D_MODEL)), _full((D_MODEL, LANES)), _full((1, LANES)), _full((ROW_TILE, ROW_TILE))],
        out_specs=[_rows(D_MODEL), _rows(D_MODEL // 2), _rows(LANES), _full((1, LANES))],
        out_shape=[jax.ShapeDtypeStruct((n, D_MODEL), F32), jax.ShapeDtypeStruct((n, D_MODEL // 2), jnp.uint32),
                   jax.ShapeDtypeStruct((n, LANES), F32), jax.ShapeDtypeStruct((1, LANES), F32)],
        scratch_shapes=[pltpu.VMEM((1, LANES), F32)],
        compiler_params=_params(1),
        name="post_mixer",
    )(h, *mix_ins, *w_parts, g_ffn.reshape(1, D_MODEL), wr, br, tri)


MOE_ROWS = 512


def _ffn_body(be_ref, nb_ref, x_ref, w1_ref, w3_ref, w2_ref, y_ref, w1_sc, w3_sc, w2_sc):
    i = pl.program_id(0)
    new_expert = jnp.logical_or(i == 0, be_ref[i] != be_ref[jnp.maximum(i - 1, 0)])

    @pl.when(jnp.logical_and(new_expert, i < nb_ref[0]))
    def _():
        w1_sc[...] = w1_ref[0, 0].astype(BF16)
        w3_sc[...] = w3_ref[0, 0].astype(BF16)
        w2_sc[...] = w2_ref[0, 0].astype(BF16)

    @pl.when(i < nb_ref[0])
    def _():
        x = _unpack_bf16_halves(x_ref[...])
        a = jnp.dot(x, w1_sc[...], preferred_element_type=F32)
        b = jnp.dot(x, w3_sc[...], preferred_element_type=F32)
        hdn = (a * jax.nn.sigmoid(a) * b).astype(BF16)
        y_ref[...] = jnp.dot(hdn, w2_sc[...], preferred_element_type=F32)

    @pl.when(i >= nb_ref[0])
    def _():
        y_ref[...] = jnp.zeros(y_ref.shape, F32)


def expert_ffn(xs, blk_exp, n_active, w1, w3, w2, layer):
    n_slots = xs.shape[0]
    nb = n_slots // MOE_ROWS
    wmap = lambda i, be, na: (layer, be[i], 0, 0)
    xmap = lambda i, be, na: (jnp.minimum(i, na[0] - 1), 0)
    return pl.pallas_call(
        _ffn_body,
        grid_spec=pltpu.PrefetchScalarGridSpec(
            num_scalar_prefetch=2,
            grid=(nb,),
            in_specs=[pl.BlockSpec((MOE_ROWS, D_MODEL // 2), xmap),
                      pl.BlockSpec((1, 1, D_MODEL, E_HID), wmap),
                      pl.BlockSpec((1, 1, D_MODEL, E_HID), wmap),
                      pl.BlockSpec((1, 1, E_HID, D_MODEL), wmap)],
            out_specs=pl.BlockSpec((MOE_ROWS, D_MODEL), lambda i, be, na: (i, 0)),
            scratch_shapes=[pltpu.VMEM((D_MODEL, E_HID), BF16), pltpu.VMEM((D_MODEL, E_HID), BF16),
                            pltpu.VMEM((E_HID, D_MODEL), BF16)],
        ),
        out_shape=jax.ShapeDtypeStruct((n_slots, D_MODEL), F32),
        compiler_params=_params(1),
        name="expert_ffn",
    )(blk_exp, n_active, xs, w1, w3, w2)


def moe_dispatch(route, counts_row, n):
    e = route[:, ROUTE_E1:ROUTE_E2 + 1].astype(jnp.int32)
    rank = route[:, ROUTE_R1:ROUTE_R2 + 1].astype(jnp.int32)
    counts = counts_row[0, EXPERT_LANE0:EXPERT_LANE0 + N_EXPERTS].astype(jnp.int32)
    padded = ((counts + MOE_ROWS - 1) // MOE_ROWS) * MOE_ROWS
    pad_end = jnp.cumsum(padded)
    pad_start = pad_end - padded
    dest = pad_start[e] + rank
    nb = -(-(n * TOP_K_IN_GROUP) // MOE_ROWS) + N_EXPERTS
    n_slots = nb * MOE_ROWS
    tok = jnp.broadcast_to(jnp.arange(n, dtype=jnp.int32)[:, None], (n, TOP_K_IN_GROUP))
    slot_tok = jnp.zeros((n_slots,), jnp.int32).at[dest.reshape(-1)].set(
        tok.reshape(-1), unique_indices=True, mode='promise_in_bounds')
    blk_start = jnp.arange(nb, dtype=jnp.int32) * MOE_ROWS
    blk_exp = jnp.sum((pad_end[None, :] <= blk_start[:, None]).astype(jnp.int32), axis=1)
    blk_exp = jnp.minimum(blk_exp, N_EXPERTS - 1)
    n_active = (pad_end[-1] // MOE_ROWS).astype(jnp.int32).reshape(1)
    return slot_tok, dest, blk_exp, n_active


def moe_experts(hn, route, counts_row, w1, w3, w2, layer):
    n = hn.shape[0]
    slot_tok, dest, blk_exp, n_active = moe_dispatch(route, counts_row, n)
    xs = hn.at[slot_tok].get(mode='promise_in_bounds')
    ys = expert_ffn(xs, blk_exp, n_active, w1, w3, w2, layer)
    return [ys.at[dest[:, j]].get(mode='promise_in_bounds') for j in range(TOP_K_IN_GROUP)]


def _combine(h_ref, y1_ref, y2_ref, route_ref):
    lane = lax.broadcasted_iota(jnp.int32, (ROW_TILE, LANES), 1)
    r = route_ref[...]
    g1 = jnp.sum(jnp.where(lane == ROUTE_G1, r, 0.0), axis=-1, keepdims=True)
    g2 = jnp.sum(jnp.where(lane == ROUTE_G2, r, 0.0), axis=-1, keepdims=True)
    return h_ref[...] + (y1_ref[...] * g1 + y2_ref[...] * g2)


def _combine_proj_body(h_ref, y1_ref, y2_ref, route_ref, g_ref, w_ref, h2_ref, u_ref):
    h2 = _combine(h_ref, y1_ref, y2_ref, route_ref)
    h2_ref[...] = h2
    u_ref[...] = jnp.dot(_rms(h2, g_ref[...]).astype(BF16), w_ref[...], preferred_element_type=F32)


def _combine_norm_body(h_ref, y1_ref, y2_ref, route_ref, g_ref, y_ref):
    y_ref[...] = _rms(_combine(h_ref, y1_ref, y2_ref, route_ref), g_ref[...])


def combine_proj(h, ys, route, g, w):
    n = h.shape[0]
    nn = w.shape[1]
    return pl.pallas_call(
        _combine_proj_body,
        grid=(n // ROW_TILE,),
        in_specs=[_rows(D_MODEL), _rows(D_MODEL), _rows(D_MODEL), _rows(LANES), _full((1, D_MODEL)),
                  _full((D_MODEL, nn))],
        out_specs=[_rows(D_MODEL), _rows(nn)],
        out_shape=[jax.ShapeDtypeStruct((n, D_MODEL), F32), jax.ShapeDtypeStruct((n, nn), F32)],
        compiler_params=_params(1),
        name="combine_proj",
    )(h, *ys, route, g.reshape(1, D_MODEL), w.astype(BF16))


def combine_norm(h, ys, route, g):
    n = h.shape[0]
    return pl.pallas_call(
        _combine_norm_body,
        grid=(n // ROW_TILE,),
        in_specs=[_rows(D_MODEL), _rows(D_MODEL), _rows(D_MODEL), _rows(LANES), _full((1, D_MODEL))],
        out_specs=_rows(D_MODEL),
        out_shape=jax.ShapeDtypeStruct((n, D_MODEL), F32),
        compiler_params=_params(1),
        name="combine_norm",
    )(h, *ys, route, g.reshape(1, D_MODEL))


def _even_first(x, axis):
    n = x.shape[axis]
    idx = jnp.concatenate([jnp.arange(0, n, 2), jnp.arange(1, n, 2)])
    return jnp.take(x, idx, axis=axis)


def _decode_query_cols(q_s, gates_s):
    B, T, G, R = DEC_BATCH, DEC_SEQ, NSA_KV, NSA_REP
    qg = q_s.reshape(B, T, G, R, HD).transpose(0, 2, 4, 3, 1).reshape(B, G, HD, R * T)
    qt = jnp.zeros((B, G, HD, G, R * T), BF16)
    for g in range(G):
        qt = qt.at[:, g, :, g, :].set(qg[:, g])
    qt = jnp.pad(qt.reshape(B, G * HD, G * R * T), ((0, 0), (0, 0), (0, LANES - G * R * T)))
    gr = gates_s[:, :3 * NSA_HEADS].reshape(B, T, NSA_HEADS, 3).transpose(0, 3, 2, 1).reshape(B, 3, NSA_HEADS * T)
    return qt, jnp.pad(gr, ((0, 0), (0, 0), (0, LANES - NSA_HEADS * T)))


def _feature_major(cache):
    lead, rows = cache.shape[:2]
    return jnp.transpose(cache, (0, 2, 3, 4, 1)).reshape(lead, NSA_KV_W, rows)


def _kv_rows_from_feature_major(xt, lead):
    x = xt.reshape(2, NSA_KV, HD, *lead)
    n = len(lead)
    return jnp.transpose(x, tuple(range(3, 3 + n)) + (0, 1, 2))


def mixer_a(h, p, past):
    pos = jnp.concatenate([jnp.tile(jnp.arange(SEQ, dtype=jnp.int32), BATCH),
                           PAST_LEN + jnp.tile(jnp.arange(DEC_SEQ, dtype=jnp.int32), DEC_BATCH)])
    (q, kvc, kvs, kvw, kvct, kvst, kvwt, ks, vst, kw, vwt, gates, gq, gk, gv, la, sgg) = inproj_a(
        h, p['norm_mix'][0], p['a_w_in'][0], p['a_gla_wa2'][0], p['a_gla_ba'][0], pos)
    pe, wc, gnorm = p['a_cmp_pe'][0], p['a_cmp_w'][0], p['a_gla_norm'][0]
    P = N_PROMPT
    ckv = cmp_blocks_rows(kvc, P, pe, wc)
    ckv = _even_first(ckv.reshape(BATCH, N_CMP_PROMPT, NSA_KV_W), 1)
    o_nsa_p = nsa_prompt_attn(q, gates, ckv, ks, vst, kw, vwt)
    o_gla_p, st_p = gla_seq(gq, gk, la, gv, sgg, gnorm)
    win_p = jnp.concatenate([kvwt[:, (b + 1) * SEQ - WINDOW:(b + 1) * SEQ] for b in range(BATCH)], axis=1)
    new_p = (_kv_rows_from_feature_major(kvct[:, :P], (BATCH, SEQ)),
             _kv_rows_from_feature_major(kvst[:, :P], (BATCH, SEQ)),
             _kv_rows_from_feature_major(win_p, (BATCH, WINDOW)), _gla_state_from_pairs(st_p))
    o_gla_s, st_s = gla_step(gq, gk, la, gv, sgg, gnorm, _gla_state_to_pairs(past['state_gla'][0]), P)
    n_pool = past['cache_cmp_kv'].shape[1]
    ckv_pool = cmp_blocks_pages(_feature_major(past['cache_cmp_kv'][0]), pe, wc)
    ckv_seq = ckv_pool.reshape(n_pool, PAGE_SIZE // CMP_BLOCK, NSA_KV_W)[past['page_table']]
    ckv_seq = _even_first(ckv_seq.reshape(DEC_BATCH, N_CMP_DEC, NSA_KV_W), 1)
    qt, gate_rows = _decode_query_cols(q[P:], gates[P:])
    per_seq = lambda x: x[P:].reshape(DEC_BATCH, 1, DEC_SEQ * NSA_KV_W)
    o_t, win_new = nsa_decode_attn(qt, gate_rows, ckv_seq, _feature_major(past['cache_slc_kv'][0]),
                                   _feature_major(past['cache_win_kv'][0]), per_seq(kvs), per_seq(kvw),
                                   past['page_table'])
    o_nsa_s = o_t[:, :, :NSA_HEADS * DEC_SEQ].reshape(DEC_BATCH, HD, NSA_HEADS, DEC_SEQ)
    o_nsa_s = o_nsa_s.transpose(0, 3, 2, 1).reshape(N_SAMPLE, NSA_Q_W).astype(BF16)
    n_buf = win_new.shape[2]
    win_s = jnp.transpose(win_new.reshape(DEC_BATCH, 2, NSA_KV, HD, n_buf), (0, 4, 1, 2, 3))
    new_s = (_kv_rows_from_feature_major(kvct[:, P:], (DEC_BATCH, DEC_SEQ)),
             _kv_rows_from_feature_major(kvst[:, P:], (DEC_BATCH, DEC_SEQ)), win_s, _gla_state_from_pairs(st_s))
    o_nsa = jnp.concatenate([o_nsa_p, o_nsa_s], axis=0)
    o_gla = jnp.concatenate([o_gla_p, o_gla_s], axis=0)
    return o_nsa, o_gla, new_p, new_s


def run_trunk(x_prompt, x_sample, p, past):
    h = jnp.concatenate([x_prompt.reshape(N_PROMPT, D_MODEL), x_sample.reshape(N_SAMPLE, D_MODEL)], axis=0)
    o_nsa, o_gla, new_p, new_s = mixer_a(h, p, past)
    h, hn, route, counts = post_mixer(h, [o_nsa, o_gla], p['a_w_out'][0], p['norm_ffn'][0], p['m_w_rg'][0],
                                      p['m_b_rg'][0], p['m_w_re'][0], p['m_b_re'][0])
    ys = moe_experts(hn, route, counts, p['m_w1'], p['m_w3'], p['m_w2'], 0)
    h, u = combine_proj(h, ys, route, p['norm_mix'][1], p['c_w_in'][0])
    lru_w = (p['c_conv_w'][0], p['c_conv_b'][0], p['c_w_a'][0], p['c_b_a'][0], p['c_w_x'][0], p['c_b_x'][0],
             p['c_lam'][0])
    us = u[N_PROMPT:].reshape(DEC_BATCH, DEC_SEQ, 2 * D_RNN)
    y_p, lru_p = lru_seq(u, *lru_w)
    y_s, lru_s = lru_step(jnp.swapaxes(us, 0, 1), past['state_conv'][0], past['state_lru'][0], *lru_w)
    conv_p = jnp.stack([u[(b + 1) * SEQ - (CONV_W - 1):(b + 1) * SEQ, D_RNN:] for b in range(BATCH)])
    conv_s = us[:, DEC_SEQ - (CONV_W - 1):, D_RNN:]
    mix_in = jnp.concatenate([y_p, jnp.swapaxes(y_s, 0, 1).reshape(N_SAMPLE, D_RNN)], axis=0)
    h, hn, route, counts = post_mixer(h, [mix_in], p['c_w_out'][0], p['norm_ffn'][1], p['m_w_rg'][1],
                                      p['m_b_rg'][1], p['m_w_re'][1], p['m_b_re'][1])
    ys = moe_experts(hn, route, counts, p['m_w1'], p['m_w3'], p['m_w2'], 1)
    y = combine_norm(h, ys, route, p['norm_final'])
    y_prompt = y[:N_PROMPT].reshape(BATCH, SEQ, D_MODEL)
    y_sample = y[N_PROMPT:].reshape(DEC_BATCH, DEC_SEQ, D_MODEL)
    return (y_prompt, y_sample), new_p + (lru_p, conv_p), new_s + (lru_s, conv_s)


def kernel(x_prompt, x_sample, cache_cmp_kv, cache_slc_kv, cache_win_kv, state_gla, state_lru, state_conv,
           page_table, norm_mix, norm_ffn, norm_final, a_w_in, a_cmp_pe, a_cmp_w, a_gla_wa2, a_gla_ba,
           a_gla_norm, a_w_out, c_w_in, c_conv_w, c_conv_b, c_w_a, c_b_a, c_w_x, c_b_x, c_lam, c_w_out,
           m_w_rg, m_b_rg, m_w_re, m_b_re, m_w1, m_w3, m_w2):
    p = {'norm_mix': norm_mix, 'norm_ffn': norm_ffn, 'norm_final': norm_final,
         'a_w_in': a_w_in, 'a_cmp_pe': a_cmp_pe, 'a_cmp_w': a_cmp_w, 'a_gla_wa2': a_gla_wa2,
         'a_gla_ba': a_gla_ba, 'a_gla_norm': a_gla_norm, 'a_w_out': a_w_out,
         'c_w_in': c_w_in, 'c_conv_w': c_conv_w, 'c_conv_b': c_conv_b, 'c_w_a': c_w_a, 'c_b_a': c_b_a,
         'c_w_x': c_w_x, 'c_b_x': c_b_x, 'c_lam': c_lam, 'c_w_out': c_w_out,
         'm_w_rg': m_w_rg, 'm_b_rg': m_b_rg, 'm_w_re': m_w_re, 'm_b_re': m_b_re,
         'm_w1': m_w1, 'm_w3': m_w3, 'm_w2': m_w2}
    past = {'cache_cmp_kv': cache_cmp_kv, 'cache_slc_kv': cache_slc_kv, 'cache_win_kv': cache_win_kv,
            'state_gla': state_gla, 'state_lru': state_lru, 'state_conv': state_conv,
            'page_table': page_table}
    (y_p, y_s), sp, ss = run_trunk(x_prompt, x_sample, p, past)
    outs = [y_p, y_s]
    for a, b in zip(sp, ss):
        outs += [a[None], b[None]]
    return tuple(outs)
```

```python
import functools
import jax, jax.numpy as jnp
from jax import lax
import numpy as np
from jax.experimental import pallas as pl
from jax.experimental.pallas import tpu as pltpu

D_MODEL = 1024
BATCH = 2
SEQ = 8192
DEC_BATCH = 128
DEC_SEQ = 4
PAST_LEN = 2048
PAGE_SIZE = 128
EPS = 1e-6
NSA_HEADS = 8
NSA_KV = 2
NSA_REP = NSA_HEADS // NSA_KV
HD = 64
CMP_BLOCK = 32
SLC_BLOCK = 64
SLC_TOPK = 16
WINDOW = 512
Q_BLOCK = 128
ROPE_DIM = HD // 4
ROPE_THETA = 500000.0
GLA_HEADS = 4
GLA_DK = 64
GLA_DV = 128
GLA_LOWRANK = 16
GLA_TAU = 16.0
D_RNN = 1280
LRU_BLOCKS = 10
LRU_BW = D_RNN // LRU_BLOCKS
CONV_W = 4
LRU_C = 8.0
N_GROUPS = 4
EXP_PER_GROUP = 8
N_EXPERTS = N_GROUPS * EXP_PER_GROUP
E_HID = 512
TOP_K_IN_GROUP = 2
NSA_Q_W = NSA_HEADS * HD
NSA_KV_W = 2 * NSA_KV * HD
GLA_K_W = GLA_HEADS * GLA_DK
GLA_V_W = GLA_HEADS * GLA_DV
A_SIZES = (NSA_Q_W, NSA_KV_W, NSA_KV_W, NSA_KV_W, 3 * NSA_HEADS, GLA_K_W, GLA_K_W, GLA_V_W, GLA_LOWRANK, GLA_V_W)
N_PROMPT = BATCH * SEQ
N_SAMPLE = DEC_BATCH * DEC_SEQ
N_TOK = N_PROMPT + N_SAMPLE
N_PAGES = PAST_LEN // PAGE_SIZE

F32 = jnp.float32
BF16 = jnp.bfloat16
VMEM_LIMIT_BYTES = 56 * 1024 * 1024
LANES = 128
SUBLANES = 8
ROW_TILE = 512


def _params(n_axes):
    return pltpu.CompilerParams(dimension_semantics=("arbitrary",) * n_axes, vmem_limit_bytes=VMEM_LIMIT_BYTES)


def _full(shape):
    return pl.BlockSpec(shape, lambda *_: (0,) * len(shape))


def _rows(width):
    return pl.BlockSpec((ROW_TILE, width), lambda i: (i, 0))


def _rms(x, g):
    return x * lax.rsqrt(jnp.mean(x * x, axis=-1, keepdims=True) + EPS) * g


def _softplus(x):
    return jnp.maximum(x, 0.0) + jnp.log1p(jnp.exp(-jnp.abs(x)))


def _gelu_tanh(x):
    return x * (0.5 * (1.0 + jnp.tanh(0.7978845608028654 * (x + 0.044715 * (x * x * x)))))


def _nt_dot(a, b):
    return lax.dot_general(a, b, (((1,), (1,)), ((), ())), preferred_element_type=F32)


def _tn_dot(a, b):
    return lax.dot_general(a, b, (((0,), (0,)), ((), ())), preferred_element_type=F32)


A_Q0, A_KVC0, A_KVS0, A_KVW0 = 0, 512, 768, 1024
A_GQ0, A_GK0, A_GV0, A_GG0, A_MISC0 = 1280, 1536, 1792, 2304, 2816
A_COLS = A_MISC0 + LANES
MISC_LR0 = 3 * NSA_HEADS
KV_HALF = NSA_KV * HD


def _rope_lanes(x, cos_t, sin_lo, sin_hi):
    reps = x.shape[1] // LANES
    tile = (lambda t: jnp.concatenate([t] * reps, axis=1)) if reps > 1 else (lambda t: t)
    w = x.shape[1]
    half = ROPE_DIM // 2
    return x * tile(cos_t) + pltpu.roll(x, half, 1) * tile(sin_hi) + pltpu.roll(x, w - half, 1) * tile(sin_lo)


def _inproj_a_body(h_ref, g_ref, w_ref, wa2_ref, ba_ref, cos_ref, slo_ref, shi_ref,
                   q_ref, kvc_ref, kvs_ref, kvw_ref, kvct_ref, kvst_ref, kvwt_ref, ks_ref, vst_ref, kw_ref,
                   vwt_ref, gates_ref, gq_ref, gk_ref, gv_ref, la_ref, sgg_ref):
    y = _rms(h_ref[...], g_ref[...]).astype(BF16)
    proj = lambda a, b: jnp.dot(y, w_ref[:, a:b], preferred_element_type=F32)
    cos_t, sin_lo, sin_hi = cos_ref[...], slo_ref[...], shi_ref[...]
    q_ref[...] = (_rope_lanes(proj(A_Q0, A_KVC0), cos_t, sin_lo, sin_hi) * (HD ** -0.5)).astype(BF16)
    kvc = proj(A_KVC0, A_KVS0)
    kvc_ref[...] = kvc
    kvct_ref[...] = kvc.T
    for a0, kv_ref, kvt_ref, k_ref, vt_ref in ((A_KVS0, kvs_ref, kvst_ref, ks_ref, vst_ref),
                                               (A_KVW0, kvw_ref, kvwt_ref, kw_ref, vwt_ref)):
        kv = proj(a0, a0 + 2 * KV_HALF)
        k = _rope_lanes(kv[:, :KV_HALF], cos_t, sin_lo, sin_hi)
        vt = kv[:, KV_HALF:].T
        kv_ref[:, :KV_HALF] = k
        kv_ref[:, KV_HALF:] = kv[:, KV_HALF:]
        kvt_ref[:KV_HALF, :] = k.T
        kvt_ref[KV_HALF:, :] = vt
        k_ref[...] = k.astype(BF16)
        vt_ref[...] = vt.astype(BF16)
    misc = proj(A_MISC0, A_COLS)
    gates_ref[...] = jax.nn.sigmoid(misc)
    z = jnp.dot(misc.astype(BF16), wa2_ref[...], preferred_element_type=F32) + ba_ref[...]
    la_ref[...] = -_softplus(-z) * (1.0 / GLA_TAU)
    gq_ref[...] = proj(A_GQ0, A_GK0) * (GLA_DK ** -0.5)
    gk_ref[...] = proj(A_GK0, A_GV0)
    gv_ref[...] = proj(A_GV0, A_GG0).astype(BF16)
    gg = proj(A_GG0, A_MISC0)
    sgg_ref[...] = gg * jax.nn.sigmoid(gg)


def _rope_tables(pos):
    half = ROPE_DIM // 2
    inv = 1.0 / (ROPE_THETA ** (jnp.arange(0, ROPE_DIM, 2, dtype=F32) / ROPE_DIM))
    ang = pos.astype(F32)[:, None] * inv[None, :]
    cos, sin = jnp.cos(ang), jnp.sin(ang)
    n = pos.shape[0]
    one = jnp.ones((n, HD - ROPE_DIM), F32)
    zero = jnp.zeros((n, HD - ROPE_DIM), F32)
    zh = jnp.zeros((n, half), F32)
    seg = lambda a, b, rest: jnp.concatenate([a, b, rest] * (LANES // HD), axis=1)
    return seg(cos, cos, one), seg(-sin, zh, zero), seg(zh, sin, zero)


def inproj_a(h, g, w_in, wa2, ba, pos):
    n = h.shape[0]
    zpad = jnp.zeros((D_MODEL, LANES - 3 * NSA_HEADS - GLA_LOWRANK), F32)
    o = np.cumsum((0,) + A_SIZES)
    w = jnp.concatenate([w_in[:, o[0]:o[4]], w_in[:, o[5]:o[8]], w_in[:, o[9]:o[10]],
                         w_in[:, o[4]:o[5]], w_in[:, o[8]:o[9]], zpad], axis=1).astype(BF16)
    wa2p = jnp.zeros((LANES, GLA_K_W), F32).at[MISC_LR0:MISC_LR0 + GLA_LOWRANK].set(wa2).astype(BF16)
    cols = pl.BlockSpec((KV_HALF, ROW_TILE), lambda i: (0, i))
    kvt = (pl.BlockSpec((NSA_KV_W, ROW_TILE), lambda i: (0, i)), (NSA_KV_W, n), F32)
    outs = [(_rows(NSA_Q_W), (n, NSA_Q_W), BF16), (_rows(NSA_KV_W), (n, NSA_KV_W), F32),
            (_rows(NSA_KV_W), (n, NSA_KV_W), F32), (_rows(NSA_KV_W), (n, NSA_KV_W), F32), kvt, kvt, kvt,
            (_rows(KV_HALF), (n, KV_HALF), BF16), (cols, (KV_HALF, n), BF16),
            (_rows(KV_HALF), (n, KV_HALF), BF16), (cols, (KV_HALF, n), BF16),
            (_rows(LANES), (n, LANES), F32), (_rows(GLA_K_W), (n, GLA_K_W), F32),
            (_rows(GLA_K_W), (n, GLA_K_W), F32), (_rows(GLA_V_W), (n, GLA_V_W), BF16),
            (_rows(GLA_K_W), (n, GLA_K_W), F32), (_rows(GLA_V_W), (n, GLA_V_W), F32)]
    return pl.pallas_call(
        _inproj_a_body,
        grid=(n // ROW_TILE,),
        in_specs=[_rows(D_MODEL), _full((1, D_MODEL)), _full((D_MODEL, A_COLS)), _full((LANES, GLA_K_W)),
                  _full((1, GLA_K_W)), _rows(LANES), _rows(LANES), _rows(LANES)],
        out_specs=[s for s, _, _ in outs],
        out_shape=[jax.ShapeDtypeStruct(shape, dt) for _, shape, dt in outs],
        compiler_params=_params(1),
        name="inproj_a",
    )(h, g.reshape(1, D_MODEL), w, wa2p, ba.reshape(1, GLA_K_W), *_rope_tables(pos))


CMP_TILE_BLOCKS = 256
CMP_TILE_ROWS = CMP_TILE_BLOCKS * CMP_BLOCK
CMP_TILE_PAGES = CMP_TILE_ROWS // PAGE_SIZE


def _cmp_reduce(xk_ref, xv_ref, pe_ref, w_ref):
    acc = jnp.zeros((CMP_TILE_BLOCKS, NSA_KV_W), F32)
    for l in range(CMP_BLOCK):
        rows = pl.ds(l, CMP_TILE_BLOCKS, stride=CMP_BLOCK)
        xl = jnp.concatenate([xk_ref[rows, :], xv_ref[rows, :]], axis=1) + pe_ref[l:l + 1, :]
        acc = acc + jnp.dot(xl.astype(BF16), w_ref[l], preferred_element_type=F32)
    return acc


def _cmp_rows_body(xk_ref, xv_ref, pe_ref, w_ref, o_ref):
    o_ref[...] = _cmp_reduce(xk_ref, xv_ref, pe_ref, w_ref)


def _cmp_pages_body(x_ref, pe_ref, w_ref, o_ref, xk_sc, xv_sc):
    for pg in range(CMP_TILE_PAGES):
        rows = slice(pg * PAGE_SIZE, (pg + 1) * PAGE_SIZE)
        xk_sc[rows, :] = x_ref[pg, :KV_HALF, :].T
        xv_sc[rows, :] = x_ref[pg, KV_HALF:, :].T
    o_ref[...] = _cmp_reduce(xk_sc, xv_sc, pe_ref, w_ref)


def _cmp_weights(pe, w_cmp):
    pe_rows = jnp.broadcast_to(pe[:, :, None, :], (CMP_BLOCK, 2, NSA_KV, HD)).reshape(CMP_BLOCK, NSA_KV_W)
    w_bd = jnp.einsum('lcde,cx,gy->lcgdxye', w_cmp, jnp.eye(2, dtype=F32), jnp.eye(NSA_KV, dtype=F32))
    return pe_rows, w_bd.reshape(CMP_BLOCK, NSA_KV_W, NSA_KV_W).astype(BF16)


def cmp_blocks_rows(x, n_rows, pe, w_cmp):
    return pl.pallas_call(
        _cmp_rows_body,
        grid=(n_rows // CMP_TILE_ROWS,),
        in_specs=[pl.BlockSpec((CMP_TILE_ROWS, KV_HALF), lambda i: (i, 0)),
                  pl.BlockSpec((CMP_TILE_ROWS, KV_HALF), lambda i: (i, 1)), _full((CMP_BLOCK, NSA_KV_W)),
                  _full((CMP_BLOCK, NSA_KV_W, NSA_KV_W))],
        out_specs=pl.BlockSpec((CMP_TILE_BLOCKS, NSA_KV_W), lambda i: (i, 0)),
        out_shape=jax.ShapeDtypeStruct((n_rows // CMP_BLOCK, NSA_KV_W), F32),
        compiler_params=_params(1),
        name="cmp_blocks_rows",
    )(x, x, *_cmp_weights(pe, w_cmp))


def cmp_blocks_pages(xt, pe, w_cmp):
    n_pages = xt.shape[0]
    return pl.pallas_call(
        _cmp_pages_body,
        grid=(n_pages // CMP_TILE_PAGES,),
        in_specs=[pl.BlockSpec((CMP_TILE_PAGES, NSA_KV_W, PAGE_SIZE), lambda i: (i, 0, 0)),
                  _full((CMP_BLOCK, NSA_KV_W)), _full((CMP_BLOCK, NSA_KV_W, NSA_KV_W))],
        out_specs=pl.BlockSpec((CMP_TILE_BLOCKS, NSA_KV_W), lambda i: (i, 0)),
        out_shape=jax.ShapeDtypeStruct((n_pages * PAGE_SIZE // CMP_BLOCK, NSA_KV_W), F32),
        scratch_shapes=[pltpu.VMEM((CMP_TILE_ROWS, KV_HALF), F32), pltpu.VMEM((CMP_TILE_ROWS, KV_HALF), F32)],
        compiler_params=_params(1),
        name="cmp_blocks_pages",
    )(xt, *_cmp_weights(pe, w_cmp))


KEY_TILE = 256
NEG_BIG = -1e30
N_CMP_PROMPT = SEQ // CMP_BLOCK
N_SLC_PROMPT = SEQ // SLC_BLOCK
SLC_SHIFT = SLC_BLOCK.bit_length() - 1
N_CMP_DEC = PAST_LEN // CMP_BLOCK
N_SLC_DEC = -(-(PAST_LEN + DEC_SEQ) // SLC_BLOCK)
N_SLC_DEC_PAD = -(-N_SLC_DEC // LANES) * LANES
DEC_COLS_PER_GROUP = NSA_REP * DEC_SEQ


def _tile_cols(x, reps):
    return jnp.concatenate([x] * reps, axis=1) if reps > 1 else x


def _even_first_cmp_end(n_cmp):
    j = lax.broadcasted_iota(jnp.int32, (n_cmp, 1), 0)
    blk = jnp.where(j < n_cmp // 2, 2 * j, 2 * (j - n_cmp // 2) + 1)
    return (blk + 1) * CMP_BLOCK - 1


def _softmax_cols(s, mask):
    s = jnp.where(mask, s, -jnp.inf)
    m = jnp.max(s, axis=0, keepdims=True)
    m = jnp.where(m > -jnp.inf, m, 0.0)
    e = jnp.where(mask, jnp.exp(s - m), 0.0)
    return e / jnp.maximum(jnp.sum(e, axis=0, keepdims=True), 1e-30)


def _select_cols(p_slc, qpos, n_top):
    ns = p_slc.shape[0]
    blk = lax.broadcasted_iota(jnp.int32, p_slc.shape, 0)
    cur = qpos >> SLC_SHIFT
    forced = (blk == 0) | (blk == cur) | (blk == cur - 1)
    score = jnp.where(forced, jnp.inf, p_slc)
    score = jnp.where(blk <= cur, score, -jnp.inf)
    sel = jnp.zeros(p_slc.shape, F32)
    for _ in range(n_top):
        m = jnp.max(score, axis=0, keepdims=True)
        idx = jnp.min(jnp.where(score == m, blk, ns), axis=0, keepdims=True)
        hit = blk == idx
        sel = jnp.where(hit & (m > -jnp.inf), 1.0, sel)
        score = jnp.where(hit, -jnp.inf, score)
    return sel


def _flash_init(m_sc, l_sc, acc_sc):
    m_sc[...] = jnp.full(m_sc.shape, NEG_BIG, F32)
    l_sc[...] = jnp.zeros(l_sc.shape, F32)
    acc_sc[...] = jnp.zeros(acc_sc.shape, F32)


def _flash_cols(scores, mask, pv, m_sc, l_sc, acc_sc):
    s = jnp.where(mask, scores, NEG_BIG)
    m_old = m_sc[...]
    m_new = jnp.maximum(m_old, jnp.max(s, axis=0, keepdims=True))
    alpha = jnp.exp(m_old - m_new)
    p = jnp.where(mask, jnp.exp(s - m_new), 0.0)
    l_sc[...] = alpha * l_sc[...] + jnp.sum(p, axis=0, keepdims=True)
    acc_sc[...] = alpha * acc_sc[...] + pv(p.astype(BF16))
    m_sc[...] = m_new


def _flash_stream(score_fn, pv_fn, first, lo, n, stream_sc, m_sc, l_sc, acc_sc):
    (sa, ca), (sb, cb) = stream_sc

    def issue(s_ref, c_ref, kt, self_tile=False):
        s = score_fn(kt, self_tile)
        s_ref[...] = s
        c_ref[...] = jnp.max(s, axis=0, keepdims=True)

    def consume(s_ref, c_ref, kt):
        m_old = m_sc[...]
        m_new = jnp.maximum(m_old, c_ref[...])
        alpha = jnp.exp(m_old - m_new)
        p = jnp.exp(s_ref[...] - m_new)
        l_sc[...] = alpha * l_sc[...] + jnp.sum(p, axis=0, keepdims=True)
        acc_sc[...] = alpha * acc_sc[...] + pv_fn(kt, p.astype(BF16))
        m_sc[...] = m_new

    _flash_init(m_sc, l_sc, acc_sc)
    issue(sa, ca, first, True)

    def two_tiles(jj, kt_a):
        t0 = lo + 2 * jj
        issue(sb, cb, t0)
        consume(sa, ca, kt_a)
        issue(sa, ca, t0 + 1)
        consume(sb, cb, t0)
        return t0 + 1

    kt_a = lax.fori_loop(0, n // 2, two_tiles, first)

    @pl.when(n % 2 == 1)
    def _():
        issue(sb, cb, lo + n - 1)
        consume(sa, ca, kt_a)
        consume(sb, cb, lo + n - 1)

    @pl.when(n % 2 == 0)
    def _():
        consume(sa, ca, kt_a)

    return acc_sc[...] / jnp.maximum(l_sc[...], 1e-30)


def _flash_out(l_sc, acc_sc):
    return acc_sc[...] / jnp.maximum(l_sc[...], 1e-30)


def _pv_split(vt, p):
    c2 = p.shape[1] // 2
    return jnp.concatenate([jnp.dot(vt[:HD], p[:, :c2], preferred_element_type=F32),
                            jnp.dot(vt[HD:], p[:, c2:], preferred_element_type=F32)], axis=1)


def _nsa_prompt_body(q_ref, gates_ref, ckv_ref, cos_ref, slo_ref, shi_ref, ks_ref, vst_ref, kw_ref, vwt_ref,
                     o_ref, m_sc, l_sc, acc_sc, sel_bias_sc, sa_sc, ca_sc, sb_sc, cb_sc):
    stream_sc = ((sa_sc, ca_sc), (sb_sc, cb_sc))
    i = pl.program_id(1)
    nq = Q_BLOCK
    cols = NSA_HEADS * nq
    qpos = i * nq + lax.broadcasted_iota(jnp.int32, (1, nq), 1)
    q = q_ref[...].astype(F32)
    pairs = [q[:, j * LANES:(j + 1) * LANES].T for j in range(NSA_HEADS // 2)]
    zero = jnp.zeros((HD, cols // 2), F32)
    qt_g = [jnp.concatenate([pairs[2 * g][:HD], pairs[2 * g][HD:], pairs[2 * g + 1][:HD], pairs[2 * g + 1][HD:]],
                            axis=1) for g in range(NSA_KV)]
    qt = jnp.concatenate([jnp.concatenate([qt_g[0], zero], axis=1),
                          jnp.concatenate([zero, qt_g[1]], axis=1)], axis=0).astype(BF16)
    ckv = ckv_ref[0]
    ck = _rope_lanes(ckv[:, :KV_HALF], cos_ref[...], slo_ref[...], shi_ref[...]).astype(BF16)
    cvt = ckv[:, KV_HALF:].T.astype(BF16)
    c_mask = _even_first_cmp_end(N_CMP_PROMPT) <= qpos
    p = _softmax_cols(jnp.dot(ck, qt, preferred_element_type=F32), _tile_cols(c_mask, NSA_HEADS))
    o_c = _pv_split(cvt, p.astype(BF16))
    sel = []
    for g in range(NSA_KV):
        c0 = g * NSA_REP * nq
        p_grp = p[:, c0:c0 + nq]
        for r in range(1, NSA_REP):
            p_grp = p_grp + p[:, c0 + r * nq:c0 + (r + 1) * nq]
        p_slc = p_grp[:N_CMP_PROMPT // 2] + p_grp[N_CMP_PROMPT // 2:]
        sel.append(_select_cols(p_slc, qpos, SLC_TOPK))
    sel = jnp.concatenate(sel, axis=1)
    sel_bias_sc[...] = jnp.where(sel > 0.5, 0.0, NEG_BIG)
    key_row = lax.broadcasted_iota(jnp.int32, (KEY_TILE, 1), 0)
    blocks_per_tile = KEY_TILE // SLC_BLOCK
    kt_self = i // (KEY_TILE // nq)

    def slc_scores(kt, self_tile):
        k0 = pl.multiple_of(kt * KEY_TILE, KEY_TILE)
        rows = []
        for j in range(blocks_per_tile):
            b = sel_bias_sc[pl.ds(kt * blocks_per_tile + j, 1), :]
            rows.append(jnp.concatenate(
                [jnp.broadcast_to(b[:, g * nq:(g + 1) * nq], (SLC_BLOCK, nq)) for g in range(NSA_KV)
                 for _ in range(NSA_REP)], axis=1))
        bias = jnp.concatenate(rows, axis=0)
        if self_tile:
            bias = bias + _tile_cols(jnp.where(k0 + key_row <= qpos, 0.0, NEG_BIG), NSA_HEADS)
        return jnp.dot(ks_ref[pl.ds(k0, KEY_TILE), :], qt, preferred_element_type=F32) + bias

    def slc_pv(kt, pb):
        return _pv_split(vst_ref[:, pl.ds(pl.multiple_of(kt * KEY_TILE, KEY_TILE), KEY_TILE)], pb)

    o_s = _flash_stream(slc_scores, slc_pv, kt_self, 0, kt_self, stream_sc, m_sc, l_sc, acc_sc)

    def win_scores(kt, self_tile):
        k0 = pl.multiple_of(kt * KEY_TILE, KEY_TILE)
        d = qpos - (k0 + key_row)
        bias = jnp.where((d >= 0) & (d < WINDOW), 0.0, NEG_BIG)
        return jnp.dot(kw_ref[pl.ds(k0, KEY_TILE), :], qt, preferred_element_type=F32) + _tile_cols(bias, NSA_HEADS)

    def win_pv(kt, pb):
        return _pv_split(vwt_ref[:, pl.ds(pl.multiple_of(kt * KEY_TILE, KEY_TILE), KEY_TILE)], pb)

    win_lo = jnp.maximum(i - WINDOW // nq, 0) // (KEY_TILE // nq)
    o_w = _flash_stream(win_scores, win_pv, kt_self, win_lo, kt_self - win_lo, stream_sc, m_sc, l_sc, acc_sc)
    gates_t = gates_ref[...].T
    merged = []
    for hd in range(NSA_HEADS):
        cs = slice(hd * nq, (hd + 1) * nq)
        gate = lambda br: gates_t[3 * hd + br:3 * hd + br + 1]
        merged.append(gate(0) * o_c[:, cs] + gate(1) * o_s[:, cs] + gate(2) * o_w[:, cs])
    for j in range(NSA_HEADS // 2):
        pair = jnp.concatenate([merged[2 * j], merged[2 * j + 1]], axis=0).T
        o_ref[:, j * LANES:(j + 1) * LANES] = pair.astype(BF16)


def nsa_prompt_attn(q, gates, ckv, ks, vst, kw, vwt):
    nqb = SEQ // Q_BLOCK
    tok = lambda wd: pl.BlockSpec((Q_BLOCK, wd), lambda b, i: (b * nqb + i, 0))
    seq_rows = pl.BlockSpec((SEQ, KV_HALF), lambda b, i: (b, 0))
    seq_cols = pl.BlockSpec((KV_HALF, SEQ), lambda b, i: (0, b))
    n_cmp = N_CMP_PROMPT
    c_blk = jnp.concatenate([jnp.arange(0, n_cmp, 2), jnp.arange(1, n_cmp, 2)]).astype(jnp.int32)
    cols = NSA_HEADS * Q_BLOCK
    return pl.pallas_call(
        _nsa_prompt_body,
        grid=(BATCH, nqb),
        in_specs=[tok(NSA_Q_W), tok(LANES), pl.BlockSpec((1, n_cmp, NSA_KV_W), lambda b, i: (b, 0, 0)),
                  _full((n_cmp, LANES)), _full((n_cmp, LANES)), _full((n_cmp, LANES)),
                  seq_rows, seq_cols, seq_rows, seq_cols],
        out_specs=tok(NSA_Q_W),
        out_shape=jax.ShapeDtypeStruct((N_PROMPT, NSA_Q_W), BF16),
        scratch_shapes=[pltpu.VMEM((1, cols), F32), pltpu.VMEM((1, cols), F32), pltpu.VMEM((HD, cols), F32),
                        pltpu.VMEM((N_SLC_PROMPT, NSA_KV * Q_BLOCK), F32),
                        pltpu.VMEM((KEY_TILE, cols), F32), pltpu.VMEM((1, cols), F32),
                        pltpu.VMEM((KEY_TILE, cols), F32), pltpu.VMEM((1, cols), F32)],
        compiler_params=_params(2),
        name="nsa_prompt",
    )(q, gates, ckv, *_rope_tables((c_blk + 1) * CMP_BLOCK - 1), ks, vst, kw, vwt)


def _nsa_decode_body(pt_ref, qt_ref, gate_ref, ckv_ref, cos_ref, slo_ref, shi_ref, hsum_ref, *refs):
    page_refs = refs[:N_PAGES]
    win_ref, kvs_new_ref, kvw_new_ref, o_ref, win_out_ref, m_sc, l_sc, acc_sc = refs[N_PAGES:]
    qt = qt_ref[0]
    lane = lax.broadcasted_iota(jnp.int32, (1, LANES), 1)
    qpos = PAST_LEN + (lane & (DEC_SEQ - 1))
    group0 = lane < DEC_COLS_PER_GROUP
    ckv = ckv_ref[0]
    ck = _rope_lanes(ckv[:, :KV_HALF], cos_ref[...], slo_ref[...], shi_ref[...]).astype(BF16)
    c_mask = _even_first_cmp_end(N_CMP_DEC) <= qpos
    p = _softmax_cols(jnp.dot(ck, qt, preferred_element_type=F32), c_mask)
    o_c = _tn_dot(ckv[:, KV_HALF:].astype(BF16), p.astype(BF16))
    p_grp = jnp.dot(p, hsum_ref[...], preferred_element_type=F32, precision=lax.Precision.HIGHEST)
    p_slc = jnp.concatenate([p_grp[:N_CMP_DEC // 2] + p_grp[N_CMP_DEC // 2:],
                             jnp.zeros((N_SLC_DEC_PAD - N_CMP_DEC // 2, LANES), F32)], axis=0)
    sel = _select_cols(p_slc, qpos, SLC_TOPK)

    def new_rows(ref):
        row = ref[0]
        kv = jnp.concatenate([row[:, t * NSA_KV_W:(t + 1) * NSA_KV_W] for t in range(DEC_SEQ)], axis=0)
        return jnp.concatenate([kv, jnp.zeros((SUBLANES - DEC_SEQ, NSA_KV_W), F32)], axis=0)

    new_row = lax.broadcasted_iota(jnp.int32, (SUBLANES, 1), 0)
    new_pos = PAST_LEN + new_row
    new_valid = new_row < DEC_SEQ
    _flash_init(m_sc, l_sc, acc_sc)
    kt_old = jnp.concatenate([r[0, :KV_HALF, :] for r in page_refs], axis=1).astype(BF16)
    vt_old = jnp.concatenate([r[0, KV_HALF:, :] for r in page_refs], axis=1).astype(BF16)
    key_blk = lax.broadcasted_iota(jnp.int32, (PAST_LEN, N_SLC_DEC_PAD), 0) >> SLC_SHIFT
    blk_col = lax.broadcasted_iota(jnp.int32, (PAST_LEN, N_SLC_DEC_PAD), 1)
    chosen = jnp.dot(jnp.where(key_blk == blk_col, 1.0, 0.0).astype(BF16), sel.astype(BF16),
                     preferred_element_type=F32) > 0.5
    old_pos = lax.broadcasted_iota(jnp.int32, (PAST_LEN, 1), 0)
    _flash_cols(_tn_dot(kt_old, qt), chosen & (old_pos <= qpos),
                lambda pb: jnp.dot(vt_old, pb, preferred_element_type=F32), m_sc, l_sc, acc_sc)
    kv_new = new_rows(kvs_new_ref)
    sel_new = sel[(PAST_LEN >> SLC_SHIFT):(PAST_LEN >> SLC_SHIFT) + 1] > 0.5
    v_new = kv_new[:, KV_HALF:].astype(BF16)
    _flash_cols(jnp.dot(kv_new[:, :KV_HALF].astype(BF16), qt, preferred_element_type=F32),
                sel_new & new_valid & (new_pos <= qpos), lambda pb: _tn_dot(v_new, pb), m_sc, l_sc, acc_sc)
    o_s = _flash_out(l_sc, acc_sc)
    _flash_init(m_sc, l_sc, acc_sc)
    n_buf = win_ref.shape[2]
    win = win_ref[0]
    d = qpos - (PAST_LEN - n_buf + lax.broadcasted_iota(jnp.int32, (n_buf, 1), 0))
    vt_win = win[KV_HALF:].astype(BF16)
    _flash_cols(_tn_dot(win[:KV_HALF].astype(BF16), qt), (d >= 0) & (d < WINDOW),
                lambda pb: jnp.dot(vt_win, pb, preferred_element_type=F32), m_sc, l_sc, acc_sc)
    kw_new = new_rows(kvw_new_ref)
    d = qpos - new_pos
    vw_new = kw_new[:, KV_HALF:].astype(BF16)
    _flash_cols(jnp.dot(kw_new[:, :KV_HALF].astype(BF16), qt, preferred_element_type=F32),
                new_valid & (d >= 0) & (d < WINDOW), lambda pb: _tn_dot(vw_new, pb), m_sc, l_sc, acc_sc)
    o_w = _flash_out(l_sc, acc_sc)
    g = gate_ref[0]
    o = g[0:1] * o_c + g[1:2] * o_s + g[2:3] * o_w
    o_ref[0] = jnp.where(group0, o[:HD], o[HD:])
    key = lax.broadcasted_iota(jnp.int32, (SUBLANES, n_buf), 1)
    place = jnp.where((key == n_buf - DEC_SEQ + new_row) & new_valid, 1.0, 0.0)
    placed = lax.dot_general(kw_new, place, (((0,), (0,)), ((), ())), preferred_element_type=F32,
                             precision=lax.Precision.HIGHEST)
    keep = lax.broadcasted_iota(jnp.int32, (1, n_buf), 1) < n_buf - DEC_SEQ
    win_out_ref[0] = jnp.where(keep, pltpu.roll(win, n_buf - DEC_SEQ, 1), placed)


def nsa_decode_attn(qt, gate_rows, ckv, slc_pool, win_buf, kvs_new, kvw_new, page_table):
    n_buf = win_buf.shape[2]
    per_b = lambda *shape: pl.BlockSpec((1,) + shape, lambda b, pt: (b,) + (0,) * len(shape))
    const = lambda *shape: pl.BlockSpec(shape, lambda b, pt: (0,) * len(shape))
    page = lambda j: pl.BlockSpec((1, NSA_KV_W, PAGE_SIZE), lambda b, pt: (pt[b, j], 0, 0))
    c_blk = jnp.concatenate([jnp.arange(0, N_CMP_DEC, 2), jnp.arange(1, N_CMP_DEC, 2)]).astype(jnp.int32)
    col = jnp.arange(LANES)
    used = col < NSA_KV * DEC_COLS_PER_GROUP
    same = (col[:, None] // DEC_COLS_PER_GROUP == col[None, :] // DEC_COLS_PER_GROUP) & \
           (col[:, None] % DEC_SEQ == col[None, :] % DEC_SEQ) & used[:, None] & used[None, :]
    return pl.pallas_call(
        _nsa_decode_body,
        grid_spec=pltpu.PrefetchScalarGridSpec(
            num_scalar_prefetch=1,
            grid=(DEC_BATCH,),
            in_specs=[per_b(LANES, LANES), per_b(3, LANES), per_b(N_CMP_DEC, NSA_KV_W),
                      const(N_CMP_DEC, LANES), const(N_CMP_DEC, LANES), const(N_CMP_DEC, LANES),
                      const(LANES, LANES)] + [page(j) for j in range(N_PAGES)]
                     + [per_b(NSA_KV_W, n_buf), per_b(1, DEC_SEQ * NSA_KV_W), per_b(1, DEC_SEQ * NSA_KV_W)],
            out_specs=[per_b(HD, LANES), per_b(NSA_KV_W, n_buf)],
            scratch_shapes=[pltpu.VMEM((1, LANES), F32), pltpu.VMEM((1, LANES), F32),
                            pltpu.VMEM((LANES, LANES), F32)],
        ),
        out_shape=[jax.ShapeDtypeStruct((DEC_BATCH, HD, LANES), F32),
                   jax.ShapeDtypeStruct((DEC_BATCH, NSA_KV_W, n_buf), F32)],
        compiler_params=_params(1),
        name="nsa_decode",
    )(page_table, qt, gate_rows, ckv, *_rope_tables((c_blk + 1) * CMP_BLOCK - 1), same.astype(F32),
      *([slc_pool] * N_PAGES), win_buf, kvs_new, kvw_new)


GLA_PAIRS = GLA_HEADS // 2
GLA_ROWS = 128
GLA_SUB = 16
GLA_STEP_SEQS = 8


def _gla_rows(q, k, la, v, sgg, gnorm, st_ref, sub):
    R = q.shape[0]
    row = lax.broadcasted_iota(jnp.int32, (R, GLA_K_W), 0)
    rin = row % sub
    cum = la
    d = 1
    while d < sub:
        cum = cum + jnp.where(rin >= d, pltpu.roll(cum, d, 0), 0.0)
        d *= 2
    lane = lax.broadcasted_iota(jnp.int32, (sub, LANES), 1)
    lo = lane < GLA_DK
    rsub = lax.broadcasted_iota(jnp.int32, (sub, LANES), 0)
    out_rows = []
    for c in range(R // sub):
        rs = slice(c * sub, (c + 1) * sub)
        cum_c = cum[rs]
        last = cum_c[sub - 1:sub]
        qe = q[rs] * jnp.exp(cum_c)
        kdec = k[rs] * jnp.exp(last - cum_c)
        v_c = v[rs]
        heads = []
        for pr in range(GLA_PAIRS):
            ls = slice(pr * LANES, (pr + 1) * LANES)
            st = st_ref[pr]
            st_b = st.astype(BF16)
            qe_p, kd_p, q_p, k_p, cum_p = qe[:, ls], kdec[:, ls], q[rs, ls], k[rs, ls], cum_c[:, ls]
            v_pair = [v_c[:, (2 * pr + hh) * GLA_DV:(2 * pr + hh + 1) * GLA_DV] for hh in range(2)]
            upd = jnp.zeros((GLA_DV, LANES), F32)
            o_pair = []
            for hh in range(2):
                keep = lo if hh == 0 else jnp.logical_not(lo)
                o_pair.append(_nt_dot(jnp.where(keep, qe_p, 0.0).astype(BF16), st_b))
                upd = upd + _tn_dot(v_pair[hh].astype(BF16), jnp.where(keep, kd_p, 0.0).astype(BF16))
            for j in range(sub):
                dj = jnp.where(rsub >= j, jnp.exp(cum_p - cum_p[j:j + 1]), 0.0)
                w = q_p * k_p[j:j + 1] * dj
                a_lo = jnp.sum(jnp.where(lo, w, 0.0), axis=-1, keepdims=True)
                a_hi = jnp.sum(jnp.where(lo, 0.0, w), axis=-1, keepdims=True)
                o_pair[0] = o_pair[0] + a_lo * v_pair[0][j:j + 1]
                o_pair[1] = o_pair[1] + a_hi * v_pair[1][j:j + 1]
            st_ref[pr] = st * jnp.exp(last[:, ls]) + upd
            heads += o_pair
        out_rows.append(jnp.concatenate([_rms(x, gnorm) for x in heads], axis=1))
    return jnp.concatenate(out_rows, axis=0) * sgg


def _gla_seq_body(q_ref, k_ref, la_ref, v_ref, sgg_ref, gn_ref, o_ref, st_out_ref, st_sc):
    @pl.when(pl.program_id(1) == 0)
    def _():
        st_sc[...] = jnp.zeros(st_sc.shape, F32)

    o = _gla_rows(q_ref[...], k_ref[...], la_ref[...], v_ref[...].astype(F32), sgg_ref[...], gn_ref[...],
                  st_sc, GLA_SUB)
    o_ref[...] = o.astype(BF16)
    st_out_ref[0] = st_sc[...]


def gla_seq(q, k, la, v, sgg, gnorm):
    nt = SEQ // GLA_ROWS
    rows = lambda wd: pl.BlockSpec((GLA_ROWS, wd), lambda b, t: (b * nt + t, 0))
    return pl.pallas_call(
        _gla_seq_body,
        grid=(BATCH, nt),
        in_specs=[rows(GLA_K_W), rows(GLA_K_W), rows(GLA_K_W), rows(GLA_V_W), rows(GLA_V_W),
                  _full((1, GLA_DV))],
        out_specs=[rows(GLA_V_W), pl.BlockSpec((1, GLA_PAIRS, GLA_DV, LANES), lambda b, t: (b, 0, 0, 0))],
        out_shape=[jax.ShapeDtypeStruct((N_PROMPT, GLA_V_W), BF16),
                   jax.ShapeDtypeStruct((BATCH, GLA_PAIRS, GLA_DV, LANES), F32)],
        scratch_shapes=[pltpu.VMEM((GLA_PAIRS, GLA_DV, LANES), F32)],
        compiler_params=_params(2),
        name="gla_seq",
    )(q, k, la, v, sgg, gnorm.reshape(1, GLA_DV))


def _gla_step_body(q_ref, k_ref, la_ref, v_ref, sgg_ref, gn_ref, st_in_ref, o_ref, st_out_ref):
    st_out_ref[...] = st_in_ref[...]
    q, k, la, v, sgg = q_ref[...], k_ref[...], la_ref[...], v_ref[...].astype(F32), sgg_ref[...]
    for j in range(GLA_STEP_SEQS):
        rs = slice(j * DEC_SEQ, (j + 1) * DEC_SEQ)
        o = _gla_rows(q[rs], k[rs], la[rs], v[rs], sgg[rs], gn_ref[...], st_out_ref.at[j], DEC_SEQ)
        o_ref[rs, :] = o.astype(BF16)


def gla_step(q, k, la, v, sgg, gnorm, st_in, row0):
    rows_per = GLA_STEP_SEQS * DEC_SEQ
    blk0 = row0 // rows_per
    rows = lambda wd: pl.BlockSpec((rows_per, wd), lambda i: (blk0 + i, 0))
    st_spec = pl.BlockSpec((GLA_STEP_SEQS, GLA_PAIRS, GLA_DV, LANES), lambda i: (i, 0, 0, 0))
    return pl.pallas_call(
        _gla_step_body,
        grid=(DEC_BATCH // GLA_STEP_SEQS,),
        in_specs=[rows(GLA_K_W), rows(GLA_K_W), rows(GLA_K_W), rows(GLA_V_W), rows(GLA_V_W),
                  _full((1, GLA_DV)), st_spec],
        out_specs=[pl.BlockSpec((rows_per, GLA_V_W), lambda i: (i, 0)), st_spec],
        out_shape=[jax.ShapeDtypeStruct((N_SAMPLE, GLA_V_W), BF16),
                   jax.ShapeDtypeStruct((DEC_BATCH, GLA_PAIRS, GLA_DV, LANES), F32)],
        compiler_params=_params(1),
        name="gla_step",
    )(q, k, la, v, sgg, gnorm.reshape(1, GLA_DV), st_in)


def _gla_state_to_pairs(s):
    B = s.shape[0]
    return s.reshape(B, GLA_PAIRS, 2, GLA_DK, GLA_DV).transpose(0, 1, 4, 2, 3).reshape(B, GLA_PAIRS, GLA_DV, LANES)


def _gla_state_from_pairs(st):
    B = st.shape[0]
    return st.reshape(B, GLA_PAIRS, GLA_DV, 2, GLA_DK).transpose(0, 1, 3, 4, 2).reshape(B, GLA_HEADS, GLA_DK, GLA_DV)


LRU_TIME_TILE = 256


def _lru_gates(xc, wa_ref, ba, wx_ref, bx, lam):
    xcb = xc.astype(BF16)
    r_parts, i_parts = [], []
    for n in range(LRU_BLOCKS):
        xs = xcb[:, n * LRU_BW:(n + 1) * LRU_BW]
        r_parts.append(jnp.dot(xs, wa_ref[n], preferred_element_type=F32))
        i_parts.append(jnp.dot(xs, wx_ref[n], preferred_element_type=F32))
    r = jax.nn.sigmoid(jnp.concatenate(r_parts, axis=-1) + ba)
    i = jax.nn.sigmoid(jnp.concatenate(i_parts, axis=-1) + bx)
    log_a = -LRU_C * r * _softplus(-lam)
    a = jnp.exp(log_a)
    u = jnp.sqrt(1.0 - a * a) * (i * xc)
    return a, u


def _lru_seq_body(u_ref, cw_ref, cb_ref, wa_ref, ba_ref, wx_ref, bx_ref, lam_ref, y_ref, hT_ref, xp_sc, h_sc):
    tt = LRU_TIME_TILE

    @pl.when(pl.program_id(1) == 0)
    def _():
        xp_sc[0:SUBLANES, :] = jnp.zeros((SUBLANES, D_RNN), F32)
        h_sc[...] = jnp.zeros((1, D_RNN), F32)

    xp_sc[SUBLANES:SUBLANES + tt, :] = u_ref[:, D_RNN:]
    xc = cb_ref[...]
    for w in range(CONV_W):
        off = SUBLANES - (CONV_W - 1) + w
        xc = xc + cw_ref[w:w + 1, :] * xp_sc[off:off + tt, :]
    a, u = _lru_gates(xc, wa_ref, ba_ref[...], wx_ref, bx_ref[...], lam_ref[...])
    row = lax.broadcasted_iota(jnp.int32, (tt, D_RNN), 0) % SUBLANES
    d = 1
    while d < SUBLANES:
        keep = row >= d
        a_prev = jnp.where(keep, pltpu.roll(a, d, 0), 1.0)
        u_prev = jnp.where(keep, pltpu.roll(u, d, 0), 0.0)
        u = a * u_prev + u
        a = a * a_prev
        d *= 2
    carry = h_sc[...]
    hs = []
    for grp in range(tt // SUBLANES):
        rs = slice(grp * SUBLANES, (grp + 1) * SUBLANES)
        hs.append(a[rs] * carry + u[rs])
        carry = hs[-1][SUBLANES - 1:SUBLANES, :]
    h_sc[...] = carry
    hT_ref[0] = carry
    y_ref[...] = (_gelu_tanh(u_ref[:, :D_RNN]) * jnp.concatenate(hs, axis=0)).astype(BF16)
    xp_sc[0:SUBLANES, :] = xp_sc[tt:tt + SUBLANES, :]


def _lru_weight_args(cw, cb, wa, ba, wx, bx, lam):
    row = lambda v: v.reshape(1, D_RNN)
    return (cw, row(cb), wa.astype(BF16), row(ba), wx.astype(BF16), row(bx), row(lam))


_LRU_WEIGHT_SPECS = [_full((CONV_W, D_RNN)), _full((1, D_RNN)), _full((LRU_BLOCKS, LRU_BW, LRU_BW)),
                     _full((1, D_RNN)), _full((LRU_BLOCKS, LRU_BW, LRU_BW)), _full((1, D_RNN)),
                     _full((1, D_RNN))]


def lru_seq(u, cw, cb, wa, ba, wx, bx, lam):
    tt = LRU_TIME_TILE
    nt = SEQ // tt
    y, hT = pl.pallas_call(
        _lru_seq_body,
        grid=(BATCH, nt),
        in_specs=[pl.BlockSpec((tt, 2 * D_RNN), lambda b, t: (b * nt + t, 0))] + _LRU_WEIGHT_SPECS,
        out_specs=[pl.BlockSpec((tt, D_RNN), lambda b, t: (b * nt + t, 0)),
                   pl.BlockSpec((1, 1, D_RNN), lambda b, t: (b, 0, 0))],
        out_shape=[jax.ShapeDtypeStruct((N_PROMPT, D_RNN), BF16), jax.ShapeDtypeStruct((BATCH, 1, D_RNN), F32)],
        scratch_shapes=[pltpu.VMEM((tt + SUBLANES, D_RNN), F32), pltpu.VMEM((1, D_RNN), F32)],
        compiler_params=_params(2),
        name="lru_seq",
    )(u, *_lru_weight_args(cw, cb, wa, ba, wx, bx, lam))
    return y, hT.reshape(BATCH, D_RNN)


def _lru_step_body(u_ref, cs_ref, h0_ref, cw_ref, cb_ref, wa_ref, ba_ref, wx_ref, bx_ref, lam_ref, y_ref, hT_ref):
    n_t = u_ref.shape[0]
    hist = [cs_ref[:, w, :] for w in range(CONV_W - 1)] + [u_ref[t, :, D_RNN:] for t in range(n_t)]
    h = h0_ref[...]
    for t in range(n_t):
        xc = cb_ref[...]
        for w in range(CONV_W):
            xc = xc + cw_ref[w:w + 1, :] * hist[t + w]
        a, u = _lru_gates(xc, wa_ref, ba_ref[...], wx_ref, bx_ref[...], lam_ref[...])
        h = a * h + u
        y_ref[t] = (_gelu_tanh(u_ref[t, :, :D_RNN]) * h).astype(BF16)
    hT_ref[...] = h


def lru_step(u, conv_state, h0, cw, cb, wa, ba, wx, bx, lam):
    T, B, _ = u.shape
    return pl.pallas_call(
        _lru_step_body,
        out_shape=[jax.ShapeDtypeStruct((T, B, D_RNN), BF16), jax.ShapeDtypeStruct((B, D_RNN), F32)],
        compiler_params=pltpu.CompilerParams(vmem_limit_bytes=VMEM_LIMIT_BYTES),
        name="lru_step",
    )(u, conv_state, h0, *_lru_weight_args(cw, cb, wa, ba, wx, bx, lam))


ROUTE_E1, ROUTE_E2, ROUTE_G1, ROUTE_G2, ROUTE_R1, ROUTE_R2 = range(6)
EXPERT_LANE0 = N_GROUPS


def _lane_pick(val_by_lane):
    rows = next(iter(val_by_lane.values())).shape[0]
    lane = lax.broadcasted_iota(jnp.int32, (rows, LANES), 1)
    out = jnp.zeros((rows, LANES), F32)
    for l, v in val_by_lane.items():
        out = jnp.where(lane == l, v, out)
    return out


def _route_rows(logits, tri_ref, carry_ref):
    rows = logits.shape[0]
    lane = lax.broadcasted_iota(jnp.int32, (rows, LANES), 1)
    neg = -jnp.inf
    gl = jnp.where(lane < N_GROUPS, logits, neg)
    gmax = jnp.max(gl, axis=-1, keepdims=True)
    gtop = jnp.min(jnp.where(gl == gmax, lane, LANES), axis=-1, keepdims=True)
    gsum = jnp.sum(jnp.where(lane < N_GROUPS, jnp.exp(logits - gmax), 0.0), axis=-1, keepdims=True)
    g_w = 1.0 / gsum
    lo = EXPERT_LANE0 + EXP_PER_GROUP * gtop
    el = jnp.where((lane >= lo) & (lane < lo + EXP_PER_GROUP), logits, neg)
    v1 = jnp.max(el, axis=-1, keepdims=True)
    i1 = jnp.min(jnp.where(el == v1, lane, LANES), axis=-1, keepdims=True)
    el2 = jnp.where(lane == i1, neg, el)
    v2 = jnp.max(el2, axis=-1, keepdims=True)
    i2 = jnp.min(jnp.where(el2 == v2, lane, LANES), axis=-1, keepdims=True)
    p2 = jnp.exp(v2 - v1)
    den = 1.0 + p2
    gate1 = (1.0 / den) * g_w
    gate2 = (p2 / den) * g_w
    hit1 = lane == i1
    hit2 = lane == i2
    onehot = jnp.where(hit1 | hit2, 1.0, 0.0)
    before = jnp.dot(tri_ref[...], onehot.astype(BF16), preferred_element_type=F32) + carry_ref[...]
    rank1 = jnp.sum(jnp.where(hit1, before, 0.0), axis=-1, keepdims=True)
    rank2 = jnp.sum(jnp.where(hit2, before, 0.0), axis=-1, keepdims=True)
    carry_ref[...] = carry_ref[...] + jnp.sum(onehot, axis=0, keepdims=True)
    return _lane_pick({ROUTE_E1: (i1 - EXPERT_LANE0).astype(F32), ROUTE_E2: (i2 - EXPERT_LANE0).astype(F32),
                       ROUTE_G1: gate1, ROUTE_G2: gate2, ROUTE_R1: rank1, ROUTE_R2: rank2})


def _pack_bf16_halves(x):
    w = x.shape[1] // 2
    bits = lambda v: pltpu.bitcast(v.astype(F32), jnp.uint32)
    return pltpu.bitcast((bits(x[:, :w]) >> 16) | bits(x[:, w:]), F32)


def _unpack_bf16_halves(words):
    p = pltpu.bitcast(words, jnp.uint32)
    lo = pltpu.bitcast(p << 16, F32).astype(BF16)
    hi = pltpu.bitcast(p & jnp.uint32(0xFFFF0000), F32).astype(BF16)
    return jnp.concatenate([lo, hi], axis=1)


def _post_mixer_body(n_mix, h_ref, *refs):
    m_refs, wo_refs = refs[:n_mix], refs[n_mix:2 * n_mix]
    g_ref, wr_ref, br_ref, tri_ref, h1_ref, hn_ref, route_ref, cnt_ref, carry_sc = refs[2 * n_mix:]

    @pl.when(pl.program_id(0) == 0)
    def _():
        carry_sc[...] = jnp.zeros((1, LANES), F32)

    mix = jnp.dot(m_refs[0][...], wo_refs[0][...], preferred_element_type=F32)
    for m_ref, wo_ref in zip(m_refs[1:], wo_refs[1:]):
        mix = mix + jnp.dot(m_ref[...], wo_ref[...], preferred_element_type=F32)
    h1 = h_ref[...] + mix
    h1_ref[...] = h1
    hn = _rms(h1, g_ref[...]).astype(BF16)
    hn_ref[...] = _pack_bf16_halves(hn)
    logits = jnp.dot(hn, wr_ref[...], preferred_element_type=F32) + br_ref[...]
    route_ref[...] = _route_rows(logits, tri_ref, carry_sc)
    cnt_ref[...] = carry_sc[...]


def post_mixer(h, mix_ins, w_out, g_ffn, w_rg, b_rg, w_re, b_re):
    n = h.shape[0]
    ks = [m.shape[1] for m in mix_ins]
    offs = np.cumsum([0] + ks)
    w_parts = [w_out[offs[j]:offs[j + 1]].astype(BF16) for j in range(len(ks))]
    pad = LANES - N_GROUPS - N_EXPERTS
    wr = jnp.concatenate([w_rg, w_re, jnp.zeros((D_MODEL, pad), F32)], axis=1).astype(BF16)
    br = jnp.concatenate([b_rg, b_re, jnp.zeros((pad,), F32)]).reshape(1, LANES)
    tri = jnp.tril(jnp.ones((ROW_TILE, ROW_TILE), BF16), -1)
    return pl.pallas_call(
        functools.partial(_post_mixer_body, len(ks)),
        grid=(n // ROW_TILE,),
        in_specs=[_rows(D_MODEL)] + [_rows(k) for k in ks] + [_full((k, D_MODEL)) for k in ks]
                 + [_full((1, D_MODEL)), _full((D_MODEL, LANES)), _full((1, LANES)), _full((ROW_TILE, ROW_TILE))],
        out_specs=[_rows(D_MODEL), _rows(D_MODEL // 2), _rows(LANES), _full((1, LANES))],
        out_shape=[jax.ShapeDtypeStruct((n, D_MODEL), F32), jax.ShapeDtypeStruct((n, D_MODEL // 2), F32),
                   jax.ShapeDtypeStruct((n, LANES), F32), jax.ShapeDtypeStruct((1, LANES), F32)],
        scratch_shapes=[pltpu.VMEM((1, LANES), F32)],
        compiler_params=_params(1),
        name="post_mixer",
    )(h, *mix_ins, *w_parts, g_ffn.reshape(1, D_MODEL), wr, br, tri)


MOE_ROWS = 512


def _ffn_body(be_ref, nb_ref, xa_ref, xb_ref, w1_ref, w3_ref, w2_ref, y_ref, w1_sc, w3_sc, w2_sc, x_sc):
    i = pl.program_id(0)
    half = pl.num_programs(0) // 2
    new_expert = jnp.logical_or(i == 0, be_ref[i] != be_ref[jnp.maximum(i - 1, 0)])

    @pl.when(jnp.logical_and(new_expert, i < nb_ref[0]))
    def _():
        w1_sc[...] = w1_ref[0, 0].astype(BF16)
        w3_sc[...] = w3_ref[0, 0].astype(BF16)
        w2_sc[...] = w2_ref[0, 0].astype(BF16)

    @pl.when(jnp.logical_and(i < nb_ref[0], i < half))
    def _():
        x_sc[...] = xa_ref[...]

    @pl.when(jnp.logical_and(i < nb_ref[0], i >= half))
    def _():
        x_sc[...] = xb_ref[...]

    @pl.when(i < nb_ref[0])
    def _():
        x = _unpack_bf16_halves(x_sc[...])
        a = jnp.dot(x, w1_sc[...], preferred_element_type=F32)
        b = jnp.dot(x, w3_sc[...], preferred_element_type=F32)
        hdn = (a * jax.nn.sigmoid(a) * b).astype(BF16)
        y_ref[...] = jnp.dot(hdn, w2_sc[...], preferred_element_type=F32)

    @pl.when(i >= nb_ref[0])
    def _():
        y_ref[...] = jnp.zeros(y_ref.shape, F32)


def expert_ffn(xa, xb, blk_exp, n_active, w1, w3, w2, layer):
    hb = xa.shape[0] // MOE_ROWS
    nb = 2 * hb
    n_slots = nb * MOE_ROWS
    wmap = lambda i, be, na: (layer, be[i], 0, 0)
    live = lambda i, na: jnp.minimum(i, na[0] - 1)
    return pl.pallas_call(
        _ffn_body,
        grid_spec=pltpu.PrefetchScalarGridSpec(
            num_scalar_prefetch=2,
            grid=(nb,),
            in_specs=[pl.BlockSpec((MOE_ROWS, D_MODEL // 2), lambda i, be, na: (jnp.minimum(live(i, na), hb - 1), 0)),
                      pl.BlockSpec((MOE_ROWS, D_MODEL // 2), lambda i, be, na: (jnp.maximum(live(i, na) - hb, 0), 0)),
                      pl.BlockSpec((1, 1, D_MODEL, E_HID), wmap),
                      pl.BlockSpec((1, 1, D_MODEL, E_HID), wmap),
                      pl.BlockSpec((1, 1, E_HID, D_MODEL), wmap)],
            out_specs=pl.BlockSpec((MOE_ROWS, D_MODEL), lambda i, be, na: (i, 0)),
            scratch_shapes=[pltpu.VMEM((D_MODEL, E_HID), BF16), pltpu.VMEM((D_MODEL, E_HID), BF16),
                            pltpu.VMEM((E_HID, D_MODEL), BF16), pltpu.VMEM((MOE_ROWS, D_MODEL // 2), F32)],
        ),
        out_shape=jax.ShapeDtypeStruct((n_slots, D_MODEL), F32),
        compiler_params=_params(1),
        name="expert_ffn",
    )(blk_exp, n_active, xa, xb, w1, w3, w2)


def moe_dispatch(route, counts_row, n):
    e = route[:, ROUTE_E1:ROUTE_E2 + 1].astype(jnp.int32)
    rank = route[:, ROUTE_R1:ROUTE_R2 + 1].astype(jnp.int32)
    counts = counts_row[0, EXPERT_LANE0:EXPERT_LANE0 + N_EXPERTS].astype(jnp.int32)
    padded = ((counts + MOE_ROWS - 1) // MOE_ROWS) * MOE_ROWS
    pad_end = jnp.cumsum(padded)
    pad_start = pad_end - padded
    dest = pad_start[e] + rank
    nb = -(-(n * TOP_K_IN_GROUP) // MOE_ROWS) + N_EXPERTS
    n_slots = nb * MOE_ROWS
    tok = jnp.broadcast_to(jnp.arange(n, dtype=jnp.int32)[:, None], (n, TOP_K_IN_GROUP))
    slot_tok = jnp.zeros((n_slots,), jnp.int32).at[dest.reshape(-1)].set(
        tok.reshape(-1), unique_indices=True, mode='promise_in_bounds')
    blk_start = jnp.arange(nb, dtype=jnp.int32) * MOE_ROWS
    blk_exp = jnp.sum((pad_end[None, :] <= blk_start[:, None]).astype(jnp.int32), axis=1)
    blk_exp = jnp.minimum(blk_exp, N_EXPERTS - 1)
    n_active = (pad_end[-1] // MOE_ROWS).astype(jnp.int32).reshape(1)
    return slot_tok, dest, blk_exp, n_active


def moe_experts(hn, route, counts_row, w1, w3, w2, layer):
    n = hn.shape[0]
    slot_tok, dest, blk_exp, n_active = moe_dispatch(route, counts_row, n)
    half = slot_tok.shape[0] // 2
    xa = hn.at[slot_tok[:half]].get(mode='promise_in_bounds')
    xb = hn.at[slot_tok[half:]].get(mode='promise_in_bounds')
    ys = expert_ffn(xa, xb, blk_exp, n_active, w1, w3, w2, layer)
    return [ys.at[dest[:, j]].get(mode='promise_in_bounds') for j in range(TOP_K_IN_GROUP)]


def _combine(h_ref, y1_ref, y2_ref, route_ref):
    lane = lax.broadcasted_iota(jnp.int32, (ROW_TILE, LANES), 1)
    r = route_ref[...]
    g1 = jnp.sum(jnp.where(lane == ROUTE_G1, r, 0.0), axis=-1, keepdims=True)
    g2 = jnp.sum(jnp.where(lane == ROUTE_G2, r, 0.0), axis=-1, keepdims=True)
    return h_ref[...] + (y1_ref[...] * g1 + y2_ref[...] * g2)


def _combine_proj_body(h_ref, y1_ref, y2_ref, route_ref, g_ref, w_ref, h2_ref, u_ref):
    h2 = _combine(h_ref, y1_ref, y2_ref, route_ref)
    h2_ref[...] = h2
    u_ref[...] = jnp.dot(_rms(h2, g_ref[...]).astype(BF16), w_ref[...], preferred_element_type=F32)


def _combine_norm_body(h_ref, y1_ref, y2_ref, route_ref, g_ref, y_ref):
    y_ref[...] = _rms(_combine(h_ref, y1_ref, y2_ref, route_ref), g_ref[...])


def combine_proj(h, ys, route, g, w):
    n = h.shape[0]
    nn = w.shape[1]
    return pl.pallas_call(
        _combine_proj_body,
        grid=(n // ROW_TILE,),
        in_specs=[_rows(D_MODEL), _rows(D_MODEL), _rows(D_MODEL), _rows(LANES), _full((1, D_MODEL)),
                  _full((D_MODEL, nn))],
        out_specs=[_rows(D_MODEL), _rows(nn)],
        out_shape=[jax.ShapeDtypeStruct((n, D_MODEL), F32), jax.ShapeDtypeStruct((n, nn), F32)],
        compiler_params=_params(1),
        name="combine_proj",
    )(h, *ys, route, g.reshape(1, D_MODEL), w.astype(BF16))


def combine_norm(h, ys, route, g):
    n = h.shape[0]
    return pl.pallas_call(
        _combine_norm_body,
        grid=(n // ROW_TILE,),
        in_specs=[_rows(D_MODEL), _rows(D_MODEL), _rows(D_MODEL), _rows(LANES), _full((1, D_MODEL))],
        out_specs=_rows(D_MODEL),
        out_shape=jax.ShapeDtypeStruct((n, D_MODEL), F32),
        compiler_params=_params(1),
        name="combine_norm",
    )(h, *ys, route, g.reshape(1, D_MODEL))


def _even_first(x, axis):
    n = x.shape[axis]
    idx = jnp.concatenate([jnp.arange(0, n, 2), jnp.arange(1, n, 2)])
    return jnp.take(x, idx, axis=axis)


def _decode_query_cols(q_s, gates_s):
    B, T, G, R = DEC_BATCH, DEC_SEQ, NSA_KV, NSA_REP
    qg = q_s.reshape(B, T, G, R, HD).transpose(0, 2, 4, 3, 1).reshape(B, G, HD, R * T)
    qt = jnp.zeros((B, G, HD, G, R * T), BF16)
    for g in range(G):
        qt = qt.at[:, g, :, g, :].set(qg[:, g])
    qt = jnp.pad(qt.reshape(B, G * HD, G * R * T), ((0, 0), (0, 0), (0, LANES - G * R * T)))
    gr = gates_s[:, :3 * NSA_HEADS].reshape(B, T, NSA_HEADS, 3).transpose(0, 3, 2, 1).reshape(B, 3, NSA_HEADS * T)
    return qt, jnp.pad(gr, ((0, 0), (0, 0), (0, LANES - NSA_HEADS * T)))


def _feature_major(cache):
    lead, rows = cache.shape[:2]
    return jnp.transpose(cache, (0, 2, 3, 4, 1)).reshape(lead, NSA_KV_W, rows)


def _kv_rows_from_feature_major(xt, lead):
    x = xt.reshape(2, NSA_KV, HD, *lead)
    n = len(lead)
    return jnp.transpose(x, tuple(range(3, 3 + n)) + (0, 1, 2))


def mixer_a(h, p, past):
    pos = jnp.concatenate([jnp.tile(jnp.arange(SEQ, dtype=jnp.int32), BATCH),
                           PAST_LEN + jnp.tile(jnp.arange(DEC_SEQ, dtype=jnp.int32), DEC_BATCH)])
    (q, kvc, kvs, kvw, kvct, kvst, kvwt, ks, vst, kw, vwt, gates, gq, gk, gv, la, sgg) = inproj_a(
        h, p['norm_mix'][0], p['a_w_in'][0], p['a_gla_wa2'][0], p['a_gla_ba'][0], pos)
    pe, wc, gnorm = p['a_cmp_pe'][0], p['a_cmp_w'][0], p['a_gla_norm'][0]
    P = N_PROMPT
    ckv = cmp_blocks_rows(kvc, P, pe, wc)
    ckv = _even_first(ckv.reshape(BATCH, N_CMP_PROMPT, NSA_KV_W), 1)
    o_nsa_p = nsa_prompt_attn(q, gates, ckv, ks, vst, kw, vwt)
    o_gla_p, st_p = gla_seq(gq, gk, la, gv, sgg, gnorm)
    win_p = jnp.concatenate([kvwt[:, (b + 1) * SEQ - WINDOW:(b + 1) * SEQ] for b in range(BATCH)], axis=1)
    new_p = (_kv_rows_from_feature_major(kvct[:, :P], (BATCH, SEQ)),
             _kv_rows_from_feature_major(kvst[:, :P], (BATCH, SEQ)),
             _kv_rows_from_feature_major(win_p, (BATCH, WINDOW)), _gla_state_from_pairs(st_p))
    o_gla_s, st_s = gla_step(gq, gk, la, gv, sgg, gnorm, _gla_state_to_pairs(past['state_gla'][0]), P)
    n_pool = past['cache_cmp_kv'].shape[1]
    ckv_pool = cmp_blocks_pages(_feature_major(past['cache_cmp_kv'][0]), pe, wc)
    ckv_seq = ckv_pool.reshape(n_pool, PAGE_SIZE // CMP_BLOCK, NSA_KV_W)[past['page_table']]
    ckv_seq = _even_first(ckv_seq.reshape(DEC_BATCH, N_CMP_DEC, NSA_KV_W), 1)
    qt, gate_rows = _decode_query_cols(q[P:], gates[P:])
    per_seq = lambda x: x[P:].reshape(DEC_BATCH, 1, DEC_SEQ * NSA_KV_W)
    o_t, win_new = nsa_decode_attn(qt, gate_rows, ckv_seq, _feature_major(past['cache_slc_kv'][0]),
                                   _feature_major(past['cache_win_kv'][0]), per_seq(kvs), per_seq(kvw),
                                   past['page_table'])
    o_nsa_s = o_t[:, :, :NSA_HEADS * DEC_SEQ].reshape(DEC_BATCH, HD, NSA_HEADS, DEC_SEQ)
    o_nsa_s = o_nsa_s.transpose(0, 3, 2, 1).reshape(N_SAMPLE, NSA_Q_W).astype(BF16)
    n_buf = win_new.shape[2]
    win_s = jnp.transpose(win_new.reshape(DEC_BATCH, 2, NSA_KV, HD, n_buf), (0, 4, 1, 2, 3))
    new_s = (_kv_rows_from_feature_major(kvct[:, P:], (DEC_BATCH, DEC_SEQ)),
             _kv_rows_from_feature_major(kvst[:, P:], (DEC_BATCH, DEC_SEQ)), win_s, _gla_state_from_pairs(st_s))
    o_nsa = jnp.concatenate([o_nsa_p, o_nsa_s], axis=0)
    o_gla = jnp.concatenate([o_gla_p, o_gla_s], axis=0)
    return o_nsa, o_gla, new_p, new_s


def run_trunk(x_prompt, x_sample, p, past):
    h = jnp.concatenate([x_prompt.reshape(N_PROMPT, D_MODEL), x_sample.reshape(N_SAMPLE, D_MODEL)], axis=0)
    o_nsa, o_gla, new_p, new_s = mixer_a(h, p, past)
    h, hn, route, counts = post_mixer(h, [o_nsa, o_gla], p['a_w_out'][0], p['norm_ffn'][0], p['m_w_rg'][0],
                                      p['m_b_rg'][0], p['m_w_re'][0], p['m_b_re'][0])
    ys = moe_experts(hn, route, counts, p['m_w1'], p['m_w3'], p['m_w2'], 0)
    h, u = combine_proj(h, ys, route, p['norm_mix'][1], p['c_w_in'][0])
    lru_w = (p['c_conv_w'][0], p['c_conv_b'][0], p['c_w_a'][0], p['c_b_a'][0], p['c_w_x'][0], p['c_b_x'][0],
             p['c_lam'][0])
    us = u[N_PROMPT:].reshape(DEC_BATCH, DEC_SEQ, 2 * D_RNN)
    y_p, lru_p = lru_seq(u, *lru_w)
    y_s, lru_s = lru_step(jnp.swapaxes(us, 0, 1), past['state_conv'][0], past['state_lru'][0], *lru_w)
    conv_p = jnp.stack([u[(b + 1) * SEQ - (CONV_W - 1):(b + 1) * SEQ, D_RNN:] for b in range(BATCH)])
    conv_s = us[:, DEC_SEQ - (CONV_W - 1):, D_RNN:]
    mix_in = jnp.concatenate([y_p, jnp.swapaxes(y_s, 0, 1).reshape(N_SAMPLE, D_RNN)], axis=0)
    h, hn, route, counts = post_mixer(h, [mix_in], p['c_w_out'][0], p['norm_ffn'][1], p['m_w_rg'][1],
                                      p['m_b_rg'][1], p['m_w_re'][1], p['m_b_re'][1])
    ys = moe_experts(hn, route, counts, p['m_w1'], p['m_w3'], p['m_w2'], 1)
    y = combine_norm(h, ys, route, p['norm_final'])
    y_prompt = y[:N_PROMPT].reshape(BATCH, SEQ, D_MODEL)
    y_sample = y[N_PROMPT:].reshape(DEC_BATCH, DEC_SEQ, D_MODEL)
    return (y_prompt, y_sample), new_p + (lru_p, conv_p), new_s + (lru_s, conv_s)


def kernel(x_prompt, x_sample, cache_cmp_kv, cache_slc_kv, cache_win_kv, state_gla, state_lru, state_conv,
           page_table, norm_mix, norm_ffn, norm_final, a_w_in, a_cmp_pe, a_cmp_w, a_gla_wa2, a_gla_ba,
           a_gla_norm, a_w_out, c_w_in, c_conv_w, c_conv_b, c_w_a, c_b_a, c_w_x, c_b_x, c_lam, c_w_out,
           m_w_rg, m_b_rg, m_w_re, m_b_re, m_w1, m_w3, m_w2):
    p = {'norm_mix': norm_mix, 'norm_ffn': norm_ffn, 'norm_final': norm_final,
         'a_w_in': a_w_in, 'a_cmp_pe': a_cmp_pe, 'a_cmp_w': a_cmp_w, 'a_gla_wa2': a_gla_wa2,
         'a_gla_ba': a_gla_ba, 'a_gla_norm': a_gla_norm, 'a_w_out': a_w_out,
         'c_w_in': c_w_in, 'c_conv_w': c_conv_w, 'c_conv_b': c_conv_b, 'c_w_a': c_w_a, 'c_b_a': c_b_a,
         'c_w_x': c_w_x, 'c_b_x': c_b_x, 'c_lam': c_lam, 'c_w_out': c_w_out,
         'm_w_rg': m_w_rg, 'm_b_rg': m_b_rg, 'm_w_re': m_w_re, 'm_b_re': m_b_re,
         'm_w1': m_w1, 'm_w3': m_w3, 'm_w2': m_w2}
    past = {'cache_cmp_kv': cache_cmp_kv, 'cache_slc_kv': cache_slc_kv, 'cache_win_kv': cache_win_kv,
            'state_gla': state_gla, 'state_lru': state_lru, 'state_conv': state_conv,
            'page_table': page_table}
    (y_p, y_s), sp, ss = run_trunk(x_prompt, x_sample, p, past)
    outs = [y_p, y_s]
    for a, b in zip(sp, ss):
        outs += [a[None], b[None]]
    return tuple(outs)
```

```python
import functools
import jax, jax.numpy as jnp
from jax import lax
import numpy as np
from jax.experimental import pallas as pl
from jax.experimental.pallas import tpu as pltpu

D_MODEL = 1024
BATCH = 2
SEQ = 8192
DEC_BATCH = 128
DEC_SEQ = 4
PAST_LEN = 2048
PAGE_SIZE = 128
EPS = 1e-6
NSA_HEADS = 8
NSA_KV = 2
NSA_REP = NSA_HEADS // NSA_KV
HD = 64
CMP_BLOCK = 32
SLC_BLOCK = 64
SLC_TOPK = 16
WINDOW = 512
Q_BLOCK = 128
ROPE_DIM = HD // 4
ROPE_THETA = 500000.0
GLA_HEADS = 4
GLA_DK = 64
GLA_DV = 128
GLA_LOWRANK = 16
GLA_TAU = 16.0
D_RNN = 1280
LRU_BLOCKS = 10
LRU_BW = D_RNN // LRU_BLOCKS
CONV_W = 4
LRU_C = 8.0
N_GROUPS = 4
EXP_PER_GROUP = 8
N_EXPERTS = N_GROUPS * EXP_PER_GROUP
E_HID = 512
TOP_K_IN_GROUP = 2
NSA_Q_W = NSA_HEADS * HD
NSA_KV_W = 2 * NSA_KV * HD
GLA_K_W = GLA_HEADS * GLA_DK
GLA_V_W = GLA_HEADS * GLA_DV
A_SIZES = (NSA_Q_W, NSA_KV_W, NSA_KV_W, NSA_KV_W, 3 * NSA_HEADS, GLA_K_W, GLA_K_W, GLA_V_W, GLA_LOWRANK, GLA_V_W)
N_PROMPT = BATCH * SEQ
N_SAMPLE = DEC_BATCH * DEC_SEQ
N_TOK = N_PROMPT + N_SAMPLE
N_PAGES = PAST_LEN // PAGE_SIZE

F32 = jnp.float32
BF16 = jnp.bfloat16
VMEM_LIMIT_BYTES = 56 * 1024 * 1024
LANES = 128
SUBLANES = 8
ROW_TILE = 512


def _params(n_axes):
    return pltpu.CompilerParams(dimension_semantics=("arbitrary",) * n_axes, vmem_limit_bytes=VMEM_LIMIT_BYTES)


def _full(shape):
    return pl.BlockSpec(shape, lambda *_: (0,) * len(shape))


def _rows(width):
    return pl.BlockSpec((ROW_TILE, width), lambda i: (i, 0))


def _rms(x, g):
    return x * lax.rsqrt(jnp.mean(x * x, axis=-1, keepdims=True) + EPS) * g


def _softplus(x):
    return jnp.maximum(x, 0.0) + jnp.log1p(jnp.exp(-jnp.abs(x)))


def _gelu_tanh(x):
    return x * (0.5 * (1.0 + jnp.tanh(0.7978845608028654 * (x + 0.044715 * (x * x * x)))))


def _nt_dot(a, b):
    return lax.dot_general(a, b, (((1,), (1,)), ((), ())), preferred_element_type=F32)


def _tn_dot(a, b):
    return lax.dot_general(a, b, (((0,), (0,)), ((), ())), preferred_element_type=F32)


A_Q0, A_KVC0, A_KVS0, A_KVW0 = 0, 512, 768, 1024
A_GQ0, A_GK0, A_GV0, A_GG0, A_MISC0 = 1280, 1536, 1792, 2304, 2816
A_COLS = A_MISC0 + LANES
MISC_LR0 = 3 * NSA_HEADS
KV_HALF = NSA_KV * HD


def _rope_lanes(x, cos_t, sin_lo, sin_hi):
    reps = x.shape[1] // LANES
    tile = (lambda t: jnp.concatenate([t] * reps, axis=1)) if reps > 1 else (lambda t: t)
    w = x.shape[1]
    half = ROPE_DIM // 2
    return x * tile(cos_t) + pltpu.roll(x, half, 1) * tile(sin_hi) + pltpu.roll(x, w - half, 1) * tile(sin_lo)


N_PROMPT_TILES = N_PROMPT // ROW_TILE
_PROMPT_ROWS = pl.BlockSpec((ROW_TILE, D_MODEL), lambda i: (jnp.minimum(i, N_PROMPT_TILES - 1), 0))
_SAMPLE_ROWS = pl.BlockSpec((ROW_TILE, D_MODEL), lambda i: (0, 0))


def _token_rows(prompt_ref, sample_ref):
    return jnp.where(pl.program_id(0) < N_PROMPT_TILES, prompt_ref[...], sample_ref[...])


def _inproj_a_body(hp_ref, hs_ref, g_ref, w_ref, wa2_ref, ba_ref, cos_ref, slo_ref, shi_ref,
                   q_ref, kvc_ref, kvs_ref, kvw_ref, kvct_ref, kvst_ref, kvwt_ref, ks_ref, vst_ref, kw_ref,
                   vwt_ref, gates_ref, gq_ref, gk_ref, gv_ref, la_ref, sgg_ref):
    y = _rms(_token_rows(hp_ref, hs_ref), g_ref[...]).astype(BF16)
    proj = lambda a, b: jnp.dot(y, w_ref[:, a:b], preferred_element_type=F32)
    cos_t, sin_lo, sin_hi = cos_ref[...], slo_ref[...], shi_ref[...]
    q_ref[...] = (_rope_lanes(proj(A_Q0, A_KVC0), cos_t, sin_lo, sin_hi) * (HD ** -0.5)).astype(BF16)
    kvc = proj(A_KVC0, A_KVS0)
    kvc_ref[...] = kvc
    kvct_ref[...] = kvc.T
    for a0, kv_ref, kvt_ref, k_ref, vt_ref in ((A_KVS0, kvs_ref, kvst_ref, ks_ref, vst_ref),
                                               (A_KVW0, kvw_ref, kvwt_ref, kw_ref, vwt_ref)):
        kv = proj(a0, a0 + 2 * KV_HALF)
        k = _rope_lanes(kv[:, :KV_HALF], cos_t, sin_lo, sin_hi)
        vt = kv[:, KV_HALF:].T
        kv_ref[:, :KV_HALF] = k
        kv_ref[:, KV_HALF:] = kv[:, KV_HALF:]
        kvt_ref[:KV_HALF, :] = k.T
        kvt_ref[KV_HALF:, :] = vt
        k_ref[...] = k.astype(BF16)
        vt_ref[...] = vt.astype(BF16)
    misc = proj(A_MISC0, A_COLS)
    gates_ref[...] = jax.nn.sigmoid(misc)
    z = jnp.dot(misc.astype(BF16), wa2_ref[...], preferred_element_type=F32) + ba_ref[...]
    la_ref[...] = -_softplus(-z) * (1.0 / GLA_TAU)
    gq_ref[...] = proj(A_GQ0, A_GK0) * (GLA_DK ** -0.5)
    gk_ref[...] = proj(A_GK0, A_GV0)
    gv_ref[...] = proj(A_GV0, A_GG0).astype(BF16)
    gg = proj(A_GG0, A_MISC0)
    sgg_ref[...] = gg * jax.nn.sigmoid(gg)


def _rope_tables(pos):
    half = ROPE_DIM // 2
    inv = 1.0 / (ROPE_THETA ** (jnp.arange(0, ROPE_DIM, 2, dtype=F32) / ROPE_DIM))
    ang = pos.astype(F32)[:, None] * inv[None, :]
    cos, sin = jnp.cos(ang), jnp.sin(ang)
    n = pos.shape[0]
    one = jnp.ones((n, HD - ROPE_DIM), F32)
    zero = jnp.zeros((n, HD - ROPE_DIM), F32)
    zh = jnp.zeros((n, half), F32)
    seg = lambda a, b, rest: jnp.concatenate([a, b, rest] * (LANES // HD), axis=1)
    return seg(cos, cos, one), seg(-sin, zh, zero), seg(zh, sin, zero)


def inproj_a(h_prompt, h_sample, g, w_in, wa2, ba, pos):
    n = h_prompt.shape[0] + h_sample.shape[0]
    zpad = jnp.zeros((D_MODEL, LANES - 3 * NSA_HEADS - GLA_LOWRANK), F32)
    o = np.cumsum((0,) + A_SIZES)
    w = jnp.concatenate([w_in[:, o[0]:o[4]], w_in[:, o[5]:o[8]], w_in[:, o[9]:o[10]],
                         w_in[:, o[4]:o[5]], w_in[:, o[8]:o[9]], zpad], axis=1).astype(BF16)
    wa2p = jnp.zeros((LANES, GLA_K_W), F32).at[MISC_LR0:MISC_LR0 + GLA_LOWRANK].set(wa2).astype(BF16)
    cols = pl.BlockSpec((KV_HALF, ROW_TILE), lambda i: (0, i))
    kvt = (pl.BlockSpec((NSA_KV_W, ROW_TILE), lambda i: (0, i)), (NSA_KV_W, n), F32)
    outs = [(_rows(NSA_Q_W), (n, NSA_Q_W), BF16), (_rows(NSA_KV_W), (n, NSA_KV_W), F32),
            (_rows(NSA_KV_W), (n, NSA_KV_W), F32), (_rows(NSA_KV_W), (n, NSA_KV_W), F32), kvt, kvt, kvt,
            (_rows(KV_HALF), (n, KV_HALF), BF16), (cols, (KV_HALF, n), BF16),
            (_rows(KV_HALF), (n, KV_HALF), BF16), (cols, (KV_HALF, n), BF16),
            (_rows(LANES), (n, LANES), F32), (_rows(GLA_K_W), (n, GLA_K_W), F32),
            (_rows(GLA_K_W), (n, GLA_K_W), F32), (_rows(GLA_V_W), (n, GLA_V_W), BF16),
            (_rows(GLA_K_W), (n, GLA_K_W), F32), (_rows(GLA_V_W), (n, GLA_V_W), F32)]
    return pl.pallas_call(
        _inproj_a_body,
        grid=(n // ROW_TILE,),
        in_specs=[_PROMPT_ROWS, _SAMPLE_ROWS, _full((1, D_MODEL)), _full((D_MODEL, A_COLS)), _full((LANES, GLA_K_W)),
                  _full((1, GLA_K_W)), _rows(LANES), _rows(LANES), _rows(LANES)],
        out_specs=[s for s, _, _ in outs],
        out_shape=[jax.ShapeDtypeStruct(shape, dt) for _, shape, dt in outs],
        compiler_params=_params(1),
        name="inproj_a",
    )(h_prompt, h_sample, g.reshape(1, D_MODEL), w, wa2p, ba.reshape(1, GLA_K_W), *_rope_tables(pos))


CMP_TILE_BLOCKS = 256
CMP_TILE_ROWS = CMP_TILE_BLOCKS * CMP_BLOCK
CMP_TILE_PAGES = CMP_TILE_ROWS // PAGE_SIZE


def _cmp_reduce(xk_ref, xv_ref, pe_ref, w_ref):
    acc = jnp.zeros((CMP_TILE_BLOCKS, NSA_KV_W), F32)
    for l in range(CMP_BLOCK):
        rows = pl.ds(l, CMP_TILE_BLOCKS, stride=CMP_BLOCK)
        xl = jnp.concatenate([xk_ref[rows, :], xv_ref[rows, :]], axis=1) + pe_ref[l:l + 1, :]
        acc = acc + jnp.dot(xl.astype(BF16), w_ref[l], preferred_element_type=F32)
    return acc


def _cmp_rows_body(xk_ref, xv_ref, pe_ref, w_ref, o_ref):
    o_ref[...] = _cmp_reduce(xk_ref, xv_ref, pe_ref, w_ref)


def _cmp_pages_body(x_ref, pe_ref, w_ref, o_ref, xk_sc, xv_sc):
    for pg in range(CMP_TILE_PAGES):
        rows = slice(pg * PAGE_SIZE, (pg + 1) * PAGE_SIZE)
        xk_sc[rows, :] = x_ref[pg, :KV_HALF, :].T
        xv_sc[rows, :] = x_ref[pg, KV_HALF:, :].T
    o_ref[...] = _cmp_reduce(xk_sc, xv_sc, pe_ref, w_ref)


def _cmp_weights(pe, w_cmp):
    pe_rows = jnp.broadcast_to(pe[:, :, None, :], (CMP_BLOCK, 2, NSA_KV, HD)).reshape(CMP_BLOCK, NSA_KV_W)
    w_bd = jnp.einsum('lcde,cx,gy->lcgdxye', w_cmp, jnp.eye(2, dtype=F32), jnp.eye(NSA_KV, dtype=F32))
    return pe_rows, w_bd.reshape(CMP_BLOCK, NSA_KV_W, NSA_KV_W).astype(BF16)


def cmp_blocks_rows(x, n_rows, pe, w_cmp):
    return pl.pallas_call(
        _cmp_rows_body,
        grid=(n_rows // CMP_TILE_ROWS,),
        in_specs=[pl.BlockSpec((CMP_TILE_ROWS, KV_HALF), lambda i: (i, 0)),
                  pl.BlockSpec((CMP_TILE_ROWS, KV_HALF), lambda i: (i, 1)), _full((CMP_BLOCK, NSA_KV_W)),
                  _full((CMP_BLOCK, NSA_KV_W, NSA_KV_W))],
        out_specs=pl.BlockSpec((CMP_TILE_BLOCKS, NSA_KV_W), lambda i: (i, 0)),
        out_shape=jax.ShapeDtypeStruct((n_rows // CMP_BLOCK, NSA_KV_W), F32),
        compiler_params=_params(1),
        name="cmp_blocks_rows",
    )(x, x, *_cmp_weights(pe, w_cmp))


def cmp_blocks_pages(xt, pe, w_cmp):
    n_pages = xt.shape[0]
    return pl.pallas_call(
        _cmp_pages_body,
        grid=(n_pages // CMP_TILE_PAGES,),
        in_specs=[pl.BlockSpec((CMP_TILE_PAGES, NSA_KV_W, PAGE_SIZE), lambda i: (i, 0, 0)),
                  _full((CMP_BLOCK, NSA_KV_W)), _full((CMP_BLOCK, NSA_KV_W, NSA_KV_W))],
        out_specs=pl.BlockSpec((CMP_TILE_BLOCKS, NSA_KV_W), lambda i: (i, 0)),
        out_shape=jax.ShapeDtypeStruct((n_pages * PAGE_SIZE // CMP_BLOCK, NSA_KV_W), F32),
        scratch_shapes=[pltpu.VMEM((CMP_TILE_ROWS, KV_HALF), F32), pltpu.VMEM((CMP_TILE_ROWS, KV_HALF), F32)],
        compiler_params=_params(1),
        name="cmp_blocks_pages",
    )(xt, *_cmp_weights(pe, w_cmp))


KEY_TILE = 256
NEG_BIG = -1e30
N_CMP_PROMPT = SEQ // CMP_BLOCK
N_SLC_PROMPT = SEQ // SLC_BLOCK
SLC_SHIFT = SLC_BLOCK.bit_length() - 1
N_CMP_DEC = PAST_LEN // CMP_BLOCK
N_SLC_DEC = -(-(PAST_LEN + DEC_SEQ) // SLC_BLOCK)
N_SLC_DEC_PAD = -(-N_SLC_DEC // LANES) * LANES
DEC_COLS_PER_GROUP = NSA_REP * DEC_SEQ


def _tile_cols(x, reps):
    return jnp.concatenate([x] * reps, axis=1) if reps > 1 else x


def _even_first_cmp_end(n_cmp):
    j = lax.broadcasted_iota(jnp.int32, (n_cmp, 1), 0)
    blk = jnp.where(j < n_cmp // 2, 2 * j, 2 * (j - n_cmp // 2) + 1)
    return (blk + 1) * CMP_BLOCK - 1


def _softmax_cols(s, mask):
    s = jnp.where(mask, s, -jnp.inf)
    m = jnp.max(s, axis=0, keepdims=True)
    m = jnp.where(m > -jnp.inf, m, 0.0)
    e = jnp.where(mask, jnp.exp(s - m), 0.0)
    return e / jnp.maximum(jnp.sum(e, axis=0, keepdims=True), 1e-30)


def _softmax_cols_biased(s):
    m = jnp.max(s, axis=0, keepdims=True)
    e = jnp.exp(s - m)
    scale = jnp.where(m > 0.5 * NEG_BIG, 1.0 / jnp.sum(e, axis=0, keepdims=True), 0.0)
    return e * scale


N_FORCED = 3


def _select_cols(p_slc, qpos, n_top):
    ns = p_slc.shape[0]
    blk = lax.broadcasted_iota(jnp.int32, p_slc.shape, 0)
    cur = qpos >> SLC_SHIFT
    forced = ((blk == 0) | (blk == cur) | (blk == cur - 1)) & (blk <= cur)
    score = jnp.where((blk <= cur) & jnp.logical_not(forced), p_slc, -jnp.inf)
    sel = jnp.where(forced, 1.0, 0.0)
    for _ in range(n_top - N_FORCED):
        m = jnp.max(score, axis=0, keepdims=True)
        idx = jnp.min(jnp.where(score == m, blk, ns), axis=0, keepdims=True)
        hit = blk == idx
        sel = jnp.where(hit & (m > -jnp.inf), 1.0, sel)
        score = jnp.where(hit, -jnp.inf, score)
    return sel


def _flash_init(m_sc, l_sc, acc_sc):
    m_sc[...] = jnp.full(m_sc.shape, NEG_BIG, F32)
    l_sc[...] = jnp.zeros(l_sc.shape, F32)
    acc_sc[...] = jnp.zeros(acc_sc.shape, F32)


def _flash_cols(scores, mask, pv, m_sc, l_sc, acc_sc):
    s = jnp.where(mask, scores, NEG_BIG)
    m_old = m_sc[...]
    m_new = jnp.maximum(m_old, jnp.max(s, axis=0, keepdims=True))
    alpha = jnp.exp(m_old - m_new)
    p = jnp.where(mask, jnp.exp(s - m_new), 0.0)
    l_sc[...] = alpha * l_sc[...] + jnp.sum(p, axis=0, keepdims=True)
    acc_sc[...] = alpha * acc_sc[...] + pv(p.astype(BF16))
    m_sc[...] = m_new


def _flash_stream(score_fn, pv_fn, first, lo, n, stream_sc, m_sc, l_sc, acc_sc):
    (sa, ca), (sb, cb) = stream_sc

    def issue(s_ref, c_ref, kt, self_tile=False):
        s = score_fn(kt, self_tile)
        s_ref[...] = s
        c_ref[...] = jnp.max(s, axis=0, keepdims=True)

    def consume(s_ref, c_ref, kt):
        m_old = m_sc[...]
        m_new = jnp.maximum(m_old, c_ref[...])
        alpha = jnp.exp(m_old - m_new)
        p = jnp.exp(s_ref[...] - m_new)
        l_sc[...] = alpha * l_sc[...] + jnp.sum(p, axis=0, keepdims=True)
        acc_sc[...] = alpha * acc_sc[...] + pv_fn(kt, p.astype(BF16))
        m_sc[...] = m_new

    _flash_init(m_sc, l_sc, acc_sc)
    issue(sa, ca, first, True)

    def two_tiles(jj, kt_a):
        t0 = lo + 2 * jj
        issue(sb, cb, t0)
        consume(sa, ca, kt_a)
        issue(sa, ca, t0 + 1)
        consume(sb, cb, t0)
        return t0 + 1

    kt_a = lax.fori_loop(0, n // 2, two_tiles, first)

    @pl.when(n % 2 == 1)
    def _():
        issue(sb, cb, lo + n - 1)
        consume(sa, ca, kt_a)
        consume(sb, cb, lo + n - 1)

    @pl.when(n % 2 == 0)
    def _():
        consume(sa, ca, kt_a)

    return acc_sc[...] / jnp.maximum(l_sc[...], 1e-30)


def _flash_out(l_sc, acc_sc):
    return acc_sc[...] / jnp.maximum(l_sc[...], 1e-30)


def _pv_split(vt, p):
    c2 = p.shape[1] // 2
    return jnp.concatenate([jnp.dot(vt[:HD], p[:, :c2], preferred_element_type=F32),
                            jnp.dot(vt[HD:], p[:, c2:], preferred_element_type=F32)], axis=1)


def _nsa_prompt_body(q_ref, gates_ref, ckv_ref, cos_ref, slo_ref, shi_ref, ks_ref, vst_ref, kw_ref, vwt_ref,
                     o_ref, m_sc, l_sc, acc_sc, sel_bias_sc, sa_sc, ca_sc, sb_sc, cb_sc):
    stream_sc = ((sa_sc, ca_sc), (sb_sc, cb_sc))
    i = pl.program_id(1)
    nq = Q_BLOCK
    cols = NSA_HEADS * nq
    qpos = i * nq + lax.broadcasted_iota(jnp.int32, (1, nq), 1)
    q = q_ref[...].astype(F32)
    pairs = [q[:, j * LANES:(j + 1) * LANES].T for j in range(NSA_HEADS // 2)]
    zero = jnp.zeros((HD, cols // 2), F32)
    qt_g = [jnp.concatenate([pairs[2 * g][:HD], pairs[2 * g][HD:], pairs[2 * g + 1][:HD], pairs[2 * g + 1][HD:]],
                            axis=1) for g in range(NSA_KV)]
    qt = jnp.concatenate([jnp.concatenate([qt_g[0], zero], axis=1),
                          jnp.concatenate([zero, qt_g[1]], axis=1)], axis=0).astype(BF16)
    ckv = ckv_ref[0]
    ck = _rope_lanes(ckv[:, :KV_HALF], cos_ref[...], slo_ref[...], shi_ref[...]).astype(BF16)
    cvt = ckv[:, KV_HALF:].T.astype(BF16)
    c_bias = jnp.where(_even_first_cmp_end(N_CMP_PROMPT) <= qpos, 0.0, NEG_BIG)
    p = _softmax_cols_biased(jnp.dot(ck, qt, preferred_element_type=F32) + _tile_cols(c_bias, NSA_HEADS))
    o_c = _pv_split(cvt, p.astype(BF16))
    sel = []
    for g in range(NSA_KV):
        c0 = g * NSA_REP * nq
        p_grp = p[:, c0:c0 + nq]
        for r in range(1, NSA_REP):
            p_grp = p_grp + p[:, c0 + r * nq:c0 + (r + 1) * nq]
        p_slc = p_grp[:N_CMP_PROMPT // 2] + p_grp[N_CMP_PROMPT // 2:]
        sel.append(_select_cols(p_slc, qpos, SLC_TOPK))
    sel = jnp.concatenate(sel, axis=1)
    sel_bias_sc[...] = jnp.where(sel > 0.5, 0.0, NEG_BIG)
    key_row = lax.broadcasted_iota(jnp.int32, (KEY_TILE, 1), 0)
    blocks_per_tile = KEY_TILE // SLC_BLOCK
    kt_self = i // (KEY_TILE // nq)

    def slc_scores(kt, self_tile):
        k0 = pl.multiple_of(kt * KEY_TILE, KEY_TILE)
        rows = []
        for j in range(blocks_per_tile):
            b = sel_bias_sc[pl.ds(kt * blocks_per_tile + j, 1), :]
            rows.append(jnp.concatenate(
                [jnp.broadcast_to(b[:, g * nq:(g + 1) * nq], (SLC_BLOCK, nq)) for g in range(NSA_KV)
                 for _ in range(NSA_REP)], axis=1))
        bias = jnp.concatenate(rows, axis=0)
        if self_tile:
            bias = bias + _tile_cols(jnp.where(k0 + key_row <= qpos, 0.0, NEG_BIG), NSA_HEADS)
        return jnp.dot(ks_ref[pl.ds(k0, KEY_TILE), :], qt, preferred_element_type=F32) + bias

    def slc_pv(kt, pb):
        return _pv_split(vst_ref[:, pl.ds(pl.multiple_of(kt * KEY_TILE, KEY_TILE), KEY_TILE)], pb)

    o_s = _flash_stream(slc_scores, slc_pv, kt_self, 0, kt_self, stream_sc, m_sc, l_sc, acc_sc)

    def win_scores(kt, self_tile):
        k0 = pl.multiple_of(kt * KEY_TILE, KEY_TILE)
        d = qpos - (k0 + key_row)
        bias = jnp.where((d >= 0) & (d < WINDOW), 0.0, NEG_BIG)
        return jnp.dot(kw_ref[pl.ds(k0, KEY_TILE), :], qt, preferred_element_type=F32) + _tile_cols(bias, NSA_HEADS)

    def win_pv(kt, pb):
        return _pv_split(vwt_ref[:, pl.ds(pl.multiple_of(kt * KEY_TILE, KEY_TILE), KEY_TILE)], pb)

    win_lo = jnp.maximum(i - WINDOW // nq, 0) // (KEY_TILE // nq)
    o_w = _flash_stream(win_scores, win_pv, kt_self, win_lo, kt_self - win_lo, stream_sc, m_sc, l_sc, acc_sc)
    gates_t = gates_ref[...].T
    merged = []
    for hd in range(NSA_HEADS):
        cs = slice(hd * nq, (hd + 1) * nq)
        gate = lambda br: gates_t[3 * hd + br:3 * hd + br + 1]
        merged.append(gate(0) * o_c[:, cs] + gate(1) * o_s[:, cs] + gate(2) * o_w[:, cs])
    for j in range(NSA_HEADS // 2):
        pair = jnp.concatenate([merged[2 * j], merged[2 * j + 1]], axis=0).T
        o_ref[:, j * LANES:(j + 1) * LANES] = pair.astype(BF16)


def nsa_prompt_attn(q, gates, ckv, ks, vst, kw, vwt):
    nqb = SEQ // Q_BLOCK
    tok = lambda wd: pl.BlockSpec((Q_BLOCK, wd), lambda b, i: (b * nqb + i, 0))
    seq_rows = pl.BlockSpec((SEQ, KV_HALF), lambda b, i: (b, 0))
    seq_cols = pl.BlockSpec((KV_HALF, SEQ), lambda b, i: (0, b))
    n_cmp = N_CMP_PROMPT
    c_blk = jnp.concatenate([jnp.arange(0, n_cmp, 2), jnp.arange(1, n_cmp, 2)]).astype(jnp.int32)
    cols = NSA_HEADS * Q_BLOCK
    return pl.pallas_call(
        _nsa_prompt_body,
        grid=(BATCH, nqb),
        in_specs=[tok(NSA_Q_W), tok(LANES), pl.BlockSpec((1, n_cmp, NSA_KV_W), lambda b, i: (b, 0, 0)),
                  _full((n_cmp, LANES)), _full((n_cmp, LANES)), _full((n_cmp, LANES)),
                  seq_rows, seq_cols, seq_rows, seq_cols],
        out_specs=tok(NSA_Q_W),
        out_shape=jax.ShapeDtypeStruct((N_PROMPT, NSA_Q_W), BF16),
        scratch_shapes=[pltpu.VMEM((1, cols), F32), pltpu.VMEM((1, cols), F32), pltpu.VMEM((HD, cols), F32),
                        pltpu.VMEM((N_SLC_PROMPT, NSA_KV * Q_BLOCK), F32),
                        pltpu.VMEM((KEY_TILE, cols), F32), pltpu.VMEM((1, cols), F32),
                        pltpu.VMEM((KEY_TILE, cols), F32), pltpu.VMEM((1, cols), F32)],
        compiler_params=_params(2),
        name="nsa_prompt",
    )(q, gates, ckv, *_rope_tables((c_blk + 1) * CMP_BLOCK - 1), ks, vst, kw, vwt)


def _nsa_decode_body(pt_ref, qt_ref, gate_ref, ckv_ref, cos_ref, slo_ref, shi_ref, hsum_ref, *refs):
    page_refs = refs[:N_PAGES]
    win_ref, kvs_new_ref, kvw_new_ref, o_ref, win_out_ref, m_sc, l_sc, acc_sc = refs[N_PAGES:]
    qt = qt_ref[0]
    lane = lax.broadcasted_iota(jnp.int32, (1, LANES), 1)
    qpos = PAST_LEN + (lane & (DEC_SEQ - 1))
    group0 = lane < DEC_COLS_PER_GROUP
    ckv = ckv_ref[0]
    ck = _rope_lanes(ckv[:, :KV_HALF], cos_ref[...], slo_ref[...], shi_ref[...]).astype(BF16)
    c_mask = _even_first_cmp_end(N_CMP_DEC) <= qpos
    p = _softmax_cols(jnp.dot(ck, qt, preferred_element_type=F32), c_mask)
    o_c = _tn_dot(ckv[:, KV_HALF:].astype(BF16), p.astype(BF16))
    p_grp = jnp.dot(p, hsum_ref[...], preferred_element_type=F32, precision=lax.Precision.HIGHEST)
    p_slc = jnp.concatenate([p_grp[:N_CMP_DEC // 2] + p_grp[N_CMP_DEC // 2:],
                             jnp.zeros((N_SLC_DEC_PAD - N_CMP_DEC // 2, LANES), F32)], axis=0)
    sel = _select_cols(p_slc, qpos, SLC_TOPK)

    def new_rows(ref):
        row = ref[0]
        kv = jnp.concatenate([row[:, t * NSA_KV_W:(t + 1) * NSA_KV_W] for t in range(DEC_SEQ)], axis=0)
        return jnp.concatenate([kv, jnp.zeros((SUBLANES - DEC_SEQ, NSA_KV_W), F32)], axis=0)

    new_row = lax.broadcasted_iota(jnp.int32, (SUBLANES, 1), 0)
    new_pos = PAST_LEN + new_row
    new_valid = new_row < DEC_SEQ
    _flash_init(m_sc, l_sc, acc_sc)
    kt_old = jnp.concatenate([r[0, :KV_HALF, :] for r in page_refs], axis=1).astype(BF16)
    vt_old = jnp.concatenate([r[0, KV_HALF:, :] for r in page_refs], axis=1).astype(BF16)
    key_blk = lax.broadcasted_iota(jnp.int32, (PAST_LEN, N_SLC_DEC_PAD), 0) >> SLC_SHIFT
    blk_col = lax.broadcasted_iota(jnp.int32, (PAST_LEN, N_SLC_DEC_PAD), 1)
    chosen = jnp.dot(jnp.where(key_blk == blk_col, 1.0, 0.0).astype(BF16), sel.astype(BF16),
                     preferred_element_type=F32) > 0.5
    old_pos = lax.broadcasted_iota(jnp.int32, (PAST_LEN, 1), 0)
    _flash_cols(_tn_dot(kt_old, qt), chosen & (old_pos <= qpos),
                lambda pb: jnp.dot(vt_old, pb, preferred_element_type=F32), m_sc, l_sc, acc_sc)
    kv_new = new_rows(kvs_new_ref)
    sel_new = sel[(PAST_LEN >> SLC_SHIFT):(PAST_LEN >> SLC_SHIFT) + 1] > 0.5
    v_new = kv_new[:, KV_HALF:].astype(BF16)
    _flash_cols(jnp.dot(kv_new[:, :KV_HALF].astype(BF16), qt, preferred_element_type=F32),
                sel_new & new_valid & (new_pos <= qpos), lambda pb: _tn_dot(v_new, pb), m_sc, l_sc, acc_sc)
    o_s = _flash_out(l_sc, acc_sc)
    _flash_init(m_sc, l_sc, acc_sc)
    n_buf = win_ref.shape[2]
    win = win_ref[0]
    d = qpos - (PAST_LEN - n_buf + lax.broadcasted_iota(jnp.int32, (n_buf, 1), 0))
    vt_win = win[KV_HALF:].astype(BF16)
    _flash_cols(_tn_dot(win[:KV_HALF].astype(BF16), qt), (d >= 0) & (d < WINDOW),
                lambda pb: jnp.dot(vt_win, pb, preferred_element_type=F32), m_sc, l_sc, acc_sc)
    kw_new = new_rows(kvw_new_ref)
    d = qpos - new_pos
    vw_new = kw_new[:, KV_HALF:].astype(BF16)
    _flash_cols(jnp.dot(kw_new[:, :KV_HALF].astype(BF16), qt, preferred_element_type=F32),
                new_valid & (d >= 0) & (d < WINDOW), lambda pb: _tn_dot(vw_new, pb), m_sc, l_sc, acc_sc)
    o_w = _flash_out(l_sc, acc_sc)
    g = gate_ref[0]
    o = g[0:1] * o_c + g[1:2] * o_s + g[2:3] * o_w
    o_ref[0] = jnp.where(group0, o[:HD], o[HD:])
    key = lax.broadcasted_iota(jnp.int32, (SUBLANES, n_buf), 1)
    place = jnp.where((key == n_buf - DEC_SEQ + new_row) & new_valid, 1.0, 0.0)
    placed = lax.dot_general(kw_new, place, (((0,), (0,)), ((), ())), preferred_element_type=F32,
                             precision=lax.Precision.HIGHEST)
    keep = lax.broadcasted_iota(jnp.int32, (1, n_buf), 1) < n_buf - DEC_SEQ
    win_out_ref[0] = jnp.where(keep, pltpu.roll(win, n_buf - DEC_SEQ, 1), placed)


def nsa_decode_attn(qt, gate_rows, ckv, slc_pool, win_buf, kvs_new, kvw_new, page_table):
    n_buf = win_buf.shape[2]
    per_b = lambda *shape: pl.BlockSpec((1,) + shape, lambda b, pt: (b,) + (0,) * len(shape))
    const = lambda *shape: pl.BlockSpec(shape, lambda b, pt: (0,) * len(shape))
    page = lambda j: pl.BlockSpec((1, NSA_KV_W, PAGE_SIZE), lambda b, pt: (pt[b, j], 0, 0))
    c_blk = jnp.concatenate([jnp.arange(0, N_CMP_DEC, 2), jnp.arange(1, N_CMP_DEC, 2)]).astype(jnp.int32)
    col = jnp.arange(LANES)
    used = col < NSA_KV * DEC_COLS_PER_GROUP
    same = (col[:, None] // DEC_COLS_PER_GROUP == col[None, :] // DEC_COLS_PER_GROUP) & \
           (col[:, None] % DEC_SEQ == col[None, :] % DEC_SEQ) & used[:, None] & used[None, :]
    return pl.pallas_call(
        _nsa_decode_body,
        grid_spec=pltpu.PrefetchScalarGridSpec(
            num_scalar_prefetch=1,
            grid=(DEC_BATCH,),
            in_specs=[per_b(LANES, LANES), per_b(3, LANES), per_b(N_CMP_DEC, NSA_KV_W),
                      const(N_CMP_DEC, LANES), const(N_CMP_DEC, LANES), const(N_CMP_DEC, LANES),
                      const(LANES, LANES)] + [page(j) for j in range(N_PAGES)]
                     + [per_b(NSA_KV_W, n_buf), per_b(1, DEC_SEQ * NSA_KV_W), per_b(1, DEC_SEQ * NSA_KV_W)],
            out_specs=[per_b(HD, LANES), per_b(NSA_KV_W, n_buf)],
            scratch_shapes=[pltpu.VMEM((1, LANES), F32), pltpu.VMEM((1, LANES), F32),
                            pltpu.VMEM((LANES, LANES), F32)],
        ),
        out_shape=[jax.ShapeDtypeStruct((DEC_BATCH, HD, LANES), F32),
                   jax.ShapeDtypeStruct((DEC_BATCH, NSA_KV_W, n_buf), F32)],
        compiler_params=_params(1),
        name="nsa_decode",
    )(page_table, qt, gate_rows, ckv, *_rope_tables((c_blk + 1) * CMP_BLOCK - 1), same.astype(F32),
      *([slc_pool] * N_PAGES), win_buf, kvs_new, kvw_new)


GLA_PAIRS = GLA_HEADS // 2
GLA_ROWS = 128
GLA_SUB = 16
GLA_STEP_SEQS = 8


def _gla_rows(q, k, la, v, sgg, gnorm, st_ref, sub):
    R = q.shape[0]
    row = lax.broadcasted_iota(jnp.int32, (R, GLA_K_W), 0)
    rin = row % sub
    cum = la
    d = 1
    while d < sub:
        cum = cum + jnp.where(rin >= d, pltpu.roll(cum, d, 0), 0.0)
        d *= 2
    lane = lax.broadcasted_iota(jnp.int32, (sub, LANES), 1)
    lo = lane < GLA_DK
    rsub = lax.broadcasted_iota(jnp.int32, (sub, LANES), 0)
    out_rows = []
    for c in range(R // sub):
        rs = slice(c * sub, (c + 1) * sub)
        cum_c = cum[rs]
        last = cum_c[sub - 1:sub]
        qe = q[rs] * jnp.exp(cum_c)
        kdec = k[rs] * jnp.exp(last - cum_c)
        v_c = v[rs]
        heads = []
        for pr in range(GLA_PAIRS):
            ls = slice(pr * LANES, (pr + 1) * LANES)
            st = st_ref[pr]
            st_b = st.astype(BF16)
            qe_p, kd_p, q_p, k_p, cum_p = qe[:, ls], kdec[:, ls], q[rs, ls], k[rs, ls], cum_c[:, ls]
            v_pair = [v_c[:, (2 * pr + hh) * GLA_DV:(2 * pr + hh + 1) * GLA_DV] for hh in range(2)]
            upd = jnp.zeros((GLA_DV, LANES), F32)
            o_pair = []
            for hh in range(2):
                keep = lo if hh == 0 else jnp.logical_not(lo)
                o_pair.append(_nt_dot(jnp.where(keep, qe_p, 0.0).astype(BF16), st_b))
                upd = upd + _tn_dot(v_pair[hh].astype(BF16), jnp.where(keep, kd_p, 0.0).astype(BF16))
            for j in range(sub):
                dj = jnp.where(rsub >= j, jnp.exp(cum_p - cum_p[j:j + 1]), 0.0)
                w = q_p * k_p[j:j + 1] * dj
                a_lo = jnp.sum(jnp.where(lo, w, 0.0), axis=-1, keepdims=True)
                a_hi = jnp.sum(jnp.where(lo, 0.0, w), axis=-1, keepdims=True)
                o_pair[0] = o_pair[0] + a_lo * v_pair[0][j:j + 1]
                o_pair[1] = o_pair[1] + a_hi * v_pair[1][j:j + 1]
            st_ref[pr] = st * jnp.exp(last[:, ls]) + upd
            heads += o_pair
        out_rows.append(jnp.concatenate([_rms(x, gnorm) for x in heads], axis=1))
    return jnp.concatenate(out_rows, axis=0) * sgg


def _gla_seq_body(q_ref, k_ref, la_ref, v_ref, sgg_ref, gn_ref, o_ref, st_out_ref, st_sc):
    @pl.when(pl.program_id(1) == 0)
    def _():
        st_sc[...] = jnp.zeros(st_sc.shape, F32)

    o = _gla_rows(q_ref[...], k_ref[...], la_ref[...], v_ref[...].astype(F32), sgg_ref[...], gn_ref[...],
                  st_sc, GLA_SUB)
    o_ref[...] = o.astype(BF16)
    st_out_ref[0] = st_sc[...]


def gla_seq(q, k, la, v, sgg, gnorm):
    nt = SEQ // GLA_ROWS
    rows = lambda wd: pl.BlockSpec((GLA_ROWS, wd), lambda b, t: (b * nt + t, 0))
    return pl.pallas_call(
        _gla_seq_body,
        grid=(BATCH, nt),
        in_specs=[rows(GLA_K_W), rows(GLA_K_W), rows(GLA_K_W), rows(GLA_V_W), rows(GLA_V_W),
                  _full((1, GLA_DV))],
        out_specs=[rows(GLA_V_W), pl.BlockSpec((1, GLA_PAIRS, GLA_DV, LANES), lambda b, t: (b, 0, 0, 0))],
        out_shape=[jax.ShapeDtypeStruct((N_PROMPT, GLA_V_W), BF16),
                   jax.ShapeDtypeStruct((BATCH, GLA_PAIRS, GLA_DV, LANES), F32)],
        scratch_shapes=[pltpu.VMEM((GLA_PAIRS, GLA_DV, LANES), F32)],
        compiler_params=_params(2),
        name="gla_seq",
    )(q, k, la, v, sgg, gnorm.reshape(1, GLA_DV))


def _gla_step_body(q_ref, k_ref, la_ref, v_ref, sgg_ref, gn_ref, st_in_ref, o_ref, st_out_ref):
    st_out_ref[...] = st_in_ref[...]
    q, k, la, v, sgg = q_ref[...], k_ref[...], la_ref[...], v_ref[...].astype(F32), sgg_ref[...]
    for j in range(GLA_STEP_SEQS):
        rs = slice(j * DEC_SEQ, (j + 1) * DEC_SEQ)
        o = _gla_rows(q[rs], k[rs], la[rs], v[rs], sgg[rs], gn_ref[...], st_out_ref.at[j], DEC_SEQ)
        o_ref[rs, :] = o.astype(BF16)


def gla_step(q, k, la, v, sgg, gnorm, st_in, row0):
    rows_per = GLA_STEP_SEQS * DEC_SEQ
    blk0 = row0 // rows_per
    rows = lambda wd: pl.BlockSpec((rows_per, wd), lambda i: (blk0 + i, 0))
    st_spec = pl.BlockSpec((GLA_STEP_SEQS, GLA_PAIRS, GLA_DV, LANES), lambda i: (i, 0, 0, 0))
    return pl.pallas_call(
        _gla_step_body,
        grid=(DEC_BATCH // GLA_STEP_SEQS,),
        in_specs=[rows(GLA_K_W), rows(GLA_K_W), rows(GLA_K_W), rows(GLA_V_W), rows(GLA_V_W),
                  _full((1, GLA_DV)), st_spec],
        out_specs=[pl.BlockSpec((rows_per, GLA_V_W), lambda i: (i, 0)), st_spec],
        out_shape=[jax.ShapeDtypeStruct((N_SAMPLE, GLA_V_W), BF16),
                   jax.ShapeDtypeStruct((DEC_BATCH, GLA_PAIRS, GLA_DV, LANES), F32)],
        compiler_params=_params(1),
        name="gla_step",
    )(q, k, la, v, sgg, gnorm.reshape(1, GLA_DV), st_in)


def _gla_state_to_pairs(s):
    B = s.shape[0]
    return s.reshape(B, GLA_PAIRS, 2, GLA_DK, GLA_DV).transpose(0, 1, 4, 2, 3).reshape(B, GLA_PAIRS, GLA_DV, LANES)


def _gla_state_from_pairs(st):
    B = st.shape[0]
    return st.reshape(B, GLA_PAIRS, GLA_DV, 2, GLA_DK).transpose(0, 1, 3, 4, 2).reshape(B, GLA_HEADS, GLA_DK, GLA_DV)


LRU_TIME_TILE = 256


def _lru_gates(xc, wa_ref, ba, wx_ref, bx, lam):
    xcb = xc.astype(BF16)
    r_parts, i_parts = [], []
    for n in range(LRU_BLOCKS):
        xs = xcb[:, n * LRU_BW:(n + 1) * LRU_BW]
        r_parts.append(jnp.dot(xs, wa_ref[n], preferred_element_type=F32))
        i_parts.append(jnp.dot(xs, wx_ref[n], preferred_element_type=F32))
    r = jax.nn.sigmoid(jnp.concatenate(r_parts, axis=-1) + ba)
    i = jax.nn.sigmoid(jnp.concatenate(i_parts, axis=-1) + bx)
    log_a = -LRU_C * r * _softplus(-lam)
    a = jnp.exp(log_a)
    u = jnp.sqrt(1.0 - a * a) * (i * xc)
    return a, u


def _lru_seq_body(u_ref, cw_ref, cb_ref, wa_ref, ba_ref, wx_ref, bx_ref, lam_ref, y_ref, hT_ref, xp_sc, h_sc):
    tt = LRU_TIME_TILE

    @pl.when(pl.program_id(1) == 0)
    def _():
        xp_sc[0:SUBLANES, :] = jnp.zeros((SUBLANES, D_RNN), F32)
        h_sc[...] = jnp.zeros((1, D_RNN), F32)

    xp_sc[SUBLANES:SUBLANES + tt, :] = u_ref[:, D_RNN:]
    xc = cb_ref[...]
    for w in range(CONV_W):
        off = SUBLANES - (CONV_W - 1) + w
        xc = xc + cw_ref[w:w + 1, :] * xp_sc[off:off + tt, :]
    a, u = _lru_gates(xc, wa_ref, ba_ref[...], wx_ref, bx_ref[...], lam_ref[...])
    row = lax.broadcasted_iota(jnp.int32, (tt, D_RNN), 0) % SUBLANES
    d = 1
    while d < SUBLANES:
        keep = row >= d
        a_prev = jnp.where(keep, pltpu.roll(a, d, 0), 1.0)
        u_prev = jnp.where(keep, pltpu.roll(u, d, 0), 0.0)
        u = a * u_prev + u
        a = a * a_prev
        d *= 2
    carry = h_sc[...]
    hs = []
    for grp in range(tt // SUBLANES):
        rs = slice(grp * SUBLANES, (grp + 1) * SUBLANES)
        hs.append(a[rs] * carry + u[rs])
        carry = hs[-1][SUBLANES - 1:SUBLANES, :]
    h_sc[...] = carry
    hT_ref[0] = carry
    y_ref[...] = (_gelu_tanh(u_ref[:, :D_RNN]) * jnp.concatenate(hs, axis=0)).astype(BF16)
    xp_sc[0:SUBLANES, :] = xp_sc[tt:tt + SUBLANES, :]


def _lru_weight_args(cw, cb, wa, ba, wx, bx, lam):
    row = lambda v: v.reshape(1, D_RNN)
    return (cw, row(cb), wa.astype(BF16), row(ba), wx.astype(BF16), row(bx), row(lam))


_LRU_WEIGHT_SPECS = [_full((CONV_W, D_RNN)), _full((1, D_RNN)), _full((LRU_BLOCKS, LRU_BW, LRU_BW)),
                     _full((1, D_RNN)), _full((LRU_BLOCKS, LRU_BW, LRU_BW)), _full((1, D_RNN)),
                     _full((1, D_RNN))]


def lru_seq(u, cw, cb, wa, ba, wx, bx, lam):
    tt = LRU_TIME_TILE
    nt = SEQ // tt
    y, hT = pl.pallas_call(
        _lru_seq_body,
        grid=(BATCH, nt),
        in_specs=[pl.BlockSpec((tt, 2 * D_RNN), lambda b, t: (b * nt + t, 0))] + _LRU_WEIGHT_SPECS,
        out_specs=[pl.BlockSpec((tt, D_RNN), lambda b, t: (b * nt + t, 0)),
                   pl.BlockSpec((1, 1, D_RNN), lambda b, t: (b, 0, 0))],
        out_shape=[jax.ShapeDtypeStruct((N_PROMPT, D_RNN), BF16), jax.ShapeDtypeStruct((BATCH, 1, D_RNN), F32)],
        scratch_shapes=[pltpu.VMEM((tt + SUBLANES, D_RNN), F32), pltpu.VMEM((1, D_RNN), F32)],
        compiler_params=_params(2),
        name="lru_seq",
    )(u, *_lru_weight_args(cw, cb, wa, ba, wx, bx, lam))
    return y, hT.reshape(BATCH, D_RNN)


def _lru_step_body(u_ref, cs_ref, h0_ref, cw_ref, cb_ref, wa_ref, ba_ref, wx_ref, bx_ref, lam_ref, y_ref, hT_ref):
    n_t = u_ref.shape[0]
    hist = [cs_ref[:, w, :] for w in range(CONV_W - 1)] + [u_ref[t, :, D_RNN:] for t in range(n_t)]
    h = h0_ref[...]
    for t in range(n_t):
        xc = cb_ref[...]
        for w in range(CONV_W):
            xc = xc + cw_ref[w:w + 1, :] * hist[t + w]
        a, u = _lru_gates(xc, wa_ref, ba_ref[...], wx_ref, bx_ref[...], lam_ref[...])
        h = a * h + u
        y_ref[t] = (_gelu_tanh(u_ref[t, :, :D_RNN]) * h).astype(BF16)
    hT_ref[...] = h


def lru_step(u, conv_state, h0, cw, cb, wa, ba, wx, bx, lam):
    T, B, _ = u.shape
    return pl.pallas_call(
        _lru_step_body,
        out_shape=[jax.ShapeDtypeStruct((T, B, D_RNN), BF16), jax.ShapeDtypeStruct((B, D_RNN), F32)],
        compiler_params=pltpu.CompilerParams(vmem_limit_bytes=VMEM_LIMIT_BYTES),
        name="lru_step",
    )(u, conv_state, h0, *_lru_weight_args(cw, cb, wa, ba, wx, bx, lam))


ROUTE_E1, ROUTE_E2, ROUTE_G1, ROUTE_G2, ROUTE_R1, ROUTE_R2 = range(6)
EXPERT_LANE0 = N_GROUPS


def _lane_pick(val_by_lane):
    rows = next(iter(val_by_lane.values())).shape[0]
    lane = lax.broadcasted_iota(jnp.int32, (rows, LANES), 1)
    out = jnp.zeros((rows, LANES), F32)
    for l, v in val_by_lane.items():
        out = jnp.where(lane == l, v, out)
    return out


def _route_rows(logits, tri_ref, carry_ref):
    rows = logits.shape[0]
    lane = lax.broadcasted_iota(jnp.int32, (rows, LANES), 1)
    neg = -jnp.inf
    gl = jnp.where(lane < N_GROUPS, logits, neg)
    gmax = jnp.max(gl, axis=-1, keepdims=True)
    gtop = jnp.min(jnp.where(gl == gmax, lane, LANES), axis=-1, keepdims=True)
    gsum = jnp.sum(jnp.where(lane < N_GROUPS, jnp.exp(logits - gmax), 0.0), axis=-1, keepdims=True)
    g_w = 1.0 / gsum
    lo = EXPERT_LANE0 + EXP_PER_GROUP * gtop
    el = jnp.where((lane >= lo) & (lane < lo + EXP_PER_GROUP), logits, neg)
    v1 = jnp.max(el, axis=-1, keepdims=True)
    i1 = jnp.min(jnp.where(el == v1, lane, LANES), axis=-1, keepdims=True)
    el2 = jnp.where(lane == i1, neg, el)
    v2 = jnp.max(el2, axis=-1, keepdims=True)
    i2 = jnp.min(jnp.where(el2 == v2, lane, LANES), axis=-1, keepdims=True)
    p2 = jnp.exp(v2 - v1)
    den = 1.0 + p2
    gate1 = (1.0 / den) * g_w
    gate2 = (p2 / den) * g_w
    hit1 = lane == i1
    hit2 = lane == i2
    onehot = jnp.where(hit1 | hit2, 1.0, 0.0)
    before = jnp.dot(tri_ref[...], onehot.astype(BF16), preferred_element_type=F32) + carry_ref[...]
    rank1 = jnp.sum(jnp.where(hit1, before, 0.0), axis=-1, keepdims=True)
    rank2 = jnp.sum(jnp.where(hit2, before, 0.0), axis=-1, keepdims=True)
    carry_ref[...] = carry_ref[...] + jnp.sum(onehot, axis=0, keepdims=True)
    return _lane_pick({ROUTE_E1: (i1 - EXPERT_LANE0).astype(F32), ROUTE_E2: (i2 - EXPERT_LANE0).astype(F32),
                       ROUTE_G1: gate1, ROUTE_G2: gate2, ROUTE_R1: rank1, ROUTE_R2: rank2})


def _pack_bf16_halves(x):
    w = x.shape[1] // 2
    bits = lambda v: pltpu.bitcast(v.astype(F32), jnp.uint32)
    return pltpu.bitcast((bits(x[:, :w]) >> 16) | bits(x[:, w:]), F32)


def _unpack_bf16_halves(words):
    p = pltpu.bitcast(words, jnp.uint32)
    lo = pltpu.bitcast(p << 16, F32).astype(BF16)
    hi = pltpu.bitcast(p & jnp.uint32(0xFFFF0000), F32).astype(BF16)
    return jnp.concatenate([lo, hi], axis=1)


def _post_mixer_body(n_h, n_mix, *refs):
    h_refs, refs = refs[:n_h], refs[n_h:]
    m_refs, wo_refs = refs[:n_mix], refs[n_mix:2 * n_mix]
    g_ref, wr_ref, br_ref, tri_ref, h1_ref, hn_ref, route_ref, cnt_ref, carry_sc = refs[2 * n_mix:]

    @pl.when(pl.program_id(0) == 0)
    def _():
        carry_sc[...] = jnp.zeros((1, LANES), F32)

    mix = jnp.dot(m_refs[0][...], wo_refs[0][...], preferred_element_type=F32)
    for m_ref, wo_ref in zip(m_refs[1:], wo_refs[1:]):
        mix = mix + jnp.dot(m_ref[...], wo_ref[...], preferred_element_type=F32)
    h1 = (h_refs[0][...] if n_h == 1 else _token_rows(*h_refs)) + mix
    h1_ref[...] = h1
    hn = _rms(h1, g_ref[...]).astype(BF16)
    hn_ref[...] = _pack_bf16_halves(hn)
    logits = jnp.dot(hn, wr_ref[...], preferred_element_type=F32) + br_ref[...]
    route_ref[...] = _route_rows(logits, tri_ref, carry_sc)
    cnt_ref[...] = carry_sc[...]


def post_mixer(h, mix_ins, w_out, g_ffn, w_rg, b_rg, w_re, b_re):
    hs = list(h) if isinstance(h, (tuple, list)) else [h]
    h_specs = [_PROMPT_ROWS, _SAMPLE_ROWS] if len(hs) == 2 else [_rows(D_MODEL)]
    n = mix_ins[0].shape[0]
    ks = [m.shape[1] for m in mix_ins]
    offs = np.cumsum([0] + ks)
    w_parts = [w_out[offs[j]:offs[j + 1]].astype(BF16) for j in range(len(ks))]
    pad = LANES - N_GROUPS - N_EXPERTS
    wr = jnp.concatenate([w_rg, w_re, jnp.zeros((D_MODEL, pad), F32)], axis=1).astype(BF16)
    br = jnp.concatenate([b_rg, b_re, jnp.zeros((pad,), F32)]).reshape(1, LANES)
    tri = jnp.tril(jnp.ones((ROW_TILE, ROW_TILE), BF16), -1)
    return pl.pallas_call(
        functools.partial(_post_mixer_body, len(hs), len(ks)),
        grid=(n // ROW_TILE,),
        in_specs=h_specs + [_rows(k) for k in ks] + [_full((k, D_MODEL)) for k in ks]
                 + [_full((1, D_MODEL)), _full((D_MODEL, LANES)), _full((1, LANES)), _full((ROW_TILE, ROW_TILE))],
        out_specs=[_rows(D_MODEL), _rows(D_MODEL // 2), _rows(LANES), _full((1, LANES))],
        out_shape=[jax.ShapeDtypeStruct((n, D_MODEL), F32), jax.ShapeDtypeStruct((n, D_MODEL // 2), F32),
                   jax.ShapeDtypeStruct((n, LANES), F32), jax.ShapeDtypeStruct((1, LANES), F32)],
        scratch_shapes=[pltpu.VMEM((1, LANES), F32)],
        compiler_params=_params(1),
        name="post_mixer",
    )(*hs, *mix_ins, *w_parts, g_ffn.reshape(1, D_MODEL), wr, br, tri)


MOE_ROWS = 512


def _ffn_body(be_ref, nb_ref, x_ref, w1_ref, w3_ref, w2_ref, y_ref, w1_sc, w3_sc, w2_sc):
    i = pl.program_id(0)
    new_expert = jnp.logical_or(i == 0, be_ref[i] != be_ref[jnp.maximum(i - 1, 0)])

    @pl.when(jnp.logical_and(new_expert, i < nb_ref[0]))
    def _():
        w1_sc[...] = w1_ref[0, 0].astype(BF16)
        w3_sc[...] = w3_ref[0, 0].astype(BF16)
        w2_sc[...] = w2_ref[0, 0].astype(BF16)

    @pl.when(i < nb_ref[0])
    def _():
        x = _unpack_bf16_halves(x_ref[...])
        a = jnp.dot(x, w1_sc[...], preferred_element_type=F32)
        b = jnp.dot(x, w3_sc[...], preferred_element_type=F32)
        hdn = (a * jax.nn.sigmoid(a) * b).astype(BF16)
        y_ref[...] = jnp.dot(hdn, w2_sc[...], preferred_element_type=F32)

    @pl.when(i >= nb_ref[0])
    def _():
        y_ref[...] = jnp.zeros(y_ref.shape, F32)


def expert_ffn(xs, blk_exp, n_active, w1, w3, w2, layer):
    n_slots = xs.shape[0]
    nb = n_slots // MOE_ROWS
    wmap = lambda i, be, na: (layer, be[i], 0, 0)
    xmap = lambda i, be, na: (jnp.minimum(i, na[0] - 1), 0)
    return pl.pallas_call(
        _ffn_body,
        grid_spec=pltpu.PrefetchScalarGridSpec(
            num_scalar_prefetch=2,
            grid=(nb,),
            in_specs=[pl.BlockSpec((MOE_ROWS, D_MODEL // 2), xmap),
                      pl.BlockSpec((1, 1, D_MODEL, E_HID), wmap),
                      pl.BlockSpec((1, 1, D_MODEL, E_HID), wmap),
                      pl.BlockSpec((1, 1, E_HID, D_MODEL), wmap)],
            out_specs=pl.BlockSpec((MOE_ROWS, D_MODEL), lambda i, be, na: (i, 0)),
            scratch_shapes=[pltpu.VMEM((D_MODEL, E_HID), BF16), pltpu.VMEM((D_MODEL, E_HID), BF16),
                            pltpu.VMEM((E_HID, D_MODEL), BF16)],
        ),
        out_shape=jax.ShapeDtypeStruct((n_slots, D_MODEL), F32),
        compiler_params=_params(1),
        name="expert_ffn",
    )(blk_exp, n_active, xs, w1, w3, w2)


def moe_dispatch(route, counts_row, n):
    e = route[:, ROUTE_E1:ROUTE_E2 + 1].astype(jnp.int32)
    rank = route[:, ROUTE_R1:ROUTE_R2 + 1].astype(jnp.int32)
    counts = counts_row[0, EXPERT_LANE0:EXPERT_LANE0 + N_EXPERTS].astype(jnp.int32)
    padded = ((counts + MOE_ROWS - 1) // MOE_ROWS) * MOE_ROWS
    pad_end = jnp.cumsum(padded)
    pad_start = pad_end - padded
    dest = pad_start[e] + rank
    nb = -(-(n * TOP_K_IN_GROUP) // MOE_ROWS) + N_EXPERTS
    n_slots = nb * MOE_ROWS
    tok = jnp.broadcast_to(jnp.arange(n, dtype=jnp.int32)[:, None], (n, TOP_K_IN_GROUP))
    slot_tok = jnp.zeros((n_slots,), jnp.int32).at[dest.reshape(-1)].set(
        tok.reshape(-1), unique_indices=True, mode='promise_in_bounds')
    blk_start = jnp.arange(nb, dtype=jnp.int32) * MOE_ROWS
    blk_exp = jnp.sum((pad_end[None, :] <= blk_start[:, None]).astype(jnp.int32), axis=1)
    blk_exp = jnp.minimum(blk_exp, N_EXPERTS - 1)
    n_active = (pad_end[-1] // MOE_ROWS).astype(jnp.int32).reshape(1)
    return slot_tok, dest, blk_exp, n_active


def moe_experts(hn, route, counts_row, w1, w3, w2, layer):
    n = hn.shape[0]
    slot_tok, dest, blk_exp, n_active = moe_dispatch(route, counts_row, n)
    xs = hn.at[slot_tok].get(mode='promise_in_bounds')
    ys = expert_ffn(xs, blk_exp, n_active, w1, w3, w2, layer)
    return [ys.at[dest[:, j]].get(mode='promise_in_bounds') for j in range(TOP_K_IN_GROUP)]


def _combine(h_ref, y1_ref, y2_ref, route_ref):
    lane = lax.broadcasted_iota(jnp.int32, (ROW_TILE, LANES), 1)
    r = route_ref[...]
    g1 = jnp.sum(jnp.where(lane == ROUTE_G1, r, 0.0), axis=-1, keepdims=True)
    g2 = jnp.sum(jnp.where(lane == ROUTE_G2, r, 0.0), axis=-1, keepdims=True)
    return h_ref[...] + (y1_ref[...] * g1 + y2_ref[...] * g2)


def _combine_proj_body(h_ref, y1_ref, y2_ref, route_ref, g_ref, w_ref, h2_ref, u_ref):
    h2 = _combine(h_ref, y1_ref, y2_ref, route_ref)
    h2_ref[...] = h2
    u_ref[...] = jnp.dot(_rms(h2, g_ref[...]).astype(BF16), w_ref[...], preferred_element_type=F32)


def _combine_norm_body(h_ref, y1_ref, y2_ref, route_ref, g_ref, yp_ref, ys_ref):
    y = _rms(_combine(h_ref, y1_ref, y2_ref, route_ref), g_ref[...])

    @pl.when(pl.program_id(0) < N_PROMPT_TILES)
    def _():
        yp_ref[...] = y

    @pl.when(pl.program_id(0) >= N_PROMPT_TILES)
    def _():
        ys_ref[...] = y


def combine_proj(h, ys, route, g, w):
    n = h.shape[0]
    nn = w.shape[1]
    return pl.pallas_call(
        _combine_proj_body,
        grid=(n // ROW_TILE,),
        in_specs=[_rows(D_MODEL), _rows(D_MODEL), _rows(D_MODEL), _rows(LANES), _full((1, D_MODEL)),
                  _full((D_MODEL, nn))],
        out_specs=[_rows(D_MODEL), _rows(nn)],
        out_shape=[jax.ShapeDtypeStruct((n, D_MODEL), F32), jax.ShapeDtypeStruct((n, nn), F32)],
        compiler_params=_params(1),
        name="combine_proj",
    )(h, *ys, route, g.reshape(1, D_MODEL), w.astype(BF16))


def combine_norm(h, ys, route, g):
    n = h.shape[0]
    return pl.pallas_call(
        _combine_norm_body,
        grid=(n // ROW_TILE,),
        in_specs=[_rows(D_MODEL), _rows(D_MODEL), _rows(D_MODEL), _rows(LANES), _full((1, D_MODEL))],
        out_specs=[_PROMPT_ROWS, _SAMPLE_ROWS],
        out_shape=[jax.ShapeDtypeStruct((N_PROMPT, D_MODEL), F32), jax.ShapeDtypeStruct((N_SAMPLE, D_MODEL), F32)],
        compiler_params=_params(1),
        name="combine_norm",
    )(h, *ys, route, g.reshape(1, D_MODEL))


def _even_first(x, axis):
    n = x.shape[axis]
    idx = jnp.concatenate([jnp.arange(0, n, 2), jnp.arange(1, n, 2)])
    return jnp.take(x, idx, axis=axis)


def _decode_query_cols(q_s, gates_s):
    B, T, G, R = DEC_BATCH, DEC_SEQ, NSA_KV, NSA_REP
    qg = q_s.reshape(B, T, G, R, HD).transpose(0, 2, 4, 3, 1).reshape(B, G, HD, R * T)
    qt = jnp.zeros((B, G, HD, G, R * T), BF16)
    for g in range(G):
        qt = qt.at[:, g, :, g, :].set(qg[:, g])
    qt = jnp.pad(qt.reshape(B, G * HD, G * R * T), ((0, 0), (0, 0), (0, LANES - G * R * T)))
    gr = gates_s[:, :3 * NSA_HEADS].reshape(B, T, NSA_HEADS, 3).transpose(0, 3, 2, 1).reshape(B, 3, NSA_HEADS * T)
    return qt, jnp.pad(gr, ((0, 0), (0, 0), (0, LANES - NSA_HEADS * T)))


def _feature_major(cache):
    lead, rows = cache.shape[:2]
    return jnp.transpose(cache, (0, 2, 3, 4, 1)).reshape(lead, NSA_KV_W, rows)


def _kv_rows_from_feature_major(xt, lead):
    x = xt.reshape(2, NSA_KV, HD, *lead)
    n = len(lead)
    return jnp.transpose(x, tuple(range(3, 3 + n)) + (0, 1, 2))


def mixer_a(h, p, past):
    pos = jnp.concatenate([jnp.tile(jnp.arange(SEQ, dtype=jnp.int32), BATCH),
                           PAST_LEN + jnp.tile(jnp.arange(DEC_SEQ, dtype=jnp.int32), DEC_BATCH)])
    (q, kvc, kvs, kvw, kvct, kvst, kvwt, ks, vst, kw, vwt, gates, gq, gk, gv, la, sgg) = inproj_a(
        *h, p['norm_mix'][0], p['a_w_in'][0], p['a_gla_wa2'][0], p['a_gla_ba'][0], pos)
    pe, wc, gnorm = p['a_cmp_pe'][0], p['a_cmp_w'][0], p['a_gla_norm'][0]
    P = N_PROMPT
    ckv = cmp_blocks_rows(kvc, P, pe, wc)
    ckv = _even_first(ckv.reshape(BATCH, N_CMP_PROMPT, NSA_KV_W), 1)
    o_nsa_p = nsa_prompt_attn(q, gates, ckv, ks, vst, kw, vwt)
    o_gla_p, st_p = gla_seq(gq, gk, la, gv, sgg, gnorm)
    win_p = jnp.concatenate([kvwt[:, (b + 1) * SEQ - WINDOW:(b + 1) * SEQ] for b in range(BATCH)], axis=1)
    new_p = (_kv_rows_from_feature_major(kvct[:, :P], (BATCH, SEQ)),
             _kv_rows_from_feature_major(kvst[:, :P], (BATCH, SEQ)),
             _kv_rows_from_feature_major(win_p, (BATCH, WINDOW)), _gla_state_from_pairs(st_p))
    o_gla_s, st_s = gla_step(gq, gk, la, gv, sgg, gnorm, _gla_state_to_pairs(past['state_gla'][0]), P)
    n_pool = past['cache_cmp_kv'].shape[1]
    ckv_pool = cmp_blocks_pages(_feature_major(past['cache_cmp_kv'][0]), pe, wc)
    ckv_seq = ckv_pool.reshape(n_pool, PAGE_SIZE // CMP_BLOCK, NSA_KV_W)[past['page_table']]
    ckv_seq = _even_first(ckv_seq.reshape(DEC_BATCH, N_CMP_DEC, NSA_KV_W), 1)
    qt, gate_rows = _decode_query_cols(q[P:], gates[P:])
    per_seq = lambda x: x[P:].reshape(DEC_BATCH, 1, DEC_SEQ * NSA_KV_W)
    o_t, win_new = nsa_decode_attn(qt, gate_rows, ckv_seq, _feature_major(past['cache_slc_kv'][0]),
                                   _feature_major(past['cache_win_kv'][0]), per_seq(kvs), per_seq(kvw),
                                   past['page_table'])
    o_nsa_s = o_t[:, :, :NSA_HEADS * DEC_SEQ].reshape(DEC_BATCH, HD, NSA_HEADS, DEC_SEQ)
    o_nsa_s = o_nsa_s.transpose(0, 3, 2, 1).reshape(N_SAMPLE, NSA_Q_W).astype(BF16)
    n_buf = win_new.shape[2]
    win_s = jnp.transpose(win_new.reshape(DEC_BATCH, 2, NSA_KV, HD, n_buf), (0, 4, 1, 2, 3))
    new_s = (_kv_rows_from_feature_major(kvct[:, P:], (DEC_BATCH, DEC_SEQ)),
             _kv_rows_from_feature_major(kvst[:, P:], (DEC_BATCH, DEC_SEQ)), win_s, _gla_state_from_pairs(st_s))
    o_nsa = jnp.concatenate([o_nsa_p, o_nsa_s], axis=0)
    o_gla = jnp.concatenate([o_gla_p, o_gla_s], axis=0)
    return o_nsa, o_gla, new_p, new_s


def run_trunk(x_prompt, x_sample, p, past):
    h0 = (x_prompt.reshape(N_PROMPT, D_MODEL), x_sample.reshape(N_SAMPLE, D_MODEL))
    o_nsa, o_gla, new_p, new_s = mixer_a(h0, p, past)
    h, hn, route, counts = post_mixer(h0, [o_nsa, o_gla], p['a_w_out'][0], p['norm_ffn'][0], p['m_w_rg'][0],
                                      p['m_b_rg'][0], p['m_w_re'][0], p['m_b_re'][0])
    ys = moe_experts(hn, route, counts, p['m_w1'], p['m_w3'], p['m_w2'], 0)
    h, u = combine_proj(h, ys, route, p['norm_mix'][1], p['c_w_in'][0])
    lru_w = (p['c_conv_w'][0], p['c_conv_b'][0], p['c_w_a'][0], p['c_b_a'][0], p['c_w_x'][0], p['c_b_x'][0],
             p['c_lam'][0])
    us = u[N_PROMPT:].reshape(DEC_BATCH, DEC_SEQ, 2 * D_RNN)
    y_p, lru_p = lru_seq(u, *lru_w)
    y_s, lru_s = lru_step(jnp.swapaxes(us, 0, 1), past['state_conv'][0], past['state_lru'][0], *lru_w)
    conv_p = jnp.stack([u[(b + 1) * SEQ - (CONV_W - 1):(b + 1) * SEQ, D_RNN:] for b in range(BATCH)])
    conv_s = us[:, DEC_SEQ - (CONV_W - 1):, D_RNN:]
    mix_in = jnp.concatenate([y_p, jnp.swapaxes(y_s, 0, 1).reshape(N_SAMPLE, D_RNN)], axis=0)
    h, hn, route, counts = post_mixer(h, [mix_in], p['c_w_out'][0], p['norm_ffn'][1], p['m_w_rg'][1],
                                      p['m_b_rg'][1], p['m_w_re'][1], p['m_b_re'][1])
    ys = moe_experts(hn, route, counts, p['m_w1'], p['m_w3'], p['m_w2'], 1)
    y_p, y_s = combine_norm(h, ys, route, p['norm_final'])
    y_prompt = y_p.reshape(BATCH, SEQ, D_MODEL)
    y_sample = y_s.reshape(DEC_BATCH, DEC_SEQ, D_MODEL)
    return (y_prompt, y_sample), new_p + (lru_p, conv_p), new_s + (lru_s, conv_s)


def kernel(x_prompt, x_sample, cache_cmp_kv, cache_slc_kv, cache_win_kv, state_gla, state_lru, state_conv,
           page_table, norm_mix, norm_ffn, norm_final, a_w_in, a_cmp_pe, a_cmp_w, a_gla_wa2, a_gla_ba,
           a_gla_norm, a_w_out, c_w_in, c_conv_w, c_conv_b, c_w_a, c_b_a, c_w_x, c_b_x, c_lam, c_w_out,
           m_w_rg, m_b_rg, m_w_re, m_b_re, m_w1, m_w3, m_w2):
    p = {'norm_mix': norm_mix, 'norm_ffn': norm_ffn, 'norm_final': norm_final,
         'a_w_in': a_w_in, 'a_cmp_pe': a_cmp_pe, 'a_cmp_w': a_cmp_w, 'a_gla_wa2': a_gla_wa2,
         'a_gla_ba': a_gla_ba, 'a_gla_norm': a_gla_norm, 'a_w_out': a_w_out,
         'c_w_in': c_w_in, 'c_conv_w': c_conv_w, 'c_conv_b': c_conv_b, 'c_w_a': c_w_a, 'c_b_a': c_b_a,
         'c_w_x': c_w_x, 'c_b_x': c_b_x, 'c_lam': c_lam, 'c_w_out': c_w_out,
         'm_w_rg': m_w_rg, 'm_b_rg': m_b_rg, 'm_w_re': m_w_re, 'm_b_re': m_b_re,
         'm_w1': m_w1, 'm_w3': m_w3, 'm_w2': m_w2}
    past = {'cache_cmp_kv': cache_cmp_kv, 'cache_slc_kv': cache_slc_kv, 'cache_win_kv': cache_win_kv,
            'state_gla': state_gla, 'state_lru': state_lru, 'state_conv': state_conv,
            'page_table': page_table}
    (y_p, y_s), sp, ss = run_trunk(x_prompt, x_sample, p, past)
    outs = [y_p, y_s]
    for a, b in zip(sp, ss):
        outs += [a[None], b[None]]
    return tuple(outs)
```

```python
import functools
import jax, jax.numpy as jnp
from jax import lax
import numpy as np
from jax.experimental import pallas as pl
from jax.experimental.pallas import tpu as pltpu

D_MODEL = 1024
BATCH = 2
SEQ = 8192
DEC_BATCH = 128
DEC_SEQ = 4
PAST_LEN = 2048
PAGE_SIZE = 128
EPS = 1e-6
NSA_HEADS = 8
NSA_KV = 2
NSA_REP = NSA_HEADS // NSA_KV
HD = 64
CMP_BLOCK = 32
SLC_BLOCK = 64
SLC_TOPK = 16
WINDOW = 512
Q_BLOCK = 128
ROPE_DIM = HD // 4
ROPE_THETA = 500000.0
GLA_HEADS = 4
GLA_DK = 64
GLA_DV = 128
GLA_LOWRANK = 16
GLA_TAU = 16.0
D_RNN = 1280
LRU_BLOCKS = 10
LRU_BW = D_RNN // LRU_BLOCKS
CONV_W = 4
LRU_C = 8.0
N_GROUPS = 4
EXP_PER_GROUP = 8
N_EXPERTS = N_GROUPS * EXP_PER_GROUP
E_HID = 512
TOP_K_IN_GROUP = 2
NSA_Q_W = NSA_HEADS * HD
NSA_KV_W = 2 * NSA_KV * HD
GLA_K_W = GLA_HEADS * GLA_DK
GLA_V_W = GLA_HEADS * GLA_DV
A_SIZES = (NSA_Q_W, NSA_KV_W, NSA_KV_W, NSA_KV_W, 3 * NSA_HEADS, GLA_K_W, GLA_K_W, GLA_V_W, GLA_LOWRANK, GLA_V_W)
N_PROMPT = BATCH * SEQ
N_SAMPLE = DEC_BATCH * DEC_SEQ
N_TOK = N_PROMPT + N_SAMPLE
N_PAGES = PAST_LEN // PAGE_SIZE

F32 = jnp.float32
BF16 = jnp.bfloat16
VMEM_LIMIT_BYTES = 56 * 1024 * 1024
LANES = 128
SUBLANES = 8
ROW_TILE = 512


def _params(n_axes):
    return pltpu.CompilerParams(dimension_semantics=("arbitrary",) * n_axes, vmem_limit_bytes=VMEM_LIMIT_BYTES)


def _full(shape):
    return pl.BlockSpec(shape, lambda *_: (0,) * len(shape))


def _rows(width):
    return pl.BlockSpec((ROW_TILE, width), lambda i: (i, 0))


def _rms(x, g):
    return x * lax.rsqrt(jnp.mean(x * x, axis=-1, keepdims=True) + EPS) * g


def _softplus(x):
    return jnp.maximum(x, 0.0) + jnp.log1p(jnp.exp(-jnp.abs(x)))


def _gelu_tanh(x):
    return x * (0.5 * (1.0 + jnp.tanh(0.7978845608028654 * (x + 0.044715 * (x * x * x)))))


def _nt_dot(a, b):
    return lax.dot_general(a, b, (((1,), (1,)), ((), ())), preferred_element_type=F32)


def _tn_dot(a, b):
    return lax.dot_general(a, b, (((0,), (0,)), ((), ())), preferred_element_type=F32)


A_Q0, A_KVC0, A_KVS0, A_KVW0 = 0, 512, 768, 1024
A_GQ0, A_GK0, A_GV0, A_GG0, A_MISC0 = 1280, 1536, 1792, 2304, 2816
A_COLS = A_MISC0 + LANES
MISC_LR0 = 3 * NSA_HEADS
KV_HALF = NSA_KV * HD


def _rope_lanes(x, cos_t, sin_lo, sin_hi):
    reps = x.shape[1] // LANES
    tile = (lambda t: jnp.concatenate([t] * reps, axis=1)) if reps > 1 else (lambda t: t)
    w = x.shape[1]
    half = ROPE_DIM // 2
    return x * tile(cos_t) + pltpu.roll(x, half, 1) * tile(sin_hi) + pltpu.roll(x, w - half, 1) * tile(sin_lo)


N_PROMPT_TILES = N_PROMPT // ROW_TILE
_PROMPT_ROWS = pl.BlockSpec((ROW_TILE, D_MODEL), lambda i: (jnp.minimum(i, N_PROMPT_TILES - 1), 0))
_SAMPLE_ROWS = pl.BlockSpec((ROW_TILE, D_MODEL), lambda i: (0, 0))


def _token_rows(prompt_ref, sample_ref):
    return jnp.where(pl.program_id(0) < N_PROMPT_TILES, prompt_ref[...], sample_ref[...])


def _inproj_a_body(hp_ref, hs_ref, g_ref, w_ref, wa2_ref, ba_ref, cos_ref, slo_ref, shi_ref,
                   q_ref, kvc_ref, kvs_ref, kvw_ref, kvct_p_ref, kvct_s_ref, kvst_p_ref, kvst_s_ref, kvwt_p_ref,
                   kvwt_s_ref, ks_ref, vst_ref, kw_ref, vwt_ref, gates_ref, gq_ref, gk_ref, gv_ref, la_ref,
                   sgg_ref):
    y = _rms(_token_rows(hp_ref, hs_ref), g_ref[...]).astype(BF16)
    proj = lambda a, b: jnp.dot(y, w_ref[:, a:b], preferred_element_type=F32)
    cos_t, sin_lo, sin_hi = cos_ref[...], slo_ref[...], shi_ref[...]
    q_ref[...] = (_rope_lanes(proj(A_Q0, A_KVC0), cos_t, sin_lo, sin_hi) * (HD ** -0.5)).astype(BF16)

    def store_feature_major(p_ref, s_ref, xt):
        @pl.when(pl.program_id(0) < N_PROMPT_TILES)
        def _():
            p_ref[0] = xt

        @pl.when(pl.program_id(0) >= N_PROMPT_TILES)
        def _():
            s_ref[...] = xt

    kvc = proj(A_KVC0, A_KVS0)
    kvc_ref[...] = kvc
    store_feature_major(kvct_p_ref, kvct_s_ref, kvc.T)
    for a0, kv_ref, kvt_p_ref, kvt_s_ref, k_ref, vt_ref in (
            (A_KVS0, kvs_ref, kvst_p_ref, kvst_s_ref, ks_ref, vst_ref),
            (A_KVW0, kvw_ref, kvwt_p_ref, kvwt_s_ref, kw_ref, vwt_ref)):
        kv = proj(a0, a0 + 2 * KV_HALF)
        k = _rope_lanes(kv[:, :KV_HALF], cos_t, sin_lo, sin_hi)
        vt = kv[:, KV_HALF:].T
        kv_ref[:, :KV_HALF] = k
        kv_ref[:, KV_HALF:] = kv[:, KV_HALF:]
        store_feature_major(kvt_p_ref, kvt_s_ref, jnp.concatenate([k.T, vt], axis=0))
        k_ref[...] = k.astype(BF16)
        vt_ref[...] = vt.astype(BF16)
    misc = proj(A_MISC0, A_COLS)
    gates_ref[...] = jax.nn.sigmoid(misc)
    z = jnp.dot(misc.astype(BF16), wa2_ref[...], preferred_element_type=F32) + ba_ref[...]
    la_ref[...] = -_softplus(-z) * (1.0 / GLA_TAU)
    gq_ref[...] = proj(A_GQ0, A_GK0) * (GLA_DK ** -0.5)
    gk_ref[...] = proj(A_GK0, A_GV0)
    gv_ref[...] = proj(A_GV0, A_GG0).astype(BF16)
    gg = proj(A_GG0, A_MISC0)
    sgg_ref[...] = gg * jax.nn.sigmoid(gg)


def _rope_tables(pos):
    pos = np.asarray(pos)
    half = ROPE_DIM // 2
    f4 = np.float32
    inv = (f4(1.0) / (f4(ROPE_THETA) ** (np.arange(0, ROPE_DIM, 2, dtype=f4) / f4(ROPE_DIM)))).astype(f4)
    ang = (pos.astype(f4)[:, None] * inv[None, :]).astype(np.float64)
    cos, sin = np.cos(ang).astype(f4), np.sin(ang).astype(f4)
    n = pos.shape[0]
    one = np.ones((n, HD - ROPE_DIM), f4)
    zero = np.zeros((n, HD - ROPE_DIM), f4)
    zh = np.zeros((n, half), f4)
    seg = lambda a, b, rest: jnp.asarray(np.concatenate([a, b, rest] * (LANES // HD), axis=1))
    return seg(cos, cos, one), seg(-sin, zh, zero), seg(zh, sin, zero)


def inproj_a(h_prompt, h_sample, g, w_in, wa2, ba, pos):
    n = h_prompt.shape[0] + h_sample.shape[0]
    zpad = jnp.zeros((D_MODEL, LANES - 3 * NSA_HEADS - GLA_LOWRANK), F32)
    o = np.cumsum((0,) + A_SIZES)
    w = jnp.concatenate([w_in[:, o[0]:o[4]], w_in[:, o[5]:o[8]], w_in[:, o[9]:o[10]],
                         w_in[:, o[4]:o[5]], w_in[:, o[8]:o[9]], zpad], axis=1).astype(BF16)
    wa2p = jnp.zeros((LANES, GLA_K_W), F32).at[MISC_LR0:MISC_LR0 + GLA_LOWRANK].set(wa2).astype(BF16)
    seq_tiles = SEQ // ROW_TILE
    pos_rows = pl.BlockSpec((ROW_TILE, LANES),
                            lambda i: (jnp.where(i < N_PROMPT_TILES, i % seq_tiles, seq_tiles), 0))
    cols = pl.BlockSpec((KV_HALF, ROW_TILE), lambda i: (0, i))
    tile = lambda i: jnp.minimum(i, N_PROMPT_TILES - 1)
    kvt_p = (pl.BlockSpec((1, NSA_KV_W, ROW_TILE), lambda i: (tile(i) // seq_tiles, 0, tile(i) % seq_tiles)),
             (BATCH, NSA_KV_W, SEQ), F32)
    kvt_s = (pl.BlockSpec((NSA_KV_W, ROW_TILE), lambda i: (0, 0)), (NSA_KV_W, N_SAMPLE), F32)
    outs = [(_rows(NSA_Q_W), (n, NSA_Q_W), BF16), (_rows(NSA_KV_W), (n, NSA_KV_W), F32),
            (_rows(NSA_KV_W), (n, NSA_KV_W), F32), (_rows(NSA_KV_W), (n, NSA_KV_W), F32),
            kvt_p, kvt_s, kvt_p, kvt_s, kvt_p, kvt_s,
            (_rows(KV_HALF), (n, KV_HALF), BF16), (cols, (KV_HALF, n), BF16),
            (_rows(KV_HALF), (n, KV_HALF), BF16), (cols, (KV_HALF, n), BF16),
            (_rows(LANES), (n, LANES), F32), (_rows(GLA_K_W), (n, GLA_K_W), F32),
            (_rows(GLA_K_W), (n, GLA_K_W), F32), (_rows(GLA_V_W), (n, GLA_V_W), BF16),
            (_rows(GLA_K_W), (n, GLA_K_W), F32), (_rows(GLA_V_W), (n, GLA_V_W), F32)]
    return pl.pallas_call(
        _inproj_a_body,
        grid=(n // ROW_TILE,),
        in_specs=[_PROMPT_ROWS, _SAMPLE_ROWS, _full((1, D_MODEL)), _full((D_MODEL, A_COLS)), _full((LANES, GLA_K_W)),
                  _full((1, GLA_K_W)), pos_rows, pos_rows, pos_rows],
        out_specs=[s for s, _, _ in outs],
        out_shape=[jax.ShapeDtypeStruct(shape, dt) for _, shape, dt in outs],
        compiler_params=_params(1),
        name="inproj_a",
    )(h_prompt, h_sample, g.reshape(1, D_MODEL), w, wa2p, ba.reshape(1, GLA_K_W), *_rope_tables(pos))


CMP_TILE_BLOCKS = 256
CMP_TILE_ROWS = CMP_TILE_BLOCKS * CMP_BLOCK
CMP_TILE_PAGES = CMP_TILE_ROWS // PAGE_SIZE


def _cmp_reduce(xk_ref, xv_ref, pe_ref, w_ref):
    acc = jnp.zeros((CMP_TILE_BLOCKS, NSA_KV_W), F32)
    for l in range(CMP_BLOCK):
        rows = pl.ds(l, CMP_TILE_BLOCKS, stride=CMP_BLOCK)
        xl = jnp.concatenate([xk_ref[rows, :], xv_ref[rows, :]], axis=1) + pe_ref[l:l + 1, :]
        acc = acc + jnp.dot(xl.astype(BF16), w_ref[l], preferred_element_type=F32)
    return acc


def _cmp_rows_body(xk_ref, xv_ref, pe_ref, w_ref, o_ref):
    o_ref[...] = _cmp_reduce(xk_ref, xv_ref, pe_ref, w_ref)


def _cmp_pages_body(x_ref, pe_ref, w_ref, o_ref, xk_sc, xv_sc):
    for pg in range(CMP_TILE_PAGES):
        rows = slice(pg * PAGE_SIZE, (pg + 1) * PAGE_SIZE)
        xk_sc[rows, :] = x_ref[pg, :KV_HALF, :].T
        xv_sc[rows, :] = x_ref[pg, KV_HALF:, :].T
    o_ref[...] = _cmp_reduce(xk_sc, xv_sc, pe_ref, w_ref)


def _cmp_weights(pe, w_cmp):
    pe_rows = jnp.broadcast_to(pe[:, :, None, :], (CMP_BLOCK, 2, NSA_KV, HD)).reshape(CMP_BLOCK, NSA_KV_W)
    w = w_cmp.astype(BF16)
    zero = jnp.zeros((CMP_BLOCK, HD, HD), BF16)
    diag = [w[:, c] for c in range(2) for _ in range(NSA_KV)]
    w_bd = jnp.concatenate([jnp.concatenate([blk if j == i else zero for j in range(len(diag))], axis=2)
                            for i, blk in enumerate(diag)], axis=1)
    return pe_rows, w_bd


def cmp_blocks_rows(x, n_rows, cmp_weights):
    return pl.pallas_call(
        _cmp_rows_body,
        grid=(n_rows // CMP_TILE_ROWS,),
        in_specs=[pl.BlockSpec((CMP_TILE_ROWS, KV_HALF), lambda i: (i, 0)),
                  pl.BlockSpec((CMP_TILE_ROWS, KV_HALF), lambda i: (i, 1)), _full((CMP_BLOCK, NSA_KV_W)),
                  _full((CMP_BLOCK, NSA_KV_W, NSA_KV_W))],
        out_specs=pl.BlockSpec((CMP_TILE_BLOCKS, NSA_KV_W), lambda i: (i, 0)),
        out_shape=jax.ShapeDtypeStruct((n_rows // CMP_BLOCK, NSA_KV_W), F32),
        compiler_params=_params(1),
        name="cmp_blocks_rows",
    )(x, x, *cmp_weights)


def cmp_blocks_pages(xt, cmp_weights):
    n_pages = xt.shape[0]
    return pl.pallas_call(
        _cmp_pages_body,
        grid=(n_pages // CMP_TILE_PAGES,),
        in_specs=[pl.BlockSpec((CMP_TILE_PAGES, NSA_KV_W, PAGE_SIZE), lambda i: (i, 0, 0)),
                  _full((CMP_BLOCK, NSA_KV_W)), _full((CMP_BLOCK, NSA_KV_W, NSA_KV_W))],
        out_specs=pl.BlockSpec((CMP_TILE_BLOCKS, NSA_KV_W), lambda i: (i, 0)),
        out_shape=jax.ShapeDtypeStruct((n_pages * PAGE_SIZE // CMP_BLOCK, NSA_KV_W), F32),
        scratch_shapes=[pltpu.VMEM((CMP_TILE_ROWS, KV_HALF), F32), pltpu.VMEM((CMP_TILE_ROWS, KV_HALF), F32)],
        compiler_params=_params(1),
        name="cmp_blocks_pages",
    )(xt, *cmp_weights)


KEY_TILE = 256
NEG_BIG = -1e30
N_CMP_PROMPT = SEQ // CMP_BLOCK
N_SLC_PROMPT = SEQ // SLC_BLOCK
SLC_SHIFT = SLC_BLOCK.bit_length() - 1
N_CMP_DEC = PAST_LEN // CMP_BLOCK
N_SLC_DEC = -(-(PAST_LEN + DEC_SEQ) // SLC_BLOCK)
N_SLC_DEC_PAD = -(-N_SLC_DEC // LANES) * LANES
DEC_COLS_PER_GROUP = NSA_REP * DEC_SEQ


def _tile_cols(x, reps):
    return jnp.concatenate([x] * reps, axis=1) if reps > 1 else x


def _even_first_cmp_end(n_cmp):
    j = lax.broadcasted_iota(jnp.int32, (n_cmp, 1), 0)
    blk = jnp.where(j < n_cmp // 2, 2 * j, 2 * (j - n_cmp // 2) + 1)
    return (blk + 1) * CMP_BLOCK - 1


def _softmax_cols(s, mask):
    s = jnp.where(mask, s, -jnp.inf)
    m = jnp.max(s, axis=0, keepdims=True)
    m = jnp.where(m > -jnp.inf, m, 0.0)
    e = jnp.where(mask, jnp.exp(s - m), 0.0)
    return e / jnp.maximum(jnp.sum(e, axis=0, keepdims=True), 1e-30)


def _softmax_cols_biased(s):
    m = jnp.max(s, axis=0, keepdims=True)
    e = jnp.exp(s - m)
    scale = jnp.where(m > 0.5 * NEG_BIG, 1.0 / jnp.sum(e, axis=0, keepdims=True), 0.0)
    return e * scale


N_FORCED = 3


def _select_cols(p_slc, qpos, n_top):
    ns = p_slc.shape[0]
    blk = lax.broadcasted_iota(jnp.int32, p_slc.shape, 0)
    cur = qpos >> SLC_SHIFT
    forced = ((blk == 0) | (blk == cur) | (blk == cur - 1)) & (blk <= cur)
    score = jnp.where((blk <= cur) & jnp.logical_not(forced), p_slc, -jnp.inf)
    sel = jnp.where(forced, 1.0, 0.0)
    for _ in range(n_top - N_FORCED):
        m = jnp.max(score, axis=0, keepdims=True)
        idx = jnp.min(jnp.where(score == m, blk, ns), axis=0, keepdims=True)
        hit = blk == idx
        sel = jnp.where(hit & (m > -jnp.inf), 1.0, sel)
        score = jnp.where(hit, -jnp.inf, score)
    return sel


def _flash_init(m_sc, l_sc, acc_sc):
    m_sc[...] = jnp.full(m_sc.shape, NEG_BIG, F32)
    l_sc[...] = jnp.zeros(l_sc.shape, F32)
    acc_sc[...] = jnp.zeros(acc_sc.shape, F32)


def _flash_cols(scores, mask, pv, m_sc, l_sc, acc_sc):
    s = jnp.where(mask, scores, NEG_BIG)
    m_old = m_sc[...]
    m_new = jnp.maximum(m_old, jnp.max(s, axis=0, keepdims=True))
    alpha = jnp.exp(m_old - m_new)
    p = jnp.where(mask, jnp.exp(s - m_new), 0.0)
    l_sc[...] = alpha * l_sc[...] + jnp.sum(p, axis=0, keepdims=True)
    acc_sc[...] = alpha * acc_sc[...] + pv(p.astype(BF16))
    m_sc[...] = m_new


def _flash_stream(score_fn, pv_fn, first, lo, n, stream_sc, m_sc, l_sc, acc_sc):
    (sa, ca), (sb, cb) = stream_sc

    def issue(s_ref, c_ref, kt, self_tile=False):
        s = score_fn(kt, self_tile)
        s_ref[...] = s
        c_ref[...] = jnp.max(s, axis=0, keepdims=True)

    def consume(s_ref, c_ref, kt):
        m_old = m_sc[...]
        m_new = jnp.maximum(m_old, c_ref[...])
        alpha = jnp.exp(m_old - m_new)
        p = jnp.exp(s_ref[...] - m_new)
        l_sc[...] = alpha * l_sc[...] + jnp.sum(p, axis=0, keepdims=True)
        acc_sc[...] = alpha * acc_sc[...] + pv_fn(kt, p.astype(BF16))
        m_sc[...] = m_new

    _flash_init(m_sc, l_sc, acc_sc)
    issue(sa, ca, first, True)

    def two_tiles(jj, kt_a):
        t0 = lo + 2 * jj
        issue(sb, cb, t0)
        consume(sa, ca, kt_a)
        issue(sa, ca, t0 + 1)
        consume(sb, cb, t0)
        return t0 + 1

    kt_a = lax.fori_loop(0, n // 2, two_tiles, first)

    @pl.when(n % 2 == 1)
    def _():
        issue(sb, cb, lo + n - 1)
        consume(sa, ca, kt_a)
        consume(sb, cb, lo + n - 1)

    @pl.when(n % 2 == 0)
    def _():
        consume(sa, ca, kt_a)

    return acc_sc[...] / jnp.maximum(l_sc[...], 1e-30)


def _flash_out(l_sc, acc_sc):
    return acc_sc[...] / jnp.maximum(l_sc[...], 1e-30)


def _pv_split(vt, p):
    c2 = p.shape[1] // 2
    return jnp.concatenate([jnp.dot(vt[:HD], p[:, :c2], preferred_element_type=F32),
                            jnp.dot(vt[HD:], p[:, c2:], preferred_element_type=F32)], axis=1)


def _nsa_prompt_body(q_ref, gates_ref, ckv_ref, cos_ref, slo_ref, shi_ref, ks_ref, vst_ref, kw_ref, vwt_ref,
                     o_ref, m_sc, l_sc, acc_sc, sel_bias_sc, sa_sc, ca_sc, sb_sc, cb_sc):
    stream_sc = ((sa_sc, ca_sc), (sb_sc, cb_sc))
    i = pl.program_id(1)
    nq = Q_BLOCK
    cols = NSA_HEADS * nq
    qpos = i * nq + lax.broadcasted_iota(jnp.int32, (1, nq), 1)
    q = q_ref[...].astype(F32)
    pairs = [q[:, j * LANES:(j + 1) * LANES].T for j in range(NSA_HEADS // 2)]
    zero = jnp.zeros((HD, cols // 2), F32)
    qt_g = [jnp.concatenate([pairs[2 * g][:HD], pairs[2 * g][HD:], pairs[2 * g + 1][:HD], pairs[2 * g + 1][HD:]],
                            axis=1) for g in range(NSA_KV)]
    qt = jnp.concatenate([jnp.concatenate([qt_g[0], zero], axis=1),
                          jnp.concatenate([zero, qt_g[1]], axis=1)], axis=0).astype(BF16)
    ckv = ckv_ref[0]
    ck = _rope_lanes(ckv[:, :KV_HALF], cos_ref[...], slo_ref[...], shi_ref[...]).astype(BF16)
    cvt = ckv[:, KV_HALF:].T.astype(BF16)
    c_bias = jnp.where(_even_first_cmp_end(N_CMP_PROMPT) <= qpos, 0.0, NEG_BIG)
    p = _softmax_cols_biased(jnp.dot(ck, qt, preferred_element_type=F32) + _tile_cols(c_bias, NSA_HEADS))
    o_c = _pv_split(cvt, p.astype(BF16))
    sel = []
    for g in range(NSA_KV):
        c0 = g * NSA_REP * nq
        p_grp = p[:, c0:c0 + nq]
        for r in range(1, NSA_REP):
            p_grp = p_grp + p[:, c0 + r * nq:c0 + (r + 1) * nq]
        p_slc = p_grp[:N_CMP_PROMPT // 2] + p_grp[N_CMP_PROMPT // 2:]
        sel.append(_select_cols(p_slc, qpos, SLC_TOPK))
    sel = jnp.concatenate(sel, axis=1)
    sel_bias_sc[...] = jnp.where(sel > 0.5, 0.0, NEG_BIG)
    key_row = lax.broadcasted_iota(jnp.int32, (KEY_TILE, 1), 0)
    blocks_per_tile = KEY_TILE // SLC_BLOCK
    kt_self = i // (KEY_TILE // nq)

    def slc_scores(kt, self_tile):
        k0 = pl.multiple_of(kt * KEY_TILE, KEY_TILE)
        rows = []
        for j in range(blocks_per_tile):
            b = sel_bias_sc[pl.ds(kt * blocks_per_tile + j, 1), :]
            rows.append(jnp.concatenate(
                [jnp.broadcast_to(b[:, g * nq:(g + 1) * nq], (SLC_BLOCK, nq)) for g in range(NSA_KV)
                 for _ in range(NSA_REP)], axis=1))
        bias = jnp.concatenate(rows, axis=0)
        if self_tile:
            bias = bias + _tile_cols(jnp.where(k0 + key_row <= qpos, 0.0, NEG_BIG), NSA_HEADS)
        return jnp.dot(ks_ref[pl.ds(k0, KEY_TILE), :], qt, preferred_element_type=F32) + bias

    def slc_pv(kt, pb):
        return _pv_split(vst_ref[:, pl.ds(pl.multiple_of(kt * KEY_TILE, KEY_TILE), KEY_TILE)], pb)

    o_s = _flash_stream(slc_scores, slc_pv, kt_self, 0, kt_self, stream_sc, m_sc, l_sc, acc_sc)

    def win_scores(kt, self_tile):
        k0 = pl.multiple_of(kt * KEY_TILE, KEY_TILE)
        d = qpos - (k0 + key_row)
        bias = jnp.where((d >= 0) & (d < WINDOW), 0.0, NEG_BIG)
        return jnp.dot(kw_ref[pl.ds(k0, KEY_TILE), :], qt, preferred_element_type=F32) + _tile_cols(bias, NSA_HEADS)

    def win_pv(kt, pb):
        return _pv_split(vwt_ref[:, pl.ds(pl.multiple_of(kt * KEY_TILE, KEY_TILE), KEY_TILE)], pb)

    win_lo = jnp.maximum(i - WINDOW // nq, 0) // (KEY_TILE // nq)
    o_w = _flash_stream(win_scores, win_pv, kt_self, win_lo, kt_self - win_lo, stream_sc, m_sc, l_sc, acc_sc)
    gates_t = gates_ref[...].T
    merged = []
    for hd in range(NSA_HEADS):
        cs = slice(hd * nq, (hd + 1) * nq)
        gate = lambda br: gates_t[3 * hd + br:3 * hd + br + 1]
        merged.append(gate(0) * o_c[:, cs] + gate(1) * o_s[:, cs] + gate(2) * o_w[:, cs])
    for j in range(NSA_HEADS // 2):
        pair = jnp.concatenate([merged[2 * j], merged[2 * j + 1]], axis=0).T
        o_ref[:, j * LANES:(j + 1) * LANES] = pair.astype(BF16)


def nsa_prompt_attn(q, gates, ckv, ks, vst, kw, vwt):
    nqb = SEQ // Q_BLOCK
    tok = lambda wd: pl.BlockSpec((Q_BLOCK, wd), lambda b, i: (b * nqb + i, 0))
    seq_rows = pl.BlockSpec((SEQ, KV_HALF), lambda b, i: (b, 0))
    seq_cols = pl.BlockSpec((KV_HALF, SEQ), lambda b, i: (0, b))
    n_cmp = N_CMP_PROMPT
    c_blk = np.concatenate([np.arange(0, n_cmp, 2), np.arange(1, n_cmp, 2)])
    cols = NSA_HEADS * Q_BLOCK
    return pl.pallas_call(
        _nsa_prompt_body,
        grid=(BATCH, nqb),
        in_specs=[tok(NSA_Q_W), tok(LANES), pl.BlockSpec((1, n_cmp, NSA_KV_W), lambda b, i: (b, 0, 0)),
                  _full((n_cmp, LANES)), _full((n_cmp, LANES)), _full((n_cmp, LANES)),
                  seq_rows, seq_cols, seq_rows, seq_cols],
        out_specs=tok(NSA_Q_W),
        out_shape=jax.ShapeDtypeStruct((N_PROMPT, NSA_Q_W), BF16),
        scratch_shapes=[pltpu.VMEM((1, cols), F32), pltpu.VMEM((1, cols), F32), pltpu.VMEM((HD, cols), F32),
                        pltpu.VMEM((N_SLC_PROMPT, NSA_KV * Q_BLOCK), F32),
                        pltpu.VMEM((KEY_TILE, cols), F32), pltpu.VMEM((1, cols), F32),
                        pltpu.VMEM((KEY_TILE, cols), F32), pltpu.VMEM((1, cols), F32)],
        compiler_params=_params(2),
        name="nsa_prompt",
    )(q, gates, ckv, *_rope_tables((c_blk + 1) * CMP_BLOCK - 1), ks, vst, kw, vwt)


def _nsa_decode_body(pt_ref, qt_ref, gate_ref, ckv_ref, cos_ref, slo_ref, shi_ref, hsum_ref, *refs):
    page_refs = refs[:N_PAGES]
    win_ref, kvs_new_ref, kvw_new_ref, o_ref, win_out_ref, m_sc, l_sc, acc_sc = refs[N_PAGES:]
    qt = qt_ref[0]
    lane = lax.broadcasted_iota(jnp.int32, (1, LANES), 1)
    qpos = PAST_LEN + (lane & (DEC_SEQ - 1))
    group0 = lane < DEC_COLS_PER_GROUP
    ckv = ckv_ref[0]
    ck = _rope_lanes(ckv[:, :KV_HALF], cos_ref[...], slo_ref[...], shi_ref[...]).astype(BF16)
    c_mask = _even_first_cmp_end(N_CMP_DEC) <= qpos
    p = _softmax_cols(jnp.dot(ck, qt, preferred_element_type=F32), c_mask)
    o_c = _tn_dot(ckv[:, KV_HALF:].astype(BF16), p.astype(BF16))
    p_grp = jnp.dot(p, hsum_ref[...], preferred_element_type=F32, precision=lax.Precision.HIGHEST)
    p_slc = jnp.concatenate([p_grp[:N_CMP_DEC // 2] + p_grp[N_CMP_DEC // 2:],
                             jnp.zeros((N_SLC_DEC_PAD - N_CMP_DEC // 2, LANES), F32)], axis=0)
    sel = _select_cols(p_slc, qpos, SLC_TOPK)

    def new_rows(ref):
        row = ref[0]
        kv = jnp.concatenate([row[:, t * NSA_KV_W:(t + 1) * NSA_KV_W] for t in range(DEC_SEQ)], axis=0)
        return jnp.concatenate([kv, jnp.zeros((SUBLANES - DEC_SEQ, NSA_KV_W), F32)], axis=0)

    new_row = lax.broadcasted_iota(jnp.int32, (SUBLANES, 1), 0)
    new_pos = PAST_LEN + new_row
    new_valid = new_row < DEC_SEQ
    _flash_init(m_sc, l_sc, acc_sc)
    kt_old = jnp.concatenate([r[0, :KV_HALF, :] for r in page_refs], axis=1).astype(BF16)
    vt_old = jnp.concatenate([r[0, KV_HALF:, :] for r in page_refs], axis=1).astype(BF16)
    key_blk = lax.broadcasted_iota(jnp.int32, (PAST_LEN, N_SLC_DEC_PAD), 0) >> SLC_SHIFT
    blk_col = lax.broadcasted_iota(jnp.int32, (PAST_LEN, N_SLC_DEC_PAD), 1)
    chosen = jnp.dot(jnp.where(key_blk == blk_col, 1.0, 0.0).astype(BF16), sel.astype(BF16),
                     preferred_element_type=F32) > 0.5
    old_pos = lax.broadcasted_iota(jnp.int32, (PAST_LEN, 1), 0)
    _flash_cols(_tn_dot(kt_old, qt), chosen & (old_pos <= qpos),
                lambda pb: jnp.dot(vt_old, pb, preferred_element_type=F32), m_sc, l_sc, acc_sc)
    kv_new = new_rows(kvs_new_ref)
    sel_new = sel[(PAST_LEN >> SLC_SHIFT):(PAST_LEN >> SLC_SHIFT) + 1] > 0.5
    v_new = kv_new[:, KV_HALF:].astype(BF16)
    _flash_cols(jnp.dot(kv_new[:, :KV_HALF].astype(BF16), qt, preferred_element_type=F32),
                sel_new & new_valid & (new_pos <= qpos), lambda pb: _tn_dot(v_new, pb), m_sc, l_sc, acc_sc)
    o_s = _flash_out(l_sc, acc_sc)
    _flash_init(m_sc, l_sc, acc_sc)
    n_buf = win_ref.shape[2]
    win = win_ref[0]
    d = qpos - (PAST_LEN - n_buf + lax.broadcasted_iota(jnp.int32, (n_buf, 1), 0))
    vt_win = win[KV_HALF:].astype(BF16)
    _flash_cols(_tn_dot(win[:KV_HALF].astype(BF16), qt), (d >= 0) & (d < WINDOW),
                lambda pb: jnp.dot(vt_win, pb, preferred_element_type=F32), m_sc, l_sc, acc_sc)
    kw_new = new_rows(kvw_new_ref)
    d = qpos - new_pos
    vw_new = kw_new[:, KV_HALF:].astype(BF16)
    _flash_cols(jnp.dot(kw_new[:, :KV_HALF].astype(BF16), qt, preferred_element_type=F32),
                new_valid & (d >= 0) & (d < WINDOW), lambda pb: _tn_dot(vw_new, pb), m_sc, l_sc, acc_sc)
    o_w = _flash_out(l_sc, acc_sc)
    g = gate_ref[0]
    o = g[0:1] * o_c + g[1:2] * o_s + g[2:3] * o_w
    o_ref[0] = jnp.where(group0, o[:HD], o[HD:])
    key = lax.broadcasted_iota(jnp.int32, (SUBLANES, n_buf), 1)
    place = jnp.where((key == n_buf - DEC_SEQ + new_row) & new_valid, 1.0, 0.0)
    placed = lax.dot_general(kw_new, place, (((0,), (0,)), ((), ())), preferred_element_type=F32,
                             precision=lax.Precision.HIGHEST)
    keep = lax.broadcasted_iota(jnp.int32, (1, n_buf), 1) < n_buf - DEC_SEQ
    win_out_ref[0] = jnp.where(keep, pltpu.roll(win, n_buf - DEC_SEQ, 1), placed)


def nsa_decode_attn(qt, gate_rows, ckv, slc_pool, win_buf, kvs_new, kvw_new, page_table):
    n_buf = win_buf.shape[2]
    per_b = lambda *shape: pl.BlockSpec((1,) + shape, lambda b, pt: (b,) + (0,) * len(shape))
    const = lambda *shape: pl.BlockSpec(shape, lambda b, pt: (0,) * len(shape))
    page = lambda j: pl.BlockSpec((1, NSA_KV_W, PAGE_SIZE), lambda b, pt: (pt[b, j], 0, 0))
    c_blk = np.concatenate([np.arange(0, N_CMP_DEC, 2), np.arange(1, N_CMP_DEC, 2)])
    col = jnp.arange(LANES)
    used = col < NSA_KV * DEC_COLS_PER_GROUP
    same = (col[:, None] // DEC_COLS_PER_GROUP == col[None, :] // DEC_COLS_PER_GROUP) & \
           (col[:, None] % DEC_SEQ == col[None, :] % DEC_SEQ) & used[:, None] & used[None, :]
    return pl.pallas_call(
        _nsa_decode_body,
        grid_spec=pltpu.PrefetchScalarGridSpec(
            num_scalar_prefetch=1,
            grid=(DEC_BATCH,),
            in_specs=[per_b(LANES, LANES), per_b(3, LANES), per_b(N_CMP_DEC, NSA_KV_W),
                      const(N_CMP_DEC, LANES), const(N_CMP_DEC, LANES), const(N_CMP_DEC, LANES),
                      const(LANES, LANES)] + [page(j) for j in range(N_PAGES)]
                     + [per_b(NSA_KV_W, n_buf), per_b(1, DEC_SEQ * NSA_KV_W), per_b(1, DEC_SEQ * NSA_KV_W)],
            out_specs=[per_b(HD, LANES), per_b(NSA_KV_W, n_buf)],
            scratch_shapes=[pltpu.VMEM((1, LANES), F32), pltpu.VMEM((1, LANES), F32),
                            pltpu.VMEM((LANES, LANES), F32)],
        ),
        out_shape=[jax.ShapeDtypeStruct((DEC_BATCH, HD, LANES), F32),
                   jax.ShapeDtypeStruct((DEC_BATCH, NSA_KV_W, n_buf), F32)],
        compiler_params=_params(1),
        name="nsa_decode",
    )(page_table, qt, gate_rows, ckv, *_rope_tables((c_blk + 1) * CMP_BLOCK - 1), same.astype(F32),
      *([slc_pool] * N_PAGES), win_buf, kvs_new, kvw_new)


GLA_PAIRS = GLA_HEADS // 2
GLA_ROWS = 128
GLA_SUB = 16
GLA_STEP_SEQS = 8


def _gla_rows(q, k, la, v, sgg, gnorm, st_ref, sub):
    R = q.shape[0]
    row = lax.broadcasted_iota(jnp.int32, (R, GLA_K_W), 0)
    rin = row % sub
    cum = la
    d = 1
    while d < sub:
        cum = cum + jnp.where(rin >= d, pltpu.roll(cum, d, 0), 0.0)
        d *= 2
    lane = lax.broadcasted_iota(jnp.int32, (sub, LANES), 1)
    lo = lane < GLA_DK
    rsub = lax.broadcasted_iota(jnp.int32, (sub, LANES), 0)
    out_rows = []
    for c in range(R // sub):
        rs = slice(c * sub, (c + 1) * sub)
        cum_c = cum[rs]
        last = cum_c[sub - 1:sub]
        qe = q[rs] * jnp.exp(cum_c)
        kdec = k[rs] * jnp.exp(last - cum_c)
        v_c = v[rs]
        heads = []
        for pr in range(GLA_PAIRS):
            ls = slice(pr * LANES, (pr + 1) * LANES)
            st = st_ref[pr]
            st_b = st.astype(BF16)
            qe_p, kd_p, q_p, k_p, cum_p = qe[:, ls], kdec[:, ls], q[rs, ls], k[rs, ls], cum_c[:, ls]
            v_pair = [v_c[:, (2 * pr + hh) * GLA_DV:(2 * pr + hh + 1) * GLA_DV] for hh in range(2)]
            upd = jnp.zeros((GLA_DV, LANES), F32)
            o_pair = []
            for hh in range(2):
                keep = lo if hh == 0 else jnp.logical_not(lo)
                o_pair.append(_nt_dot(jnp.where(keep, qe_p, 0.0).astype(BF16), st_b))
                upd = upd + _tn_dot(v_pair[hh].astype(BF16), jnp.where(keep, kd_p, 0.0).astype(BF16))
            for j in range(sub):
                dj = jnp.where(rsub >= j, jnp.exp(cum_p - cum_p[j:j + 1]), 0.0)
                w = q_p * k_p[j:j + 1] * dj
                a_lo = jnp.sum(jnp.where(lo, w, 0.0), axis=-1, keepdims=True)
                a_hi = jnp.sum(jnp.where(lo, 0.0, w), axis=-1, keepdims=True)
                o_pair[0] = o_pair[0] + a_lo * v_pair[0][j:j + 1]
                o_pair[1] = o_pair[1] + a_hi * v_pair[1][j:j + 1]
            st_ref[pr] = st * jnp.exp(last[:, ls]) + upd
            heads += o_pair
        out_rows.append(jnp.concatenate([_rms(x, gnorm) for x in heads], axis=1))
    return jnp.concatenate(out_rows, axis=0) * sgg


def _gla_seq_body(q_ref, k_ref, la_ref, v_ref, sgg_ref, gn_ref, o_ref, st_out_ref, st_sc):
    @pl.when(pl.program_id(1) == 0)
    def _():
        st_sc[...] = jnp.zeros(st_sc.shape, F32)

    o = _gla_rows(q_ref[...], k_ref[...], la_ref[...], v_ref[...].astype(F32), sgg_ref[...], gn_ref[...],
                  st_sc, GLA_SUB)
    o_ref[...] = o.astype(BF16)
    st_out_ref[0] = st_sc[...]


def gla_seq(q, k, la, v, sgg, gnorm):
    nt = SEQ // GLA_ROWS
    rows = lambda wd: pl.BlockSpec((GLA_ROWS, wd), lambda b, t: (b * nt + t, 0))
    return pl.pallas_call(
        _gla_seq_body,
        grid=(BATCH, nt),
        in_specs=[rows(GLA_K_W), rows(GLA_K_W), rows(GLA_K_W), rows(GLA_V_W), rows(GLA_V_W),
                  _full((1, GLA_DV))],
        out_specs=[rows(GLA_V_W), pl.BlockSpec((1, GLA_PAIRS, GLA_DV, LANES), lambda b, t: (b, 0, 0, 0))],
        out_shape=[jax.ShapeDtypeStruct((N_PROMPT, GLA_V_W), BF16),
                   jax.ShapeDtypeStruct((BATCH, GLA_PAIRS, GLA_DV, LANES), F32)],
        scratch_shapes=[pltpu.VMEM((GLA_PAIRS, GLA_DV, LANES), F32)],
        compiler_params=_params(2),
        name="gla_seq",
    )(q, k, la, v, sgg, gnorm.reshape(1, GLA_DV))


def _gla_step_body(q_ref, k_ref, la_ref, v_ref, sgg_ref, gn_ref, st_in_ref, o_ref, st_out_ref):
    st_out_ref[...] = st_in_ref[...]
    q, k, la, v, sgg = q_ref[...], k_ref[...], la_ref[...], v_ref[...].astype(F32), sgg_ref[...]
    for j in range(GLA_STEP_SEQS):
        rs = slice(j * DEC_SEQ, (j + 1) * DEC_SEQ)
        o = _gla_rows(q[rs], k[rs], la[rs], v[rs], sgg[rs], gn_ref[...], st_out_ref.at[j], DEC_SEQ)
        o_ref[rs, :] = o.astype(BF16)


def gla_step(q, k, la, v, sgg, gnorm, st_in, row0):
    rows_per = GLA_STEP_SEQS * DEC_SEQ
    blk0 = row0 // rows_per
    rows = lambda wd: pl.BlockSpec((rows_per, wd), lambda i: (blk0 + i, 0))
    st_spec = pl.BlockSpec((GLA_STEP_SEQS, GLA_PAIRS, GLA_DV, LANES), lambda i: (i, 0, 0, 0))
    return pl.pallas_call(
        _gla_step_body,
        grid=(DEC_BATCH // GLA_STEP_SEQS,),
        in_specs=[rows(GLA_K_W), rows(GLA_K_W), rows(GLA_K_W), rows(GLA_V_W), rows(GLA_V_W),
                  _full((1, GLA_DV)), st_spec],
        out_specs=[pl.BlockSpec((rows_per, GLA_V_W), lambda i: (i, 0)), st_spec],
        out_shape=[jax.ShapeDtypeStruct((N_SAMPLE, GLA_V_W), BF16),
                   jax.ShapeDtypeStruct((DEC_BATCH, GLA_PAIRS, GLA_DV, LANES), F32)],
        compiler_params=_params(1),
        name="gla_step",
    )(q, k, la, v, sgg, gnorm.reshape(1, GLA_DV), st_in)


def _gla_state_to_pairs(s):
    B = s.shape[0]
    return s.reshape(B, GLA_PAIRS, 2, GLA_DK, GLA_DV).transpose(0, 1, 4, 2, 3).reshape(B, GLA_PAIRS, GLA_DV, LANES)


def _gla_state_from_pairs(st):
    B = st.shape[0]
    return st.reshape(B, GLA_PAIRS, GLA_DV, 2, GLA_DK).transpose(0, 1, 3, 4, 2).reshape(B, GLA_HEADS, GLA_DK, GLA_DV)


LRU_TIME_TILE = 256


def _lru_gates(xc, wa_ref, ba, wx_ref, bx, lam):
    xcb = xc.astype(BF16)
    r_parts, i_parts = [], []
    for n in range(LRU_BLOCKS):
        xs = xcb[:, n * LRU_BW:(n + 1) * LRU_BW]
        r_parts.append(jnp.dot(xs, wa_ref[n], preferred_element_type=F32))
        i_parts.append(jnp.dot(xs, wx_ref[n], preferred_element_type=F32))
    r = jax.nn.sigmoid(jnp.concatenate(r_parts, axis=-1) + ba)
    i = jax.nn.sigmoid(jnp.concatenate(i_parts, axis=-1) + bx)
    log_a = -LRU_C * r * _softplus(-lam)
    a = jnp.exp(log_a)
    u = jnp.sqrt(1.0 - a * a) * (i * xc)
    return a, u


def _lru_seq_body(u_ref, cw_ref, cb_ref, wa_ref, ba_ref, wx_ref, bx_ref, lam_ref, y_ref, hT_ref, xp_sc, h_sc):
    tt = LRU_TIME_TILE

    @pl.when(pl.program_id(1) == 0)
    def _():
        xp_sc[0:SUBLANES, :] = jnp.zeros((SUBLANES, D_RNN), F32)
        h_sc[...] = jnp.zeros((1, D_RNN), F32)

    xp_sc[SUBLANES:SUBLANES + tt, :] = u_ref[:, D_RNN:]
    xc = cb_ref[...]
    for w in range(CONV_W):
        off = SUBLANES - (CONV_W - 1) + w
        xc = xc + cw_ref[w:w + 1, :] * xp_sc[off:off + tt, :]
    a, u = _lru_gates(xc, wa_ref, ba_ref[...], wx_ref, bx_ref[...], lam_ref[...])
    row = lax.broadcasted_iota(jnp.int32, (tt, D_RNN), 0) % SUBLANES
    d = 1
    while d < SUBLANES:
        keep = row >= d
        a_prev = jnp.where(keep, pltpu.roll(a, d, 0), 1.0)
        u_prev = jnp.where(keep, pltpu.roll(u, d, 0), 0.0)
        u = a * u_prev + u
        a = a * a_prev
        d *= 2
    carry = h_sc[...]
    hs = []
    for grp in range(tt // SUBLANES):
        rs = slice(grp * SUBLANES, (grp + 1) * SUBLANES)
        hs.append(a[rs] * carry + u[rs])
        carry = hs[-1][SUBLANES - 1:SUBLANES, :]
    h_sc[...] = carry
    hT_ref[0] = carry
    y_ref[...] = (_gelu_tanh(u_ref[:, :D_RNN]) * jnp.concatenate(hs, axis=0)).astype(BF16)
    xp_sc[0:SUBLANES, :] = xp_sc[tt:tt + SUBLANES, :]


def _lru_weight_args(cw, cb, wa, ba, wx, bx, lam):
    row = lambda v: v.reshape(1, D_RNN)
    return (cw, row(cb), wa.astype(BF16), row(ba), wx.astype(BF16), row(bx), row(lam))


_LRU_WEIGHT_SPECS = [_full((CONV_W, D_RNN)), _full((1, D_RNN)), _full((LRU_BLOCKS, LRU_BW, LRU_BW)),
                     _full((1, D_RNN)), _full((LRU_BLOCKS, LRU_BW, LRU_BW)), _full((1, D_RNN)),
                     _full((1, D_RNN))]


def lru_seq(u, cw, cb, wa, ba, wx, bx, lam):
    tt = LRU_TIME_TILE
    nt = SEQ // tt
    y, hT = pl.pallas_call(
        _lru_seq_body,
        grid=(BATCH, nt),
        in_specs=[pl.BlockSpec((tt, 2 * D_RNN), lambda b, t: (b * nt + t, 0))] + _LRU_WEIGHT_SPECS,
        out_specs=[pl.BlockSpec((tt, D_RNN), lambda b, t: (b * nt + t, 0)),
                   pl.BlockSpec((1, 1, D_RNN), lambda b, t: (b, 0, 0))],
        out_shape=[jax.ShapeDtypeStruct((N_PROMPT, D_RNN), BF16), jax.ShapeDtypeStruct((BATCH, 1, D_RNN), F32)],
        scratch_shapes=[pltpu.VMEM((tt + SUBLANES, D_RNN), F32), pltpu.VMEM((1, D_RNN), F32)],
        compiler_params=_params(2),
        name="lru_seq",
    )(u, *_lru_weight_args(cw, cb, wa, ba, wx, bx, lam))
    return y, hT.reshape(BATCH, D_RNN)


def _lru_step_body(u_ref, cs_ref, h0_ref, cw_ref, cb_ref, wa_ref, ba_ref, wx_ref, bx_ref, lam_ref, y_ref, hT_ref):
    n_t = u_ref.shape[0]
    hist = [cs_ref[:, w, :] for w in range(CONV_W - 1)] + [u_ref[t, :, D_RNN:] for t in range(n_t)]
    h = h0_ref[...]
    for t in range(n_t):
        xc = cb_ref[...]
        for w in range(CONV_W):
            xc = xc + cw_ref[w:w + 1, :] * hist[t + w]
        a, u = _lru_gates(xc, wa_ref, ba_ref[...], wx_ref, bx_ref[...], lam_ref[...])
        h = a * h + u
        y_ref[t] = (_gelu_tanh(u_ref[t, :, :D_RNN]) * h).astype(BF16)
    hT_ref[...] = h


def lru_step(u, conv_state, h0, cw, cb, wa, ba, wx, bx, lam):
    T, B, _ = u.shape
    return pl.pallas_call(
        _lru_step_body,
        out_shape=[jax.ShapeDtypeStruct((T, B, D_RNN), BF16), jax.ShapeDtypeStruct((B, D_RNN), F32)],
        compiler_params=pltpu.CompilerParams(vmem_limit_bytes=VMEM_LIMIT_BYTES),
        name="lru_step",
    )(u, conv_state, h0, *_lru_weight_args(cw, cb, wa, ba, wx, bx, lam))


ROUTE_E1, ROUTE_E2, ROUTE_G1, ROUTE_G2, ROUTE_R1, ROUTE_R2 = range(6)
EXPERT_LANE0 = N_GROUPS


def _lane_pick(val_by_lane):
    rows = next(iter(val_by_lane.values())).shape[0]
    lane = lax.broadcasted_iota(jnp.int32, (rows, LANES), 1)
    out = jnp.zeros((rows, LANES), F32)
    for l, v in val_by_lane.items():
        out = jnp.where(lane == l, v, out)
    return out


def _route_rows(logits, tri_ref, carry_ref):
    rows = logits.shape[0]
    lane = lax.broadcasted_iota(jnp.int32, (rows, LANES), 1)
    neg = -jnp.inf
    gl = jnp.where(lane < N_GROUPS, logits, neg)
    gmax = jnp.max(gl, axis=-1, keepdims=True)
    gtop = jnp.min(jnp.where(gl == gmax, lane, LANES), axis=-1, keepdims=True)
    gsum = jnp.sum(jnp.where(lane < N_GROUPS, jnp.exp(logits - gmax), 0.0), axis=-1, keepdims=True)
    g_w = 1.0 / gsum
    lo = EXPERT_LANE0 + EXP_PER_GROUP * gtop
    el = jnp.where((lane >= lo) & (lane < lo + EXP_PER_GROUP), logits, neg)
    v1 = jnp.max(el, axis=-1, keepdims=True)
    i1 = jnp.min(jnp.where(el == v1, lane, LANES), axis=-1, keepdims=True)
    el2 = jnp.where(lane == i1, neg, el)
    v2 = jnp.max(el2, axis=-1, keepdims=True)
    i2 = jnp.min(jnp.where(el2 == v2, lane, LANES), axis=-1, keepdims=True)
    p2 = jnp.exp(v2 - v1)
    den = 1.0 + p2
    gate1 = (1.0 / den) * g_w
    gate2 = (p2 / den) * g_w
    hit1 = lane == i1
    hit2 = lane == i2
    onehot = jnp.where(hit1 | hit2, 1.0, 0.0)
    before = jnp.dot(tri_ref[...], onehot.astype(BF16), preferred_element_type=F32) + carry_ref[...]
    rank1 = jnp.sum(jnp.where(hit1, before, 0.0), axis=-1, keepdims=True)
    rank2 = jnp.sum(jnp.where(hit2, before, 0.0), axis=-1, keepdims=True)
    carry_ref[...] = carry_ref[...] + jnp.sum(onehot, axis=0, keepdims=True)
    return _lane_pick({ROUTE_E1: (i1 - EXPERT_LANE0).astype(F32), ROUTE_E2: (i2 - EXPERT_LANE0).astype(F32),
                       ROUTE_G1: gate1, ROUTE_G2: gate2, ROUTE_R1: rank1, ROUTE_R2: rank2})


def _pack_bf16_halves(x):
    w = x.shape[1] // 2
    bits = lambda v: pltpu.bitcast(v.astype(F32), jnp.uint32)
    return pltpu.bitcast((bits(x[:, :w]) >> 16) | bits(x[:, w:]), F32)


def _unpack_bf16_halves(words):
    p = pltpu.bitcast(words, jnp.uint32)
    lo = pltpu.bitcast(p << 16, F32).astype(BF16)
    hi = pltpu.bitcast(p & jnp.uint32(0xFFFF0000), F32).astype(BF16)
    return jnp.concatenate([lo, hi], axis=1)


def _post_mixer_body(n_h, n_mix, *refs):
    h_refs, refs = refs[:n_h], refs[n_h:]
    m_refs, wo_refs = refs[:2 * n_mix], refs[2 * n_mix:3 * n_mix]
    g_ref, wr_ref, br_ref, tri_ref, h1_ref, hn_ref, route_ref, cnt_ref, carry_sc = refs[3 * n_mix:]

    @pl.when(pl.program_id(0) == 0)
    def _():
        carry_sc[...] = jnp.zeros((1, LANES), F32)

    mix = None
    for j, wo_ref in enumerate(wo_refs):
        part = jnp.dot(_token_rows(m_refs[2 * j], m_refs[2 * j + 1]), wo_ref[...], preferred_element_type=F32)
        mix = part if mix is None else mix + part
    h1 = (h_refs[0][...] if n_h == 1 else _token_rows(*h_refs)) + mix
    h1_ref[...] = h1
    hn = _rms(h1, g_ref[...]).astype(BF16)
    hn_ref[...] = _pack_bf16_halves(hn)
    logits = jnp.dot(hn, wr_ref[...], preferred_element_type=F32) + br_ref[...]
    route_ref[...] = _route_rows(logits, tri_ref, carry_sc)
    cnt_ref[...] = carry_sc[...]


def post_mixer(h, mix_ins, w_out, g_ffn, w_rg, b_rg, w_re, b_re):
    hs = list(h) if isinstance(h, (tuple, list)) else [h]
    h_specs = [_PROMPT_ROWS, _SAMPLE_ROWS] if len(hs) == 2 else [_rows(D_MODEL)]
    n = N_TOK
    ks = [m[0].shape[1] for m in mix_ins]
    two_part = lambda k: [pl.BlockSpec((ROW_TILE, k), lambda i: (jnp.minimum(i, N_PROMPT_TILES - 1), 0)),
                          pl.BlockSpec((ROW_TILE, k), lambda i: (0, 0))]
    offs = np.cumsum([0] + ks)
    w_parts = [w_out[offs[j]:offs[j + 1]].astype(BF16) for j in range(len(ks))]
    pad = LANES - N_GROUPS - N_EXPERTS
    wr = jnp.concatenate([w_rg, w_re, jnp.zeros((D_MODEL, pad), F32)], axis=1).astype(BF16)
    br = jnp.concatenate([b_rg, b_re, jnp.zeros((pad,), F32)]).reshape(1, LANES)
    tri = jnp.tril(jnp.ones((ROW_TILE, ROW_TILE), BF16), -1)
    return pl.pallas_call(
        functools.partial(_post_mixer_body, len(hs), len(ks)),
        grid=(n // ROW_TILE,),
        in_specs=h_specs + [s for k in ks for s in two_part(k)] + [_full((k, D_MODEL)) for k in ks]
                 + [_full((1, D_MODEL)), _full((D_MODEL, LANES)), _full((1, LANES)), _full((ROW_TILE, ROW_TILE))],
        out_specs=[_rows(D_MODEL), _rows(D_MODEL // 2), _rows(LANES), _full((1, LANES))],
        out_shape=[jax.ShapeDtypeStruct((n, D_MODEL), F32), jax.ShapeDtypeStruct((n, D_MODEL // 2), F32),
                   jax.ShapeDtypeStruct((n, LANES), F32), jax.ShapeDtypeStruct((1, LANES), F32)],
        scratch_shapes=[pltpu.VMEM((1, LANES), F32)],
        compiler_params=_params(1),
        name="post_mixer",
    )(*hs, *[part for m in mix_ins for part in m], *w_parts, g_ffn.reshape(1, D_MODEL), wr, br, tri)


MOE_ROWS = 512


def _ffn_body(be_ref, nb_ref, x_ref, w1_ref, w3_ref, w2_ref, y_ref, w1_sc, w3_sc, w2_sc):
    i = pl.program_id(0)
    new_expert = jnp.logical_or(i == 0, be_ref[i] != be_ref[jnp.maximum(i - 1, 0)])

    @pl.when(jnp.logical_and(new_expert, i < nb_ref[0]))
    def _():
        w1_sc[...] = w1_ref[0, 0].astype(BF16)
        w3_sc[...] = w3_ref[0, 0].astype(BF16)
        w2_sc[...] = w2_ref[0, 0].astype(BF16)

    @pl.when(i < nb_ref[0])
    def _():
        x = _unpack_bf16_halves(x_ref[...])
        a = jnp.dot(x, w1_sc[...], preferred_element_type=F32)
        b = jnp.dot(x, w3_sc[...], preferred_element_type=F32)
        hdn = (a * jax.nn.sigmoid(a) * b).astype(BF16)
        y_ref[...] = jnp.dot(hdn, w2_sc[...], preferred_element_type=F32)

    @pl.when(i >= nb_ref[0])
    def _():
        y_ref[...] = jnp.zeros(y_ref.shape, F32)


def expert_ffn(xs, blk_exp, n_active, w1, w3, w2, layer):
    n_slots = xs.shape[0]
    nb = n_slots // MOE_ROWS
    wmap = lambda i, be, na: (layer, be[i], 0, 0)
    xmap = lambda i, be, na: (jnp.minimum(i, na[0] - 1), 0)
    return pl.pallas_call(
        _ffn_body,
        grid_spec=pltpu.PrefetchScalarGridSpec(
            num_scalar_prefetch=2,
            grid=(nb,),
            in_specs=[pl.BlockSpec((MOE_ROWS, D_MODEL // 2), xmap),
                      pl.BlockSpec((1, 1, D_MODEL, E_HID), wmap),
                      pl.BlockSpec((1, 1, D_MODEL, E_HID), wmap),
                      pl.BlockSpec((1, 1, E_HID, D_MODEL), wmap)],
            out_specs=pl.BlockSpec((MOE_ROWS, D_MODEL), lambda i, be, na: (i, 0)),
            scratch_shapes=[pltpu.VMEM((D_MODEL, E_HID), BF16), pltpu.VMEM((D_MODEL, E_HID), BF16),
                            pltpu.VMEM((E_HID, D_MODEL), BF16)],
        ),
        out_shape=jax.ShapeDtypeStruct((n_slots, D_MODEL), F32),
        compiler_params=_params(1),
        name="expert_ffn",
    )(blk_exp, n_active, xs, w1, w3, w2)


def moe_dispatch(route, counts_row, n):
    counts = counts_row[0, EXPERT_LANE0:EXPERT_LANE0 + N_EXPERTS].astype(jnp.int32)
    padded = ((counts + MOE_ROWS - 1) // MOE_ROWS) * MOE_ROWS
    pad_end = jnp.cumsum(padded)
    pad_start = (pad_end - padded).astype(F32)
    expert_lane = lax.broadcasted_iota(jnp.int32, (1, N_EXPERTS), 1).astype(F32)
    dest = []
    for e_lane, r_lane in ((ROUTE_E1, ROUTE_R1), (ROUTE_E2, ROUTE_R2)):
        start = jnp.sum(jnp.where(route[:, e_lane:e_lane + 1] == expert_lane, pad_start[None, :], 0.0), axis=1)
        dest.append((start + route[:, r_lane]).astype(jnp.int32))
    nb = -(-(n * TOP_K_IN_GROUP) // MOE_ROWS) + N_EXPERTS
    n_slots = nb * MOE_ROWS
    tok = jnp.arange(n, dtype=jnp.int32)
    slot_tok = jnp.zeros((n_slots,), jnp.int32).at[jnp.concatenate(dest)].set(
        jnp.concatenate([tok] * TOP_K_IN_GROUP), unique_indices=True, mode='promise_in_bounds')
    blk_start = jnp.arange(nb, dtype=jnp.int32) * MOE_ROWS
    blk_exp = jnp.sum((pad_end[None, :] <= blk_start[:, None]).astype(jnp.int32), axis=1)
    blk_exp = jnp.minimum(blk_exp, N_EXPERTS - 1)
    n_active = (pad_end[-1] // MOE_ROWS).astype(jnp.int32).reshape(1)
    return slot_tok, dest, blk_exp, n_active


def moe_experts(hn, route, counts_row, w1, w3, w2, layer):
    n = hn.shape[0]
    slot_tok, dest, blk_exp, n_active = moe_dispatch(route, counts_row, n)
    xs = hn.at[slot_tok].get(mode='promise_in_bounds')
    ys = expert_ffn(xs, blk_exp, n_active, w1, w3, w2, layer)
    return [ys.at[d].get(mode='promise_in_bounds') for d in dest]


def _combine(h_ref, y1_ref, y2_ref, route_ref):
    lane = lax.broadcasted_iota(jnp.int32, (ROW_TILE, LANES), 1)
    r = route_ref[...]
    g1 = jnp.sum(jnp.where(lane == ROUTE_G1, r, 0.0), axis=-1, keepdims=True)
    g2 = jnp.sum(jnp.where(lane == ROUTE_G2, r, 0.0), axis=-1, keepdims=True)
    return h_ref[...] + (y1_ref[...] * g1 + y2_ref[...] * g2)


def _combine_proj_body(h_ref, y1_ref, y2_ref, route_ref, g_ref, w_ref, h2_ref, u_ref):
    h2 = _combine(h_ref, y1_ref, y2_ref, route_ref)
    h2_ref[...] = h2
    u_ref[...] = jnp.dot(_rms(h2, g_ref[...]).astype(BF16), w_ref[...], preferred_element_type=F32)


def _combine_norm_body(h_ref, y1_ref, y2_ref, route_ref, g_ref, yp_ref, ys_ref):
    y = _rms(_combine(h_ref, y1_ref, y2_ref, route_ref), g_ref[...])

    @pl.when(pl.program_id(0) < N_PROMPT_TILES)
    def _():
        yp_ref[...] = y

    @pl.when(pl.program_id(0) >= N_PROMPT_TILES)
    def _():
        ys_ref[...] = y


def combine_proj(h, ys, route, g, w):
    n = h.shape[0]
    nn = w.shape[1]
    return pl.pallas_call(
        _combine_proj_body,
        grid=(n // ROW_TILE,),
        in_specs=[_rows(D_MODEL), _rows(D_MODEL), _rows(D_MODEL), _rows(LANES), _full((1, D_MODEL)),
                  _full((D_MODEL, nn))],
        out_specs=[_rows(D_MODEL), _rows(nn)],
        out_shape=[jax.ShapeDtypeStruct((n, D_MODEL), F32), jax.ShapeDtypeStruct((n, nn), F32)],
        compiler_params=_params(1),
        name="combine_proj",
    )(h, *ys, route, g.reshape(1, D_MODEL), w.astype(BF16))


def combine_norm(h, ys, route, g):
    n = h.shape[0]
    return pl.pallas_call(
        _combine_norm_body,
        grid=(n // ROW_TILE,),
        in_specs=[_rows(D_MODEL), _rows(D_MODEL), _rows(D_MODEL), _rows(LANES), _full((1, D_MODEL))],
        out_specs=[_PROMPT_ROWS, _SAMPLE_ROWS],
        out_shape=[jax.ShapeDtypeStruct((N_PROMPT, D_MODEL), F32), jax.ShapeDtypeStruct((N_SAMPLE, D_MODEL), F32)],
        compiler_params=_params(1),
        name="combine_norm",
    )(h, *ys, route, g.reshape(1, D_MODEL))


def _even_first(x, axis):
    n = x.shape[axis]
    idx = jnp.concatenate([jnp.arange(0, n, 2), jnp.arange(1, n, 2)])
    return jnp.take(x, idx, axis=axis)


def _decode_query_cols(q_s, gates_s):
    B, T, G, R = DEC_BATCH, DEC_SEQ, NSA_KV, NSA_REP
    qg = q_s.reshape(B, T, G, R, HD).transpose(0, 2, 4, 3, 1).reshape(B, G, HD, R * T)
    qt = jnp.zeros((B, G, HD, G, R * T), BF16)
    for g in range(G):
        qt = qt.at[:, g, :, g, :].set(qg[:, g])
    qt = jnp.pad(qt.reshape(B, G * HD, G * R * T), ((0, 0), (0, 0), (0, LANES - G * R * T)))
    gr = gates_s[:, :3 * NSA_HEADS].reshape(B, T, NSA_HEADS, 3).transpose(0, 3, 2, 1).reshape(B, 3, NSA_HEADS * T)
    return qt, jnp.pad(gr, ((0, 0), (0, 0), (0, LANES - NSA_HEADS * T)))


def _feature_major(cache):
    lead, rows = cache.shape[:2]
    return jnp.transpose(cache, (0, 2, 3, 4, 1)).reshape(lead, NSA_KV_W, rows)


def _kv_rows_from_feature_major(xt, lead):
    x = xt.reshape(2, NSA_KV, HD, *lead)
    n = len(lead)
    return jnp.transpose(x, tuple(range(3, 3 + n)) + (0, 1, 2))


def mixer_a(h, p, past):
    pos = np.concatenate([np.arange(SEQ), PAST_LEN + np.tile(np.arange(DEC_SEQ), DEC_BATCH)])
    (q, kvc, kvs, kvw, kvct_p, kvct_s, kvst_p, kvst_s, kvwt_p, kvwt_s, ks, vst, kw, vwt, gates, gq, gk, gv,
     la, sgg) = inproj_a(
        *h, p['norm_mix'][0], p['a_w_in'][0], p['a_gla_wa2'][0], p['a_gla_ba'][0], pos)
    cmp_w, gnorm = _cmp_weights(p['a_cmp_pe'][0], p['a_cmp_w'][0]), p['a_gla_norm'][0]
    P = N_PROMPT
    ckv = cmp_blocks_rows(kvc, P, cmp_w)
    ckv = _even_first(ckv.reshape(BATCH, N_CMP_PROMPT, NSA_KV_W), 1)
    o_nsa_p = nsa_prompt_attn(q, gates, ckv, ks, vst, kw, vwt)
    o_gla_p, st_p = gla_seq(gq, gk, la, gv, sgg, gnorm)
    rows_p = lambda xt: jnp.transpose(xt.reshape(BATCH, 2, NSA_KV, HD, xt.shape[2]), (0, 4, 1, 2, 3))
    new_p = (rows_p(kvct_p), rows_p(kvst_p), rows_p(kvwt_p[:, :, SEQ - WINDOW:]), _gla_state_from_pairs(st_p))
    o_gla_s, st_s = gla_step(gq, gk, la, gv, sgg, gnorm, _gla_state_to_pairs(past['state_gla'][0]), P)
    n_pool = past['cache_cmp_kv'].shape[1]
    ckv_pool = cmp_blocks_pages(_feature_major(past['cache_cmp_kv'][0]), cmp_w)
    ckv_seq = ckv_pool.reshape(n_pool, PAGE_SIZE // CMP_BLOCK, NSA_KV_W)[past['page_table']]
    ckv_seq = _even_first(ckv_seq.reshape(DEC_BATCH, N_CMP_DEC, NSA_KV_W), 1)
    qt, gate_rows = _decode_query_cols(q[P:], gates[P:])
    per_seq = lambda x: x[P:].reshape(DEC_BATCH, 1, DEC_SEQ * NSA_KV_W)
    o_t, win_new = nsa_decode_attn(qt, gate_rows, ckv_seq, _feature_major(past['cache_slc_kv'][0]),
                                   _feature_major(past['cache_win_kv'][0]), per_seq(kvs), per_seq(kvw),
                                   past['page_table'])
    o_nsa_s = o_t[:, :, :NSA_HEADS * DEC_SEQ].reshape(DEC_BATCH, HD, NSA_HEADS, DEC_SEQ)
    o_nsa_s = o_nsa_s.transpose(0, 3, 2, 1).reshape(N_SAMPLE, NSA_Q_W).astype(BF16)
    n_buf = win_new.shape[2]
    win_s = jnp.transpose(win_new.reshape(DEC_BATCH, 2, NSA_KV, HD, n_buf), (0, 4, 1, 2, 3))
    new_s = (_kv_rows_from_feature_major(kvct_s, (DEC_BATCH, DEC_SEQ)),
             _kv_rows_from_feature_major(kvst_s, (DEC_BATCH, DEC_SEQ)), win_s, _gla_state_from_pairs(st_s))
    return (o_nsa_p, o_nsa_s), (o_gla_p, o_gla_s), new_p, new_s


def run_trunk(x_prompt, x_sample, p, past):
    h0 = (x_prompt.reshape(N_PROMPT, D_MODEL), x_sample.reshape(N_SAMPLE, D_MODEL))
    o_nsa, o_gla, new_p, new_s = mixer_a(h0, p, past)
    h, hn, route, counts = post_mixer(h0, [o_nsa, o_gla], p['a_w_out'][0], p['norm_ffn'][0], p['m_w_rg'][0],
                                      p['m_b_rg'][0], p['m_w_re'][0], p['m_b_re'][0])
    ys = moe_experts(hn, route, counts, p['m_w1'], p['m_w3'], p['m_w2'], 0)
    h, u = combine_proj(h, ys, route, p['norm_mix'][1], p['c_w_in'][0])
    lru_w = (p['c_conv_w'][0], p['c_conv_b'][0], p['c_w_a'][0], p['c_b_a'][0], p['c_w_x'][0], p['c_b_x'][0],
             p['c_lam'][0])
    us = u[N_PROMPT:].reshape(DEC_BATCH, DEC_SEQ, 2 * D_RNN)
    y_p, lru_p = lru_seq(u, *lru_w)
    y_s, lru_s = lru_step(jnp.swapaxes(us, 0, 1), past['state_conv'][0], past['state_lru'][0], *lru_w)
    conv_p = jnp.stack([u[(b + 1) * SEQ - (CONV_W - 1):(b + 1) * SEQ, D_RNN:] for b in range(BATCH)])
    conv_s = us[:, DEC_SEQ - (CONV_W - 1):, D_RNN:]
    mix_in = (y_p, jnp.swapaxes(y_s, 0, 1).reshape(N_SAMPLE, D_RNN))
    h, hn, route, counts = post_mixer(h, [mix_in], p['c_w_out'][0], p['norm_ffn'][1], p['m_w_rg'][1],
                                      p['m_b_rg'][1], p['m_w_re'][1], p['m_b_re'][1])
    ys = moe_experts(hn, route, counts, p['m_w1'], p['m_w3'], p['m_w2'], 1)
    y_p, y_s = combine_norm(h, ys, route, p['norm_final'])
    y_prompt = y_p.reshape(BATCH, SEQ, D_MODEL)
    y_sample = y_s.reshape(DEC_BATCH, DEC_SEQ, D_MODEL)
    return (y_prompt, y_sample), new_p + (lru_p, conv_p), new_s + (lru_s, conv_s)


def kernel(x_prompt, x_sample, cache_cmp_kv, cache_slc_kv, cache_win_kv, state_gla, state_lru, state_conv,
           page_table, norm_mix, norm_ffn, norm_final, a_w_in, a_cmp_pe, a_cmp_w, a_gla_wa2, a_gla_ba,
           a_gla_norm, a_w_out, c_w_in, c_conv_w, c_conv_b, c_w_a, c_b_a, c_w_x, c_b_x, c_lam, c_w_out,
           m_w_rg, m_b_rg, m_w_re, m_b_re, m_w1, m_w3, m_w2):
    p = {'norm_mix': norm_mix, 'norm_ffn': norm_ffn, 'norm_final': norm_final,
         'a_w_in': a_w_in, 'a_cmp_pe': a_cmp_pe, 'a_cmp_w': a_cmp_w, 'a_gla_wa2': a_gla_wa2,
         'a_gla_ba': a_gla_ba, 'a_gla_norm': a_gla_norm, 'a_w_out': a_w_out,
         'c_w_in': c_w_in, 'c_conv_w': c_conv_w, 'c_conv_b': c_conv_b, 'c_w_a': c_w_a, 'c_b_a': c_b_a,
         'c_w_x': c_w_x, 'c_b_x': c_b_x, 'c_lam': c_lam, 'c_w_out': c_w_out,
         'm_w_rg': m_w_rg, 'm_b_rg': m_b_rg, 'm_w_re': m_w_re, 'm_b_re': m_b_re,
         'm_w1': m_w1, 'm_w3': m_w3, 'm_w2': m_w2}
    past = {'cache_cmp_kv': cache_cmp_kv, 'cache_slc_kv': cache_slc_kv, 'cache_win_kv': cache_win_kv,
            'state_gla': state_gla, 'state_lru': state_lru, 'state_conv': state_conv,
            'page_table': page_table}
    (y_p, y_s), sp, ss = run_trunk(x_prompt, x_sample, p, past)
    outs = [y_p, y_s]
    for a, b in zip(sp, ss):
        outs += [a[None], b[None]]
    return tuple(outs)
```

```python
import functools
import jax, jax.numpy as jnp
from jax import lax
import numpy as np
from jax.experimental import pallas as pl
from jax.experimental.pallas import tpu as pltpu

D_MODEL = 1024
BATCH = 2
SEQ = 8192
DEC_BATCH = 128
DEC_SEQ = 4
PAST_LEN = 2048
PAGE_SIZE = 128
EPS = 1e-6
NSA_HEADS = 8
NSA_KV = 2
NSA_REP = NSA_HEADS // NSA_KV
HD = 64
CMP_BLOCK = 32
SLC_BLOCK = 64
SLC_TOPK = 16
WINDOW = 512
Q_BLOCK = 128
ROPE_DIM = HD // 4
ROPE_THETA = 500000.0
GLA_HEADS = 4
GLA_DK = 64
GLA_DV = 128
GLA_LOWRANK = 16
GLA_TAU = 16.0
D_RNN = 1280
LRU_BLOCKS = 10
LRU_BW = D_RNN // LRU_BLOCKS
CONV_W = 4
LRU_C = 8.0
N_GROUPS = 4
EXP_PER_GROUP = 8
N_EXPERTS = N_GROUPS * EXP_PER_GROUP
E_HID = 512
TOP_K_IN_GROUP = 2
NSA_Q_W = NSA_HEADS * HD
NSA_KV_W = 2 * NSA_KV * HD
GLA_K_W = GLA_HEADS * GLA_DK
GLA_V_W = GLA_HEADS * GLA_DV
A_SIZES = (NSA_Q_W, NSA_KV_W, NSA_KV_W, NSA_KV_W, 3 * NSA_HEADS, GLA_K_W, GLA_K_W, GLA_V_W, GLA_LOWRANK, GLA_V_W)
N_PROMPT = BATCH * SEQ
N_SAMPLE = DEC_BATCH * DEC_SEQ
N_TOK = N_PROMPT + N_SAMPLE
N_PAGES = PAST_LEN // PAGE_SIZE

F32 = jnp.float32
BF16 = jnp.bfloat16
VMEM_LIMIT_BYTES = 56 * 1024 * 1024
LANES = 128
SUBLANES = 8
ROW_TILE = 512


def _params(n_axes):
    return pltpu.CompilerParams(dimension_semantics=("arbitrary",) * n_axes, vmem_limit_bytes=VMEM_LIMIT_BYTES)


def _full(shape):
    return pl.BlockSpec(shape, lambda *_: (0,) * len(shape))


def _rows(width):
    return pl.BlockSpec((ROW_TILE, width), lambda i: (i, 0))


def _rms(x, g):
    return x * lax.rsqrt(jnp.mean(x * x, axis=-1, keepdims=True) + EPS) * g


def _softplus(x):
    return jnp.maximum(x, 0.0) + jnp.log1p(jnp.exp(-jnp.abs(x)))


def _gelu_tanh(x):
    return x * (0.5 * (1.0 + jnp.tanh(0.7978845608028654 * (x + 0.044715 * (x * x * x)))))


def _nt_dot(a, b):
    return lax.dot_general(a, b, (((1,), (1,)), ((), ())), preferred_element_type=F32)


def _tn_dot(a, b):
    return lax.dot_general(a, b, (((0,), (0,)), ((), ())), preferred_element_type=F32)


A_Q0, A_KVC0, A_KVS0, A_KVW0 = 0, 512, 768, 1024
A_GQ0, A_GK0, A_GV0, A_GG0, A_MISC0 = 1280, 1536, 1792, 2304, 2816
A_COLS = A_MISC0 + LANES
MISC_LR0 = 3 * NSA_HEADS
KV_HALF = NSA_KV * HD


def _rope_lanes(x, cos_t, sin_lo, sin_hi):
    reps = x.shape[1] // LANES
    tile = (lambda t: jnp.concatenate([t] * reps, axis=1)) if reps > 1 else (lambda t: t)
    w = x.shape[1]
    half = ROPE_DIM // 2
    return x * tile(cos_t) + pltpu.roll(x, half, 1) * tile(sin_hi) + pltpu.roll(x, w - half, 1) * tile(sin_lo)


N_PROMPT_TILES = N_PROMPT // ROW_TILE
_PROMPT_ROWS = pl.BlockSpec((ROW_TILE, D_MODEL), lambda i: (jnp.minimum(i, N_PROMPT_TILES - 1), 0))
_SAMPLE_ROWS = pl.BlockSpec((ROW_TILE, D_MODEL), lambda i: (0, 0))


def _token_rows(prompt_ref, sample_ref):
    return jnp.where(pl.program_id(0) < N_PROMPT_TILES, prompt_ref[...], sample_ref[...])


def _inproj_a_body(hp_ref, hs_ref, g_ref, w_ref, wa2_ref, ba_ref, cos_ref, slo_ref, shi_ref,
                   q_ref, kvc_ref, kvs_ref, kvw_ref, kvct_p_ref, kvct_s_ref, kvst_p_ref, kvst_s_ref, kvwt_p_ref,
                   kvwt_s_ref, ks_ref, vst_ref, kw_ref, vwt_ref, gates_ref, gq_ref, gk_ref, gv_ref, la_ref,
                   sgg_ref):
    y = _rms(jnp.where(pl.program_id(0) == 0, hs_ref[...], hp_ref[...]), g_ref[...]).astype(BF16)
    proj = lambda a, b: jnp.dot(y, w_ref[:, a:b], preferred_element_type=F32)
    cos_t, sin_lo, sin_hi = cos_ref[...], slo_ref[...], shi_ref[...]
    q_ref[...] = (_rope_lanes(proj(A_Q0, A_KVC0), cos_t, sin_lo, sin_hi) * (HD ** -0.5)).astype(BF16)

    def store_feature_major(p_ref, s_ref, xt):
        p_ref[0] = xt
        s_ref[...] = xt

    kvc = proj(A_KVC0, A_KVS0)
    kvc_ref[...] = kvc
    store_feature_major(kvct_p_ref, kvct_s_ref, kvc.T)
    for a0, kv_ref, kvt_p_ref, kvt_s_ref, k_ref, vt_ref in (
            (A_KVS0, kvs_ref, kvst_p_ref, kvst_s_ref, ks_ref, vst_ref),
            (A_KVW0, kvw_ref, kvwt_p_ref, kvwt_s_ref, kw_ref, vwt_ref)):
        kv = proj(a0, a0 + 2 * KV_HALF)
        k = _rope_lanes(kv[:, :KV_HALF], cos_t, sin_lo, sin_hi)
        vt = kv[:, KV_HALF:].T
        kv_ref[:, :KV_HALF] = k
        kv_ref[:, KV_HALF:] = kv[:, KV_HALF:]
        store_feature_major(kvt_p_ref, kvt_s_ref, jnp.concatenate([k.T, vt], axis=0))
        k_ref[...] = k.astype(BF16)
        vt_ref[...] = vt.astype(BF16)
    misc = proj(A_MISC0, A_COLS)
    gates_ref[...] = jax.nn.sigmoid(misc)
    z = jnp.dot(misc.astype(BF16), wa2_ref[...], preferred_element_type=F32) + ba_ref[...]
    la_ref[...] = -_softplus(-z) * (1.0 / GLA_TAU)
    gq_ref[...] = proj(A_GQ0, A_GK0) * (GLA_DK ** -0.5)
    gk_ref[...] = proj(A_GK0, A_GV0)
    gv_ref[...] = proj(A_GV0, A_GG0).astype(BF16)
    gg = proj(A_GG0, A_MISC0)
    sgg_ref[...] = gg * jax.nn.sigmoid(gg)


def _rope_tables(pos):
    pos = np.asarray(pos)
    half = ROPE_DIM // 2
    f4 = np.float32
    inv = (f4(1.0) / (f4(ROPE_THETA) ** (np.arange(0, ROPE_DIM, 2, dtype=f4) / f4(ROPE_DIM)))).astype(f4)
    ang = (pos.astype(f4)[:, None] * inv[None, :]).astype(np.float64)
    cos, sin = np.cos(ang).astype(f4), np.sin(ang).astype(f4)
    n = pos.shape[0]
    one = np.ones((n, HD - ROPE_DIM), f4)
    zero = np.zeros((n, HD - ROPE_DIM), f4)
    zh = np.zeros((n, half), f4)
    seg = lambda a, b, rest: jnp.asarray(np.concatenate([a, b, rest] * (LANES // HD), axis=1))
    return seg(cos, cos, one), seg(-sin, zh, zero), seg(zh, sin, zero)


def inproj_a(h_prompt, h_sample, g, w_in, wa2, ba, pos):
    n = h_prompt.shape[0] + h_sample.shape[0]
    zpad = jnp.zeros((D_MODEL, LANES - 3 * NSA_HEADS - GLA_LOWRANK), F32)
    o = np.cumsum((0,) + A_SIZES)
    w = jnp.concatenate([w_in[:, o[0]:o[4]], w_in[:, o[5]:o[8]], w_in[:, o[9]:o[10]],
                         w_in[:, o[4]:o[5]], w_in[:, o[8]:o[9]], zpad], axis=1).astype(BF16)
    wa2p = jnp.zeros((LANES, GLA_K_W), F32).at[MISC_LR0:MISC_LR0 + GLA_LOWRANK].set(wa2).astype(BF16)
    seq_tiles = SEQ // ROW_TILE
    ptile = lambda s: jnp.maximum(s - 1, 0)
    tile = lambda s: jnp.where(s == 0, N_PROMPT_TILES, s - 1)
    rows_t = lambda wd: pl.BlockSpec((ROW_TILE, wd), lambda s: (tile(s), 0))
    pos_rows = pl.BlockSpec((ROW_TILE, LANES), lambda s: (jnp.where(s == 0, seq_tiles, (s - 1) % seq_tiles), 0))
    cols = pl.BlockSpec((KV_HALF, ROW_TILE), lambda s: (0, tile(s)))
    kvt_p = (pl.BlockSpec((1, NSA_KV_W, ROW_TILE), lambda s: (ptile(s) // seq_tiles, 0, ptile(s) % seq_tiles)),
             (BATCH, NSA_KV_W, SEQ), F32)
    kvt_s = (pl.BlockSpec((NSA_KV_W, ROW_TILE), lambda s: (0, jnp.minimum(s, 1))), (NSA_KV_W, 2 * ROW_TILE), F32)
    outs = [(rows_t(NSA_Q_W), (n, NSA_Q_W), BF16), (rows_t(NSA_KV_W), (n, NSA_KV_W), F32),
            (rows_t(NSA_KV_W), (n, NSA_KV_W), F32), (rows_t(NSA_KV_W), (n, NSA_KV_W), F32),
            kvt_p, kvt_s, kvt_p, kvt_s, kvt_p, kvt_s,
            (rows_t(KV_HALF), (n, KV_HALF), BF16), (cols, (KV_HALF, n), BF16),
            (rows_t(KV_HALF), (n, KV_HALF), BF16), (cols, (KV_HALF, n), BF16),
            (rows_t(LANES), (n, LANES), F32), (rows_t(GLA_K_W), (n, GLA_K_W), F32),
            (rows_t(GLA_K_W), (n, GLA_K_W), F32), (rows_t(GLA_V_W), (n, GLA_V_W), BF16),
            (rows_t(GLA_K_W), (n, GLA_K_W), F32), (rows_t(GLA_V_W), (n, GLA_V_W), F32)]
    return pl.pallas_call(
        _inproj_a_body,
        grid=(n // ROW_TILE,),
        in_specs=[pl.BlockSpec((ROW_TILE, D_MODEL), lambda s: (ptile(s), 0)), _SAMPLE_ROWS, _full((1, D_MODEL)),
                  _full((D_MODEL, A_COLS)), _full((LANES, GLA_K_W)),
                  _full((1, GLA_K_W)), pos_rows, pos_rows, pos_rows],
        out_specs=[s for s, _, _ in outs],
        out_shape=[jax.ShapeDtypeStruct(shape, dt) for _, shape, dt in outs],
        compiler_params=_params(1),
        name="inproj_a",
    )(h_prompt, h_sample, g.reshape(1, D_MODEL), w, wa2p, ba.reshape(1, GLA_K_W), *_rope_tables(pos))


CMP_TILE_BLOCKS = 256
CMP_TILE_ROWS = CMP_TILE_BLOCKS * CMP_BLOCK
CMP_TILE_PAGES = CMP_TILE_ROWS // PAGE_SIZE


def _cmp_reduce(xk_ref, xv_ref, pe_ref, w_ref):
    acc = jnp.zeros((CMP_TILE_BLOCKS, NSA_KV_W), F32)
    for l in range(CMP_BLOCK):
        rows = pl.ds(l, CMP_TILE_BLOCKS, stride=CMP_BLOCK)
        xl = jnp.concatenate([xk_ref[rows, :], xv_ref[rows, :]], axis=1) + pe_ref[l:l + 1, :]
        acc = acc + jnp.dot(xl.astype(BF16), w_ref[l], preferred_element_type=F32)
    return acc


def _cmp_rows_body(xk_ref, xv_ref, pe_ref, w_ref, o_ref):
    o_ref[...] = _cmp_reduce(xk_ref, xv_ref, pe_ref, w_ref)


def _cmp_pages_body(x_ref, pe_ref, w_ref, o_ref, xk_sc, xv_sc):
    for pg in range(CMP_TILE_PAGES):
        rows = slice(pg * PAGE_SIZE, (pg + 1) * PAGE_SIZE)
        xk_sc[rows, :] = x_ref[pg, :KV_HALF, :].T
        xv_sc[rows, :] = x_ref[pg, KV_HALF:, :].T
    o_ref[...] = _cmp_reduce(xk_sc, xv_sc, pe_ref, w_ref)


def _cmp_weights(pe, w_cmp):
    pe_rows = jnp.broadcast_to(pe[:, :, None, :], (CMP_BLOCK, 2, NSA_KV, HD)).reshape(CMP_BLOCK, NSA_KV_W)
    w = w_cmp.astype(BF16)
    zero = jnp.zeros((CMP_BLOCK, HD, HD), BF16)
    diag = [w[:, c] for c in range(2) for _ in range(NSA_KV)]
    w_bd = jnp.concatenate([jnp.concatenate([blk if j == i else zero for j in range(len(diag))], axis=2)
                            for i, blk in enumerate(diag)], axis=1)
    return pe_rows, w_bd


def cmp_blocks_rows(x, n_rows, cmp_weights):
    return pl.pallas_call(
        _cmp_rows_body,
        grid=(n_rows // CMP_TILE_ROWS,),
        in_specs=[pl.BlockSpec((CMP_TILE_ROWS, KV_HALF), lambda i: (i, 0)),
                  pl.BlockSpec((CMP_TILE_ROWS, KV_HALF), lambda i: (i, 1)), _full((CMP_BLOCK, NSA_KV_W)),
                  _full((CMP_BLOCK, NSA_KV_W, NSA_KV_W))],
        out_specs=pl.BlockSpec((CMP_TILE_BLOCKS, NSA_KV_W), lambda i: (i, 0)),
        out_shape=jax.ShapeDtypeStruct((n_rows // CMP_BLOCK, NSA_KV_W), F32),
        compiler_params=_params(1),
        name="cmp_blocks_rows",
    )(x, x, *cmp_weights)


def cmp_blocks_pages(xt, cmp_weights):
    n_pages = xt.shape[0]
    return pl.pallas_call(
        _cmp_pages_body,
        grid=(n_pages // CMP_TILE_PAGES,),
        in_specs=[pl.BlockSpec((CMP_TILE_PAGES, NSA_KV_W, PAGE_SIZE), lambda i: (i, 0, 0)),
                  _full((CMP_BLOCK, NSA_KV_W)), _full((CMP_BLOCK, NSA_KV_W, NSA_KV_W))],
        out_specs=pl.BlockSpec((CMP_TILE_BLOCKS, NSA_KV_W), lambda i: (i, 0)),
        out_shape=jax.ShapeDtypeStruct((n_pages * PAGE_SIZE // CMP_BLOCK, NSA_KV_W), F32),
        scratch_shapes=[pltpu.VMEM((CMP_TILE_ROWS, KV_HALF), F32), pltpu.VMEM((CMP_TILE_ROWS, KV_HALF), F32)],
        compiler_params=_params(1),
        name="cmp_blocks_pages",
    )(xt, *cmp_weights)


KEY_TILE = 256
NEG_BIG = -1e30
N_CMP_PROMPT = SEQ // CMP_BLOCK
N_SLC_PROMPT = SEQ // SLC_BLOCK
SLC_SHIFT = SLC_BLOCK.bit_length() - 1
N_CMP_DEC = PAST_LEN // CMP_BLOCK
N_SLC_DEC = -(-(PAST_LEN + DEC_SEQ) // SLC_BLOCK)
N_SLC_DEC_PAD = -(-N_SLC_DEC // LANES) * LANES
DEC_COLS_PER_GROUP = NSA_REP * DEC_SEQ


def _tile_cols(x, reps):
    return jnp.concatenate([x] * reps, axis=1) if reps > 1 else x


def _even_first_cmp_end(n_cmp):
    j = lax.broadcasted_iota(jnp.int32, (n_cmp, 1), 0)
    blk = jnp.where(j < n_cmp // 2, 2 * j, 2 * (j - n_cmp // 2) + 1)
    return (blk + 1) * CMP_BLOCK - 1


def _softmax_cols(s, mask):
    s = jnp.where(mask, s, -jnp.inf)
    m = jnp.max(s, axis=0, keepdims=True)
    m = jnp.where(m > -jnp.inf, m, 0.0)
    e = jnp.where(mask, jnp.exp(s - m), 0.0)
    return e / jnp.maximum(jnp.sum(e, axis=0, keepdims=True), 1e-30)


def _softmax_cols_biased(s):
    m = jnp.max(s, axis=0, keepdims=True)
    e = jnp.exp(s - m)
    scale = jnp.where(m > 0.5 * NEG_BIG, 1.0 / jnp.sum(e, axis=0, keepdims=True), 0.0)
    return e * scale


N_FORCED = 3


def _select_cols(p_slc, qpos, n_top):
    ns = p_slc.shape[0]
    blk = lax.broadcasted_iota(jnp.int32, p_slc.shape, 0)
    cur = qpos >> SLC_SHIFT
    forced = ((blk == 0) | (blk == cur) | (blk == cur - 1)) & (blk <= cur)
    score = jnp.where((blk <= cur) & jnp.logical_not(forced), p_slc, -jnp.inf)
    sel = jnp.where(forced, 1.0, 0.0)
    for _ in range(n_top - N_FORCED):
        m = jnp.max(score, axis=0, keepdims=True)
        idx = jnp.min(jnp.where(score == m, blk, ns), axis=0, keepdims=True)
        hit = blk == idx
        sel = jnp.where(hit & (m > -jnp.inf), 1.0, sel)
        score = jnp.where(hit, -jnp.inf, score)
    return sel


def _flash_init(m_sc, l_sc, acc_sc):
    m_sc[...] = jnp.full(m_sc.shape, NEG_BIG, F32)
    l_sc[...] = jnp.zeros(l_sc.shape, F32)
    acc_sc[...] = jnp.zeros(acc_sc.shape, F32)


def _flash_cols(scores, mask, pv, m_sc, l_sc, acc_sc):
    s = jnp.where(mask, scores, NEG_BIG)
    m_old = m_sc[...]
    m_new = jnp.maximum(m_old, jnp.max(s, axis=0, keepdims=True))
    alpha = jnp.exp(m_old - m_new)
    p = jnp.where(mask, jnp.exp(s - m_new), 0.0)
    l_sc[...] = alpha * l_sc[...] + jnp.sum(p, axis=0, keepdims=True)
    acc_sc[...] = alpha * acc_sc[...] + pv(p.astype(BF16))
    m_sc[...] = m_new


def _flash_stream(score_fn, pv_fn, first, lo, n, stream_sc, m_sc, l_sc, acc_sc):
    (sa, ca), (sb, cb) = stream_sc

    def issue(s_ref, c_ref, kt, self_tile=False):
        s = score_fn(kt, self_tile)
        s_ref[...] = s
        c_ref[...] = jnp.max(s, axis=0, keepdims=True)

    def consume(s_ref, c_ref, kt):
        m_old = m_sc[...]
        m_new = jnp.maximum(m_old, c_ref[...])
        alpha = jnp.exp(m_old - m_new)
        p = jnp.exp(s_ref[...] - m_new)
        l_sc[...] = alpha * l_sc[...] + jnp.sum(p, axis=0, keepdims=True)
        acc_sc[...] = alpha * acc_sc[...] + pv_fn(kt, p.astype(BF16))
        m_sc[...] = m_new

    _flash_init(m_sc, l_sc, acc_sc)
    issue(sa, ca, first, True)

    def two_tiles(jj, kt_a):
        t0 = lo + 2 * jj
        issue(sb, cb, t0)
        consume(sa, ca, kt_a)
        issue(sa, ca, t0 + 1)
        consume(sb, cb, t0)
        return t0 + 1

    kt_a = lax.fori_loop(0, n // 2, two_tiles, first)

    @pl.when(n % 2 == 1)
    def _():
        issue(sb, cb, lo + n - 1)
        consume(sa, ca, kt_a)
        consume(sb, cb, lo + n - 1)

    @pl.when(n % 2 == 0)
    def _():
        consume(sa, ca, kt_a)

    return acc_sc[...] / jnp.maximum(l_sc[...], 1e-30)


def _flash_out(l_sc, acc_sc):
    return acc_sc[...] / jnp.maximum(l_sc[...], 1e-30)


def _pv_split(vt, p):
    c2 = p.shape[1] // 2
    return jnp.concatenate([jnp.dot(vt[:HD], p[:, :c2], preferred_element_type=F32),
                            jnp.dot(vt[HD:], p[:, c2:], preferred_element_type=F32)], axis=1)


def _nsa_prompt_body(q_ref, gates_ref, ckv_ref, cos_ref, slo_ref, shi_ref, ks_ref, vst_ref, kw_ref, vwt_ref,
                     o_ref, m_sc, l_sc, acc_sc, sel_bias_sc, sa_sc, ca_sc, sb_sc, cb_sc):
    stream_sc = ((sa_sc, ca_sc), (sb_sc, cb_sc))
    i = pl.program_id(1)
    nq = Q_BLOCK
    cols = NSA_HEADS * nq
    qpos = i * nq + lax.broadcasted_iota(jnp.int32, (1, nq), 1)
    q = q_ref[...].astype(F32)
    pairs = [q[:, j * LANES:(j + 1) * LANES].T for j in range(NSA_HEADS // 2)]
    zero = jnp.zeros((HD, cols // 2), F32)
    qt_g = [jnp.concatenate([pairs[2 * g][:HD], pairs[2 * g][HD:], pairs[2 * g + 1][:HD], pairs[2 * g + 1][HD:]],
                            axis=1) for g in range(NSA_KV)]
    qt = jnp.concatenate([jnp.concatenate([qt_g[0], zero], axis=1),
                          jnp.concatenate([zero, qt_g[1]], axis=1)], axis=0).astype(BF16)
    ckv = ckv_ref[0]
    ck = _rope_lanes(ckv[:, :KV_HALF], cos_ref[...], slo_ref[...], shi_ref[...]).astype(BF16)
    cvt = ckv[:, KV_HALF:].T.astype(BF16)
    c_bias = jnp.where(_even_first_cmp_end(N_CMP_PROMPT) <= qpos, 0.0, NEG_BIG)
    p = _softmax_cols_biased(jnp.dot(ck, qt, preferred_element_type=F32) + _tile_cols(c_bias, NSA_HEADS))
    o_c = _pv_split(cvt, p.astype(BF16))
    sel = []
    for g in range(NSA_KV):
        c0 = g * NSA_REP * nq
        p_grp = p[:, c0:c0 + nq]
        for r in range(1, NSA_REP):
            p_grp = p_grp + p[:, c0 + r * nq:c0 + (r + 1) * nq]
        p_slc = p_grp[:N_CMP_PROMPT // 2] + p_grp[N_CMP_PROMPT // 2:]
        sel.append(_select_cols(p_slc, qpos, SLC_TOPK))
    sel = jnp.concatenate(sel, axis=1)
    sel_bias_sc[...] = jnp.where(sel > 0.5, 0.0, NEG_BIG)
    key_row = lax.broadcasted_iota(jnp.int32, (KEY_TILE, 1), 0)
    blocks_per_tile = KEY_TILE // SLC_BLOCK
    kt_self = i // (KEY_TILE // nq)

    def slc_scores(kt, self_tile):
        k0 = pl.multiple_of(kt * KEY_TILE, KEY_TILE)
        rows = []
        for j in range(blocks_per_tile):
            b = sel_bias_sc[pl.ds(kt * blocks_per_tile + j, 1), :]
            rows.append(jnp.concatenate(
                [jnp.broadcast_to(b[:, g * nq:(g + 1) * nq], (SLC_BLOCK, nq)) for g in range(NSA_KV)
                 for _ in range(NSA_REP)], axis=1))
        bias = jnp.concatenate(rows, axis=0)
        if self_tile:
            bias = bias + _tile_cols(jnp.where(k0 + key_row <= qpos, 0.0, NEG_BIG), NSA_HEADS)
        return jnp.dot(ks_ref[pl.ds(k0, KEY_TILE), :], qt, preferred_element_type=F32) + bias

    def slc_pv(kt, pb):
        return _pv_split(vst_ref[:, pl.ds(pl.multiple_of(kt * KEY_TILE, KEY_TILE), KEY_TILE)], pb)

    o_s = _flash_stream(slc_scores, slc_pv, kt_self, 0, kt_self, stream_sc, m_sc, l_sc, acc_sc)

    def win_scores(kt, self_tile):
        k0 = pl.multiple_of(kt * KEY_TILE, KEY_TILE)
        d = qpos - (k0 + key_row)
        bias = jnp.where((d >= 0) & (d < WINDOW), 0.0, NEG_BIG)
        return jnp.dot(kw_ref[pl.ds(k0, KEY_TILE), :], qt, preferred_element_type=F32) + _tile_cols(bias, NSA_HEADS)

    def win_pv(kt, pb):
        return _pv_split(vwt_ref[:, pl.ds(pl.multiple_of(kt * KEY_TILE, KEY_TILE), KEY_TILE)], pb)

    win_lo = jnp.maximum(i - WINDOW // nq, 0) // (KEY_TILE // nq)
    o_w = _flash_stream(win_scores, win_pv, kt_self, win_lo, kt_self - win_lo, stream_sc, m_sc, l_sc, acc_sc)
    gates_t = gates_ref[...].T
    merged = []
    for hd in range(NSA_HEADS):
        cs = slice(hd * nq, (hd + 1) * nq)
        gate = lambda br: gates_t[3 * hd + br:3 * hd + br + 1]
        merged.append(gate(0) * o_c[:, cs] + gate(1) * o_s[:, cs] + gate(2) * o_w[:, cs])
    for j in range(NSA_HEADS // 2):
        pair = jnp.concatenate([merged[2 * j], merged[2 * j + 1]], axis=0).T
        o_ref[:, j * LANES:(j + 1) * LANES] = pair.astype(BF16)


def nsa_prompt_attn(q, gates, ckv, ks, vst, kw, vwt):
    nqb = SEQ // Q_BLOCK
    tok = lambda wd: pl.BlockSpec((Q_BLOCK, wd), lambda b, i: (b * nqb + i, 0))
    seq_rows = pl.BlockSpec((SEQ, KV_HALF), lambda b, i: (b, 0))
    seq_cols = pl.BlockSpec((KV_HALF, SEQ), lambda b, i: (0, b))
    n_cmp = N_CMP_PROMPT
    c_blk = np.concatenate([np.arange(0, n_cmp, 2), np.arange(1, n_cmp, 2)])
    cols = NSA_HEADS * Q_BLOCK
    return pl.pallas_call(
        _nsa_prompt_body,
        grid=(BATCH, nqb),
        in_specs=[tok(NSA_Q_W), tok(LANES), pl.BlockSpec((1, n_cmp, NSA_KV_W), lambda b, i: (b, 0, 0)),
                  _full((n_cmp, LANES)), _full((n_cmp, LANES)), _full((n_cmp, LANES)),
                  seq_rows, seq_cols, seq_rows, seq_cols],
        out_specs=tok(NSA_Q_W),
        out_shape=jax.ShapeDtypeStruct((N_PROMPT, NSA_Q_W), BF16),
        scratch_shapes=[pltpu.VMEM((1, cols), F32), pltpu.VMEM((1, cols), F32), pltpu.VMEM((HD, cols), F32),
                        pltpu.VMEM((N_SLC_PROMPT, NSA_KV * Q_BLOCK), F32),
                        pltpu.VMEM((KEY_TILE, cols), F32), pltpu.VMEM((1, cols), F32),
                        pltpu.VMEM((KEY_TILE, cols), F32), pltpu.VMEM((1, cols), F32)],
        compiler_params=_params(2),
        name="nsa_prompt",
    )(q, gates, ckv, *_rope_tables((c_blk + 1) * CMP_BLOCK - 1), ks, vst, kw, vwt)


def _nsa_decode_body(pt_ref, qt_ref, gate_ref, ckv_ref, cos_ref, slo_ref, shi_ref, hsum_ref, *refs):
    page_refs = refs[:N_PAGES]
    win_ref, kvs_new_ref, kvw_new_ref, o_ref, win_out_ref, m_sc, l_sc, acc_sc = refs[N_PAGES:]
    qt = qt_ref[0]
    lane = lax.broadcasted_iota(jnp.int32, (1, LANES), 1)
    qpos = PAST_LEN + (lane & (DEC_SEQ - 1))
    group0 = lane < DEC_COLS_PER_GROUP
    ckv = ckv_ref[0]
    ck = _rope_lanes(ckv[:, :KV_HALF], cos_ref[...], slo_ref[...], shi_ref[...]).astype(BF16)
    c_mask = _even_first_cmp_end(N_CMP_DEC) <= qpos
    p = _softmax_cols(jnp.dot(ck, qt, preferred_element_type=F32), c_mask)
    o_c = _tn_dot(ckv[:, KV_HALF:].astype(BF16), p.astype(BF16))
    p_grp = jnp.dot(p, hsum_ref[...], preferred_element_type=F32, precision=lax.Precision.HIGHEST)
    p_slc = jnp.concatenate([p_grp[:N_CMP_DEC // 2] + p_grp[N_CMP_DEC // 2:],
                             jnp.zeros((N_SLC_DEC_PAD - N_CMP_DEC // 2, LANES), F32)], axis=0)
    sel = _select_cols(p_slc, qpos, SLC_TOPK)

    def new_rows(ref):
        row = ref[0]
        kv = jnp.concatenate([row[:, t * NSA_KV_W:(t + 1) * NSA_KV_W] for t in range(DEC_SEQ)], axis=0)
        return jnp.concatenate([kv, jnp.zeros((SUBLANES - DEC_SEQ, NSA_KV_W), F32)], axis=0)

    new_row = lax.broadcasted_iota(jnp.int32, (SUBLANES, 1), 0)
    new_pos = PAST_LEN + new_row
    new_valid = new_row < DEC_SEQ
    _flash_init(m_sc, l_sc, acc_sc)
    kt_old = jnp.concatenate([r[0, :KV_HALF, :] for r in page_refs], axis=1).astype(BF16)
    vt_old = jnp.concatenate([r[0, KV_HALF:, :] for r in page_refs], axis=1).astype(BF16)
    key_blk = lax.broadcasted_iota(jnp.int32, (PAST_LEN, N_SLC_DEC_PAD), 0) >> SLC_SHIFT
    blk_col = lax.broadcasted_iota(jnp.int32, (PAST_LEN, N_SLC_DEC_PAD), 1)
    chosen = jnp.dot(jnp.where(key_blk == blk_col, 1.0, 0.0).astype(BF16), sel.astype(BF16),
                     preferred_element_type=F32) > 0.5
    old_pos = lax.broadcasted_iota(jnp.int32, (PAST_LEN, 1), 0)
    _flash_cols(_tn_dot(kt_old, qt), chosen & (old_pos <= qpos),
                lambda pb: jnp.dot(vt_old, pb, preferred_element_type=F32), m_sc, l_sc, acc_sc)
    kv_new = new_rows(kvs_new_ref)
    sel_new = sel[(PAST_LEN >> SLC_SHIFT):(PAST_LEN >> SLC_SHIFT) + 1] > 0.5
    v_new = kv_new[:, KV_HALF:].astype(BF16)
    _flash_cols(jnp.dot(kv_new[:, :KV_HALF].astype(BF16), qt, preferred_element_type=F32),
                sel_new & new_valid & (new_pos <= qpos), lambda pb: _tn_dot(v_new, pb), m_sc, l_sc, acc_sc)
    o_s = _flash_out(l_sc, acc_sc)
    _flash_init(m_sc, l_sc, acc_sc)
    n_buf = win_ref.shape[2]
    win = win_ref[0]
    d = qpos - (PAST_LEN - n_buf + lax.broadcasted_iota(jnp.int32, (n_buf, 1), 0))
    vt_win = win[KV_HALF:].astype(BF16)
    _flash_cols(_tn_dot(win[:KV_HALF].astype(BF16), qt), (d >= 0) & (d < WINDOW),
                lambda pb: jnp.dot(vt_win, pb, preferred_element_type=F32), m_sc, l_sc, acc_sc)
    kw_new = new_rows(kvw_new_ref)
    d = qpos - new_pos
    vw_new = kw_new[:, KV_HALF:].astype(BF16)
    _flash_cols(jnp.dot(kw_new[:, :KV_HALF].astype(BF16), qt, preferred_element_type=F32),
                new_valid & (d >= 0) & (d < WINDOW), lambda pb: _tn_dot(vw_new, pb), m_sc, l_sc, acc_sc)
    o_w = _flash_out(l_sc, acc_sc)
    g = gate_ref[0]
    o = g[0:1] * o_c + g[1:2] * o_s + g[2:3] * o_w
    o_ref[0] = jnp.where(group0, o[:HD], o[HD:])
    key = lax.broadcasted_iota(jnp.int32, (SUBLANES, n_buf), 1)
    place = jnp.where((key == n_buf - DEC_SEQ + new_row) & new_valid, 1.0, 0.0)
    placed = lax.dot_general(kw_new, place, (((0,), (0,)), ((), ())), preferred_element_type=F32,
                             precision=lax.Precision.HIGHEST)
    keep = lax.broadcasted_iota(jnp.int32, (1, n_buf), 1) < n_buf - DEC_SEQ
    win_out_ref[0] = jnp.where(keep, pltpu.roll(win, n_buf - DEC_SEQ, 1), placed)


def nsa_decode_attn(qt, gate_rows, ckv, slc_pool, win_buf, kvs_new, kvw_new, page_table):
    n_buf = win_buf.shape[2]
    per_b = lambda *shape: pl.BlockSpec((1,) + shape, lambda b, pt: (b,) + (0,) * len(shape))
    const = lambda *shape: pl.BlockSpec(shape, lambda b, pt: (0,) * len(shape))
    page = lambda j: pl.BlockSpec((1, NSA_KV_W, PAGE_SIZE), lambda b, pt: (pt[b, j], 0, 0))
    c_blk = np.concatenate([np.arange(0, N_CMP_DEC, 2), np.arange(1, N_CMP_DEC, 2)])
    col = jnp.arange(LANES)
    used = col < NSA_KV * DEC_COLS_PER_GROUP
    same = (col[:, None] // DEC_COLS_PER_GROUP == col[None, :] // DEC_COLS_PER_GROUP) & \
           (col[:, None] % DEC_SEQ == col[None, :] % DEC_SEQ) & used[:, None] & used[None, :]
    return pl.pallas_call(
        _nsa_decode_body,
        grid_spec=pltpu.PrefetchScalarGridSpec(
            num_scalar_prefetch=1,
            grid=(DEC_BATCH,),
            in_specs=[per_b(LANES, LANES), per_b(3, LANES), per_b(N_CMP_DEC, NSA_KV_W),
                      const(N_CMP_DEC, LANES), const(N_CMP_DEC, LANES), const(N_CMP_DEC, LANES),
                      const(LANES, LANES)] + [page(j) for j in range(N_PAGES)]
                     + [per_b(NSA_KV_W, n_buf), per_b(1, DEC_SEQ * NSA_KV_W), per_b(1, DEC_SEQ * NSA_KV_W)],
            out_specs=[per_b(HD, LANES), per_b(NSA_KV_W, n_buf)],
            scratch_shapes=[pltpu.VMEM((1, LANES), F32), pltpu.VMEM((1, LANES), F32),
                            pltpu.VMEM((LANES, LANES), F32)],
        ),
        out_shape=[jax.ShapeDtypeStruct((DEC_BATCH, HD, LANES), F32),
                   jax.ShapeDtypeStruct((DEC_BATCH, NSA_KV_W, n_buf), F32)],
        compiler_params=_params(1),
        name="nsa_decode",
    )(page_table, qt, gate_rows, ckv, *_rope_tables((c_blk + 1) * CMP_BLOCK - 1), same.astype(F32),
      *([slc_pool] * N_PAGES), win_buf, kvs_new, kvw_new)


GLA_PAIRS = GLA_HEADS // 2
GLA_ROWS = 128
GLA_SUB = 16
GLA_STEP_SEQS = 8


def _gla_rows(q, k, la, v, sgg, gnorm, st_ref, sub):
    R = q.shape[0]
    row = lax.broadcasted_iota(jnp.int32, (R, GLA_K_W), 0)
    rin = row % sub
    cum = la
    d = 1
    while d < sub:
        cum = cum + jnp.where(rin >= d, pltpu.roll(cum, d, 0), 0.0)
        d *= 2
    lane = lax.broadcasted_iota(jnp.int32, (sub, LANES), 1)
    lo = lane < GLA_DK
    rsub = lax.broadcasted_iota(jnp.int32, (sub, LANES), 0)
    out_rows = []
    for c in range(R // sub):
        rs = slice(c * sub, (c + 1) * sub)
        cum_c = cum[rs]
        last = cum_c[sub - 1:sub]
        qe = q[rs] * jnp.exp(cum_c)
        kdec = k[rs] * jnp.exp(last - cum_c)
        v_c = v[rs]
        heads = []
        for pr in range(GLA_PAIRS):
            ls = slice(pr * LANES, (pr + 1) * LANES)
            st = st_ref[pr]
            st_b = st.astype(BF16)
            qe_p, kd_p, q_p, k_p, cum_p = qe[:, ls], kdec[:, ls], q[rs, ls], k[rs, ls], cum_c[:, ls]
            v_pair = [v_c[:, (2 * pr + hh) * GLA_DV:(2 * pr + hh + 1) * GLA_DV] for hh in range(2)]
            upd = jnp.zeros((GLA_DV, LANES), F32)
            o_pair = []
            for hh in range(2):
                keep = lo if hh == 0 else jnp.logical_not(lo)
                o_pair.append(_nt_dot(jnp.where(keep, qe_p, 0.0).astype(BF16), st_b))
                upd = upd + _tn_dot(v_pair[hh].astype(BF16), jnp.where(keep, kd_p, 0.0).astype(BF16))
            for j in range(sub):
                dj = jnp.where(rsub >= j, jnp.exp(cum_p - cum_p[j:j + 1]), 0.0)
                w = q_p * k_p[j:j + 1] * dj
                a_lo = jnp.sum(jnp.where(lo, w, 0.0), axis=-1, keepdims=True)
                a_hi = jnp.sum(jnp.where(lo, 0.0, w), axis=-1, keepdims=True)
                o_pair[0] = o_pair[0] + a_lo * v_pair[0][j:j + 1]
                o_pair[1] = o_pair[1] + a_hi * v_pair[1][j:j + 1]
            st_ref[pr] = st * jnp.exp(last[:, ls]) + upd
            heads += o_pair
        out_rows.append(jnp.concatenate([_rms(x, gnorm) for x in heads], axis=1))
    return jnp.concatenate(out_rows, axis=0) * sgg


def _gla_seq_body(q_ref, k_ref, la_ref, v_ref, sgg_ref, gn_ref, o_ref, st_out_ref, st_sc):
    @pl.when(pl.program_id(1) == 0)
    def _():
        st_sc[...] = jnp.zeros(st_sc.shape, F32)

    o = _gla_rows(q_ref[...], k_ref[...], la_ref[...], v_ref[...].astype(F32), sgg_ref[...], gn_ref[...],
                  st_sc, GLA_SUB)
    o_ref[...] = o.astype(BF16)
    st_out_ref[0] = st_sc[...]


def gla_seq(q, k, la, v, sgg, gnorm):
    nt = SEQ // GLA_ROWS
    rows = lambda wd: pl.BlockSpec((GLA_ROWS, wd), lambda b, t: (b * nt + t, 0))
    return pl.pallas_call(
        _gla_seq_body,
        grid=(BATCH, nt),
        in_specs=[rows(GLA_K_W), rows(GLA_K_W), rows(GLA_K_W), rows(GLA_V_W), rows(GLA_V_W),
                  _full((1, GLA_DV))],
        out_specs=[rows(GLA_V_W), pl.BlockSpec((1, GLA_PAIRS, GLA_DV, LANES), lambda b, t: (b, 0, 0, 0))],
        out_shape=[jax.ShapeDtypeStruct((N_PROMPT, GLA_V_W), BF16),
                   jax.ShapeDtypeStruct((BATCH, GLA_PAIRS, GLA_DV, LANES), F32)],
        scratch_shapes=[pltpu.VMEM((GLA_PAIRS, GLA_DV, LANES), F32)],
        compiler_params=_params(2),
        name="gla_seq",
    )(q, k, la, v, sgg, gnorm.reshape(1, GLA_DV))


def _gla_step_body(q_ref, k_ref, la_ref, v_ref, sgg_ref, gn_ref, st_in_ref, o_ref, st_out_ref):
    st_out_ref[...] = st_in_ref[...]
    q, k, la, v, sgg = q_ref[...], k_ref[...], la_ref[...], v_ref[...].astype(F32), sgg_ref[...]
    for j in range(GLA_STEP_SEQS):
        rs = slice(j * DEC_SEQ, (j + 1) * DEC_SEQ)
        o = _gla_rows(q[rs], k[rs], la[rs], v[rs], sgg[rs], gn_ref[...], st_out_ref.at[j], DEC_SEQ)
        o_ref[rs, :] = o.astype(BF16)


def gla_step(q, k, la, v, sgg, gnorm, st_in, row0):
    rows_per = GLA_STEP_SEQS * DEC_SEQ
    blk0 = row0 // rows_per
    rows = lambda wd: pl.BlockSpec((rows_per, wd), lambda i: (blk0 + i, 0))
    st_spec = pl.BlockSpec((GLA_STEP_SEQS, GLA_PAIRS, GLA_DV, LANES), lambda i: (i, 0, 0, 0))
    return pl.pallas_call(
        _gla_step_body,
        grid=(DEC_BATCH // GLA_STEP_SEQS,),
        in_specs=[rows(GLA_K_W), rows(GLA_K_W), rows(GLA_K_W), rows(GLA_V_W), rows(GLA_V_W),
                  _full((1, GLA_DV)), st_spec],
        out_specs=[pl.BlockSpec((rows_per, GLA_V_W), lambda i: (i, 0)), st_spec],
        out_shape=[jax.ShapeDtypeStruct((N_SAMPLE, GLA_V_W), BF16),
                   jax.ShapeDtypeStruct((DEC_BATCH, GLA_PAIRS, GLA_DV, LANES), F32)],
        compiler_params=_params(1),
        name="gla_step",
    )(q, k, la, v, sgg, gnorm.reshape(1, GLA_DV), st_in)


def _gla_state_to_pairs(s):
    B = s.shape[0]
    return s.reshape(B, GLA_PAIRS, 2, GLA_DK, GLA_DV).transpose(0, 1, 4, 2, 3).reshape(B, GLA_PAIRS, GLA_DV, LANES)


def _gla_state_from_pairs(st):
    B = st.shape[0]
    return st.reshape(B, GLA_PAIRS, GLA_DV, 2, GLA_DK).transpose(0, 1, 3, 4, 2).reshape(B, GLA_HEADS, GLA_DK, GLA_DV)


LRU_TIME_TILE = 256


def _lru_gates(xc, wa_ref, ba, wx_ref, bx, lam):
    xcb = xc.astype(BF16)
    r_parts, i_parts = [], []
    for n in range(LRU_BLOCKS):
        xs = xcb[:, n * LRU_BW:(n + 1) * LRU_BW]
        r_parts.append(jnp.dot(xs, wa_ref[n], preferred_element_type=F32))
        i_parts.append(jnp.dot(xs, wx_ref[n], preferred_element_type=F32))
    r = jax.nn.sigmoid(jnp.concatenate(r_parts, axis=-1) + ba)
    i = jax.nn.sigmoid(jnp.concatenate(i_parts, axis=-1) + bx)
    log_a = -LRU_C * r * _softplus(-lam)
    a = jnp.exp(log_a)
    u = jnp.sqrt(1.0 - a * a) * (i * xc)
    return a, u


def _lru_seq_body(u_ref, cw_ref, cb_ref, wa_ref, ba_ref, wx_ref, bx_ref, lam_ref, y_ref, hT_ref, xp_sc, h_sc):
    tt = LRU_TIME_TILE

    @pl.when(pl.program_id(1) == 0)
    def _():
        xp_sc[0:SUBLANES, :] = jnp.zeros((SUBLANES, D_RNN), F32)
        h_sc[...] = jnp.zeros((1, D_RNN), F32)

    xp_sc[SUBLANES:SUBLANES + tt, :] = u_ref[:, D_RNN:]
    xc = cb_ref[...]
    for w in range(CONV_W):
        off = SUBLANES - (CONV_W - 1) + w
        xc = xc + cw_ref[w:w + 1, :] * xp_sc[off:off + tt, :]
    a, u = _lru_gates(xc, wa_ref, ba_ref[...], wx_ref, bx_ref[...], lam_ref[...])
    row = lax.broadcasted_iota(jnp.int32, (tt, D_RNN), 0) % SUBLANES
    d = 1
    while d < SUBLANES:
        keep = row >= d
        a_prev = jnp.where(keep, pltpu.roll(a, d, 0), 1.0)
        u_prev = jnp.where(keep, pltpu.roll(u, d, 0), 0.0)
        u = a * u_prev + u
        a = a * a_prev
        d *= 2
    carry = h_sc[...]
    hs = []
    for grp in range(tt // SUBLANES):
        rs = slice(grp * SUBLANES, (grp + 1) * SUBLANES)
        hs.append(a[rs] * carry + u[rs])
        carry = hs[-1][SUBLANES - 1:SUBLANES, :]
    h_sc[...] = carry
    hT_ref[0] = carry
    y_ref[...] = (_gelu_tanh(u_ref[:, :D_RNN]) * jnp.concatenate(hs, axis=0)).astype(BF16)
    xp_sc[0:SUBLANES, :] = xp_sc[tt:tt + SUBLANES, :]


def _lru_weight_args(cw, cb, wa, ba, wx, bx, lam):
    row = lambda v: v.reshape(1, D_RNN)
    return (cw, row(cb), wa.astype(BF16), row(ba), wx.astype(BF16), row(bx), row(lam))


_LRU_WEIGHT_SPECS = [_full((CONV_W, D_RNN)), _full((1, D_RNN)), _full((LRU_BLOCKS, LRU_BW, LRU_BW)),
                     _full((1, D_RNN)), _full((LRU_BLOCKS, LRU_BW, LRU_BW)), _full((1, D_RNN)),
                     _full((1, D_RNN))]


def lru_seq(u, cw, cb, wa, ba, wx, bx, lam):
    tt = LRU_TIME_TILE
    nt = SEQ // tt
    y, hT = pl.pallas_call(
        _lru_seq_body,
        grid=(BATCH, nt),
        in_specs=[pl.BlockSpec((tt, 2 * D_RNN), lambda b, t: (b * nt + t, 0))] + _LRU_WEIGHT_SPECS,
        out_specs=[pl.BlockSpec((tt, D_RNN), lambda b, t: (b * nt + t, 0)),
                   pl.BlockSpec((1, 1, D_RNN), lambda b, t: (b, 0, 0))],
        out_shape=[jax.ShapeDtypeStruct((N_PROMPT, D_RNN), BF16), jax.ShapeDtypeStruct((BATCH, 1, D_RNN), F32)],
        scratch_shapes=[pltpu.VMEM((tt + SUBLANES, D_RNN), F32), pltpu.VMEM((1, D_RNN), F32)],
        compiler_params=_params(2),
        name="lru_seq",
    )(u, *_lru_weight_args(cw, cb, wa, ba, wx, bx, lam))
    return y, hT.reshape(BATCH, D_RNN)


def _lru_step_body(u_ref, cs_ref, h0_ref, cw_ref, cb_ref, wa_ref, ba_ref, wx_ref, bx_ref, lam_ref, y_ref, hT_ref):
    n_t = u_ref.shape[0]
    hist = [cs_ref[:, w, :] for w in range(CONV_W - 1)] + [u_ref[t, :, D_RNN:] for t in range(n_t)]
    h = h0_ref[...]
    for t in range(n_t):
        xc = cb_ref[...]
        for w in range(CONV_W):
            xc = xc + cw_ref[w:w + 1, :] * hist[t + w]
        a, u = _lru_gates(xc, wa_ref, ba_ref[...], wx_ref, bx_ref[...], lam_ref[...])
        h = a * h + u
        y_ref[t] = (_gelu_tanh(u_ref[t, :, :D_RNN]) * h).astype(BF16)
    hT_ref[...] = h


def lru_step(u, conv_state, h0, cw, cb, wa, ba, wx, bx, lam):
    T, B, _ = u.shape
    return pl.pallas_call(
        _lru_step_body,
        out_shape=[jax.ShapeDtypeStruct((T, B, D_RNN), BF16), jax.ShapeDtypeStruct((B, D_RNN), F32)],
        compiler_params=pltpu.CompilerParams(vmem_limit_bytes=VMEM_LIMIT_BYTES),
        name="lru_step",
    )(u, conv_state, h0, *_lru_weight_args(cw, cb, wa, ba, wx, bx, lam))


ROUTE_E1, ROUTE_E2, ROUTE_G1, ROUTE_G2, ROUTE_R1, ROUTE_R2 = range(6)
EXPERT_LANE0 = N_GROUPS


def _lane_pick(val_by_lane):
    rows = next(iter(val_by_lane.values())).shape[0]
    lane = lax.broadcasted_iota(jnp.int32, (rows, LANES), 1)
    out = jnp.zeros((rows, LANES), F32)
    for l, v in val_by_lane.items():
        out = jnp.where(lane == l, v, out)
    return out


def _route_rows(logits, tri_ref, carry_ref):
    rows = logits.shape[0]
    lane = lax.broadcasted_iota(jnp.int32, (rows, LANES), 1)
    neg = -jnp.inf
    gl = jnp.where(lane < N_GROUPS, logits, neg)
    gmax = jnp.max(gl, axis=-1, keepdims=True)
    gtop = jnp.min(jnp.where(gl == gmax, lane, LANES), axis=-1, keepdims=True)
    gsum = jnp.sum(jnp.where(lane < N_GROUPS, jnp.exp(logits - gmax), 0.0), axis=-1, keepdims=True)
    g_w = 1.0 / gsum
    lo = EXPERT_LANE0 + EXP_PER_GROUP * gtop
    el = jnp.where((lane >= lo) & (lane < lo + EXP_PER_GROUP), logits, neg)
    v1 = jnp.max(el, axis=-1, keepdims=True)
    i1 = jnp.min(jnp.where(el == v1, lane, LANES), axis=-1, keepdims=True)
    el2 = jnp.where(lane == i1, neg, el)
    v2 = jnp.max(el2, axis=-1, keepdims=True)
    i2 = jnp.min(jnp.where(el2 == v2, lane, LANES), axis=-1, keepdims=True)
    p2 = jnp.exp(v2 - v1)
    den = 1.0 + p2
    gate1 = (1.0 / den) * g_w
    gate2 = (p2 / den) * g_w
    hit1 = lane == i1
    hit2 = lane == i2
    onehot = jnp.where(hit1 | hit2, 1.0, 0.0)
    before = jnp.dot(tri_ref[...], onehot.astype(BF16), preferred_element_type=F32) + carry_ref[...]
    rank1 = jnp.sum(jnp.where(hit1, before, 0.0), axis=-1, keepdims=True)
    rank2 = jnp.sum(jnp.where(hit2, before, 0.0), axis=-1, keepdims=True)
    carry_ref[...] = carry_ref[...] + jnp.sum(onehot, axis=0, keepdims=True)
    return _lane_pick({ROUTE_E1: (i1 - EXPERT_LANE0).astype(F32), ROUTE_E2: (i2 - EXPERT_LANE0).astype(F32),
                       ROUTE_G1: gate1, ROUTE_G2: gate2, ROUTE_R1: rank1, ROUTE_R2: rank2})


def _pack_bf16_halves(x):
    w = x.shape[1] // 2
    bits = lambda v: pltpu.bitcast(v.astype(F32), jnp.uint32)
    return pltpu.bitcast((bits(x[:, :w]) >> 16) | bits(x[:, w:]), F32)


def _unpack_bf16_halves(words):
    p = pltpu.bitcast(words, jnp.uint32)
    lo = pltpu.bitcast(p << 16, F32).astype(BF16)
    hi = pltpu.bitcast(p & jnp.uint32(0xFFFF0000), F32).astype(BF16)
    return jnp.concatenate([lo, hi], axis=1)


def _post_mixer_body(n_h, n_mix, *refs):
    h_refs, refs = refs[:n_h], refs[n_h:]
    m_refs, wo_refs = refs[:2 * n_mix], refs[2 * n_mix:3 * n_mix]
    g_ref, wr_ref, br_ref, tri_ref, h1_ref, hn_ref, route_ref, cnt_ref, carry_sc = refs[3 * n_mix:]

    @pl.when(pl.program_id(0) == 0)
    def _():
        carry_sc[...] = jnp.zeros((1, LANES), F32)

    mix = None
    for j, wo_ref in enumerate(wo_refs):
        part = jnp.dot(_token_rows(m_refs[2 * j], m_refs[2 * j + 1]), wo_ref[...], preferred_element_type=F32)
        mix = part if mix is None else mix + part
    h1 = (h_refs[0][...] if n_h == 1 else _token_rows(*h_refs)) + mix
    h1_ref[...] = h1
    hn = _rms(h1, g_ref[...]).astype(BF16)
    hn_ref[...] = _pack_bf16_halves(hn)
    logits = jnp.dot(hn, wr_ref[...], preferred_element_type=F32) + br_ref[...]
    route_ref[...] = _route_rows(logits, tri_ref, carry_sc)
    cnt_ref[...] = carry_sc[...]


def post_mixer(h, mix_ins, w_out, g_ffn, w_rg, b_rg, w_re, b_re):
    hs = list(h) if isinstance(h, (tuple, list)) else [h]
    h_specs = [_PROMPT_ROWS, _SAMPLE_ROWS] if len(hs) == 2 else [_rows(D_MODEL)]
    n = N_TOK
    ks = [m[0].shape[1] for m in mix_ins]
    two_part = lambda k: [pl.BlockSpec((ROW_TILE, k), lambda i: (jnp.minimum(i, N_PROMPT_TILES - 1), 0)),
                          pl.BlockSpec((ROW_TILE, k), lambda i: (0, 0))]
    offs = np.cumsum([0] + ks)
    w_parts = [w_out[offs[j]:offs[j + 1]].astype(BF16) for j in range(len(ks))]
    pad = LANES - N_GROUPS - N_EXPERTS
    wr = jnp.concatenate([w_rg, w_re, jnp.zeros((D_MODEL, pad), F32)], axis=1).astype(BF16)
    br = jnp.concatenate([b_rg, b_re, jnp.zeros((pad,), F32)]).reshape(1, LANES)
    tri = jnp.tril(jnp.ones((ROW_TILE, ROW_TILE), BF16), -1)
    return pl.pallas_call(
        functools.partial(_post_mixer_body, len(hs), len(ks)),
        grid=(n // ROW_TILE,),
        in_specs=h_specs + [s for k in ks for s in two_part(k)] + [_full((k, D_MODEL)) for k in ks]
                 + [_full((1, D_MODEL)), _full((D_MODEL, LANES)), _full((1, LANES)), _full((ROW_TILE, ROW_TILE))],
        out_specs=[_rows(D_MODEL), _rows(D_MODEL // 2), _rows(LANES), _full((1, LANES))],
        out_shape=[jax.ShapeDtypeStruct((n, D_MODEL), F32), jax.ShapeDtypeStruct((n, D_MODEL // 2), F32),
                   jax.ShapeDtypeStruct((n, LANES), F32), jax.ShapeDtypeStruct((1, LANES), F32)],
        scratch_shapes=[pltpu.VMEM((1, LANES), F32)],
        compiler_params=_params(1),
        name="post_mixer",
    )(*hs, *[part for m in mix_ins for part in m], *w_parts, g_ffn.reshape(1, D_MODEL), wr, br, tri)


MOE_ROWS = 512


def _ffn_body(be_ref, nb_ref, x_ref, w1_ref, w3_ref, w2_ref, y_ref, w1_sc, w3_sc, w2_sc):
    i = pl.program_id(0)
    new_expert = jnp.logical_or(i == 0, be_ref[i] != be_ref[jnp.maximum(i - 1, 0)])

    @pl.when(jnp.logical_and(new_expert, i < nb_ref[0]))
    def _():
        w1_sc[...] = w1_ref[0, 0].astype(BF16)
        w3_sc[...] = w3_ref[0, 0].astype(BF16)
        w2_sc[...] = w2_ref[0, 0].astype(BF16)

    @pl.when(i < nb_ref[0])
    def _():
        x = _unpack_bf16_halves(x_ref[...])
        a = jnp.dot(x, w1_sc[...], preferred_element_type=F32)
        b = jnp.dot(x, w3_sc[...], preferred_element_type=F32)
        hdn = (a * jax.nn.sigmoid(a) * b).astype(BF16)
        y_ref[...] = jnp.dot(hdn, w2_sc[...], preferred_element_type=F32)

    @pl.when(i >= nb_ref[0])
    def _():
        y_ref[...] = jnp.zeros(y_ref.shape, F32)


def expert_ffn(xs, blk_exp, n_active, w1, w3, w2, layer):
    n_slots = xs.shape[0]
    nb = n_slots // MOE_ROWS
    wmap = lambda i, be, na: (layer, be[i], 0, 0)
    xmap = lambda i, be, na: (jnp.minimum(i, na[0] - 1), 0)
    return pl.pallas_call(
        _ffn_body,
        grid_spec=pltpu.PrefetchScalarGridSpec(
            num_scalar_prefetch=2,
            grid=(nb,),
            in_specs=[pl.BlockSpec((MOE_ROWS, D_MODEL // 2), xmap),
                      pl.BlockSpec((1, 1, D_MODEL, E_HID), wmap),
                      pl.BlockSpec((1, 1, D_MODEL, E_HID), wmap),
                      pl.BlockSpec((1, 1, E_HID, D_MODEL), wmap)],
            out_specs=pl.BlockSpec((MOE_ROWS, D_MODEL), lambda i, be, na: (i, 0)),
            scratch_shapes=[pltpu.VMEM((D_MODEL, E_HID), BF16), pltpu.VMEM((D_MODEL, E_HID), BF16),
                            pltpu.VMEM((E_HID, D_MODEL), BF16)],
        ),
        out_shape=jax.ShapeDtypeStruct((n_slots, D_MODEL), F32),
        compiler_params=_params(1),
        name="expert_ffn",
    )(blk_exp, n_active, xs, w1, w3, w2)


def moe_dispatch(route, counts_row, n):
    counts = counts_row[0, EXPERT_LANE0:EXPERT_LANE0 + N_EXPERTS].astype(jnp.int32)
    padded = ((counts + MOE_ROWS - 1) // MOE_ROWS) * MOE_ROWS
    pad_end = jnp.cumsum(padded)
    pad_start = (pad_end - padded).astype(F32)
    expert_lane = lax.broadcasted_iota(jnp.int32, (1, N_EXPERTS), 1).astype(F32)
    dest = []
    for e_lane, r_lane in ((ROUTE_E1, ROUTE_R1), (ROUTE_E2, ROUTE_R2)):
        start = jnp.sum(jnp.where(route[:, e_lane:e_lane + 1] == expert_lane, pad_start[None, :], 0.0), axis=1)
        dest.append((start + route[:, r_lane]).astype(jnp.int32))
    nb = -(-(n * TOP_K_IN_GROUP) // MOE_ROWS) + N_EXPERTS
    n_slots = nb * MOE_ROWS
    tok = jnp.arange(n, dtype=jnp.int32)
    slot_tok = jnp.zeros((n_slots,), jnp.int32).at[jnp.concatenate(dest)].set(
        jnp.concatenate([tok] * TOP_K_IN_GROUP), unique_indices=True, mode='promise_in_bounds')
    blk_start = jnp.arange(nb, dtype=jnp.int32) * MOE_ROWS
    blk_exp = jnp.sum((pad_end[None, :] <= blk_start[:, None]).astype(jnp.int32), axis=1)
    blk_exp = jnp.minimum(blk_exp, N_EXPERTS - 1)
    n_active = (pad_end[-1] // MOE_ROWS).astype(jnp.int32).reshape(1)
    return slot_tok, dest, blk_exp, n_active


def moe_experts(hn, route, counts_row, w1, w3, w2, layer):
    n = hn.shape[0]
    slot_tok, dest, blk_exp, n_active = moe_dispatch(route, counts_row, n)
    xs = hn.at[slot_tok].get(mode='promise_in_bounds')
    ys = expert_ffn(xs, blk_exp, n_active, w1, w3, w2, layer)
    return [ys.at[d].get(mode='promise_in_bounds') for d in dest]


def _combine(h_ref, y1_ref, y2_ref, route_ref):
    lane = lax.broadcasted_iota(jnp.int32, (ROW_TILE, LANES), 1)
    r = route_ref[...]
    g1 = jnp.sum(jnp.where(lane == ROUTE_G1, r, 0.0), axis=-1, keepdims=True)
    g2 = jnp.sum(jnp.where(lane == ROUTE_G2, r, 0.0), axis=-1, keepdims=True)
    return h_ref[...] + (y1_ref[...] * g1 + y2_ref[...] * g2)


def _combine_proj_body(h_ref, y1_ref, y2_ref, route_ref, g_ref, w_ref, h2_ref, u_ref):
    h2 = _combine(h_ref, y1_ref, y2_ref, route_ref)
    h2_ref[...] = h2
    u_ref[...] = jnp.dot(_rms(h2, g_ref[...]).astype(BF16), w_ref[...], preferred_element_type=F32)


def _combine_norm_body(h_ref, y1_ref, y2_ref, route_ref, g_ref, yp_ref, ys_ref):
    y = _rms(_combine(h_ref, y1_ref, y2_ref, route_ref), g_ref[...])

    @pl.when(pl.program_id(0) < N_PROMPT_TILES)
    def _():
        yp_ref[...] = y

    @pl.when(pl.program_id(0) >= N_PROMPT_TILES)
    def _():
        ys_ref[...] = y


def combine_proj(h, ys, route, g, w):
    n = h.shape[0]
    nn = w.shape[1]
    return pl.pallas_call(
        _combine_proj_body,
        grid=(n // ROW_TILE,),
        in_specs=[_rows(D_MODEL), _rows(D_MODEL), _rows(D_MODEL), _rows(LANES), _full((1, D_MODEL)),
                  _full((D_MODEL, nn))],
        out_specs=[_rows(D_MODEL), _rows(nn)],
        out_shape=[jax.ShapeDtypeStruct((n, D_MODEL), F32), jax.ShapeDtypeStruct((n, nn), F32)],
        compiler_params=_params(1),
        name="combine_proj",
    )(h, *ys, route, g.reshape(1, D_MODEL), w.astype(BF16))


def combine_norm(h, ys, route, g):
    n = h.shape[0]
    return pl.pallas_call(
        _combine_norm_body,
        grid=(n // ROW_TILE,),
        in_specs=[_rows(D_MODEL), _rows(D_MODEL), _rows(D_MODEL), _rows(LANES), _full((1, D_MODEL))],
        out_specs=[_PROMPT_ROWS, _SAMPLE_ROWS],
        out_shape=[jax.ShapeDtypeStruct((N_PROMPT, D_MODEL), F32), jax.ShapeDtypeStruct((N_SAMPLE, D_MODEL), F32)],
        compiler_params=_params(1),
        name="combine_norm",
    )(h, *ys, route, g.reshape(1, D_MODEL))


def _even_first(x, axis):
    n = x.shape[axis]
    idx = jnp.concatenate([jnp.arange(0, n, 2), jnp.arange(1, n, 2)])
    return jnp.take(x, idx, axis=axis)


def _decode_query_cols(q_s, gates_s):
    B, T, G, R = DEC_BATCH, DEC_SEQ, NSA_KV, NSA_REP
    qg = q_s.reshape(B, T, G, R, HD).transpose(0, 2, 4, 3, 1).reshape(B, G, HD, R * T)
    qt = jnp.zeros((B, G, HD, G, R * T), BF16)
    for g in range(G):
        qt = qt.at[:, g, :, g, :].set(qg[:, g])
    qt = jnp.pad(qt.reshape(B, G * HD, G * R * T), ((0, 0), (0, 0), (0, LANES - G * R * T)))
    gr = gates_s[:, :3 * NSA_HEADS].reshape(B, T, NSA_HEADS, 3).transpose(0, 3, 2, 1).reshape(B, 3, NSA_HEADS * T)
    return qt, jnp.pad(gr, ((0, 0), (0, 0), (0, LANES - NSA_HEADS * T)))


def _feature_major(cache):
    lead, rows = cache.shape[:2]
    return jnp.transpose(cache, (0, 2, 3, 4, 1)).reshape(lead, NSA_KV_W, rows)


def _kv_rows_from_feature_major(xt, lead):
    x = xt.reshape(2, NSA_KV, HD, *lead)
    n = len(lead)
    return jnp.transpose(x, tuple(range(3, 3 + n)) + (0, 1, 2))


def mixer_a(h, p, past):
    pos = np.concatenate([np.arange(SEQ), PAST_LEN + np.tile(np.arange(DEC_SEQ), DEC_BATCH)])
    (q, kvc, kvs, kvw, kvct_p, kvct_s, kvst_p, kvst_s, kvwt_p, kvwt_s, ks, vst, kw, vwt, gates, gq, gk, gv,
     la, sgg) = inproj_a(
        *h, p['norm_mix'][0], p['a_w_in'][0], p['a_gla_wa2'][0], p['a_gla_ba'][0], pos)
    cmp_w, gnorm = _cmp_weights(p['a_cmp_pe'][0], p['a_cmp_w'][0]), p['a_gla_norm'][0]
    P = N_PROMPT
    ckv = cmp_blocks_rows(kvc, P, cmp_w)
    ckv = _even_first(ckv.reshape(BATCH, N_CMP_PROMPT, NSA_KV_W), 1)
    o_nsa_p = nsa_prompt_attn(q, gates, ckv, ks, vst, kw, vwt)
    o_gla_p, st_p = gla_seq(gq, gk, la, gv, sgg, gnorm)
    rows_p = lambda xt: jnp.transpose(xt.reshape(BATCH, 2, NSA_KV, HD, xt.shape[2]), (0, 4, 1, 2, 3))
    new_p = (rows_p(kvct_p), rows_p(kvst_p), rows_p(kvwt_p[:, :, SEQ - WINDOW:]), _gla_state_from_pairs(st_p))
    o_gla_s, st_s = gla_step(gq, gk, la, gv, sgg, gnorm, _gla_state_to_pairs(past['state_gla'][0]), P)
    n_pool = past['cache_cmp_kv'].shape[1]
    ckv_pool = cmp_blocks_pages(_feature_major(past['cache_cmp_kv'][0]), cmp_w)
    ckv_seq = ckv_pool.reshape(n_pool, PAGE_SIZE // CMP_BLOCK, NSA_KV_W)[past['page_table']]
    ckv_seq = _even_first(ckv_seq.reshape(DEC_BATCH, N_CMP_DEC, NSA_KV_W), 1)
    qt, gate_rows = _decode_query_cols(q[P:], gates[P:])
    per_seq = lambda x: x[P:].reshape(DEC_BATCH, 1, DEC_SEQ * NSA_KV_W)
    o_t, win_new = nsa_decode_attn(qt, gate_rows, ckv_seq, _feature_major(past['cache_slc_kv'][0]),
                                   _feature_major(past['cache_win_kv'][0]), per_seq(kvs), per_seq(kvw),
                                   past['page_table'])
    o_nsa_s = o_t[:, :, :NSA_HEADS * DEC_SEQ].reshape(DEC_BATCH, HD, NSA_HEADS, DEC_SEQ)
    o_nsa_s = o_nsa_s.transpose(0, 3, 2, 1).reshape(N_SAMPLE, NSA_Q_W).astype(BF16)
    n_buf = win_new.shape[2]
    win_s = jnp.transpose(win_new.reshape(DEC_BATCH, 2, NSA_KV, HD, n_buf), (0, 4, 1, 2, 3))
    new_s = (_kv_rows_from_feature_major(kvct_s[:, :N_SAMPLE], (DEC_BATCH, DEC_SEQ)),
             _kv_rows_from_feature_major(kvst_s[:, :N_SAMPLE], (DEC_BATCH, DEC_SEQ)), win_s,
             _gla_state_from_pairs(st_s))
    return (o_nsa_p, o_nsa_s), (o_gla_p, o_gla_s), new_p, new_s


def run_trunk(x_prompt, x_sample, p, past):
    h0 = (x_prompt.reshape(N_PROMPT, D_MODEL), x_sample.reshape(N_SAMPLE, D_MODEL))
    o_nsa, o_gla, new_p, new_s = mixer_a(h0, p, past)
    h, hn, route, counts = post_mixer(h0, [o_nsa, o_gla], p['a_w_out'][0], p['norm_ffn'][0], p['m_w_rg'][0],
                                      p['m_b_rg'][0], p['m_w_re'][0], p['m_b_re'][0])
    ys = moe_experts(hn, route, counts, p['m_w1'], p['m_w3'], p['m_w2'], 0)
    h, u = combine_proj(h, ys, route, p['norm_mix'][1], p['c_w_in'][0])
    lru_w = (p['c_conv_w'][0], p['c_conv_b'][0], p['c_w_a'][0], p['c_b_a'][0], p['c_w_x'][0], p['c_b_x'][0],
             p['c_lam'][0])
    us = u[N_PROMPT:].reshape(DEC_BATCH, DEC_SEQ, 2 * D_RNN)
    y_p, lru_p = lru_seq(u, *lru_w)
    y_s, lru_s = lru_step(jnp.swapaxes(us, 0, 1), past['state_conv'][0], past['state_lru'][0], *lru_w)
    conv_p = jnp.stack([u[(b + 1) * SEQ - (CONV_W - 1):(b + 1) * SEQ, D_RNN:] for b in range(BATCH)])
    conv_s = us[:, DEC_SEQ - (CONV_W - 1):, D_RNN:]
    mix_in = (y_p, jnp.swapaxes(y_s, 0, 1).reshape(N_SAMPLE, D_RNN))
    h, hn, route, counts = post_mixer(h, [mix_in], p['c_w_out'][0], p['norm_ffn'][1], p['m_w_rg'][1],
                                      p['m_b_rg'][1], p['m_w_re'][1], p['m_b_re'][1])
    ys = moe_experts(hn, route, counts, p['m_w1'], p['m_w3'], p['m_w2'], 1)
    y_p, y_s = combine_norm(h, ys, route, p['norm_final'])
    y_prompt = y_p.reshape(BATCH, SEQ, D_MODEL)
    y_sample = y_s.reshape(DEC_BATCH, DEC_SEQ, D_MODEL)
    return (y_prompt, y_sample), new_p + (lru_p, conv_p), new_s + (lru_s, conv_s)


def kernel(x_prompt, x_sample, cache_cmp_kv, cache_slc_kv, cache_win_kv, state_gla, state_lru, state_conv,
           page_table, norm_mix, norm_ffn, norm_final, a_w_in, a_cmp_pe, a_cmp_w, a_gla_wa2, a_gla_ba,
           a_gla_norm, a_w_out, c_w_in, c_conv_w, c_conv_b, c_w_a, c_b_a, c_w_x, c_b_x, c_lam, c_w_out,
           m_w_rg, m_b_rg, m_w_re, m_b_re, m_w1, m_w3, m_w2):
    p = {'norm_mix': norm_mix, 'norm_ffn': norm_ffn, 'norm_final': norm_final,
         'a_w_in': a_w_in, 'a_cmp_pe': a_cmp_pe, 'a_cmp_w': a_cmp_w, 'a_gla_wa2': a_gla_wa2,
         'a_gla_ba': a_gla_ba, 'a_gla_norm': a_gla_norm, 'a_w_out': a_w_out,
         'c_w_in': c_w_in, 'c_conv_w': c_conv_w, 'c_conv_b': c_conv_b, 'c_w_a': c_w_a, 'c_b_a': c_b_a,
         'c_w_x': c_w_x, 'c_b_x': c_b_x, 'c_lam': c_lam, 'c_w_out': c_w_out,
         'm_w_rg': m_w_rg, 'm_b_rg': m_b_rg, 'm_w_re': m_w_re, 'm_b_re': m_b_re,
         'm_w1': m_w1, 'm_w3': m_w3, 'm_w2': m_w2}
    past = {'cache_cmp_kv': cache_cmp_kv, 'cache_slc_kv': cache_slc_kv, 'cache_win_kv': cache_win_kv,
            'state_gla': state_gla, 'state_lru': state_lru, 'state_conv': state_conv,
            'page_table': page_table}
    (y_p, y_s), sp, ss = run_trunk(x_prompt, x_sample, p, past)
    outs = [y_p, y_s]
    for a, b in zip(sp, ss):
        outs += [a[None], b[None]]
    return tuple(outs)
```

```python
import functools
import jax, jax.numpy as jnp
from jax import lax
import numpy as np
from jax.experimental import pallas as pl
from jax.experimental.pallas import tpu as pltpu

D_MODEL = 1024
BATCH = 2
SEQ = 8192
DEC_BATCH = 128
DEC_SEQ = 4
PAST_LEN = 2048
PAGE_SIZE = 128
EPS = 1e-6
NSA_HEADS = 8
NSA_KV = 2
NSA_REP = NSA_HEADS // NSA_KV
HD = 64
CMP_BLOCK = 32
SLC_BLOCK = 64
SLC_TOPK = 16
WINDOW = 512
Q_BLOCK = 128
ROPE_DIM = HD // 4
ROPE_THETA = 500000.0
GLA_HEADS = 4
GLA_DK = 64
GLA_DV = 128
GLA_LOWRANK = 16
GLA_TAU = 16.0
D_RNN = 1280
LRU_BLOCKS = 10
LRU_BW = D_RNN // LRU_BLOCKS
CONV_W = 4
LRU_C = 8.0
N_GROUPS = 4
EXP_PER_GROUP = 8
N_EXPERTS = N_GROUPS * EXP_PER_GROUP
E_HID = 512
TOP_K_IN_GROUP = 2
NSA_Q_W = NSA_HEADS * HD
NSA_KV_W = 2 * NSA_KV * HD
GLA_K_W = GLA_HEADS * GLA_DK
GLA_V_W = GLA_HEADS * GLA_DV
A_SIZES = (NSA_Q_W, NSA_KV_W, NSA_KV_W, NSA_KV_W, 3 * NSA_HEADS, GLA_K_W, GLA_K_W, GLA_V_W, GLA_LOWRANK, GLA_V_W)
N_PROMPT = BATCH * SEQ
N_SAMPLE = DEC_BATCH * DEC_SEQ
N_TOK = N_PROMPT + N_SAMPLE
N_PAGES = PAST_LEN // PAGE_SIZE

F32 = jnp.float32
BF16 = jnp.bfloat16
VMEM_LIMIT_BYTES = 56 * 1024 * 1024
LANES = 128
SUBLANES = 8
ROW_TILE = 512


def _params(n_axes):
    return pltpu.CompilerParams(dimension_semantics=("arbitrary",) * n_axes, vmem_limit_bytes=VMEM_LIMIT_BYTES)


def _full(shape):
    return pl.BlockSpec(shape, lambda *_: (0,) * len(shape))


def _rows(width):
    return pl.BlockSpec((ROW_TILE, width), lambda i: (i, 0))


def _rms(x, g):
    return x * lax.rsqrt(jnp.mean(x * x, axis=-1, keepdims=True) + EPS) * g


def _softplus(x):
    return jnp.maximum(x, 0.0) + jnp.log1p(jnp.exp(-jnp.abs(x)))


def _gelu_tanh(x):
    return x * (0.5 * (1.0 + jnp.tanh(0.7978845608028654 * (x + 0.044715 * (x * x * x)))))


def _nt_dot(a, b):
    return lax.dot_general(a, b, (((1,), (1,)), ((), ())), preferred_element_type=F32)


def _tn_dot(a, b):
    return lax.dot_general(a, b, (((0,), (0,)), ((), ())), preferred_element_type=F32)


A_Q0, A_KVC0, A_KVS0, A_KVW0 = 0, 512, 768, 1024
A_GQ0, A_GK0, A_GV0, A_GG0, A_MISC0 = 1280, 1536, 1792, 2304, 2816
A_COLS = A_MISC0 + LANES
MISC_LR0 = 3 * NSA_HEADS
KV_HALF = NSA_KV * HD


def _rope_lanes(x, cos_t, sin_lo, sin_hi):
    reps = x.shape[1] // LANES
    tile = (lambda t: jnp.concatenate([t] * reps, axis=1)) if reps > 1 else (lambda t: t)
    w = x.shape[1]
    half = ROPE_DIM // 2
    return x * tile(cos_t) + pltpu.roll(x, half, 1) * tile(sin_hi) + pltpu.roll(x, w - half, 1) * tile(sin_lo)


N_PROMPT_TILES = N_PROMPT // ROW_TILE
_PROMPT_ROWS = pl.BlockSpec((ROW_TILE, D_MODEL), lambda i: (jnp.minimum(i, N_PROMPT_TILES - 1), 0))
_SAMPLE_ROWS = pl.BlockSpec((ROW_TILE, D_MODEL), lambda i: (0, 0))


def _token_rows(prompt_ref, sample_ref):
    return jnp.where(pl.program_id(0) < N_PROMPT_TILES, prompt_ref[...], sample_ref[...])


def _inproj_a_body(hp_ref, hs_ref, g_ref, w_ref, wa2_ref, ba_ref, cos_ref, slo_ref, shi_ref,
                   q_ref, kvc_ref, kvs_ref, kvw_ref, kvct_p_ref, kvct_s_ref, kvst_p_ref, kvst_s_ref, kvwt_p_ref,
                   kvwt_s_ref, ks_ref, vst_ref, kw_ref, vwt_ref, gates_ref, gq_ref, gk_ref, gv_ref, la_ref,
                   sgg_ref):
    y = _rms(jnp.where(pl.program_id(0) == 0, hs_ref[...], hp_ref[...]), g_ref[...]).astype(BF16)
    proj = lambda a, b: jnp.dot(y, w_ref[:, a:b], preferred_element_type=F32)
    cos_t, sin_lo, sin_hi = cos_ref[...], slo_ref[...], shi_ref[...]
    q_ref[...] = (_rope_lanes(proj(A_Q0, A_KVC0), cos_t, sin_lo, sin_hi) * (HD ** -0.5)).astype(BF16)

    def store_feature_major(p_ref, s_ref, xt):
        p_ref[0] = xt
        s_ref[...] = xt

    kvc = proj(A_KVC0, A_KVS0)
    kvc_ref[...] = kvc
    store_feature_major(kvct_p_ref, kvct_s_ref, kvc.T)
    for a0, kv_ref, kvt_p_ref, kvt_s_ref, k_ref, vt_ref in (
            (A_KVS0, kvs_ref, kvst_p_ref, kvst_s_ref, ks_ref, vst_ref),
            (A_KVW0, kvw_ref, kvwt_p_ref, kvwt_s_ref, kw_ref, vwt_ref)):
        kv = proj(a0, a0 + 2 * KV_HALF)
        k = _rope_lanes(kv[:, :KV_HALF], cos_t, sin_lo, sin_hi)
        vt = kv[:, KV_HALF:].T
        kv_ref[:, :KV_HALF] = k
        kv_ref[:, KV_HALF:] = kv[:, KV_HALF:]
        store_feature_major(kvt_p_ref, kvt_s_ref, jnp.concatenate([k.T, vt], axis=0))
        k_ref[...] = k.astype(BF16)
        vt_ref[...] = vt.astype(BF16)
    misc = proj(A_MISC0, A_COLS)
    gates_ref[...] = jax.nn.sigmoid(misc)
    z = jnp.dot(misc.astype(BF16), wa2_ref[...], preferred_element_type=F32) + ba_ref[...]
    la_ref[...] = -_softplus(-z) * (1.0 / GLA_TAU)
    gq_ref[...] = proj(A_GQ0, A_GK0) * (GLA_DK ** -0.5)
    gk_ref[...] = proj(A_GK0, A_GV0)
    gv_ref[...] = proj(A_GV0, A_GG0).astype(BF16)
    gg = proj(A_GG0, A_MISC0)
    sgg_ref[...] = gg * jax.nn.sigmoid(gg)


def _rope_tables(pos):
    pos = np.asarray(pos)
    half = ROPE_DIM // 2
    f4 = np.float32
    inv = (f4(1.0) / (f4(ROPE_THETA) ** (np.arange(0, ROPE_DIM, 2, dtype=f4) / f4(ROPE_DIM)))).astype(f4)
    ang = (pos.astype(f4)[:, None] * inv[None, :]).astype(np.float64)
    cos, sin = np.cos(ang).astype(f4), np.sin(ang).astype(f4)
    n = pos.shape[0]
    one = np.ones((n, HD - ROPE_DIM), f4)
    zero = np.zeros((n, HD - ROPE_DIM), f4)
    zh = np.zeros((n, half), f4)
    seg = lambda a, b, rest: jnp.asarray(np.concatenate([a, b, rest] * (LANES // HD), axis=1))
    return seg(cos, cos, one), seg(-sin, zh, zero), seg(zh, sin, zero)


def inproj_a(h_prompt, h_sample, g, w_in, wa2, ba, pos):
    n = h_prompt.shape[0] + h_sample.shape[0]
    zpad = jnp.zeros((D_MODEL, LANES - 3 * NSA_HEADS - GLA_LOWRANK), F32)
    o = np.cumsum((0,) + A_SIZES)
    w = jnp.concatenate([w_in[:, o[0]:o[4]], w_in[:, o[5]:o[8]], w_in[:, o[9]:o[10]],
                         w_in[:, o[4]:o[5]], w_in[:, o[8]:o[9]], zpad], axis=1).astype(BF16)
    wa2p = jnp.zeros((LANES, GLA_K_W), F32).at[MISC_LR0:MISC_LR0 + GLA_LOWRANK].set(wa2).astype(BF16)
    seq_tiles = SEQ // ROW_TILE
    ptile = lambda s: jnp.maximum(s - 1, 0)
    tile = lambda s: jnp.where(s == 0, N_PROMPT_TILES, s - 1)
    rows_t = lambda wd: pl.BlockSpec((ROW_TILE, wd), lambda s: (tile(s), 0))
    pos_rows = pl.BlockSpec((ROW_TILE, LANES), lambda s: (jnp.where(s == 0, seq_tiles, (s - 1) % seq_tiles), 0))
    cols = pl.BlockSpec((KV_HALF, ROW_TILE), lambda s: (0, tile(s)))
    kvt_p = (pl.BlockSpec((1, NSA_KV_W, ROW_TILE), lambda s: (ptile(s) // seq_tiles, 0, ptile(s) % seq_tiles)),
             (BATCH, NSA_KV_W, SEQ), F32)
    kvt_s = (pl.BlockSpec((NSA_KV_W, ROW_TILE), lambda s: (0, jnp.minimum(s, 1))), (NSA_KV_W, 2 * ROW_TILE), F32)
    outs = [(rows_t(NSA_Q_W), (n, NSA_Q_W), BF16), (rows_t(NSA_KV_W), (n, NSA_KV_W), F32),
            (rows_t(NSA_KV_W), (n, NSA_KV_W), F32), (rows_t(NSA_KV_W), (n, NSA_KV_W), F32),
            kvt_p, kvt_s, kvt_p, kvt_s, kvt_p, kvt_s,
            (rows_t(KV_HALF), (n, KV_HALF), BF16), (cols, (KV_HALF, n), BF16),
            (rows_t(KV_HALF), (n, KV_HALF), BF16), (cols, (KV_HALF, n), BF16),
            (rows_t(LANES), (n, LANES), F32), (rows_t(GLA_K_W), (n, GLA_K_W), F32),
            (rows_t(GLA_K_W), (n, GLA_K_W), F32), (rows_t(GLA_V_W), (n, GLA_V_W), BF16),
            (rows_t(GLA_K_W), (n, GLA_K_W), F32), (rows_t(GLA_V_W), (n, GLA_V_W), F32)]
    return pl.pallas_call(
        _inproj_a_body,
        grid=(n // ROW_TILE,),
        in_specs=[pl.BlockSpec((ROW_TILE, D_MODEL), lambda s: (ptile(s), 0)), _SAMPLE_ROWS, _full((1, D_MODEL)),
                  _full((D_MODEL, A_COLS)), _full((LANES, GLA_K_W)),
                  _full((1, GLA_K_W)), pos_rows, pos_rows, pos_rows],
        out_specs=[s for s, _, _ in outs],
        out_shape=[jax.ShapeDtypeStruct(shape, dt) for _, shape, dt in outs],
        compiler_params=_params(1),
        name="inproj_a",
    )(h_prompt, h_sample, g.reshape(1, D_MODEL), w, wa2p, ba.reshape(1, GLA_K_W), *_rope_tables(pos))


CMP_TILE_BLOCKS = 256
CMP_TILE_ROWS = CMP_TILE_BLOCKS * CMP_BLOCK
CMP_TILE_PAGES = CMP_TILE_ROWS // PAGE_SIZE


def _cmp_reduce(xk_ref, xv_ref, pe_ref, w_ref):
    acc = jnp.zeros((CMP_TILE_BLOCKS, NSA_KV_W), F32)
    for l in range(CMP_BLOCK):
        rows = pl.ds(l, CMP_TILE_BLOCKS, stride=CMP_BLOCK)
        xl = jnp.concatenate([xk_ref[rows, :], xv_ref[rows, :]], axis=1) + pe_ref[l:l + 1, :]
        acc = acc + jnp.dot(xl.astype(BF16), w_ref[l], preferred_element_type=F32)
    return acc


def _cmp_rows_body(xk_ref, xv_ref, pe_ref, w_ref, o_ref):
    o_ref[...] = _cmp_reduce(xk_ref, xv_ref, pe_ref, w_ref)


def _cmp_pages_body(x_ref, pe_ref, w_ref, o_ref, xk_sc, xv_sc):
    for pg in range(CMP_TILE_PAGES):
        rows = slice(pg * PAGE_SIZE, (pg + 1) * PAGE_SIZE)
        xk_sc[rows, :] = x_ref[pg, :KV_HALF, :].T
        xv_sc[rows, :] = x_ref[pg, KV_HALF:, :].T
    o_ref[...] = _cmp_reduce(xk_sc, xv_sc, pe_ref, w_ref)


def _cmp_weights(pe, w_cmp):
    pe_rows = jnp.broadcast_to(pe[:, :, None, :], (CMP_BLOCK, 2, NSA_KV, HD)).reshape(CMP_BLOCK, NSA_KV_W)
    w = w_cmp.astype(BF16)
    zero = jnp.zeros((CMP_BLOCK, HD, HD), BF16)
    diag = [w[:, c] for c in range(2) for _ in range(NSA_KV)]
    w_bd = jnp.concatenate([jnp.concatenate([blk if j == i else zero for j in range(len(diag))], axis=2)
                            for i, blk in enumerate(diag)], axis=1)
    return pe_rows, w_bd


def cmp_blocks_rows(x, n_rows, cmp_weights):
    return pl.pallas_call(
        _cmp_rows_body,
        grid=(n_rows // CMP_TILE_ROWS,),
        in_specs=[pl.BlockSpec((CMP_TILE_ROWS, KV_HALF), lambda i: (i, 0)),
                  pl.BlockSpec((CMP_TILE_ROWS, KV_HALF), lambda i: (i, 1)), _full((CMP_BLOCK, NSA_KV_W)),
                  _full((CMP_BLOCK, NSA_KV_W, NSA_KV_W))],
        out_specs=pl.BlockSpec((CMP_TILE_BLOCKS, NSA_KV_W), lambda i: (i, 0)),
        out_shape=jax.ShapeDtypeStruct((n_rows // CMP_BLOCK, NSA_KV_W), F32),
        compiler_params=_params(1),
        name="cmp_blocks_rows",
    )(x, x, *cmp_weights)


def cmp_blocks_pages(xt, cmp_weights):
    n_pages = xt.shape[0]
    return pl.pallas_call(
        _cmp_pages_body,
        grid=(n_pages // CMP_TILE_PAGES,),
        in_specs=[pl.BlockSpec((CMP_TILE_PAGES, NSA_KV_W, PAGE_SIZE), lambda i: (i, 0, 0)),
                  _full((CMP_BLOCK, NSA_KV_W)), _full((CMP_BLOCK, NSA_KV_W, NSA_KV_W))],
        out_specs=pl.BlockSpec((CMP_TILE_BLOCKS, NSA_KV_W), lambda i: (i, 0)),
        out_shape=jax.ShapeDtypeStruct((n_pages * PAGE_SIZE // CMP_BLOCK, NSA_KV_W), F32),
        scratch_shapes=[pltpu.VMEM((CMP_TILE_ROWS, KV_HALF), F32), pltpu.VMEM((CMP_TILE_ROWS, KV_HALF), F32)],
        compiler_params=_params(1),
        name="cmp_blocks_pages",
    )(xt, *cmp_weights)


KEY_TILE = 256
NEG_BIG = -1e30
N_CMP_PROMPT = SEQ // CMP_BLOCK
N_SLC_PROMPT = SEQ // SLC_BLOCK
SLC_SHIFT = SLC_BLOCK.bit_length() - 1
N_CMP_DEC = PAST_LEN // CMP_BLOCK
N_SLC_DEC = -(-(PAST_LEN + DEC_SEQ) // SLC_BLOCK)
N_SLC_DEC_PAD = -(-N_SLC_DEC // LANES) * LANES
DEC_COLS_PER_GROUP = NSA_REP * DEC_SEQ


def _tile_cols(x, reps):
    return jnp.concatenate([x] * reps, axis=1) if reps > 1 else x


def _even_first_cmp_end(n_cmp):
    j = lax.broadcasted_iota(jnp.int32, (n_cmp, 1), 0)
    blk = jnp.where(j < n_cmp // 2, 2 * j, 2 * (j - n_cmp // 2) + 1)
    return (blk + 1) * CMP_BLOCK - 1


def _softmax_cols(s, mask):
    s = jnp.where(mask, s, -jnp.inf)
    m = jnp.max(s, axis=0, keepdims=True)
    m = jnp.where(m > -jnp.inf, m, 0.0)
    e = jnp.where(mask, jnp.exp(s - m), 0.0)
    return e / jnp.maximum(jnp.sum(e, axis=0, keepdims=True), 1e-30)


def _softmax_cols_biased(s):
    m = jnp.max(s, axis=0, keepdims=True)
    e = jnp.exp(s - m)
    scale = jnp.where(m > 0.5 * NEG_BIG, 1.0 / jnp.sum(e, axis=0, keepdims=True), 0.0)
    return e * scale


N_FORCED = 3


def _select_cols(p_slc, qpos, n_top):
    ns = p_slc.shape[0]
    blk = lax.broadcasted_iota(jnp.int32, p_slc.shape, 0)
    cur = qpos >> SLC_SHIFT
    forced = ((blk == 0) | (blk == cur) | (blk == cur - 1)) & (blk <= cur)
    score = jnp.where((blk <= cur) & jnp.logical_not(forced), p_slc, -jnp.inf)
    sel = jnp.where(forced, 1.0, 0.0)
    for _ in range(n_top - N_FORCED):
        m = jnp.max(score, axis=0, keepdims=True)
        idx = jnp.min(jnp.where(score == m, blk, ns), axis=0, keepdims=True)
        hit = blk == idx
        sel = jnp.where(hit & (m > -jnp.inf), 1.0, sel)
        score = jnp.where(hit, -jnp.inf, score)
    return sel


def _flash_init(m_sc, l_sc, acc_sc):
    m_sc[...] = jnp.full(m_sc.shape, NEG_BIG, F32)
    l_sc[...] = jnp.zeros(l_sc.shape, F32)
    acc_sc[...] = jnp.zeros(acc_sc.shape, F32)


def _flash_cols(scores, mask, pv, m_sc, l_sc, acc_sc):
    s = jnp.where(mask, scores, NEG_BIG)
    m_old = m_sc[...]
    m_new = jnp.maximum(m_old, jnp.max(s, axis=0, keepdims=True))
    alpha = jnp.exp(m_old - m_new)
    p = jnp.where(mask, jnp.exp(s - m_new), 0.0)
    l_sc[...] = alpha * l_sc[...] + jnp.sum(p, axis=0, keepdims=True)
    acc_sc[...] = alpha * acc_sc[...] + pv(p.astype(BF16))
    m_sc[...] = m_new


def _flash_stream(score_fn, pv_fn, first, lo, n, stream_sc, m_sc, l_sc, acc_sc):
    (sa, ca), (sb, cb) = stream_sc

    def issue(s_ref, c_ref, kt, self_tile=False):
        s = score_fn(kt, self_tile)
        s_ref[...] = s
        c_ref[...] = jnp.max(s, axis=0, keepdims=True)

    def consume(s_ref, c_ref, kt):
        m_old = m_sc[...]
        m_new = jnp.maximum(m_old, c_ref[...])
        alpha = jnp.exp(m_old - m_new)
        p = jnp.exp(s_ref[...] - m_new)
        l_sc[...] = alpha * l_sc[...] + jnp.sum(p, axis=0, keepdims=True)
        acc_sc[...] = alpha * acc_sc[...] + pv_fn(kt, p.astype(BF16))
        m_sc[...] = m_new

    _flash_init(m_sc, l_sc, acc_sc)
    issue(sa, ca, first, True)

    def two_tiles(jj, kt_a):
        t0 = lo + 2 * jj
        issue(sb, cb, t0)
        consume(sa, ca, kt_a)
        issue(sa, ca, t0 + 1)
        consume(sb, cb, t0)
        return t0 + 1

    kt_a = lax.fori_loop(0, n // 2, two_tiles, first)

    @pl.when(n % 2 == 1)
    def _():
        issue(sb, cb, lo + n - 1)
        consume(sa, ca, kt_a)
        consume(sb, cb, lo + n - 1)

    @pl.when(n % 2 == 0)
    def _():
        consume(sa, ca, kt_a)

    return acc_sc[...] / jnp.maximum(l_sc[...], 1e-30)


def _flash_out(l_sc, acc_sc):
    return acc_sc[...] / jnp.maximum(l_sc[...], 1e-30)


def _pv_split(vt, p):
    c2 = p.shape[1] // 2
    return jnp.concatenate([jnp.dot(vt[:HD], p[:, :c2], preferred_element_type=F32),
                            jnp.dot(vt[HD:], p[:, c2:], preferred_element_type=F32)], axis=1)


def _nsa_prompt_body(q_ref, gates_ref, ckv_ref, cos_ref, slo_ref, shi_ref, ks_ref, vst_ref, kw_ref, vwt_ref,
                     o_ref, m_sc, l_sc, acc_sc, sel_bias_sc, sa_sc, ca_sc, sb_sc, cb_sc):
    stream_sc = ((sa_sc, ca_sc), (sb_sc, cb_sc))
    i = pl.program_id(1)
    nq = Q_BLOCK
    cols = NSA_HEADS * nq
    qpos = i * nq + lax.broadcasted_iota(jnp.int32, (1, nq), 1)
    q = q_ref[...].astype(F32)
    pairs = [q[:, j * LANES:(j + 1) * LANES].T for j in range(NSA_HEADS // 2)]
    zero = jnp.zeros((HD, cols // 2), F32)
    qt_g = [jnp.concatenate([pairs[2 * g][:HD], pairs[2 * g][HD:], pairs[2 * g + 1][:HD], pairs[2 * g + 1][HD:]],
                            axis=1) for g in range(NSA_KV)]
    qt = jnp.concatenate([jnp.concatenate([qt_g[0], zero], axis=1),
                          jnp.concatenate([zero, qt_g[1]], axis=1)], axis=0).astype(BF16)
    ckv = ckv_ref[0]
    ck = _rope_lanes(ckv[:, :KV_HALF], cos_ref[...], slo_ref[...], shi_ref[...]).astype(BF16)
    cvt = ckv[:, KV_HALF:].T.astype(BF16)
    c_bias = jnp.where(_even_first_cmp_end(N_CMP_PROMPT) <= qpos, 0.0, NEG_BIG)
    p = _softmax_cols_biased(jnp.dot(ck, qt, preferred_element_type=F32) + _tile_cols(c_bias, NSA_HEADS))
    o_c = _pv_split(cvt, p.astype(BF16))
    sel = []
    for g in range(NSA_KV):
        c0 = g * NSA_REP * nq
        p_grp = p[:, c0:c0 + nq]
        for r in range(1, NSA_REP):
            p_grp = p_grp + p[:, c0 + r * nq:c0 + (r + 1) * nq]
        p_slc = p_grp[:N_CMP_PROMPT // 2] + p_grp[N_CMP_PROMPT // 2:]
        sel.append(_select_cols(p_slc, qpos, SLC_TOPK))
    sel = jnp.concatenate(sel, axis=1)
    sel_bias_sc[...] = jnp.where(sel > 0.5, 0.0, NEG_BIG)
    key_row = lax.broadcasted_iota(jnp.int32, (KEY_TILE, 1), 0)
    blocks_per_tile = KEY_TILE // SLC_BLOCK
    kt_self = i // (KEY_TILE // nq)

    def slc_scores(kt, self_tile):
        k0 = pl.multiple_of(kt * KEY_TILE, KEY_TILE)
        rows = []
        for j in range(blocks_per_tile):
            b = sel_bias_sc[pl.ds(kt * blocks_per_tile + j, 1), :]
            rows.append(jnp.concatenate(
                [jnp.broadcast_to(b[:, g * nq:(g + 1) * nq], (SLC_BLOCK, nq)) for g in range(NSA_KV)
                 for _ in range(NSA_REP)], axis=1))
        bias = jnp.concatenate(rows, axis=0)
        if self_tile:
            bias = bias + _tile_cols(jnp.where(k0 + key_row <= qpos, 0.0, NEG_BIG), NSA_HEADS)
        return jnp.dot(ks_ref[pl.ds(k0, KEY_TILE), :], qt, preferred_element_type=F32) + bias

    def slc_pv(kt, pb):
        return _pv_split(vst_ref[:, pl.ds(pl.multiple_of(kt * KEY_TILE, KEY_TILE), KEY_TILE)], pb)

    o_s = _flash_stream(slc_scores, slc_pv, kt_self, 0, kt_self, stream_sc, m_sc, l_sc, acc_sc)

    def win_scores(kt, self_tile):
        k0 = pl.multiple_of(kt * KEY_TILE, KEY_TILE)
        d = qpos - (k0 + key_row)
        bias = jnp.where((d >= 0) & (d < WINDOW), 0.0, NEG_BIG)
        return jnp.dot(kw_ref[pl.ds(k0, KEY_TILE), :], qt, preferred_element_type=F32) + _tile_cols(bias, NSA_HEADS)

    def win_pv(kt, pb):
        return _pv_split(vwt_ref[:, pl.ds(pl.multiple_of(kt * KEY_TILE, KEY_TILE), KEY_TILE)], pb)

    win_lo = jnp.maximum(i - WINDOW // nq, 0) // (KEY_TILE // nq)
    o_w = _flash_stream(win_scores, win_pv, kt_self, win_lo, kt_self - win_lo, stream_sc, m_sc, l_sc, acc_sc)
    gates_t = gates_ref[...].T
    merged = []
    for hd in range(NSA_HEADS):
        cs = slice(hd * nq, (hd + 1) * nq)
        gate = lambda br: gates_t[3 * hd + br:3 * hd + br + 1]
        merged.append(gate(0) * o_c[:, cs] + gate(1) * o_s[:, cs] + gate(2) * o_w[:, cs])
    for j in range(NSA_HEADS // 2):
        pair = jnp.concatenate([merged[2 * j], merged[2 * j + 1]], axis=0).T
        o_ref[:, j * LANES:(j + 1) * LANES] = pair.astype(BF16)


def nsa_prompt_attn(q, gates, ckv, ks, vst, kw, vwt):
    nqb = SEQ // Q_BLOCK
    tok = lambda wd: pl.BlockSpec((Q_BLOCK, wd), lambda b, i: (b * nqb + i, 0))
    seq_rows = pl.BlockSpec((SEQ, KV_HALF), lambda b, i: (b, 0))
    seq_cols = pl.BlockSpec((KV_HALF, SEQ), lambda b, i: (0, b))
    n_cmp = N_CMP_PROMPT
    c_blk = np.concatenate([np.arange(0, n_cmp, 2), np.arange(1, n_cmp, 2)])
    cols = NSA_HEADS * Q_BLOCK
    return pl.pallas_call(
        _nsa_prompt_body,
        grid=(BATCH, nqb),
        in_specs=[tok(NSA_Q_W), tok(LANES), pl.BlockSpec((1, n_cmp, NSA_KV_W), lambda b, i: (b, 0, 0)),
                  _full((n_cmp, LANES)), _full((n_cmp, LANES)), _full((n_cmp, LANES)),
                  seq_rows, seq_cols, seq_rows, seq_cols],
        out_specs=tok(NSA_Q_W),
        out_shape=jax.ShapeDtypeStruct((N_PROMPT, NSA_Q_W), BF16),
        scratch_shapes=[pltpu.VMEM((1, cols), F32), pltpu.VMEM((1, cols), F32), pltpu.VMEM((HD, cols), F32),
                        pltpu.VMEM((N_SLC_PROMPT, NSA_KV * Q_BLOCK), F32),
                        pltpu.VMEM((KEY_TILE, cols), F32), pltpu.VMEM((1, cols), F32),
                        pltpu.VMEM((KEY_TILE, cols), F32), pltpu.VMEM((1, cols), F32)],
        compiler_params=_params(2),
        name="nsa_prompt",
    )(q, gates, ckv, *_rope_tables((c_blk + 1) * CMP_BLOCK - 1), ks, vst, kw, vwt)


def _nsa_decode_body(pt_ref, qt_ref, gate_ref, ckv_ref, cos_ref, slo_ref, shi_ref, hsum_ref, *refs):
    page_refs = refs[:N_PAGES]
    win_ref, kvs_new_ref, kvw_new_ref, o_ref, win_out_ref, m_sc, l_sc, acc_sc = refs[N_PAGES:]
    qt = qt_ref[0]
    lane = lax.broadcasted_iota(jnp.int32, (1, LANES), 1)
    qpos = PAST_LEN + (lane & (DEC_SEQ - 1))
    group0 = lane < DEC_COLS_PER_GROUP
    ckv = ckv_ref[0]
    ck = _rope_lanes(ckv[:, :KV_HALF], cos_ref[...], slo_ref[...], shi_ref[...]).astype(BF16)
    c_mask = _even_first_cmp_end(N_CMP_DEC) <= qpos
    p = _softmax_cols(jnp.dot(ck, qt, preferred_element_type=F32), c_mask)
    o_c = _tn_dot(ckv[:, KV_HALF:].astype(BF16), p.astype(BF16))
    p_grp = jnp.dot(p, hsum_ref[...], preferred_element_type=F32, precision=lax.Precision.HIGHEST)
    p_slc = jnp.concatenate([p_grp[:N_CMP_DEC // 2] + p_grp[N_CMP_DEC // 2:],
                             jnp.zeros((N_SLC_DEC_PAD - N_CMP_DEC // 2, LANES), F32)], axis=0)
    sel = _select_cols(p_slc, qpos, SLC_TOPK)

    def new_rows(ref):
        row = ref[0]
        kv = jnp.concatenate([row[:, t * NSA_KV_W:(t + 1) * NSA_KV_W] for t in range(DEC_SEQ)], axis=0)
        return jnp.concatenate([kv, jnp.zeros((SUBLANES - DEC_SEQ, NSA_KV_W), F32)], axis=0)

    new_row = lax.broadcasted_iota(jnp.int32, (SUBLANES, 1), 0)
    new_pos = PAST_LEN + new_row
    new_valid = new_row < DEC_SEQ
    _flash_init(m_sc, l_sc, acc_sc)
    kt_old = jnp.concatenate([r[0, :KV_HALF, :] for r in page_refs], axis=1).astype(BF16)
    vt_old = jnp.concatenate([r[0, KV_HALF:, :] for r in page_refs], axis=1).astype(BF16)
    key_blk = lax.broadcasted_iota(jnp.int32, (PAST_LEN, N_SLC_DEC_PAD), 0) >> SLC_SHIFT
    blk_col = lax.broadcasted_iota(jnp.int32, (PAST_LEN, N_SLC_DEC_PAD), 1)
    chosen = jnp.dot(jnp.where(key_blk == blk_col, 1.0, 0.0).astype(BF16), sel.astype(BF16),
                     preferred_element_type=F32) > 0.5
    old_pos = lax.broadcasted_iota(jnp.int32, (PAST_LEN, 1), 0)
    _flash_cols(_tn_dot(kt_old, qt), chosen & (old_pos <= qpos),
                lambda pb: jnp.dot(vt_old, pb, preferred_element_type=F32), m_sc, l_sc, acc_sc)
    kv_new = new_rows(kvs_new_ref)
    sel_new = sel[(PAST_LEN >> SLC_SHIFT):(PAST_LEN >> SLC_SHIFT) + 1] > 0.5
    v_new = kv_new[:, KV_HALF:].astype(BF16)
    _flash_cols(jnp.dot(kv_new[:, :KV_HALF].astype(BF16), qt, preferred_element_type=F32),
                sel_new & new_valid & (new_pos <= qpos), lambda pb: _tn_dot(v_new, pb), m_sc, l_sc, acc_sc)
    o_s = _flash_out(l_sc, acc_sc)
    _flash_init(m_sc, l_sc, acc_sc)
    n_buf = win_ref.shape[2]
    win = win_ref[0]
    d = qpos - (PAST_LEN - n_buf + lax.broadcasted_iota(jnp.int32, (n_buf, 1), 0))
    vt_win = win[KV_HALF:].astype(BF16)
    _flash_cols(_tn_dot(win[:KV_HALF].astype(BF16), qt), (d >= 0) & (d < WINDOW),
                lambda pb: jnp.dot(vt_win, pb, preferred_element_type=F32), m_sc, l_sc, acc_sc)
    kw_new = new_rows(kvw_new_ref)
    d = qpos - new_pos
    vw_new = kw_new[:, KV_HALF:].astype(BF16)
    _flash_cols(jnp.dot(kw_new[:, :KV_HALF].astype(BF16), qt, preferred_element_type=F32),
                new_valid & (d >= 0) & (d < WINDOW), lambda pb: _tn_dot(vw_new, pb), m_sc, l_sc, acc_sc)
    o_w = _flash_out(l_sc, acc_sc)
    g = gate_ref[0]
    o = g[0:1] * o_c + g[1:2] * o_s + g[2:3] * o_w
    o_ref[0] = jnp.where(group0, o[:HD], o[HD:])
    key = lax.broadcasted_iota(jnp.int32, (SUBLANES, n_buf), 1)
    place = jnp.where((key == n_buf - DEC_SEQ + new_row) & new_valid, 1.0, 0.0)
    placed = lax.dot_general(kw_new, place, (((0,), (0,)), ((), ())), preferred_element_type=F32,
                             precision=lax.Precision.HIGHEST)
    keep = lax.broadcasted_iota(jnp.int32, (1, n_buf), 1) < n_buf - DEC_SEQ
    win_out_ref[0] = jnp.where(keep, pltpu.roll(win, n_buf - DEC_SEQ, 1), placed)


def nsa_decode_attn(qt, gate_rows, ckv, slc_pool, win_buf, kvs_new, kvw_new, page_table):
    n_buf = win_buf.shape[2]
    per_b = lambda *shape: pl.BlockSpec((1,) + shape, lambda b, pt: (b,) + (0,) * len(shape))
    const = lambda *shape: pl.BlockSpec(shape, lambda b, pt: (0,) * len(shape))
    page = lambda j: pl.BlockSpec((1, NSA_KV_W, PAGE_SIZE), lambda b, pt: (pt[b, j], 0, 0))
    c_blk = np.concatenate([np.arange(0, N_CMP_DEC, 2), np.arange(1, N_CMP_DEC, 2)])
    col = jnp.arange(LANES)
    used = col < NSA_KV * DEC_COLS_PER_GROUP
    same = (col[:, None] // DEC_COLS_PER_GROUP == col[None, :] // DEC_COLS_PER_GROUP) & \
           (col[:, None] % DEC_SEQ == col[None, :] % DEC_SEQ) & used[:, None] & used[None, :]
    return pl.pallas_call(
        _nsa_decode_body,
        grid_spec=pltpu.PrefetchScalarGridSpec(
            num_scalar_prefetch=1,
            grid=(DEC_BATCH,),
            in_specs=[per_b(LANES, LANES), per_b(3, LANES), per_b(N_CMP_DEC, NSA_KV_W),
                      const(N_CMP_DEC, LANES), const(N_CMP_DEC, LANES), const(N_CMP_DEC, LANES),
                      const(LANES, LANES)] + [page(j) for j in range(N_PAGES)]
                     + [per_b(NSA_KV_W, n_buf), per_b(1, DEC_SEQ * NSA_KV_W), per_b(1, DEC_SEQ * NSA_KV_W)],
            out_specs=[per_b(HD, LANES), per_b(NSA_KV_W, n_buf)],
            scratch_shapes=[pltpu.VMEM((1, LANES), F32), pltpu.VMEM((1, LANES), F32),
                            pltpu.VMEM((LANES, LANES), F32)],
        ),
        out_shape=[jax.ShapeDtypeStruct((DEC_BATCH, HD, LANES), F32),
                   jax.ShapeDtypeStruct((DEC_BATCH, NSA_KV_W, n_buf), F32)],
        compiler_params=_params(1),
        name="nsa_decode",
    )(page_table, qt, gate_rows, ckv, *_rope_tables((c_blk + 1) * CMP_BLOCK - 1), same.astype(F32),
      *([slc_pool] * N_PAGES), win_buf, kvs_new, kvw_new)


GLA_PAIRS = GLA_HEADS // 2
GLA_ROWS = 256
GLA_SUB = 16
GLA_STEP_SEQS = 8


def _gla_rows(q, k, la, v, sgg, gnorm, st_ref, sub):
    R = q.shape[0]
    row = lax.broadcasted_iota(jnp.int32, (R, GLA_K_W), 0)
    rin = row % sub
    cum = la
    d = 1
    while d < sub:
        cum = cum + jnp.where(rin >= d, pltpu.roll(cum, d, 0), 0.0)
        d *= 2
    lane = lax.broadcasted_iota(jnp.int32, (sub, LANES), 1)
    lo = lane < GLA_DK
    rsub = lax.broadcasted_iota(jnp.int32, (sub, LANES), 0)
    out_rows = []
    for c in range(R // sub):
        rs = slice(c * sub, (c + 1) * sub)
        cum_c = cum[rs]
        last = cum_c[sub - 1:sub]
        qe = q[rs] * jnp.exp(cum_c)
        kdec = k[rs] * jnp.exp(last - cum_c)
        v_c = v[rs]
        heads = []
        for pr in range(GLA_PAIRS):
            ls = slice(pr * LANES, (pr + 1) * LANES)
            st = st_ref[pr]
            st_b = st.astype(BF16)
            qe_p, kd_p, q_p, k_p, cum_p = qe[:, ls], kdec[:, ls], q[rs, ls], k[rs, ls], cum_c[:, ls]
            v_pair = [v_c[:, (2 * pr + hh) * GLA_DV:(2 * pr + hh + 1) * GLA_DV] for hh in range(2)]
            upd = jnp.zeros((GLA_DV, LANES), F32)
            o_pair = []
            for hh in range(2):
                keep = lo if hh == 0 else jnp.logical_not(lo)
                o_pair.append(_nt_dot(jnp.where(keep, qe_p, 0.0).astype(BF16), st_b))
                upd = upd + _tn_dot(v_pair[hh].astype(BF16), jnp.where(keep, kd_p, 0.0).astype(BF16))
            for j in range(sub):
                dj = jnp.where(rsub >= j, jnp.exp(cum_p - cum_p[j:j + 1]), 0.0)
                w = q_p * k_p[j:j + 1] * dj
                a_lo = jnp.sum(jnp.where(lo, w, 0.0), axis=-1, keepdims=True)
                a_hi = jnp.sum(jnp.where(lo, 0.0, w), axis=-1, keepdims=True)
                o_pair[0] = o_pair[0] + a_lo * v_pair[0][j:j + 1]
                o_pair[1] = o_pair[1] + a_hi * v_pair[1][j:j + 1]
            st_ref[pr] = st * jnp.exp(last[:, ls]) + upd
            heads += o_pair
        out_rows.append(jnp.concatenate([_rms(x, gnorm) for x in heads], axis=1))
    return jnp.concatenate(out_rows, axis=0) * sgg


def _gla_seq_body(q_ref, k_ref, la_ref, v_ref, sgg_ref, gn_ref, o_ref, st_out_ref, st_sc):
    @pl.when(pl.program_id(1) == 0)
    def _():
        st_sc[...] = jnp.zeros(st_sc.shape, F32)

    o = _gla_rows(q_ref[...], k_ref[...], la_ref[...], v_ref[...].astype(F32), sgg_ref[...], gn_ref[...],
                  st_sc, GLA_SUB)
    o_ref[...] = o.astype(BF16)
    st_out_ref[0] = st_sc[...]


def gla_seq(q, k, la, v, sgg, gnorm):
    nt = SEQ // GLA_ROWS
    rows = lambda wd: pl.BlockSpec((GLA_ROWS, wd), lambda b, t: (b * nt + t, 0))
    return pl.pallas_call(
        _gla_seq_body,
        grid=(BATCH, nt),
        in_specs=[rows(GLA_K_W), rows(GLA_K_W), rows(GLA_K_W), rows(GLA_V_W), rows(GLA_V_W),
                  _full((1, GLA_DV))],
        out_specs=[rows(GLA_V_W), pl.BlockSpec((1, GLA_PAIRS, GLA_DV, LANES), lambda b, t: (b, 0, 0, 0))],
        out_shape=[jax.ShapeDtypeStruct((N_PROMPT, GLA_V_W), BF16),
                   jax.ShapeDtypeStruct((BATCH, GLA_PAIRS, GLA_DV, LANES), F32)],
        scratch_shapes=[pltpu.VMEM((GLA_PAIRS, GLA_DV, LANES), F32)],
        compiler_params=_params(2),
        name="gla_seq",
    )(q, k, la, v, sgg, gnorm.reshape(1, GLA_DV))


def _gla_step_body(q_ref, k_ref, la_ref, v_ref, sgg_ref, gn_ref, st_in_ref, o_ref, st_out_ref):
    st_out_ref[...] = st_in_ref[...]
    q, k, la, v, sgg = q_ref[...], k_ref[...], la_ref[...], v_ref[...].astype(F32), sgg_ref[...]
    for j in range(GLA_STEP_SEQS):
        rs = slice(j * DEC_SEQ, (j + 1) * DEC_SEQ)
        o = _gla_rows(q[rs], k[rs], la[rs], v[rs], sgg[rs], gn_ref[...], st_out_ref.at[j], DEC_SEQ)
        o_ref[rs, :] = o.astype(BF16)


def gla_step(q, k, la, v, sgg, gnorm, st_in, row0):
    rows_per = GLA_STEP_SEQS * DEC_SEQ
    blk0 = row0 // rows_per
    rows = lambda wd: pl.BlockSpec((rows_per, wd), lambda i: (blk0 + i, 0))
    st_spec = pl.BlockSpec((GLA_STEP_SEQS, GLA_PAIRS, GLA_DV, LANES), lambda i: (i, 0, 0, 0))
    return pl.pallas_call(
        _gla_step_body,
        grid=(DEC_BATCH // GLA_STEP_SEQS,),
        in_specs=[rows(GLA_K_W), rows(GLA_K_W), rows(GLA_K_W), rows(GLA_V_W), rows(GLA_V_W),
                  _full((1, GLA_DV)), st_spec],
        out_specs=[pl.BlockSpec((rows_per, GLA_V_W), lambda i: (i, 0)), st_spec],
        out_shape=[jax.ShapeDtypeStruct((N_SAMPLE, GLA_V_W), BF16),
                   jax.ShapeDtypeStruct((DEC_BATCH, GLA_PAIRS, GLA_DV, LANES), F32)],
        compiler_params=_params(1),
        name="gla_step",
    )(q, k, la, v, sgg, gnorm.reshape(1, GLA_DV), st_in)


def _gla_state_to_pairs(s):
    B = s.shape[0]
    return s.reshape(B, GLA_PAIRS, 2, GLA_DK, GLA_DV).transpose(0, 1, 4, 2, 3).reshape(B, GLA_PAIRS, GLA_DV, LANES)


def _gla_state_from_pairs(st):
    B = st.shape[0]
    return st.reshape(B, GLA_PAIRS, GLA_DV, 2, GLA_DK).transpose(0, 1, 3, 4, 2).reshape(B, GLA_HEADS, GLA_DK, GLA_DV)


LRU_TIME_TILE = 256


def _lru_gates(xc, wa_ref, ba, wx_ref, bx, lam):
    xcb = xc.astype(BF16)
    r_parts, i_parts = [], []
    for n in range(LRU_BLOCKS):
        xs = xcb[:, n * LRU_BW:(n + 1) * LRU_BW]
        r_parts.append(jnp.dot(xs, wa_ref[n], preferred_element_type=F32))
        i_parts.append(jnp.dot(xs, wx_ref[n], preferred_element_type=F32))
    r = jax.nn.sigmoid(jnp.concatenate(r_parts, axis=-1) + ba)
    i = jax.nn.sigmoid(jnp.concatenate(i_parts, axis=-1) + bx)
    log_a = -LRU_C * r * _softplus(-lam)
    a = jnp.exp(log_a)
    u = jnp.sqrt(1.0 - a * a) * (i * xc)
    return a, u


def _lru_seq_body(u_ref, cw_ref, cb_ref, wa_ref, ba_ref, wx_ref, bx_ref, lam_ref, y_ref, hT_ref, xp_sc, h_sc):
    tt = LRU_TIME_TILE

    @pl.when(pl.program_id(1) == 0)
    def _():
        xp_sc[...] = jnp.zeros((SUBLANES, D_RNN), F32)
        h_sc[...] = jnp.zeros((1, D_RNN), F32)

    x = u_ref[:, D_RNN:]
    head = jnp.concatenate([xp_sc[...], x[:SUBLANES]], axis=0)
    xc = cb_ref[...] + cw_ref[CONV_W - 1:CONV_W, :] * x
    for k in range(1, CONV_W):
        shifted = jnp.concatenate([head[SUBLANES - k:2 * SUBLANES - k], pltpu.roll(x, k, 0)[SUBLANES:]], axis=0)
        xc = xc + cw_ref[CONV_W - 1 - k:CONV_W - k, :] * shifted
    xp_sc[...] = x[tt - SUBLANES:]
    a, u = _lru_gates(xc, wa_ref, ba_ref[...], wx_ref, bx_ref[...], lam_ref[...])
    row = lax.broadcasted_iota(jnp.int32, (tt, D_RNN), 0) % SUBLANES
    d = 1
    while d < SUBLANES:
        keep = row >= d
        a_prev = jnp.where(keep, pltpu.roll(a, d, 0), 1.0)
        u_prev = jnp.where(keep, pltpu.roll(u, d, 0), 0.0)
        u = a * u_prev + u
        a = a * a_prev
        d *= 2
    carry = h_sc[...]
    hs = []
    for grp in range(tt // SUBLANES):
        rs = slice(grp * SUBLANES, (grp + 1) * SUBLANES)
        hs.append(a[rs] * carry + u[rs])
        carry = hs[-1][SUBLANES - 1:SUBLANES, :]
    h_sc[...] = carry
    hT_ref[0] = carry
    y_ref[...] = (_gelu_tanh(u_ref[:, :D_RNN]) * jnp.concatenate(hs, axis=0)).astype(BF16)


def _lru_weight_args(cw, cb, wa, ba, wx, bx, lam):
    row = lambda v: v.reshape(1, D_RNN)
    return (cw, row(cb), wa.astype(BF16), row(ba), wx.astype(BF16), row(bx), row(lam))


_LRU_WEIGHT_SPECS = [_full((CONV_W, D_RNN)), _full((1, D_RNN)), _full((LRU_BLOCKS, LRU_BW, LRU_BW)),
                     _full((1, D_RNN)), _full((LRU_BLOCKS, LRU_BW, LRU_BW)), _full((1, D_RNN)),
                     _full((1, D_RNN))]


def lru_seq(u, cw, cb, wa, ba, wx, bx, lam):
    tt = LRU_TIME_TILE
    nt = SEQ // tt
    y, hT = pl.pallas_call(
        _lru_seq_body,
        grid=(BATCH, nt),
        in_specs=[pl.BlockSpec((tt, 2 * D_RNN), lambda b, t: (b * nt + t, 0))] + _LRU_WEIGHT_SPECS,
        out_specs=[pl.BlockSpec((tt, D_RNN), lambda b, t: (b * nt + t, 0)),
                   pl.BlockSpec((1, 1, D_RNN), lambda b, t: (b, 0, 0))],
        out_shape=[jax.ShapeDtypeStruct((N_PROMPT, D_RNN), BF16), jax.ShapeDtypeStruct((BATCH, 1, D_RNN), F32)],
        scratch_shapes=[pltpu.VMEM((SUBLANES, D_RNN), F32), pltpu.VMEM((1, D_RNN), F32)],
        compiler_params=_params(2),
        name="lru_seq",
    )(u, *_lru_weight_args(cw, cb, wa, ba, wx, bx, lam))
    return y, hT.reshape(BATCH, D_RNN)


def _lru_step_body(u_ref, cs_ref, h0_ref, cw_ref, cb_ref, wa_ref, ba_ref, wx_ref, bx_ref, lam_ref, y_ref, hT_ref):
    n_t = u_ref.shape[0]
    hist = [cs_ref[:, w, :] for w in range(CONV_W - 1)] + [u_ref[t, :, D_RNN:] for t in range(n_t)]
    h = h0_ref[...]
    for t in range(n_t):
        xc = cb_ref[...]
        for w in range(CONV_W):
            xc = xc + cw_ref[w:w + 1, :] * hist[t + w]
        a, u = _lru_gates(xc, wa_ref, ba_ref[...], wx_ref, bx_ref[...], lam_ref[...])
        h = a * h + u
        y_ref[t] = (_gelu_tanh(u_ref[t, :, :D_RNN]) * h).astype(BF16)
    hT_ref[...] = h


def lru_step(u, conv_state, h0, cw, cb, wa, ba, wx, bx, lam):
    T, B, _ = u.shape
    return pl.pallas_call(
        _lru_step_body,
        out_shape=[jax.ShapeDtypeStruct((T, B, D_RNN), BF16), jax.ShapeDtypeStruct((B, D_RNN), F32)],
        compiler_params=pltpu.CompilerParams(vmem_limit_bytes=VMEM_LIMIT_BYTES),
        name="lru_step",
    )(u, conv_state, h0, *_lru_weight_args(cw, cb, wa, ba, wx, bx, lam))


ROUTE_E1, ROUTE_E2, ROUTE_G1, ROUTE_G2, ROUTE_R1, ROUTE_R2 = range(6)
EXPERT_LANE0 = N_GROUPS


def _lane_pick(val_by_lane):
    rows = next(iter(val_by_lane.values())).shape[0]
    lane = lax.broadcasted_iota(jnp.int32, (rows, LANES), 1)
    out = jnp.zeros((rows, LANES), F32)
    for l, v in val_by_lane.items():
        out = jnp.where(lane == l, v, out)
    return out


def _route_rows(logits, tri_ref, carry_ref):
    rows = logits.shape[0]
    lane = lax.broadcasted_iota(jnp.int32, (rows, LANES), 1)
    neg = -jnp.inf
    gl = jnp.where(lane < N_GROUPS, logits, neg)
    gmax = jnp.max(gl, axis=-1, keepdims=True)
    gtop = jnp.min(jnp.where(gl == gmax, lane, LANES), axis=-1, keepdims=True)
    gsum = jnp.sum(jnp.where(lane < N_GROUPS, jnp.exp(logits - gmax), 0.0), axis=-1, keepdims=True)
    g_w = 1.0 / gsum
    lo = EXPERT_LANE0 + EXP_PER_GROUP * gtop
    el = jnp.where((lane >= lo) & (lane < lo + EXP_PER_GROUP), logits, neg)
    v1 = jnp.max(el, axis=-1, keepdims=True)
    i1 = jnp.min(jnp.where(el == v1, lane, LANES), axis=-1, keepdims=True)
    el2 = jnp.where(lane == i1, neg, el)
    v2 = jnp.max(el2, axis=-1, keepdims=True)
    i2 = jnp.min(jnp.where(el2 == v2, lane, LANES), axis=-1, keepdims=True)
    p2 = jnp.exp(v2 - v1)
    den = 1.0 + p2
    gate1 = (1.0 / den) * g_w
    gate2 = (p2 / den) * g_w
    hit1 = lane == i1
    hit2 = lane == i2
    onehot = jnp.where(hit1 | hit2, 1.0, 0.0)
    before = jnp.dot(tri_ref[...], onehot.astype(BF16), preferred_element_type=F32) + carry_ref[...]
    rank1 = jnp.sum(jnp.where(hit1, before, 0.0), axis=-1, keepdims=True)
    rank2 = jnp.sum(jnp.where(hit2, before, 0.0), axis=-1, keepdims=True)
    carry_ref[...] = carry_ref[...] + jnp.sum(onehot, axis=0, keepdims=True)
    return _lane_pick({ROUTE_E1: (i1 - EXPERT_LANE0).astype(F32), ROUTE_E2: (i2 - EXPERT_LANE0).astype(F32),
                       ROUTE_G1: gate1, ROUTE_G2: gate2, ROUTE_R1: rank1, ROUTE_R2: rank2})


def _pack_bf16_halves(x):
    w = x.shape[1] // 2
    bits = lambda v: pltpu.bitcast(v.astype(F32), jnp.uint32)
    return pltpu.bitcast((bits(x[:, :w]) >> 16) | bits(x[:, w:]), F32)


def _unpack_bf16_halves(words):
    p = pltpu.bitcast(words, jnp.uint32)
    lo = pltpu.bitcast(p << 16, F32).astype(BF16)
    hi = pltpu.bitcast(p & jnp.uint32(0xFFFF0000), F32).astype(BF16)
    return jnp.concatenate([lo, hi], axis=1)


def _post_mixer_body(n_h, n_mix, *refs):
    h_refs, refs = refs[:n_h], refs[n_h:]
    m_refs, wo_refs = refs[:2 * n_mix], refs[2 * n_mix:3 * n_mix]
    g_ref, wr_ref, br_ref, tri_ref, h1_ref, hn_ref, route_ref, cnt_ref, carry_sc = refs[3 * n_mix:]

    @pl.when(pl.program_id(0) == 0)
    def _():
        carry_sc[...] = jnp.zeros((1, LANES), F32)

    mix = None
    for j, wo_ref in enumerate(wo_refs):
        part = jnp.dot(_token_rows(m_refs[2 * j], m_refs[2 * j + 1]), wo_ref[...], preferred_element_type=F32)
        mix = part if mix is None else mix + part
    h1 = (h_refs[0][...] if n_h == 1 else _token_rows(*h_refs)) + mix
    h1_ref[...] = h1
    hn = _rms(h1, g_ref[...]).astype(BF16)
    hn_ref[...] = _pack_bf16_halves(hn)
    logits = jnp.dot(hn, wr_ref[...], preferred_element_type=F32) + br_ref[...]
    route_ref[...] = _route_rows(logits, tri_ref, carry_sc)
    cnt_ref[...] = carry_sc[...]


def post_mixer(h, mix_ins, w_out, g_ffn, w_rg, b_rg, w_re, b_re):
    hs = list(h) if isinstance(h, (tuple, list)) else [h]
    h_specs = [_PROMPT_ROWS, _SAMPLE_ROWS] if len(hs) == 2 else [_rows(D_MODEL)]
    n = N_TOK
    ks = [m[0].shape[1] for m in mix_ins]
    two_part = lambda k: [pl.BlockSpec((ROW_TILE, k), lambda i: (jnp.minimum(i, N_PROMPT_TILES - 1), 0)),
                          pl.BlockSpec((ROW_TILE, k), lambda i: (0, 0))]
    offs = np.cumsum([0] + ks)
    w_parts = [w_out[offs[j]:offs[j + 1]].astype(BF16) for j in range(len(ks))]
    pad = LANES - N_GROUPS - N_EXPERTS
    wr = jnp.concatenate([w_rg, w_re, jnp.zeros((D_MODEL, pad), F32)], axis=1).astype(BF16)
    br = jnp.concatenate([b_rg, b_re, jnp.zeros((pad,), F32)]).reshape(1, LANES)
    tri = jnp.tril(jnp.ones((ROW_TILE, ROW_TILE), BF16), -1)
    return pl.pallas_call(
        functools.partial(_post_mixer_body, len(hs), len(ks)),
        grid=(n // ROW_TILE,),
        in_specs=h_specs + [s for k in ks for s in two_part(k)] + [_full((k, D_MODEL)) for k in ks]
                 + [_full((1, D_MODEL)), _full((D_MODEL, LANES)), _full((1, LANES)), _full((ROW_TILE, ROW_TILE))],
        out_specs=[_rows(D_MODEL), _rows(D_MODEL // 2), _rows(LANES), _full((1, LANES))],
        out_shape=[jax.ShapeDtypeStruct((n, D_MODEL), F32), jax.ShapeDtypeStruct((n, D_MODEL // 2), F32),
                   jax.ShapeDtypeStruct((n, LANES), F32), jax.ShapeDtypeStruct((1, LANES), F32)],
        scratch_shapes=[pltpu.VMEM((1, LANES), F32)],
        compiler_params=_params(1),
        name="post_mixer",
    )(*hs, *[part for m in mix_ins for part in m], *w_parts, g_ffn.reshape(1, D_MODEL), wr, br, tri)


MOE_ROWS = 512


def _ffn_body(be_ref, nb_ref, x_ref, w1_ref, w3_ref, w2_ref, y_ref, w1_sc, w3_sc, w2_sc):
    i = pl.program_id(0)
    new_expert = jnp.logical_or(i == 0, be_ref[i] != be_ref[jnp.maximum(i - 1, 0)])

    @pl.when(jnp.logical_and(new_expert, i < nb_ref[0]))
    def _():
        w1_sc[...] = w1_ref[0, 0].astype(BF16)
        w3_sc[...] = w3_ref[0, 0].astype(BF16)
        w2_sc[...] = w2_ref[0, 0].astype(BF16)

    @pl.when(i < nb_ref[0])
    def _():
        x = _unpack_bf16_halves(x_ref[...])
        a = jnp.dot(x, w1_sc[...], preferred_element_type=F32)
        b = jnp.dot(x, w3_sc[...], preferred_element_type=F32)
        hdn = (a * jax.nn.sigmoid(a) * b).astype(BF16)
        y_ref[...] = jnp.dot(hdn, w2_sc[...], preferred_element_type=F32)

    @pl.when(i >= nb_ref[0])
    def _():
        y_ref[...] = jnp.zeros(y_ref.shape, F32)


def expert_ffn(xs, blk_exp, n_active, w1, w3, w2, layer):
    n_slots = xs.shape[0]
    nb = n_slots // MOE_ROWS
    wmap = lambda i, be, na: (layer, be[i], 0, 0)
    xmap = lambda i, be, na: (jnp.minimum(i, na[0] - 1), 0)
    return pl.pallas_call(
        _ffn_body,
        grid_spec=pltpu.PrefetchScalarGridSpec(
            num_scalar_prefetch=2,
            grid=(nb,),
            in_specs=[pl.BlockSpec((MOE_ROWS, D_MODEL // 2), xmap),
                      pl.BlockSpec((1, 1, D_MODEL, E_HID), wmap),
                      pl.BlockSpec((1, 1, D_MODEL, E_HID), wmap),
                      pl.BlockSpec((1, 1, E_HID, D_MODEL), wmap)],
            out_specs=pl.BlockSpec((MOE_ROWS, D_MODEL), lambda i, be, na: (i, 0)),
            scratch_shapes=[pltpu.VMEM((D_MODEL, E_HID), BF16), pltpu.VMEM((D_MODEL, E_HID), BF16),
                            pltpu.VMEM((E_HID, D_MODEL), BF16)],
        ),
        out_shape=jax.ShapeDtypeStruct((n_slots, D_MODEL), F32),
        compiler_params=_params(1),
        name="expert_ffn",
    )(blk_exp, n_active, xs, w1, w3, w2)


def moe_dispatch(route, counts_row, n):
    counts = counts_row[0, EXPERT_LANE0:EXPERT_LANE0 + N_EXPERTS].astype(jnp.int32)
    padded = ((counts + MOE_ROWS - 1) // MOE_ROWS) * MOE_ROWS
    pad_end = jnp.cumsum(padded)
    pad_start = (pad_end - padded).astype(F32)
    expert_lane = lax.broadcasted_iota(jnp.int32, (1, N_EXPERTS), 1).astype(F32)
    dest = []
    for e_lane, r_lane in ((ROUTE_E1, ROUTE_R1), (ROUTE_E2, ROUTE_R2)):
        start = jnp.sum(jnp.where(route[:, e_lane:e_lane + 1] == expert_lane, pad_start[None, :], 0.0), axis=1)
        dest.append((start + route[:, r_lane]).astype(jnp.int32))
    nb = -(-(n * TOP_K_IN_GROUP) // MOE_ROWS) + N_EXPERTS
    n_slots = nb * MOE_ROWS
    tok = jnp.arange(n, dtype=jnp.int32)
    slot_tok = jnp.zeros((n_slots,), jnp.int32).at[jnp.concatenate(dest)].set(
        jnp.concatenate([tok] * TOP_K_IN_GROUP), unique_indices=True, mode='promise_in_bounds')
    blk_start = jnp.arange(nb, dtype=jnp.int32) * MOE_ROWS
    blk_exp = jnp.sum((pad_end[None, :] <= blk_start[:, None]).astype(jnp.int32), axis=1)
    blk_exp = jnp.minimum(blk_exp, N_EXPERTS - 1)
    n_active = (pad_end[-1] // MOE_ROWS).astype(jnp.int32).reshape(1)
    return slot_tok, dest, blk_exp, n_active


def moe_experts(hn, route, counts_row, w1, w3, w2, layer):
    n = hn.shape[0]
    slot_tok, dest, blk_exp, n_active = moe_dispatch(route, counts_row, n)
    xs = hn.at[slot_tok].get(mode='promise_in_bounds')
    ys = expert_ffn(xs, blk_exp, n_active, w1, w3, w2, layer)
    return [ys.at[d].get(mode='promise_in_bounds') for d in dest]


def _combine(h_ref, y1_ref, y2_ref, route_ref):
    lane = lax.broadcasted_iota(jnp.int32, (ROW_TILE, LANES), 1)
    r = route_ref[...]
    g1 = jnp.sum(jnp.where(lane == ROUTE_G1, r, 0.0), axis=-1, keepdims=True)
    g2 = jnp.sum(jnp.where(lane == ROUTE_G2, r, 0.0), axis=-1, keepdims=True)
    return h_ref[...] + (y1_ref[...] * g1 + y2_ref[...] * g2)


def _combine_proj_body(h_ref, y1_ref, y2_ref, route_ref, g_ref, w_ref, h2_ref, u_ref):
    h2 = _combine(h_ref, y1_ref, y2_ref, route_ref)
    h2_ref[...] = h2
    u_ref[...] = jnp.dot(_rms(h2, g_ref[...]).astype(BF16), w_ref[...], preferred_element_type=F32)


def _combine_norm_body(h_ref, y1_ref, y2_ref, route_ref, g_ref, yp_ref, ys_ref):
    y = _rms(_combine(h_ref, y1_ref, y2_ref, route_ref), g_ref[...])

    @pl.when(pl.program_id(0) < N_PROMPT_TILES)
    def _():
        yp_ref[...] = y

    @pl.when(pl.program_id(0) >= N_PROMPT_TILES)
    def _():
        ys_ref[...] = y


def combine_proj(h, ys, route, g, w):
    n = h.shape[0]
    nn = w.shape[1]
    return pl.pallas_call(
        _combine_proj_body,
        grid=(n // ROW_TILE,),
        in_specs=[_rows(D_MODEL), _rows(D_MODEL), _rows(D_MODEL), _rows(LANES), _full((1, D_MODEL)),
                  _full((D_MODEL, nn))],
        out_specs=[_rows(D_MODEL), _rows(nn)],
        out_shape=[jax.ShapeDtypeStruct((n, D_MODEL), F32), jax.ShapeDtypeStruct((n, nn), F32)],
        compiler_params=_params(1),
        name="combine_proj",
    )(h, *ys, route, g.reshape(1, D_MODEL), w.astype(BF16))


def combine_norm(h, ys, route, g):
    n = h.shape[0]
    return pl.pallas_call(
        _combine_norm_body,
        grid=(n // ROW_TILE,),
        in_specs=[_rows(D_MODEL), _rows(D_MODEL), _rows(D_MODEL), _rows(LANES), _full((1, D_MODEL))],
        out_specs=[_PROMPT_ROWS, _SAMPLE_ROWS],
        out_shape=[jax.ShapeDtypeStruct((N_PROMPT, D_MODEL), F32), jax.ShapeDtypeStruct((N_SAMPLE, D_MODEL), F32)],
        compiler_params=_params(1),
        name="combine_norm",
    )(h, *ys, route, g.reshape(1, D_MODEL))


def _even_first(x, axis):
    n = x.shape[axis]
    idx = jnp.concatenate([jnp.arange(0, n, 2), jnp.arange(1, n, 2)])
    return jnp.take(x, idx, axis=axis)


def _decode_query_cols(q_s, gates_s):
    B, T, G, R = DEC_BATCH, DEC_SEQ, NSA_KV, NSA_REP
    qg = q_s.reshape(B, T, G, R, HD).transpose(0, 2, 4, 3, 1).reshape(B, G, HD, R * T)
    qt = jnp.zeros((B, G, HD, G, R * T), BF16)
    for g in range(G):
        qt = qt.at[:, g, :, g, :].set(qg[:, g])
    qt = jnp.pad(qt.reshape(B, G * HD, G * R * T), ((0, 0), (0, 0), (0, LANES - G * R * T)))
    gr = gates_s[:, :3 * NSA_HEADS].reshape(B, T, NSA_HEADS, 3).transpose(0, 3, 2, 1).reshape(B, 3, NSA_HEADS * T)
    return qt, jnp.pad(gr, ((0, 0), (0, 0), (0, LANES - NSA_HEADS * T)))


def _feature_major(cache):
    lead, rows = cache.shape[:2]
    return jnp.transpose(cache, (0, 2, 3, 4, 1)).reshape(lead, NSA_KV_W, rows)


def _kv_rows_from_feature_major(xt, lead):
    x = xt.reshape(2, NSA_KV, HD, *lead)
    n = len(lead)
    return jnp.transpose(x, tuple(range(3, 3 + n)) + (0, 1, 2))


def mixer_a(h, p, past):
    pos = np.concatenate([np.arange(SEQ), PAST_LEN + np.tile(np.arange(DEC_SEQ), DEC_BATCH)])
    (q, kvc, kvs, kvw, kvct_p, kvct_s, kvst_p, kvst_s, kvwt_p, kvwt_s, ks, vst, kw, vwt, gates, gq, gk, gv,
     la, sgg) = inproj_a(
        *h, p['norm_mix'][0], p['a_w_in'][0], p['a_gla_wa2'][0], p['a_gla_ba'][0], pos)
    cmp_w, gnorm = _cmp_weights(p['a_cmp_pe'][0], p['a_cmp_w'][0]), p['a_gla_norm'][0]
    P = N_PROMPT
    ckv = cmp_blocks_rows(kvc, P, cmp_w)
    ckv = _even_first(ckv.reshape(BATCH, N_CMP_PROMPT, NSA_KV_W), 1)
    o_nsa_p = nsa_prompt_attn(q, gates, ckv, ks, vst, kw, vwt)
    o_gla_p, st_p = gla_seq(gq, gk, la, gv, sgg, gnorm)
    rows_p = lambda xt: jnp.transpose(xt.reshape(BATCH, 2, NSA_KV, HD, xt.shape[2]), (0, 4, 1, 2, 3))
    new_p = (rows_p(kvct_p), rows_p(kvst_p), rows_p(kvwt_p[:, :, SEQ - WINDOW:]), _gla_state_from_pairs(st_p))
    o_gla_s, st_s = gla_step(gq, gk, la, gv, sgg, gnorm, _gla_state_to_pairs(past['state_gla'][0]), P)
    n_pool = past['cache_cmp_kv'].shape[1]
    ckv_pool = cmp_blocks_pages(_feature_major(past['cache_cmp_kv'][0]), cmp_w)
    ckv_seq = ckv_pool.reshape(n_pool, PAGE_SIZE // CMP_BLOCK, NSA_KV_W)[past['page_table']]
    ckv_seq = _even_first(ckv_seq.reshape(DEC_BATCH, N_CMP_DEC, NSA_KV_W), 1)
    qt, gate_rows = _decode_query_cols(q[P:], gates[P:])
    per_seq = lambda x: x[P:].reshape(DEC_BATCH, 1, DEC_SEQ * NSA_KV_W)
    o_t, win_new = nsa_decode_attn(qt, gate_rows, ckv_seq, _feature_major(past['cache_slc_kv'][0]),
                                   _feature_major(past['cache_win_kv'][0]), per_seq(kvs), per_seq(kvw),
                                   past['page_table'])
    o_nsa_s = o_t[:, :, :NSA_HEADS * DEC_SEQ].reshape(DEC_BATCH, HD, NSA_HEADS, DEC_SEQ)
    o_nsa_s = o_nsa_s.transpose(0, 3, 2, 1).reshape(N_SAMPLE, NSA_Q_W).astype(BF16)
    n_buf = win_new.shape[2]
    win_s = jnp.transpose(win_new.reshape(DEC_BATCH, 2, NSA_KV, HD, n_buf), (0, 4, 1, 2, 3))
    new_s = (_kv_rows_from_feature_major(kvct_s[:, :N_SAMPLE], (DEC_BATCH, DEC_SEQ)),
             _kv_rows_from_feature_major(kvst_s[:, :N_SAMPLE], (DEC_BATCH, DEC_SEQ)), win_s,
             _gla_state_from_pairs(st_s))
    return (o_nsa_p, o_nsa_s), (o_gla_p, o_gla_s), new_p, new_s


def run_trunk(x_prompt, x_sample, p, past):
    h0 = (x_prompt.reshape(N_PROMPT, D_MODEL), x_sample.reshape(N_SAMPLE, D_MODEL))
    o_nsa, o_gla, new_p, new_s = mixer_a(h0, p, past)
    h, hn, route, counts = post_mixer(h0, [o_nsa, o_gla], p['a_w_out'][0], p['norm_ffn'][0], p['m_w_rg'][0],
                                      p['m_b_rg'][0], p['m_w_re'][0], p['m_b_re'][0])
    ys = moe_experts(hn, route, counts, p['m_w1'], p['m_w3'], p['m_w2'], 0)
    h, u = combine_proj(h, ys, route, p['norm_mix'][1], p['c_w_in'][0])
    lru_w = (p['c_conv_w'][0], p['c_conv_b'][0], p['c_w_a'][0], p['c_b_a'][0], p['c_w_x'][0], p['c_b_x'][0],
             p['c_lam'][0])
    us = u[N_PROMPT:].reshape(DEC_BATCH, DEC_SEQ, 2 * D_RNN)
    y_p, lru_p = lru_seq(u, *lru_w)
    y_s, lru_s = lru_step(jnp.swapaxes(us, 0, 1), past['state_conv'][0], past['state_lru'][0], *lru_w)
    conv_p = jnp.stack([u[(b + 1) * SEQ - (CONV_W - 1):(b + 1) * SEQ, D_RNN:] for b in range(BATCH)])
    conv_s = us[:, DEC_SEQ - (CONV_W - 1):, D_RNN:]
    mix_in = (y_p, jnp.swapaxes(y_s, 0, 1).reshape(N_SAMPLE, D_RNN))
    h, hn, route, counts = post_mixer(h, [mix_in], p['c_w_out'][0], p['norm_ffn'][1], p['m_w_rg'][1],
                                      p['m_b_rg'][1], p['m_w_re'][1], p['m_b_re'][1])
    ys = moe_experts(hn, route, counts, p['m_w1'], p['m_w3'], p['m_w2'], 1)
    y_p, y_s = combine_norm(h, ys, route, p['norm_final'])
    y_prompt = y_p.reshape(BATCH, SEQ, D_MODEL)
    y_sample = y_s.reshape(DEC_BATCH, DEC_SEQ, D_MODEL)
    return (y_prompt, y_sample), new_p + (lru_p, conv_p), new_s + (lru_s, conv_s)


def kernel(x_prompt, x_sample, cache_cmp_kv, cache_slc_kv, cache_win_kv, state_gla, state_lru, state_conv,
           page_table, norm_mix, norm_ffn, norm_final, a_w_in, a_cmp_pe, a_cmp_w, a_gla_wa2, a_gla_ba,
           a_gla_norm, a_w_out, c_w_in, c_conv_w, c_conv_b, c_w_a, c_b_a, c_w_x, c_b_x, c_lam, c_w_out,
           m_w_rg, m_b_rg, m_w_re, m_b_re, m_w1, m_w3, m_w2):
    p = {'norm_mix': norm_mix, 'norm_ffn': norm_ffn, 'norm_final': norm_final,
         'a_w_in': a_w_in, 'a_cmp_pe': a_cmp_pe, 'a_cmp_w': a_cmp_w, 'a_gla_wa2': a_gla_wa2,
         'a_gla_ba': a_gla_ba, 'a_gla_norm': a_gla_norm, 'a_w_out': a_w_out,
         'c_w_in': c_w_in, 'c_conv_w': c_conv_w, 'c_conv_b': c_conv_b, 'c_w_a': c_w_a, 'c_b_a': c_b_a,
         'c_w_x': c_w_x, 'c_b_x': c_b_x, 'c_lam': c_lam, 'c_w_out': c_w_out,
         'm_w_rg': m_w_rg, 'm_b_rg': m_b_rg, 'm_w_re': m_w_re, 'm_b_re': m_b_re,
         'm_w1': m_w1, 'm_w3': m_w3, 'm_w2': m_w2}
    past = {'cache_cmp_kv': cache_cmp_kv, 'cache_slc_kv': cache_slc_kv, 'cache_win_kv': cache_win_kv,
            'state_gla': state_gla, 'state_lru': state_lru, 'state_conv': state_conv,
            'page_table': page_table}
    (y_p, y_s), sp, ss = run_trunk(x_prompt, x_sample, p, past)
    outs = [y_p, y_s]
    for a, b in zip(sp, ss):
        outs += [a[None], b[None]]
    return tuple(outs)
```

```python
import functools
import jax, jax.numpy as jnp
from jax import lax
import numpy as np
from jax.experimental import pallas as pl
from jax.experimental.pallas import tpu as pltpu

D_MODEL = 1024
BATCH = 2
SEQ = 8192
DEC_BATCH = 128
DEC_SEQ = 4
PAST_LEN = 2048
PAGE_SIZE = 128
EPS = 1e-6
NSA_HEADS = 8
NSA_KV = 2
NSA_REP = NSA_HEADS // NSA_KV
HD = 64
CMP_BLOCK = 32
SLC_BLOCK = 64
SLC_TOPK = 16
WINDOW = 512
Q_BLOCK = 128
ROPE_DIM = HD // 4
ROPE_THETA = 500000.0
GLA_HEADS = 4
GLA_DK = 64
GLA_DV = 128
GLA_LOWRANK = 16
GLA_TAU = 16.0
D_RNN = 1280
LRU_BLOCKS = 10
LRU_BW = D_RNN // LRU_BLOCKS
CONV_W = 4
LRU_C = 8.0
N_GROUPS = 4
EXP_PER_GROUP = 8
N_EXPERTS = N_GROUPS * EXP_PER_GROUP
E_HID = 512
TOP_K_IN_GROUP = 2
NSA_Q_W = NSA_HEADS * HD
NSA_KV_W = 2 * NSA_KV * HD
GLA_K_W = GLA_HEADS * GLA_DK
GLA_V_W = GLA_HEADS * GLA_DV
A_SIZES = (NSA_Q_W, NSA_KV_W, NSA_KV_W, NSA_KV_W, 3 * NSA_HEADS, GLA_K_W, GLA_K_W, GLA_V_W, GLA_LOWRANK, GLA_V_W)
N_PROMPT = BATCH * SEQ
N_SAMPLE = DEC_BATCH * DEC_SEQ
N_TOK = N_PROMPT + N_SAMPLE
N_PAGES = PAST_LEN // PAGE_SIZE

F32 = jnp.float32
BF16 = jnp.bfloat16
VMEM_LIMIT_BYTES = 56 * 1024 * 1024
LANES = 128
SUBLANES = 8
ROW_TILE = 512


def _params(n_axes):
    return pltpu.CompilerParams(dimension_semantics=("arbitrary",) * n_axes, vmem_limit_bytes=VMEM_LIMIT_BYTES)


def _full(shape):
    return pl.BlockSpec(shape, lambda *_: (0,) * len(shape))


def _rows(width):
    return pl.BlockSpec((ROW_TILE, width), lambda i: (i, 0))


def _rms(x, g):
    return x * lax.rsqrt(jnp.mean(x * x, axis=-1, keepdims=True) + EPS) * g


def _softplus(x):
    return jnp.maximum(x, 0.0) + jnp.log1p(jnp.exp(-jnp.abs(x)))


def _gelu_tanh(x):
    return x * (0.5 * (1.0 + jnp.tanh(0.7978845608028654 * (x + 0.044715 * (x * x * x)))))


def _nt_dot(a, b):
    return lax.dot_general(a, b, (((1,), (1,)), ((), ())), preferred_element_type=F32)


def _tn_dot(a, b):
    return lax.dot_general(a, b, (((0,), (0,)), ((), ())), preferred_element_type=F32)


A_Q0, A_KVC0, A_KVS0, A_KVW0 = 0, 512, 768, 1024
A_GQ0, A_GK0, A_GV0, A_GG0, A_MISC0 = 1280, 1536, 1792, 2304, 2816
A_COLS = A_MISC0 + LANES
MISC_LR0 = 3 * NSA_HEADS
KV_HALF = NSA_KV * HD


def _rope_lanes(x, cos_t, sin_lo, sin_hi):
    reps = x.shape[1] // LANES
    tile = (lambda t: jnp.concatenate([t] * reps, axis=1)) if reps > 1 else (lambda t: t)
    w = x.shape[1]
    half = ROPE_DIM // 2
    return x * tile(cos_t) + pltpu.roll(x, half, 1) * tile(sin_hi) + pltpu.roll(x, w - half, 1) * tile(sin_lo)


N_PROMPT_TILES = N_PROMPT // ROW_TILE
_PROMPT_ROWS = pl.BlockSpec((ROW_TILE, D_MODEL), lambda i: (jnp.minimum(i, N_PROMPT_TILES - 1), 0))
_SAMPLE_ROWS = pl.BlockSpec((ROW_TILE, D_MODEL), lambda i: (0, 0))


def _token_rows(prompt_ref, sample_ref):
    return jnp.where(pl.program_id(0) < N_PROMPT_TILES, prompt_ref[...], sample_ref[...])


def _inproj_a_body(hp_ref, hs_ref, g_ref, w_ref, wa2_ref, ba_ref, cos_ref, slo_ref, shi_ref,
                   q_ref, kvc_ref, kvs_ref, kvw_ref, kvct_p_ref, kvct_s_ref, kvst_p_ref, kvst_s_ref, kvwt_p_ref,
                   kvwt_s_ref, ks_ref, vst_ref, kw_ref, vwt_ref, gates_ref, gq_ref, gk_ref, gv_ref, la_ref,
                   sgg_ref):
    y = _rms(jnp.where(pl.program_id(0) == 0, hs_ref[...], hp_ref[...]), g_ref[...]).astype(BF16)
    proj = lambda a, b: jnp.dot(y, w_ref[:, a:b], preferred_element_type=F32)
    cos_t, sin_lo, sin_hi = cos_ref[...], slo_ref[...], shi_ref[...]
    q_ref[...] = (_rope_lanes(proj(A_Q0, A_KVC0), cos_t, sin_lo, sin_hi) * (HD ** -0.5)).astype(BF16)

    def store_feature_major(p_ref, s_ref, xt):
        p_ref[0] = xt
        s_ref[...] = xt

    kvc = proj(A_KVC0, A_KVS0)
    kvc_ref[...] = kvc
    store_feature_major(kvct_p_ref, kvct_s_ref, kvc.T)
    for a0, kv_ref, kvt_p_ref, kvt_s_ref, k_ref, vt_ref in (
            (A_KVS0, kvs_ref, kvst_p_ref, kvst_s_ref, ks_ref, vst_ref),
            (A_KVW0, kvw_ref, kvwt_p_ref, kvwt_s_ref, kw_ref, vwt_ref)):
        kv = proj(a0, a0 + 2 * KV_HALF)
        k = _rope_lanes(kv[:, :KV_HALF], cos_t, sin_lo, sin_hi)
        vt = kv[:, KV_HALF:].T
        kv_ref[:, :KV_HALF] = k
        kv_ref[:, KV_HALF:] = kv[:, KV_HALF:]
        store_feature_major(kvt_p_ref, kvt_s_ref, jnp.concatenate([k.T, vt], axis=0))
        k_ref[...] = k.astype(BF16)
        vt_ref[...] = vt.astype(BF16)
    misc = proj(A_MISC0, A_COLS)
    gates_ref[...] = jax.nn.sigmoid(misc)
    z = jnp.dot(misc.astype(BF16), wa2_ref[...], preferred_element_type=F32) + ba_ref[...]
    la_ref[...] = -_softplus(-z) * (1.0 / GLA_TAU)
    gq_ref[...] = proj(A_GQ0, A_GK0) * (GLA_DK ** -0.5)
    gk_ref[...] = proj(A_GK0, A_GV0)
    gv_ref[...] = proj(A_GV0, A_GG0).astype(BF16)
    gg = proj(A_GG0, A_MISC0)
    sgg_ref[...] = gg * jax.nn.sigmoid(gg)


def _rope_tables(pos):
    pos = np.asarray(pos)
    half = ROPE_DIM // 2
    f4 = np.float32
    inv = (f4(1.0) / (f4(ROPE_THETA) ** (np.arange(0, ROPE_DIM, 2, dtype=f4) / f4(ROPE_DIM)))).astype(f4)
    ang = (pos.astype(f4)[:, None] * inv[None, :]).astype(np.float64)
    cos, sin = np.cos(ang).astype(f4), np.sin(ang).astype(f4)
    n = pos.shape[0]
    one = np.ones((n, HD - ROPE_DIM), f4)
    zero = np.zeros((n, HD - ROPE_DIM), f4)
    zh = np.zeros((n, half), f4)
    seg = lambda a, b, rest: jnp.asarray(np.concatenate([a, b, rest] * (LANES // HD), axis=1))
    return seg(cos, cos, one), seg(-sin, zh, zero), seg(zh, sin, zero)


def inproj_a(h_prompt, h_sample, g, w_in, wa2, ba, pos):
    n = h_prompt.shape[0] + h_sample.shape[0]
    zpad = jnp.zeros((D_MODEL, LANES - 3 * NSA_HEADS - GLA_LOWRANK), F32)
    o = np.cumsum((0,) + A_SIZES)
    w = jnp.concatenate([w_in[:, o[0]:o[4]], w_in[:, o[5]:o[8]], w_in[:, o[9]:o[10]],
                         w_in[:, o[4]:o[5]], w_in[:, o[8]:o[9]], zpad], axis=1).astype(BF16)
    wa2p = jnp.zeros((LANES, GLA_K_W), F32).at[MISC_LR0:MISC_LR0 + GLA_LOWRANK].set(wa2).astype(BF16)
    seq_tiles = SEQ // ROW_TILE
    ptile = lambda s: jnp.maximum(s - 1, 0)
    tile = lambda s: jnp.where(s == 0, N_PROMPT_TILES, s - 1)
    rows_t = lambda wd: pl.BlockSpec((ROW_TILE, wd), lambda s: (tile(s), 0))
    pos_rows = pl.BlockSpec((ROW_TILE, LANES), lambda s: (jnp.where(s == 0, seq_tiles, (s - 1) % seq_tiles), 0))
    cols = pl.BlockSpec((KV_HALF, ROW_TILE), lambda s: (0, tile(s)))
    kvt_p = (pl.BlockSpec((1, NSA_KV_W, ROW_TILE), lambda s: (ptile(s) // seq_tiles, 0, ptile(s) % seq_tiles)),
             (BATCH, NSA_KV_W, SEQ), F32)
    kvt_s = (pl.BlockSpec((NSA_KV_W, ROW_TILE), lambda s: (0, jnp.minimum(s, 1))), (NSA_KV_W, 2 * ROW_TILE), F32)
    outs = [(rows_t(NSA_Q_W), (n, NSA_Q_W), BF16), (rows_t(NSA_KV_W), (n, NSA_KV_W), F32),
            (rows_t(NSA_KV_W), (n, NSA_KV_W), F32), (rows_t(NSA_KV_W), (n, NSA_KV_W), F32),
            kvt_p, kvt_s, kvt_p, kvt_s, kvt_p, kvt_s,
            (rows_t(KV_HALF), (n, KV_HALF), BF16), (cols, (KV_HALF, n), BF16),
            (rows_t(KV_HALF), (n, KV_HALF), BF16), (cols, (KV_HALF, n), BF16),
            (rows_t(LANES), (n, LANES), F32), (rows_t(GLA_K_W), (n, GLA_K_W), F32),
            (rows_t(GLA_K_W), (n, GLA_K_W), F32), (rows_t(GLA_V_W), (n, GLA_V_W), BF16),
            (rows_t(GLA_K_W), (n, GLA_K_W), F32), (rows_t(GLA_V_W), (n, GLA_V_W), F32)]
    return pl.pallas_call(
        _inproj_a_body,
        grid=(n // ROW_TILE,),
        in_specs=[pl.BlockSpec((ROW_TILE, D_MODEL), lambda s: (ptile(s), 0)), _SAMPLE_ROWS, _full((1, D_MODEL)),
                  _full((D_MODEL, A_COLS)), _full((LANES, GLA_K_W)),
                  _full((1, GLA_K_W)), pos_rows, pos_rows, pos_rows],
        out_specs=[s for s, _, _ in outs],
        out_shape=[jax.ShapeDtypeStruct(shape, dt) for _, shape, dt in outs],
        compiler_params=_params(1),
        name="inproj_a",
    )(h_prompt, h_sample, g.reshape(1, D_MODEL), w, wa2p, ba.reshape(1, GLA_K_W), *_rope_tables(pos))


CMP_TILE_BLOCKS = 256
CMP_TILE_ROWS = CMP_TILE_BLOCKS * CMP_BLOCK
CMP_TILE_PAGES = CMP_TILE_ROWS // PAGE_SIZE


def _cmp_reduce(xk_ref, xv_ref, pe_ref, w_ref):
    acc = jnp.zeros((CMP_TILE_BLOCKS, NSA_KV_W), F32)
    for l in range(CMP_BLOCK):
        rows = pl.ds(l, CMP_TILE_BLOCKS, stride=CMP_BLOCK)
        xl = jnp.concatenate([xk_ref[rows, :], xv_ref[rows, :]], axis=1) + pe_ref[l:l + 1, :]
        acc = acc + jnp.dot(xl.astype(BF16), w_ref[l], preferred_element_type=F32)
    return acc


def _cmp_rows_body(xk_ref, xv_ref, pe_ref, w_ref, o_ref):
    o_ref[...] = _cmp_reduce(xk_ref, xv_ref, pe_ref, w_ref)


def _cmp_pages_body(x_ref, pe_ref, w_ref, o_ref, xk_sc, xv_sc):
    for pg in range(CMP_TILE_PAGES):
        rows = slice(pg * PAGE_SIZE, (pg + 1) * PAGE_SIZE)
        xk_sc[rows, :] = x_ref[pg, :KV_HALF, :].T
        xv_sc[rows, :] = x_ref[pg, KV_HALF:, :].T
    o_ref[...] = _cmp_reduce(xk_sc, xv_sc, pe_ref, w_ref)


def _cmp_weights(pe, w_cmp):
    pe_rows = jnp.broadcast_to(pe[:, :, None, :], (CMP_BLOCK, 2, NSA_KV, HD)).reshape(CMP_BLOCK, NSA_KV_W)
    w = w_cmp.astype(BF16)
    zero = jnp.zeros((CMP_BLOCK, HD, HD), BF16)
    diag = [w[:, c] for c in range(2) for _ in range(NSA_KV)]
    w_bd = jnp.concatenate([jnp.concatenate([blk if j == i else zero for j in range(len(diag))], axis=2)
                            for i, blk in enumerate(diag)], axis=1)
    return pe_rows, w_bd


def cmp_blocks_rows(x, n_rows, cmp_weights):
    return pl.pallas_call(
        _cmp_rows_body,
        grid=(n_rows // CMP_TILE_ROWS,),
        in_specs=[pl.BlockSpec((CMP_TILE_ROWS, KV_HALF), lambda i: (i, 0)),
                  pl.BlockSpec((CMP_TILE_ROWS, KV_HALF), lambda i: (i, 1)), _full((CMP_BLOCK, NSA_KV_W)),
                  _full((CMP_BLOCK, NSA_KV_W, NSA_KV_W))],
        out_specs=pl.BlockSpec((CMP_TILE_BLOCKS, NSA_KV_W), lambda i: (i, 0)),
        out_shape=jax.ShapeDtypeStruct((n_rows // CMP_BLOCK, NSA_KV_W), F32),
        compiler_params=_params(1),
        name="cmp_blocks_rows",
    )(x, x, *cmp_weights)


def cmp_blocks_pages(xt, cmp_weights):
    n_pages = xt.shape[0]
    return pl.pallas_call(
        _cmp_pages_body,
        grid=(n_pages // CMP_TILE_PAGES,),
        in_specs=[pl.BlockSpec((CMP_TILE_PAGES, NSA_KV_W, PAGE_SIZE), lambda i: (i, 0, 0)),
                  _full((CMP_BLOCK, NSA_KV_W)), _full((CMP_BLOCK, NSA_KV_W, NSA_KV_W))],
        out_specs=pl.BlockSpec((CMP_TILE_BLOCKS, NSA_KV_W), lambda i: (i, 0)),
        out_shape=jax.ShapeDtypeStruct((n_pages * PAGE_SIZE // CMP_BLOCK, NSA_KV_W), F32),
        scratch_shapes=[pltpu.VMEM((CMP_TILE_ROWS, KV_HALF), F32), pltpu.VMEM((CMP_TILE_ROWS, KV_HALF), F32)],
        compiler_params=_params(1),
        name="cmp_blocks_pages",
    )(xt, *cmp_weights)


KEY_TILE = 256
NEG_BIG = -1e30
N_CMP_PROMPT = SEQ // CMP_BLOCK
N_SLC_PROMPT = SEQ // SLC_BLOCK
SLC_SHIFT = SLC_BLOCK.bit_length() - 1
N_CMP_DEC = PAST_LEN // CMP_BLOCK
N_SLC_DEC = -(-(PAST_LEN + DEC_SEQ) // SLC_BLOCK)
N_SLC_DEC_PAD = -(-N_SLC_DEC // LANES) * LANES
DEC_COLS_PER_GROUP = NSA_REP * DEC_SEQ


def _tile_cols(x, reps):
    return jnp.concatenate([x] * reps, axis=1) if reps > 1 else x


def _even_first_cmp_end(n_cmp):
    j = lax.broadcasted_iota(jnp.int32, (n_cmp, 1), 0)
    blk = jnp.where(j < n_cmp // 2, 2 * j, 2 * (j - n_cmp // 2) + 1)
    return (blk + 1) * CMP_BLOCK - 1


def _softmax_cols(s, mask):
    s = jnp.where(mask, s, -jnp.inf)
    m = jnp.max(s, axis=0, keepdims=True)
    m = jnp.where(m > -jnp.inf, m, 0.0)
    e = jnp.where(mask, jnp.exp(s - m), 0.0)
    return e / jnp.maximum(jnp.sum(e, axis=0, keepdims=True), 1e-30)


def _softmax_cols_biased(s):
    m = jnp.max(s, axis=0, keepdims=True)
    e = jnp.exp(s - m)
    scale = jnp.where(m > 0.5 * NEG_BIG, 1.0 / jnp.sum(e, axis=0, keepdims=True), 0.0)
    return e * scale


N_FORCED = 3


def _select_cols(p_slc, qpos, n_top):
    ns = p_slc.shape[0]
    blk = lax.broadcasted_iota(jnp.int32, p_slc.shape, 0)
    cur = qpos >> SLC_SHIFT
    forced = ((blk == 0) | (blk == cur) | (blk == cur - 1)) & (blk <= cur)
    score = jnp.where((blk <= cur) & jnp.logical_not(forced), p_slc, -jnp.inf)
    sel = jnp.where(forced, 1.0, 0.0)
    for _ in range(n_top - N_FORCED):
        m = jnp.max(score, axis=0, keepdims=True)
        idx = jnp.min(jnp.where(score == m, blk, ns), axis=0, keepdims=True)
        hit = blk == idx
        sel = jnp.where(hit & (m > -jnp.inf), 1.0, sel)
        score = jnp.where(hit, -jnp.inf, score)
    return sel


def _flash_init(m_sc, l_sc, acc_sc):
    m_sc[...] = jnp.full(m_sc.shape, NEG_BIG, F32)
    l_sc[...] = jnp.zeros(l_sc.shape, F32)
    acc_sc[...] = jnp.zeros(acc_sc.shape, F32)


def _flash_cols(scores, mask, pv, m_sc, l_sc, acc_sc):
    s = jnp.where(mask, scores, NEG_BIG)
    m_old = m_sc[...]
    m_new = jnp.maximum(m_old, jnp.max(s, axis=0, keepdims=True))
    alpha = jnp.exp(m_old - m_new)
    p = jnp.where(mask, jnp.exp(s - m_new), 0.0)
    l_sc[...] = alpha * l_sc[...] + jnp.sum(p, axis=0, keepdims=True)
    acc_sc[...] = alpha * acc_sc[...] + pv(p.astype(BF16))
    m_sc[...] = m_new


def _flash_stream(score_fn, pv_fn, first, lo, n, stream_sc, m_sc, l_sc, acc_sc):
    (sa, ca), (sb, cb) = stream_sc

    def issue(s_ref, c_ref, kt, self_tile=False):
        s = score_fn(kt, self_tile)
        s_ref[...] = s
        c_ref[...] = jnp.max(s, axis=0, keepdims=True)

    def consume(s_ref, c_ref, kt):
        m_old = m_sc[...]
        m_new = jnp.maximum(m_old, c_ref[...])
        alpha = jnp.exp(m_old - m_new)
        p = jnp.exp(s_ref[...] - m_new)
        l_sc[...] = alpha * l_sc[...] + jnp.sum(p, axis=0, keepdims=True)
        acc_sc[...] = alpha * acc_sc[...] + pv_fn(kt, p.astype(BF16))
        m_sc[...] = m_new

    _flash_init(m_sc, l_sc, acc_sc)
    issue(sa, ca, first, True)

    def two_tiles(jj, kt_a):
        t0 = lo + 2 * jj
        issue(sb, cb, t0)
        consume(sa, ca, kt_a)
        issue(sa, ca, t0 + 1)
        consume(sb, cb, t0)
        return t0 + 1

    kt_a = lax.fori_loop(0, n // 2, two_tiles, first)

    @pl.when(n % 2 == 1)
    def _():
        issue(sb, cb, lo + n - 1)
        consume(sa, ca, kt_a)
        consume(sb, cb, lo + n - 1)

    @pl.when(n % 2 == 0)
    def _():
        consume(sa, ca, kt_a)

    return acc_sc[...] / jnp.maximum(l_sc[...], 1e-30)


def _flash_out(l_sc, acc_sc):
    return acc_sc[...] / jnp.maximum(l_sc[...], 1e-30)


def _pv_split(vt, p):
    c2 = p.shape[1] // 2
    return jnp.concatenate([jnp.dot(vt[:HD], p[:, :c2], preferred_element_type=F32),
                            jnp.dot(vt[HD:], p[:, c2:], preferred_element_type=F32)], axis=1)


def _nsa_prompt_body(q_ref, gates_ref, ckv_ref, cos_ref, slo_ref, shi_ref, ks_ref, vst_ref, kw_ref, vwt_ref,
                     o_ref, m_sc, l_sc, acc_sc, sel_bias_sc, sa_sc, ca_sc, sb_sc, cb_sc):
    stream_sc = ((sa_sc, ca_sc), (sb_sc, cb_sc))
    i = pl.program_id(1)
    nq = Q_BLOCK
    cols = NSA_HEADS * nq
    qpos = i * nq + lax.broadcasted_iota(jnp.int32, (1, nq), 1)
    q = q_ref[...].astype(F32)
    pairs = [q[:, j * LANES:(j + 1) * LANES].T for j in range(NSA_HEADS // 2)]
    zero = jnp.zeros((HD, cols // 2), F32)
    qt_g = [jnp.concatenate([pairs[2 * g][:HD], pairs[2 * g][HD:], pairs[2 * g + 1][:HD], pairs[2 * g + 1][HD:]],
                            axis=1) for g in range(NSA_KV)]
    qt = jnp.concatenate([jnp.concatenate([qt_g[0], zero], axis=1),
                          jnp.concatenate([zero, qt_g[1]], axis=1)], axis=0).astype(BF16)
    ckv = ckv_ref[0]
    ck = _rope_lanes(ckv[:, :KV_HALF], cos_ref[...], slo_ref[...], shi_ref[...]).astype(BF16)
    cvt = ckv[:, KV_HALF:].T.astype(BF16)
    c_bias = jnp.where(_even_first_cmp_end(N_CMP_PROMPT) <= qpos, 0.0, NEG_BIG)
    p = _softmax_cols_biased(jnp.dot(ck, qt, preferred_element_type=F32) + _tile_cols(c_bias, NSA_HEADS))
    o_c = _pv_split(cvt, p.astype(BF16))
    sel = []
    for g in range(NSA_KV):
        c0 = g * NSA_REP * nq
        p_grp = p[:, c0:c0 + nq]
        for r in range(1, NSA_REP):
            p_grp = p_grp + p[:, c0 + r * nq:c0 + (r + 1) * nq]
        p_slc = p_grp[:N_CMP_PROMPT // 2] + p_grp[N_CMP_PROMPT // 2:]
        sel.append(_select_cols(p_slc, qpos, SLC_TOPK))
    sel = jnp.concatenate(sel, axis=1)
    sel_bias_sc[...] = jnp.where(sel > 0.5, 0.0, NEG_BIG)
    key_row = lax.broadcasted_iota(jnp.int32, (KEY_TILE, 1), 0)
    blocks_per_tile = KEY_TILE // SLC_BLOCK
    kt_self = i // (KEY_TILE // nq)

    def slc_scores(kt, self_tile):
        k0 = pl.multiple_of(kt * KEY_TILE, KEY_TILE)
        rows = []
        for j in range(blocks_per_tile):
            b = sel_bias_sc[pl.ds(kt * blocks_per_tile + j, 1), :]
            rows.append(jnp.concatenate(
                [jnp.broadcast_to(b[:, g * nq:(g + 1) * nq], (SLC_BLOCK, nq)) for g in range(NSA_KV)
                 for _ in range(NSA_REP)], axis=1))
        bias = jnp.concatenate(rows, axis=0)
        if self_tile:
            bias = bias + _tile_cols(jnp.where(k0 + key_row <= qpos, 0.0, NEG_BIG), NSA_HEADS)
        return jnp.dot(ks_ref[pl.ds(k0, KEY_TILE), :], qt, preferred_element_type=F32) + bias

    def slc_pv(kt, pb):
        return _pv_split(vst_ref[:, pl.ds(pl.multiple_of(kt * KEY_TILE, KEY_TILE), KEY_TILE)], pb)

    o_s = _flash_stream(slc_scores, slc_pv, kt_self, 0, kt_self, stream_sc, m_sc, l_sc, acc_sc)

    def win_scores(kt, self_tile):
        k0 = pl.multiple_of(kt * KEY_TILE, KEY_TILE)
        d = qpos - (k0 + key_row)
        bias = jnp.where((d >= 0) & (d < WINDOW), 0.0, NEG_BIG)
        return jnp.dot(kw_ref[pl.ds(k0, KEY_TILE), :], qt, preferred_element_type=F32) + _tile_cols(bias, NSA_HEADS)

    def win_pv(kt, pb):
        return _pv_split(vwt_ref[:, pl.ds(pl.multiple_of(kt * KEY_TILE, KEY_TILE), KEY_TILE)], pb)

    win_lo = jnp.maximum(i - WINDOW // nq, 0) // (KEY_TILE // nq)
    o_w = _flash_stream(win_scores, win_pv, kt_self, win_lo, kt_self - win_lo, stream_sc, m_sc, l_sc, acc_sc)
    gates_t = gates_ref[...].T
    merged = []
    for hd in range(NSA_HEADS):
        cs = slice(hd * nq, (hd + 1) * nq)
        gate = lambda br: gates_t[3 * hd + br:3 * hd + br + 1]
        merged.append(gate(0) * o_c[:, cs] + gate(1) * o_s[:, cs] + gate(2) * o_w[:, cs])
    for j in range(NSA_HEADS // 2):
        pair = jnp.concatenate([merged[2 * j], merged[2 * j + 1]], axis=0).T
        o_ref[:, j * LANES:(j + 1) * LANES] = pair.astype(BF16)


def nsa_prompt_attn(q, gates, ckv, ks, vst, kw, vwt):
    nqb = SEQ // Q_BLOCK
    tok = lambda wd: pl.BlockSpec((Q_BLOCK, wd), lambda b, i: (b * nqb + i, 0))
    seq_rows = pl.BlockSpec((SEQ, KV_HALF), lambda b, i: (b, 0))
    seq_cols = pl.BlockSpec((KV_HALF, SEQ), lambda b, i: (0, b))
    n_cmp = N_CMP_PROMPT
    c_blk = np.concatenate([np.arange(0, n_cmp, 2), np.arange(1, n_cmp, 2)])
    cols = NSA_HEADS * Q_BLOCK
    return pl.pallas_call(
        _nsa_prompt_body,
        grid=(BATCH, nqb),
        in_specs=[tok(NSA_Q_W), tok(LANES), pl.BlockSpec((1, n_cmp, NSA_KV_W), lambda b, i: (b, 0, 0)),
                  _full((n_cmp, LANES)), _full((n_cmp, LANES)), _full((n_cmp, LANES)),
                  seq_rows, seq_cols, seq_rows, seq_cols],
        out_specs=tok(NSA_Q_W),
        out_shape=jax.ShapeDtypeStruct((N_PROMPT, NSA_Q_W), BF16),
        scratch_shapes=[pltpu.VMEM((1, cols), F32), pltpu.VMEM((1, cols), F32), pltpu.VMEM((HD, cols), F32),
                        pltpu.VMEM((N_SLC_PROMPT, NSA_KV * Q_BLOCK), F32),
                        pltpu.VMEM((KEY_TILE, cols), F32), pltpu.VMEM((1, cols), F32),
                        pltpu.VMEM((KEY_TILE, cols), F32), pltpu.VMEM((1, cols), F32)],
        compiler_params=_params(2),
        name="nsa_prompt",
    )(q, gates, ckv, *_rope_tables((c_blk + 1) * CMP_BLOCK - 1), ks, vst, kw, vwt)


def _nsa_decode_body(pt_ref, qt_ref, gate_ref, ckv_ref, cos_ref, slo_ref, shi_ref, hsum_ref, *refs):
    page_refs = refs[:N_PAGES]
    win_ref, kvs_new_ref, kvw_new_ref, o_ref, win_out_ref, m_sc, l_sc, acc_sc = refs[N_PAGES:]
    qt = qt_ref[0]
    lane = lax.broadcasted_iota(jnp.int32, (1, LANES), 1)
    qpos = PAST_LEN + (lane & (DEC_SEQ - 1))
    group0 = lane < DEC_COLS_PER_GROUP
    ckv = ckv_ref[0]
    ck = _rope_lanes(ckv[:, :KV_HALF], cos_ref[...], slo_ref[...], shi_ref[...]).astype(BF16)
    c_mask = _even_first_cmp_end(N_CMP_DEC) <= qpos
    p = _softmax_cols(jnp.dot(ck, qt, preferred_element_type=F32), c_mask)
    o_c = _tn_dot(ckv[:, KV_HALF:].astype(BF16), p.astype(BF16))
    p_grp = jnp.dot(p, hsum_ref[...], preferred_element_type=F32, precision=lax.Precision.HIGHEST)
    p_slc = jnp.concatenate([p_grp[:N_CMP_DEC // 2] + p_grp[N_CMP_DEC // 2:],
                             jnp.zeros((N_SLC_DEC_PAD - N_CMP_DEC // 2, LANES), F32)], axis=0)
    sel = _select_cols(p_slc, qpos, SLC_TOPK)

    def new_rows(ref):
        row = ref[0]
        kv = jnp.concatenate([row[:, t * NSA_KV_W:(t + 1) * NSA_KV_W] for t in range(DEC_SEQ)], axis=0)
        return jnp.concatenate([kv, jnp.zeros((SUBLANES - DEC_SEQ, NSA_KV_W), F32)], axis=0)

    new_row = lax.broadcasted_iota(jnp.int32, (SUBLANES, 1), 0)
    new_pos = PAST_LEN + new_row
    new_valid = new_row < DEC_SEQ
    _flash_init(m_sc, l_sc, acc_sc)
    kt_old = jnp.concatenate([r[0, :KV_HALF, :] for r in page_refs], axis=1).astype(BF16)
    vt_old = jnp.concatenate([r[0, KV_HALF:, :] for r in page_refs], axis=1).astype(BF16)
    key_blk = lax.broadcasted_iota(jnp.int32, (PAST_LEN, N_SLC_DEC_PAD), 0) >> SLC_SHIFT
    blk_col = lax.broadcasted_iota(jnp.int32, (PAST_LEN, N_SLC_DEC_PAD), 1)
    chosen = jnp.dot(jnp.where(key_blk == blk_col, 1.0, 0.0).astype(BF16), sel.astype(BF16),
                     preferred_element_type=F32) > 0.5
    old_pos = lax.broadcasted_iota(jnp.int32, (PAST_LEN, 1), 0)
    _flash_cols(_tn_dot(kt_old, qt), chosen & (old_pos <= qpos),
                lambda pb: jnp.dot(vt_old, pb, preferred_element_type=F32), m_sc, l_sc, acc_sc)
    kv_new = new_rows(kvs_new_ref)
    sel_new = sel[(PAST_LEN >> SLC_SHIFT):(PAST_LEN >> SLC_SHIFT) + 1] > 0.5
    v_new = kv_new[:, KV_HALF:].astype(BF16)
    _flash_cols(jnp.dot(kv_new[:, :KV_HALF].astype(BF16), qt, preferred_element_type=F32),
                sel_new & new_valid & (new_pos <= qpos), lambda pb: _tn_dot(v_new, pb), m_sc, l_sc, acc_sc)
    o_s = _flash_out(l_sc, acc_sc)
    _flash_init(m_sc, l_sc, acc_sc)
    n_buf = win_ref.shape[2]
    win = win_ref[0]
    d = qpos - (PAST_LEN - n_buf + lax.broadcasted_iota(jnp.int32, (n_buf, 1), 0))
    vt_win = win[KV_HALF:].astype(BF16)
    _flash_cols(_tn_dot(win[:KV_HALF].astype(BF16), qt), (d >= 0) & (d < WINDOW),
                lambda pb: jnp.dot(vt_win, pb, preferred_element_type=F32), m_sc, l_sc, acc_sc)
    kw_new = new_rows(kvw_new_ref)
    d = qpos - new_pos
    vw_new = kw_new[:, KV_HALF:].astype(BF16)
    _flash_cols(jnp.dot(kw_new[:, :KV_HALF].astype(BF16), qt, preferred_element_type=F32),
                new_valid & (d >= 0) & (d < WINDOW), lambda pb: _tn_dot(vw_new, pb), m_sc, l_sc, acc_sc)
    o_w = _flash_out(l_sc, acc_sc)
    g = gate_ref[0]
    o = g[0:1] * o_c + g[1:2] * o_s + g[2:3] * o_w
    o_ref[0] = jnp.where(group0, o[:HD], o[HD:])
    key = lax.broadcasted_iota(jnp.int32, (SUBLANES, n_buf), 1)
    place = jnp.where((key == n_buf - DEC_SEQ + new_row) & new_valid, 1.0, 0.0)
    placed = lax.dot_general(kw_new, place, (((0,), (0,)), ((), ())), preferred_element_type=F32,
                             precision=lax.Precision.HIGHEST)
    keep = lax.broadcasted_iota(jnp.int32, (1, n_buf), 1) < n_buf - DEC_SEQ
    win_out_ref[0] = jnp.where(keep, pltpu.roll(win, n_buf - DEC_SEQ, 1), placed)


def nsa_decode_attn(qt, gate_rows, ckv, slc_pool, win_buf, kvs_new, kvw_new, page_table):
    n_buf = win_buf.shape[2]
    per_b = lambda *shape: pl.BlockSpec((1,) + shape, lambda b, pt: (b,) + (0,) * len(shape))
    const = lambda *shape: pl.BlockSpec(shape, lambda b, pt: (0,) * len(shape))
    page = lambda j: pl.BlockSpec((1, NSA_KV_W, PAGE_SIZE), lambda b, pt: (pt[b, j], 0, 0))
    c_blk = np.concatenate([np.arange(0, N_CMP_DEC, 2), np.arange(1, N_CMP_DEC, 2)])
    col = jnp.arange(LANES)
    used = col < NSA_KV * DEC_COLS_PER_GROUP
    same = (col[:, None] // DEC_COLS_PER_GROUP == col[None, :] // DEC_COLS_PER_GROUP) & \
           (col[:, None] % DEC_SEQ == col[None, :] % DEC_SEQ) & used[:, None] & used[None, :]
    return pl.pallas_call(
        _nsa_decode_body,
        grid_spec=pltpu.PrefetchScalarGridSpec(
            num_scalar_prefetch=1,
            grid=(DEC_BATCH,),
            in_specs=[per_b(LANES, LANES), per_b(3, LANES), per_b(N_CMP_DEC, NSA_KV_W),
                      const(N_CMP_DEC, LANES), const(N_CMP_DEC, LANES), const(N_CMP_DEC, LANES),
                      const(LANES, LANES)] + [page(j) for j in range(N_PAGES)]
                     + [per_b(NSA_KV_W, n_buf), per_b(1, DEC_SEQ * NSA_KV_W), per_b(1, DEC_SEQ * NSA_KV_W)],
            out_specs=[per_b(HD, LANES), per_b(NSA_KV_W, n_buf)],
            scratch_shapes=[pltpu.VMEM((1, LANES), F32), pltpu.VMEM((1, LANES), F32),
                            pltpu.VMEM((LANES, LANES), F32)],
        ),
        out_shape=[jax.ShapeDtypeStruct((DEC_BATCH, HD, LANES), F32),
                   jax.ShapeDtypeStruct((DEC_BATCH, NSA_KV_W, n_buf), F32)],
        compiler_params=_params(1),
        name="nsa_decode",
    )(page_table, qt, gate_rows, ckv, *_rope_tables((c_blk + 1) * CMP_BLOCK - 1), same.astype(F32),
      *([slc_pool] * N_PAGES), win_buf, kvs_new, kvw_new)


GLA_PAIRS = GLA_HEADS // 2
GLA_ROWS = 512
GLA_SUB = 16
GLA_STEP_SEQS = 8


def _gla_rows(q, k, la, v, sgg, gnorm, st_ref, sub):
    R = q.shape[0]
    row = lax.broadcasted_iota(jnp.int32, (R, GLA_K_W), 0)
    rin = row % sub
    cum = la
    d = 1
    while d < sub:
        cum = cum + jnp.where(rin >= d, pltpu.roll(cum, d, 0), 0.0)
        d *= 2
    lane = lax.broadcasted_iota(jnp.int32, (sub, LANES), 1)
    lo = lane < GLA_DK
    rsub = lax.broadcasted_iota(jnp.int32, (sub, LANES), 0)
    out_rows = []
    for c in range(R // sub):
        rs = slice(c * sub, (c + 1) * sub)
        cum_c = cum[rs]
        last = cum_c[sub - 1:sub]
        qe = q[rs] * jnp.exp(cum_c)
        kdec = k[rs] * jnp.exp(last - cum_c)
        v_c = v[rs]
        heads = []
        for pr in range(GLA_PAIRS):
            ls = slice(pr * LANES, (pr + 1) * LANES)
            st = st_ref[pr]
            st_b = st.astype(BF16)
            qe_p, kd_p, q_p, k_p, cum_p = qe[:, ls], kdec[:, ls], q[rs, ls], k[rs, ls], cum_c[:, ls]
            v_pair = [v_c[:, (2 * pr + hh) * GLA_DV:(2 * pr + hh + 1) * GLA_DV] for hh in range(2)]
            upd = jnp.zeros((GLA_DV, LANES), F32)
            o_pair = []
            for hh in range(2):
                keep = lo if hh == 0 else jnp.logical_not(lo)
                o_pair.append(_nt_dot(jnp.where(keep, qe_p, 0.0).astype(BF16), st_b))
                upd = upd + _tn_dot(v_pair[hh].astype(BF16), jnp.where(keep, kd_p, 0.0).astype(BF16))
            for j in range(sub):
                dj = jnp.where(rsub >= j, jnp.exp(cum_p - cum_p[j:j + 1]), 0.0)
                w = q_p * k_p[j:j + 1] * dj
                a_lo = jnp.sum(jnp.where(lo, w, 0.0), axis=-1, keepdims=True)
                a_hi = jnp.sum(jnp.where(lo, 0.0, w), axis=-1, keepdims=True)
                o_pair[0] = o_pair[0] + a_lo * v_pair[0][j:j + 1]
                o_pair[1] = o_pair[1] + a_hi * v_pair[1][j:j + 1]
            st_ref[pr] = st * jnp.exp(last[:, ls]) + upd
            heads += o_pair
        out_rows.append(jnp.concatenate([_rms(x, gnorm) for x in heads], axis=1))
    return jnp.concatenate(out_rows, axis=0) * sgg


def _gla_seq_body(q_ref, k_ref, la_ref, v_ref, sgg_ref, gn_ref, o_ref, st_out_ref, st_sc):
    @pl.when(pl.program_id(1) == 0)
    def _():
        st_sc[...] = jnp.zeros(st_sc.shape, F32)

    o = _gla_rows(q_ref[...], k_ref[...], la_ref[...], v_ref[...].astype(F32), sgg_ref[...], gn_ref[...],
                  st_sc, GLA_SUB)
    o_ref[...] = o.astype(BF16)
    st_out_ref[0] = st_sc[...]


def gla_seq(q, k, la, v, sgg, gnorm):
    nt = SEQ // GLA_ROWS
    rows = lambda wd: pl.BlockSpec((GLA_ROWS, wd), lambda b, t: (b * nt + t, 0))
    return pl.pallas_call(
        _gla_seq_body,
        grid=(BATCH, nt),
        in_specs=[rows(GLA_K_W), rows(GLA_K_W), rows(GLA_K_W), rows(GLA_V_W), rows(GLA_V_W),
                  _full((1, GLA_DV))],
        out_specs=[rows(GLA_V_W), pl.BlockSpec((1, GLA_PAIRS, GLA_DV, LANES), lambda b, t: (b, 0, 0, 0))],
        out_shape=[jax.ShapeDtypeStruct((N_PROMPT, GLA_V_W), BF16),
                   jax.ShapeDtypeStruct((BATCH, GLA_PAIRS, GLA_DV, LANES), F32)],
        scratch_shapes=[pltpu.VMEM((GLA_PAIRS, GLA_DV, LANES), F32)],
        compiler_params=_params(2),
        name="gla_seq",
    )(q, k, la, v, sgg, gnorm.reshape(1, GLA_DV))


def _gla_step_body(q_ref, k_ref, la_ref, v_ref, sgg_ref, gn_ref, st_in_ref, o_ref, st_out_ref):
    st_out_ref[...] = st_in_ref[...]
    q, k, la, v, sgg = q_ref[...], k_ref[...], la_ref[...], v_ref[...].astype(F32), sgg_ref[...]
    for j in range(GLA_STEP_SEQS):
        rs = slice(j * DEC_SEQ, (j + 1) * DEC_SEQ)
        o = _gla_rows(q[rs], k[rs], la[rs], v[rs], sgg[rs], gn_ref[...], st_out_ref.at[j], DEC_SEQ)
        o_ref[rs, :] = o.astype(BF16)


def gla_step(q, k, la, v, sgg, gnorm, st_in, row0):
    rows_per = GLA_STEP_SEQS * DEC_SEQ
    blk0 = row0 // rows_per
    rows = lambda wd: pl.BlockSpec((rows_per, wd), lambda i: (blk0 + i, 0))
    st_spec = pl.BlockSpec((GLA_STEP_SEQS, GLA_PAIRS, GLA_DV, LANES), lambda i: (i, 0, 0, 0))
    return pl.pallas_call(
        _gla_step_body,
        grid=(DEC_BATCH // GLA_STEP_SEQS,),
        in_specs=[rows(GLA_K_W), rows(GLA_K_W), rows(GLA_K_W), rows(GLA_V_W), rows(GLA_V_W),
                  _full((1, GLA_DV)), st_spec],
        out_specs=[pl.BlockSpec((rows_per, GLA_V_W), lambda i: (i, 0)), st_spec],
        out_shape=[jax.ShapeDtypeStruct((N_SAMPLE, GLA_V_W), BF16),
                   jax.ShapeDtypeStruct((DEC_BATCH, GLA_PAIRS, GLA_DV, LANES), F32)],
        compiler_params=_params(1),
        name="gla_step",
    )(q, k, la, v, sgg, gnorm.reshape(1, GLA_DV), st_in)


def _gla_state_to_pairs(s):
    B = s.shape[0]
    return s.reshape(B, GLA_PAIRS, 2, GLA_DK, GLA_DV).transpose(0, 1, 4, 2, 3).reshape(B, GLA_PAIRS, GLA_DV, LANES)


def _gla_state_from_pairs(st):
    B = st.shape[0]
    return st.reshape(B, GLA_PAIRS, GLA_DV, 2, GLA_DK).transpose(0, 1, 3, 4, 2).reshape(B, GLA_HEADS, GLA_DK, GLA_DV)


LRU_TIME_TILE = 512


def _lru_gates(xc, wa_ref, ba, wx_ref, bx, lam):
    xcb = xc.astype(BF16)
    r_parts, i_parts = [], []
    for n in range(LRU_BLOCKS):
        xs = xcb[:, n * LRU_BW:(n + 1) * LRU_BW]
        r_parts.append(jnp.dot(xs, wa_ref[n], preferred_element_type=F32))
        i_parts.append(jnp.dot(xs, wx_ref[n], preferred_element_type=F32))
    r = jax.nn.sigmoid(jnp.concatenate(r_parts, axis=-1) + ba)
    i = jax.nn.sigmoid(jnp.concatenate(i_parts, axis=-1) + bx)
    log_a = -LRU_C * r * _softplus(-lam)
    a = jnp.exp(log_a)
    u = jnp.sqrt(1.0 - a * a) * (i * xc)
    return a, u


def _lru_seq_body(u_ref, cw_ref, cb_ref, wa_ref, ba_ref, wx_ref, bx_ref, lam_ref, y_ref, hT_ref, xp_sc, h_sc):
    tt = LRU_TIME_TILE

    @pl.when(pl.program_id(1) == 0)
    def _():
        xp_sc[...] = jnp.zeros((SUBLANES, D_RNN), F32)
        h_sc[...] = jnp.zeros((1, D_RNN), F32)

    x = u_ref[:, D_RNN:]
    head = jnp.concatenate([xp_sc[...], x[:SUBLANES]], axis=0)
    xc = cb_ref[...] + cw_ref[CONV_W - 1:CONV_W, :] * x
    for k in range(1, CONV_W):
        shifted = jnp.concatenate([head[SUBLANES - k:2 * SUBLANES - k], pltpu.roll(x, k, 0)[SUBLANES:]], axis=0)
        xc = xc + cw_ref[CONV_W - 1 - k:CONV_W - k, :] * shifted
    xp_sc[...] = x[tt - SUBLANES:]
    a, u = _lru_gates(xc, wa_ref, ba_ref[...], wx_ref, bx_ref[...], lam_ref[...])
    row = lax.broadcasted_iota(jnp.int32, (tt, D_RNN), 0) % SUBLANES
    d = 1
    while d < SUBLANES:
        keep = row >= d
        a_prev = jnp.where(keep, pltpu.roll(a, d, 0), 1.0)
        u_prev = jnp.where(keep, pltpu.roll(u, d, 0), 0.0)
        u = a * u_prev + u
        a = a * a_prev
        d *= 2
    carry = h_sc[...]
    hs = []
    for grp in range(tt // SUBLANES):
        rs = slice(grp * SUBLANES, (grp + 1) * SUBLANES)
        hs.append(a[rs] * carry + u[rs])
        carry = hs[-1][SUBLANES - 1:SUBLANES, :]
    h_sc[...] = carry
    hT_ref[0] = carry
    y_ref[...] = (_gelu_tanh(u_ref[:, :D_RNN]) * jnp.concatenate(hs, axis=0)).astype(BF16)


def _lru_weight_args(cw, cb, wa, ba, wx, bx, lam):
    row = lambda v: v.reshape(1, D_RNN)
    return (cw, row(cb), wa.astype(BF16), row(ba), wx.astype(BF16), row(bx), row(lam))


_LRU_WEIGHT_SPECS = [_full((CONV_W, D_RNN)), _full((1, D_RNN)), _full((LRU_BLOCKS, LRU_BW, LRU_BW)),
                     _full((1, D_RNN)), _full((LRU_BLOCKS, LRU_BW, LRU_BW)), _full((1, D_RNN)),
                     _full((1, D_RNN))]


def lru_seq(u, cw, cb, wa, ba, wx, bx, lam):
    tt = LRU_TIME_TILE
    nt = SEQ // tt
    y, hT = pl.pallas_call(
        _lru_seq_body,
        grid=(BATCH, nt),
        in_specs=[pl.BlockSpec((tt, 2 * D_RNN), lambda b, t: (b * nt + t, 0))] + _LRU_WEIGHT_SPECS,
        out_specs=[pl.BlockSpec((tt, D_RNN), lambda b, t: (b * nt + t, 0)),
                   pl.BlockSpec((1, 1, D_RNN), lambda b, t: (b, 0, 0))],
        out_shape=[jax.ShapeDtypeStruct((N_PROMPT, D_RNN), BF16), jax.ShapeDtypeStruct((BATCH, 1, D_RNN), F32)],
        scratch_shapes=[pltpu.VMEM((SUBLANES, D_RNN), F32), pltpu.VMEM((1, D_RNN), F32)],
        compiler_params=_params(2),
        name="lru_seq",
    )(u, *_lru_weight_args(cw, cb, wa, ba, wx, bx, lam))
    return y, hT.reshape(BATCH, D_RNN)


def _lru_step_body(u_ref, cs_ref, h0_ref, cw_ref, cb_ref, wa_ref, ba_ref, wx_ref, bx_ref, lam_ref, y_ref, hT_ref):
    n_t = u_ref.shape[0]
    hist = [cs_ref[:, w, :] for w in range(CONV_W - 1)] + [u_ref[t, :, D_RNN:] for t in range(n_t)]
    h = h0_ref[...]
    for t in range(n_t):
        xc = cb_ref[...]
        for w in range(CONV_W):
            xc = xc + cw_ref[w:w + 1, :] * hist[t + w]
        a, u = _lru_gates(xc, wa_ref, ba_ref[...], wx_ref, bx_ref[...], lam_ref[...])
        h = a * h + u
        y_ref[t] = (_gelu_tanh(u_ref[t, :, :D_RNN]) * h).astype(BF16)
    hT_ref[...] = h


def lru_step(u, conv_state, h0, cw, cb, wa, ba, wx, bx, lam):
    T, B, _ = u.shape
    return pl.pallas_call(
        _lru_step_body,
        out_shape=[jax.ShapeDtypeStruct((T, B, D_RNN), BF16), jax.ShapeDtypeStruct((B, D_RNN), F32)],
        compiler_params=pltpu.CompilerParams(vmem_limit_bytes=VMEM_LIMIT_BYTES),
        name="lru_step",
    )(u, conv_state, h0, *_lru_weight_args(cw, cb, wa, ba, wx, bx, lam))


ROUTE_E1, ROUTE_E2, ROUTE_G1, ROUTE_G2, ROUTE_R1, ROUTE_R2 = range(6)
EXPERT_LANE0 = N_GROUPS


def _lane_pick(val_by_lane):
    rows = next(iter(val_by_lane.values())).shape[0]
    lane = lax.broadcasted_iota(jnp.int32, (rows, LANES), 1)
    out = jnp.zeros((rows, LANES), F32)
    for l, v in val_by_lane.items():
        out = jnp.where(lane == l, v, out)
    return out


def _route_rows(logits, tri_ref, carry_ref):
    rows = logits.shape[0]
    lane = lax.broadcasted_iota(jnp.int32, (rows, LANES), 1)
    neg = -jnp.inf
    gl = jnp.where(lane < N_GROUPS, logits, neg)
    gmax = jnp.max(gl, axis=-1, keepdims=True)
    gtop = jnp.min(jnp.where(gl == gmax, lane, LANES), axis=-1, keepdims=True)
    gsum = jnp.sum(jnp.where(lane < N_GROUPS, jnp.exp(logits - gmax), 0.0), axis=-1, keepdims=True)
    g_w = 1.0 / gsum
    lo = EXPERT_LANE0 + EXP_PER_GROUP * gtop
    el = jnp.where((lane >= lo) & (lane < lo + EXP_PER_GROUP), logits, neg)
    v1 = jnp.max(el, axis=-1, keepdims=True)
    i1 = jnp.min(jnp.where(el == v1, lane, LANES), axis=-1, keepdims=True)
    el2 = jnp.where(lane == i1, neg, el)
    v2 = jnp.max(el2, axis=-1, keepdims=True)
    i2 = jnp.min(jnp.where(el2 == v2, lane, LANES), axis=-1, keepdims=True)
    p2 = jnp.exp(v2 - v1)
    den = 1.0 + p2
    gate1 = (1.0 / den) * g_w
    gate2 = (p2 / den) * g_w
    hit1 = lane == i1
    hit2 = lane == i2
    onehot = jnp.where(hit1 | hit2, 1.0, 0.0)
    before = jnp.dot(tri_ref[...], onehot.astype(BF16), preferred_element_type=F32) + carry_ref[...]
    rank1 = jnp.sum(jnp.where(hit1, before, 0.0), axis=-1, keepdims=True)
    rank2 = jnp.sum(jnp.where(hit2, before, 0.0), axis=-1, keepdims=True)
    carry_ref[...] = carry_ref[...] + jnp.sum(onehot, axis=0, keepdims=True)
    return _lane_pick({ROUTE_E1: (i1 - EXPERT_LANE0).astype(F32), ROUTE_E2: (i2 - EXPERT_LANE0).astype(F32),
                       ROUTE_G1: gate1, ROUTE_G2: gate2, ROUTE_R1: rank1, ROUTE_R2: rank2})


def _pack_bf16_halves(x):
    w = x.shape[1] // 2
    bits = lambda v: pltpu.bitcast(v.astype(F32), jnp.uint32)
    return pltpu.bitcast((bits(x[:, :w]) >> 16) | bits(x[:, w:]), F32)


def _unpack_bf16_halves(words):
    p = pltpu.bitcast(words, jnp.uint32)
    lo = pltpu.bitcast(p << 16, F32).astype(BF16)
    hi = pltpu.bitcast(p & jnp.uint32(0xFFFF0000), F32).astype(BF16)
    return jnp.concatenate([lo, hi], axis=1)


def _post_mixer_body(n_h, n_mix, *refs):
    h_refs, refs = refs[:n_h], refs[n_h:]
    m_refs, wo_refs = refs[:2 * n_mix], refs[2 * n_mix:3 * n_mix]
    g_ref, wr_ref, br_ref, tri_ref, h1_ref, hn_ref, route_ref, cnt_ref, carry_sc = refs[3 * n_mix:]

    @pl.when(pl.program_id(0) == 0)
    def _():
        carry_sc[...] = jnp.zeros((1, LANES), F32)

    mix = None
    for j, wo_ref in enumerate(wo_refs):
        part = jnp.dot(_token_rows(m_refs[2 * j], m_refs[2 * j + 1]), wo_ref[...], preferred_element_type=F32)
        mix = part if mix is None else mix + part
    h1 = (h_refs[0][...] if n_h == 1 else _token_rows(*h_refs)) + mix
    h1_ref[...] = h1
    hn = _rms(h1, g_ref[...]).astype(BF16)
    hn_ref[...] = _pack_bf16_halves(hn)
    logits = jnp.dot(hn, wr_ref[...], preferred_element_type=F32) + br_ref[...]
    route_ref[...] = _route_rows(logits, tri_ref, carry_sc)
    cnt_ref[...] = carry_sc[...]


def post_mixer(h, mix_ins, w_out, g_ffn, w_rg, b_rg, w_re, b_re):
    hs = list(h) if isinstance(h, (tuple, list)) else [h]
    h_specs = [_PROMPT_ROWS, _SAMPLE_ROWS] if len(hs) == 2 else [_rows(D_MODEL)]
    n = N_TOK
    ks = [m[0].shape[1] for m in mix_ins]
    two_part = lambda k: [pl.BlockSpec((ROW_TILE, k), lambda i: (jnp.minimum(i, N_PROMPT_TILES - 1), 0)),
                          pl.BlockSpec((ROW_TILE, k), lambda i: (0, 0))]
    offs = np.cumsum([0] + ks)
    w_parts = [w_out[offs[j]:offs[j + 1]].astype(BF16) for j in range(len(ks))]
    pad = LANES - N_GROUPS - N_EXPERTS
    wr = jnp.concatenate([w_rg, w_re, jnp.zeros((D_MODEL, pad), F32)], axis=1).astype(BF16)
    br = jnp.concatenate([b_rg, b_re, jnp.zeros((pad,), F32)]).reshape(1, LANES)
    tri = jnp.tril(jnp.ones((ROW_TILE, ROW_TILE), BF16), -1)
    return pl.pallas_call(
        functools.partial(_post_mixer_body, len(hs), len(ks)),
        grid=(n // ROW_TILE,),
        in_specs=h_specs + [s for k in ks for s in two_part(k)] + [_full((k, D_MODEL)) for k in ks]
                 + [_full((1, D_MODEL)), _full((D_MODEL, LANES)), _full((1, LANES)), _full((ROW_TILE, ROW_TILE))],
        out_specs=[_rows(D_MODEL), _rows(D_MODEL // 2), _rows(LANES), _full((1, LANES))],
        out_shape=[jax.ShapeDtypeStruct((n, D_MODEL), F32), jax.ShapeDtypeStruct((n, D_MODEL // 2), F32),
                   jax.ShapeDtypeStruct((n, LANES), F32), jax.ShapeDtypeStruct((1, LANES), F32)],
        scratch_shapes=[pltpu.VMEM((1, LANES), F32)],
        compiler_params=_params(1),
        name="post_mixer",
    )(*hs, *[part for m in mix_ins for part in m], *w_parts, g_ffn.reshape(1, D_MODEL), wr, br, tri)


MOE_ROWS = 512


def _ffn_body(be_ref, nb_ref, x_ref, w1_ref, w3_ref, w2_ref, y_ref, w1_sc, w3_sc, w2_sc):
    i = pl.program_id(0)
    new_expert = jnp.logical_or(i == 0, be_ref[i] != be_ref[jnp.maximum(i - 1, 0)])

    @pl.when(jnp.logical_and(new_expert, i < nb_ref[0]))
    def _():
        w1_sc[...] = w1_ref[0, 0].astype(BF16)
        w3_sc[...] = w3_ref[0, 0].astype(BF16)
        w2_sc[...] = w2_ref[0, 0].astype(BF16)

    @pl.when(i < nb_ref[0])
    def _():
        x = _unpack_bf16_halves(x_ref[...])
        a = jnp.dot(x, w1_sc[...], preferred_element_type=F32)
        b = jnp.dot(x, w3_sc[...], preferred_element_type=F32)
        hdn = (a * jax.nn.sigmoid(a) * b).astype(BF16)
        y_ref[...] = jnp.dot(hdn, w2_sc[...], preferred_element_type=F32)

    @pl.when(i >= nb_ref[0])
    def _():
        y_ref[...] = jnp.zeros(y_ref.shape, F32)


def expert_ffn(xs, blk_exp, n_active, w1, w3, w2, layer):
    n_slots = xs.shape[0]
    nb = n_slots // MOE_ROWS
    wmap = lambda i, be, na: (layer, be[i], 0, 0)
    xmap = lambda i, be, na: (jnp.minimum(i, na[0] - 1), 0)
    return pl.pallas_call(
        _ffn_body,
        grid_spec=pltpu.PrefetchScalarGridSpec(
            num_scalar_prefetch=2,
            grid=(nb,),
            in_specs=[pl.BlockSpec((MOE_ROWS, D_MODEL // 2), xmap),
                      pl.BlockSpec((1, 1, D_MODEL, E_HID), wmap),
                      pl.BlockSpec((1, 1, D_MODEL, E_HID), wmap),
                      pl.BlockSpec((1, 1, E_HID, D_MODEL), wmap)],
            out_specs=pl.BlockSpec((MOE_ROWS, D_MODEL), lambda i, be, na: (i, 0)),
            scratch_shapes=[pltpu.VMEM((D_MODEL, E_HID), BF16), pltpu.VMEM((D_MODEL, E_HID), BF16),
                            pltpu.VMEM((E_HID, D_MODEL), BF16)],
        ),
        out_shape=jax.ShapeDtypeStruct((n_slots, D_MODEL), F32),
        compiler_params=_params(1),
        name="expert_ffn",
    )(blk_exp, n_active, xs, w1, w3, w2)


def moe_dispatch(route, counts_row, n):
    counts = counts_row[0, EXPERT_LANE0:EXPERT_LANE0 + N_EXPERTS].astype(jnp.int32)
    padded = ((counts + MOE_ROWS - 1) // MOE_ROWS) * MOE_ROWS
    pad_end = jnp.cumsum(padded)
    pad_start = (pad_end - padded).astype(F32)
    expert_lane = lax.broadcasted_iota(jnp.int32, (1, N_EXPERTS), 1).astype(F32)
    dest = []
    for e_lane, r_lane in ((ROUTE_E1, ROUTE_R1), (ROUTE_E2, ROUTE_R2)):
        start = jnp.sum(jnp.where(route[:, e_lane:e_lane + 1] == expert_lane, pad_start[None, :], 0.0), axis=1)
        dest.append((start + route[:, r_lane]).astype(jnp.int32))
    nb = -(-(n * TOP_K_IN_GROUP) // MOE_ROWS) + N_EXPERTS
    n_slots = nb * MOE_ROWS
    tok = jnp.arange(n, dtype=jnp.int32)
    slot_tok = jnp.zeros((n_slots,), jnp.int32).at[jnp.concatenate(dest)].set(
        jnp.concatenate([tok] * TOP_K_IN_GROUP), unique_indices=True, mode='promise_in_bounds')
    blk_start = jnp.arange(nb, dtype=jnp.int32) * MOE_ROWS
    blk_exp = jnp.sum((pad_end[None, :] <= blk_start[:, None]).astype(jnp.int32), axis=1)
    blk_exp = jnp.minimum(blk_exp, N_EXPERTS - 1)
    n_active = (pad_end[-1] // MOE_ROWS).astype(jnp.int32).reshape(1)
    return slot_tok, dest, blk_exp, n_active


def moe_experts(hn, route, counts_row, w1, w3, w2, layer):
    n = hn.shape[0]
    slot_tok, dest, blk_exp, n_active = moe_dispatch(route, counts_row, n)
    xs = hn.at[slot_tok].get(mode='promise_in_bounds')
    ys = expert_ffn(xs, blk_exp, n_active, w1, w3, w2, layer)
    return [ys.at[d].get(mode='promise_in_bounds') for d in dest]


def _combine(h_ref, y1_ref, y2_ref, route_ref):
    lane = lax.broadcasted_iota(jnp.int32, (ROW_TILE, LANES), 1)
    r = route_ref[...]
    g1 = jnp.sum(jnp.where(lane == ROUTE_G1, r, 0.0), axis=-1, keepdims=True)
    g2 = jnp.sum(jnp.where(lane == ROUTE_G2, r, 0.0), axis=-1, keepdims=True)
    return h_ref[...] + (y1_ref[...] * g1 + y2_ref[...] * g2)


def _combine_proj_body(h_ref, y1_ref, y2_ref, route_ref, g_ref, w_ref, h2_ref, u_ref):
    h2 = _combine(h_ref, y1_ref, y2_ref, route_ref)
    h2_ref[...] = h2
    u_ref[...] = jnp.dot(_rms(h2, g_ref[...]).astype(BF16), w_ref[...], preferred_element_type=F32)


def _combine_norm_body(h_ref, y1_ref, y2_ref, route_ref, g_ref, yp_ref, ys_ref):
    y = _rms(_combine(h_ref, y1_ref, y2_ref, route_ref), g_ref[...])

    @pl.when(pl.program_id(0) < N_PROMPT_TILES)
    def _():
        yp_ref[...] = y

    @pl.when(pl.program_id(0) >= N_PROMPT_TILES)
    def _():
        ys_ref[...] = y


def combine_proj(h, ys, route, g, w):
    n = h.shape[0]
    nn = w.shape[1]
    return pl.pallas_call(
        _combine_proj_body,
        grid=(n // ROW_TILE,),
        in_specs=[_rows(D_MODEL), _rows(D_MODEL), _rows(D_MODEL), _rows(LANES), _full((1, D_MODEL)),
                  _full((D_MODEL, nn))],
        out_specs=[_rows(D_MODEL), _rows(nn)],
        out_shape=[jax.ShapeDtypeStruct((n, D_MODEL), F32), jax.ShapeDtypeStruct((n, nn), F32)],
        compiler_params=_params(1),
        name="combine_proj",
    )(h, *ys, route, g.reshape(1, D_MODEL), w.astype(BF16))


def combine_norm(h, ys, route, g):
    n = h.shape[0]
    return pl.pallas_call(
        _combine_norm_body,
        grid=(n // ROW_TILE,),
        in_specs=[_rows(D_MODEL), _rows(D_MODEL), _rows(D_MODEL), _rows(LANES), _full((1, D_MODEL))],
        out_specs=[_PROMPT_ROWS, _SAMPLE_ROWS],
        out_shape=[jax.ShapeDtypeStruct((N_PROMPT, D_MODEL), F32), jax.ShapeDtypeStruct((N_SAMPLE, D_MODEL), F32)],
        compiler_params=_params(1),
        name="combine_norm",
    )(h, *ys, route, g.reshape(1, D_MODEL))


def _even_first(x, axis):
    n = x.shape[axis]
    idx = jnp.concatenate([jnp.arange(0, n, 2), jnp.arange(1, n, 2)])
    return jnp.take(x, idx, axis=axis)


def _decode_query_cols(q_s, gates_s):
    B, T, G, R = DEC_BATCH, DEC_SEQ, NSA_KV, NSA_REP
    qg = q_s.reshape(B, T, G, R, HD).transpose(0, 2, 4, 3, 1).reshape(B, G, HD, R * T)
    qt = jnp.zeros((B, G, HD, G, R * T), BF16)
    for g in range(G):
        qt = qt.at[:, g, :, g, :].set(qg[:, g])
    qt = jnp.pad(qt.reshape(B, G * HD, G * R * T), ((0, 0), (0, 0), (0, LANES - G * R * T)))
    gr = gates_s[:, :3 * NSA_HEADS].reshape(B, T, NSA_HEADS, 3).transpose(0, 3, 2, 1).reshape(B, 3, NSA_HEADS * T)
    return qt, jnp.pad(gr, ((0, 0), (0, 0), (0, LANES - NSA_HEADS * T)))


def _feature_major(cache):
    lead, rows = cache.shape[:2]
    return jnp.transpose(cache, (0, 2, 3, 4, 1)).reshape(lead, NSA_KV_W, rows)


def _kv_rows_from_feature_major(xt, lead):
    x = xt.reshape(2, NSA_KV, HD, *lead)
    n = len(lead)
    return jnp.transpose(x, tuple(range(3, 3 + n)) + (0, 1, 2))


def mixer_a(h, p, past):
    pos = np.concatenate([np.arange(SEQ), PAST_LEN + np.tile(np.arange(DEC_SEQ), DEC_BATCH)])
    (q, kvc, kvs, kvw, kvct_p, kvct_s, kvst_p, kvst_s, kvwt_p, kvwt_s, ks, vst, kw, vwt, gates, gq, gk, gv,
     la, sgg) = inproj_a(
        *h, p['norm_mix'][0], p['a_w_in'][0], p['a_gla_wa2'][0], p['a_gla_ba'][0], pos)
    cmp_w, gnorm = _cmp_weights(p['a_cmp_pe'][0], p['a_cmp_w'][0]), p['a_gla_norm'][0]
    P = N_PROMPT
    ckv = cmp_blocks_rows(kvc, P, cmp_w)
    ckv = _even_first(ckv.reshape(BATCH, N_CMP_PROMPT, NSA_KV_W), 1)
    o_nsa_p = nsa_prompt_attn(q, gates, ckv, ks, vst, kw, vwt)
    o_gla_p, st_p = gla_seq(gq, gk, la, gv, sgg, gnorm)
    rows_p = lambda xt: jnp.transpose(xt.reshape(BATCH, 2, NSA_KV, HD, xt.shape[2]), (0, 4, 1, 2, 3))
    new_p = (rows_p(kvct_p), rows_p(kvst_p), rows_p(kvwt_p[:, :, SEQ - WINDOW:]), _gla_state_from_pairs(st_p))
    o_gla_s, st_s = gla_step(gq, gk, la, gv, sgg, gnorm, _gla_state_to_pairs(past['state_gla'][0]), P)
    n_pool = past['cache_cmp_kv'].shape[1]
    ckv_pool = cmp_blocks_pages(_feature_major(past['cache_cmp_kv'][0]), cmp_w)
    ckv_seq = ckv_pool.reshape(n_pool, PAGE_SIZE // CMP_BLOCK, NSA_KV_W)[past['page_table']]
    ckv_seq = _even_first(ckv_seq.reshape(DEC_BATCH, N_CMP_DEC, NSA_KV_W), 1)
    qt, gate_rows = _decode_query_cols(q[P:], gates[P:])
    per_seq = lambda x: x[P:].reshape(DEC_BATCH, 1, DEC_SEQ * NSA_KV_W)
    o_t, win_new = nsa_decode_attn(qt, gate_rows, ckv_seq, _feature_major(past['cache_slc_kv'][0]),
                                   _feature_major(past['cache_win_kv'][0]), per_seq(kvs), per_seq(kvw),
                                   past['page_table'])
    o_nsa_s = o_t[:, :, :NSA_HEADS * DEC_SEQ].reshape(DEC_BATCH, HD, NSA_HEADS, DEC_SEQ)
    o_nsa_s = o_nsa_s.transpose(0, 3, 2, 1).reshape(N_SAMPLE, NSA_Q_W).astype(BF16)
    n_buf = win_new.shape[2]
    win_s = jnp.transpose(win_new.reshape(DEC_BATCH, 2, NSA_KV, HD, n_buf), (0, 4, 1, 2, 3))
    new_s = (_kv_rows_from_feature_major(kvct_s[:, :N_SAMPLE], (DEC_BATCH, DEC_SEQ)),
             _kv_rows_from_feature_major(kvst_s[:, :N_SAMPLE], (DEC_BATCH, DEC_SEQ)), win_s,
             _gla_state_from_pairs(st_s))
    return (o_nsa_p, o_nsa_s), (o_gla_p, o_gla_s), new_p, new_s


def run_trunk(x_prompt, x_sample, p, past):
    h0 = (x_prompt.reshape(N_PROMPT, D_MODEL), x_sample.reshape(N_SAMPLE, D_MODEL))
    o_nsa, o_gla, new_p, new_s = mixer_a(h0, p, past)
    h, hn, route, counts = post_mixer(h0, [o_nsa, o_gla], p['a_w_out'][0], p['norm_ffn'][0], p['m_w_rg'][0],
                                      p['m_b_rg'][0], p['m_w_re'][0], p['m_b_re'][0])
    ys = moe_experts(hn, route, counts, p['m_w1'], p['m_w3'], p['m_w2'], 0)
    h, u = combine_proj(h, ys, route, p['norm_mix'][1], p['c_w_in'][0])
    lru_w = (p['c_conv_w'][0], p['c_conv_b'][0], p['c_w_a'][0], p['c_b_a'][0], p['c_w_x'][0], p['c_b_x'][0],
             p['c_lam'][0])
    us = u[N_PROMPT:].reshape(DEC_BATCH, DEC_SEQ, 2 * D_RNN)
    y_p, lru_p = lru_seq(u, *lru_w)
    y_s, lru_s = lru_step(jnp.swapaxes(us, 0, 1), past['state_conv'][0], past['state_lru'][0], *lru_w)
    conv_p = jnp.stack([u[(b + 1) * SEQ - (CONV_W - 1):(b + 1) * SEQ, D_RNN:] for b in range(BATCH)])
    conv_s = us[:, DEC_SEQ - (CONV_W - 1):, D_RNN:]
    mix_in = (y_p, jnp.swapaxes(y_s, 0, 1).reshape(N_SAMPLE, D_RNN))
    h, hn, route, counts = post_mixer(h, [mix_in], p['c_w_out'][0], p['norm_ffn'][1], p['m_w_rg'][1],
                                      p['m_b_rg'][1], p['m_w_re'][1], p['m_b_re'][1])
    ys = moe_experts(hn, route, counts, p['m_w1'], p['m_w3'], p['m_w2'], 1)
    y_p, y_s = combine_norm(h, ys, route, p['norm_final'])
    y_prompt = y_p.reshape(BATCH, SEQ, D_MODEL)
    y_sample = y_s.reshape(DEC_BATCH, DEC_SEQ, D_MODEL)
    return (y_prompt, y_sample), new_p + (lru_p, conv_p), new_s + (lru_s, conv_s)


def kernel(x_prompt, x_sample, cache_cmp_kv, cache_slc_kv, cache_win_kv, state_gla, state_lru, state_conv,
           page_table, norm_mix, norm_ffn, norm_final, a_w_in, a_cmp_pe, a_cmp_w, a_gla_wa2, a_gla_ba,
           a_gla_norm, a_w_out, c_w_in, c_conv_w, c_conv_b, c_w_a, c_b_a, c_w_x, c_b_x, c_lam, c_w_out,
           m_w_rg, m_b_rg, m_w_re, m_b_re, m_w1, m_w3, m_w2):
    p = {'norm_mix': norm_mix, 'norm_ffn': norm_ffn, 'norm_final': norm_final,
         'a_w_in': a_w_in, 'a_cmp_pe': a_cmp_pe, 'a_cmp_w': a_cmp_w, 'a_gla_wa2': a_gla_wa2,
         'a_gla_ba': a_gla_ba, 'a_gla_norm': a_gla_norm, 'a_w_out': a_w_out,
         'c_w_in': c_w_in, 'c_conv_w': c_conv_w, 'c_conv_b': c_conv_b, 'c_w_a': c_w_a, 'c_b_a': c_b_a,
         'c_w_x': c_w_x, 'c_b_x': c_b_x, 'c_lam': c_lam, 'c_w_out': c_w_out,
         'm_w_rg': m_w_rg, 'm_b_rg': m_b_rg, 'm_w_re': m_w_re, 'm_b_re': m_b_re,
         'm_w1': m_w1, 'm_w3': m_w3, 'm_w2': m_w2}
    past = {'cache_cmp_kv': cache_cmp_kv, 'cache_slc_kv': cache_slc_kv, 'cache_win_kv': cache_win_kv,
            'state_gla': state_gla, 'state_lru': state_lru, 'state_conv': state_conv,
            'page_table': page_table}
    (y_p, y_s), sp, ss = run_trunk(x_prompt, x_sample, p, past)
    outs = [y_p, y_s]
    for a, b in zip(sp, ss):
        outs += [a[None], b[None]]
    return tuple(outs)
```

```python
import functools
import jax, jax.numpy as jnp
from jax import lax
import numpy as np
from jax.experimental import pallas as pl
from jax.experimental.pallas import tpu as pltpu

D_MODEL = 1024
BATCH = 2
SEQ = 8192
DEC_BATCH = 128
DEC_SEQ = 4
PAST_LEN = 2048
PAGE_SIZE = 128
EPS = 1e-6
NSA_HEADS = 8
NSA_KV = 2
NSA_REP = NSA_HEADS // NSA_KV
HD = 64
CMP_BLOCK = 32
SLC_BLOCK = 64
SLC_TOPK = 16
WINDOW = 512
Q_BLOCK = 128
ROPE_DIM = HD // 4
ROPE_THETA = 500000.0
GLA_HEADS = 4
GLA_DK = 64
GLA_DV = 128
GLA_LOWRANK = 16
GLA_TAU = 16.0
D_RNN = 1280
LRU_BLOCKS = 10
LRU_BW = D_RNN // LRU_BLOCKS
CONV_W = 4
LRU_C = 8.0
N_GROUPS = 4
EXP_PER_GROUP = 8
N_EXPERTS = N_GROUPS * EXP_PER_GROUP
E_HID = 512
TOP_K_IN_GROUP = 2
NSA_Q_W = NSA_HEADS * HD
NSA_KV_W = 2 * NSA_KV * HD
GLA_K_W = GLA_HEADS * GLA_DK
GLA_V_W = GLA_HEADS * GLA_DV
A_SIZES = (NSA_Q_W, NSA_KV_W, NSA_KV_W, NSA_KV_W, 3 * NSA_HEADS, GLA_K_W, GLA_K_W, GLA_V_W, GLA_LOWRANK, GLA_V_W)
N_PROMPT = BATCH * SEQ
N_SAMPLE = DEC_BATCH * DEC_SEQ
N_TOK = N_PROMPT + N_SAMPLE
N_PAGES = PAST_LEN // PAGE_SIZE

F32 = jnp.float32
BF16 = jnp.bfloat16
VMEM_LIMIT_BYTES = 56 * 1024 * 1024
LANES = 128
SUBLANES = 8
ROW_TILE = 512


def _params(n_axes):
    return pltpu.CompilerParams(dimension_semantics=("arbitrary",) * n_axes, vmem_limit_bytes=VMEM_LIMIT_BYTES)


def _full(shape):
    return pl.BlockSpec(shape, lambda *_: (0,) * len(shape))


def _rows(width):
    return pl.BlockSpec((ROW_TILE, width), lambda i: (i, 0))


def _rms(x, g):
    return x * lax.rsqrt(jnp.mean(x * x, axis=-1, keepdims=True) + EPS) * g


def _softplus(x):
    return jnp.maximum(x, 0.0) + jnp.log1p(jnp.exp(-jnp.abs(x)))


def _gelu_tanh(x):
    return x * (0.5 * (1.0 + jnp.tanh(0.7978845608028654 * (x + 0.044715 * (x * x * x)))))


def _nt_dot(a, b):
    return lax.dot_general(a, b, (((1,), (1,)), ((), ())), preferred_element_type=F32)


def _tn_dot(a, b):
    return lax.dot_general(a, b, (((0,), (0,)), ((), ())), preferred_element_type=F32)


A_Q0, A_KVC0, A_KVS0, A_KVW0 = 0, 512, 768, 1024
A_GQ0, A_GK0, A_GV0, A_GG0, A_MISC0 = 1280, 1536, 1792, 2304, 2816
A_COLS = A_MISC0 + LANES
MISC_LR0 = 3 * NSA_HEADS
KV_HALF = NSA_KV * HD


def _rope_lanes(x, cos_t, sin_lo, sin_hi):
    reps = x.shape[1] // LANES
    tile = (lambda t: jnp.concatenate([t] * reps, axis=1)) if reps > 1 else (lambda t: t)
    w = x.shape[1]
    half = ROPE_DIM // 2
    return x * tile(cos_t) + pltpu.roll(x, half, 1) * tile(sin_hi) + pltpu.roll(x, w - half, 1) * tile(sin_lo)


N_PROMPT_TILES = N_PROMPT // ROW_TILE
_PROMPT_ROWS = pl.BlockSpec((ROW_TILE, D_MODEL), lambda i: (jnp.minimum(i, N_PROMPT_TILES - 1), 0))
_SAMPLE_ROWS = pl.BlockSpec((ROW_TILE, D_MODEL), lambda i: (0, 0))


def _token_rows(prompt_ref, sample_ref):
    return jnp.where(pl.program_id(0) < N_PROMPT_TILES, prompt_ref[...], sample_ref[...])


def _inproj_a_body(hp_ref, hs_ref, g_ref, w_ref, wa2_ref, ba_ref, cos_ref, slo_ref, shi_ref,
                   q_ref, kvc_ref, kvs_ref, kvw_ref, kvct_p_ref, kvct_s_ref, kvst_p_ref, kvst_s_ref, kvwt_p_ref,
                   kvwt_s_ref, ks_ref, vst_ref, kw_ref, vwt_ref, gates_ref, gq_ref, gk_ref, gv_ref, la_ref,
                   sgg_ref):
    y = _rms(jnp.where(pl.program_id(0) == 0, hs_ref[...], hp_ref[...]), g_ref[...]).astype(BF16)
    proj = lambda a, b: jnp.dot(y, w_ref[:, a:b], preferred_element_type=F32)
    cos_t, sin_lo, sin_hi = cos_ref[...], slo_ref[...], shi_ref[...]
    q_ref[...] = (_rope_lanes(proj(A_Q0, A_KVC0), cos_t, sin_lo, sin_hi) * (HD ** -0.5)).astype(BF16)

    def store_feature_major(p_ref, s_ref, xt):
        p_ref[0] = xt
        s_ref[...] = xt

    kvc = proj(A_KVC0, A_KVS0)
    kvc_ref[...] = kvc
    store_feature_major(kvct_p_ref, kvct_s_ref, kvc.T)
    for a0, kv_ref, kvt_p_ref, kvt_s_ref, k_ref, vt_ref in (
            (A_KVS0, kvs_ref, kvst_p_ref, kvst_s_ref, ks_ref, vst_ref),
            (A_KVW0, kvw_ref, kvwt_p_ref, kvwt_s_ref, kw_ref, vwt_ref)):
        kv = proj(a0, a0 + 2 * KV_HALF)
        k = _rope_lanes(kv[:, :KV_HALF], cos_t, sin_lo, sin_hi)
        vt = kv[:, KV_HALF:].T
        kv_ref[:, :KV_HALF] = k
        kv_ref[:, KV_HALF:] = kv[:, KV_HALF:]
        store_feature_major(kvt_p_ref, kvt_s_ref, jnp.concatenate([k.T, vt], axis=0))
        k_ref[...] = k.astype(BF16)
        vt_ref[...] = vt.astype(BF16)
    misc = proj(A_MISC0, A_COLS)
    gates_ref[...] = jax.nn.sigmoid(misc)
    z = jnp.dot(misc.astype(BF16), wa2_ref[...], preferred_element_type=F32) + ba_ref[...]
    la_ref[...] = -_softplus(-z) * (1.0 / GLA_TAU)
    gq_ref[...] = proj(A_GQ0, A_GK0) * (GLA_DK ** -0.5)
    gk_ref[...] = proj(A_GK0, A_GV0)
    gv_ref[...] = proj(A_GV0, A_GG0).astype(BF16)
    gg = proj(A_GG0, A_MISC0)
    sgg_ref[...] = gg * jax.nn.sigmoid(gg)


def _rope_tables(pos):
    pos = np.asarray(pos)
    half = ROPE_DIM // 2
    f4 = np.float32
    inv = (f4(1.0) / (f4(ROPE_THETA) ** (np.arange(0, ROPE_DIM, 2, dtype=f4) / f4(ROPE_DIM)))).astype(f4)
    ang = (pos.astype(f4)[:, None] * inv[None, :]).astype(np.float64)
    cos, sin = np.cos(ang).astype(f4), np.sin(ang).astype(f4)
    n = pos.shape[0]
    one = np.ones((n, HD - ROPE_DIM), f4)
    zero = np.zeros((n, HD - ROPE_DIM), f4)
    zh = np.zeros((n, half), f4)
    seg = lambda a, b, rest: jnp.asarray(np.concatenate([a, b, rest] * (LANES // HD), axis=1))
    return seg(cos, cos, one), seg(-sin, zh, zero), seg(zh, sin, zero)


def inproj_a(h_prompt, h_sample, g, w_in, wa2, ba, pos):
    n = h_prompt.shape[0] + h_sample.shape[0]
    zpad = jnp.zeros((D_MODEL, LANES - 3 * NSA_HEADS - GLA_LOWRANK), F32)
    o = np.cumsum((0,) + A_SIZES)
    w = jnp.concatenate([w_in[:, o[0]:o[4]], w_in[:, o[5]:o[8]], w_in[:, o[9]:o[10]],
                         w_in[:, o[4]:o[5]], w_in[:, o[8]:o[9]], zpad], axis=1).astype(BF16)
    wa2p = jnp.zeros((LANES, GLA_K_W), F32).at[MISC_LR0:MISC_LR0 + GLA_LOWRANK].set(wa2).astype(BF16)
    seq_tiles = SEQ // ROW_TILE
    ptile = lambda s: jnp.maximum(s - 1, 0)
    tile = lambda s: jnp.where(s == 0, N_PROMPT_TILES, s - 1)
    rows_t = lambda wd: pl.BlockSpec((ROW_TILE, wd), lambda s: (tile(s), 0))
    pos_rows = pl.BlockSpec((ROW_TILE, LANES), lambda s: (jnp.where(s == 0, seq_tiles, (s - 1) % seq_tiles), 0))
    cols = pl.BlockSpec((KV_HALF, ROW_TILE), lambda s: (0, tile(s)))
    kvt_p = (pl.BlockSpec((1, NSA_KV_W, ROW_TILE), lambda s: (ptile(s) // seq_tiles, 0, ptile(s) % seq_tiles)),
             (BATCH, NSA_KV_W, SEQ), F32)
    kvt_s = (pl.BlockSpec((NSA_KV_W, ROW_TILE), lambda s: (0, jnp.minimum(s, 1))), (NSA_KV_W, 2 * ROW_TILE), F32)
    outs = [(rows_t(NSA_Q_W), (n, NSA_Q_W), BF16), (rows_t(NSA_KV_W), (n, NSA_KV_W), F32),
            (rows_t(NSA_KV_W), (n, NSA_KV_W), F32), (rows_t(NSA_KV_W), (n, NSA_KV_W), F32),
            kvt_p, kvt_s, kvt_p, kvt_s, kvt_p, kvt_s,
            (rows_t(KV_HALF), (n, KV_HALF), BF16), (cols, (KV_HALF, n), BF16),
            (rows_t(KV_HALF), (n, KV_HALF), BF16), (cols, (KV_HALF, n), BF16),
            (rows_t(LANES), (n, LANES), F32), (rows_t(GLA_K_W), (n, GLA_K_W), F32),
            (rows_t(GLA_K_W), (n, GLA_K_W), F32), (rows_t(GLA_V_W), (n, GLA_V_W), BF16),
            (rows_t(GLA_K_W), (n, GLA_K_W), F32), (rows_t(GLA_V_W), (n, GLA_V_W), F32)]
    return pl.pallas_call(
        _inproj_a_body,
        grid=(n // ROW_TILE,),
        in_specs=[pl.BlockSpec((ROW_TILE, D_MODEL), lambda s: (ptile(s), 0)), _SAMPLE_ROWS, _full((1, D_MODEL)),
                  _full((D_MODEL, A_COLS)), _full((LANES, GLA_K_W)),
                  _full((1, GLA_K_W)), pos_rows, pos_rows, pos_rows],
        out_specs=[s for s, _, _ in outs],
        out_shape=[jax.ShapeDtypeStruct(shape, dt) for _, shape, dt in outs],
        compiler_params=_params(1),
        name="inproj_a",
    )(h_prompt, h_sample, g.reshape(1, D_MODEL), w, wa2p, ba.reshape(1, GLA_K_W), *_rope_tables(pos))


CMP_TILE_BLOCKS = 256
CMP_TILE_ROWS = CMP_TILE_BLOCKS * CMP_BLOCK
CMP_TILE_PAGES = CMP_TILE_ROWS // PAGE_SIZE


def _cmp_reduce(xk_ref, xv_ref, pe_ref, w_ref):
    acc = jnp.zeros((CMP_TILE_BLOCKS, NSA_KV_W), F32)
    for l in range(CMP_BLOCK):
        rows = pl.ds(l, CMP_TILE_BLOCKS, stride=CMP_BLOCK)
        xl = jnp.concatenate([xk_ref[rows, :], xv_ref[rows, :]], axis=1) + pe_ref[l:l + 1, :]
        acc = acc + jnp.dot(xl.astype(BF16), w_ref[l], preferred_element_type=F32)
    return acc


def _cmp_rows_body(xk_ref, xv_ref, pe_ref, w_ref, o_ref):
    o_ref[...] = _cmp_reduce(xk_ref, xv_ref, pe_ref, w_ref)


def _cmp_pages_body(x_ref, pe_ref, w_ref, o_ref, xk_sc, xv_sc):
    for pg in range(CMP_TILE_PAGES):
        rows = slice(pg * PAGE_SIZE, (pg + 1) * PAGE_SIZE)
        xk_sc[rows, :] = x_ref[pg, :KV_HALF, :].T
        xv_sc[rows, :] = x_ref[pg, KV_HALF:, :].T
    o_ref[...] = _cmp_reduce(xk_sc, xv_sc, pe_ref, w_ref)


def _cmp_weights(pe, w_cmp):
    pe_rows = jnp.broadcast_to(pe[:, :, None, :], (CMP_BLOCK, 2, NSA_KV, HD)).reshape(CMP_BLOCK, NSA_KV_W)
    w = w_cmp.astype(BF16)
    zero = jnp.zeros((CMP_BLOCK, HD, HD), BF16)
    diag = [w[:, c] for c in range(2) for _ in range(NSA_KV)]
    w_bd = jnp.concatenate([jnp.concatenate([blk if j == i else zero for j in range(len(diag))], axis=2)
                            for i, blk in enumerate(diag)], axis=1)
    return pe_rows, w_bd


def cmp_blocks_rows(x, n_rows, cmp_weights):
    return pl.pallas_call(
        _cmp_rows_body,
        grid=(n_rows // CMP_TILE_ROWS,),
        in_specs=[pl.BlockSpec((CMP_TILE_ROWS, KV_HALF), lambda i: (i, 0)),
                  pl.BlockSpec((CMP_TILE_ROWS, KV_HALF), lambda i: (i, 1)), _full((CMP_BLOCK, NSA_KV_W)),
                  _full((CMP_BLOCK, NSA_KV_W, NSA_KV_W))],
        out_specs=pl.BlockSpec((CMP_TILE_BLOCKS, NSA_KV_W), lambda i: (i, 0)),
        out_shape=jax.ShapeDtypeStruct((n_rows // CMP_BLOCK, NSA_KV_W), F32),
        compiler_params=_params(1),
        name="cmp_blocks_rows",
    )(x, x, *cmp_weights)


def cmp_blocks_pages(xt, cmp_weights):
    n_pages = xt.shape[0]
    return pl.pallas_call(
        _cmp_pages_body,
        grid=(n_pages // CMP_TILE_PAGES,),
        in_specs=[pl.BlockSpec((CMP_TILE_PAGES, NSA_KV_W, PAGE_SIZE), lambda i: (i, 0, 0)),
                  _full((CMP_BLOCK, NSA_KV_W)), _full((CMP_BLOCK, NSA_KV_W, NSA_KV_W))],
        out_specs=pl.BlockSpec((CMP_TILE_BLOCKS, NSA_KV_W), lambda i: (i, 0)),
        out_shape=jax.ShapeDtypeStruct((n_pages * PAGE_SIZE // CMP_BLOCK, NSA_KV_W), F32),
        scratch_shapes=[pltpu.VMEM((CMP_TILE_ROWS, KV_HALF), F32), pltpu.VMEM((CMP_TILE_ROWS, KV_HALF), F32)],
        compiler_params=_params(1),
        name="cmp_blocks_pages",
    )(xt, *cmp_weights)


KEY_TILE = 256
NEG_BIG = -1e30
N_CMP_PROMPT = SEQ // CMP_BLOCK
N_SLC_PROMPT = SEQ // SLC_BLOCK
SLC_SHIFT = SLC_BLOCK.bit_length() - 1
N_CMP_DEC = PAST_LEN // CMP_BLOCK
N_SLC_DEC = -(-(PAST_LEN + DEC_SEQ) // SLC_BLOCK)
N_SLC_DEC_PAD = -(-N_SLC_DEC // LANES) * LANES
DEC_COLS_PER_GROUP = NSA_REP * DEC_SEQ


def _tile_cols(x, reps):
    return jnp.concatenate([x] * reps, axis=1) if reps > 1 else x


def _even_first_cmp_end(n_cmp):
    j = lax.broadcasted_iota(jnp.int32, (n_cmp, 1), 0)
    blk = jnp.where(j < n_cmp // 2, 2 * j, 2 * (j - n_cmp // 2) + 1)
    return (blk + 1) * CMP_BLOCK - 1


def _softmax_cols(s, mask):
    s = jnp.where(mask, s, -jnp.inf)
    m = jnp.max(s, axis=0, keepdims=True)
    m = jnp.where(m > -jnp.inf, m, 0.0)
    e = jnp.where(mask, jnp.exp(s - m), 0.0)
    return e / jnp.maximum(jnp.sum(e, axis=0, keepdims=True), 1e-30)


def _softmax_cols_biased(s):
    m = jnp.max(s, axis=0, keepdims=True)
    e = jnp.exp(s - m)
    scale = jnp.where(m > 0.5 * NEG_BIG, 1.0 / jnp.sum(e, axis=0, keepdims=True), 0.0)
    return e * scale


N_FORCED = 3


def _select_cols(p_slc, qpos, n_top):
    ns = p_slc.shape[0]
    blk = lax.broadcasted_iota(jnp.int32, p_slc.shape, 0)
    cur = qpos >> SLC_SHIFT
    forced = ((blk == 0) | (blk == cur) | (blk == cur - 1)) & (blk <= cur)
    score = jnp.where((blk <= cur) & jnp.logical_not(forced), p_slc, -jnp.inf)
    sel = jnp.where(forced, 1.0, 0.0)
    for _ in range(n_top - N_FORCED):
        m = jnp.max(score, axis=0, keepdims=True)
        idx = jnp.min(jnp.where(score == m, blk, ns), axis=0, keepdims=True)
        hit = blk == idx
        sel = jnp.where(hit & (m > -jnp.inf), 1.0, sel)
        score = jnp.where(hit, -jnp.inf, score)
    return sel


def _flash_init(m_sc, l_sc, acc_sc):
    m_sc[...] = jnp.full(m_sc.shape, NEG_BIG, F32)
    l_sc[...] = jnp.zeros(l_sc.shape, F32)
    acc_sc[...] = jnp.zeros(acc_sc.shape, F32)


def _flash_cols(scores, mask, pv, m_sc, l_sc, acc_sc):
    s = jnp.where(mask, scores, NEG_BIG)
    m_old = m_sc[...]
    m_new = jnp.maximum(m_old, jnp.max(s, axis=0, keepdims=True))
    alpha = jnp.exp(m_old - m_new)
    p = jnp.where(mask, jnp.exp(s - m_new), 0.0)
    l_sc[...] = alpha * l_sc[...] + jnp.sum(p, axis=0, keepdims=True)
    acc_sc[...] = alpha * acc_sc[...] + pv(p.astype(BF16))
    m_sc[...] = m_new


def _flash_stream(score_fn, pv_fn, first, lo, n, stream_sc, m_sc, l_sc, acc_sc):
    (sa, ca), (sb, cb) = stream_sc

    def issue(s_ref, c_ref, kt, self_tile=False):
        s = score_fn(kt, self_tile)
        s_ref[...] = s
        c_ref[...] = jnp.max(s, axis=0, keepdims=True)

    def consume(s_ref, c_ref, kt):
        m_old = m_sc[...]
        m_new = jnp.maximum(m_old, c_ref[...])
        alpha = jnp.exp(m_old - m_new)
        p = jnp.exp(s_ref[...] - m_new)
        l_sc[...] = alpha * l_sc[...] + jnp.sum(p, axis=0, keepdims=True)
        acc_sc[...] = alpha * acc_sc[...] + pv_fn(kt, p.astype(BF16))
        m_sc[...] = m_new

    _flash_init(m_sc, l_sc, acc_sc)
    issue(sa, ca, first, True)

    def two_tiles(jj, kt_a):
        t0 = lo + 2 * jj
        issue(sb, cb, t0)
        consume(sa, ca, kt_a)
        issue(sa, ca, t0 + 1)
        consume(sb, cb, t0)
        return t0 + 1

    kt_a = lax.fori_loop(0, n // 2, two_tiles, first)

    @pl.when(n % 2 == 1)
    def _():
        issue(sb, cb, lo + n - 1)
        consume(sa, ca, kt_a)
        consume(sb, cb, lo + n - 1)

    @pl.when(n % 2 == 0)
    def _():
        consume(sa, ca, kt_a)

    return acc_sc[...] / jnp.maximum(l_sc[...], 1e-30)


def _flash_out(l_sc, acc_sc):
    return acc_sc[...] / jnp.maximum(l_sc[...], 1e-30)


def _pv_split(vt, p):
    c2 = p.shape[1] // 2
    return jnp.concatenate([jnp.dot(vt[:HD], p[:, :c2], preferred_element_type=F32),
                            jnp.dot(vt[HD:], p[:, c2:], preferred_element_type=F32)], axis=1)


def _nsa_prompt_body(q_ref, gates_ref, ckv_ref, cos_ref, slo_ref, shi_ref, ks_ref, vst_ref, kw_ref, vwt_ref,
                     o_ref, m_sc, l_sc, acc_sc, sel_bias_sc, sa_sc, ca_sc, sb_sc, cb_sc):
    stream_sc = ((sa_sc, ca_sc), (sb_sc, cb_sc))
    i = pl.program_id(1)
    nq = Q_BLOCK
    cols = NSA_HEADS * nq
    qpos = i * nq + lax.broadcasted_iota(jnp.int32, (1, nq), 1)
    q = q_ref[...].astype(F32)
    pairs = [q[:, j * LANES:(j + 1) * LANES].T for j in range(NSA_HEADS // 2)]
    zero = jnp.zeros((HD, cols // 2), F32)
    qt_g = [jnp.concatenate([pairs[2 * g][:HD], pairs[2 * g][HD:], pairs[2 * g + 1][:HD], pairs[2 * g + 1][HD:]],
                            axis=1) for g in range(NSA_KV)]
    qt = jnp.concatenate([jnp.concatenate([qt_g[0], zero], axis=1),
                          jnp.concatenate([zero, qt_g[1]], axis=1)], axis=0).astype(BF16)
    ckv = ckv_ref[0]
    ck = _rope_lanes(ckv[:, :KV_HALF], cos_ref[...], slo_ref[...], shi_ref[...]).astype(BF16)
    cvt = ckv[:, KV_HALF:].T.astype(BF16)
    c_bias = jnp.where(_even_first_cmp_end(N_CMP_PROMPT) <= qpos, 0.0, NEG_BIG)
    p = _softmax_cols_biased(jnp.dot(ck, qt, preferred_element_type=F32) + _tile_cols(c_bias, NSA_HEADS))
    o_c = _pv_split(cvt, p.astype(BF16))
    sel = []
    for g in range(NSA_KV):
        c0 = g * NSA_REP * nq
        p_grp = p[:, c0:c0 + nq]
        for r in range(1, NSA_REP):
            p_grp = p_grp + p[:, c0 + r * nq:c0 + (r + 1) * nq]
        p_slc = p_grp[:N_CMP_PROMPT // 2] + p_grp[N_CMP_PROMPT // 2:]
        sel.append(_select_cols(p_slc, qpos, SLC_TOPK))
    sel = jnp.concatenate(sel, axis=1)
    sel_bias_sc[...] = jnp.where(sel > 0.5, 0.0, NEG_BIG)
    key_row = lax.broadcasted_iota(jnp.int32, (KEY_TILE, 1), 0)
    blocks_per_tile = KEY_TILE // SLC_BLOCK
    kt_self = i // (KEY_TILE // nq)

    def slc_scores(kt, self_tile):
        k0 = pl.multiple_of(kt * KEY_TILE, KEY_TILE)
        rows = []
        for j in range(blocks_per_tile):
            b = sel_bias_sc[pl.ds(kt * blocks_per_tile + j, 1), :]
            rows.append(jnp.concatenate(
                [jnp.broadcast_to(b[:, g * nq:(g + 1) * nq], (SLC_BLOCK, nq)) for g in range(NSA_KV)
                 for _ in range(NSA_REP)], axis=1))
        bias = jnp.concatenate(rows, axis=0)
        if self_tile:
            bias = bias + _tile_cols(jnp.where(k0 + key_row <= qpos, 0.0, NEG_BIG), NSA_HEADS)
        return jnp.dot(ks_ref[pl.ds(k0, KEY_TILE), :], qt, preferred_element_type=F32) + bias

    def slc_pv(kt, pb):
        return _pv_split(vst_ref[:, pl.ds(pl.multiple_of(kt * KEY_TILE, KEY_TILE), KEY_TILE)], pb)

    o_s = _flash_stream(slc_scores, slc_pv, kt_self, 0, kt_self, stream_sc, m_sc, l_sc, acc_sc)

    def win_scores(kt, self_tile):
        k0 = pl.multiple_of(kt * KEY_TILE, KEY_TILE)
        d = qpos - (k0 + key_row)
        bias = jnp.where((d >= 0) & (d < WINDOW), 0.0, NEG_BIG)
        return jnp.dot(kw_ref[pl.ds(k0, KEY_TILE), :], qt, preferred_element_type=F32) + _tile_cols(bias, NSA_HEADS)

    def win_pv(kt, pb):
        return _pv_split(vwt_ref[:, pl.ds(pl.multiple_of(kt * KEY_TILE, KEY_TILE), KEY_TILE)], pb)

    win_lo = jnp.maximum(i - WINDOW // nq, 0) // (KEY_TILE // nq)
    o_w = _flash_stream(win_scores, win_pv, kt_self, win_lo, kt_self - win_lo, stream_sc, m_sc, l_sc, acc_sc)
    gates_t = gates_ref[...].T
    merged = []
    for hd in range(NSA_HEADS):
        cs = slice(hd * nq, (hd + 1) * nq)
        gate = lambda br: gates_t[3 * hd + br:3 * hd + br + 1]
        merged.append(gate(0) * o_c[:, cs] + gate(1) * o_s[:, cs] + gate(2) * o_w[:, cs])
    for j in range(NSA_HEADS // 2):
        pair = jnp.concatenate([merged[2 * j], merged[2 * j + 1]], axis=0).T
        o_ref[:, j * LANES:(j + 1) * LANES] = pair.astype(BF16)


def nsa_prompt_attn(q, gates, ckv, ks, vst, kw, vwt):
    nqb = SEQ // Q_BLOCK
    tok = lambda wd: pl.BlockSpec((Q_BLOCK, wd), lambda b, i: (b * nqb + i, 0))
    seq_rows = pl.BlockSpec((SEQ, KV_HALF), lambda b, i: (b, 0))
    seq_cols = pl.BlockSpec((KV_HALF, SEQ), lambda b, i: (0, b))
    n_cmp = N_CMP_PROMPT
    c_blk = np.concatenate([np.arange(0, n_cmp, 2), np.arange(1, n_cmp, 2)])
    cols = NSA_HEADS * Q_BLOCK
    return pl.pallas_call(
        _nsa_prompt_body,
        grid=(BATCH, nqb),
        in_specs=[tok(NSA_Q_W), tok(LANES), pl.BlockSpec((1, n_cmp, NSA_KV_W), lambda b, i: (b, 0, 0)),
                  _full((n_cmp, LANES)), _full((n_cmp, LANES)), _full((n_cmp, LANES)),
                  seq_rows, seq_cols, seq_rows, seq_cols],
        out_specs=tok(NSA_Q_W),
        out_shape=jax.ShapeDtypeStruct((N_PROMPT, NSA_Q_W), BF16),
        scratch_shapes=[pltpu.VMEM((1, cols), F32), pltpu.VMEM((1, cols), F32), pltpu.VMEM((HD, cols), F32),
                        pltpu.VMEM((N_SLC_PROMPT, NSA_KV * Q_BLOCK), F32),
                        pltpu.VMEM((KEY_TILE, cols), F32), pltpu.VMEM((1, cols), F32),
                        pltpu.VMEM((KEY_TILE, cols), F32), pltpu.VMEM((1, cols), F32)],
        compiler_params=_params(2),
        name="nsa_prompt",
    )(q, gates, ckv, *_rope_tables((c_blk + 1) * CMP_BLOCK - 1), ks, vst, kw, vwt)


def _nsa_decode_body(pt_ref, qt_ref, gate_ref, ckv_ref, cos_ref, slo_ref, shi_ref, hsum_ref, *refs):
    page_refs = refs[:N_PAGES]
    win_ref, kvs_new_ref, kvw_new_ref, o_ref, win_out_ref, m_sc, l_sc, acc_sc = refs[N_PAGES:]
    qt = qt_ref[0]
    lane = lax.broadcasted_iota(jnp.int32, (1, LANES), 1)
    qpos = PAST_LEN + (lane & (DEC_SEQ - 1))
    group0 = lane < DEC_COLS_PER_GROUP
    ckv = ckv_ref[0]
    ck = _rope_lanes(ckv[:, :KV_HALF], cos_ref[...], slo_ref[...], shi_ref[...]).astype(BF16)
    c_mask = _even_first_cmp_end(N_CMP_DEC) <= qpos
    p = _softmax_cols(jnp.dot(ck, qt, preferred_element_type=F32), c_mask)
    o_c = _tn_dot(ckv[:, KV_HALF:].astype(BF16), p.astype(BF16))
    p_grp = jnp.dot(p, hsum_ref[...], preferred_element_type=F32, precision=lax.Precision.HIGHEST)
    p_slc = jnp.concatenate([p_grp[:N_CMP_DEC // 2] + p_grp[N_CMP_DEC // 2:],
                             jnp.zeros((N_SLC_DEC_PAD - N_CMP_DEC // 2, LANES), F32)], axis=0)
    sel = _select_cols(p_slc, qpos, SLC_TOPK)

    def new_rows(ref):
        row = ref[0]
        kv = jnp.concatenate([row[:, t * NSA_KV_W:(t + 1) * NSA_KV_W] for t in range(DEC_SEQ)], axis=0)
        return jnp.concatenate([kv, jnp.zeros((SUBLANES - DEC_SEQ, NSA_KV_W), F32)], axis=0)

    new_row = lax.broadcasted_iota(jnp.int32, (SUBLANES, 1), 0)
    new_pos = PAST_LEN + new_row
    new_valid = new_row < DEC_SEQ
    _flash_init(m_sc, l_sc, acc_sc)
    kt_old = jnp.concatenate([r[0, :KV_HALF, :] for r in page_refs], axis=1).astype(BF16)
    vt_old = jnp.concatenate([r[0, KV_HALF:, :] for r in page_refs], axis=1).astype(BF16)
    key_blk = lax.broadcasted_iota(jnp.int32, (PAST_LEN, N_SLC_DEC_PAD), 0) >> SLC_SHIFT
    blk_col = lax.broadcasted_iota(jnp.int32, (PAST_LEN, N_SLC_DEC_PAD), 1)
    chosen = jnp.dot(jnp.where(key_blk == blk_col, 1.0, 0.0).astype(BF16), sel.astype(BF16),
                     preferred_element_type=F32) > 0.5
    old_pos = lax.broadcasted_iota(jnp.int32, (PAST_LEN, 1), 0)
    _flash_cols(_tn_dot(kt_old, qt), chosen & (old_pos <= qpos),
                lambda pb: jnp.dot(vt_old, pb, preferred_element_type=F32), m_sc, l_sc, acc_sc)
    kv_new = new_rows(kvs_new_ref)
    sel_new = sel[(PAST_LEN >> SLC_SHIFT):(PAST_LEN >> SLC_SHIFT) + 1] > 0.5
    v_new = kv_new[:, KV_HALF:].astype(BF16)
    _flash_cols(jnp.dot(kv_new[:, :KV_HALF].astype(BF16), qt, preferred_element_type=F32),
                sel_new & new_valid & (new_pos <= qpos), lambda pb: _tn_dot(v_new, pb), m_sc, l_sc, acc_sc)
    o_s = _flash_out(l_sc, acc_sc)
    _flash_init(m_sc, l_sc, acc_sc)
    n_buf = win_ref.shape[2]
    win = win_ref[0]
    d = qpos - (PAST_LEN - n_buf + lax.broadcasted_iota(jnp.int32, (n_buf, 1), 0))
    vt_win = win[KV_HALF:].astype(BF16)
    _flash_cols(_tn_dot(win[:KV_HALF].astype(BF16), qt), (d >= 0) & (d < WINDOW),
                lambda pb: jnp.dot(vt_win, pb, preferred_element_type=F32), m_sc, l_sc, acc_sc)
    kw_new = new_rows(kvw_new_ref)
    d = qpos - new_pos
    vw_new = kw_new[:, KV_HALF:].astype(BF16)
    _flash_cols(jnp.dot(kw_new[:, :KV_HALF].astype(BF16), qt, preferred_element_type=F32),
                new_valid & (d >= 0) & (d < WINDOW), lambda pb: _tn_dot(vw_new, pb), m_sc, l_sc, acc_sc)
    o_w = _flash_out(l_sc, acc_sc)
    g = gate_ref[0]
    o = g[0:1] * o_c + g[1:2] * o_s + g[2:3] * o_w
    o_ref[0] = jnp.where(group0, o[:HD], o[HD:])
    key = lax.broadcasted_iota(jnp.int32, (SUBLANES, n_buf), 1)
    place = jnp.where((key == n_buf - DEC_SEQ + new_row) & new_valid, 1.0, 0.0)
    placed = lax.dot_general(kw_new, place, (((0,), (0,)), ((), ())), preferred_element_type=F32,
                             precision=lax.Precision.HIGHEST)
    keep = lax.broadcasted_iota(jnp.int32, (1, n_buf), 1) < n_buf - DEC_SEQ
    win_out_ref[0] = jnp.where(keep, pltpu.roll(win, n_buf - DEC_SEQ, 1), placed)


def nsa_decode_attn(qt, gate_rows, ckv, slc_pool, win_buf, kvs_new, kvw_new, page_table):
    n_buf = win_buf.shape[2]
    per_b = lambda *shape: pl.BlockSpec((1,) + shape, lambda b, pt: (b,) + (0,) * len(shape))
    const = lambda *shape: pl.BlockSpec(shape, lambda b, pt: (0,) * len(shape))
    page = lambda j: pl.BlockSpec((1, NSA_KV_W, PAGE_SIZE), lambda b, pt: (pt[b, j], 0, 0))
    c_blk = np.concatenate([np.arange(0, N_CMP_DEC, 2), np.arange(1, N_CMP_DEC, 2)])
    col = jnp.arange(LANES)
    used = col < NSA_KV * DEC_COLS_PER_GROUP
    same = (col[:, None] // DEC_COLS_PER_GROUP == col[None, :] // DEC_COLS_PER_GROUP) & \
           (col[:, None] % DEC_SEQ == col[None, :] % DEC_SEQ) & used[:, None] & used[None, :]
    return pl.pallas_call(
        _nsa_decode_body,
        grid_spec=pltpu.PrefetchScalarGridSpec(
            num_scalar_prefetch=1,
            grid=(DEC_BATCH,),
            in_specs=[per_b(LANES, LANES), per_b(3, LANES), per_b(N_CMP_DEC, NSA_KV_W),
                      const(N_CMP_DEC, LANES), const(N_CMP_DEC, LANES), const(N_CMP_DEC, LANES),
                      const(LANES, LANES)] + [page(j) for j in range(N_PAGES)]
                     + [per_b(NSA_KV_W, n_buf), per_b(1, DEC_SEQ * NSA_KV_W), per_b(1, DEC_SEQ * NSA_KV_W)],
            out_specs=[per_b(HD, LANES), per_b(NSA_KV_W, n_buf)],
            scratch_shapes=[pltpu.VMEM((1, LANES), F32), pltpu.VMEM((1, LANES), F32),
                            pltpu.VMEM((LANES, LANES), F32)],
        ),
        out_shape=[jax.ShapeDtypeStruct((DEC_BATCH, HD, LANES), F32),
                   jax.ShapeDtypeStruct((DEC_BATCH, NSA_KV_W, n_buf), F32)],
        compiler_params=_params(1),
        name="nsa_decode",
    )(page_table, qt, gate_rows, ckv, *_rope_tables((c_blk + 1) * CMP_BLOCK - 1), same.astype(F32),
      *([slc_pool] * N_PAGES), win_buf, kvs_new, kvw_new)


GLA_PAIRS = GLA_HEADS // 2
GLA_ROWS = 512
GLA_SUB = 16
GLA_STEP_SEQS = 8


def _gla_rows(q, k, la, v, sgg, gnorm, st_ref, sub):
    R = q.shape[0]
    row = lax.broadcasted_iota(jnp.int32, (R, GLA_K_W), 0)
    rin = row % sub
    cum = la
    d = 1
    while d < sub:
        cum = cum + jnp.where(rin >= d, pltpu.roll(cum, d, 0), 0.0)
        d *= 2
    lane = lax.broadcasted_iota(jnp.int32, (sub, LANES), 1)
    lo = lane < GLA_DK
    rsub = lax.broadcasted_iota(jnp.int32, (sub, LANES), 0)
    out_rows = []
    for c in range(R // sub):
        rs = slice(c * sub, (c + 1) * sub)
        cum_c = cum[rs]
        last = cum_c[sub - 1:sub]
        qe = q[rs] * jnp.exp(cum_c)
        kdec = k[rs] * jnp.exp(last - cum_c)
        v_c = v[rs]
        heads = []
        for pr in range(GLA_PAIRS):
            ls = slice(pr * LANES, (pr + 1) * LANES)
            st = st_ref[pr]
            st_b = st.astype(BF16)
            qe_p, kd_p, q_p, k_p, cum_p = qe[:, ls], kdec[:, ls], q[rs, ls], k[rs, ls], cum_c[:, ls]
            v_pair = [v_c[:, (2 * pr + hh) * GLA_DV:(2 * pr + hh + 1) * GLA_DV] for hh in range(2)]
            upd = jnp.zeros((GLA_DV, LANES), F32)
            o_pair = []
            for hh in range(2):
                keep = lo if hh == 0 else jnp.logical_not(lo)
                o_pair.append(_nt_dot(jnp.where(keep, qe_p, 0.0).astype(BF16), st_b))
                upd = upd + _tn_dot(v_pair[hh].astype(BF16), jnp.where(keep, kd_p, 0.0).astype(BF16))
            for j in range(sub):
                dj = jnp.where(rsub >= j, jnp.exp(cum_p - cum_p[j:j + 1]), 0.0)
                w = q_p * k_p[j:j + 1] * dj
                a_lo = jnp.sum(jnp.where(lo, w, 0.0), axis=-1, keepdims=True)
                a_hi = jnp.sum(jnp.where(lo, 0.0, w), axis=-1, keepdims=True)
                o_pair[0] = o_pair[0] + a_lo * v_pair[0][j:j + 1]
                o_pair[1] = o_pair[1] + a_hi * v_pair[1][j:j + 1]
            st_ref[pr] = st * jnp.exp(last[:, ls]) + upd
            heads += o_pair
        out_rows.append(jnp.concatenate([_rms(x, gnorm) for x in heads], axis=1))
    return jnp.concatenate(out_rows, axis=0) * sgg


def _gla_seq_body(q_ref, k_ref, la_ref, v_ref, sgg_ref, gn_ref, o_ref, st_out_ref, st_sc):
    @pl.when(pl.program_id(1) == 0)
    def _():
        st_sc[...] = jnp.zeros(st_sc.shape, F32)

    o = _gla_rows(q_ref[...], k_ref[...], la_ref[...], v_ref[...].astype(F32), sgg_ref[...], gn_ref[...],
                  st_sc, GLA_SUB)
    o_ref[...] = o.astype(BF16)
    st_out_ref[0] = st_sc[...]


def gla_seq(q, k, la, v, sgg, gnorm):
    nt = SEQ // GLA_ROWS
    rows = lambda wd: pl.BlockSpec((GLA_ROWS, wd), lambda b, t: (b * nt + t, 0))
    return pl.pallas_call(
        _gla_seq_body,
        grid=(BATCH, nt),
        in_specs=[rows(GLA_K_W), rows(GLA_K_W), rows(GLA_K_W), rows(GLA_V_W), rows(GLA_V_W),
                  _full((1, GLA_DV))],
        out_specs=[rows(GLA_V_W), pl.BlockSpec((1, GLA_PAIRS, GLA_DV, LANES), lambda b, t: (b, 0, 0, 0))],
        out_shape=[jax.ShapeDtypeStruct((N_PROMPT, GLA_V_W), BF16),
                   jax.ShapeDtypeStruct((BATCH, GLA_PAIRS, GLA_DV, LANES), F32)],
        scratch_shapes=[pltpu.VMEM((GLA_PAIRS, GLA_DV, LANES), F32)],
        compiler_params=_params(2),
        name="gla_seq",
    )(q, k, la, v, sgg, gnorm.reshape(1, GLA_DV))


def _gla_step_body(q_ref, k_ref, la_ref, v_ref, sgg_ref, gn_ref, st_in_ref, o_ref, st_out_ref):
    st_out_ref[...] = st_in_ref[...]
    q, k, la, v, sgg = q_ref[...], k_ref[...], la_ref[...], v_ref[...].astype(F32), sgg_ref[...]
    for j in range(GLA_STEP_SEQS):
        rs = slice(j * DEC_SEQ, (j + 1) * DEC_SEQ)
        o = _gla_rows(q[rs], k[rs], la[rs], v[rs], sgg[rs], gn_ref[...], st_out_ref.at[j], DEC_SEQ)
        o_ref[rs, :] = o.astype(BF16)


def gla_step(q, k, la, v, sgg, gnorm, st_in, row0):
    rows_per = GLA_STEP_SEQS * DEC_SEQ
    blk0 = row0 // rows_per
    rows = lambda wd: pl.BlockSpec((rows_per, wd), lambda i: (blk0 + i, 0))
    st_spec = pl.BlockSpec((GLA_STEP_SEQS, GLA_PAIRS, GLA_DV, LANES), lambda i: (i, 0, 0, 0))
    return pl.pallas_call(
        _gla_step_body,
        grid=(DEC_BATCH // GLA_STEP_SEQS,),
        in_specs=[rows(GLA_K_W), rows(GLA_K_W), rows(GLA_K_W), rows(GLA_V_W), rows(GLA_V_W),
                  _full((1, GLA_DV)), st_spec],
        out_specs=[pl.BlockSpec((rows_per, GLA_V_W), lambda i: (i, 0)), st_spec],
        out_shape=[jax.ShapeDtypeStruct((N_SAMPLE, GLA_V_W), BF16),
                   jax.ShapeDtypeStruct((DEC_BATCH, GLA_PAIRS, GLA_DV, LANES), F32)],
        compiler_params=_params(1),
        name="gla_step",
    )(q, k, la, v, sgg, gnorm.reshape(1, GLA_DV), st_in)


def _gla_state_to_pairs(s):
    B = s.shape[0]
    return s.reshape(B, GLA_PAIRS, 2, GLA_DK, GLA_DV).transpose(0, 1, 4, 2, 3).reshape(B, GLA_PAIRS, GLA_DV, LANES)


def _gla_state_from_pairs(st):
    B = st.shape[0]
    return st.reshape(B, GLA_PAIRS, GLA_DV, 2, GLA_DK).transpose(0, 1, 3, 4, 2).reshape(B, GLA_HEADS, GLA_DK, GLA_DV)


LRU_TIME_TILE = 512


def _lru_gates(xc, wa_ref, ba, wx_ref, bx, lam):
    xcb = xc.astype(BF16)
    r_parts, i_parts = [], []
    for n in range(LRU_BLOCKS):
        xs = xcb[:, n * LRU_BW:(n + 1) * LRU_BW]
        r_parts.append(jnp.dot(xs, wa_ref[n], preferred_element_type=F32))
        i_parts.append(jnp.dot(xs, wx_ref[n], preferred_element_type=F32))
    r = jax.nn.sigmoid(jnp.concatenate(r_parts, axis=-1) + ba)
    i = jax.nn.sigmoid(jnp.concatenate(i_parts, axis=-1) + bx)
    log_a = -LRU_C * r * _softplus(-lam)
    a = jnp.exp(log_a)
    u = jnp.sqrt(1.0 - a * a) * (i * xc)
    return a, u


def _lru_seq_body(u_ref, cw_ref, cb_ref, wa_ref, ba_ref, wx_ref, bx_ref, lam_ref, y_ref, hT_ref, xp_sc, h_sc):
    tt = LRU_TIME_TILE

    @pl.when(pl.program_id(1) == 0)
    def _():
        xp_sc[...] = jnp.zeros((SUBLANES, D_RNN), F32)
        h_sc[...] = jnp.zeros((1, D_RNN), F32)

    x = u_ref[:, D_RNN:]
    head = jnp.concatenate([xp_sc[...], x[:SUBLANES]], axis=0)
    xc = cb_ref[...] + cw_ref[CONV_W - 1:CONV_W, :] * x
    for k in range(1, CONV_W):
        shifted = jnp.concatenate([head[SUBLANES - k:2 * SUBLANES - k], pltpu.roll(x, k, 0)[SUBLANES:]], axis=0)
        xc = xc + cw_ref[CONV_W - 1 - k:CONV_W - k, :] * shifted
    xp_sc[...] = x[tt - SUBLANES:]
    a, u = _lru_gates(xc, wa_ref, ba_ref[...], wx_ref, bx_ref[...], lam_ref[...])
    row = lax.broadcasted_iota(jnp.int32, (tt, D_RNN), 0) % SUBLANES
    d = 1
    while d < SUBLANES:
        keep = row >= d
        a_prev = jnp.where(keep, pltpu.roll(a, d, 0), 1.0)
        u_prev = jnp.where(keep, pltpu.roll(u, d, 0), 0.0)
        u = a * u_prev + u
        a = a * a_prev
        d *= 2
    carry = h_sc[...]
    hs = []
    for grp in range(tt // SUBLANES):
        rs = slice(grp * SUBLANES, (grp + 1) * SUBLANES)
        hs.append(a[rs] * carry + u[rs])
        carry = hs[-1][SUBLANES - 1:SUBLANES, :]
    h_sc[...] = carry
    hT_ref[0] = carry
    y_ref[...] = (_gelu_tanh(u_ref[:, :D_RNN]) * jnp.concatenate(hs, axis=0)).astype(BF16)


def _lru_weight_args(cw, cb, wa, ba, wx, bx, lam):
    row = lambda v: v.reshape(1, D_RNN)
    return (cw, row(cb), wa.astype(BF16), row(ba), wx.astype(BF16), row(bx), row(lam))


_LRU_WEIGHT_SPECS = [_full((CONV_W, D_RNN)), _full((1, D_RNN)), _full((LRU_BLOCKS, LRU_BW, LRU_BW)),
                     _full((1, D_RNN)), _full((LRU_BLOCKS, LRU_BW, LRU_BW)), _full((1, D_RNN)),
                     _full((1, D_RNN))]


def lru_seq(u, cw, cb, wa, ba, wx, bx, lam):
    tt = LRU_TIME_TILE
    nt = SEQ // tt
    y, hT = pl.pallas_call(
        _lru_seq_body,
        grid=(BATCH, nt),
        in_specs=[pl.BlockSpec((tt, 2 * D_RNN), lambda b, t: (b * nt + t, 0))] + _LRU_WEIGHT_SPECS,
        out_specs=[pl.BlockSpec((tt, D_RNN), lambda b, t: (b * nt + t, 0)),
                   pl.BlockSpec((1, 1, D_RNN), lambda b, t: (b, 0, 0))],
        out_shape=[jax.ShapeDtypeStruct((N_PROMPT, D_RNN), BF16), jax.ShapeDtypeStruct((BATCH, 1, D_RNN), F32)],
        scratch_shapes=[pltpu.VMEM((SUBLANES, D_RNN), F32), pltpu.VMEM((1, D_RNN), F32)],
        compiler_params=_params(2),
        name="lru_seq",
    )(u, *_lru_weight_args(cw, cb, wa, ba, wx, bx, lam))
    return y, hT.reshape(BATCH, D_RNN)


def _lru_step_body(u_ref, cs_ref, h0_ref, cw_ref, cb_ref, wa_ref, ba_ref, wx_ref, bx_ref, lam_ref, y_ref, hT_ref):
    n_t = u_ref.shape[0]
    hist = [cs_ref[:, w, :] for w in range(CONV_W - 1)] + [u_ref[t, :, D_RNN:] for t in range(n_t)]
    h = h0_ref[...]
    for t in range(n_t):
        xc = cb_ref[...]
        for w in range(CONV_W):
            xc = xc + cw_ref[w:w + 1, :] * hist[t + w]
        a, u = _lru_gates(xc, wa_ref, ba_ref[...], wx_ref, bx_ref[...], lam_ref[...])
        h = a * h + u
        y_ref[t] = (_gelu_tanh(u_ref[t, :, :D_RNN]) * h).astype(BF16)
    hT_ref[...] = h


def lru_step(u, conv_state, h0, cw, cb, wa, ba, wx, bx, lam):
    T, B, _ = u.shape
    return pl.pallas_call(
        _lru_step_body,
        out_shape=[jax.ShapeDtypeStruct((T, B, D_RNN), BF16), jax.ShapeDtypeStruct((B, D_RNN), F32)],
        compiler_params=pltpu.CompilerParams(vmem_limit_bytes=VMEM_LIMIT_BYTES),
        name="lru_step",
    )(u, conv_state, h0, *_lru_weight_args(cw, cb, wa, ba, wx, bx, lam))


ROUTE_E1, ROUTE_E2, ROUTE_G1, ROUTE_G2, ROUTE_R1, ROUTE_R2 = range(6)
EXPERT_LANE0 = N_GROUPS


def _lane_pick(val_by_lane):
    rows = next(iter(val_by_lane.values())).shape[0]
    lane = lax.broadcasted_iota(jnp.int32, (rows, LANES), 1)
    out = jnp.zeros((rows, LANES), F32)
    for l, v in val_by_lane.items():
        out = jnp.where(lane == l, v, out)
    return out


def _route_rows(logits, tri_ref, carry_ref):
    rows = logits.shape[0]
    lane = lax.broadcasted_iota(jnp.int32, (rows, LANES), 1)
    neg = -jnp.inf
    gl = jnp.where(lane < N_GROUPS, logits, neg)
    gmax = jnp.max(gl, axis=-1, keepdims=True)
    gtop = jnp.min(jnp.where(gl == gmax, lane, LANES), axis=-1, keepdims=True)
    gsum = jnp.sum(jnp.where(lane < N_GROUPS, jnp.exp(logits - gmax), 0.0), axis=-1, keepdims=True)
    g_w = 1.0 / gsum
    lo = EXPERT_LANE0 + EXP_PER_GROUP * gtop
    el = jnp.where((lane >= lo) & (lane < lo + EXP_PER_GROUP), logits, neg)
    v1 = jnp.max(el, axis=-1, keepdims=True)
    i1 = jnp.min(jnp.where(el == v1, lane, LANES), axis=-1, keepdims=True)
    el2 = jnp.where(lane == i1, neg, el)
    v2 = jnp.max(el2, axis=-1, keepdims=True)
    i2 = jnp.min(jnp.where(el2 == v2, lane, LANES), axis=-1, keepdims=True)
    p2 = jnp.exp(v2 - v1)
    den = 1.0 + p2
    gate1 = (1.0 / den) * g_w
    gate2 = (p2 / den) * g_w
    hit1 = lane == i1
    hit2 = lane == i2
    onehot = jnp.where(hit1 | hit2, 1.0, 0.0)
    before = jnp.dot(tri_ref[...], onehot.astype(BF16), preferred_element_type=F32) + carry_ref[...]
    rank1 = jnp.sum(jnp.where(hit1, before, 0.0), axis=-1, keepdims=True)
    rank2 = jnp.sum(jnp.where(hit2, before, 0.0), axis=-1, keepdims=True)
    carry_ref[...] = carry_ref[...] + jnp.sum(onehot, axis=0, keepdims=True)
    return _lane_pick({ROUTE_E1: (i1 - EXPERT_LANE0).astype(F32), ROUTE_E2: (i2 - EXPERT_LANE0).astype(F32),
                       ROUTE_G1: gate1, ROUTE_G2: gate2, ROUTE_R1: rank1, ROUTE_R2: rank2})


def _pack_bf16_halves(x):
    w = x.shape[1] // 2
    bits = lambda v: pltpu.bitcast(v.astype(F32), jnp.uint32)
    return pltpu.bitcast((bits(x[:, :w]) >> 16) | bits(x[:, w:]), F32)


def _unpack_bf16_halves(words):
    p = pltpu.bitcast(words, jnp.uint32)
    lo = pltpu.bitcast(p << 16, F32).astype(BF16)
    hi = pltpu.bitcast(p & jnp.uint32(0xFFFF0000), F32).astype(BF16)
    return jnp.concatenate([lo, hi], axis=1)


def _post_mixer_body(n_h, n_mix, *refs):
    h_refs, refs = refs[:n_h], refs[n_h:]
    m_refs, wo_refs = refs[:2 * n_mix], refs[2 * n_mix:3 * n_mix]
    g_ref, wr_ref, br_ref, tri_ref, h1_ref, hn_ref, route_ref, cnt_ref, carry_sc = refs[3 * n_mix:]

    @pl.when(pl.program_id(0) == 0)
    def _():
        carry_sc[...] = jnp.zeros((1, LANES), F32)

    mix = None
    for j, wo_ref in enumerate(wo_refs):
        part = jnp.dot(_token_rows(m_refs[2 * j], m_refs[2 * j + 1]), wo_ref[...], preferred_element_type=F32)
        mix = part if mix is None else mix + part
    h1 = (h_refs[0][...] if n_h == 1 else _token_rows(*h_refs)) + mix
    h1_ref[...] = h1
    hn = _rms(h1, g_ref[...]).astype(BF16)
    hn_ref[...] = _pack_bf16_halves(hn)
    logits = jnp.dot(hn, wr_ref[...], preferred_element_type=F32) + br_ref[...]
    route_ref[...] = _route_rows(logits, tri_ref, carry_sc)
    cnt_ref[...] = carry_sc[...]


def post_mixer(h, mix_ins, w_out, g_ffn, w_rg, b_rg, w_re, b_re):
    hs = list(h) if isinstance(h, (tuple, list)) else [h]
    h_specs = [_PROMPT_ROWS, _SAMPLE_ROWS] if len(hs) == 2 else [_rows(D_MODEL)]
    n = N_TOK
    ks = [m[0].shape[1] for m in mix_ins]
    two_part = lambda k: [pl.BlockSpec((ROW_TILE, k), lambda i: (jnp.minimum(i, N_PROMPT_TILES - 1), 0)),
                          pl.BlockSpec((ROW_TILE, k), lambda i: (0, 0))]
    offs = np.cumsum([0] + ks)
    w_parts = [w_out[offs[j]:offs[j + 1]].astype(BF16) for j in range(len(ks))]
    pad = LANES - N_GROUPS - N_EXPERTS
    wr = jnp.concatenate([w_rg, w_re, jnp.zeros((D_MODEL, pad), F32)], axis=1).astype(BF16)
    br = jnp.concatenate([b_rg, b_re, jnp.zeros((pad,), F32)]).reshape(1, LANES)
    tri = jnp.tril(jnp.ones((ROW_TILE, ROW_TILE), BF16), -1)
    return pl.pallas_call(
        functools.partial(_post_mixer_body, len(hs), len(ks)),
        grid=(n // ROW_TILE,),
        in_specs=h_specs + [s for k in ks for s in two_part(k)] + [_full((k, D_MODEL)) for k in ks]
                 + [_full((1, D_MODEL)), _full((D_MODEL, LANES)), _full((1, LANES)), _full((ROW_TILE, ROW_TILE))],
        out_specs=[_rows(D_MODEL), _rows(D_MODEL // 2), _rows(LANES), _full((1, LANES))],
        out_shape=[jax.ShapeDtypeStruct((n, D_MODEL), F32), jax.ShapeDtypeStruct((n, D_MODEL // 2), F32),
                   jax.ShapeDtypeStruct((n, LANES), F32), jax.ShapeDtypeStruct((1, LANES), F32)],
        scratch_shapes=[pltpu.VMEM((1, LANES), F32)],
        compiler_params=_params(1),
        name="post_mixer",
    )(*hs, *[part for m in mix_ins for part in m], *w_parts, g_ffn.reshape(1, D_MODEL), wr, br, tri)


MOE_ROWS = 256


def _ffn_body(be_ref, nb_ref, x_ref, w1_ref, w3_ref, w2_ref, y_ref, w1_sc, w3_sc, w2_sc):
    i = pl.program_id(0)
    new_expert = jnp.logical_or(i == 0, be_ref[i] != be_ref[jnp.maximum(i - 1, 0)])

    @pl.when(jnp.logical_and(new_expert, i < nb_ref[0]))
    def _():
        w1_sc[...] = w1_ref[0, 0].astype(BF16)
        w3_sc[...] = w3_ref[0, 0].astype(BF16)
        w2_sc[...] = w2_ref[0, 0].astype(BF16)

    @pl.when(i < nb_ref[0])
    def _():
        x = _unpack_bf16_halves(x_ref[...])
        a = jnp.dot(x, w1_sc[...], preferred_element_type=F32)
        b = jnp.dot(x, w3_sc[...], preferred_element_type=F32)
        hdn = (a * jax.nn.sigmoid(a) * b).astype(BF16)
        y_ref[...] = jnp.dot(hdn, w2_sc[...], preferred_element_type=F32)

    @pl.when(i >= nb_ref[0])
    def _():
        y_ref[...] = jnp.zeros(y_ref.shape, F32)


def expert_ffn(xs, blk_exp, n_active, w1, w3, w2, layer):
    n_slots = xs.shape[0]
    nb = n_slots // MOE_ROWS
    wmap = lambda i, be, na: (layer, be[i], 0, 0)
    xmap = lambda i, be, na: (jnp.minimum(i, na[0] - 1), 0)
    return pl.pallas_call(
        _ffn_body,
        grid_spec=pltpu.PrefetchScalarGridSpec(
            num_scalar_prefetch=2,
            grid=(nb,),
            in_specs=[pl.BlockSpec((MOE_ROWS, D_MODEL // 2), xmap),
                      pl.BlockSpec((1, 1, D_MODEL, E_HID), wmap),
                      pl.BlockSpec((1, 1, D_MODEL, E_HID), wmap),
                      pl.BlockSpec((1, 1, E_HID, D_MODEL), wmap)],
            out_specs=pl.BlockSpec((MOE_ROWS, D_MODEL), lambda i, be, na: (i, 0)),
            scratch_shapes=[pltpu.VMEM((D_MODEL, E_HID), BF16), pltpu.VMEM((D_MODEL, E_HID), BF16),
                            pltpu.VMEM((E_HID, D_MODEL), BF16)],
        ),
        out_shape=jax.ShapeDtypeStruct((n_slots, D_MODEL), F32),
        compiler_params=_params(1),
        name="expert_ffn",
    )(blk_exp, n_active, xs, w1, w3, w2)


def moe_dispatch(route, counts_row, n):
    counts = counts_row[0, EXPERT_LANE0:EXPERT_LANE0 + N_EXPERTS].astype(jnp.int32)
    padded = ((counts + MOE_ROWS - 1) // MOE_ROWS) * MOE_ROWS
    pad_end = jnp.cumsum(padded)
    pad_start = (pad_end - padded).astype(F32)
    expert_lane = lax.broadcasted_iota(jnp.int32, (1, N_EXPERTS), 1).astype(F32)
    dest = []
    for e_lane, r_lane in ((ROUTE_E1, ROUTE_R1), (ROUTE_E2, ROUTE_R2)):
        start = jnp.sum(jnp.where(route[:, e_lane:e_lane + 1] == expert_lane, pad_start[None, :], 0.0), axis=1)
        dest.append((start + route[:, r_lane]).astype(jnp.int32))
    nb = -(-(n * TOP_K_IN_GROUP) // MOE_ROWS) + N_EXPERTS
    n_slots = nb * MOE_ROWS
    tok = jnp.arange(n, dtype=jnp.int32)
    slot_tok = jnp.zeros((n_slots,), jnp.int32).at[jnp.concatenate(dest)].set(
        jnp.concatenate([tok] * TOP_K_IN_GROUP), unique_indices=True, mode='promise_in_bounds')
    blk_start = jnp.arange(nb, dtype=jnp.int32) * MOE_ROWS
    blk_exp = jnp.sum((pad_end[None, :] <= blk_start[:, None]).astype(jnp.int32), axis=1)
    blk_exp = jnp.minimum(blk_exp, N_EXPERTS - 1)
    n_active = (pad_end[-1] // MOE_ROWS).astype(jnp.int32).reshape(1)
    return slot_tok, dest, blk_exp, n_active


def moe_experts(hn, route, counts_row, w1, w3, w2, layer):
    n = hn.shape[0]
    slot_tok, dest, blk_exp, n_active = moe_dispatch(route, counts_row, n)
    xs = hn.at[slot_tok].get(mode='promise_in_bounds')
    ys = expert_ffn(xs, blk_exp, n_active, w1, w3, w2, layer)
    return [ys.at[d].get(mode='promise_in_bounds') for d in dest]


def _combine(h_ref, y1_ref, y2_ref, route_ref):
    lane = lax.broadcasted_iota(jnp.int32, (ROW_TILE, LANES), 1)
    r = route_ref[...]
    g1 = jnp.sum(jnp.where(lane == ROUTE_G1, r, 0.0), axis=-1, keepdims=True)
    g2 = jnp.sum(jnp.where(lane == ROUTE_G2, r, 0.0), axis=-1, keepdims=True)
    return h_ref[...] + (y1_ref[...] * g1 + y2_ref[...] * g2)


def _combine_proj_body(h_ref, y1_ref, y2_ref, route_ref, g_ref, w_ref, h2_ref, u_ref):
    h2 = _combine(h_ref, y1_ref, y2_ref, route_ref)
    h2_ref[...] = h2
    u_ref[...] = jnp.dot(_rms(h2, g_ref[...]).astype(BF16), w_ref[...], preferred_element_type=F32)


def _combine_norm_body(h_ref, y1_ref, y2_ref, route_ref, g_ref, yp_ref, ys_ref):
    y = _rms(_combine(h_ref, y1_ref, y2_ref, route_ref), g_ref[...])

    @pl.when(pl.program_id(0) < N_PROMPT_TILES)
    def _():
        yp_ref[...] = y

    @pl.when(pl.program_id(0) >= N_PROMPT_TILES)
    def _():
        ys_ref[...] = y


def combine_proj(h, ys, route, g, w):
    n = h.shape[0]
    nn = w.shape[1]
    return pl.pallas_call(
        _combine_proj_body,
        grid=(n // ROW_TILE,),
        in_specs=[_rows(D_MODEL), _rows(D_MODEL), _rows(D_MODEL), _rows(LANES), _full((1, D_MODEL)),
                  _full((D_MODEL, nn))],
        out_specs=[_rows(D_MODEL), _rows(nn)],
        out_shape=[jax.ShapeDtypeStruct((n, D_MODEL), F32), jax.ShapeDtypeStruct((n, nn), F32)],
        compiler_params=_params(1),
        name="combine_proj",
    )(h, *ys, route, g.reshape(1, D_MODEL), w.astype(BF16))


def combine_norm(h, ys, route, g):
    n = h.shape[0]
    return pl.pallas_call(
        _combine_norm_body,
        grid=(n // ROW_TILE,),
        in_specs=[_rows(D_MODEL), _rows(D_MODEL), _rows(D_MODEL), _rows(LANES), _full((1, D_MODEL))],
        out_specs=[_PROMPT_ROWS, _SAMPLE_ROWS],
        out_shape=[jax.ShapeDtypeStruct((N_PROMPT, D_MODEL), F32), jax.ShapeDtypeStruct((N_SAMPLE, D_MODEL), F32)],
        compiler_params=_params(1),
        name="combine_norm",
    )(h, *ys, route, g.reshape(1, D_MODEL))


def _even_first(x, axis):
    n = x.shape[axis]
    idx = jnp.concatenate([jnp.arange(0, n, 2), jnp.arange(1, n, 2)])
    return jnp.take(x, idx, axis=axis)


def _decode_query_cols(q_s, gates_s):
    B, T, G, R = DEC_BATCH, DEC_SEQ, NSA_KV, NSA_REP
    qg = q_s.reshape(B, T, G, R, HD).transpose(0, 2, 4, 3, 1).reshape(B, G, HD, R * T)
    qt = jnp.zeros((B, G, HD, G, R * T), BF16)
    for g in range(G):
        qt = qt.at[:, g, :, g, :].set(qg[:, g])
    qt = jnp.pad(qt.reshape(B, G * HD, G * R * T), ((0, 0), (0, 0), (0, LANES - G * R * T)))
    gr = gates_s[:, :3 * NSA_HEADS].reshape(B, T, NSA_HEADS, 3).transpose(0, 3, 2, 1).reshape(B, 3, NSA_HEADS * T)
    return qt, jnp.pad(gr, ((0, 0), (0, 0), (0, LANES - NSA_HEADS * T)))


def _feature_major(cache):
    lead, rows = cache.shape[:2]
    return jnp.transpose(cache, (0, 2, 3, 4, 1)).reshape(lead, NSA_KV_W, rows)


def _kv_rows_from_feature_major(xt, lead):
    x = xt.reshape(2, NSA_KV, HD, *lead)
    n = len(lead)
    return jnp.transpose(x, tuple(range(3, 3 + n)) + (0, 1, 2))


def mixer_a(h, p, past):
    pos = np.concatenate([np.arange(SEQ), PAST_LEN + np.tile(np.arange(DEC_SEQ), DEC_BATCH)])
    (q, kvc, kvs, kvw, kvct_p, kvct_s, kvst_p, kvst_s, kvwt_p, kvwt_s, ks, vst, kw, vwt, gates, gq, gk, gv,
     la, sgg) = inproj_a(
        *h, p['norm_mix'][0], p['a_w_in'][0], p['a_gla_wa2'][0], p['a_gla_ba'][0], pos)
    cmp_w, gnorm = _cmp_weights(p['a_cmp_pe'][0], p['a_cmp_w'][0]), p['a_gla_norm'][0]
    P = N_PROMPT
    ckv = cmp_blocks_rows(kvc, P, cmp_w)
    ckv = _even_first(ckv.reshape(BATCH, N_CMP_PROMPT, NSA_KV_W), 1)
    o_nsa_p = nsa_prompt_attn(q, gates, ckv, ks, vst, kw, vwt)
    o_gla_p, st_p = gla_seq(gq, gk, la, gv, sgg, gnorm)
    rows_p = lambda xt: jnp.transpose(xt.reshape(BATCH, 2, NSA_KV, HD, xt.shape[2]), (0, 4, 1, 2, 3))
    new_p = (rows_p(kvct_p), rows_p(kvst_p), rows_p(kvwt_p[:, :, SEQ - WINDOW:]), _gla_state_from_pairs(st_p))
    o_gla_s, st_s = gla_step(gq, gk, la, gv, sgg, gnorm, _gla_state_to_pairs(past['state_gla'][0]), P)
    n_pool = past['cache_cmp_kv'].shape[1]
    ckv_pool = cmp_blocks_pages(_feature_major(past['cache_cmp_kv'][0]), cmp_w)
    ckv_seq = ckv_pool.reshape(n_pool, PAGE_SIZE // CMP_BLOCK, NSA_KV_W)[past['page_table']]
    ckv_seq = _even_first(ckv_seq.reshape(DEC_BATCH, N_CMP_DEC, NSA_KV_W), 1)
    qt, gate_rows = _decode_query_cols(q[P:], gates[P:])
    per_seq = lambda x: x[P:].reshape(DEC_BATCH, 1, DEC_SEQ * NSA_KV_W)
    o_t, win_new = nsa_decode_attn(qt, gate_rows, ckv_seq, _feature_major(past['cache_slc_kv'][0]),
                                   _feature_major(past['cache_win_kv'][0]), per_seq(kvs), per_seq(kvw),
                                   past['page_table'])
    o_nsa_s = o_t[:, :, :NSA_HEADS * DEC_SEQ].reshape(DEC_BATCH, HD, NSA_HEADS, DEC_SEQ)
    o_nsa_s = o_nsa_s.transpose(0, 3, 2, 1).reshape(N_SAMPLE, NSA_Q_W).astype(BF16)
    n_buf = win_new.shape[2]
    win_s = jnp.transpose(win_new.reshape(DEC_BATCH, 2, NSA_KV, HD, n_buf), (0, 4, 1, 2, 3))
    new_s = (_kv_rows_from_feature_major(kvct_s[:, :N_SAMPLE], (DEC_BATCH, DEC_SEQ)),
             _kv_rows_from_feature_major(kvst_s[:, :N_SAMPLE], (DEC_BATCH, DEC_SEQ)), win_s,
             _gla_state_from_pairs(st_s))
    return (o_nsa_p, o_nsa_s), (o_gla_p, o_gla_s), new_p, new_s


def run_trunk(x_prompt, x_sample, p, past):
    h0 = (x_prompt.reshape(N_PROMPT, D_MODEL), x_sample.reshape(N_SAMPLE, D_MODEL))
    o_nsa, o_gla, new_p, new_s = mixer_a(h0, p, past)
    h, hn, route, counts = post_mixer(h0, [o_nsa, o_gla], p['a_w_out'][0], p['norm_ffn'][0], p['m_w_rg'][0],
                                      p['m_b_rg'][0], p['m_w_re'][0], p['m_b_re'][0])
    ys = moe_experts(hn, route, counts, p['m_w1'], p['m_w3'], p['m_w2'], 0)
    h, u = combine_proj(h, ys, route, p['norm_mix'][1], p['c_w_in'][0])
    lru_w = (p['c_conv_w'][0], p['c_conv_b'][0], p['c_w_a'][0], p['c_b_a'][0], p['c_w_x'][0], p['c_b_x'][0],
             p['c_lam'][0])
    us = u[N_PROMPT:].reshape(DEC_BATCH, DEC_SEQ, 2 * D_RNN)
    y_p, lru_p = lru_seq(u, *lru_w)
    y_s, lru_s = lru_step(jnp.swapaxes(us, 0, 1), past['state_conv'][0], past['state_lru'][0], *lru_w)
    conv_p = jnp.stack([u[(b + 1) * SEQ - (CONV_W - 1):(b + 1) * SEQ, D_RNN:] for b in range(BATCH)])
    conv_s = us[:, DEC_SEQ - (CONV_W - 1):, D_RNN:]
    mix_in = (y_p, jnp.swapaxes(y_s, 0, 1).reshape(N_SAMPLE, D_RNN))
    h, hn, route, counts = post_mixer(h, [mix_in], p['c_w_out'][0], p['norm_ffn'][1], p['m_w_rg'][1],
                                      p['m_b_rg'][1], p['m_w_re'][1], p['m_b_re'][1])
    ys = moe_experts(hn, route, counts, p['m_w1'], p['m_w3'], p['m_w2'], 1)
    y_p, y_s = combine_norm(h, ys, route, p['norm_final'])
    y_prompt = y_p.reshape(BATCH, SEQ, D_MODEL)
    y_sample = y_s.reshape(DEC_BATCH, DEC_SEQ, D_MODEL)
    return (y_prompt, y_sample), new_p + (lru_p, conv_p), new_s + (lru_s, conv_s)


def kernel(x_prompt, x_sample, cache_cmp_kv, cache_slc_kv, cache_win_kv, state_gla, state_lru, state_conv,
           page_table, norm_mix, norm_ffn, norm_final, a_w_in, a_cmp_pe, a_cmp_w, a_gla_wa2, a_gla_ba,
           a_gla_norm, a_w_out, c_w_in, c_conv_w, c_conv_b, c_w_a, c_b_a, c_w_x, c_b_x, c_lam, c_w_out,
           m_w_rg, m_b_rg, m_w_re, m_b_re, m_w1, m_w3, m_w2):
    p = {'norm_mix': norm_mix, 'norm_ffn': norm_ffn, 'norm_final': norm_final,
         'a_w_in': a_w_in, 'a_cmp_pe': a_cmp_pe, 'a_cmp_w': a_cmp_w, 'a_gla_wa2': a_gla_wa2,
         'a_gla_ba': a_gla_ba, 'a_gla_norm': a_gla_norm, 'a_w_out': a_w_out,
         'c_w_in': c_w_in, 'c_conv_w': c_conv_w, 'c_conv_b': c_conv_b, 'c_w_a': c_w_a, 'c_b_a': c_b_a,
         'c_w_x': c_w_x, 'c_b_x': c_b_x, 'c_lam': c_lam, 'c_w_out': c_w_out,
         'm_w_rg': m_w_rg, 'm_b_rg': m_b_rg, 'm_w_re': m_w_re, 'm_b_re': m_b_re,
         'm_w1': m_w1, 'm_w3': m_w3, 'm_w2': m_w2}
    past = {'cache_cmp_kv': cache_cmp_kv, 'cache_slc_kv': cache_slc_kv, 'cache_win_kv': cache_win_kv,
            'state_gla': state_gla, 'state_lru': state_lru, 'state_conv': state_conv,
            'page_table': page_table}
    (y_p, y_s), sp, ss = run_trunk(x_prompt, x_sample, p, past)
    outs = [y_p, y_s]
    for a, b in zip(sp, ss):
        outs += [a[None], b[None]]
    return tuple(outs)
```
